```python
import math
import jax, jax.numpy as jnp
from jax import lax
import numpy as np

D_MODEL = 2048
BATCH = 8
SEQ = 2048
DEPTH = 1

CHUNK = 64
Q_BLOCK = 128
SB_HEADS = 8
SB_HEAD_DIM = 128
D_SB = SB_HEADS * SB_HEAD_DIM
D_CONV = D_MODEL // 2
CONV_WIDTH = 31
D_FF = 4 * D_MODEL
EPS = 1e-6
IN_SPLITS = (D_SB, D_SB, D_SB, D_CONV, D_CONV, D_MODEL, D_MODEL)
D_IN = sum(IN_SPLITS)

kernel_name = "stickbreak_conformer_gated_hybrid"


def rmsnorm(x, g):
    xf = x.astype(jnp.float32)
    y = xf * lax.rsqrt(jnp.mean(xf * xf, axis=-1, keepdims=True) + EPS)
    return (y * g.astype(jnp.float32)).astype(x.dtype)


def layernorm(x, g, b):
    xf = x.astype(jnp.float32)
    mu = jnp.mean(xf, axis=-1, keepdims=True)
    var = jnp.mean(jnp.square(xf - mu), axis=-1, keepdims=True)
    y = (xf - mu) * lax.rsqrt(var + EPS)
    return (y * g.astype(jnp.float32) + b.astype(jnp.float32)).astype(x.dtype)


def stick_breaking_attention(q, k, v):
    b, s, h, dh = q.shape
    scale = 1.0 / math.sqrt(dh)
    outs = []
    for i in range(s // Q_BLOCK):
        t0, t1 = i * Q_BLOCK, (i + 1) * Q_BLOCK
        qb = q[:, t0:t1]
        kb = k[:, :t1]
        vb = v[:, :t1]
        z = jnp.einsum('bqhd,bkhd->bhqk', qb, kb).astype(jnp.float32) * scale
        tq = t0 + jnp.arange(Q_BLOCK)[:, None]
        sk = jnp.arange(t1)[None, :]
        mask = sk < tq
        log_1m = jnp.where(mask, jax.nn.log_sigmoid(-z), 0.0)
        suffix = lax.cumsum(log_1m, axis=3, reverse=True) - log_1m
        a = jnp.where(mask, jnp.exp(jax.nn.log_sigmoid(z) + suffix), 0.0)
        outs.append(jnp.einsum('bhqk,bkhd->bqhd', a.astype(vb.dtype), vb))
    return jnp.concatenate(outs, axis=1)


def causal_depthwise_conv(u, w, bias):
    c = u.shape[-1]
    y = lax.conv_general_dilated(
        u, w.reshape(CONV_WIDTH, 1, c).astype(u.dtype),
        window_strides=(1,), padding=[(CONV_WIDTH - 1, 0)],
        dimension_numbers=('NWC', 'WIO', 'NWC'), feature_group_count=c)
    return y + bias


def _fwd_setup_inputs(seed: int = 0) -> dict:
    key = jax.random.key(seed)
    ks = jax.random.split(key, 17)
    L = DEPTH
    nrm = lambda k, shape, fan_in: jax.random.normal(k, shape, jnp.float32) * (fan_in ** -0.5)
    gain = lambda k, shape: 1.0 + 0.02 * jax.random.normal(k, shape, jnp.float32)
    small = lambda k, shape: 0.02 * jax.random.normal(k, shape, jnp.float32)
    return {
        "x": jax.random.normal(ks[0], (BATCH, SEQ, D_MODEL), jnp.float32),
        "g_pre_mix": gain(ks[1], (L, D_MODEL)),
        "w_in": nrm(ks[2], (L, D_MODEL, D_IN), D_MODEL),
        "b_in": small(ks[3], (L, D_IN)),
        "w_dw": nrm(ks[4], (L, CONV_WIDTH, D_CONV), CONV_WIDTH),
        "b_dw": small(ks[5], (L, D_CONV)),
        "g_conv_ln": gain(ks[6], (L, D_CONV)),
        "b_conv_ln": small(ks[7], (L, D_CONV)),
        "w_sb_out": nrm(ks[8], (L, D_SB, D_MODEL), D_SB),
        "w_conv_out": nrm(ks[9], (L, D_CONV, D_MODEL), D_CONV),
        "w_o": nrm(ks[10], (L, D_MODEL, D_MODEL), D_MODEL),
        "g_post_mix": gain(ks[11], (L, D_MODEL)),
        "g_pre_mlp": gain(ks[12], (L, D_MODEL)),
        "w_up": nrm(ks[13], (L, D_MODEL, D_FF), D_MODEL),
        "w_down": nrm(ks[14], (L, D_FF, D_MODEL), D_FF),
        "g_post_mlp": gain(ks[15], (L, D_MODEL)),
    }


def _fwd_reference(x, g_pre_mix, w_in, b_in, w_dw, b_dw, g_conv_ln, b_conv_ln,
              w_sb_out, w_conv_out, w_o, g_post_mix, g_pre_mlp, w_up, w_down,
              g_post_mlp):
    b, s, _ = x.shape
    offs = np.cumsum(IN_SPLITS)[:-1].tolist()
    for l in range(DEPTH):
        h = rmsnorm(x, g_pre_mix[l])
        proj = jnp.einsum('bsd,de->bse', h, w_in[l]) + b_in[l]
        q, k, v, glu_a, glu_b, gate_sb, gate_cv = jnp.split(proj, offs, axis=-1)
        hd = (b, s, SB_HEADS, SB_HEAD_DIM)
        o_sb = stick_breaking_attention(q.reshape(hd), k.reshape(hd), v.reshape(hd))
        o_sb = jnp.einsum('bse,ed->bsd', o_sb.reshape(b, s, D_SB), w_sb_out[l])
        u = glu_a * jax.nn.sigmoid(glu_b)
        u = causal_depthwise_conv(u, w_dw[l], b_dw[l])
        u = jax.nn.silu(layernorm(u, g_conv_ln[l], b_conv_ln[l]))
        o_cv = jnp.einsum('bsc,cd->bsd', u, w_conv_out[l])
        merged = jax.nn.sigmoid(gate_sb) * o_sb + jax.nn.sigmoid(gate_cv) * o_cv
        y = jnp.einsum('bsd,de->bse', merged, w_o[l])
        x = x + rmsnorm(y, g_post_mix[l])
        h = rmsnorm(x, g_pre_mlp[l])
        f = jnp.square(jax.nn.relu(jnp.einsum('bsd,df->bsf', h, w_up[l])))
        f = jnp.einsum('bsf,fd->bsd', f, w_down[l])
        x = x + rmsnorm(f, g_post_mlp[l])
    return x


import jax as _jax
import jax.numpy as _jnp

TWIN_FORMAT = 'train_step'
FWD_PARAMS = ['x', 'g_pre_mix', 'w_in', 'b_in', 'w_dw', 'b_dw', 'g_conv_ln', 'b_conv_ln', 'w_sb_out', 'w_conv_out', 'w_o', 'g_post_mix', 'g_pre_mlp', 'w_up', 'w_down', 'g_post_mlp']
TWIN_WEIGHTS = ['g_pre_mix', 'w_in', 'b_in', 'w_dw', 'b_dw', 'g_conv_ln', 'b_conv_ln', 'w_sb_out', 'w_conv_out', 'w_o', 'g_post_mix', 'g_pre_mlp', 'w_up', 'w_down', 'g_post_mlp']
TWIN_DIFF_INPUT = 'x'
TWIN_INPUTS = ['x', 'g_pre_mix', 'w_in', 'b_in', 'w_dw', 'b_dw', 'g_conv_ln', 'b_conv_ln', 'w_sb_out', 'w_conv_out', 'w_o', 'g_post_mix', 'g_pre_mlp', 'w_up', 'w_down', 'g_post_mlp', 'loss_target', 'm_g_pre_mix', 'm_w_in', 'm_b_in', 'm_w_dw', 'm_b_dw', 'm_g_conv_ln', 'm_b_conv_ln', 'm_w_sb_out', 'm_w_conv_out', 'm_w_o', 'm_g_post_mix', 'm_g_pre_mlp', 'm_w_up', 'm_w_down', 'm_g_post_mlp', 'v_g_pre_mix', 'v_w_in', 'v_b_in', 'v_w_dw', 'v_b_dw', 'v_g_conv_ln', 'v_b_conv_ln', 'v_w_sb_out', 'v_w_conv_out', 'v_w_o', 'v_g_post_mix', 'v_g_pre_mlp', 'v_w_up', 'v_w_down', 'v_g_post_mlp']
TWIN_OUTPUTS = ['loss', 'grad_x', 'grad_g_pre_mix', 'grad_w_in', 'grad_b_in', 'grad_w_dw', 'grad_b_dw', 'grad_g_conv_ln', 'grad_b_conv_ln', 'grad_w_sb_out', 'grad_w_conv_out', 'grad_w_o', 'grad_g_post_mix', 'grad_g_pre_mlp', 'grad_w_up', 'grad_w_down', 'grad_g_post_mlp', 'delta_g_pre_mix', 'delta_w_in', 'delta_b_in', 'delta_w_dw', 'delta_b_dw', 'delta_g_conv_ln', 'delta_b_conv_ln', 'delta_w_sb_out', 'delta_w_conv_out', 'delta_w_o', 'delta_g_post_mix', 'delta_g_pre_mlp', 'delta_w_up', 'delta_w_down', 'delta_g_post_mlp', 'new_m_g_pre_mix', 'new_m_w_in', 'new_m_b_in', 'new_m_w_dw', 'new_m_b_dw', 'new_m_g_conv_ln', 'new_m_b_conv_ln', 'new_m_w_sb_out', 'new_m_w_conv_out', 'new_m_w_o', 'new_m_g_post_mix', 'new_m_g_pre_mlp', 'new_m_w_up', 'new_m_w_down', 'new_m_g_post_mlp', 'new_v_g_pre_mix', 'new_v_w_in', 'new_v_b_in', 'new_v_w_dw', 'new_v_b_dw', 'new_v_g_conv_ln', 'new_v_b_conv_ln', 'new_v_w_sb_out', 'new_v_w_conv_out', 'new_v_w_o', 'new_v_g_post_mix', 'new_v_g_pre_mlp', 'new_v_w_up', 'new_v_w_down', 'new_v_g_post_mlp']
TWIN_LEAF_KINDS = {'loss': 'loss', 'grad_x': 'grad_x', 'grad_g_pre_mix': 'grad_w', 'grad_w_in': 'grad_w', 'grad_b_in': 'grad_w', 'grad_w_dw': 'grad_w', 'grad_b_dw': 'grad_w', 'grad_g_conv_ln': 'grad_w', 'grad_b_conv_ln': 'grad_w', 'grad_w_sb_out': 'grad_w', 'grad_w_conv_out': 'grad_w', 'grad_w_o': 'grad_w', 'grad_g_post_mix': 'grad_w', 'grad_g_pre_mlp': 'grad_w', 'grad_w_up': 'grad_w', 'grad_w_down': 'grad_w', 'grad_g_post_mlp': 'grad_w', 'delta_g_pre_mix': 'delta_w', 'delta_w_in': 'delta_w', 'delta_b_in': 'delta_w', 'delta_w_dw': 'delta_w', 'delta_b_dw': 'delta_w', 'delta_g_conv_ln': 'delta_w', 'delta_b_conv_ln': 'delta_w', 'delta_w_sb_out': 'delta_w', 'delta_w_conv_out': 'delta_w', 'delta_w_o': 'delta_w', 'delta_g_post_mix': 'delta_w', 'delta_g_pre_mlp': 'delta_w', 'delta_w_up': 'delta_w', 'delta_w_down': 'delta_w', 'delta_g_post_mlp': 'delta_w', 'new_m_g_pre_mix': 'new_m', 'new_m_w_in': 'new_m', 'new_m_b_in': 'new_m', 'new_m_w_dw': 'new_m', 'new_m_b_dw': 'new_m', 'new_m_g_conv_ln': 'new_m', 'new_m_b_conv_ln': 'new_m', 'new_m_w_sb_out': 'new_m', 'new_m_w_conv_out': 'new_m', 'new_m_w_o': 'new_m', 'new_m_g_post_mix': 'new_m', 'new_m_g_pre_mlp': 'new_m', 'new_m_w_up': 'new_m', 'new_m_w_down': 'new_m', 'new_m_g_post_mlp': 'new_m', 'new_v_g_pre_mix': 'new_v', 'new_v_w_in': 'new_v', 'new_v_b_in': 'new_v', 'new_v_w_dw': 'new_v', 'new_v_b_dw': 'new_v', 'new_v_g_conv_ln': 'new_v', 'new_v_b_conv_ln': 'new_v', 'new_v_w_sb_out': 'new_v', 'new_v_w_conv_out': 'new_v', 'new_v_w_o': 'new_v', 'new_v_g_post_mix': 'new_v', 'new_v_g_pre_mlp': 'new_v', 'new_v_w_up': 'new_v', 'new_v_w_down': 'new_v', 'new_v_g_post_mlp': 'new_v'}


def _forward(args):
    return _fwd_reference(*[args[k] for k in FWD_PARAMS])


def _output_shape():
    out = _jax.eval_shape(lambda: _forward(_fwd_setup_inputs(0)))
    return out.shape, out.dtype

N_MICROBATCH = 1
ADAM_LR = 0.001
ADAM_B1 = 0.9
ADAM_B2 = 0.999
ADAM_EPS = 1e-08
ADAM_WD = 0.01
ADAM_STEP = 10
PER_EXAMPLE_BATCH_AXIS = {'x': 0, 'loss_target': 0}
SHARED_INPUTS = []
_WEIGHT_DTYPES = {'g_pre_mix': _jnp.float32, 'w_in': _jnp.float32, 'b_in': _jnp.float32, 'w_dw': _jnp.float32, 'b_dw': _jnp.float32, 'g_conv_ln': _jnp.float32, 'b_conv_ln': _jnp.float32, 'w_sb_out': _jnp.float32, 'w_conv_out': _jnp.float32, 'w_o': _jnp.float32, 'g_post_mix': _jnp.float32, 'g_pre_mlp': _jnp.float32, 'w_up': _jnp.float32, 'w_down': _jnp.float32, 'g_post_mlp': _jnp.float32}
MOMENT_SCALE = {'g_pre_mix': 1.974421e-01, 'w_in': 8.456589e-02, 'b_in': 1.980685e+00, 'w_dw': 2.862622e-01, 'b_dw': 4.583041e+00, 'g_conv_ln': 1.789674e+00, 'b_conv_ln': 2.628329e+00, 'w_sb_out': 1.321230e-01, 'w_conv_out': 7.285287e-01, 'w_o': 7.046593e-01, 'g_post_mix': 8.016961e+00, 'g_pre_mlp': 2.338783e-01, 'w_up': 1.187546e-01, 'w_down': 6.921773e-01, 'g_post_mlp': 8.243181e+00}


def _to_microbatches(a, axis):
    t = _jnp.moveaxis(a, axis, 0)
    t = t.reshape((N_MICROBATCH, t.shape[0] // N_MICROBATCH) + t.shape[1:])
    return _jnp.moveaxis(t, 1, axis + 1)


def setup_inputs(seed: int = 0) -> dict:
    inp = _fwd_setup_inputs(seed)
    key = _jax.random.fold_in(_jax.random.key(seed), 7919)
    shape, _ = _output_shape()
    out = dict(inp)
    out["loss_target"] = _jax.random.normal(_jax.random.fold_in(key, 0), shape, _jnp.float32)
    for i, name in enumerate(TWIN_WEIGHTS):
        w = inp[name].astype(_jnp.float32)
        if MOMENT_SCALE is None:
            s = _jnp.sqrt(_jnp.mean(_jnp.square(w)) + 1e-30)
        else:
            s = MOMENT_SCALE[name]
        km, kv = _jax.random.split(_jax.random.fold_in(key, i + 1))
        out[name] = w
        out["m_" + name] = s * _jax.random.normal(km, w.shape, _jnp.float32)
        out["v_" + name] = (s * s) * _jax.random.uniform(kv, w.shape, _jnp.float32, 0.5, 1.5)
    if N_MICROBATCH > 1:
        for name, axis in PER_EXAMPLE_BATCH_AXIS.items():
            out[name] = _to_microbatches(out[name], axis)
    return {'x': out['x'], 'g_pre_mix': out['g_pre_mix'], 'w_in': out['w_in'], 'b_in': out['b_in'], 'w_dw': out['w_dw'], 'b_dw': out['b_dw'], 'g_conv_ln': out['g_conv_ln'], 'b_conv_ln': out['b_conv_ln'], 'w_sb_out': out['w_sb_out'], 'w_conv_out': out['w_conv_out'], 'w_o': out['w_o'], 'g_post_mix': out['g_post_mix'], 'g_pre_mlp': out['g_pre_mlp'], 'w_up': out['w_up'], 'w_down': out['w_down'], 'g_post_mlp': out['g_post_mlp'], 'loss_target': out['loss_target'], 'm_g_pre_mix': out['m_g_pre_mix'], 'm_w_in': out['m_w_in'], 'm_b_in': out['m_b_in'], 'm_w_dw': out['m_w_dw'], 'm_b_dw': out['m_b_dw'], 'm_g_conv_ln': out['m_g_conv_ln'], 'm_b_conv_ln': out['m_b_conv_ln'], 'm_w_sb_out': out['m_w_sb_out'], 'm_w_conv_out': out['m_w_conv_out'], 'm_w_o': out['m_w_o'], 'm_g_post_mix': out['m_g_post_mix'], 'm_g_pre_mlp': out['m_g_pre_mlp'], 'm_w_up': out['m_w_up'], 'm_w_down': out['m_w_down'], 'm_g_post_mlp': out['m_g_post_mlp'], 'v_g_pre_mix': out['v_g_pre_mix'], 'v_w_in': out['v_w_in'], 'v_b_in': out['v_b_in'], 'v_w_dw': out['v_w_dw'], 'v_b_dw': out['v_b_dw'], 'v_g_conv_ln': out['v_g_conv_ln'], 'v_b_conv_ln': out['v_b_conv_ln'], 'v_w_sb_out': out['v_w_sb_out'], 'v_w_conv_out': out['v_w_conv_out'], 'v_w_o': out['v_w_o'], 'v_g_post_mix': out['v_g_post_mix'], 'v_g_pre_mlp': out['v_g_pre_mlp'], 'v_w_up': out['v_w_up'], 'v_w_down': out['v_w_down'], 'v_g_post_mlp': out['v_g_post_mlp']}


def _loss(weights, diff, rest, loss_target):
    with _jax.named_scope("forward"):
        args = {**rest, TWIN_DIFF_INPUT: diff, **{k: w.astype(_WEIGHT_DTYPES[k]) for k, w in weights.items()}}
        y = _forward(args)
    with _jax.named_scope("loss_head"):
        err = _jnp.square(y.astype(_jnp.float32) - loss_target)
        return 0.5 * _jnp.sum(_jnp.mean(err, axis=-1)) if err.ndim else 0.5 * err


def _adamw(w, g, m, v):
    m = ADAM_B1 * m + (1.0 - ADAM_B1) * g
    v = ADAM_B2 * v + (1.0 - ADAM_B2) * _jnp.square(g)
    m_hat = m / (1.0 - ADAM_B1 ** ADAM_STEP)
    v_hat = v / (1.0 - ADAM_B2 ** ADAM_STEP)
    delta = -ADAM_LR * (m_hat / (_jnp.sqrt(v_hat) + ADAM_EPS) + ADAM_WD * w)
    return delta, m, v


def reference(x, g_pre_mix, w_in, b_in, w_dw, b_dw, g_conv_ln, b_conv_ln, w_sb_out, w_conv_out, w_o, g_post_mix, g_pre_mlp, w_up, w_down, g_post_mlp, loss_target, m_g_pre_mix, m_w_in, m_b_in, m_w_dw, m_b_dw, m_g_conv_ln, m_b_conv_ln, m_w_sb_out, m_w_conv_out, m_w_o, m_g_post_mix, m_g_pre_mlp, m_w_up, m_w_down, m_g_post_mlp, v_g_pre_mix, v_w_in, v_b_in, v_w_dw, v_b_dw, v_g_conv_ln, v_b_conv_ln, v_w_sb_out, v_w_conv_out, v_w_o, v_g_post_mix, v_g_pre_mlp, v_w_up, v_w_down, v_g_post_mlp):
    given = dict(x=x, g_pre_mix=g_pre_mix, w_in=w_in, b_in=b_in, w_dw=w_dw, b_dw=b_dw, g_conv_ln=g_conv_ln, b_conv_ln=b_conv_ln, w_sb_out=w_sb_out, w_conv_out=w_conv_out, w_o=w_o, g_post_mix=g_post_mix, g_pre_mlp=g_pre_mlp, w_up=w_up, w_down=w_down, g_post_mlp=g_post_mlp, loss_target=loss_target, m_g_pre_mix=m_g_pre_mix, m_w_in=m_w_in, m_b_in=m_b_in, m_w_dw=m_w_dw, m_b_dw=m_b_dw, m_g_conv_ln=m_g_conv_ln, m_b_conv_ln=m_b_conv_ln, m_w_sb_out=m_w_sb_out, m_w_conv_out=m_w_conv_out, m_w_o=m_w_o, m_g_post_mix=m_g_post_mix, m_g_pre_mlp=m_g_pre_mlp, m_w_up=m_w_up, m_w_down=m_w_down, m_g_post_mlp=m_g_post_mlp, v_g_pre_mix=v_g_pre_mix, v_w_in=v_w_in, v_b_in=v_b_in, v_w_dw=v_w_dw, v_b_dw=v_b_dw, v_g_conv_ln=v_g_conv_ln, v_b_conv_ln=v_b_conv_ln, v_w_sb_out=v_w_sb_out, v_w_conv_out=v_w_conv_out, v_w_o=v_w_o, v_g_post_mix=v_g_post_mix, v_g_pre_mlp=v_g_pre_mlp, v_w_up=v_w_up, v_w_down=v_w_down, v_g_post_mlp=v_g_post_mlp)
    weights = {n: given[n] for n in TWIN_WEIGHTS}
    shared = {n: given[n] for n in SHARED_INPUTS}
    per_example = {n: given[n] for n in ['x']}
    grad_fn = _jax.value_and_grad(_loss, argnums=(0, 1))

    def one_microbatch(ex, loss_target):
        ex = dict(ex)
        diff = ex.pop(TWIN_DIFF_INPUT)
        return grad_fn(weights, diff, {**shared, **ex}, loss_target)

    if N_MICROBATCH == 1:
        loss, (grad_w, grad_x) = one_microbatch(per_example, given["loss_target"])
    else:
        def body(carry, xs):
            loss_sum, grad_sum = carry
            l_k, (gw_k, gx_k) = one_microbatch(xs[0], xs[1])
            with _jax.named_scope("update"):
                return (loss_sum + l_k, _jax.tree.map(_jnp.add, grad_sum, gw_k)), gx_k

        init = (_jnp.zeros((), _jnp.float32), _jax.tree.map(_jnp.zeros_like, weights))
        (loss, grad_w), grad_x = _jax.lax.scan(body, init, (per_example, given["loss_target"]))
    with _jax.named_scope("update"):
        delta_w, new_m, new_v = {}, {}, {}
        for n in TWIN_WEIGHTS:
            delta_w[n], new_m[n], new_v[n] = _adamw(weights[n], grad_w[n], given["m_" + n], given["v_" + n])
    return (loss, grad_x, *[grad_w[n] for n in TWIN_WEIGHTS], *[delta_w[n] for n in TWIN_WEIGHTS],
            *[new_m[n] for n in TWIN_WEIGHTS], *[new_v[n] for n in TWIN_WEIGHTS])
```

```python
import functools
import math

import jax
import jax.numpy as jnp
from jax import lax
from jax.experimental import pallas as pl
from jax.experimental.pallas import tpu as pltpu

F32 = jnp.float32
BF16 = jnp.bfloat16
MESH = pl.DeviceIdType.MESH

HEAD_DIM = 128
CONV_WIDTH = 31
CONV_HALO = 32
EPS = 1e-6
ADAM_LR = 0.001
ADAM_B1 = 0.9
ADAM_B2 = 0.999
ADAM_EPS = 1e-08
ADAM_WD = 0.01
ADAM_STEP = 10
N_CHIPS = 4
VMEM_LIMIT = 48 * 1024 * 1024


def _params(**kw):
    return pltpu.CompilerParams(vmem_limit_bytes=VMEM_LIMIT, **kw)


def _tile(n, want, align=128):
    if n <= want:
        return n
    for t in range(want - want % align, 0, -align):
        if n % t == 0:
            return t
    raise ValueError((n, want, align))


def _mm(a, b, *, name, ta=False, tb=False, b_groups=1, out_groups=1, bias=None, extra=None, epilogue=None,
        out_dtypes=(F32,), tm=1024, tn=1024, tk=512):
    if ta:
        K, M = a.shape
    else:
        M, K = a.shape
    if b_groups == 1:
        br, bc = b.shape
    else:
        _, br, bcg = b.shape
        bc = bcg * b_groups
    if tb:
        N, Kb = br, bc
    else:
        Kb, N = br, bc
    assert Kb == K, (name, a.shape, b.shape)
    tm, tn, tk = _tile(M, tm), _tile(N, tn), _tile(K, tk)
    if b_groups > 1:
        if tb:
            tk = _tile(K // b_groups, tk)
        else:
            tn = _tile(N // b_groups, tn)
    if out_groups > 1:
        tn = _tile(N // out_groups, tn)
        if b_groups > 1 and not tb:
            tn = _tile(N // b_groups, tn)
    nm, nn, nk = M // tm, N // tn, K // tk
    extra = tuple(extra or ())
    n_extra = len(extra)
    has_bias = bias is not None
    n_out = len(out_dtypes)

    a_spec = pl.BlockSpec((tk, tm), lambda i, j, k: (k, i)) if ta else pl.BlockSpec((tm, tk), lambda i, j, k: (i, k))
    if b_groups == 1:
        b_spec = pl.BlockSpec((tn, tk), lambda i, j, k: (j, k)) if tb else pl.BlockSpec((tk, tn), lambda i, j, k: (k, j))
    elif tb:
        kpg = (K // b_groups) // tk
        b_spec = pl.BlockSpec((None, tn, tk), lambda i, j, k: (k // kpg, j, k % kpg))
    else:
        npg = (N // b_groups) // tn
        b_spec = pl.BlockSpec((None, tk, tn), lambda i, j, k: (j // npg, k, j % npg))
    in_specs = [a_spec, b_spec]
    operands = [a, b]
    if has_bias:
        in_specs.append(pl.BlockSpec((1, tn), lambda i, j, k: (0, j)))
        operands.append(bias)
    for e in extra:
        in_specs.append(pl.BlockSpec((tm, tn), lambda i, j, k: (i, j)))
        operands.append(e)
    if out_groups == 1:
        o_spec = pl.BlockSpec((tm, tn), lambda i, j, k: (i, j))
        o_shape = (M, N)
    else:
        opg = (N // out_groups) // tn
        o_spec = pl.BlockSpec((None, tm, tn), lambda i, j, k: (j // opg, i, j % opg))
        o_shape = (out_groups, M, N // out_groups)
    dims = (((0 if ta else 1,), (1 if tb else 0,)), ((), ()))

    def body(*refs):
        a_ref, b_ref = refs[0], refs[1]
        pos = 2
        bias_ref = None
        if has_bias:
            bias_ref = refs[pos]
            pos += 1
        extra_refs = refs[pos:pos + n_extra]
        pos += n_extra
        out_refs = refs[pos:pos + n_out]
        pos += n_out
        acc_ref = refs[pos] if nk > 1 else None

        part = lax.dot_general(a_ref[...].astype(BF16), b_ref[...].astype(BF16), dims, preferred_element_type=F32)

        def finish(acc):
            if has_bias:
                acc = acc + bias_ref[...]
            outs = epilogue(acc, *[e[...] for e in extra_refs]) if epilogue is not None else (acc,)
            for o_ref, o in zip(out_refs, outs):
                o_ref[...] = o.astype(o_ref.dtype)

        if nk == 1:
            finish(part)
        else:
            k = pl.program_id(2)

            @pl.when(k == 0)
            def _():
                acc_ref[...] = part

            @pl.when(k > 0)
            def _():
                acc_ref[...] += part

            @pl.when(k == nk - 1)
            def _():
                finish(acc_ref[...])

    outs = pl.pallas_call(
        body,
        name=name,
        grid=(nm, nn, nk),
        in_specs=in_specs,
        out_specs=[o_spec] * n_out,
        out_shape=[jax.ShapeDtypeStruct(o_shape, dt) for dt in out_dtypes],
        scratch_shapes=[pltpu.VMEM((tm, tn), F32)] if nk > 1 else [],
        compiler_params=_params(dimension_semantics=("parallel", "parallel", "arbitrary")),
    )(*operands)
    return outs[0] if n_out == 1 else outs


def _rms_stats(x):
    return lax.rsqrt(jnp.mean(x * x, axis=-1, keepdims=True) + EPS)


def _rms_bwd(x, r, g, dy):
    gy = dy * g
    return r * gy - x * (r * r * r) * jnp.mean(x * gy, axis=-1, keepdims=True)


def _sigmoid(x):
    return 1.0 / (1.0 + jnp.exp(-x))


def _row_call(body, *, name, rows, tr, ins, outs, acc_outs=()):
    n = rows // tr
    in_specs = []
    for arr, kind in ins:
        if kind == "row":
            in_specs.append(pl.BlockSpec((tr, arr.shape[1]), lambda i: (i, 0)))
        else:
            in_specs.append(pl.BlockSpec(arr.shape, lambda i: (0, 0)))
    out_specs = [pl.BlockSpec((tr, w), lambda i: (i, 0)) for w, _ in outs]
    out_shape = [jax.ShapeDtypeStruct((rows, w), dt) for w, dt in outs]
    out_specs += [pl.BlockSpec((1, w), lambda i: (0, 0)) for w in acc_outs]
    out_shape += [jax.ShapeDtypeStruct((1, w), F32) for w in acc_outs]
    return pl.pallas_call(
        body, name=name, grid=(n,), in_specs=in_specs, out_specs=out_specs, out_shape=out_shape,
        compiler_params=_params(dimension_semantics=("arbitrary",)),
    )(*[a for a, _ in ins])


def _accumulate(ref, val):
    @pl.when(pl.program_id(0) == 0)
    def _():
        ref[...] = val

    @pl.when(pl.program_id(0) > 0)
    def _():
        ref[...] += val


def _rms_fwd(x, g, tr=256):
    def body(x_ref, g_ref, h_ref):
        xv = x_ref[...]
        h_ref[...] = (xv * _rms_stats(xv) * g_ref[...]).astype(BF16)

    (h,) = _row_call(body, name="rms_fwd", rows=x.shape[0], tr=tr, ins=[(x, "row"), (g, "vec")], outs=[(x.shape[1], BF16)])
    return h


def _resid_rms(x, y, g2, g3, tr=256):
    def body(x_ref, y_ref, g2_ref, g3_ref, x2_ref, h2_ref):
        yv = y_ref[...]
        x2 = x_ref[...] + yv * _rms_stats(yv) * g2_ref[...]
        x2_ref[...] = x2
        h2_ref[...] = (x2 * _rms_stats(x2) * g3_ref[...]).astype(BF16)

    d = x.shape[1]
    return _row_call(body, name="resid_rms", rows=x.shape[0], tr=tr,
                     ins=[(x, "row"), (y, "row"), (g2, "vec"), (g3, "vec")], outs=[(d, F32), (d, BF16)])


def _final(x2, dn, g4, tgt, tr=256):
    d = x2.shape[1]

    def body(x2_ref, dn_ref, g4_ref, t_ref, dx3_ref, ddn_ref, dg4_ref, loss_ref):
        dn_v = dn_ref[...]
        r = _rms_stats(dn_v)
        g = g4_ref[...]
        e = x2_ref[...] + dn_v * r * g - t_ref[...]
        dx3 = e * (1.0 / d)
        dx3_ref[...] = dx3
        ddn_ref[...] = _rms_bwd(dn_v, r, g, dx3).astype(BF16)
        _accumulate(dg4_ref, jnp.sum(dx3 * dn_v * r, axis=0, keepdims=True))
        part = 0.5 * jnp.sum(jnp.mean(e * e, axis=-1, keepdims=True), axis=0, keepdims=True)
        _accumulate(loss_ref, jnp.broadcast_to(part, loss_ref.shape))

    return _row_call(body, name="final", rows=x2.shape[0], tr=tr,
                     ins=[(x2, "row"), (dn, "row"), (g4, "vec"), (tgt, "row")],
                     outs=[(d, F32), (d, BF16)], acc_outs=(d, 128))


def _rms_bwd2(dx3, dh2, x2, y, g2, g3, tr=256):
    d = x2.shape[1]

    def body(dx3_ref, dh2_ref, x2_ref, y_ref, g2_ref, g3_ref, dx2_ref, dy_ref, dg3_ref, dg2_ref):
        x2v, dh2v, yv = x2_ref[...], dh2_ref[...], y_ref[...]
        r3 = _rms_stats(x2v)
        dx2 = dx3_ref[...] + _rms_bwd(x2v, r3, g3_ref[...], dh2v)
        dx2_ref[...] = dx2
        _accumulate(dg3_ref, jnp.sum(dh2v * x2v * r3, axis=0, keepdims=True))
        r2 = _rms_stats(yv)
        dy_ref[...] = _rms_bwd(yv, r2, g2_ref[...], dx2).astype(BF16)
        _accumulate(dg2_ref, jnp.sum(dx2 * yv * r2, axis=0, keepdims=True))

    return _row_call(body, name="rms_bwd2", rows=x2.shape[0], tr=tr,
                     ins=[(dx3, "row"), (dh2, "row"), (x2, "row"), (y, "row"), (g2, "vec"), (g3, "vec")],
                     outs=[(d, F32), (d, BF16)], acc_outs=(d, d))


def _rms_bwd1(dx2, dh, x, g1, tr=256):
    d = x.shape[1]

    def body(dx2_ref, dh_ref, x_ref, g1_ref, gx_ref, dg1_ref):
        xv, dhv = x_ref[...], dh_ref[...]
        r = _rms_stats(xv)
        gx_ref[...] = dx2_ref[...] + _rms_bwd(xv, r, g1_ref[...], dhv)
        _accumulate(dg1_ref, jnp.sum(dhv * xv * r, axis=0, keepdims=True))

    return _row_call(body, name="rms_bwd1", rows=x.shape[0], tr=tr,
                     ins=[(dx2, "row"), (dh, "row"), (x, "row"), (g1, "vec")], outs=[(d, F32)], acc_outs=(d,))


def _merge_fwd(proj, o_sb, o_cv, off_sb, off_cv, gw, tr=256):
    s, d = o_sb.shape
    nj = d // gw
    b_sb, b_cv = off_sb // gw, off_cv // gw

    def body(gs_ref, gc_ref, osb_ref, ocv_ref, m_ref):
        m_ref[...] = (_sigmoid(gs_ref[...]) * osb_ref[...] + _sigmoid(gc_ref[...]) * ocv_ref[...]).astype(BF16)

    blk = lambda i, j: (i, j)
    return pl.pallas_call(
        body, name="merge_fwd", grid=(s // tr, nj),
        in_specs=[pl.BlockSpec((tr, gw), lambda i, j: (i, b_sb + j)), pl.BlockSpec((tr, gw), lambda i, j: (i, b_cv + j)),
                  pl.BlockSpec((tr, gw), blk), pl.BlockSpec((tr, gw), blk)],
        out_specs=pl.BlockSpec((tr, gw), blk), out_shape=jax.ShapeDtypeStruct((s, d), BF16),
        compiler_params=_params(dimension_semantics=("parallel", "parallel")),
    )(proj, proj, o_sb, o_cv)


def _merge_bwd(proj, o_sb, o_cv, dmerged, off_sb, off_cv, gw, tr=256):
    s, d = o_sb.shape
    nj = d // gw
    ni = s // tr
    b_sb, b_cv = off_sb // gw, off_cv // gw

    def body(gs_ref, gc_ref, osb_ref, ocv_ref, dm_ref, dosb_ref, docv_ref, dgs_ref, dgc_ref, sgs_ref, sgc_ref):
        dm = dm_ref[...]
        s_sb, s_cv = _sigmoid(gs_ref[...]), _sigmoid(gc_ref[...])
        dosb_ref[...] = (dm * s_sb).astype(BF16)
        docv_ref[...] = (dm * s_cv).astype(BF16)
        dgs = dm * osb_ref[...] * s_sb * (1.0 - s_sb)
        dgc = dm * ocv_ref[...] * s_cv * (1.0 - s_cv)
        dgs_ref[...] = dgs.astype(BF16)
        dgc_ref[...] = dgc.astype(BF16)
        i = pl.program_id(1)
        for ref, val in ((sgs_ref, dgs), (sgc_ref, dgc)):
            col = jnp.sum(val, axis=0, keepdims=True)

            @pl.when(i == 0)
            def _():
                ref[...] = col

            @pl.when(i > 0)
            def _():
                ref[...] += col

    blk = lambda j, i: (i, j)
    outs = pl.pallas_call(
        body, name="merge_bwd", grid=(nj, ni),
        in_specs=[pl.BlockSpec((tr, gw), lambda j, i: (i, b_sb + j)), pl.BlockSpec((tr, gw), lambda j, i: (i, b_cv + j)),
                  pl.BlockSpec((tr, gw), blk), pl.BlockSpec((tr, gw), blk), pl.BlockSpec((tr, gw), blk)],
        out_specs=[pl.BlockSpec((tr, gw), blk), pl.BlockSpec((tr, gw), blk),
                   pl.BlockSpec((tr, gw), blk), pl.BlockSpec((tr, gw), blk),
                   pl.BlockSpec((1, gw), lambda j, i: (0, j)), pl.BlockSpec((1, gw), lambda j, i: (0, j))],
        out_shape=[jax.ShapeDtypeStruct((s, d), BF16)] * 4 + [jax.ShapeDtypeStruct((1, d), F32)] * 2,
        compiler_params=_params(dimension_semantics=("parallel", "arbitrary")),
    )(proj, proj, o_sb, o_cv, dmerged)
    return outs


def _split_bf16(v):
    hi = v.astype(BF16)
    lo = (v - hi.astype(F32)).astype(BF16)
    return hi, lo


def _dot_nt(a, b):
    return lax.dot_general(a, b, (((1,), (1,)), ((), ())), preferred_element_type=F32)


def _dot_tn(a, b):
    return lax.dot_general(a, b, (((0,), (0,)), ((), ())), preferred_element_type=F32)


def _dot_nn(a, b):
    return lax.dot_general(a, b, (((1,), (0,)), ((), ())), preferred_element_type=F32)


def _sum_right(v, tri):
    hi, lo = _split_bf16(v)
    return _dot_nn(hi, tri) + _dot_nn(lo, tri)


def _sb_tile(q, kb, i, j, tq, scale):
    z = _dot_nt(q, kb) * scale
    row = lax.broadcasted_iota(jnp.int32, z.shape, 0) + i * tq
    col = lax.broadcasted_iota(jnp.int32, z.shape, 1) + j * tq
    mask = col < row
    sp = jnp.log1p(jnp.exp(-jnp.abs(z)))
    log_b = jnp.minimum(z, 0.0) - sp
    log_1m = jnp.where(mask, log_b - z, 0.0)
    return log_b, log_1m, mask


def _tri(tq, before):
    r = lax.broadcasted_iota(jnp.int32, (tq, tq), 0)
    c = lax.broadcasted_iota(jnp.int32, (tq, tq), 1)
    return jnp.where((r < c) if before else (r > c), 1.0, 0.0).astype(BF16)


def _attn_fwd(proj, n_heads, d_sb, tq=256):
    s = proj.shape[0]
    tq = _tile(s, tq)
    hb = d_sb // HEAD_DIM
    scale = 1.0 / math.sqrt(HEAD_DIM)

    def body(q_ref, k_ref, v_ref, o_ref):
        i = pl.program_id(1)
        q = q_ref[...].astype(BF16)
        tri = _tri(tq, False)

        def step(jj, carry):
            c_l, acc = carry
            j = i - jj
            rows = pl.ds(pl.multiple_of(j * tq, tq), tq)
            kb = k_ref[rows, :].astype(BF16)
            vb = v_ref[rows, :].astype(BF16)
            log_b, log_1m, mask = _sb_tile(q, kb, i, j, tq, scale)
            suffix = c_l + _sum_right(log_1m, tri)
            a = jnp.where(mask, jnp.exp(log_b + suffix), 0.0)
            acc = acc + _dot_nn(a.astype(BF16), vb)
            return c_l + jnp.sum(log_1m, axis=1, keepdims=True), acc

        _, acc = lax.fori_loop(0, i + 1, step, (jnp.zeros((tq, 1), F32), jnp.zeros((tq, HEAD_DIM), F32)))
        o_ref[...] = acc.astype(BF16)

    return pl.pallas_call(
        body, name="attn_fwd", grid=(n_heads, s // tq),
        in_specs=[pl.BlockSpec((tq, HEAD_DIM), lambda h, i: (i, h)),
                  pl.BlockSpec((s, HEAD_DIM), lambda h, i: (0, hb + h)),
                  pl.BlockSpec((s, HEAD_DIM), lambda h, i: (0, 2 * hb + h))],
        out_specs=pl.BlockSpec((tq, HEAD_DIM), lambda h, i: (i, h)),
        out_shape=jax.ShapeDtypeStruct((s, d_sb), BF16),
        compiler_params=_params(dimension_semantics=("parallel", "arbitrary")),
    )(proj, proj, proj)


def _attn_bwd(proj, dattn, n_heads, d_sb, tq=256):
    s = proj.shape[0]
    tq = _tile(s, tq)
    nq = s // tq
    hb = d_sb // HEAD_DIM
    scale = 1.0 / math.sqrt(HEAD_DIM)

    def body(q_ref, k_ref, v_ref, do_ref, dq_ref, dk_ref, dv_ref, sq_ref, sk_ref, sv_ref, dk_acc, dv_acc, g_st, b_st):
        i = pl.program_id(1)

        @pl.when(i == 0)
        def _():
            dk_acc[...] = jnp.zeros_like(dk_acc)
            dv_acc[...] = jnp.zeros_like(dv_acc)

        q = q_ref[...].astype(BF16)
        do = do_ref[...].astype(BF16)
        tri_after = _tri(tq, False)
        tri_before = _tri(tq, True)

        def newest_first(jj, c_l):
            j = i - jj
            rows = pl.ds(pl.multiple_of(j * tq, tq), tq)
            kb = k_ref[rows, :].astype(BF16)
            vb = v_ref[rows, :].astype(BF16)
            log_b, log_1m, mask = _sb_tile(q, kb, i, j, tq, scale)
            suffix = c_l + _sum_right(log_1m, tri_after)
            a = jnp.where(mask, jnp.exp(log_b + suffix), 0.0)
            g_st[j] = a * _dot_nt(do, vb)
            b_st[j] = jnp.where(mask, jnp.exp(log_b), 0.0)
            dv_acc[rows, :] += _dot_tn(a.astype(BF16), do)
            return c_l + jnp.sum(log_1m, axis=1, keepdims=True)

        lax.fori_loop(0, i + 1, newest_first, jnp.zeros((tq, 1), F32))

        def oldest_first(j, carry):
            c_g, dq = carry
            rows = pl.ds(pl.multiple_of(j * tq, tq), tq)
            kb = k_ref[rows, :].astype(BF16)
            g = g_st[j]
            beta = b_st[j]
            g_before = c_g + _sum_right(g, tri_before)
            dz16 = ((g * (1.0 - beta) - g_before * beta) * scale).astype(BF16)
            dk_acc[rows, :] += _dot_tn(dz16, q)
            return c_g + jnp.sum(g, axis=1, keepdims=True), dq + _dot_nn(dz16, kb)

        _, dq = lax.fori_loop(0, i + 1, oldest_first, (jnp.zeros((tq, 1), F32), jnp.zeros((tq, HEAD_DIM), F32)))
        dq_ref[...] = dq.astype(BF16)
        col = jnp.sum(dq, axis=0, keepdims=True)

        @pl.when(i == 0)
        def _():
            sq_ref[...] = col

        @pl.when(i > 0)
        def _():
            sq_ref[...] += col

        @pl.when(i == nq - 1)
        def _():
            dk = dk_acc[...]
            dv = dv_acc[...]
            dk_ref[...] = dk.astype(BF16)
            dv_ref[...] = dv.astype(BF16)
            sk_ref[...] = jnp.sum(dk, axis=0, keepdims=True)
            sv_ref[...] = jnp.sum(dv, axis=0, keepdims=True)

    qblk = pl.BlockSpec((tq, HEAD_DIM), lambda h, i: (i, h))
    kblk = pl.BlockSpec((s, HEAD_DIM), lambda h, i: (0, h))
    col_sum = pl.BlockSpec((1, HEAD_DIM), lambda h, i: (0, h))
    outs = pl.pallas_call(
        body, name="attn_bwd", grid=(n_heads, nq),
        in_specs=[qblk,
                  pl.BlockSpec((s, HEAD_DIM), lambda h, i: (0, hb + h)),
                  pl.BlockSpec((s, HEAD_DIM), lambda h, i: (0, 2 * hb + h)),
                  qblk],
        out_specs=[qblk, kblk, kblk, col_sum, col_sum, col_sum],
        out_shape=[jax.ShapeDtypeStruct((s, d_sb), BF16)] * 3 + [jax.ShapeDtypeStruct((1, d_sb), F32)] * 3,
        scratch_shapes=[pltpu.VMEM((s, HEAD_DIM), F32), pltpu.VMEM((s, HEAD_DIM), F32),
                        pltpu.VMEM((nq, tq, tq), F32), pltpu.VMEM((nq, tq, tq), F32)],
        compiler_params=_params(dimension_semantics=("parallel", "arbitrary")),
    )(proj, proj, proj, dattn)
    return outs


LANES = 128


def _glu(proj, off_a, dc, tr=256):
    s = proj.shape[0]
    ba = off_a // dc

    def body(a_ref, b_ref, u_ref):
        u_ref[...] = a_ref[...] * _sigmoid(b_ref[...])

    return pl.pallas_call(
        body, name="glu", grid=(s // tr,),
        in_specs=[pl.BlockSpec((tr, dc), lambda i: (i, ba)), pl.BlockSpec((tr, dc), lambda i: (i, ba + 1))],
        out_specs=pl.BlockSpec((tr, dc), lambda i: (i, 0)), out_shape=jax.ShapeDtypeStruct((s, dc), F32),
        compiler_params=_params(dimension_semantics=("parallel",)),
    )(proj, proj)


def _conv_fwd(u0, taps, b_dw, g_ln, b_ln, tt=256):
    s, dc = u0.shape
    tt = _tile(s, tt)
    hpb = tt // CONV_HALO

    def body(cur_ref, halo_ref, w_ref, bdw_ref, g_ref, b_ref, u1_ref, u3_ref, xs):
        i = pl.program_id(0)
        xs[pl.ds(0, CONV_HALO), :] = jnp.where(i > 0, halo_ref[...], 0.0)
        xs[pl.ds(CONV_HALO, tt), :] = cur_ref[...]
        for c0 in range(0, dc, LANES):
            cols = pl.ds(c0, LANES)
            acc = jnp.broadcast_to(bdw_ref[:, cols], (tt, LANES))
            for k in range(CONV_WIDTH):
                acc = acc + w_ref[pl.ds(k, 1), cols] * xs[pl.ds(CONV_HALO - (CONV_WIDTH - 1) + k, tt), cols]
            u1_ref[:, cols] = acc
        u1 = u1_ref[...]
        mu = jnp.mean(u1, axis=-1, keepdims=True)
        xc = u1 - mu
        rstd = lax.rsqrt(jnp.mean(xc * xc, axis=-1, keepdims=True) + EPS)
        u2 = xc * rstd * g_ref[...] + b_ref[...]
        u3_ref[...] = (u2 * _sigmoid(u2)).astype(BF16)

    vec = pl.BlockSpec((1, dc), lambda i: (0, 0))
    return pl.pallas_call(
        body, name="conv_fwd", grid=(s // tt,),
        in_specs=[pl.BlockSpec((tt, dc), lambda i: (i, 0)),
                  pl.BlockSpec((CONV_HALO, dc), lambda i: (jnp.maximum(i * hpb - 1, 0), 0)),
                  pl.BlockSpec((CONV_HALO, dc), lambda i: (0, 0)), vec, vec, vec],
        out_specs=[pl.BlockSpec((tt, dc), lambda i: (i, 0)), pl.BlockSpec((tt, dc), lambda i: (i, 0))],
        out_shape=[jax.ShapeDtypeStruct((s, dc), F32), jax.ShapeDtypeStruct((s, dc), BF16)],
        scratch_shapes=[pltpu.VMEM((tt + CONV_HALO, dc), F32)],
        compiler_params=_params(dimension_semantics=("arbitrary",)),
    )(u0, u0, taps, b_dw, g_ln, b_ln)


def _conv_bwd_norm(u1, du3, g_ln, b_ln, tr=256):
    dc = u1.shape[1]

    def body(u1_ref, du3_ref, g_ref, b_ref, du1_ref, dg_ref, db_ref, dbdw_ref):
        u1v = u1_ref[...]
        g = g_ref[...]
        mu = jnp.mean(u1v, axis=-1, keepdims=True)
        xc = u1v - mu
        rstd = lax.rsqrt(jnp.mean(xc * xc, axis=-1, keepdims=True) + EPS)
        xhat = xc * rstd
        u2 = xhat * g + b_ref[...]
        sg = _sigmoid(u2)
        du2 = du3_ref[...] * (sg * (1.0 + u2 * (1.0 - sg)))
        _accumulate(dg_ref, jnp.sum(du2 * xhat, axis=0, keepdims=True))
        _accumulate(db_ref, jnp.sum(du2, axis=0, keepdims=True))
        gy = du2 * g
        du1 = rstd * (gy - jnp.mean(gy, axis=-1, keepdims=True) - xhat * jnp.mean(gy * xhat, axis=-1, keepdims=True))
        du1_ref[...] = du1
        _accumulate(dbdw_ref, jnp.sum(du1, axis=0, keepdims=True))

    return _row_call(body, name="conv_bwd_norm", rows=u1.shape[0], tr=tr,
                     ins=[(u1, "row"), (du3, "row"), (g_ln, "vec"), (b_ln, "vec")],
                     outs=[(dc, F32)], acc_outs=(dc, dc, dc))


def _conv_bwd_taps(du1, u0, proj, taps, off_a, tt=256):
    s, dc = u0.shape
    tt = _tile(s, tt)
    n = s // tt
    hpb = tt // CONV_HALO
    ba = off_a // dc
    lead = CONV_HALO - (CONV_WIDTH - 1)

    def body(d_ref, dnext_ref, cur_ref, halo_ref, a_ref, b_ref, w_ref, da_ref, db_ref, sa_ref, sb_ref, dw_ref,
             xs, ds, du0):
        i = pl.program_id(0)
        xs[pl.ds(0, CONV_HALO), :] = jnp.where(i > 0, halo_ref[...], 0.0)
        xs[pl.ds(CONV_HALO, tt), :] = cur_ref[...]
        ds[pl.ds(0, tt), :] = d_ref[...]
        ds[pl.ds(tt, CONV_HALO), :] = jnp.where(i < n - 1, dnext_ref[...], 0.0)

        @pl.when(i == 0)
        def _():
            dw_ref[...] = jnp.zeros_like(dw_ref)

        for c0 in range(0, dc, LANES):
            cols = pl.ds(c0, LANES)
            d_cur = ds[pl.ds(0, tt), cols]
            acc = jnp.zeros((tt, LANES), F32)
            for k in range(CONV_WIDTH):
                acc = acc + w_ref[pl.ds(k, 1), cols] * ds[pl.ds(CONV_WIDTH - 1 - k, tt), cols]
                dw_ref[pl.ds(k, 1), cols] += jnp.sum(d_cur * xs[pl.ds(lead + k, tt), cols], axis=0, keepdims=True)
            du0[:, cols] = acc
        d0 = du0[...]
        sg = _sigmoid(b_ref[...])
        da = d0 * sg
        db = d0 * a_ref[...] * sg * (1.0 - sg)
        da_ref[...] = da.astype(BF16)
        db_ref[...] = db.astype(BF16)
        _accumulate(sa_ref, jnp.sum(da, axis=0, keepdims=True))
        _accumulate(sb_ref, jnp.sum(db, axis=0, keepdims=True))

    row = pl.BlockSpec((tt, dc), lambda i: (i, 0))
    vec = pl.BlockSpec((1, dc), lambda i: (0, 0))
    taps_spec = pl.BlockSpec((CONV_HALO, dc), lambda i: (0, 0))
    return pl.pallas_call(
        body, name="conv_bwd_taps", grid=(n,),
        in_specs=[row, pl.BlockSpec((CONV_HALO, dc), lambda i: (jnp.minimum((i + 1) * hpb, n * hpb - 1), 0)),
                  row, pl.BlockSpec((CONV_HALO, dc), lambda i: (jnp.maximum(i * hpb - 1, 0), 0)),
                  pl.BlockSpec((tt, dc), lambda i: (i, ba)), pl.BlockSpec((tt, dc), lambda i: (i, ba + 1)), taps_spec],
        out_specs=[row, row, vec, vec, taps_spec],
        out_shape=[jax.ShapeDtypeStruct((s, dc), BF16), jax.ShapeDtypeStruct((s, dc), BF16),
                   jax.ShapeDtypeStruct((1, dc), F32), jax.ShapeDtypeStruct((1, dc), F32),
                   jax.ShapeDtypeStruct((CONV_HALO, dc), F32)],
        scratch_shapes=[pltpu.VMEM((tt + CONV_HALO, dc), F32), pltpu.VMEM((tt + CONV_HALO, dc), F32),
                        pltpu.VMEM((tt, dc), F32)],
        compiler_params=_params(dimension_semantics=("arbitrary",)),
    )(du1, du1, u0, u0, proj, proj, taps)


def _elementwise(body, *, name, ins, out_dtypes, tr=128):
    rows, cols = ins[0].shape
    tr = _tile(rows, tr, 8)
    spec = pl.BlockSpec((tr, cols), lambda i: (i, 0))
    return pl.pallas_call(
        body, name=name, grid=(rows // tr,), in_specs=[spec] * len(ins), out_specs=[spec] * len(out_dtypes),
        out_shape=[jax.ShapeDtypeStruct((rows, cols), dt) for dt in out_dtypes],
        compiler_params=_params(dimension_semantics=("parallel",)),
    )(*ins)


def _cast_bf16(w, name):
    def body(w_ref, o_ref):
        o_ref[...] = w_ref[...].astype(BF16)

    return _elementwise(body, name=name, ins=[w], out_dtypes=[BF16], tr=256)[0]


def _add_pair(a, b, name):
    def body(a_ref, b_ref, o_ref):
        o_ref[...] = (a_ref[...].astype(F32) + b_ref[...].astype(F32)).astype(BF16)

    return _elementwise(body, name=name, ins=[a, b], out_dtypes=[BF16], tr=256)[0]


def _sum_slots(t, name, tr=256):
    _, rows, cols = t.shape
    tr = _tile(rows, tr, 8)

    def body(t_ref, o_ref):
        o_ref[...] = ((t_ref[3].astype(F32) + t_ref[0].astype(F32)) + t_ref[1].astype(F32)) + t_ref[2].astype(F32)

    return pl.pallas_call(
        body, name=name, grid=(rows // tr,), in_specs=[pl.BlockSpec((4, tr, cols), lambda i: (0, i, 0))],
        out_specs=pl.BlockSpec((tr, cols), lambda i: (i, 0)), out_shape=jax.ShapeDtypeStruct((rows, cols), F32),
        compiler_params=_params(dimension_semantics=("parallel",)),
    )(t)


def _adamw(w, g, m, v, name):
    bc1 = 1.0 - ADAM_B1 ** ADAM_STEP
    bc2 = 1.0 - ADAM_B2 ** ADAM_STEP

    def body(w_ref, g_ref, m_ref, v_ref, go_ref, d_ref, mo_ref, vo_ref):
        g = g_ref[...]
        m_new = ADAM_B1 * m_ref[...] + (1.0 - ADAM_B1) * g
        v_new = ADAM_B2 * v_ref[...] + (1.0 - ADAM_B2) * (g * g)
        m_hat = m_new / bc1
        v_hat = v_new / bc2
        go_ref[...] = g
        d_ref[...] = -ADAM_LR * (m_hat / (jnp.sqrt(v_hat) + ADAM_EPS) + ADAM_WD * w_ref[...])
        mo_ref[...] = m_new
        vo_ref[...] = v_new

    return _elementwise(body, name=name, ins=[w, g, m, v], out_dtypes=[F32] * 4, tr=128)


ANY = pl.BlockSpec(memory_space=pl.ANY)


def _mesh_pos():
    return lax.axis_index("x"), lax.axis_index("y"), lax.axis_index("c")


def _other_chips(x, y):
    return [(1 - x, y), (x, 1 - y), (1 - x, 1 - y)]


def _remote(src, dst, send_sems, recv_sems, idx, to):
    return pltpu.make_async_remote_copy(src_ref=src, dst_ref=dst, send_sem=send_sems.at[idx], recv_sem=recv_sems.at[idx],
                                        device_id=to, device_id_type=MESH)


def _all_gather_weights(shards):
    n = len(shards)

    def body(*refs):
        ins, outs = refs[:n], refs[n:2 * n]
        send_sems, recv_sems, local_sems = refs[2 * n:]
        x, y, c = _mesh_pos()
        me = 2 * x + y
        sibling = (x, y, 1 - c)
        chips = _other_chips(x, y)
        started, local = [], []
        for w in range(n):
            half = ins[w].shape[0] // 2
            mine = pl.ds(c * half, half)
            lc = pltpu.make_async_copy(ins[w], outs[w].at[me], local_sems.at[w])
            lc.start()
            local.append(lc)
            for j, (cx, cy) in enumerate(chips):
                cp = _remote(ins[w].at[mine], outs[w].at[me, mine], send_sems, recv_sems, 6 * w + j, (cx, cy, c))
                cp.start()
                started.append(cp)
        for w in range(n):
            half = ins[w].shape[0] // 2
            mine = pl.ds(c * half, half)
            for j, (cx, cy) in enumerate(chips):
                got = outs[w].at[2 * cx + cy, mine]
                _remote(got, got, send_sems, recv_sems, 6 * w + j, (cx, cy, c)).wait_recv()
                fw = _remote(got, got, send_sems, recv_sems, 6 * w + 3 + j, sibling)
                fw.start()
                started.append(fw)
        for w in range(n):
            half = ins[w].shape[0] // 2
            theirs = pl.ds((1 - c) * half, half)
            for j, (cx, cy) in enumerate(chips):
                got = outs[w].at[2 * cx + cy, theirs]
                _remote(got, got, send_sems, recv_sems, 6 * w + 3 + j, sibling).wait_recv()
        for cp in started:
            cp.wait_send()
        for lc in local:
            lc.wait()

    return pl.pallas_call(
        body, name="all_gather_weights", in_specs=[ANY] * n, out_specs=[ANY] * n,
        out_shape=[jax.ShapeDtypeStruct((N_CHIPS,) + s.shape, s.dtype) for s in shards],
        scratch_shapes=[pltpu.SemaphoreType.DMA((6 * n,)), pltpu.SemaphoreType.DMA((6 * n,)), pltpu.SemaphoreType.DMA((n,))],
        compiler_params=_params(has_side_effects=True),
    )(*shards)


def _gather_taps(taps):
    def body(in_ref, out_ref, send_sems, recv_sems, local_sem):
        x, y, c = _mesh_pos()
        me = 2 * x + y
        chips = _other_chips(x, y)
        lc = pltpu.make_async_copy(in_ref, out_ref.at[me], local_sem)
        lc.start()
        sends = [_remote(in_ref, out_ref.at[me], send_sems, recv_sems, j, (cx, cy, c)) for j, (cx, cy) in enumerate(chips)]
        for cp in sends:
            cp.start()
        for j, (cx, cy) in enumerate(chips):
            got = out_ref.at[2 * cx + cy]
            _remote(got, got, send_sems, recv_sems, j, (cx, cy, c)).wait_recv()
        for cp in sends:
            cp.wait_send()
        lc.wait()

    vm = pl.BlockSpec(memory_space=pltpu.VMEM)
    return pl.pallas_call(
        body, name="gather_taps", in_specs=[vm], out_specs=vm,
        out_shape=jax.ShapeDtypeStruct((N_CHIPS,) + taps.shape, taps.dtype),
        scratch_shapes=[pltpu.SemaphoreType.DMA((3,)), pltpu.SemaphoreType.DMA((3,)), pltpu.SemaphoreType.DMA],
        compiler_params=_params(has_side_effects=True),
    )(taps)


def _exchange_halves(grads):
    n = len(grads)

    def body(*refs):
        ins, mine_refs, their_refs = refs[:n], refs[n:2 * n], refs[2 * n:3 * n]
        send_sems, recv_sems, local_sems = refs[3 * n:]
        x, y, c = _mesh_pos()
        sibling = (x, y, 1 - c)
        sends, local = [], []
        for w in range(n):
            half = ins[w].shape[1] // 2
            cp = _remote(ins[w].at[:, pl.ds((1 - c) * half, half), :], their_refs[w], send_sems, recv_sems, w, sibling)
            cp.start()
            sends.append(cp)
            lc = pltpu.make_async_copy(ins[w].at[:, pl.ds(c * half, half), :], mine_refs[w], local_sems.at[w])
            lc.start()
            local.append(lc)
        for w in range(n):
            _remote(their_refs[w], their_refs[w], send_sems, recv_sems, w, sibling).wait_recv()
        for cp in sends:
            cp.wait_send()
        for lc in local:
            lc.wait()

    halves = [jax.ShapeDtypeStruct((g.shape[0], g.shape[1] // 2, g.shape[2]), g.dtype) for g in grads]
    outs = pl.pallas_call(
        body, name="exchange_halves", in_specs=[ANY] * n, out_specs=[ANY] * (2 * n), out_shape=halves + halves,
        scratch_shapes=[pltpu.SemaphoreType.DMA((n,)), pltpu.SemaphoreType.DMA((n,)), pltpu.SemaphoreType.DMA((n,))],
        compiler_params=_params(has_side_effects=True),
    )(*grads)
    return outs[:n], outs[n:]


def _scatter_partials(parts):
    n = len(parts)

    def body(*refs):
        ins, outs = refs[:n], refs[n:2 * n]
        send_sems, recv_sems, local_sems = refs[2 * n:]
        x, y, c = _mesh_pos()
        me = 2 * x + y
        chips = _other_chips(x, y)
        sends, local = [], []
        for w in range(n):
            lc = pltpu.make_async_copy(ins[w].at[me], outs[w].at[3], local_sems.at[w])
            lc.start()
            local.append(lc)
            for j, (cx, cy) in enumerate(chips):
                cp = _remote(ins[w].at[2 * cx + cy], outs[w].at[j], send_sems, recv_sems, 3 * w + j, (cx, cy, c))
                cp.start()
                sends.append(cp)
        for w in range(n):
            for j, (cx, cy) in enumerate(chips):
                _remote(outs[w].at[j], outs[w].at[j], send_sems, recv_sems, 3 * w + j, (cx, cy, c)).wait_recv()
        for cp in sends:
            cp.wait_send()
        for lc in local:
            lc.wait()

    return pl.pallas_call(
        body, name="scatter_partials", in_specs=[ANY] * n, out_specs=[ANY] * n,
        out_shape=[jax.ShapeDtypeStruct(p.shape, p.dtype) for p in parts],
        scratch_shapes=[pltpu.SemaphoreType.DMA((3 * n,)), pltpu.SemaphoreType.DMA((3 * n,)), pltpu.SemaphoreType.DMA((n,))],
        compiler_params=_params(has_side_effects=True),
    )(*parts)


def _share_halves(halves):
    n = len(halves)

    def body(*refs):
        ins, outs = refs[:n], refs[n:2 * n]
        send_sems, recv_sems, local_sems = refs[2 * n:]
        x, y, c = _mesh_pos()
        sibling = (x, y, 1 - c)
        sends, local = [], []
        for w in range(n):
            h = ins[w].shape[0]
            mine = pl.ds(c * h, h)
            cp = _remote(ins[w], outs[w].at[mine], send_sems, recv_sems, w, sibling)
            cp.start()
            sends.append(cp)
            lc = pltpu.make_async_copy(ins[w], outs[w].at[mine], local_sems.at[w])
            lc.start()
            local.append(lc)
        for w in range(n):
            h = ins[w].shape[0]
            got = outs[w].at[pl.ds((1 - c) * h, h)]
            _remote(got, got, send_sems, recv_sems, w, sibling).wait_recv()
        for cp in sends:
            cp.wait_send()
        for lc in local:
            lc.wait()

    return pl.pallas_call(
        body, name="share_halves", in_specs=[ANY] * n, out_specs=[ANY] * n,
        out_shape=[jax.ShapeDtypeStruct((2 * h.shape[0], h.shape[1]), h.dtype) for h in halves],
        scratch_shapes=[pltpu.SemaphoreType.DMA((n,)), pltpu.SemaphoreType.DMA((n,)), pltpu.SemaphoreType.DMA((n,))],
        compiler_params=_params(has_side_effects=True),
    )(*halves)


def _all_reduce_small(buf):
    rows, width = buf.shape

    def body(in_ref, out_ref, gathered, send_sems, recv_sems):
        x, y, c = _mesh_pos()
        me = 4 * x + 2 * y + c
        gathered[me] = in_ref[...]
        flips = [(dx, dy, dz) for dx in (0, 1) for dy in (0, 1) for dz in (0, 1)][1:]
        peers = [((1 - x) if dx else x, (1 - y) if dy else y, (1 - c) if dz else c) for dx, dy, dz in flips]
        sends = [_remote(in_ref, gathered.at[me], send_sems, recv_sems, k, peer) for k, peer in enumerate(peers)]
        for cp in sends:
            cp.start()
        for k, (px, py, pc) in enumerate(peers):
            got = gathered.at[4 * px + 2 * py + pc]
            _remote(got, got, send_sems, recv_sems, k, (px, py, pc)).wait_recv()
        for cp in sends:
            cp.wait_send()
        total = gathered[0]
        for d in range(1, 8):
            total = total + gathered[d]
        out_ref[...] = total

    vm = pl.BlockSpec(memory_space=pltpu.VMEM)
    return pl.pallas_call(
        body, name="all_reduce_small", in_specs=[vm], out_specs=vm, out_shape=jax.ShapeDtypeStruct(buf.shape, F32),
        scratch_shapes=[pltpu.VMEM((8, rows, width), F32), pltpu.SemaphoreType.DMA((7,)), pltpu.SemaphoreType.DMA((7,))],
        compiler_params=_params(has_side_effects=True),
    )(buf)


PACK_W = 2048
PACK_ROWS = 8


def _pack(vectors):
    flat = jnp.concatenate([v.reshape(-1) for v in vectors])
    unit = PACK_W * PACK_ROWS
    total = -(-flat.shape[0] // unit) * unit
    return jnp.pad(flat, (0, total - flat.shape[0])).reshape(total // PACK_W, PACK_W)


def _unpack(buf, shapes):
    flat = buf.reshape(-1)
    out, pos = [], 0
    for shp in shapes:
        size = math.prod(shp)
        out.append(flat[pos:pos + size].reshape(shp))
        pos += size
    return out


def _pad_rows(a, rows):
    return jnp.pad(a, ((0, rows - a.shape[0]), (0, 0)))


def kernel(x, g_pre_mix, w_in, b_in, w_dw, b_dw, g_conv_ln, b_conv_ln, w_sb_out, w_conv_out, w_o, g_post_mix, g_pre_mlp, w_up, w_down, g_post_mlp, loss_target, m_g_pre_mix, m_w_in, m_b_in, m_w_dw, m_b_dw, m_g_conv_ln, m_b_conv_ln, m_w_sb_out, m_w_conv_out, m_w_o, m_g_post_mix, m_g_pre_mlp, m_w_up, m_w_down, m_g_post_mlp, v_g_pre_mix, v_w_in, v_b_in, v_w_dw, v_b_dw, v_g_conv_ln, v_b_conv_ln, v_w_sb_out, v_w_conv_out, v_w_o, v_g_post_mix, v_g_pre_mlp, v_w_up, v_w_down, v_g_post_mlp):
    xs = x[0]
    tgt = loss_target[0]
    s, d = xs.shape
    d_sb = w_sb_out.shape[1]
    dc = w_conv_out.shape[1]
    d_in = w_in.shape[2] * N_CHIPS
    n_heads = d_sb // HEAD_DIM
    off_a = 3 * d_sb
    off_sb = off_a + 2 * dc
    off_cv = off_sb + d
    gw = dc
    assert off_a % dc == 0 and off_sb % gw == 0 and d % gw == 0 and d_in == off_cv + d
    chip = 2 * lax.axis_index("x") + lax.axis_index("y")

    big = dict(w_in=w_in[0], w_sb_out=w_sb_out[0], w_conv_out=w_conv_out[0], w_o=w_o[0], w_up=w_up[0], w_down=w_down[0])
    names = list(big)
    col_sharded = {"w_in", "w_sb_out", "w_conv_out", "w_up"}
    shards16 = [_cast_bf16(big[n], "cast_" + n) for n in names]
    gathered = dict(zip(names, _all_gather_weights(shards16)))
    for n in names:
        if n not in col_sharded:
            g4 = gathered[n]
            gathered[n] = g4.reshape(g4.shape[0] * g4.shape[1], g4.shape[2])
    cs = w_dw.shape[2]
    taps4 = _gather_taps(_pad_rows(w_dw[0], CONV_HALO))
    taps = jnp.transpose(taps4, (1, 0, 2)).reshape(CONV_HALO, N_CHIPS * cs)

    h = _rms_fwd(xs, g_pre_mix)
    proj = _mm(h, gathered["w_in"], name="mm_proj", b_groups=N_CHIPS, bias=b_in, tn=768)
    attn = _attn_fwd(proj, n_heads, d_sb)
    u0 = _glu(proj, off_a, dc)
    u1, u3 = _conv_fwd(u0, taps, b_dw, g_conv_ln, b_conv_ln)
    o_sb = _mm(attn, gathered["w_sb_out"], name="mm_o_sb", b_groups=N_CHIPS)
    o_cv = _mm(u3, gathered["w_conv_out"], name="mm_o_cv", b_groups=N_CHIPS)
    merged = _merge_fwd(proj, o_sb, o_cv, off_sb, off_cv, gw)
    y = _mm(merged, gathered["w_o"], name="mm_y")
    x2, h2 = _resid_rms(xs, y, g_post_mix, g_pre_mlp)
    up, f = _mm(h2, gathered["w_up"], name="mm_up", b_groups=N_CHIPS, out_dtypes=(F32, BF16),
                epilogue=lambda acc: (acc, jnp.square(jnp.maximum(acc, 0.0))))
    dn = _mm(f, gathered["w_down"], name="mm_down")
    dx3, d_dn, dg_post_mlp, loss_part = _final(x2, dn, g_post_mlp, tgt)

    grads = {}
    grads["w_down"] = _mm(f, d_dn, name="mm_dw_down", ta=True, out_dtypes=(BF16,))
    dup = _mm(d_dn, gathered["w_down"], name="mm_df", tb=True, extra=[up], out_dtypes=(BF16,),
              epilogue=lambda acc, upv: (acc * (2.0 * jnp.maximum(upv, 0.0)),))
    grads["w_up"] = _mm(h2, dup, name="mm_dw_up", ta=True, out_groups=N_CHIPS, out_dtypes=(BF16,))
    dh2 = _mm(dup, gathered["w_up"], name="mm_dh2", tb=True, b_groups=N_CHIPS)
    dx2, dy, dg_pre_mlp, dg_post_mix = _rms_bwd2(dx3, dh2, x2, y, g_post_mix, g_pre_mlp)
    grads["w_o"] = _mm(merged, dy, name="mm_dw_o", ta=True, out_dtypes=(BF16,))
    dmerged = _mm(dy, gathered["w_o"], name="mm_dmerged", tb=True)
    do_sb, do_cv, dgate_sb, dgate_cv, s_gate_sb, s_gate_cv = _merge_bwd(proj, o_sb, o_cv, dmerged, off_sb, off_cv, gw)
    grads["w_sb_out"] = _mm(attn, do_sb, name="mm_dw_sb", ta=True, out_groups=N_CHIPS, out_dtypes=(BF16,))
    dattn = _mm(do_sb, gathered["w_sb_out"], name="mm_dattn", tb=True, b_groups=N_CHIPS)
    grads["w_conv_out"] = _mm(u3, do_cv, name="mm_dw_cv", ta=True, out_groups=N_CHIPS, out_dtypes=(BF16,))
    du3 = _mm(do_cv, gathered["w_conv_out"], name="mm_du3", tb=True, b_groups=N_CHIPS)
    dq, dk, dv, s_q, s_k, s_v = _attn_bwd(proj, dattn, n_heads, d_sb)
    du1, dg_conv_ln, db_conv_ln, db_dw = _conv_bwd_norm(u1, du3, g_conv_ln, b_conv_ln)
    dglu_a, dglu_b, s_a, s_b, dtaps = _conv_bwd_taps(du1, u0, proj, taps, off_a)
    dproj = jnp.concatenate([dq, dk, dv, dglu_a, dglu_b, dgate_sb, dgate_cv], axis=1)
    db_in = jnp.concatenate([s_q, s_k, s_v, s_a, s_b, s_gate_sb, s_gate_cv], axis=1)
    grads["w_in"] = _mm(h, dproj, name="mm_dw_in", ta=True, out_groups=N_CHIPS, out_dtypes=(BF16,), tn=768)
    dh = _mm(dproj, gathered["w_in"], name="mm_dh", tb=True, b_groups=N_CHIPS, tk=768)
    grad_x, dg_pre_mix = _rms_bwd1(dx2, dh, xs, g_pre_mix)

    blocks = []
    for n in names:
        g = grads[n]
        blocks.append(g if g.ndim == 3 else g.reshape(N_CHIPS, g.shape[0] // N_CHIPS, g.shape[1]))
    mine, theirs = _exchange_halves(blocks)
    pair_sums = []
    for n, a, b in zip(names, mine, theirs):
        flat = (a.shape[0] * a.shape[1], a.shape[2])
        pair_sums.append(_add_pair(a.reshape(flat), b.reshape(flat), "pair_sum_" + n).reshape(a.shape))
    slots = _scatter_partials(pair_sums)
    halves = [_sum_slots(t, "chip_sum_" + n) for n, t in zip(names, slots)]
    reduced = dict(zip(names, _share_halves(halves)))

    moments = dict(w_in=(m_w_in, v_w_in), w_sb_out=(m_w_sb_out, v_w_sb_out), w_conv_out=(m_w_conv_out, v_w_conv_out),
                   w_o=(m_w_o, v_w_o), w_up=(m_w_up, v_w_up), w_down=(m_w_down, v_w_down))
    out_g, out_d, out_m, out_v = {}, {}, {}, {}
    for n in names:
        mm_, vv_ = moments[n]
        res = _adamw(big[n], reduced[n], mm_[0], vv_[0], "adamw_" + n)
        out_g[n], out_d[n], out_m[n], out_v[n] = [r[None] for r in res]

    small = ["g_pre_mix", "b_in", "b_dw", "g_conv_ln", "b_conv_ln", "g_post_mix", "g_pre_mlp", "g_post_mlp"]
    small_w = dict(g_pre_mix=g_pre_mix, b_in=b_in, b_dw=b_dw, g_conv_ln=g_conv_ln, b_conv_ln=b_conv_ln,
                   g_post_mix=g_post_mix, g_pre_mlp=g_pre_mlp, g_post_mlp=g_post_mlp)
    small_m = dict(g_pre_mix=m_g_pre_mix, b_in=m_b_in, b_dw=m_b_dw, g_conv_ln=m_g_conv_ln, b_conv_ln=m_b_conv_ln,
                   g_post_mix=m_g_post_mix, g_pre_mlp=m_g_pre_mlp, g_post_mlp=m_g_post_mlp)
    small_v = dict(g_pre_mix=v_g_pre_mix, b_in=v_b_in, b_dw=v_b_dw, g_conv_ln=v_g_conv_ln, b_conv_ln=v_b_conv_ln,
                   g_post_mix=v_g_post_mix, g_pre_mlp=v_g_pre_mlp, g_post_mlp=v_g_post_mlp)
    small_g = dict(g_pre_mix=dg_pre_mix, b_in=db_in, b_dw=db_dw, g_conv_ln=dg_conv_ln, b_conv_ln=db_conv_ln,
                   g_post_mix=dg_post_mix, g_pre_mlp=dg_pre_mlp, g_post_mlp=dg_post_mlp)
    shapes = [small_w[n].shape for n in small]
    tail_shapes = [loss_part.shape, dtaps.shape]
    summed = _all_reduce_small(_pack([small_g[n] for n in small] + [loss_part, dtaps]))
    zeros_tail = [jnp.zeros(shp, F32) for shp in tail_shapes]
    res = _adamw(_pack([small_w[n] for n in small] + zeros_tail), summed,
                 _pack([small_m[n] for n in small] + zeros_tail), _pack([small_v[n] for n in small] + zeros_tail),
                 "adamw_small")
    unpacked = [_unpack(r, shapes + tail_shapes) for r in res]
    for i, n in enumerate(small):
        out_g[n], out_d[n], out_m[n], out_v[n] = [u[i] for u in unpacked]
    loss = unpacked[0][len(small)][0, 0]
    taps_grad = lax.dynamic_slice(unpacked[0][len(small) + 1], (0, chip * cs), (CONV_HALO, cs))
    res = _adamw(_pad_rows(w_dw[0], CONV_HALO), taps_grad, _pad_rows(m_w_dw[0], CONV_HALO), _pad_rows(v_w_dw[0], CONV_HALO),
                 "adamw_taps")
    out_g["w_dw"], out_d["w_dw"], out_m["w_dw"], out_v["w_dw"] = [r[:CONV_WIDTH][None] for r in res]

    order = ["g_pre_mix", "w_in", "b_in", "w_dw", "b_dw", "g_conv_ln", "b_conv_ln", "w_sb_out", "w_conv_out", "w_o",
             "g_post_mix", "g_pre_mlp", "w_up", "w_down", "g_post_mlp"]
    return (loss, grad_x[None], *[out_g[n] for n in order], *[out_d[n] for n in order],
            *[out_m[n] for n in order], *[out_v[n] for n in order])
```

```python
import functools
import math

import jax
import jax.numpy as jnp
from jax import lax
from jax.experimental import pallas as pl
from jax.experimental.pallas import tpu as pltpu

F32 = jnp.float32
BF16 = jnp.bfloat16
MESH = pl.DeviceIdType.MESH

HEAD_DIM = 128
CONV_WIDTH = 31
CONV_HALO = 32
EPS = 1e-6
ADAM_LR = 0.001
ADAM_B1 = 0.9
ADAM_B2 = 0.999
ADAM_EPS = 1e-08
ADAM_WD = 0.01
ADAM_STEP = 10
N_CHIPS = 4
VMEM_LIMIT = 48 * 1024 * 1024


def _params(**kw):
    return pltpu.CompilerParams(vmem_limit_bytes=VMEM_LIMIT, **kw)


def _tile(n, want, align=128):
    if n <= want:
        return n
    for t in range(want - want % align, 0, -align):
        if n % t == 0:
            return t
    raise ValueError((n, want, align))


def _mm(a, b, *, name, ta=False, tb=False, b_groups=1, out_groups=1, bias=None, extra=None, epilogue=None,
        out_dtypes=(F32,), tm=1024, tn=1024, tk=512):
    if ta:
        K, M = a.shape
    else:
        M, K = a.shape
    if b_groups == 1:
        br, bc = b.shape
    else:
        _, br, bcg = b.shape
        bc = bcg * b_groups
    if tb:
        N, Kb = br, bc
    else:
        Kb, N = br, bc
    assert Kb == K, (name, a.shape, b.shape)
    tm, tn, tk = _tile(M, tm), _tile(N, tn), _tile(K, tk)
    if b_groups > 1:
        if tb:
            tk = _tile(K // b_groups, tk)
        else:
            tn = _tile(N // b_groups, tn)
    if out_groups > 1:
        tn = _tile(N // out_groups, tn)
        if b_groups > 1 and not tb:
            tn = _tile(N // b_groups, tn)
    nm, nn, nk = M // tm, N // tn, K // tk
    extra = tuple(extra or ())
    n_extra = len(extra)
    has_bias = bias is not None
    n_out = len(out_dtypes)

    a_spec = pl.BlockSpec((tk, tm), lambda i, j, k: (k, i)) if ta else pl.BlockSpec((tm, tk), lambda i, j, k: (i, k))
    if b_groups == 1:
        b_spec = pl.BlockSpec((tn, tk), lambda i, j, k: (j, k)) if tb else pl.BlockSpec((tk, tn), lambda i, j, k: (k, j))
    elif tb:
        kpg = (K // b_groups) // tk
        b_spec = pl.BlockSpec((None, tn, tk), lambda i, j, k: (k // kpg, j, k % kpg))
    else:
        npg = (N // b_groups) // tn
        b_spec = pl.BlockSpec((None, tk, tn), lambda i, j, k: (j // npg, k, j % npg))
    in_specs = [a_spec, b_spec]
    operands = [a, b]
    if has_bias:
        in_specs.append(pl.BlockSpec((1, tn), lambda i, j, k: (0, j)))
        operands.append(bias)
    for e in extra:
        in_specs.append(pl.BlockSpec((tm, tn), lambda i, j, k: (i, j)))
        operands.append(e)
    if out_groups == 1:
        o_spec = pl.BlockSpec((tm, tn), lambda i, j, k: (i, j))
        o_shape = (M, N)
    else:
        opg = (N // out_groups) // tn
        o_spec = pl.BlockSpec((None, tm, tn), lambda i, j, k: (j // opg, i, j % opg))
        o_shape = (out_groups, M, N // out_groups)
    dims = (((0 if ta else 1,), (1 if tb else 0,)), ((), ()))

    def body(*refs):
        a_ref, b_ref = refs[0], refs[1]
        pos = 2
        bias_ref = None
        if has_bias:
            bias_ref = refs[pos]
            pos += 1
        extra_refs = refs[pos:pos + n_extra]
        pos += n_extra
        out_refs = refs[pos:pos + n_out]
        pos += n_out
        acc_ref = refs[pos] if nk > 1 else None

        part = lax.dot_general(a_ref[...].astype(BF16), b_ref[...].astype(BF16), dims, preferred_element_type=F32)

        def finish(acc):
            if has_bias:
                acc = acc + bias_ref[...]
            outs = epilogue(acc, *[e[...] for e in extra_refs]) if epilogue is not None else (acc,)
            for o_ref, o in zip(out_refs, outs):
                o_ref[...] = o.astype(o_ref.dtype)

        if nk == 1:
            finish(part)
        else:
            k = pl.program_id(2)

            @pl.when(k == 0)
            def _():
                acc_ref[...] = part

            @pl.when(k > 0)
            def _():
                acc_ref[...] += part

            @pl.when(k == nk - 1)
            def _():
                finish(acc_ref[...])

    outs = pl.pallas_call(
        body,
        name=name,
        grid=(nm, nn, nk),
        in_specs=in_specs,
        out_specs=[o_spec] * n_out,
        out_shape=[jax.ShapeDtypeStruct(o_shape, dt) for dt in out_dtypes],
        scratch_shapes=[pltpu.VMEM((tm, tn), F32)] if nk > 1 else [],
        compiler_params=_params(dimension_semantics=("parallel", "parallel", "arbitrary")),
    )(*operands)
    return outs[0] if n_out == 1 else outs


def _rms_stats(x):
    return lax.rsqrt(jnp.mean(x * x, axis=-1, keepdims=True) + EPS)


def _rms_bwd(x, r, g, dy):
    gy = dy * g
    return r * gy - x * (r * r * r) * jnp.mean(x * gy, axis=-1, keepdims=True)


def _sigmoid(x):
    return 1.0 / (1.0 + jnp.exp(-x))


def _row_call(body, *, name, rows, tr, ins, outs, acc_outs=()):
    n = rows // tr
    in_specs = []
    for arr, kind in ins:
        if kind == "row":
            in_specs.append(pl.BlockSpec((tr, arr.shape[1]), lambda i: (i, 0)))
        else:
            in_specs.append(pl.BlockSpec(arr.shape, lambda i: (0, 0)))
    out_specs = [pl.BlockSpec((tr, w), lambda i: (i, 0)) for w, _ in outs]
    out_shape = [jax.ShapeDtypeStruct((rows, w), dt) for w, dt in outs]
    out_specs += [pl.BlockSpec((1, w), lambda i: (0, 0)) for w in acc_outs]
    out_shape += [jax.ShapeDtypeStruct((1, w), F32) for w in acc_outs]
    return pl.pallas_call(
        body, name=name, grid=(n,), in_specs=in_specs, out_specs=out_specs, out_shape=out_shape,
        compiler_params=_params(dimension_semantics=("arbitrary",)),
    )(*[a for a, _ in ins])


def _accumulate(ref, val):
    @pl.when(pl.program_id(0) == 0)
    def _():
        ref[...] = val

    @pl.when(pl.program_id(0) > 0)
    def _():
        ref[...] += val


def _rms_fwd(x, g, tr=256):
    def body(x_ref, g_ref, h_ref):
        xv = x_ref[...]
        h_ref[...] = (xv * _rms_stats(xv) * g_ref[...]).astype(BF16)

    (h,) = _row_call(body, name="rms_fwd", rows=x.shape[0], tr=tr, ins=[(x, "row"), (g, "vec")], outs=[(x.shape[1], BF16)])
    return h


def _resid_rms(x, y, g2, g3, tr=256):
    def body(x_ref, y_ref, g2_ref, g3_ref, x2_ref, h2_ref):
        yv = y_ref[...]
        x2 = x_ref[...] + yv * _rms_stats(yv) * g2_ref[...]
        x2_ref[...] = x2
        h2_ref[...] = (x2 * _rms_stats(x2) * g3_ref[...]).astype(BF16)

    d = x.shape[1]
    return _row_call(body, name="resid_rms", rows=x.shape[0], tr=tr,
                     ins=[(x, "row"), (y, "row"), (g2, "vec"), (g3, "vec")], outs=[(d, F32), (d, BF16)])


def _final(x2, dn, g4, tgt, tr=256):
    d = x2.shape[1]

    def body(x2_ref, dn_ref, g4_ref, t_ref, dx3_ref, ddn_ref, dg4_ref, loss_ref):
        dn_v = dn_ref[...]
        r = _rms_stats(dn_v)
        g = g4_ref[...]
        e = x2_ref[...] + dn_v * r * g - t_ref[...]
        dx3 = e * (1.0 / d)
        dx3_ref[...] = dx3
        ddn_ref[...] = _rms_bwd(dn_v, r, g, dx3).astype(BF16)
        _accumulate(dg4_ref, jnp.sum(dx3 * dn_v * r, axis=0, keepdims=True))
        part = 0.5 * jnp.sum(jnp.mean(e * e, axis=-1, keepdims=True), axis=0, keepdims=True)
        _accumulate(loss_ref, jnp.broadcast_to(part, loss_ref.shape))

    return _row_call(body, name="final", rows=x2.shape[0], tr=tr,
                     ins=[(x2, "row"), (dn, "row"), (g4, "vec"), (tgt, "row")],
                     outs=[(d, F32), (d, BF16)], acc_outs=(d, 128))


def _rms_bwd2(dx3, dh2, x2, y, g2, g3, tr=256):
    d = x2.shape[1]

    def body(dx3_ref, dh2_ref, x2_ref, y_ref, g2_ref, g3_ref, dx2_ref, dy_ref, dg3_ref, dg2_ref):
        x2v, dh2v, yv = x2_ref[...], dh2_ref[...], y_ref[...]
        r3 = _rms_stats(x2v)
        dx2 = dx3_ref[...] + _rms_bwd(x2v, r3, g3_ref[...], dh2v)
        dx2_ref[...] = dx2
        _accumulate(dg3_ref, jnp.sum(dh2v * x2v * r3, axis=0, keepdims=True))
        r2 = _rms_stats(yv)
        dy_ref[...] = _rms_bwd(yv, r2, g2_ref[...], dx2).astype(BF16)
        _accumulate(dg2_ref, jnp.sum(dx2 * yv * r2, axis=0, keepdims=True))

    return _row_call(body, name="rms_bwd2", rows=x2.shape[0], tr=tr,
                     ins=[(dx3, "row"), (dh2, "row"), (x2, "row"), (y, "row"), (g2, "vec"), (g3, "vec")],
                     outs=[(d, F32), (d, BF16)], acc_outs=(d, d))


def _rms_bwd1(dx2, dh, x, g1, tr=256):
    d = x.shape[1]

    def body(dx2_ref, dh_ref, x_ref, g1_ref, gx_ref, dg1_ref):
        xv, dhv = x_ref[...], dh_ref[...]
        r = _rms_stats(xv)
        gx_ref[...] = dx2_ref[...] + _rms_bwd(xv, r, g1_ref[...], dhv)
        _accumulate(dg1_ref, jnp.sum(dhv * xv * r, axis=0, keepdims=True))

    return _row_call(body, name="rms_bwd1", rows=x.shape[0], tr=tr,
                     ins=[(dx2, "row"), (dh, "row"), (x, "row"), (g1, "vec")], outs=[(d, F32)], acc_outs=(d,))


def _merge_fwd(proj, o_sb, o_cv, off_sb, off_cv, gw, tr=256):
    s, d = o_sb.shape
    nj = d // gw
    b_sb, b_cv = off_sb // gw, off_cv // gw

    def body(gs_ref, gc_ref, osb_ref, ocv_ref, m_ref):
        m_ref[...] = (_sigmoid(gs_ref[...]) * osb_ref[...] + _sigmoid(gc_ref[...]) * ocv_ref[...]).astype(BF16)

    blk = lambda i, j: (i, j)
    return pl.pallas_call(
        body, name="merge_fwd", grid=(s // tr, nj),
        in_specs=[pl.BlockSpec((tr, gw), lambda i, j: (i, b_sb + j)), pl.BlockSpec((tr, gw), lambda i, j: (i, b_cv + j)),
                  pl.BlockSpec((tr, gw), blk), pl.BlockSpec((tr, gw), blk)],
        out_specs=pl.BlockSpec((tr, gw), blk), out_shape=jax.ShapeDtypeStruct((s, d), BF16),
        compiler_params=_params(dimension_semantics=("parallel", "parallel")),
    )(proj, proj, o_sb, o_cv)


def _merge_bwd(proj, o_sb, o_cv, dmerged, off_sb, off_cv, gw, tr=256):
    s, d = o_sb.shape
    nj = d // gw
    ni = s // tr
    b_sb, b_cv = off_sb // gw, off_cv // gw

    def body(gs_ref, gc_ref, osb_ref, ocv_ref, dm_ref, dosb_ref, docv_ref, dgs_ref, dgc_ref, sgs_ref, sgc_ref):
        dm = dm_ref[...]
        s_sb, s_cv = _sigmoid(gs_ref[...]), _sigmoid(gc_ref[...])
        dosb_ref[...] = (dm * s_sb).astype(BF16)
        docv_ref[...] = (dm * s_cv).astype(BF16)
        dgs = dm * osb_ref[...] * s_sb * (1.0 - s_sb)
        dgc = dm * ocv_ref[...] * s_cv * (1.0 - s_cv)
        dgs_ref[...] = dgs.astype(BF16)
        dgc_ref[...] = dgc.astype(BF16)
        i = pl.program_id(1)
        for ref, val in ((sgs_ref, dgs), (sgc_ref, dgc)):
            col = jnp.sum(val, axis=0, keepdims=True)

            @pl.when(i == 0)
            def _():
                ref[...] = col

            @pl.when(i > 0)
            def _():
                ref[...] += col

    blk = lambda j, i: (i, j)
    outs = pl.pallas_call(
        body, name="merge_bwd", grid=(nj, ni),
        in_specs=[pl.BlockSpec((tr, gw), lambda j, i: (i, b_sb + j)), pl.BlockSpec((tr, gw), lambda j, i: (i, b_cv + j)),
                  pl.BlockSpec((tr, gw), blk), pl.BlockSpec((tr, gw), blk), pl.BlockSpec((tr, gw), blk)],
        out_specs=[pl.BlockSpec((tr, gw), blk), pl.BlockSpec((tr, gw), blk),
                   pl.BlockSpec((tr, gw), blk), pl.BlockSpec((tr, gw), blk),
                   pl.BlockSpec((1, gw), lambda j, i: (0, j)), pl.BlockSpec((1, gw), lambda j, i: (0, j))],
        out_shape=[jax.ShapeDtypeStruct((s, d), BF16)] * 4 + [jax.ShapeDtypeStruct((1, d), F32)] * 2,
        compiler_params=_params(dimension_semantics=("parallel", "arbitrary")),
    )(proj, proj, o_sb, o_cv, dmerged)
    return outs


def _split_bf16(v):
    hi = v.astype(BF16)
    lo = (v - hi.astype(F32)).astype(BF16)
    return hi, lo


def _dot_nt(a, b):
    return lax.dot_general(a, b, (((1,), (1,)), ((), ())), preferred_element_type=F32)


def _dot_tn(a, b):
    return lax.dot_general(a, b, (((0,), (0,)), ((), ())), preferred_element_type=F32)


def _dot_nn(a, b):
    return lax.dot_general(a, b, (((1,), (0,)), ((), ())), preferred_element_type=F32)


def _sum_right(v, tri):
    hi, lo = _split_bf16(v)
    return _dot_nn(hi, tri) + _dot_nn(lo, tri)


def _sb_tile(q, kb, i, j, tq, scale):
    z = _dot_nt(q, kb) * scale
    row = lax.broadcasted_iota(jnp.int32, z.shape, 0) + i * tq
    col = lax.broadcasted_iota(jnp.int32, z.shape, 1) + j * tq
    mask = col < row
    sp = jnp.log1p(jnp.exp(-jnp.abs(z)))
    log_b = jnp.minimum(z, 0.0) - sp
    log_1m = jnp.where(mask, log_b - z, 0.0)
    return log_b, log_1m, mask


def _tri(tq, before):
    r = lax.broadcasted_iota(jnp.int32, (tq, tq), 0)
    c = lax.broadcasted_iota(jnp.int32, (tq, tq), 1)
    return jnp.where((r < c) if before else (r > c), 1.0, 0.0).astype(BF16)


def _attn_fwd(proj, n_heads, d_sb, tq=256):
    s = proj.shape[0]
    tq = _tile(s, tq)
    hb = d_sb // HEAD_DIM
    scale = 1.0 / math.sqrt(HEAD_DIM)

    def body(q_ref, k_ref, v_ref, o_ref):
        i = pl.program_id(1)
        q = q_ref[...].astype(BF16)
        tri = _tri(tq, False)

        def step(jj, carry):
            c_l, acc = carry
            j = i - jj
            rows = pl.ds(pl.multiple_of(j * tq, tq), tq)
            kb = k_ref[rows, :].astype(BF16)
            vb = v_ref[rows, :].astype(BF16)
            log_b, log_1m, mask = _sb_tile(q, kb, i, j, tq, scale)
            suffix = c_l + _sum_right(log_1m, tri)
            a = jnp.where(mask, jnp.exp(log_b + suffix), 0.0)
            acc = acc + _dot_nn(a.astype(BF16), vb)
            return c_l + jnp.sum(log_1m, axis=1, keepdims=True), acc

        _, acc = lax.fori_loop(0, i + 1, step, (jnp.zeros((tq, 1), F32), jnp.zeros((tq, HEAD_DIM), F32)))
        o_ref[...] = acc.astype(BF16)

    return pl.pallas_call(
        body, name="attn_fwd", grid=(n_heads, s // tq),
        in_specs=[pl.BlockSpec((tq, HEAD_DIM), lambda h, i: (i, h)),
                  pl.BlockSpec((s, HEAD_DIM), lambda h, i: (0, hb + h)),
                  pl.BlockSpec((s, HEAD_DIM), lambda h, i: (0, 2 * hb + h))],
        out_specs=pl.BlockSpec((tq, HEAD_DIM), lambda h, i: (i, h)),
        out_shape=jax.ShapeDtypeStruct((s, d_sb), BF16),
        compiler_params=_params(dimension_semantics=("parallel", "arbitrary")),
    )(proj, proj, proj)


def _attn_bwd(proj, dattn, n_heads, d_sb, tq=256):
    s = proj.shape[0]
    tq = _tile(s, tq)
    nq = s // tq
    hb = d_sb // HEAD_DIM
    scale = 1.0 / math.sqrt(HEAD_DIM)

    def body(q_ref, k_ref, v_ref, do_ref, dq_ref, dk_ref, dv_ref, sq_ref, sk_ref, sv_ref, dk_acc, dv_acc, g_st, b_st):
        i = pl.program_id(1)

        @pl.when(i == 0)
        def _():
            dk_acc[...] = jnp.zeros_like(dk_acc)
            dv_acc[...] = jnp.zeros_like(dv_acc)

        q = q_ref[...].astype(BF16)
        do = do_ref[...].astype(BF16)
        tri_after = _tri(tq, False)
        tri_before = _tri(tq, True)

        def newest_first(jj, c_l):
            j = i - jj
            rows = pl.ds(pl.multiple_of(j * tq, tq), tq)
            kb = k_ref[rows, :].astype(BF16)
            vb = v_ref[rows, :].astype(BF16)
            log_b, log_1m, mask = _sb_tile(q, kb, i, j, tq, scale)
            suffix = c_l + _sum_right(log_1m, tri_after)
            a = jnp.where(mask, jnp.exp(log_b + suffix), 0.0)
            g_st[j] = a * _dot_nt(do, vb)
            b_st[j] = jnp.where(mask, jnp.exp(log_b), 0.0)
            dv_acc[rows, :] += _dot_tn(a.astype(BF16), do)
            return c_l + jnp.sum(log_1m, axis=1, keepdims=True)

        lax.fori_loop(0, i + 1, newest_first, jnp.zeros((tq, 1), F32))

        def oldest_first(j, carry):
            c_g, dq = carry
            rows = pl.ds(pl.multiple_of(j * tq, tq), tq)
            kb = k_ref[rows, :].astype(BF16)
            g = g_st[j]
            beta = b_st[j]
            g_before = c_g + _sum_right(g, tri_before)
            dz16 = ((g * (1.0 - beta) - g_before * beta) * scale).astype(BF16)
            dk_acc[rows, :] += _dot_tn(dz16, q)
            return c_g + jnp.sum(g, axis=1, keepdims=True), dq + _dot_nn(dz16, kb)

        _, dq = lax.fori_loop(0, i + 1, oldest_first, (jnp.zeros((tq, 1), F32), jnp.zeros((tq, HEAD_DIM), F32)))
        dq_ref[...] = dq.astype(BF16)
        col = jnp.sum(dq, axis=0, keepdims=True)

        @pl.when(i == 0)
        def _():
            sq_ref[...] = col

        @pl.when(i > 0)
        def _():
            sq_ref[...] += col

        @pl.when(i == nq - 1)
        def _():
            dk = dk_acc[...]
            dv = dv_acc[...]
            dk_ref[...] = dk.astype(BF16)
            dv_ref[...] = dv.astype(BF16)
            sk_ref[...] = jnp.sum(dk, axis=0, keepdims=True)
            sv_ref[...] = jnp.sum(dv, axis=0, keepdims=True)

    qblk = pl.BlockSpec((tq, HEAD_DIM), lambda h, i: (i, h))
    kblk = pl.BlockSpec((s, HEAD_DIM), lambda h, i: (0, h))
    col_sum = pl.BlockSpec((1, HEAD_DIM), lambda h, i: (0, h))
    outs = pl.pallas_call(
        body, name="attn_bwd", grid=(n_heads, nq),
        in_specs=[qblk,
                  pl.BlockSpec((s, HEAD_DIM), lambda h, i: (0, hb + h)),
                  pl.BlockSpec((s, HEAD_DIM), lambda h, i: (0, 2 * hb + h)),
                  qblk],
        out_specs=[qblk, kblk, kblk, col_sum, col_sum, col_sum],
        out_shape=[jax.ShapeDtypeStruct((s, d_sb), BF16)] * 3 + [jax.ShapeDtypeStruct((1, d_sb), F32)] * 3,
        scratch_shapes=[pltpu.VMEM((s, HEAD_DIM), F32), pltpu.VMEM((s, HEAD_DIM), F32),
                        pltpu.VMEM((nq, tq, tq), F32), pltpu.VMEM((nq, tq, tq), F32)],
        compiler_params=_params(dimension_semantics=("parallel", "arbitrary")),
    )(proj, proj, proj, dattn)
    return outs


LANES = 128


def _glu(proj, off_a, dc, tr=256):
    s = proj.shape[0]
    ba = off_a // dc

    def body(a_ref, b_ref, u_ref):
        u_ref[...] = a_ref[...] * _sigmoid(b_ref[...])

    return pl.pallas_call(
        body, name="glu", grid=(s // tr,),
        in_specs=[pl.BlockSpec((tr, dc), lambda i: (i, ba)), pl.BlockSpec((tr, dc), lambda i: (i, ba + 1))],
        out_specs=pl.BlockSpec((tr, dc), lambda i: (i, 0)), out_shape=jax.ShapeDtypeStruct((s, dc), F32),
        compiler_params=_params(dimension_semantics=("parallel",)),
    )(proj, proj)


def _conv_fwd(u0, taps, b_dw, g_ln, b_ln, tt=256):
    s, dc = u0.shape
    tt = _tile(s, tt)
    hpb = tt // CONV_HALO

    def body(cur_ref, halo_ref, w_ref, bdw_ref, g_ref, b_ref, u1_ref, u3_ref, xs):
        i = pl.program_id(0)
        xs[pl.ds(0, CONV_HALO), :] = jnp.where(i > 0, halo_ref[...], 0.0)
        xs[pl.ds(CONV_HALO, tt), :] = cur_ref[...]
        for c0 in range(0, dc, LANES):
            cols = pl.ds(c0, LANES)
            acc = jnp.broadcast_to(bdw_ref[:, cols], (tt, LANES))
            for k in range(CONV_WIDTH):
                acc = acc + w_ref[pl.ds(k, 1), cols] * xs[pl.ds(CONV_HALO - (CONV_WIDTH - 1) + k, tt), cols]
            u1_ref[:, cols] = acc
        u1 = u1_ref[...]
        mu = jnp.mean(u1, axis=-1, keepdims=True)
        xc = u1 - mu
        rstd = lax.rsqrt(jnp.mean(xc * xc, axis=-1, keepdims=True) + EPS)
        u2 = xc * rstd * g_ref[...] + b_ref[...]
        u3_ref[...] = (u2 * _sigmoid(u2)).astype(BF16)

    vec = pl.BlockSpec((1, dc), lambda i: (0, 0))
    return pl.pallas_call(
        body, name="conv_fwd", grid=(s // tt,),
        in_specs=[pl.BlockSpec((tt, dc), lambda i: (i, 0)),
                  pl.BlockSpec((CONV_HALO, dc), lambda i: (jnp.maximum(i * hpb - 1, 0), 0)),
                  pl.BlockSpec((CONV_HALO, dc), lambda i: (0, 0)), vec, vec, vec],
        out_specs=[pl.BlockSpec((tt, dc), lambda i: (i, 0)), pl.BlockSpec((tt, dc), lambda i: (i, 0))],
        out_shape=[jax.ShapeDtypeStruct((s, dc), F32), jax.ShapeDtypeStruct((s, dc), BF16)],
        scratch_shapes=[pltpu.VMEM((tt + CONV_HALO, dc), F32)],
        compiler_params=_params(dimension_semantics=("arbitrary",)),
    )(u0, u0, taps, b_dw, g_ln, b_ln)


def _conv_bwd_norm(u1, du3, g_ln, b_ln, tr=256):
    dc = u1.shape[1]

    def body(u1_ref, du3_ref, g_ref, b_ref, du1_ref, dg_ref, db_ref, dbdw_ref):
        u1v = u1_ref[...]
        g = g_ref[...]
        mu = jnp.mean(u1v, axis=-1, keepdims=True)
        xc = u1v - mu
        rstd = lax.rsqrt(jnp.mean(xc * xc, axis=-1, keepdims=True) + EPS)
        xhat = xc * rstd
        u2 = xhat * g + b_ref[...]
        sg = _sigmoid(u2)
        du2 = du3_ref[...] * (sg * (1.0 + u2 * (1.0 - sg)))
        _accumulate(dg_ref, jnp.sum(du2 * xhat, axis=0, keepdims=True))
        _accumulate(db_ref, jnp.sum(du2, axis=0, keepdims=True))
        gy = du2 * g
        du1 = rstd * (gy - jnp.mean(gy, axis=-1, keepdims=True) - xhat * jnp.mean(gy * xhat, axis=-1, keepdims=True))
        du1_ref[...] = du1
        _accumulate(dbdw_ref, jnp.sum(du1, axis=0, keepdims=True))

    return _row_call(body, name="conv_bwd_norm", rows=u1.shape[0], tr=tr,
                     ins=[(u1, "row"), (du3, "row"), (g_ln, "vec"), (b_ln, "vec")],
                     outs=[(dc, F32)], acc_outs=(dc, dc, dc))


def _conv_bwd_taps(du1, u0, proj, taps, off_a, tt=256):
    s, dc = u0.shape
    tt = _tile(s, tt)
    n = s // tt
    hpb = tt // CONV_HALO
    ba = off_a // dc
    lead = CONV_HALO - (CONV_WIDTH - 1)

    def body(d_ref, dnext_ref, cur_ref, halo_ref, a_ref, b_ref, w_ref, da_ref, db_ref, sa_ref, sb_ref, dw_ref,
             xs, ds, du0):
        i = pl.program_id(0)
        xs[pl.ds(0, CONV_HALO), :] = jnp.where(i > 0, halo_ref[...], 0.0)
        xs[pl.ds(CONV_HALO, tt), :] = cur_ref[...]
        ds[pl.ds(0, tt), :] = d_ref[...]
        ds[pl.ds(tt, CONV_HALO), :] = jnp.where(i < n - 1, dnext_ref[...], 0.0)

        @pl.when(i == 0)
        def _():
            dw_ref[...] = jnp.zeros_like(dw_ref)

        for c0 in range(0, dc, LANES):
            cols = pl.ds(c0, LANES)
            d_cur = ds[pl.ds(0, tt), cols]
            acc = jnp.zeros((tt, LANES), F32)
            for k in range(CONV_WIDTH):
                acc = acc + w_ref[pl.ds(k, 1), cols] * ds[pl.ds(CONV_WIDTH - 1 - k, tt), cols]
                dw_ref[pl.ds(k, 1), cols] += jnp.sum(d_cur * xs[pl.ds(lead + k, tt), cols], axis=0, keepdims=True)
            du0[:, cols] = acc
        d0 = du0[...]
        sg = _sigmoid(b_ref[...])
        da = d0 * sg
        db = d0 * a_ref[...] * sg * (1.0 - sg)
        da_ref[...] = da.astype(BF16)
        db_ref[...] = db.astype(BF16)
        _accumulate(sa_ref, jnp.sum(da, axis=0, keepdims=True))
        _accumulate(sb_ref, jnp.sum(db, axis=0, keepdims=True))

    row = pl.BlockSpec((tt, dc), lambda i: (i, 0))
    vec = pl.BlockSpec((1, dc), lambda i: (0, 0))
    taps_spec = pl.BlockSpec((CONV_HALO, dc), lambda i: (0, 0))
    return pl.pallas_call(
        body, name="conv_bwd_taps", grid=(n,),
        in_specs=[row, pl.BlockSpec((CONV_HALO, dc), lambda i: (jnp.minimum((i + 1) * hpb, n * hpb - 1), 0)),
                  row, pl.BlockSpec((CONV_HALO, dc), lambda i: (jnp.maximum(i * hpb - 1, 0), 0)),
                  pl.BlockSpec((tt, dc), lambda i: (i, ba)), pl.BlockSpec((tt, dc), lambda i: (i, ba + 1)), taps_spec],
        out_specs=[row, row, vec, vec, taps_spec],
        out_shape=[jax.ShapeDtypeStruct((s, dc), BF16), jax.ShapeDtypeStruct((s, dc), BF16),
                   jax.ShapeDtypeStruct((1, dc), F32), jax.ShapeDtypeStruct((1, dc), F32),
                   jax.ShapeDtypeStruct((CONV_HALO, dc), F32)],
        scratch_shapes=[pltpu.VMEM((tt + CONV_HALO, dc), F32), pltpu.VMEM((tt + CONV_HALO, dc), F32),
                        pltpu.VMEM((tt, dc), F32)],
        compiler_params=_params(dimension_semantics=("arbitrary",)),
    )(du1, du1, u0, u0, proj, proj, taps)


def _elementwise(body, *, name, ins, out_dtypes, tr=128):
    rows, cols = ins[0].shape
    tr = _tile(rows, tr, 8)
    spec = pl.BlockSpec((tr, cols), lambda i: (i, 0))
    return pl.pallas_call(
        body, name=name, grid=(rows // tr,), in_specs=[spec] * len(ins), out_specs=[spec] * len(out_dtypes),
        out_shape=[jax.ShapeDtypeStruct((rows, cols), dt) for dt in out_dtypes],
        compiler_params=_params(dimension_semantics=("parallel",)),
    )(*ins)


def _prefetched(body, *, name, where, grid, in_specs, out_specs, out_shape, operands):
    return pl.pallas_call(
        body, name=name, out_shape=out_shape,
        grid_spec=pltpu.PrefetchScalarGridSpec(num_scalar_prefetch=1, grid=grid, in_specs=in_specs, out_specs=out_specs),
        compiler_params=_params(dimension_semantics=("parallel",) * len(grid)),
    )(where, *operands)


def _cast_into_block(w, where, name, tr=256):
    rows, cols = w.shape
    tr = _tile(rows, tr, 16)

    def body(where_ref, w_ref, o_ref):
        o_ref[...] = w_ref[...].astype(BF16)

    return _prefetched(
        body, name=name, where=where, grid=(rows // tr,),
        in_specs=[pl.BlockSpec((tr, cols), lambda i, wh: (i, 0))],
        out_specs=pl.BlockSpec((None, tr, cols), lambda i, wh: (wh[0], i, 0)),
        out_shape=jax.ShapeDtypeStruct((N_CHIPS, rows, cols), BF16), operands=[w])


def _add_pair(grad, theirs, where, name, tr=256):
    nb, half, cols = theirs.shape
    tr = _tile(half, tr, 16)
    nh = half // tr

    def body(where_ref, a_ref, b_ref, o_ref):
        o_ref[...] = (a_ref[...].astype(F32) + b_ref[...].astype(F32)).astype(BF16)

    blk = pl.BlockSpec((None, tr, cols), lambda b, i, wh: (b, i, 0))
    return _prefetched(
        body, name=name, where=where, grid=(nb, nh),
        in_specs=[pl.BlockSpec((None, tr, cols), lambda b, i, wh: (b, wh[1] * nh + i, 0)), blk],
        out_specs=blk, out_shape=jax.ShapeDtypeStruct(theirs.shape, BF16), operands=[grad, theirs])


def _sum_slots(part, slots, where, name, tr=256):
    _, half, cols = part.shape
    tr = _tile(half, tr, 16)
    nh = half // tr

    def body(where_ref, p_ref, s_ref, o_ref):
        o_ref[...] = ((p_ref[...].astype(F32) + s_ref[0].astype(F32)) + s_ref[1].astype(F32)) + s_ref[2].astype(F32)

    return _prefetched(
        body, name=name, where=where, grid=(nh,),
        in_specs=[pl.BlockSpec((None, tr, cols), lambda i, wh: (wh[0], i, 0)),
                  pl.BlockSpec((3, tr, cols), lambda i, wh: (0, i, 0))],
        out_specs=pl.BlockSpec((tr, cols), lambda i, wh: (wh[1] * nh + i, 0)),
        out_shape=jax.ShapeDtypeStruct((2 * half, cols), F32), operands=[part, slots])


def _adamw(w, g, m, v, name):
    bc1 = 1.0 - ADAM_B1 ** ADAM_STEP
    bc2 = 1.0 - ADAM_B2 ** ADAM_STEP

    def body(w_ref, g_ref, m_ref, v_ref, go_ref, d_ref, mo_ref, vo_ref):
        g = g_ref[...]
        m_new = ADAM_B1 * m_ref[...] + (1.0 - ADAM_B1) * g
        v_new = ADAM_B2 * v_ref[...] + (1.0 - ADAM_B2) * (g * g)
        m_hat = m_new / bc1
        v_hat = v_new / bc2
        go_ref[...] = g
        d_ref[...] = -ADAM_LR * (m_hat / (jnp.sqrt(v_hat) + ADAM_EPS) + ADAM_WD * w_ref[...])
        mo_ref[...] = m_new
        vo_ref[...] = v_new

    return _elementwise(body, name=name, ins=[w, g, m, v], out_dtypes=[F32] * 4, tr=128)


ANY = pl.BlockSpec(memory_space=pl.ANY)


def _mesh_pos():
    return lax.axis_index("x"), lax.axis_index("y"), lax.axis_index("c")


def _other_chips(x, y):
    return [(1 - x, y), (x, 1 - y), (1 - x, 1 - y)]


def _remote(src, dst, send_sems, recv_sems, idx, to):
    return pltpu.make_async_remote_copy(src_ref=src, dst_ref=dst, send_sem=send_sems.at[idx], recv_sem=recv_sems.at[idx],
                                        device_id=to, device_id_type=MESH)


def _all_gather_weights(slabs):
    n = len(slabs)

    def body(*refs):
        outs = refs[n:2 * n]
        send_sems, recv_sems = refs[2 * n:]
        x, y, c = _mesh_pos()
        me = 2 * x + y
        sibling = (x, y, 1 - c)
        chips = _other_chips(x, y)
        started = []
        for w in range(n):
            half = outs[w].shape[1] // 2
            mine = outs[w].at[me, pl.ds(c * half, half)]
            for j, (cx, cy) in enumerate(chips):
                cp = _remote(mine, mine, send_sems, recv_sems, 6 * w + j, (cx, cy, c))
                cp.start()
                started.append(cp)
        for w in range(n):
            half = outs[w].shape[1] // 2
            for j, (cx, cy) in enumerate(chips):
                got = outs[w].at[2 * cx + cy, pl.ds(c * half, half)]
                _remote(got, got, send_sems, recv_sems, 6 * w + j, (cx, cy, c)).wait_recv()
                fw = _remote(got, got, send_sems, recv_sems, 6 * w + 3 + j, sibling)
                fw.start()
                started.append(fw)
        for w in range(n):
            half = outs[w].shape[1] // 2
            for j, (cx, cy) in enumerate(chips):
                got = outs[w].at[2 * cx + cy, pl.ds((1 - c) * half, half)]
                _remote(got, got, send_sems, recv_sems, 6 * w + 3 + j, sibling).wait_recv()
        for cp in started:
            cp.wait_send()

    return pl.pallas_call(
        body, name="all_gather_weights", in_specs=[ANY] * n, out_specs=[ANY] * n,
        out_shape=[jax.ShapeDtypeStruct(s.shape, s.dtype) for s in slabs],
        input_output_aliases={w: w for w in range(n)},
        scratch_shapes=[pltpu.SemaphoreType.DMA((6 * n,)), pltpu.SemaphoreType.DMA((6 * n,))],
        compiler_params=_params(has_side_effects=True),
    )(*slabs)


def _gather_taps(taps):
    def body(in_ref, out_ref, send_sems, recv_sems, local_sem):
        x, y, c = _mesh_pos()
        me = 2 * x + y
        chips = _other_chips(x, y)
        lc = pltpu.make_async_copy(in_ref, out_ref.at[me], local_sem)
        lc.start()
        sends = [_remote(in_ref, out_ref.at[me], send_sems, recv_sems, j, (cx, cy, c)) for j, (cx, cy) in enumerate(chips)]
        for cp in sends:
            cp.start()
        for j, (cx, cy) in enumerate(chips):
            got = out_ref.at[2 * cx + cy]
            _remote(got, got, send_sems, recv_sems, j, (cx, cy, c)).wait_recv()
        for cp in sends:
            cp.wait_send()
        lc.wait()

    vm = pl.BlockSpec(memory_space=pltpu.VMEM)
    return pl.pallas_call(
        body, name="gather_taps", in_specs=[vm], out_specs=vm,
        out_shape=jax.ShapeDtypeStruct((N_CHIPS,) + taps.shape, taps.dtype),
        scratch_shapes=[pltpu.SemaphoreType.DMA((3,)), pltpu.SemaphoreType.DMA((3,)), pltpu.SemaphoreType.DMA],
        compiler_params=_params(has_side_effects=True),
    )(taps)


def _exchange_halves(grads):
    n = len(grads)

    def body(*refs):
        ins, their_refs = refs[:n], refs[n:2 * n]
        send_sems, recv_sems = refs[2 * n:]
        x, y, c = _mesh_pos()
        sibling = (x, y, 1 - c)
        sends = []
        for w in range(n):
            half = ins[w].shape[1] // 2
            cp = _remote(ins[w].at[:, pl.ds((1 - c) * half, half), :], their_refs[w], send_sems, recv_sems, w, sibling)
            cp.start()
            sends.append(cp)
        for w in range(n):
            _remote(their_refs[w], their_refs[w], send_sems, recv_sems, w, sibling).wait_recv()
        for cp in sends:
            cp.wait_send()

    halves = [jax.ShapeDtypeStruct((g.shape[0], g.shape[1] // 2, g.shape[2]), g.dtype) for g in grads]
    return pl.pallas_call(
        body, name="exchange_halves", in_specs=[ANY] * n, out_specs=[ANY] * n, out_shape=halves,
        scratch_shapes=[pltpu.SemaphoreType.DMA((n,)), pltpu.SemaphoreType.DMA((n,))],
        compiler_params=_params(has_side_effects=True),
    )(*grads)


def _scatter_partials(parts):
    n = len(parts)

    def body(*refs):
        ins, outs = refs[:n], refs[n:2 * n]
        send_sems, recv_sems = refs[2 * n:]
        x, y, c = _mesh_pos()
        chips = _other_chips(x, y)
        sends = []
        for w in range(n):
            for j, (cx, cy) in enumerate(chips):
                cp = _remote(ins[w].at[2 * cx + cy], outs[w].at[j], send_sems, recv_sems, 3 * w + j, (cx, cy, c))
                cp.start()
                sends.append(cp)
        for w in range(n):
            for j, (cx, cy) in enumerate(chips):
                _remote(outs[w].at[j], outs[w].at[j], send_sems, recv_sems, 3 * w + j, (cx, cy, c)).wait_recv()
        for cp in sends:
            cp.wait_send()

    return pl.pallas_call(
        body, name="scatter_partials", in_specs=[ANY] * n, out_specs=[ANY] * n,
        out_shape=[jax.ShapeDtypeStruct((3,) + p.shape[1:], p.dtype) for p in parts],
        scratch_shapes=[pltpu.SemaphoreType.DMA((3 * n,)), pltpu.SemaphoreType.DMA((3 * n,))],
        compiler_params=_params(has_side_effects=True),
    )(*parts)


def _share_halves(halves):
    n = len(halves)

    def body(*refs):
        outs = refs[n:2 * n]
        send_sems, recv_sems = refs[2 * n:]
        x, y, c = _mesh_pos()
        sibling = (x, y, 1 - c)
        sends = []
        for w in range(n):
            h = outs[w].shape[0] // 2
            mine = outs[w].at[pl.ds(c * h, h)]
            cp = _remote(mine, mine, send_sems, recv_sems, w, sibling)
            cp.start()
            sends.append(cp)
        for w in range(n):
            h = outs[w].shape[0] // 2
            got = outs[w].at[pl.ds((1 - c) * h, h)]
            _remote(got, got, send_sems, recv_sems, w, sibling).wait_recv()
        for cp in sends:
            cp.wait_send()

    return pl.pallas_call(
        body, name="share_halves", in_specs=[ANY] * n, out_specs=[ANY] * n,
        out_shape=[jax.ShapeDtypeStruct(h.shape, h.dtype) for h in halves],
        input_output_aliases={w: w for w in range(n)},
        scratch_shapes=[pltpu.SemaphoreType.DMA((n,)), pltpu.SemaphoreType.DMA((n,))],
        compiler_params=_params(has_side_effects=True),
    )(*halves)


def _all_reduce_small(buf):
    rows, width = buf.shape

    def body(in_ref, out_ref, gathered, send_sems, recv_sems):
        x, y, c = _mesh_pos()
        me = 4 * x + 2 * y + c
        gathered[me] = in_ref[...]
        flips = [(dx, dy, dz) for dx in (0, 1) for dy in (0, 1) for dz in (0, 1)][1:]
        peers = [((1 - x) if dx else x, (1 - y) if dy else y, (1 - c) if dz else c) for dx, dy, dz in flips]
        sends = [_remote(in_ref, gathered.at[me], send_sems, recv_sems, k, peer) for k, peer in enumerate(peers)]
        for cp in sends:
            cp.start()
        for k, (px, py, pc) in enumerate(peers):
            got = gathered.at[4 * px + 2 * py + pc]
            _remote(got, got, send_sems, recv_sems, k, (px, py, pc)).wait_recv()
        for cp in sends:
            cp.wait_send()
        total = gathered[0]
        for d in range(1, 8):
            total = total + gathered[d]
        out_ref[...] = total

    vm = pl.BlockSpec(memory_space=pltpu.VMEM)
    return pl.pallas_call(
        body, name="all_reduce_small", in_specs=[vm], out_specs=vm, out_shape=jax.ShapeDtypeStruct(buf.shape, F32),
        scratch_shapes=[pltpu.VMEM((8, rows, width), F32), pltpu.SemaphoreType.DMA((7,)), pltpu.SemaphoreType.DMA((7,))],
        compiler_params=_params(has_side_effects=True),
    )(buf)


PACK_W = 2048
PACK_ROWS = 8


def _pack(vectors):
    flat = jnp.concatenate([v.reshape(-1) for v in vectors])
    unit = PACK_W * PACK_ROWS
    total = -(-flat.shape[0] // unit) * unit
    return jnp.pad(flat, (0, total - flat.shape[0])).reshape(total // PACK_W, PACK_W)


def _unpack(buf, shapes):
    flat = buf.reshape(-1)
    out, pos = [], 0
    for shp in shapes:
        size = math.prod(shp)
        out.append(flat[pos:pos + size].reshape(shp))
        pos += size
    return out


def _pad_rows(a, rows):
    return jnp.pad(a, ((0, rows - a.shape[0]), (0, 0)))


def kernel(x, g_pre_mix, w_in, b_in, w_dw, b_dw, g_conv_ln, b_conv_ln, w_sb_out, w_conv_out, w_o, g_post_mix, g_pre_mlp, w_up, w_down, g_post_mlp, loss_target, m_g_pre_mix, m_w_in, m_b_in, m_w_dw, m_b_dw, m_g_conv_ln, m_b_conv_ln, m_w_sb_out, m_w_conv_out, m_w_o, m_g_post_mix, m_g_pre_mlp, m_w_up, m_w_down, m_g_post_mlp, v_g_pre_mix, v_w_in, v_b_in, v_w_dw, v_b_dw, v_g_conv_ln, v_b_conv_ln, v_w_sb_out, v_w_conv_out, v_w_o, v_g_post_mix, v_g_pre_mlp, v_w_up, v_w_down, v_g_post_mlp):
    xs = x[0]
    tgt = loss_target[0]
    s, d = xs.shape
    d_sb = w_sb_out.shape[1]
    dc = w_conv_out.shape[1]
    d_in = w_in.shape[2] * N_CHIPS
    n_heads = d_sb // HEAD_DIM
    off_a = 3 * d_sb
    off_sb = off_a + 2 * dc
    off_cv = off_sb + d
    gw = dc
    assert off_a % dc == 0 and off_sb % gw == 0 and d % gw == 0 and d_in == off_cv + d
    chip = 2 * lax.axis_index("x") + lax.axis_index("y")

    big = dict(w_in=w_in[0], w_sb_out=w_sb_out[0], w_conv_out=w_conv_out[0], w_o=w_o[0], w_up=w_up[0], w_down=w_down[0])
    names = list(big)
    col_sharded = {"w_in", "w_sb_out", "w_conv_out", "w_up"}
    where = jnp.stack([chip, lax.axis_index("c")]).astype(jnp.int32)
    slabs = [_cast_into_block(big[n], where, "cast_" + n) for n in names]
    gathered = dict(zip(names, _all_gather_weights(slabs)))
    for n in names:
        if n not in col_sharded:
            g4 = gathered[n]
            gathered[n] = g4.reshape(g4.shape[0] * g4.shape[1], g4.shape[2])
    cs = w_dw.shape[2]
    taps4 = _gather_taps(_pad_rows(w_dw[0], CONV_HALO))
    taps = jnp.transpose(taps4, (1, 0, 2)).reshape(CONV_HALO, N_CHIPS * cs)

    h = _rms_fwd(xs, g_pre_mix)
    proj = _mm(h, gathered["w_in"], name="mm_proj", b_groups=N_CHIPS, bias=b_in, tn=768)
    attn = _attn_fwd(proj, n_heads, d_sb)
    u0 = _glu(proj, off_a, dc)
    u1, u3 = _conv_fwd(u0, taps, b_dw, g_conv_ln, b_conv_ln)
    o_sb = _mm(attn, gathered["w_sb_out"], name="mm_o_sb", b_groups=N_CHIPS)
    o_cv = _mm(u3, gathered["w_conv_out"], name="mm_o_cv", b_groups=N_CHIPS)
    merged = _merge_fwd(proj, o_sb, o_cv, off_sb, off_cv, gw)
    y = _mm(merged, gathered["w_o"], name="mm_y")
    x2, h2 = _resid_rms(xs, y, g_post_mix, g_pre_mlp)
    up, f = _mm(h2, gathered["w_up"], name="mm_up", b_groups=N_CHIPS, out_dtypes=(F32, BF16),
                epilogue=lambda acc: (acc, jnp.square(jnp.maximum(acc, 0.0))))
    dn = _mm(f, gathered["w_down"], name="mm_down")
    dx3, d_dn, dg_post_mlp, loss_part = _final(x2, dn, g_post_mlp, tgt)

    grads = {}
    grads["w_down"] = _mm(f, d_dn, name="mm_dw_down", ta=True, out_dtypes=(BF16,))
    dup = _mm(d_dn, gathered["w_down"], name="mm_df", tb=True, extra=[up], out_dtypes=(BF16,),
              epilogue=lambda acc, upv: (acc * (2.0 * jnp.maximum(upv, 0.0)),))
    grads["w_up"] = _mm(h2, dup, name="mm_dw_up", ta=True, out_groups=N_CHIPS, out_dtypes=(BF16,))
    dh2 = _mm(dup, gathered["w_up"], name="mm_dh2", tb=True, b_groups=N_CHIPS)
    dx2, dy, dg_pre_mlp, dg_post_mix = _rms_bwd2(dx3, dh2, x2, y, g_post_mix, g_pre_mlp)
    grads["w_o"] = _mm(merged, dy, name="mm_dw_o", ta=True, out_dtypes=(BF16,))
    dmerged = _mm(dy, gathered["w_o"], name="mm_dmerged", tb=True)
    do_sb, do_cv, dgate_sb, dgate_cv, s_gate_sb, s_gate_cv = _merge_bwd(proj, o_sb, o_cv, dmerged, off_sb, off_cv, gw)
    grads["w_sb_out"] = _mm(attn, do_sb, name="mm_dw_sb", ta=True, out_groups=N_CHIPS, out_dtypes=(BF16,))
    dattn = _mm(do_sb, gathered["w_sb_out"], name="mm_dattn", tb=True, b_groups=N_CHIPS)
    grads["w_conv_out"] = _mm(u3, do_cv, name="mm_dw_cv", ta=True, out_groups=N_CHIPS, out_dtypes=(BF16,))
    du3 = _mm(do_cv, gathered["w_conv_out"], name="mm_du3", tb=True, b_groups=N_CHIPS)
    dq, dk, dv, s_q, s_k, s_v = _attn_bwd(proj, dattn, n_heads, d_sb)
    du1, dg_conv_ln, db_conv_ln, db_dw = _conv_bwd_norm(u1, du3, g_conv_ln, b_conv_ln)
    dglu_a, dglu_b, s_a, s_b, dtaps = _conv_bwd_taps(du1, u0, proj, taps, off_a)
    dproj = jnp.concatenate([dq, dk, dv, dglu_a, dglu_b, dgate_sb, dgate_cv], axis=1)
    db_in = jnp.concatenate([s_q, s_k, s_v, s_a, s_b, s_gate_sb, s_gate_cv], axis=1)
    grads["w_in"] = _mm(h, dproj, name="mm_dw_in", ta=True, out_groups=N_CHIPS, out_dtypes=(BF16,), tn=768)
    dh = _mm(dproj, gathered["w_in"], name="mm_dh", tb=True, b_groups=N_CHIPS, tk=768)
    grad_x, dg_pre_mix = _rms_bwd1(dx2, dh, xs, g_pre_mix)

    blocks = []
    for n in names:
        g = grads[n]
        blocks.append(g if g.ndim == 3 else g.reshape(N_CHIPS, g.shape[0] // N_CHIPS, g.shape[1]))
    theirs = _exchange_halves(blocks)
    pair_sums = [_add_pair(g, t, where, "pair_sum_" + n) for n, g, t in zip(names, blocks, theirs)]
    slots = _scatter_partials(pair_sums)
    halves = [_sum_slots(p, t, where, "chip_sum_" + n) for n, p, t in zip(names, pair_sums, slots)]
    reduced = dict(zip(names, _share_halves(halves)))

    moments = dict(w_in=(m_w_in, v_w_in), w_sb_out=(m_w_sb_out, v_w_sb_out), w_conv_out=(m_w_conv_out, v_w_conv_out),
                   w_o=(m_w_o, v_w_o), w_up=(m_w_up, v_w_up), w_down=(m_w_down, v_w_down))
    out_g, out_d, out_m, out_v = {}, {}, {}, {}
    for n in names:
        mm_, vv_ = moments[n]
        res = _adamw(big[n], reduced[n], mm_[0], vv_[0], "adamw_" + n)
        out_g[n], out_d[n], out_m[n], out_v[n] = [r[None] for r in res]

    small = ["g_pre_mix", "b_in", "b_dw", "g_conv_ln", "b_conv_ln", "g_post_mix", "g_pre_mlp", "g_post_mlp"]
    small_w = dict(g_pre_mix=g_pre_mix, b_in=b_in, b_dw=b_dw, g_conv_ln=g_conv_ln, b_conv_ln=b_conv_ln,
                   g_post_mix=g_post_mix, g_pre_mlp=g_pre_mlp, g_post_mlp=g_post_mlp)
    small_m = dict(g_pre_mix=m_g_pre_mix, b_in=m_b_in, b_dw=m_b_dw, g_conv_ln=m_g_conv_ln, b_conv_ln=m_b_conv_ln,
                   g_post_mix=m_g_post_mix, g_pre_mlp=m_g_pre_mlp, g_post_mlp=m_g_post_mlp)
    small_v = dict(g_pre_mix=v_g_pre_mix, b_in=v_b_in, b_dw=v_b_dw, g_conv_ln=v_g_conv_ln, b_conv_ln=v_b_conv_ln,
                   g_post_mix=v_g_post_mix, g_pre_mlp=v_g_pre_mlp, g_post_mlp=v_g_post_mlp)
    small_g = dict(g_pre_mix=dg_pre_mix, b_in=db_in, b_dw=db_dw, g_conv_ln=dg_conv_ln, b_conv_ln=db_conv_ln,
                   g_post_mix=dg_post_mix, g_pre_mlp=dg_pre_mlp, g_post_mlp=dg_post_mlp)
    shapes = [small_w[n].shape for n in small]
    tail_shapes = [loss_part.shape, dtaps.shape]
    summed = _all_reduce_small(_pack([small_g[n] for n in small] + [loss_part, dtaps]))
    zeros_tail = [jnp.zeros(shp, F32) for shp in tail_shapes]
    res = _adamw(_pack([small_w[n] for n in small] + zeros_tail), summed,
                 _pack([small_m[n] for n in small] + zeros_tail), _pack([small_v[n] for n in small] + zeros_tail),
                 "adamw_small")
    unpacked = [_unpack(r, shapes + tail_shapes) for r in res]
    for i, n in enumerate(small):
        out_g[n], out_d[n], out_m[n], out_v[n] = [u[i] for u in unpacked]
    loss = unpacked[0][len(small)][0, 0]
    taps_grad = lax.dynamic_slice(unpacked[0][len(small) + 1], (0, chip * cs), (CONV_HALO, cs))
    res = _adamw(_pad_rows(w_dw[0], CONV_HALO), taps_grad, _pad_rows(m_w_dw[0], CONV_HALO), _pad_rows(v_w_dw[0], CONV_HALO),
                 "adamw_taps")
    out_g["w_dw"], out_d["w_dw"], out_m["w_dw"], out_v["w_dw"] = [r[:CONV_WIDTH][None] for r in res]

    order = ["g_pre_mix", "w_in", "b_in", "w_dw", "b_dw", "g_conv_ln", "b_conv_ln", "w_sb_out", "w_conv_out", "w_o",
             "g_post_mix", "g_pre_mlp", "w_up", "w_down", "g_post_mlp"]
    return (loss, grad_x[None], *[out_g[n] for n in order], *[out_d[n] for n in order],
            *[out_m[n] for n in order], *[out_v[n] for n in order])
```

```python
import functools
import math

import jax
import jax.numpy as jnp
from jax import lax
from jax.experimental import pallas as pl
from jax.experimental.pallas import tpu as pltpu

F32 = jnp.float32
BF16 = jnp.bfloat16
MESH = pl.DeviceIdType.MESH

HEAD_DIM = 128
CONV_WIDTH = 31
CONV_HALO = 32
EPS = 1e-6
ADAM_LR = 0.001
ADAM_B1 = 0.9
ADAM_B2 = 0.999
ADAM_EPS = 1e-08
ADAM_WD = 0.01
ADAM_STEP = 10
N_CHIPS = 4
VMEM_LIMIT = 48 * 1024 * 1024


def _params(**kw):
    return pltpu.CompilerParams(vmem_limit_bytes=VMEM_LIMIT, **kw)


def _tile(n, want, align=128):
    if n <= want:
        return n
    for t in range(want - want % align, 0, -align):
        if n % t == 0:
            return t
    raise ValueError((n, want, align))


def _mm(a, b, *, name, ta=False, tb=False, b_groups=1, out_groups=1, bias=None, extra=None, epilogue=None,
        out_dtypes=(F32,), tm=1024, tn=1024, tk=512):
    if ta:
        K, M = a.shape
    else:
        M, K = a.shape
    if b_groups == 1:
        br, bc = b.shape
    else:
        _, br, bcg = b.shape
        bc = bcg * b_groups
    if tb:
        N, Kb = br, bc
    else:
        Kb, N = br, bc
    assert Kb == K, (name, a.shape, b.shape)
    tm, tn, tk = _tile(M, tm), _tile(N, tn), _tile(K, tk)
    if b_groups > 1:
        if tb:
            tk = _tile(K // b_groups, tk)
        else:
            tn = _tile(N // b_groups, tn)
    if out_groups > 1:
        tn = _tile(N // out_groups, tn)
        if b_groups > 1 and not tb:
            tn = _tile(N // b_groups, tn)
    nm, nn, nk = M // tm, N // tn, K // tk
    extra = tuple(extra or ())
    n_extra = len(extra)
    has_bias = bias is not None
    n_out = len(out_dtypes)

    a_spec = pl.BlockSpec((tk, tm), lambda i, j, k: (k, i)) if ta else pl.BlockSpec((tm, tk), lambda i, j, k: (i, k))
    if b_groups == 1:
        b_spec = pl.BlockSpec((tn, tk), lambda i, j, k: (j, k)) if tb else pl.BlockSpec((tk, tn), lambda i, j, k: (k, j))
    elif tb:
        kpg = (K // b_groups) // tk
        b_spec = pl.BlockSpec((None, tn, tk), lambda i, j, k: (k // kpg, j, k % kpg))
    else:
        npg = (N // b_groups) // tn
        b_spec = pl.BlockSpec((None, tk, tn), lambda i, j, k: (j // npg, k, j % npg))
    in_specs = [a_spec, b_spec]
    operands = [a, b]
    if has_bias:
        in_specs.append(pl.BlockSpec((1, tn), lambda i, j, k: (0, j)))
        operands.append(bias)
    for e in extra:
        in_specs.append(pl.BlockSpec((tm, tn), lambda i, j, k: (i, j)))
        operands.append(e)
    if out_groups == 1:
        o_spec = pl.BlockSpec((tm, tn), lambda i, j, k: (i, j))
        o_shape = (M, N)
    else:
        opg = (N // out_groups) // tn
        o_spec = pl.BlockSpec((None, tm, tn), lambda i, j, k: (j // opg, i, j % opg))
        o_shape = (out_groups, M, N // out_groups)
    dims = (((0 if ta else 1,), (1 if tb else 0,)), ((), ()))

    def body(*refs):
        a_ref, b_ref = refs[0], refs[1]
        pos = 2
        bias_ref = None
        if has_bias:
            bias_ref = refs[pos]
            pos += 1
        extra_refs = refs[pos:pos + n_extra]
        pos += n_extra
        out_refs = refs[pos:pos + n_out]
        pos += n_out
        acc_ref = refs[pos] if nk > 1 else None

        part = lax.dot_general(a_ref[...].astype(BF16), b_ref[...].astype(BF16), dims, preferred_element_type=F32)

        def finish(acc):
            if has_bias:
                acc = acc + bias_ref[...]
            outs = epilogue(acc, *[e[...] for e in extra_refs]) if epilogue is not None else (acc,)
            for o_ref, o in zip(out_refs, outs):
                o_ref[...] = o.astype(o_ref.dtype)

        if nk == 1:
            finish(part)
        else:
            k = pl.program_id(2)

            @pl.when(k == 0)
            def _():
                acc_ref[...] = part

            @pl.when(k > 0)
            def _():
                acc_ref[...] += part

            @pl.when(k == nk - 1)
            def _():
                finish(acc_ref[...])

    outs = pl.pallas_call(
        body,
        name=name,
        grid=(nm, nn, nk),
        in_specs=in_specs,
        out_specs=[o_spec] * n_out,
        out_shape=[jax.ShapeDtypeStruct(o_shape, dt) for dt in out_dtypes],
        scratch_shapes=[pltpu.VMEM((tm, tn), F32)] if nk > 1 else [],
        compiler_params=_params(dimension_semantics=("parallel", "parallel", "arbitrary")),
    )(*operands)
    return outs[0] if n_out == 1 else outs


def _rms_stats(x):
    return lax.rsqrt(jnp.mean(x * x, axis=-1, keepdims=True) + EPS)


def _rms_bwd(x, r, g, dy):
    gy = dy * g
    return r * gy - x * (r * r * r) * jnp.mean(x * gy, axis=-1, keepdims=True)


def _sigmoid(x):
    return 1.0 / (1.0 + jnp.exp(-x))


def _row_call(body, *, name, rows, tr, ins, outs, acc_outs=()):
    n = rows // tr
    in_specs = []
    for arr, kind in ins:
        if kind == "row":
            in_specs.append(pl.BlockSpec((tr, arr.shape[1]), lambda i: (i, 0)))
        else:
            in_specs.append(pl.BlockSpec(arr.shape, lambda i: (0, 0)))
    out_specs = [pl.BlockSpec((tr, w), lambda i: (i, 0)) for w, _ in outs]
    out_shape = [jax.ShapeDtypeStruct((rows, w), dt) for w, dt in outs]
    out_specs += [pl.BlockSpec((1, w), lambda i: (0, 0)) for w in acc_outs]
    out_shape += [jax.ShapeDtypeStruct((1, w), F32) for w in acc_outs]
    return pl.pallas_call(
        body, name=name, grid=(n,), in_specs=in_specs, out_specs=out_specs, out_shape=out_shape,
        compiler_params=_params(dimension_semantics=("arbitrary",)),
    )(*[a for a, _ in ins])


def _accumulate(ref, val):
    @pl.when(pl.program_id(0) == 0)
    def _():
        ref[...] = val

    @pl.when(pl.program_id(0) > 0)
    def _():
        ref[...] += val


def _rms_fwd(x, g, tr=256):
    def body(x_ref, g_ref, h_ref):
        xv = x_ref[...]
        h_ref[...] = (xv * _rms_stats(xv) * g_ref[...]).astype(BF16)

    (h,) = _row_call(body, name="rms_fwd", rows=x.shape[0], tr=tr, ins=[(x, "row"), (g, "vec")], outs=[(x.shape[1], BF16)])
    return h


def _resid_rms(x, y, g2, g3, tr=256):
    def body(x_ref, y_ref, g2_ref, g3_ref, x2_ref, h2_ref):
        yv = y_ref[...]
        x2 = x_ref[...] + yv * _rms_stats(yv) * g2_ref[...]
        x2_ref[...] = x2
        h2_ref[...] = (x2 * _rms_stats(x2) * g3_ref[...]).astype(BF16)

    d = x.shape[1]
    return _row_call(body, name="resid_rms", rows=x.shape[0], tr=tr,
                     ins=[(x, "row"), (y, "row"), (g2, "vec"), (g3, "vec")], outs=[(d, F32), (d, BF16)])


def _final(x2, dn, g4, tgt, tr=256):
    d = x2.shape[1]

    def body(x2_ref, dn_ref, g4_ref, t_ref, dx3_ref, ddn_ref, dg4_ref, loss_ref):
        dn_v = dn_ref[...]
        r = _rms_stats(dn_v)
        g = g4_ref[...]
        e = x2_ref[...] + dn_v * r * g - t_ref[...]
        dx3 = e * (1.0 / d)
        dx3_ref[...] = dx3
        ddn_ref[...] = _rms_bwd(dn_v, r, g, dx3).astype(BF16)
        _accumulate(dg4_ref, jnp.sum(dx3 * dn_v * r, axis=0, keepdims=True))
        part = 0.5 * jnp.sum(jnp.mean(e * e, axis=-1, keepdims=True), axis=0, keepdims=True)
        _accumulate(loss_ref, jnp.broadcast_to(part, loss_ref.shape))

    return _row_call(body, name="final", rows=x2.shape[0], tr=tr,
                     ins=[(x2, "row"), (dn, "row"), (g4, "vec"), (tgt, "row")],
                     outs=[(d, F32), (d, BF16)], acc_outs=(d, 128))


def _rms_bwd2(dx3, dh2, x2, y, g2, g3, tr=256):
    d = x2.shape[1]

    def body(dx3_ref, dh2_ref, x2_ref, y_ref, g2_ref, g3_ref, dx2_ref, dy_ref, dg3_ref, dg2_ref):
        x2v, dh2v, yv = x2_ref[...], dh2_ref[...], y_ref[...]
        r3 = _rms_stats(x2v)
        dx2 = dx3_ref[...] + _rms_bwd(x2v, r3, g3_ref[...], dh2v)
        dx2_ref[...] = dx2
        _accumulate(dg3_ref, jnp.sum(dh2v * x2v * r3, axis=0, keepdims=True))
        r2 = _rms_stats(yv)
        dy_ref[...] = _rms_bwd(yv, r2, g2_ref[...], dx2).astype(BF16)
        _accumulate(dg2_ref, jnp.sum(dx2 * yv * r2, axis=0, keepdims=True))

    return _row_call(body, name="rms_bwd2", rows=x2.shape[0], tr=tr,
                     ins=[(dx3, "row"), (dh2, "row"), (x2, "row"), (y, "row"), (g2, "vec"), (g3, "vec")],
                     outs=[(d, F32), (d, BF16)], acc_outs=(d, d))


def _rms_bwd1(dx2, dh, x, g1, tr=256):
    d = x.shape[1]

    def body(dx2_ref, dh_ref, x_ref, g1_ref, gx_ref, dg1_ref):
        xv, dhv = x_ref[...], dh_ref[...]
        r = _rms_stats(xv)
        gx_ref[...] = dx2_ref[...] + _rms_bwd(xv, r, g1_ref[...], dhv)
        _accumulate(dg1_ref, jnp.sum(dhv * xv * r, axis=0, keepdims=True))

    return _row_call(body, name="rms_bwd1", rows=x.shape[0], tr=tr,
                     ins=[(dx2, "row"), (dh, "row"), (x, "row"), (g1, "vec")], outs=[(d, F32)], acc_outs=(d,))


def _merge_fwd(proj, o_sb, o_cv, off_sb, off_cv, gw, tr=256):
    s, d = o_sb.shape
    nj = d // gw
    b_sb, b_cv = off_sb // gw, off_cv // gw

    def body(gs_ref, gc_ref, osb_ref, ocv_ref, m_ref):
        m_ref[...] = (_sigmoid(gs_ref[...]) * osb_ref[...] + _sigmoid(gc_ref[...]) * ocv_ref[...]).astype(BF16)

    blk = lambda i, j: (i, j)
    return pl.pallas_call(
        body, name="merge_fwd", grid=(s // tr, nj),
        in_specs=[pl.BlockSpec((tr, gw), lambda i, j: (i, b_sb + j)), pl.BlockSpec((tr, gw), lambda i, j: (i, b_cv + j)),
                  pl.BlockSpec((tr, gw), blk), pl.BlockSpec((tr, gw), blk)],
        out_specs=pl.BlockSpec((tr, gw), blk), out_shape=jax.ShapeDtypeStruct((s, d), BF16),
        compiler_params=_params(dimension_semantics=("parallel", "parallel")),
    )(proj, proj, o_sb, o_cv)


def _merge_bwd(proj, o_sb, o_cv, dmerged, off_sb, off_cv, gw, tr=256):
    s, d = o_sb.shape
    nj = d // gw
    ni = s // tr
    b_sb, b_cv = off_sb // gw, off_cv // gw

    def body(gs_ref, gc_ref, osb_ref, ocv_ref, dm_ref, dosb_ref, docv_ref, dgs_ref, dgc_ref, sgs_ref, sgc_ref):
        dm = dm_ref[...]
        s_sb, s_cv = _sigmoid(gs_ref[...]), _sigmoid(gc_ref[...])
        dosb_ref[...] = (dm * s_sb).astype(BF16)
        docv_ref[...] = (dm * s_cv).astype(BF16)
        dgs = dm * osb_ref[...] * s_sb * (1.0 - s_sb)
        dgc = dm * ocv_ref[...] * s_cv * (1.0 - s_cv)
        dgs_ref[...] = dgs.astype(BF16)
        dgc_ref[...] = dgc.astype(BF16)
        i = pl.program_id(1)
        for ref, val in ((sgs_ref, dgs), (sgc_ref, dgc)):
            col = jnp.sum(val, axis=0, keepdims=True)

            @pl.when(i == 0)
            def _():
                ref[...] = col

            @pl.when(i > 0)
            def _():
                ref[...] += col

    blk = lambda j, i: (i, j)
    outs = pl.pallas_call(
        body, name="merge_bwd", grid=(nj, ni),
        in_specs=[pl.BlockSpec((tr, gw), lambda j, i: (i, b_sb + j)), pl.BlockSpec((tr, gw), lambda j, i: (i, b_cv + j)),
                  pl.BlockSpec((tr, gw), blk), pl.BlockSpec((tr, gw), blk), pl.BlockSpec((tr, gw), blk)],
        out_specs=[pl.BlockSpec((tr, gw), blk), pl.BlockSpec((tr, gw), blk),
                   pl.BlockSpec((tr, gw), blk), pl.BlockSpec((tr, gw), blk),
                   pl.BlockSpec((1, gw), lambda j, i: (0, j)), pl.BlockSpec((1, gw), lambda j, i: (0, j))],
        out_shape=[jax.ShapeDtypeStruct((s, d), BF16)] * 4 + [jax.ShapeDtypeStruct((1, d), F32)] * 2,
        compiler_params=_params(dimension_semantics=("parallel", "arbitrary")),
    )(proj, proj, o_sb, o_cv, dmerged)
    return outs


def _split_bf16(v):
    hi = v.astype(BF16)
    lo = (v - hi.astype(F32)).astype(BF16)
    return hi, lo


def _dot_nt(a, b):
    return lax.dot_general(a, b, (((1,), (1,)), ((), ())), preferred_element_type=F32)


def _dot_tn(a, b):
    return lax.dot_general(a, b, (((0,), (0,)), ((), ())), preferred_element_type=F32)


def _dot_nn(a, b):
    return lax.dot_general(a, b, (((1,), (0,)), ((), ())), preferred_element_type=F32)


def _sum_right(v, tri):
    hi, lo = _split_bf16(v)
    return _dot_nn(hi, tri) + _dot_nn(lo, tri)


def _sb_tile(q, kb, i, j, tq, scale):
    z = _dot_nt(q, kb) * scale
    row = lax.broadcasted_iota(jnp.int32, z.shape, 0) + i * tq
    col = lax.broadcasted_iota(jnp.int32, z.shape, 1) + j * tq
    mask = col < row
    sp = jnp.log1p(jnp.exp(-jnp.abs(z)))
    log_b = jnp.minimum(z, 0.0) - sp
    log_1m = jnp.where(mask, log_b - z, 0.0)
    return log_b, log_1m, mask


def _tri(tq, before):
    r = lax.broadcasted_iota(jnp.int32, (tq, tq), 0)
    c = lax.broadcasted_iota(jnp.int32, (tq, tq), 1)
    return jnp.where((r < c) if before else (r > c), 1.0, 0.0).astype(BF16)


def _attn_fwd(proj, n_heads, d_sb, tq=256):
    s = proj.shape[0]
    tq = _tile(s, tq)
    hb = d_sb // HEAD_DIM
    scale = 1.0 / math.sqrt(HEAD_DIM)

    def body(q_ref, k_ref, v_ref, o_ref):
        i = pl.program_id(1)
        q = q_ref[...].astype(BF16)
        tri = _tri(tq, False)

        def step(jj, carry):
            c_l, acc = carry
            j = i - jj
            rows = pl.ds(pl.multiple_of(j * tq, tq), tq)
            kb = k_ref[rows, :].astype(BF16)
            vb = v_ref[rows, :].astype(BF16)
            log_b, log_1m, mask = _sb_tile(q, kb, i, j, tq, scale)
            suffix = c_l + _sum_right(log_1m, tri)
            a = jnp.where(mask, jnp.exp(log_b + suffix), 0.0)
            acc = acc + _dot_nn(a.astype(BF16), vb)
            return c_l + jnp.sum(log_1m, axis=1, keepdims=True), acc

        _, acc = lax.fori_loop(0, i + 1, step, (jnp.zeros((tq, 1), F32), jnp.zeros((tq, HEAD_DIM), F32)))
        o_ref[...] = acc.astype(BF16)

    return pl.pallas_call(
        body, name="attn_fwd", grid=(n_heads, s // tq),
        in_specs=[pl.BlockSpec((tq, HEAD_DIM), lambda h, i: (i, h)),
                  pl.BlockSpec((s, HEAD_DIM), lambda h, i: (0, hb + h)),
                  pl.BlockSpec((s, HEAD_DIM), lambda h, i: (0, 2 * hb + h))],
        out_specs=pl.BlockSpec((tq, HEAD_DIM), lambda h, i: (i, h)),
        out_shape=jax.ShapeDtypeStruct((s, d_sb), BF16),
        compiler_params=_params(dimension_semantics=("parallel", "arbitrary")),
    )(proj, proj, proj)


def _attn_bwd(proj, dattn, n_heads, d_sb, tq=256):
    s = proj.shape[0]
    tq = _tile(s, tq)
    nq = s // tq
    hb = d_sb // HEAD_DIM
    scale = 1.0 / math.sqrt(HEAD_DIM)

    def body(q_ref, k_ref, v_ref, do_ref, dq_ref, dk_ref, dv_ref, sq_ref, sk_ref, sv_ref, dk_acc, dv_acc, g_st, b_st):
        i = pl.program_id(1)

        @pl.when(i == 0)
        def _():
            dk_acc[...] = jnp.zeros_like(dk_acc)
            dv_acc[...] = jnp.zeros_like(dv_acc)

        q = q_ref[...].astype(BF16)
        do = do_ref[...].astype(BF16)
        tri_after = _tri(tq, False)
        tri_before = _tri(tq, True)

        def newest_first(jj, c_l):
            j = i - jj
            rows = pl.ds(pl.multiple_of(j * tq, tq), tq)
            kb = k_ref[rows, :].astype(BF16)
            vb = v_ref[rows, :].astype(BF16)
            log_b, log_1m, mask = _sb_tile(q, kb, i, j, tq, scale)
            suffix = c_l + _sum_right(log_1m, tri_after)
            a = jnp.where(mask, jnp.exp(log_b + suffix), 0.0)
            g_st[j] = a * _dot_nt(do, vb)
            b_st[j] = jnp.where(mask, jnp.exp(log_b), 0.0)
            dv_acc[rows, :] += _dot_tn(a.astype(BF16), do)
            return c_l + jnp.sum(log_1m, axis=1, keepdims=True)

        lax.fori_loop(0, i + 1, newest_first, jnp.zeros((tq, 1), F32))

        def oldest_first(j, carry):
            c_g, dq = carry
            rows = pl.ds(pl.multiple_of(j * tq, tq), tq)
            kb = k_ref[rows, :].astype(BF16)
            g = g_st[j]
            beta = b_st[j]
            g_before = c_g + _sum_right(g, tri_before)
            dz16 = ((g * (1.0 - beta) - g_before * beta) * scale).astype(BF16)
            dk_acc[rows, :] += _dot_tn(dz16, q)
            return c_g + jnp.sum(g, axis=1, keepdims=True), dq + _dot_nn(dz16, kb)

        _, dq = lax.fori_loop(0, i + 1, oldest_first, (jnp.zeros((tq, 1), F32), jnp.zeros((tq, HEAD_DIM), F32)))
        dq_ref[...] = dq.astype(BF16)
        col = jnp.sum(dq, axis=0, keepdims=True)

        @pl.when(i == 0)
        def _():
            sq_ref[...] = col

        @pl.when(i > 0)
        def _():
            sq_ref[...] += col

        @pl.when(i == nq - 1)
        def _():
            dk = dk_acc[...]
            dv = dv_acc[...]
            dk_ref[...] = dk.astype(BF16)
            dv_ref[...] = dv.astype(BF16)
            sk_ref[...] = jnp.sum(dk, axis=0, keepdims=True)
            sv_ref[...] = jnp.sum(dv, axis=0, keepdims=True)

    qblk = pl.BlockSpec((tq, HEAD_DIM), lambda h, i: (i, h))
    kblk = pl.BlockSpec((s, HEAD_DIM), lambda h, i: (0, h))
    col_sum = pl.BlockSpec((1, HEAD_DIM), lambda h, i: (0, h))
    outs = pl.pallas_call(
        body, name="attn_bwd", grid=(n_heads, nq),
        in_specs=[qblk,
                  pl.BlockSpec((s, HEAD_DIM), lambda h, i: (0, hb + h)),
                  pl.BlockSpec((s, HEAD_DIM), lambda h, i: (0, 2 * hb + h)),
                  qblk],
        out_specs=[qblk, kblk, kblk, col_sum, col_sum, col_sum],
        out_shape=[jax.ShapeDtypeStruct((s, d_sb), BF16)] * 3 + [jax.ShapeDtypeStruct((1, d_sb), F32)] * 3,
        scratch_shapes=[pltpu.VMEM((s, HEAD_DIM), F32), pltpu.VMEM((s, HEAD_DIM), F32),
                        pltpu.VMEM((nq, tq, tq), F32), pltpu.VMEM((nq, tq, tq), F32)],
        compiler_params=_params(dimension_semantics=("parallel", "arbitrary")),
    )(proj, proj, proj, dattn)
    return outs


LANES = 128


def _glu(proj, off_a, dc, tr=256):
    s = proj.shape[0]
    ba = off_a // dc

    def body(a_ref, b_ref, u_ref):
        u_ref[...] = a_ref[...] * _sigmoid(b_ref[...])

    return pl.pallas_call(
        body, name="glu", grid=(s // tr,),
        in_specs=[pl.BlockSpec((tr, dc), lambda i: (i, ba)), pl.BlockSpec((tr, dc), lambda i: (i, ba + 1))],
        out_specs=pl.BlockSpec((tr, dc), lambda i: (i, 0)), out_shape=jax.ShapeDtypeStruct((s, dc), F32),
        compiler_params=_params(dimension_semantics=("parallel",)),
    )(proj, proj)


def _conv_fwd(u0, taps, b_dw, g_ln, b_ln, tt=256):
    s, dc = u0.shape
    tt = _tile(s, tt)
    hpb = tt // CONV_HALO

    def body(cur_ref, halo_ref, w_ref, bdw_ref, g_ref, b_ref, u1_ref, u3_ref, xs):
        i = pl.program_id(0)
        xs[pl.ds(0, CONV_HALO), :] = jnp.where(i > 0, halo_ref[...], 0.0)
        xs[pl.ds(CONV_HALO, tt), :] = cur_ref[...]
        for c0 in range(0, dc, LANES):
            cols = pl.ds(c0, LANES)
            acc = jnp.broadcast_to(bdw_ref[:, cols], (tt, LANES))
            for k in range(CONV_WIDTH):
                acc = acc + w_ref[pl.ds(k, 1), cols] * xs[pl.ds(CONV_HALO - (CONV_WIDTH - 1) + k, tt), cols]
            u1_ref[:, cols] = acc
        u1 = u1_ref[...]
        mu = jnp.mean(u1, axis=-1, keepdims=True)
        xc = u1 - mu
        rstd = lax.rsqrt(jnp.mean(xc * xc, axis=-1, keepdims=True) + EPS)
        u2 = xc * rstd * g_ref[...] + b_ref[...]
        u3_ref[...] = (u2 * _sigmoid(u2)).astype(BF16)

    vec = pl.BlockSpec((1, dc), lambda i: (0, 0))
    return pl.pallas_call(
        body, name="conv_fwd", grid=(s // tt,),
        in_specs=[pl.BlockSpec((tt, dc), lambda i: (i, 0)),
                  pl.BlockSpec((CONV_HALO, dc), lambda i: (jnp.maximum(i * hpb - 1, 0), 0)),
                  pl.BlockSpec((CONV_HALO, dc), lambda i: (0, 0)), vec, vec, vec],
        out_specs=[pl.BlockSpec((tt, dc), lambda i: (i, 0)), pl.BlockSpec((tt, dc), lambda i: (i, 0))],
        out_shape=[jax.ShapeDtypeStruct((s, dc), F32), jax.ShapeDtypeStruct((s, dc), BF16)],
        scratch_shapes=[pltpu.VMEM((tt + CONV_HALO, dc), F32)],
        compiler_params=_params(dimension_semantics=("arbitrary",)),
    )(u0, u0, taps, b_dw, g_ln, b_ln)


def _conv_bwd_norm(u1, du3, g_ln, b_ln, tr=256):
    dc = u1.shape[1]

    def body(u1_ref, du3_ref, g_ref, b_ref, du1_ref, dg_ref, db_ref, dbdw_ref):
        u1v = u1_ref[...]
        g = g_ref[...]
        mu = jnp.mean(u1v, axis=-1, keepdims=True)
        xc = u1v - mu
        rstd = lax.rsqrt(jnp.mean(xc * xc, axis=-1, keepdims=True) + EPS)
        xhat = xc * rstd
        u2 = xhat * g + b_ref[...]
        sg = _sigmoid(u2)
        du2 = du3_ref[...] * (sg * (1.0 + u2 * (1.0 - sg)))
        _accumulate(dg_ref, jnp.sum(du2 * xhat, axis=0, keepdims=True))
        _accumulate(db_ref, jnp.sum(du2, axis=0, keepdims=True))
        gy = du2 * g
        du1 = rstd * (gy - jnp.mean(gy, axis=-1, keepdims=True) - xhat * jnp.mean(gy * xhat, axis=-1, keepdims=True))
        du1_ref[...] = du1
        _accumulate(dbdw_ref, jnp.sum(du1, axis=0, keepdims=True))

    return _row_call(body, name="conv_bwd_norm", rows=u1.shape[0], tr=tr,
                     ins=[(u1, "row"), (du3, "row"), (g_ln, "vec"), (b_ln, "vec")],
                     outs=[(dc, F32)], acc_outs=(dc, dc, dc))


def _conv_bwd_taps(du1, u0, proj, taps, off_a, tt=256):
    s, dc = u0.shape
    tt = _tile(s, tt)
    n = s // tt
    hpb = tt // CONV_HALO
    ba = off_a // dc
    lead = CONV_HALO - (CONV_WIDTH - 1)

    def body(d_ref, dnext_ref, cur_ref, halo_ref, a_ref, b_ref, w_ref, da_ref, db_ref, sa_ref, sb_ref, dw_ref,
             xs, ds, du0):
        i = pl.program_id(0)
        xs[pl.ds(0, CONV_HALO), :] = jnp.where(i > 0, halo_ref[...], 0.0)
        xs[pl.ds(CONV_HALO, tt), :] = cur_ref[...]
        ds[pl.ds(0, tt), :] = d_ref[...]
        ds[pl.ds(tt, CONV_HALO), :] = jnp.where(i < n - 1, dnext_ref[...], 0.0)

        @pl.when(i == 0)
        def _():
            dw_ref[...] = jnp.zeros_like(dw_ref)

        for c0 in range(0, dc, LANES):
            cols = pl.ds(c0, LANES)
            d_cur = ds[pl.ds(0, tt), cols]
            acc = jnp.zeros((tt, LANES), F32)
            for k in range(CONV_WIDTH):
                acc = acc + w_ref[pl.ds(k, 1), cols] * ds[pl.ds(CONV_WIDTH - 1 - k, tt), cols]
                dw_ref[pl.ds(k, 1), cols] += jnp.sum(d_cur * xs[pl.ds(lead + k, tt), cols], axis=0, keepdims=True)
            du0[:, cols] = acc
        d0 = du0[...]
        sg = _sigmoid(b_ref[...])
        da = d0 * sg
        db = d0 * a_ref[...] * sg * (1.0 - sg)
        da_ref[...] = da.astype(BF16)
        db_ref[...] = db.astype(BF16)
        _accumulate(sa_ref, jnp.sum(da, axis=0, keepdims=True))
        _accumulate(sb_ref, jnp.sum(db, axis=0, keepdims=True))

    row = pl.BlockSpec((tt, dc), lambda i: (i, 0))
    vec = pl.BlockSpec((1, dc), lambda i: (0, 0))
    taps_spec = pl.BlockSpec((CONV_HALO, dc), lambda i: (0, 0))
    return pl.pallas_call(
        body, name="conv_bwd_taps", grid=(n,),
        in_specs=[row, pl.BlockSpec((CONV_HALO, dc), lambda i: (jnp.minimum((i + 1) * hpb, n * hpb - 1), 0)),
                  row, pl.BlockSpec((CONV_HALO, dc), lambda i: (jnp.maximum(i * hpb - 1, 0), 0)),
                  pl.BlockSpec((tt, dc), lambda i: (i, ba)), pl.BlockSpec((tt, dc), lambda i: (i, ba + 1)), taps_spec],
        out_specs=[row, row, vec, vec, taps_spec],
        out_shape=[jax.ShapeDtypeStruct((s, dc), BF16), jax.ShapeDtypeStruct((s, dc), BF16),
                   jax.ShapeDtypeStruct((1, dc), F32), jax.ShapeDtypeStruct((1, dc), F32),
                   jax.ShapeDtypeStruct((CONV_HALO, dc), F32)],
        scratch_shapes=[pltpu.VMEM((tt + CONV_HALO, dc), F32), pltpu.VMEM((tt + CONV_HALO, dc), F32),
                        pltpu.VMEM((tt, dc), F32)],
        compiler_params=_params(dimension_semantics=("arbitrary",)),
    )(du1, du1, u0, u0, proj, proj, taps)


def _elementwise(body, *, name, ins, out_dtypes, tr=128):
    rows, cols = ins[0].shape
    tr = _tile(rows, tr, 8)
    spec = pl.BlockSpec((tr, cols), lambda i: (i, 0))
    return pl.pallas_call(
        body, name=name, grid=(rows // tr,), in_specs=[spec] * len(ins), out_specs=[spec] * len(out_dtypes),
        out_shape=[jax.ShapeDtypeStruct((rows, cols), dt) for dt in out_dtypes],
        compiler_params=_params(dimension_semantics=("parallel",)),
    )(*ins)


def _prefetched(body, *, name, where, grid, in_specs, out_specs, out_shape, operands):
    return pl.pallas_call(
        body, name=name, out_shape=out_shape,
        grid_spec=pltpu.PrefetchScalarGridSpec(num_scalar_prefetch=1, grid=grid, in_specs=in_specs, out_specs=out_specs),
        compiler_params=_params(dimension_semantics=("parallel",) * len(grid)),
    )(where, *operands)


def _cast_into_block(w, where, name, tr=256):
    rows, cols = w.shape
    tr = _tile(rows, tr, 16)

    def body(where_ref, w_ref, o_ref):
        o_ref[...] = w_ref[...].astype(BF16)

    return _prefetched(
        body, name=name, where=where, grid=(rows // tr,),
        in_specs=[pl.BlockSpec((tr, cols), lambda i, wh: (i, 0))],
        out_specs=pl.BlockSpec((None, tr, cols), lambda i, wh: (wh[0], i, 0)),
        out_shape=jax.ShapeDtypeStruct((N_CHIPS, rows, cols), BF16), operands=[w])


def _add_pair(grad, theirs, where, name, tr=256):
    nb, half, cols = theirs.shape
    tr = _tile(half, tr, 16)
    nh = half // tr

    def body(where_ref, a_ref, b_ref, o_ref):
        o_ref[...] = (a_ref[...].astype(F32) + b_ref[...].astype(F32)).astype(BF16)

    blk = pl.BlockSpec((None, tr, cols), lambda b, i, wh: (b, i, 0))
    return _prefetched(
        body, name=name, where=where, grid=(nb, nh),
        in_specs=[pl.BlockSpec((None, tr, cols), lambda b, i, wh: (b, wh[1] * nh + i, 0)), blk],
        out_specs=blk, out_shape=jax.ShapeDtypeStruct(theirs.shape, BF16), operands=[grad, theirs])


def _sum_slots(part, slots, where, name, tr=256):
    _, half, cols = part.shape
    tr = _tile(half, tr, 16)
    nh = half // tr

    def body(where_ref, p_ref, s_ref, o_ref):
        o_ref[...] = ((p_ref[...].astype(F32) + s_ref[0].astype(F32)) + s_ref[1].astype(F32)) + s_ref[2].astype(F32)

    return _prefetched(
        body, name=name, where=where, grid=(nh,),
        in_specs=[pl.BlockSpec((None, tr, cols), lambda i, wh: (wh[0], i, 0)),
                  pl.BlockSpec((3, tr, cols), lambda i, wh: (0, i, 0))],
        out_specs=pl.BlockSpec((tr, cols), lambda i, wh: (wh[1] * nh + i, 0)),
        out_shape=jax.ShapeDtypeStruct((2 * half, cols), F32), operands=[part, slots])


def _adamw(w, g, m, v, name):
    bc1 = 1.0 - ADAM_B1 ** ADAM_STEP
    bc2 = 1.0 - ADAM_B2 ** ADAM_STEP

    def body(w_ref, g_ref, m_ref, v_ref, go_ref, d_ref, mo_ref, vo_ref):
        g = g_ref[...]
        m_new = ADAM_B1 * m_ref[...] + (1.0 - ADAM_B1) * g
        v_new = ADAM_B2 * v_ref[...] + (1.0 - ADAM_B2) * (g * g)
        m_hat = m_new / bc1
        v_hat = v_new / bc2
        go_ref[...] = g
        d_ref[...] = -ADAM_LR * (m_hat / (jnp.sqrt(v_hat) + ADAM_EPS) + ADAM_WD * w_ref[...])
        mo_ref[...] = m_new
        vo_ref[...] = v_new

    return _elementwise(body, name=name, ins=[w, g, m, v], out_dtypes=[F32] * 4, tr=128)


ANY = pl.BlockSpec(memory_space=pl.ANY)


def _mesh_pos():
    return lax.axis_index("x"), lax.axis_index("y"), lax.axis_index("c")


def _other_chips(x, y):
    return [(1 - x, y), (x, 1 - y), (1 - x, 1 - y)]


def _remote(src, dst, send_sems, recv_sems, idx, to):
    return pltpu.make_async_remote_copy(src_ref=src, dst_ref=dst, send_sem=send_sems.at[idx], recv_sem=recv_sems.at[idx],
                                        device_id=to, device_id_type=MESH)


HBM = pl.BlockSpec(memory_space=pltpu.HBM)
SEM = pl.BlockSpec(memory_space=pltpu.SEMAPHORE)
EFFECT = pltpu.SideEffectType.DATAFLOW_SIDE_EFFECTING


def _in_hbm(arrays):
    return [pltpu.with_memory_space_constraint(a, pltpu.HBM) for a in arrays]


def _ici_copies(slab, send_sems, recv_sems, x, y, c):
    half = slab.shape[1] // 2
    rows = pl.ds(c * half, half)
    mine = slab.at[2 * x + y, rows]
    out = []
    for j, (cx, cy) in enumerate(_other_chips(x, y)):
        got = slab.at[2 * cx + cy, rows]
        out.append((_remote(mine, mine, send_sems, recv_sems, j, (cx, cy, c)),
                    _remote(got, got, send_sems, recv_sems, j, (cx, cy, c))))
    return out


def _gather_start(slabs):
    n = len(slabs)

    def body(*refs):
        ins = refs[:n]
        send_sems, recv_sems = refs[2 * n:3 * n], refs[3 * n:4 * n]
        x, y, c = _mesh_pos()
        for w in range(n):
            for send, _ in _ici_copies(ins[w], send_sems[w], recv_sems[w], x, y, c):
                send.start()

    outs = pl.pallas_call(
        body, name="gather_start", in_specs=[HBM] * n, out_specs=[HBM] * n + [SEM] * (2 * n),
        out_shape=[pltpu.HBM(s.shape, s.dtype) for s in slabs] + [pltpu.SemaphoreType.DMA((3,))] * (2 * n),
        input_output_aliases={w: w for w in range(n)},
        compiler_params=_params(has_side_effects=EFFECT),
    )(*_in_hbm(slabs))
    return outs[:n], outs[n:2 * n], outs[2 * n:]


def _gather_wait(slabs, send_sems, recv_sems, after, name):
    n = len(slabs)

    def body(*refs):
        ins = refs[:n]
        sends, recvs = refs[n:2 * n], refs[2 * n:3 * n]
        x, y, c = _mesh_pos()
        for w in range(n):
            for send, recv in _ici_copies(ins[w], sends[w], recvs[w], x, y, c):
                send.wait_send()
                recv.wait_recv()

    return pl.pallas_call(
        body, name=name, in_specs=[HBM] * n + [SEM] * (2 * n) + [ANY], out_specs=[HBM] * n,
        out_shape=[pltpu.HBM(s.shape, s.dtype) for s in slabs],
        input_output_aliases={w: w for w in range(n)},
        compiler_params=_params(has_side_effects=EFFECT),
    )(*slabs, *send_sems, *recv_sems, after)


def _gather_forward(slabs, name):
    n = len(slabs)

    def body(*refs):
        outs = refs[n:2 * n]
        send_sems, recv_sems = refs[2 * n:]
        x, y, c = _mesh_pos()
        sibling = (x, y, 1 - c)
        chips = _other_chips(x, y)
        started = []
        for w in range(n):
            half = outs[w].shape[1] // 2
            for j, (cx, cy) in enumerate(chips):
                got = outs[w].at[2 * cx + cy, pl.ds(c * half, half)]
                fw = _remote(got, got, send_sems, recv_sems, 3 * w + j, sibling)
                fw.start()
                started.append(fw)
        for w in range(n):
            half = outs[w].shape[1] // 2
            for j, (cx, cy) in enumerate(chips):
                got = outs[w].at[2 * cx + cy, pl.ds((1 - c) * half, half)]
                _remote(got, got, send_sems, recv_sems, 3 * w + j, sibling).wait_recv()
        for cp in started:
            cp.wait_send()

    return pl.pallas_call(
        body, name=name, in_specs=[ANY] * n, out_specs=[ANY] * n,
        out_shape=[jax.ShapeDtypeStruct(s.shape, s.dtype) for s in slabs],
        input_output_aliases={w: w for w in range(n)},
        scratch_shapes=[pltpu.SemaphoreType.DMA((3 * n,)), pltpu.SemaphoreType.DMA((3 * n,))],
        compiler_params=_params(has_side_effects=True),
    )(*slabs)


def _gather_taps(taps):
    def body(in_ref, out_ref, send_sems, recv_sems, local_sem):
        x, y, c = _mesh_pos()
        me = 2 * x + y
        chips = _other_chips(x, y)
        lc = pltpu.make_async_copy(in_ref, out_ref.at[me], local_sem)
        lc.start()
        sends = [_remote(in_ref, out_ref.at[me], send_sems, recv_sems, j, (cx, cy, c)) for j, (cx, cy) in enumerate(chips)]
        for cp in sends:
            cp.start()
        for j, (cx, cy) in enumerate(chips):
            got = out_ref.at[2 * cx + cy]
            _remote(got, got, send_sems, recv_sems, j, (cx, cy, c)).wait_recv()
        for cp in sends:
            cp.wait_send()
        lc.wait()

    vm = pl.BlockSpec(memory_space=pltpu.VMEM)
    return pl.pallas_call(
        body, name="gather_taps", in_specs=[vm], out_specs=vm,
        out_shape=jax.ShapeDtypeStruct((N_CHIPS,) + taps.shape, taps.dtype),
        scratch_shapes=[pltpu.SemaphoreType.DMA((3,)), pltpu.SemaphoreType.DMA((3,)), pltpu.SemaphoreType.DMA],
        compiler_params=_params(has_side_effects=True),
    )(taps)


def _exchange_halves(grads, name):
    n = len(grads)

    def body(*refs):
        ins, their_refs = refs[:n], refs[n:2 * n]
        send_sems, recv_sems = refs[2 * n:]
        x, y, c = _mesh_pos()
        sibling = (x, y, 1 - c)
        sends = []
        for w in range(n):
            half = ins[w].shape[1] // 2
            cp = _remote(ins[w].at[:, pl.ds((1 - c) * half, half), :], their_refs[w], send_sems, recv_sems, w, sibling)
            cp.start()
            sends.append(cp)
        for w in range(n):
            _remote(their_refs[w], their_refs[w], send_sems, recv_sems, w, sibling).wait_recv()
        for cp in sends:
            cp.wait_send()

    halves = [jax.ShapeDtypeStruct((g.shape[0], g.shape[1] // 2, g.shape[2]), g.dtype) for g in grads]
    return pl.pallas_call(
        body, name=name, in_specs=[ANY] * n, out_specs=[ANY] * n, out_shape=halves,
        scratch_shapes=[pltpu.SemaphoreType.DMA((n,)), pltpu.SemaphoreType.DMA((n,))],
        compiler_params=_params(has_side_effects=True),
    )(*grads)


def _scatter_copies(part, slots, send_sems, recv_sems, x, y, c):
    out = []
    for j, (cx, cy) in enumerate(_other_chips(x, y)):
        out.append((_remote(part.at[2 * cx + cy], slots.at[j], send_sems, recv_sems, j, (cx, cy, c)),
                    _remote(slots.at[j], slots.at[j], send_sems, recv_sems, j, (cx, cy, c))))
    return out


def _scatter_start(parts, name):
    n = len(parts)
    slots = [lax.empty((3,) + p.shape[1:], p.dtype) for p in parts]

    def body(*refs):
        ins, lands = refs[:n], refs[n:2 * n]
        send_sems, recv_sems = refs[4 * n:5 * n], refs[5 * n:6 * n]
        x, y, c = _mesh_pos()
        for w in range(n):
            for send, _ in _scatter_copies(ins[w], lands[w], send_sems[w], recv_sems[w], x, y, c):
                send.start()

    outs = pl.pallas_call(
        body, name=name, in_specs=[HBM] * (2 * n), out_specs=[HBM] * (2 * n) + [SEM] * (2 * n),
        out_shape=[pltpu.HBM(a.shape, a.dtype) for a in list(parts) + slots] + [pltpu.SemaphoreType.DMA((3,))] * (2 * n),
        input_output_aliases={k: k for k in range(2 * n)},
        compiler_params=_params(has_side_effects=EFFECT),
    )(*_in_hbm(list(parts) + slots))
    return outs[:n], outs[n:2 * n], outs[2 * n:3 * n], outs[3 * n:]


def _scatter_wait(parts, slots, send_sems, recv_sems, after, name):
    n = len(parts)

    def body(*refs):
        ins, lands = refs[:n], refs[n:2 * n]
        sends, recvs = refs[2 * n:3 * n], refs[3 * n:4 * n]
        x, y, c = _mesh_pos()
        for w in range(n):
            for send, recv in _scatter_copies(ins[w], lands[w], sends[w], recvs[w], x, y, c):
                send.wait_send()
                recv.wait_recv()

    outs = pl.pallas_call(
        body, name=name, in_specs=[HBM] * (2 * n) + [SEM] * (2 * n) + [ANY], out_specs=[HBM] * (2 * n),
        out_shape=[pltpu.HBM(a.shape, a.dtype) for a in list(parts) + list(slots)],
        input_output_aliases={k: k for k in range(2 * n)},
        compiler_params=_params(has_side_effects=EFFECT),
    )(*parts, *slots, *send_sems, *recv_sems, after)
    return outs[:n], outs[n:]


def _share_halves(halves, name):
    n = len(halves)

    def body(*refs):
        outs = refs[n:2 * n]
        send_sems, recv_sems = refs[2 * n:]
        x, y, c = _mesh_pos()
        sibling = (x, y, 1 - c)
        sends = []
        for w in range(n):
            h = outs[w].shape[0] // 2
            mine = outs[w].at[pl.ds(c * h, h)]
            cp = _remote(mine, mine, send_sems, recv_sems, w, sibling)
            cp.start()
            sends.append(cp)
        for w in range(n):
            h = outs[w].shape[0] // 2
            got = outs[w].at[pl.ds((1 - c) * h, h)]
            _remote(got, got, send_sems, recv_sems, w, sibling).wait_recv()
        for cp in sends:
            cp.wait_send()

    return pl.pallas_call(
        body, name=name, in_specs=[ANY] * n, out_specs=[ANY] * n,
        out_shape=[jax.ShapeDtypeStruct(h.shape, h.dtype) for h in halves],
        input_output_aliases={w: w for w in range(n)},
        scratch_shapes=[pltpu.SemaphoreType.DMA((n,)), pltpu.SemaphoreType.DMA((n,))],
        compiler_params=_params(has_side_effects=True),
    )(*halves)


def _all_reduce_small(buf):
    rows, width = buf.shape

    def body(in_ref, out_ref, gathered, send_sems, recv_sems):
        x, y, c = _mesh_pos()
        me = 4 * x + 2 * y + c
        gathered[me] = in_ref[...]
        flips = [(dx, dy, dz) for dx in (0, 1) for dy in (0, 1) for dz in (0, 1)][1:]
        peers = [((1 - x) if dx else x, (1 - y) if dy else y, (1 - c) if dz else c) for dx, dy, dz in flips]
        sends = [_remote(in_ref, gathered.at[me], send_sems, recv_sems, k, peer) for k, peer in enumerate(peers)]
        for cp in sends:
            cp.start()
        for k, (px, py, pc) in enumerate(peers):
            got = gathered.at[4 * px + 2 * py + pc]
            _remote(got, got, send_sems, recv_sems, k, (px, py, pc)).wait_recv()
        for cp in sends:
            cp.wait_send()
        total = gathered[0]
        for d in range(1, 8):
            total = total + gathered[d]
        out_ref[...] = total

    vm = pl.BlockSpec(memory_space=pltpu.VMEM)
    return pl.pallas_call(
        body, name="all_reduce_small", in_specs=[vm], out_specs=vm, out_shape=jax.ShapeDtypeStruct(buf.shape, F32),
        scratch_shapes=[pltpu.VMEM((8, rows, width), F32), pltpu.SemaphoreType.DMA((7,)), pltpu.SemaphoreType.DMA((7,))],
        compiler_params=_params(has_side_effects=True),
    )(buf)


PACK_W = 2048
PACK_ROWS = 8


def _pack(vectors):
    flat = jnp.concatenate([v.reshape(-1) for v in vectors])
    unit = PACK_W * PACK_ROWS
    total = -(-flat.shape[0] // unit) * unit
    return jnp.pad(flat, (0, total - flat.shape[0])).reshape(total // PACK_W, PACK_W)


def _unpack(buf, shapes):
    flat = buf.reshape(-1)
    out, pos = [], 0
    for shp in shapes:
        size = math.prod(shp)
        out.append(flat[pos:pos + size].reshape(shp))
        pos += size
    return out


def _pad_rows(a, rows):
    return jnp.pad(a, ((0, rows - a.shape[0]), (0, 0)))


def kernel(x, g_pre_mix, w_in, b_in, w_dw, b_dw, g_conv_ln, b_conv_ln, w_sb_out, w_conv_out, w_o, g_post_mix, g_pre_mlp, w_up, w_down, g_post_mlp, loss_target, m_g_pre_mix, m_w_in, m_b_in, m_w_dw, m_b_dw, m_g_conv_ln, m_b_conv_ln, m_w_sb_out, m_w_conv_out, m_w_o, m_g_post_mix, m_g_pre_mlp, m_w_up, m_w_down, m_g_post_mlp, v_g_pre_mix, v_w_in, v_b_in, v_w_dw, v_b_dw, v_g_conv_ln, v_b_conv_ln, v_w_sb_out, v_w_conv_out, v_w_o, v_g_post_mix, v_g_pre_mlp, v_w_up, v_w_down, v_g_post_mlp):
    xs = x[0]
    tgt = loss_target[0]
    s, d = xs.shape
    d_sb = w_sb_out.shape[1]
    dc = w_conv_out.shape[1]
    d_in = w_in.shape[2] * N_CHIPS
    n_heads = d_sb // HEAD_DIM
    off_a = 3 * d_sb
    off_sb = off_a + 2 * dc
    off_cv = off_sb + d
    gw = dc
    assert off_a % dc == 0 and off_sb % gw == 0 and d % gw == 0 and d_in == off_cv + d
    chip = 2 * lax.axis_index("x") + lax.axis_index("y")

    big = dict(w_in=w_in[0], w_sb_out=w_sb_out[0], w_conv_out=w_conv_out[0], w_o=w_o[0], w_up=w_up[0], w_down=w_down[0])
    names = list(big)
    col_sharded = {"w_in", "w_sb_out", "w_conv_out", "w_up"}
    where = jnp.stack([chip, lax.axis_index("c")]).astype(jnp.int32)
    slabs = [_cast_into_block(big[n], where, "cast_" + n) for n in names]
    in_flight = dict(zip(names, zip(*_gather_start(slabs))))
    gathered = {}

    def arrive(group, after, tag):
        thru, sends, recvs = zip(*[in_flight[n] for n in group])
        landed = _gather_wait(thru, sends, recvs, after, "gather_wait_" + tag)
        for n, g4 in zip(group, _gather_forward(landed, "gather_forward_" + tag)):
            gathered[n] = g4 if n in col_sharded else g4.reshape(g4.shape[0] * g4.shape[1], g4.shape[2])

    cs = w_dw.shape[2]
    taps4 = _gather_taps(_pad_rows(w_dw[0], CONV_HALO))
    taps = jnp.transpose(taps4, (1, 0, 2)).reshape(CONV_HALO, N_CHIPS * cs)

    h = _rms_fwd(xs, g_pre_mix)
    arrive(["w_in"], h, "in")
    proj = _mm(h, gathered["w_in"], name="mm_proj", b_groups=N_CHIPS, bias=b_in, tn=768)
    arrive(["w_sb_out", "w_conv_out", "w_o"], proj, "mix")
    attn = _attn_fwd(proj, n_heads, d_sb)
    u0 = _glu(proj, off_a, dc)
    u1, u3 = _conv_fwd(u0, taps, b_dw, g_conv_ln, b_conv_ln)
    arrive(["w_up"], u3, "up")
    o_sb = _mm(attn, gathered["w_sb_out"], name="mm_o_sb", b_groups=N_CHIPS)
    o_cv = _mm(u3, gathered["w_conv_out"], name="mm_o_cv", b_groups=N_CHIPS)
    merged = _merge_fwd(proj, o_sb, o_cv, off_sb, off_cv, gw)
    y = _mm(merged, gathered["w_o"], name="mm_y")
    x2, h2 = _resid_rms(xs, y, g_post_mix, g_pre_mlp)
    up, f = _mm(h2, gathered["w_up"], name="mm_up", b_groups=N_CHIPS, out_dtypes=(F32, BF16),
                epilogue=lambda acc: (acc, jnp.square(jnp.maximum(acc, 0.0))))
    arrive(["w_down"], f, "down")
    dn = _mm(f, gathered["w_down"], name="mm_down")
    dx3, d_dn, dg_post_mlp, loss_part = _final(x2, dn, g_post_mlp, tgt)

    moments = dict(w_in=(m_w_in, v_w_in), w_sb_out=(m_w_sb_out, v_w_sb_out), w_conv_out=(m_w_conv_out, v_w_conv_out),
                   w_o=(m_w_o, v_w_o), w_up=(m_w_up, v_w_up), w_down=(m_w_down, v_w_down))
    out_g, out_d, out_m, out_v = {}, {}, {}, {}
    grads = {}

    def reduce_start(group, tag):
        blocks = []
        for n in group:
            g = grads[n]
            blocks.append(g if g.ndim == 3 else g.reshape(N_CHIPS, g.shape[0] // N_CHIPS, g.shape[1]))
        theirs = _exchange_halves(blocks, "exchange_halves_" + tag)
        pair_sums = [_add_pair(g, t, where, "pair_sum_" + n) for n, g, t in zip(group, blocks, theirs)]
        return _scatter_start(pair_sums, "scatter_start_" + tag)

    def reduce_finish(group, tag, started, after):
        parts, slots = _scatter_wait(*started, after, "scatter_wait_" + tag)
        halves = [_sum_slots(p, t, where, "chip_sum_" + n) for n, p, t in zip(group, parts, slots)]
        for n, red in zip(group, _share_halves(halves, "share_halves_" + tag)):
            mm_, vv_ = moments[n]
            res = _adamw(big[n], red, mm_[0], vv_[0], "adamw_" + n)
            out_g[n], out_d[n], out_m[n], out_v[n] = [r[None] for r in res]

    group_mlp, group_mix, group_in = ["w_down", "w_up"], ["w_o", "w_sb_out", "w_conv_out"], ["w_in"]
    grads["w_down"] = _mm(f, d_dn, name="mm_dw_down", ta=True, out_dtypes=(BF16,))
    dup = _mm(d_dn, gathered["w_down"], name="mm_df", tb=True, extra=[up], out_dtypes=(BF16,),
              epilogue=lambda acc, upv: (acc * (2.0 * jnp.maximum(upv, 0.0)),))
    grads["w_up"] = _mm(h2, dup, name="mm_dw_up", ta=True, out_groups=N_CHIPS, out_dtypes=(BF16,))
    mlp_started = reduce_start(group_mlp, "mlp")
    dh2 = _mm(dup, gathered["w_up"], name="mm_dh2", tb=True, b_groups=N_CHIPS)
    dx2, dy, dg_pre_mlp, dg_post_mix = _rms_bwd2(dx3, dh2, x2, y, g_post_mix, g_pre_mlp)
    grads["w_o"] = _mm(merged, dy, name="mm_dw_o", ta=True, out_dtypes=(BF16,))
    dmerged = _mm(dy, gathered["w_o"], name="mm_dmerged", tb=True)
    do_sb, do_cv, dgate_sb, dgate_cv, s_gate_sb, s_gate_cv = _merge_bwd(proj, o_sb, o_cv, dmerged, off_sb, off_cv, gw)
    grads["w_sb_out"] = _mm(attn, do_sb, name="mm_dw_sb", ta=True, out_groups=N_CHIPS, out_dtypes=(BF16,))
    dattn = _mm(do_sb, gathered["w_sb_out"], name="mm_dattn", tb=True, b_groups=N_CHIPS)
    grads["w_conv_out"] = _mm(u3, do_cv, name="mm_dw_cv", ta=True, out_groups=N_CHIPS, out_dtypes=(BF16,))
    du3 = _mm(do_cv, gathered["w_conv_out"], name="mm_du3", tb=True, b_groups=N_CHIPS)
    reduce_finish(group_mlp, "mlp", mlp_started, du3)
    mix_started = reduce_start(group_mix, "mix")
    dq, dk, dv, s_q, s_k, s_v = _attn_bwd(proj, dattn, n_heads, d_sb)
    reduce_finish(group_mix, "mix", mix_started, dq)
    du1, dg_conv_ln, db_conv_ln, db_dw = _conv_bwd_norm(u1, du3, g_conv_ln, b_conv_ln)
    dglu_a, dglu_b, s_a, s_b, dtaps = _conv_bwd_taps(du1, u0, proj, taps, off_a)
    dproj = jnp.concatenate([dq, dk, dv, dglu_a, dglu_b, dgate_sb, dgate_cv], axis=1)
    db_in = jnp.concatenate([s_q, s_k, s_v, s_a, s_b, s_gate_sb, s_gate_cv], axis=1)
    grads["w_in"] = _mm(h, dproj, name="mm_dw_in", ta=True, out_groups=N_CHIPS, out_dtypes=(BF16,), tn=768)
    in_started = reduce_start(group_in, "in")
    dh = _mm(dproj, gathered["w_in"], name="mm_dh", tb=True, b_groups=N_CHIPS, tk=768)
    grad_x, dg_pre_mix = _rms_bwd1(dx2, dh, xs, g_pre_mix)

    small = ["g_pre_mix", "b_in", "b_dw", "g_conv_ln", "b_conv_ln", "g_post_mix", "g_pre_mlp", "g_post_mlp"]
    small_w = dict(g_pre_mix=g_pre_mix, b_in=b_in, b_dw=b_dw, g_conv_ln=g_conv_ln, b_conv_ln=b_conv_ln,
                   g_post_mix=g_post_mix, g_pre_mlp=g_pre_mlp, g_post_mlp=g_post_mlp)
    small_m = dict(g_pre_mix=m_g_pre_mix, b_in=m_b_in, b_dw=m_b_dw, g_conv_ln=m_g_conv_ln, b_conv_ln=m_b_conv_ln,
                   g_post_mix=m_g_post_mix, g_pre_mlp=m_g_pre_mlp, g_post_mlp=m_g_post_mlp)
    small_v = dict(g_pre_mix=v_g_pre_mix, b_in=v_b_in, b_dw=v_b_dw, g_conv_ln=v_g_conv_ln, b_conv_ln=v_b_conv_ln,
                   g_post_mix=v_g_post_mix, g_pre_mlp=v_g_pre_mlp, g_post_mlp=v_g_post_mlp)
    small_g = dict(g_pre_mix=dg_pre_mix, b_in=db_in, b_dw=db_dw, g_conv_ln=dg_conv_ln, b_conv_ln=db_conv_ln,
                   g_post_mix=dg_post_mix, g_pre_mlp=dg_pre_mlp, g_post_mlp=dg_post_mlp)
    shapes = [small_w[n].shape for n in small]
    tail_shapes = [loss_part.shape, dtaps.shape]
    summed = _all_reduce_small(_pack([small_g[n] for n in small] + [loss_part, dtaps]))
    zeros_tail = [jnp.zeros(shp, F32) for shp in tail_shapes]
    res = _adamw(_pack([small_w[n] for n in small] + zeros_tail), summed,
                 _pack([small_m[n] for n in small] + zeros_tail), _pack([small_v[n] for n in small] + zeros_tail),
                 "adamw_small")
    unpacked = [_unpack(r, shapes + tail_shapes) for r in res]
    for i, n in enumerate(small):
        out_g[n], out_d[n], out_m[n], out_v[n] = [u[i] for u in unpacked]
    loss = unpacked[0][len(small)][0, 0]
    taps_grad = lax.dynamic_slice(unpacked[0][len(small) + 1], (0, chip * cs), (CONV_HALO, cs))
    res = _adamw(_pad_rows(w_dw[0], CONV_HALO), taps_grad, _pad_rows(m_w_dw[0], CONV_HALO), _pad_rows(v_w_dw[0], CONV_HALO),
                 "adamw_taps")
    out_g["w_dw"], out_d["w_dw"], out_m["w_dw"], out_v["w_dw"] = [r[:CONV_WIDTH][None] for r in res]
    reduce_finish(group_in, "in", in_started, res[1])

    order = ["g_pre_mix", "w_in", "b_in", "w_dw", "b_dw", "g_conv_ln", "b_conv_ln", "w_sb_out", "w_conv_out", "w_o",
             "g_post_mix", "g_pre_mlp", "w_up", "w_down", "g_post_mlp"]
    return (loss, grad_x[None], *[out_g[n] for n in order], *[out_d[n] for n in order],
            *[out_m[n] for n in order], *[out_v[n] for n in order])
```

```python
import functools
import math

import jax
import jax.numpy as jnp
from jax import lax
from jax.experimental import pallas as pl
from jax.experimental.pallas import tpu as pltpu

F32 = jnp.float32
BF16 = jnp.bfloat16
MESH = pl.DeviceIdType.MESH

HEAD_DIM = 128
CONV_WIDTH = 31
CONV_HALO = 32
EPS = 1e-6
ADAM_LR = 0.001
ADAM_B1 = 0.9
ADAM_B2 = 0.999
ADAM_EPS = 1e-08
ADAM_WD = 0.01
ADAM_STEP = 10
N_CHIPS = 4
VMEM_LIMIT = 48 * 1024 * 1024


def _params(**kw):
    return pltpu.CompilerParams(vmem_limit_bytes=VMEM_LIMIT, **kw)


def _tile(n, want, align=128):
    if n <= want:
        return n
    for t in range(want - want % align, 0, -align):
        if n % t == 0:
            return t
    raise ValueError((n, want, align))


def _mm(a, b, *, name, ta=False, tb=False, b_groups=1, out_groups=1, bias=None, extra=None, epilogue=None,
        out_dtypes=(F32,), tm=1024, tn=1024, tk=512, after=None):
    if ta:
        K, M = a.shape
    else:
        M, K = a.shape
    if b_groups == 1:
        br, bc = b.shape
    else:
        _, br, bcg = b.shape
        bc = bcg * b_groups
    if tb:
        N, Kb = br, bc
    else:
        Kb, N = br, bc
    assert Kb == K, (name, a.shape, b.shape)
    tm, tn, tk = _tile(M, tm), _tile(N, tn), _tile(K, tk)
    if b_groups > 1:
        if tb:
            tk = _tile(K // b_groups, tk)
        else:
            tn = _tile(N // b_groups, tn)
    if out_groups > 1:
        tn = _tile(N // out_groups, tn)
        if b_groups > 1 and not tb:
            tn = _tile(N // b_groups, tn)
    nm, nn, nk = M // tm, N // tn, K // tk
    extra = tuple(extra or ())
    n_extra = len(extra)
    has_bias = bias is not None
    n_out = len(out_dtypes)

    a_spec = pl.BlockSpec((tk, tm), lambda i, j, k: (k, i)) if ta else pl.BlockSpec((tm, tk), lambda i, j, k: (i, k))
    if b_groups == 1:
        b_spec = pl.BlockSpec((tn, tk), lambda i, j, k: (j, k)) if tb else pl.BlockSpec((tk, tn), lambda i, j, k: (k, j))
    elif tb:
        kpg = (K // b_groups) // tk
        b_spec = pl.BlockSpec((None, tn, tk), lambda i, j, k: (k // kpg, j, k % kpg))
    else:
        npg = (N // b_groups) // tn
        b_spec = pl.BlockSpec((None, tk, tn), lambda i, j, k: (j // npg, k, j % npg))
    in_specs = [a_spec, b_spec]
    operands = [a, b]
    if has_bias:
        in_specs.append(pl.BlockSpec((1, tn), lambda i, j, k: (0, j)))
        operands.append(bias)
    for e in extra:
        in_specs.append(pl.BlockSpec((tm, tn), lambda i, j, k: (i, j)))
        operands.append(e)
    n_after = 0 if after is None else 1
    if after is not None:
        in_specs.append(pl.BlockSpec(memory_space=pl.ANY))
        operands.append(after)
    if out_groups == 1:
        o_spec = pl.BlockSpec((tm, tn), lambda i, j, k: (i, j))
        o_shape = (M, N)
    else:
        opg = (N // out_groups) // tn
        o_spec = pl.BlockSpec((None, tm, tn), lambda i, j, k: (j // opg, i, j % opg))
        o_shape = (out_groups, M, N // out_groups)
    dims = (((0 if ta else 1,), (1 if tb else 0,)), ((), ()))

    def body(*refs):
        a_ref, b_ref = refs[0], refs[1]
        pos = 2
        bias_ref = None
        if has_bias:
            bias_ref = refs[pos]
            pos += 1
        extra_refs = refs[pos:pos + n_extra]
        pos += n_extra + n_after
        out_refs = refs[pos:pos + n_out]
        pos += n_out
        acc_ref = refs[pos] if nk > 1 else None

        part = lax.dot_general(a_ref[...].astype(BF16), b_ref[...].astype(BF16), dims, preferred_element_type=F32)

        def finish(acc):
            if has_bias:
                acc = acc + bias_ref[...]
            outs = epilogue(acc, *[e[...] for e in extra_refs]) if epilogue is not None else (acc,)
            for o_ref, o in zip(out_refs, outs):
                o_ref[...] = o.astype(o_ref.dtype)

        if nk == 1:
            finish(part)
        else:
            k = pl.program_id(2)

            @pl.when(k == 0)
            def _():
                acc_ref[...] = part

            @pl.when(k > 0)
            def _():
                acc_ref[...] += part

            @pl.when(k == nk - 1)
            def _():
                finish(acc_ref[...])

    outs = pl.pallas_call(
        body,
        name=name,
        grid=(nm, nn, nk),
        in_specs=in_specs,
        out_specs=[o_spec] * n_out,
        out_shape=[jax.ShapeDtypeStruct(o_shape, dt) for dt in out_dtypes],
        scratch_shapes=[pltpu.VMEM((tm, tn), F32)] if nk > 1 else [],
        compiler_params=_params(dimension_semantics=("parallel", "parallel", "arbitrary")),
    )(*operands)
    return outs[0] if n_out == 1 else outs


def _rms_stats(x):
    return lax.rsqrt(jnp.mean(x * x, axis=-1, keepdims=True) + EPS)


def _rms_bwd(x, r, g, dy):
    gy = dy * g
    return r * gy - x * (r * r * r) * jnp.mean(x * gy, axis=-1, keepdims=True)


def _sigmoid(x):
    return 1.0 / (1.0 + jnp.exp(-x))


def _row_call(body, *, name, rows, tr, ins, outs, acc_outs=()):
    n = rows // tr
    in_specs = []
    for arr, kind in ins:
        if kind == "row":
            in_specs.append(pl.BlockSpec((tr, arr.shape[1]), lambda i: (i, 0)))
        else:
            in_specs.append(pl.BlockSpec(arr.shape, lambda i: (0, 0)))
    out_specs = [pl.BlockSpec((tr, w), lambda i: (i, 0)) for w, _ in outs]
    out_shape = [jax.ShapeDtypeStruct((rows, w), dt) for w, dt in outs]
    out_specs += [pl.BlockSpec((1, w), lambda i: (0, 0)) for w in acc_outs]
    out_shape += [jax.ShapeDtypeStruct((1, w), F32) for w in acc_outs]
    return pl.pallas_call(
        body, name=name, grid=(n,), in_specs=in_specs, out_specs=out_specs, out_shape=out_shape,
        compiler_params=_params(dimension_semantics=("arbitrary",)),
    )(*[a for a, _ in ins])


def _accumulate(ref, val):
    @pl.when(pl.program_id(0) == 0)
    def _():
        ref[...] = val

    @pl.when(pl.program_id(0) > 0)
    def _():
        ref[...] += val


def _rms_fwd(x, g, tr=256):
    def body(x_ref, g_ref, h_ref):
        xv = x_ref[...]
        h_ref[...] = (xv * _rms_stats(xv) * g_ref[...]).astype(BF16)

    (h,) = _row_call(body, name="rms_fwd", rows=x.shape[0], tr=tr, ins=[(x, "row"), (g, "vec")], outs=[(x.shape[1], BF16)])
    return h


def _resid_rms(x, y, g2, g3, tr=256):
    def body(x_ref, y_ref, g2_ref, g3_ref, x2_ref, h2_ref):
        yv = y_ref[...]
        x2 = x_ref[...] + yv * _rms_stats(yv) * g2_ref[...]
        x2_ref[...] = x2
        h2_ref[...] = (x2 * _rms_stats(x2) * g3_ref[...]).astype(BF16)

    d = x.shape[1]
    return _row_call(body, name="resid_rms", rows=x.shape[0], tr=tr,
                     ins=[(x, "row"), (y, "row"), (g2, "vec"), (g3, "vec")], outs=[(d, F32), (d, BF16)])


def _final(x2, dn, g4, tgt, tr=256):
    d = x2.shape[1]

    def body(x2_ref, dn_ref, g4_ref, t_ref, dx3_ref, ddn_ref, dg4_ref, loss_ref):
        dn_v = dn_ref[...]
        r = _rms_stats(dn_v)
        g = g4_ref[...]
        e = x2_ref[...] + dn_v * r * g - t_ref[...]
        dx3 = e * (1.0 / d)
        dx3_ref[...] = dx3
        ddn_ref[...] = _rms_bwd(dn_v, r, g, dx3).astype(BF16)
        _accumulate(dg4_ref, jnp.sum(dx3 * dn_v * r, axis=0, keepdims=True))
        part = 0.5 * jnp.sum(jnp.mean(e * e, axis=-1, keepdims=True), axis=0, keepdims=True)
        _accumulate(loss_ref, jnp.broadcast_to(part, loss_ref.shape))

    return _row_call(body, name="final", rows=x2.shape[0], tr=tr,
                     ins=[(x2, "row"), (dn, "row"), (g4, "vec"), (tgt, "row")],
                     outs=[(d, F32), (d, BF16)], acc_outs=(d, 128))


def _rms_bwd2(dx3, dh2, x2, y, g2, g3, tr=256):
    d = x2.shape[1]

    def body(dx3_ref, dh2_ref, x2_ref, y_ref, g2_ref, g3_ref, dx2_ref, dy_ref, dg3_ref, dg2_ref):
        x2v, dh2v, yv = x2_ref[...], dh2_ref[...], y_ref[...]
        r3 = _rms_stats(x2v)
        dx2 = dx3_ref[...] + _rms_bwd(x2v, r3, g3_ref[...], dh2v)
        dx2_ref[...] = dx2
        _accumulate(dg3_ref, jnp.sum(dh2v * x2v * r3, axis=0, keepdims=True))
        r2 = _rms_stats(yv)
        dy_ref[...] = _rms_bwd(yv, r2, g2_ref[...], dx2).astype(BF16)
        _accumulate(dg2_ref, jnp.sum(dx2 * yv * r2, axis=0, keepdims=True))

    return _row_call(body, name="rms_bwd2", rows=x2.shape[0], tr=tr,
                     ins=[(dx3, "row"), (dh2, "row"), (x2, "row"), (y, "row"), (g2, "vec"), (g3, "vec")],
                     outs=[(d, F32), (d, BF16)], acc_outs=(d, d))


def _rms_bwd1(dx2, dh, x, g1, tr=256):
    d = x.shape[1]

    def body(dx2_ref, dh_ref, x_ref, g1_ref, gx_ref, dg1_ref):
        xv, dhv = x_ref[...], dh_ref[...]
        r = _rms_stats(xv)
        gx_ref[...] = dx2_ref[...] + _rms_bwd(xv, r, g1_ref[...], dhv)
        _accumulate(dg1_ref, jnp.sum(dhv * xv * r, axis=0, keepdims=True))

    return _row_call(body, name="rms_bwd1", rows=x.shape[0], tr=tr,
                     ins=[(dx2, "row"), (dh, "row"), (x, "row"), (g1, "vec")], outs=[(d, F32)], acc_outs=(d,))


def _merge_fwd(proj, o_sb, o_cv, off_sb, off_cv, gw, tr=256):
    s, d = o_sb.shape
    nj = d // gw
    b_sb, b_cv = off_sb // gw, off_cv // gw

    def body(gs_ref, gc_ref, osb_ref, ocv_ref, m_ref):
        m_ref[...] = (_sigmoid(gs_ref[...]) * osb_ref[...] + _sigmoid(gc_ref[...]) * ocv_ref[...]).astype(BF16)

    blk = lambda i, j: (i, j)
    return pl.pallas_call(
        body, name="merge_fwd", grid=(s // tr, nj),
        in_specs=[pl.BlockSpec((tr, gw), lambda i, j: (i, b_sb + j)), pl.BlockSpec((tr, gw), lambda i, j: (i, b_cv + j)),
                  pl.BlockSpec((tr, gw), blk), pl.BlockSpec((tr, gw), blk)],
        out_specs=pl.BlockSpec((tr, gw), blk), out_shape=jax.ShapeDtypeStruct((s, d), BF16),
        compiler_params=_params(dimension_semantics=("parallel", "parallel")),
    )(proj, proj, o_sb, o_cv)


def _merge_bwd(proj, o_sb, o_cv, dmerged, off_sb, off_cv, gw, tr=256):
    s, d = o_sb.shape
    nj = d // gw
    ni = s // tr
    b_sb, b_cv = off_sb // gw, off_cv // gw

    def body(gs_ref, gc_ref, osb_ref, ocv_ref, dm_ref, dosb_ref, docv_ref, dgs_ref, dgc_ref, sgs_ref, sgc_ref):
        dm = dm_ref[...]
        s_sb, s_cv = _sigmoid(gs_ref[...]), _sigmoid(gc_ref[...])
        dosb_ref[...] = (dm * s_sb).astype(BF16)
        docv_ref[...] = (dm * s_cv).astype(BF16)
        dgs = dm * osb_ref[...] * s_sb * (1.0 - s_sb)
        dgc = dm * ocv_ref[...] * s_cv * (1.0 - s_cv)
        dgs_ref[...] = dgs.astype(BF16)
        dgc_ref[...] = dgc.astype(BF16)
        i = pl.program_id(1)
        for ref, val in ((sgs_ref, dgs), (sgc_ref, dgc)):
            col = jnp.sum(val, axis=0, keepdims=True)

            @pl.when(i == 0)
            def _():
                ref[...] = col

            @pl.when(i > 0)
            def _():
                ref[...] += col

    blk = lambda j, i: (i, j)
    outs = pl.pallas_call(
        body, name="merge_bwd", grid=(nj, ni),
        in_specs=[pl.BlockSpec((tr, gw), lambda j, i: (i, b_sb + j)), pl.BlockSpec((tr, gw), lambda j, i: (i, b_cv + j)),
                  pl.BlockSpec((tr, gw), blk), pl.BlockSpec((tr, gw), blk), pl.BlockSpec((tr, gw), blk)],
        out_specs=[pl.BlockSpec((tr, gw), blk), pl.BlockSpec((tr, gw), blk),
                   pl.BlockSpec((tr, gw), blk), pl.BlockSpec((tr, gw), blk),
                   pl.BlockSpec((1, gw), lambda j, i: (0, j)), pl.BlockSpec((1, gw), lambda j, i: (0, j))],
        out_shape=[jax.ShapeDtypeStruct((s, d), BF16)] * 4 + [jax.ShapeDtypeStruct((1, d), F32)] * 2,
        compiler_params=_params(dimension_semantics=("parallel", "arbitrary")),
    )(proj, proj, o_sb, o_cv, dmerged)
    return outs


def _split_bf16(v):
    hi = v.astype(BF16)
    lo = (v - hi.astype(F32)).astype(BF16)
    return hi, lo


def _dot_nt(a, b):
    return lax.dot_general(a, b, (((1,), (1,)), ((), ())), preferred_element_type=F32)


def _dot_tn(a, b):
    return lax.dot_general(a, b, (((0,), (0,)), ((), ())), preferred_element_type=F32)


def _dot_nn(a, b):
    return lax.dot_general(a, b, (((1,), (0,)), ((), ())), preferred_element_type=F32)


def _sum_right(v, tri):
    hi, lo = _split_bf16(v)
    return _dot_nn(hi, tri) + _dot_nn(lo, tri)


def _sb_tile(q, kb, i, j, tq, scale):
    z = _dot_nt(q, kb) * scale
    row = lax.broadcasted_iota(jnp.int32, z.shape, 0) + i * tq
    col = lax.broadcasted_iota(jnp.int32, z.shape, 1) + j * tq
    mask = col < row
    sp = jnp.log1p(jnp.exp(-jnp.abs(z)))
    log_b = jnp.minimum(z, 0.0) - sp
    log_1m = jnp.where(mask, log_b - z, 0.0)
    return log_b, log_1m, mask


def _tri(tq, before):
    r = lax.broadcasted_iota(jnp.int32, (tq, tq), 0)
    c = lax.broadcasted_iota(jnp.int32, (tq, tq), 1)
    return jnp.where((r < c) if before else (r > c), 1.0, 0.0).astype(BF16)


def _attn_fwd(proj, n_heads, d_sb, tq=256):
    s = proj.shape[0]
    tq = _tile(s, tq)
    hb = d_sb // HEAD_DIM
    scale = 1.0 / math.sqrt(HEAD_DIM)

    def body(q_ref, k_ref, v_ref, o_ref):
        i = pl.program_id(1)
        q = q_ref[...].astype(BF16)
        tri = _tri(tq, False)

        def step(jj, carry):
            c_l, acc = carry
            j = i - jj
            rows = pl.ds(pl.multiple_of(j * tq, tq), tq)
            kb = k_ref[rows, :].astype(BF16)
            vb = v_ref[rows, :].astype(BF16)
            log_b, log_1m, mask = _sb_tile(q, kb, i, j, tq, scale)
            suffix = c_l + _sum_right(log_1m, tri)
            a = jnp.where(mask, jnp.exp(log_b + suffix), 0.0)
            acc = acc + _dot_nn(a.astype(BF16), vb)
            return c_l + jnp.sum(log_1m, axis=1, keepdims=True), acc

        _, acc = lax.fori_loop(0, i + 1, step, (jnp.zeros((tq, 1), F32), jnp.zeros((tq, HEAD_DIM), F32)))
        o_ref[...] = acc.astype(BF16)

    return pl.pallas_call(
        body, name="attn_fwd", grid=(n_heads, s // tq),
        in_specs=[pl.BlockSpec((tq, HEAD_DIM), lambda h, i: (i, h)),
                  pl.BlockSpec((s, HEAD_DIM), lambda h, i: (0, hb + h)),
                  pl.BlockSpec((s, HEAD_DIM), lambda h, i: (0, 2 * hb + h))],
        out_specs=pl.BlockSpec((tq, HEAD_DIM), lambda h, i: (i, h)),
        out_shape=jax.ShapeDtypeStruct((s, d_sb), BF16),
        compiler_params=_params(dimension_semantics=("parallel", "arbitrary")),
    )(proj, proj, proj)


def _attn_bwd(proj, dattn, after, n_heads, d_sb, tq=256):
    s = proj.shape[0]
    tq = _tile(s, tq)
    nq = s // tq
    hb = d_sb // HEAD_DIM
    scale = 1.0 / math.sqrt(HEAD_DIM)

    def body(q_ref, k_ref, v_ref, do_ref, _, dq_ref, dk_ref, dv_ref, sq_ref, sk_ref, sv_ref, dk_acc, dv_acc, g_st, b_st):
        i = pl.program_id(1)

        @pl.when(i == 0)
        def _():
            dk_acc[...] = jnp.zeros_like(dk_acc)
            dv_acc[...] = jnp.zeros_like(dv_acc)

        q = q_ref[...].astype(BF16)
        do = do_ref[...].astype(BF16)
        tri_after = _tri(tq, False)
        tri_before = _tri(tq, True)

        def newest_first(jj, c_l):
            j = i - jj
            rows = pl.ds(pl.multiple_of(j * tq, tq), tq)
            kb = k_ref[rows, :].astype(BF16)
            vb = v_ref[rows, :].astype(BF16)
            log_b, log_1m, mask = _sb_tile(q, kb, i, j, tq, scale)
            suffix = c_l + _sum_right(log_1m, tri_after)
            a = jnp.where(mask, jnp.exp(log_b + suffix), 0.0)
            g_st[j] = a * _dot_nt(do, vb)
            b_st[j] = jnp.where(mask, jnp.exp(log_b), 0.0)
            dv_acc[rows, :] += _dot_tn(a.astype(BF16), do)
            return c_l + jnp.sum(log_1m, axis=1, keepdims=True)

        lax.fori_loop(0, i + 1, newest_first, jnp.zeros((tq, 1), F32))

        def oldest_first(j, carry):
            c_g, dq = carry
            rows = pl.ds(pl.multiple_of(j * tq, tq), tq)
            kb = k_ref[rows, :].astype(BF16)
            g = g_st[j]
            beta = b_st[j]
            g_before = c_g + _sum_right(g, tri_before)
            dz16 = ((g * (1.0 - beta) - g_before * beta) * scale).astype(BF16)
            dk_acc[rows, :] += _dot_tn(dz16, q)
            return c_g + jnp.sum(g, axis=1, keepdims=True), dq + _dot_nn(dz16, kb)

        _, dq = lax.fori_loop(0, i + 1, oldest_first, (jnp.zeros((tq, 1), F32), jnp.zeros((tq, HEAD_DIM), F32)))
        dq_ref[...] = dq.astype(BF16)
        col = jnp.sum(dq, axis=0, keepdims=True)

        @pl.when(i == 0)
        def _():
            sq_ref[...] = col

        @pl.when(i > 0)
        def _():
            sq_ref[...] += col

        @pl.when(i == nq - 1)
        def _():
            dk = dk_acc[...]
            dv = dv_acc[...]
            dk_ref[...] = dk.astype(BF16)
            dv_ref[...] = dv.astype(BF16)
            sk_ref[...] = jnp.sum(dk, axis=0, keepdims=True)
            sv_ref[...] = jnp.sum(dv, axis=0, keepdims=True)

    qblk = pl.BlockSpec((tq, HEAD_DIM), lambda h, i: (i, h))
    kblk = pl.BlockSpec((s, HEAD_DIM), lambda h, i: (0, h))
    col_sum = pl.BlockSpec((1, HEAD_DIM), lambda h, i: (0, h))
    outs = pl.pallas_call(
        body, name="attn_bwd", grid=(n_heads, nq),
        in_specs=[qblk,
                  pl.BlockSpec((s, HEAD_DIM), lambda h, i: (0, hb + h)),
                  pl.BlockSpec((s, HEAD_DIM), lambda h, i: (0, 2 * hb + h)),
                  qblk, pl.BlockSpec(memory_space=pl.ANY)],
        out_specs=[qblk, kblk, kblk, col_sum, col_sum, col_sum],
        out_shape=[jax.ShapeDtypeStruct((s, d_sb), BF16)] * 3 + [jax.ShapeDtypeStruct((1, d_sb), F32)] * 3,
        scratch_shapes=[pltpu.VMEM((s, HEAD_DIM), F32), pltpu.VMEM((s, HEAD_DIM), F32),
                        pltpu.VMEM((nq, tq, tq), F32), pltpu.VMEM((nq, tq, tq), F32)],
        compiler_params=_params(dimension_semantics=("parallel", "arbitrary")),
    )(proj, proj, proj, dattn, after)
    return outs


LANES = 128


def _glu(proj, off_a, dc, tr=256):
    s = proj.shape[0]
    ba = off_a // dc

    def body(a_ref, b_ref, u_ref):
        u_ref[...] = a_ref[...] * _sigmoid(b_ref[...])

    return pl.pallas_call(
        body, name="glu", grid=(s // tr,),
        in_specs=[pl.BlockSpec((tr, dc), lambda i: (i, ba)), pl.BlockSpec((tr, dc), lambda i: (i, ba + 1))],
        out_specs=pl.BlockSpec((tr, dc), lambda i: (i, 0)), out_shape=jax.ShapeDtypeStruct((s, dc), F32),
        compiler_params=_params(dimension_semantics=("parallel",)),
    )(proj, proj)


def _conv_fwd(u0, taps, b_dw, g_ln, b_ln, tt=256):
    s, dc = u0.shape
    tt = _tile(s, tt)
    hpb = tt // CONV_HALO

    def body(cur_ref, halo_ref, w_ref, bdw_ref, g_ref, b_ref, u1_ref, u3_ref, xs):
        i = pl.program_id(0)
        xs[pl.ds(0, CONV_HALO), :] = jnp.where(i > 0, halo_ref[...], 0.0)
        xs[pl.ds(CONV_HALO, tt), :] = cur_ref[...]
        for c0 in range(0, dc, LANES):
            cols = pl.ds(c0, LANES)
            acc = jnp.broadcast_to(bdw_ref[:, cols], (tt, LANES))
            for k in range(CONV_WIDTH):
                acc = acc + w_ref[pl.ds(k, 1), cols] * xs[pl.ds(CONV_HALO - (CONV_WIDTH - 1) + k, tt), cols]
            u1_ref[:, cols] = acc
        u1 = u1_ref[...]
        mu = jnp.mean(u1, axis=-1, keepdims=True)
        xc = u1 - mu
        rstd = lax.rsqrt(jnp.mean(xc * xc, axis=-1, keepdims=True) + EPS)
        u2 = xc * rstd * g_ref[...] + b_ref[...]
        u3_ref[...] = (u2 * _sigmoid(u2)).astype(BF16)

    vec = pl.BlockSpec((1, dc), lambda i: (0, 0))
    return pl.pallas_call(
        body, name="conv_fwd", grid=(s // tt,),
        in_specs=[pl.BlockSpec((tt, dc), lambda i: (i, 0)),
                  pl.BlockSpec((CONV_HALO, dc), lambda i: (jnp.maximum(i * hpb - 1, 0), 0)),
                  pl.BlockSpec((CONV_HALO, dc), lambda i: (0, 0)), vec, vec, vec],
        out_specs=[pl.BlockSpec((tt, dc), lambda i: (i, 0)), pl.BlockSpec((tt, dc), lambda i: (i, 0))],
        out_shape=[jax.ShapeDtypeStruct((s, dc), F32), jax.ShapeDtypeStruct((s, dc), BF16)],
        scratch_shapes=[pltpu.VMEM((tt + CONV_HALO, dc), F32)],
        compiler_params=_params(dimension_semantics=("arbitrary",)),
    )(u0, u0, taps, b_dw, g_ln, b_ln)


def _conv_bwd_norm(u1, du3, g_ln, b_ln, tr=256):
    dc = u1.shape[1]

    def body(u1_ref, du3_ref, g_ref, b_ref, du1_ref, dg_ref, db_ref, dbdw_ref):
        u1v = u1_ref[...]
        g = g_ref[...]
        mu = jnp.mean(u1v, axis=-1, keepdims=True)
        xc = u1v - mu
        rstd = lax.rsqrt(jnp.mean(xc * xc, axis=-1, keepdims=True) + EPS)
        xhat = xc * rstd
        u2 = xhat * g + b_ref[...]
        sg = _sigmoid(u2)
        du2 = du3_ref[...] * (sg * (1.0 + u2 * (1.0 - sg)))
        _accumulate(dg_ref, jnp.sum(du2 * xhat, axis=0, keepdims=True))
        _accumulate(db_ref, jnp.sum(du2, axis=0, keepdims=True))
        gy = du2 * g
        du1 = rstd * (gy - jnp.mean(gy, axis=-1, keepdims=True) - xhat * jnp.mean(gy * xhat, axis=-1, keepdims=True))
        du1_ref[...] = du1
        _accumulate(dbdw_ref, jnp.sum(du1, axis=0, keepdims=True))

    return _row_call(body, name="conv_bwd_norm", rows=u1.shape[0], tr=tr,
                     ins=[(u1, "row"), (du3, "row"), (g_ln, "vec"), (b_ln, "vec")],
                     outs=[(dc, F32)], acc_outs=(dc, dc, dc))


def _conv_bwd_taps(du1, u0, proj, taps, off_a, tt=256):
    s, dc = u0.shape
    tt = _tile(s, tt)
    n = s // tt
    hpb = tt // CONV_HALO
    ba = off_a // dc
    lead = CONV_HALO - (CONV_WIDTH - 1)

    def body(d_ref, dnext_ref, cur_ref, halo_ref, a_ref, b_ref, w_ref, da_ref, db_ref, sa_ref, sb_ref, dw_ref,
             xs, ds, du0):
        i = pl.program_id(0)
        xs[pl.ds(0, CONV_HALO), :] = jnp.where(i > 0, halo_ref[...], 0.0)
        xs[pl.ds(CONV_HALO, tt), :] = cur_ref[...]
        ds[pl.ds(0, tt), :] = d_ref[...]
        ds[pl.ds(tt, CONV_HALO), :] = jnp.where(i < n - 1, dnext_ref[...], 0.0)

        @pl.when(i == 0)
        def _():
            dw_ref[...] = jnp.zeros_like(dw_ref)

        for c0 in range(0, dc, LANES):
            cols = pl.ds(c0, LANES)
            d_cur = ds[pl.ds(0, tt), cols]
            acc = jnp.zeros((tt, LANES), F32)
            for k in range(CONV_WIDTH):
                acc = acc + w_ref[pl.ds(k, 1), cols] * ds[pl.ds(CONV_WIDTH - 1 - k, tt), cols]
                dw_ref[pl.ds(k, 1), cols] += jnp.sum(d_cur * xs[pl.ds(lead + k, tt), cols], axis=0, keepdims=True)
            du0[:, cols] = acc
        d0 = du0[...]
        sg = _sigmoid(b_ref[...])
        da = d0 * sg
        db = d0 * a_ref[...] * sg * (1.0 - sg)
        da_ref[...] = da.astype(BF16)
        db_ref[...] = db.astype(BF16)
        _accumulate(sa_ref, jnp.sum(da, axis=0, keepdims=True))
        _accumulate(sb_ref, jnp.sum(db, axis=0, keepdims=True))

    row = pl.BlockSpec((tt, dc), lambda i: (i, 0))
    vec = pl.BlockSpec((1, dc), lambda i: (0, 0))
    taps_spec = pl.BlockSpec((CONV_HALO, dc), lambda i: (0, 0))
    return pl.pallas_call(
        body, name="conv_bwd_taps", grid=(n,),
        in_specs=[row, pl.BlockSpec((CONV_HALO, dc), lambda i: (jnp.minimum((i + 1) * hpb, n * hpb - 1), 0)),
                  row, pl.BlockSpec((CONV_HALO, dc), lambda i: (jnp.maximum(i * hpb - 1, 0), 0)),
                  pl.BlockSpec((tt, dc), lambda i: (i, ba)), pl.BlockSpec((tt, dc), lambda i: (i, ba + 1)), taps_spec],
        out_specs=[row, row, vec, vec, taps_spec],
        out_shape=[jax.ShapeDtypeStruct((s, dc), BF16), jax.ShapeDtypeStruct((s, dc), BF16),
                   jax.ShapeDtypeStruct((1, dc), F32), jax.ShapeDtypeStruct((1, dc), F32),
                   jax.ShapeDtypeStruct((CONV_HALO, dc), F32)],
        scratch_shapes=[pltpu.VMEM((tt + CONV_HALO, dc), F32), pltpu.VMEM((tt + CONV_HALO, dc), F32),
                        pltpu.VMEM((tt, dc), F32)],
        compiler_params=_params(dimension_semantics=("arbitrary",)),
    )(du1, du1, u0, u0, proj, proj, taps)


def _elementwise(body, *, name, ins, out_dtypes, tr=128):
    rows, cols = ins[0].shape
    tr = _tile(rows, tr, 8)
    spec = pl.BlockSpec((tr, cols), lambda i: (i, 0))
    return pl.pallas_call(
        body, name=name, grid=(rows // tr,), in_specs=[spec] * len(ins), out_specs=[spec] * len(out_dtypes),
        out_shape=[jax.ShapeDtypeStruct((rows, cols), dt) for dt in out_dtypes],
        compiler_params=_params(dimension_semantics=("parallel",)),
    )(*ins)


def _prefetched(body, *, name, where, grid, in_specs, out_specs, out_shape, operands):
    return pl.pallas_call(
        body, name=name, out_shape=out_shape,
        grid_spec=pltpu.PrefetchScalarGridSpec(num_scalar_prefetch=1, grid=grid, in_specs=in_specs, out_specs=out_specs),
        compiler_params=_params(dimension_semantics=("parallel",) * len(grid)),
    )(where, *operands)


def _cast_into_block(w, where, after, name, tr=256):
    rows, cols = w.shape
    tr = _tile(rows, tr, 16)

    def body(where_ref, w_ref, _, o_ref):
        o_ref[...] = w_ref[...].astype(BF16)

    return _prefetched(
        body, name=name, where=where, grid=(rows // tr,),
        in_specs=[pl.BlockSpec((tr, cols), lambda i, wh: (i, 0)), pl.BlockSpec(memory_space=pl.ANY)],
        out_specs=pl.BlockSpec((None, tr, cols), lambda i, wh: (wh[0], i, 0)),
        out_shape=jax.ShapeDtypeStruct((N_CHIPS, rows, cols), BF16), operands=[w, after])


def _add_pair(grad, theirs, where, name, tr=256):
    nb, half, cols = theirs.shape
    tr = _tile(half, tr, 16)
    nh = half // tr

    def body(where_ref, a_ref, b_ref, o_ref):
        o_ref[...] = (a_ref[...].astype(F32) + b_ref[...].astype(F32)).astype(BF16)

    blk = pl.BlockSpec((None, tr, cols), lambda b, i, wh: (b, i, 0))
    return _prefetched(
        body, name=name, where=where, grid=(nb, nh),
        in_specs=[pl.BlockSpec((None, tr, cols), lambda b, i, wh: (b, wh[1] * nh + i, 0)), blk],
        out_specs=blk, out_shape=jax.ShapeDtypeStruct(theirs.shape, BF16), operands=[grad, theirs])


def _sum_slots(part, slots, where, name, tr=256):
    _, half, cols = part.shape
    tr = _tile(half, tr, 16)
    nh = half // tr

    def body(where_ref, p_ref, s_ref, o_ref):
        o_ref[...] = ((p_ref[...].astype(F32) + s_ref[0].astype(F32)) + s_ref[1].astype(F32)) + s_ref[2].astype(F32)

    return _prefetched(
        body, name=name, where=where, grid=(nh,),
        in_specs=[pl.BlockSpec((None, tr, cols), lambda i, wh: (wh[0], i, 0)),
                  pl.BlockSpec((3, tr, cols), lambda i, wh: (0, i, 0))],
        out_specs=pl.BlockSpec((tr, cols), lambda i, wh: (wh[1] * nh + i, 0)),
        out_shape=jax.ShapeDtypeStruct((2 * half, cols), F32), operands=[part, slots])


def _adamw(w, g, m, v, name):
    bc1 = 1.0 - ADAM_B1 ** ADAM_STEP
    bc2 = 1.0 - ADAM_B2 ** ADAM_STEP

    def body(w_ref, g_ref, m_ref, v_ref, go_ref, d_ref, mo_ref, vo_ref):
        g = g_ref[...]
        m_new = ADAM_B1 * m_ref[...] + (1.0 - ADAM_B1) * g
        v_new = ADAM_B2 * v_ref[...] + (1.0 - ADAM_B2) * (g * g)
        m_hat = m_new / bc1
        v_hat = v_new / bc2
        go_ref[...] = g
        d_ref[...] = -ADAM_LR * (m_hat / (jnp.sqrt(v_hat) + ADAM_EPS) + ADAM_WD * w_ref[...])
        mo_ref[...] = m_new
        vo_ref[...] = v_new

    return _elementwise(body, name=name, ins=[w, g, m, v], out_dtypes=[F32] * 4, tr=128)


ANY = pl.BlockSpec(memory_space=pl.ANY)


def _mesh_pos():
    return lax.axis_index("x"), lax.axis_index("y"), lax.axis_index("c")


def _other_chips(x, y):
    return [(1 - x, y), (x, 1 - y), (1 - x, 1 - y)]


def _remote(src, dst, send_sems, recv_sems, idx, to):
    return pltpu.make_async_remote_copy(src_ref=src, dst_ref=dst, send_sem=send_sems.at[idx], recv_sem=recv_sems.at[idx],
                                        device_id=to, device_id_type=MESH)


HBM = pl.BlockSpec(memory_space=pltpu.HBM)
SEM = pl.BlockSpec(memory_space=pltpu.SEMAPHORE)
EFFECT = pltpu.SideEffectType.DATAFLOW_SIDE_EFFECTING


def _in_hbm(arrays):
    return [pltpu.with_memory_space_constraint(a, pltpu.HBM) for a in arrays]


def _ici_copies(slab, send_sems, recv_sems, x, y, c):
    half = slab.shape[1] // 2
    rows = pl.ds(c * half, half)
    mine = slab.at[2 * x + y, rows]
    out = []
    for j, (cx, cy) in enumerate(_other_chips(x, y)):
        got = slab.at[2 * cx + cy, rows]
        out.append((_remote(mine, mine, send_sems, recv_sems, j, (cx, cy, c)),
                    _remote(got, got, send_sems, recv_sems, j, (cx, cy, c))))
    return out


def _gather_start(slabs, after, name):
    n = len(slabs)

    def body(*refs):
        ins = refs[:n]
        send_sems, recv_sems, token = refs[2 * n + 1:3 * n + 1], refs[3 * n + 1:4 * n + 1], refs[4 * n + 1]
        x, y, c = _mesh_pos()
        for w in range(n):
            for send, _ in _ici_copies(ins[w], send_sems[w], recv_sems[w], x, y, c):
                send.start()
        token[...] = jnp.zeros_like(token)

    outs = pl.pallas_call(
        body, name=name, in_specs=[HBM] * n + [ANY],
        out_specs=[HBM] * n + [SEM] * (2 * n) + [pl.BlockSpec(memory_space=pltpu.VMEM)],
        out_shape=[pltpu.HBM(s.shape, s.dtype) for s in slabs] + [pltpu.SemaphoreType.DMA((3,))] * (2 * n)
        + [jax.ShapeDtypeStruct((8, 128), F32)],
        input_output_aliases={w: w for w in range(n)},
        compiler_params=_params(has_side_effects=EFFECT),
    )(*_in_hbm(slabs), after)
    return (outs[:n], outs[n:2 * n], outs[2 * n:3 * n]), outs[3 * n]


def _gather_wait(slabs, send_sems, recv_sems, after, name):
    n = len(slabs)

    def body(*refs):
        ins = refs[:n]
        sends, recvs = refs[n:2 * n], refs[2 * n:3 * n]
        x, y, c = _mesh_pos()
        for w in range(n):
            for send, recv in _ici_copies(ins[w], sends[w], recvs[w], x, y, c):
                send.wait_send()
                recv.wait_recv()

    return pl.pallas_call(
        body, name=name, in_specs=[HBM] * n + [SEM] * (2 * n) + [ANY], out_specs=[HBM] * n,
        out_shape=[pltpu.HBM(s.shape, s.dtype) for s in slabs],
        input_output_aliases={w: w for w in range(n)},
        compiler_params=_params(has_side_effects=EFFECT),
    )(*slabs, *send_sems, *recv_sems, after)


def _gather_forward(slabs, name):
    n = len(slabs)

    def body(*refs):
        outs = refs[n:2 * n]
        send_sems, recv_sems = refs[2 * n:]
        x, y, c = _mesh_pos()
        sibling = (x, y, 1 - c)
        chips = _other_chips(x, y)
        started = []
        for w in range(n):
            half = outs[w].shape[1] // 2
            for j, (cx, cy) in enumerate(chips):
                got = outs[w].at[2 * cx + cy, pl.ds(c * half, half)]
                fw = _remote(got, got, send_sems, recv_sems, 3 * w + j, sibling)
                fw.start()
                started.append(fw)
        for w in range(n):
            half = outs[w].shape[1] // 2
            for j, (cx, cy) in enumerate(chips):
                got = outs[w].at[2 * cx + cy, pl.ds((1 - c) * half, half)]
                _remote(got, got, send_sems, recv_sems, 3 * w + j, sibling).wait_recv()
        for cp in started:
            cp.wait_send()

    return pl.pallas_call(
        body, name=name, in_specs=[ANY] * n, out_specs=[ANY] * n,
        out_shape=[jax.ShapeDtypeStruct(s.shape, s.dtype) for s in slabs],
        input_output_aliases={w: w for w in range(n)},
        scratch_shapes=[pltpu.SemaphoreType.DMA((3 * n,)), pltpu.SemaphoreType.DMA((3 * n,))],
        compiler_params=_params(has_side_effects=True),
    )(*slabs)


def _gather_taps(taps):
    def body(in_ref, out_ref, send_sems, recv_sems, local_sem):
        x, y, c = _mesh_pos()
        me = 2 * x + y
        chips = _other_chips(x, y)
        lc = pltpu.make_async_copy(in_ref, out_ref.at[me], local_sem)
        lc.start()
        sends = [_remote(in_ref, out_ref.at[me], send_sems, recv_sems, j, (cx, cy, c)) for j, (cx, cy) in enumerate(chips)]
        for cp in sends:
            cp.start()
        for j, (cx, cy) in enumerate(chips):
            got = out_ref.at[2 * cx + cy]
            _remote(got, got, send_sems, recv_sems, j, (cx, cy, c)).wait_recv()
        for cp in sends:
            cp.wait_send()
        lc.wait()

    vm = pl.BlockSpec(memory_space=pltpu.VMEM)
    return pl.pallas_call(
        body, name="gather_taps", in_specs=[vm], out_specs=vm,
        out_shape=jax.ShapeDtypeStruct((N_CHIPS,) + taps.shape, taps.dtype),
        scratch_shapes=[pltpu.SemaphoreType.DMA((3,)), pltpu.SemaphoreType.DMA((3,)), pltpu.SemaphoreType.DMA],
        compiler_params=_params(has_side_effects=True),
    )(taps)


def _exchange_halves(grads, name):
    n = len(grads)

    def body(*refs):
        ins, their_refs = refs[:n], refs[n:2 * n]
        send_sems, recv_sems = refs[2 * n:]
        x, y, c = _mesh_pos()
        sibling = (x, y, 1 - c)
        sends = []
        for w in range(n):
            half = ins[w].shape[1] // 2
            cp = _remote(ins[w].at[:, pl.ds((1 - c) * half, half), :], their_refs[w], send_sems, recv_sems, w, sibling)
            cp.start()
            sends.append(cp)
        for w in range(n):
            _remote(their_refs[w], their_refs[w], send_sems, recv_sems, w, sibling).wait_recv()
        for cp in sends:
            cp.wait_send()

    halves = [jax.ShapeDtypeStruct((g.shape[0], g.shape[1] // 2, g.shape[2]), g.dtype) for g in grads]
    return pl.pallas_call(
        body, name=name, in_specs=[ANY] * n, out_specs=[ANY] * n, out_shape=halves,
        scratch_shapes=[pltpu.SemaphoreType.DMA((n,)), pltpu.SemaphoreType.DMA((n,))],
        compiler_params=_params(has_side_effects=True),
    )(*grads)


def _scatter_copies(part, slots, send_sems, recv_sems, x, y, c):
    out = []
    for j, (cx, cy) in enumerate(_other_chips(x, y)):
        out.append((_remote(part.at[2 * cx + cy], slots.at[j], send_sems, recv_sems, j, (cx, cy, c)),
                    _remote(slots.at[j], slots.at[j], send_sems, recv_sems, j, (cx, cy, c))))
    return out


def _scatter_start(parts, after, name):
    n = len(parts)
    slots = [lax.empty((3,) + p.shape[1:], p.dtype) for p in parts]

    def body(*refs):
        ins, lands = refs[:n], refs[n:2 * n]
        send_sems, recv_sems, token = refs[4 * n + 1:5 * n + 1], refs[5 * n + 1:6 * n + 1], refs[6 * n + 1]
        x, y, c = _mesh_pos()
        for w in range(n):
            for send, _ in _scatter_copies(ins[w], lands[w], send_sems[w], recv_sems[w], x, y, c):
                send.start()
        token[...] = jnp.zeros_like(token)

    outs = pl.pallas_call(
        body, name=name, in_specs=[HBM] * (2 * n) + [ANY],
        out_specs=[HBM] * (2 * n) + [SEM] * (2 * n) + [pl.BlockSpec(memory_space=pltpu.VMEM)],
        out_shape=[pltpu.HBM(a.shape, a.dtype) for a in list(parts) + slots] + [pltpu.SemaphoreType.DMA((3,))] * (2 * n)
        + [jax.ShapeDtypeStruct((8, 128), F32)],
        input_output_aliases={k: k for k in range(2 * n)},
        compiler_params=_params(has_side_effects=EFFECT),
    )(*_in_hbm(list(parts) + slots), after)
    return (outs[:n], outs[n:2 * n], outs[2 * n:3 * n], outs[3 * n:4 * n]), outs[4 * n]


def _scatter_wait(parts, slots, send_sems, recv_sems, after, name):
    n = len(parts)

    def body(*refs):
        ins, lands = refs[:n], refs[n:2 * n]
        sends, recvs = refs[2 * n:3 * n], refs[3 * n:4 * n]
        x, y, c = _mesh_pos()
        for w in range(n):
            for send, recv in _scatter_copies(ins[w], lands[w], sends[w], recvs[w], x, y, c):
                send.wait_send()
                recv.wait_recv()

    outs = pl.pallas_call(
        body, name=name, in_specs=[HBM] * (2 * n) + [SEM] * (2 * n) + [ANY], out_specs=[HBM] * (2 * n),
        out_shape=[pltpu.HBM(a.shape, a.dtype) for a in list(parts) + list(slots)],
        input_output_aliases={k: k for k in range(2 * n)},
        compiler_params=_params(has_side_effects=EFFECT),
    )(*parts, *slots, *send_sems, *recv_sems, after)
    return outs[:n], outs[n:]


def _share_halves(halves, name):
    n = len(halves)

    def body(*refs):
        outs = refs[n:2 * n]
        send_sems, recv_sems = refs[2 * n:]
        x, y, c = _mesh_pos()
        sibling = (x, y, 1 - c)
        sends = []
        for w in range(n):
            h = outs[w].shape[0] // 2
            mine = outs[w].at[pl.ds(c * h, h)]
            cp = _remote(mine, mine, send_sems, recv_sems, w, sibling)
            cp.start()
            sends.append(cp)
        for w in range(n):
            h = outs[w].shape[0] // 2
            got = outs[w].at[pl.ds((1 - c) * h, h)]
            _remote(got, got, send_sems, recv_sems, w, sibling).wait_recv()
        for cp in sends:
            cp.wait_send()

    return pl.pallas_call(
        body, name=name, in_specs=[ANY] * n, out_specs=[ANY] * n,
        out_shape=[jax.ShapeDtypeStruct(h.shape, h.dtype) for h in halves],
        input_output_aliases={w: w for w in range(n)},
        scratch_shapes=[pltpu.SemaphoreType.DMA((n,)), pltpu.SemaphoreType.DMA((n,))],
        compiler_params=_params(has_side_effects=True),
    )(*halves)


def _all_reduce_small(buf):
    rows, width = buf.shape

    def body(in_ref, out_ref, gathered, send_sems, recv_sems):
        x, y, c = _mesh_pos()
        me = 4 * x + 2 * y + c
        gathered[me] = in_ref[...]
        flips = [(dx, dy, dz) for dx in (0, 1) for dy in (0, 1) for dz in (0, 1)][1:]
        peers = [((1 - x) if dx else x, (1 - y) if dy else y, (1 - c) if dz else c) for dx, dy, dz in flips]
        sends = [_remote(in_ref, gathered.at[me], send_sems, recv_sems, k, peer) for k, peer in enumerate(peers)]
        for cp in sends:
            cp.start()
        for k, (px, py, pc) in enumerate(peers):
            got = gathered.at[4 * px + 2 * py + pc]
            _remote(got, got, send_sems, recv_sems, k, (px, py, pc)).wait_recv()
        for cp in sends:
            cp.wait_send()
        total = gathered[0]
        for d in range(1, 8):
            total = total + gathered[d]
        out_ref[...] = total

    vm = pl.BlockSpec(memory_space=pltpu.VMEM)
    return pl.pallas_call(
        body, name="all_reduce_small", in_specs=[vm], out_specs=vm, out_shape=jax.ShapeDtypeStruct(buf.shape, F32),
        scratch_shapes=[pltpu.VMEM((8, rows, width), F32), pltpu.SemaphoreType.DMA((7,)), pltpu.SemaphoreType.DMA((7,))],
        compiler_params=_params(has_side_effects=True),
    )(buf)


PACK_W = 2048
PACK_ROWS = 8


def _pack(vectors):
    flat = jnp.concatenate([v.reshape(-1) for v in vectors])
    unit = PACK_W * PACK_ROWS
    total = -(-flat.shape[0] // unit) * unit
    return jnp.pad(flat, (0, total - flat.shape[0])).reshape(total // PACK_W, PACK_W)


def _unpack(buf, shapes):
    flat = buf.reshape(-1)
    out, pos = [], 0
    for shp in shapes:
        size = math.prod(shp)
        out.append(flat[pos:pos + size].reshape(shp))
        pos += size
    return out


def _pad_rows(a, rows):
    return jnp.pad(a, ((0, rows - a.shape[0]), (0, 0)))


def kernel(x, g_pre_mix, w_in, b_in, w_dw, b_dw, g_conv_ln, b_conv_ln, w_sb_out, w_conv_out, w_o, g_post_mix, g_pre_mlp, w_up, w_down, g_post_mlp, loss_target, m_g_pre_mix, m_w_in, m_b_in, m_w_dw, m_b_dw, m_g_conv_ln, m_b_conv_ln, m_w_sb_out, m_w_conv_out, m_w_o, m_g_post_mix, m_g_pre_mlp, m_w_up, m_w_down, m_g_post_mlp, v_g_pre_mix, v_w_in, v_b_in, v_w_dw, v_b_dw, v_g_conv_ln, v_b_conv_ln, v_w_sb_out, v_w_conv_out, v_w_o, v_g_post_mix, v_g_pre_mlp, v_w_up, v_w_down, v_g_post_mlp):
    xs = x[0]
    tgt = loss_target[0]
    s, d = xs.shape
    d_sb = w_sb_out.shape[1]
    dc = w_conv_out.shape[1]
    d_in = w_in.shape[2] * N_CHIPS
    n_heads = d_sb // HEAD_DIM
    off_a = 3 * d_sb
    off_sb = off_a + 2 * dc
    off_cv = off_sb + d
    gw = dc
    assert off_a % dc == 0 and off_sb % gw == 0 and d % gw == 0 and d_in == off_cv + d
    chip = 2 * lax.axis_index("x") + lax.axis_index("y")

    big = dict(w_in=w_in[0], w_sb_out=w_sb_out[0], w_conv_out=w_conv_out[0], w_o=w_o[0], w_up=w_up[0], w_down=w_down[0])
    names = list(big)
    col_sharded = {"w_in", "w_sb_out", "w_conv_out", "w_up"}
    where = jnp.stack([chip, lax.axis_index("c")]).astype(jnp.int32)
    cs = w_dw.shape[2]
    taps4 = _gather_taps(_pad_rows(w_dw[0], CONV_HALO))
    first, token = _gather_start([_cast_into_block(big["w_in"], where, taps4, "cast_w_in")], taps4, "gather_start_in")
    rest, _ = _gather_start([_cast_into_block(big[n], where, token, "cast_" + n) for n in names[1:]], token,
                            "gather_start_rest")
    in_flight = dict(zip(names, zip(*[a + b for a, b in zip(first, rest)])))
    gathered = {}

    def arrive(group, after, tag):
        thru, sends, recvs = zip(*[in_flight[n] for n in group])
        landed = _gather_wait(thru, sends, recvs, after, "gather_wait_" + tag)
        for n, g4 in zip(group, _gather_forward(landed, "gather_forward_" + tag)):
            gathered[n] = g4 if n in col_sharded else g4.reshape(g4.shape[0] * g4.shape[1], g4.shape[2])

    taps = jnp.transpose(taps4, (1, 0, 2)).reshape(CONV_HALO, N_CHIPS * cs)

    h = _rms_fwd(xs, g_pre_mix)
    arrive(["w_in"], h, "in")
    proj = _mm(h, gathered["w_in"], name="mm_proj", b_groups=N_CHIPS, bias=b_in, tn=768)
    arrive(["w_sb_out", "w_conv_out", "w_o"], proj, "mix")
    attn = _attn_fwd(proj, n_heads, d_sb)
    u0 = _glu(proj, off_a, dc)
    u1, u3 = _conv_fwd(u0, taps, b_dw, g_conv_ln, b_conv_ln)
    arrive(["w_up"], u3, "up")
    o_sb = _mm(attn, gathered["w_sb_out"], name="mm_o_sb", b_groups=N_CHIPS)
    o_cv = _mm(u3, gathered["w_conv_out"], name="mm_o_cv", b_groups=N_CHIPS)
    merged = _merge_fwd(proj, o_sb, o_cv, off_sb, off_cv, gw)
    y = _mm(merged, gathered["w_o"], name="mm_y")
    x2, h2 = _resid_rms(xs, y, g_post_mix, g_pre_mlp)
    up, f = _mm(h2, gathered["w_up"], name="mm_up", b_groups=N_CHIPS, out_dtypes=(F32, BF16),
                epilogue=lambda acc: (acc, jnp.square(jnp.maximum(acc, 0.0))))
    arrive(["w_down"], f, "down")
    dn = _mm(f, gathered["w_down"], name="mm_down")
    dx3, d_dn, dg_post_mlp, loss_part = _final(x2, dn, g_post_mlp, tgt)

    moments = dict(w_in=(m_w_in, v_w_in), w_sb_out=(m_w_sb_out, v_w_sb_out), w_conv_out=(m_w_conv_out, v_w_conv_out),
                   w_o=(m_w_o, v_w_o), w_up=(m_w_up, v_w_up), w_down=(m_w_down, v_w_down))
    out_g, out_d, out_m, out_v = {}, {}, {}, {}
    grads = {}

    def reduce_start(group, tag, after):
        blocks = []
        for n in group:
            g = grads[n]
            blocks.append(g if g.ndim == 3 else g.reshape(N_CHIPS, g.shape[0] // N_CHIPS, g.shape[1]))
        theirs = _exchange_halves(blocks, "exchange_halves_" + tag)
        pair_sums = [_add_pair(g, t, where, "pair_sum_" + n) for n, g, t in zip(group, blocks, theirs)]
        return _scatter_start(pair_sums, where if after is None else after, "scatter_start_" + tag)

    def reduce_finish(group, tag, started, after):
        parts, slots = _scatter_wait(*started, after, "scatter_wait_" + tag)
        halves = [_sum_slots(p, t, where, "chip_sum_" + n) for n, p, t in zip(group, parts, slots)]
        for n, red in zip(group, _share_halves(halves, "share_halves_" + tag)):
            mm_, vv_ = moments[n]
            res = _adamw(big[n], red, mm_[0], vv_[0], "adamw_" + n)
            out_g[n], out_d[n], out_m[n], out_v[n] = [r[None] for r in res]
        return res[1]

    group_mlp, group_mix, group_in = ["w_down", "w_up"], ["w_o", "w_sb_out", "w_conv_out"], ["w_in"]
    grads["w_down"] = _mm(f, d_dn, name="mm_dw_down", ta=True, out_dtypes=(BF16,))
    dup = _mm(d_dn, gathered["w_down"], name="mm_df", tb=True, extra=[up], out_dtypes=(BF16,),
              epilogue=lambda acc, upv: (acc * (2.0 * jnp.maximum(upv, 0.0)),))
    grads["w_up"] = _mm(h2, dup, name="mm_dw_up", ta=True, out_groups=N_CHIPS, out_dtypes=(BF16,))
    mlp_started, mlp_token = reduce_start(group_mlp, "mlp", None)
    dh2 = _mm(dup, gathered["w_up"], name="mm_dh2", tb=True, b_groups=N_CHIPS, after=mlp_token)
    dx2, dy, dg_pre_mlp, dg_post_mix = _rms_bwd2(dx3, dh2, x2, y, g_post_mix, g_pre_mlp)
    grads["w_o"] = _mm(merged, dy, name="mm_dw_o", ta=True, out_dtypes=(BF16,))
    dmerged = _mm(dy, gathered["w_o"], name="mm_dmerged", tb=True)
    do_sb, do_cv, dgate_sb, dgate_cv, s_gate_sb, s_gate_cv = _merge_bwd(proj, o_sb, o_cv, dmerged, off_sb, off_cv, gw)
    grads["w_sb_out"] = _mm(attn, do_sb, name="mm_dw_sb", ta=True, out_groups=N_CHIPS, out_dtypes=(BF16,))
    dattn = _mm(do_sb, gathered["w_sb_out"], name="mm_dattn", tb=True, b_groups=N_CHIPS)
    grads["w_conv_out"] = _mm(u3, do_cv, name="mm_dw_cv", ta=True, out_groups=N_CHIPS, out_dtypes=(BF16,))
    du3 = _mm(do_cv, gathered["w_conv_out"], name="mm_du3", tb=True, b_groups=N_CHIPS)
    mix_started, mix_token = reduce_start(group_mix, "mix", None)
    dq, dk, dv, s_q, s_k, s_v = _attn_bwd(proj, dattn, mix_token, n_heads, d_sb)
    du1, dg_conv_ln, db_conv_ln, db_dw = _conv_bwd_norm(u1, du3, g_conv_ln, b_conv_ln)
    dglu_a, dglu_b, s_a, s_b, dtaps = _conv_bwd_taps(du1, u0, proj, taps, off_a)
    dproj = jnp.concatenate([dq, dk, dv, dglu_a, dglu_b, dgate_sb, dgate_cv], axis=1)
    db_in = jnp.concatenate([s_q, s_k, s_v, s_a, s_b, s_gate_sb, s_gate_cv], axis=1)
    grads["w_in"] = _mm(h, dproj, name="mm_dw_in", ta=True, out_groups=N_CHIPS, out_dtypes=(BF16,), tn=768)
    dh = _mm(dproj, gathered["w_in"], name="mm_dh", tb=True, b_groups=N_CHIPS, tk=768)
    grad_x, dg_pre_mix = _rms_bwd1(dx2, dh, xs, g_pre_mix)

    small = ["g_pre_mix", "b_in", "b_dw", "g_conv_ln", "b_conv_ln", "g_post_mix", "g_pre_mlp", "g_post_mlp"]
    small_w = dict(g_pre_mix=g_pre_mix, b_in=b_in, b_dw=b_dw, g_conv_ln=g_conv_ln, b_conv_ln=b_conv_ln,
                   g_post_mix=g_post_mix, g_pre_mlp=g_pre_mlp, g_post_mlp=g_post_mlp)
    small_m = dict(g_pre_mix=m_g_pre_mix, b_in=m_b_in, b_dw=m_b_dw, g_conv_ln=m_g_conv_ln, b_conv_ln=m_b_conv_ln,
                   g_post_mix=m_g_post_mix, g_pre_mlp=m_g_pre_mlp, g_post_mlp=m_g_post_mlp)
    small_v = dict(g_pre_mix=v_g_pre_mix, b_in=v_b_in, b_dw=v_b_dw, g_conv_ln=v_g_conv_ln, b_conv_ln=v_b_conv_ln,
                   g_post_mix=v_g_post_mix, g_pre_mlp=v_g_pre_mlp, g_post_mlp=v_g_post_mlp)
    small_g = dict(g_pre_mix=dg_pre_mix, b_in=db_in, b_dw=db_dw, g_conv_ln=dg_conv_ln, b_conv_ln=db_conv_ln,
                   g_post_mix=dg_post_mix, g_pre_mlp=dg_pre_mlp, g_post_mlp=dg_post_mlp)
    shapes = [small_w[n].shape for n in small]
    tail_shapes = [loss_part.shape, dtaps.shape]
    summed = _all_reduce_small(_pack([small_g[n] for n in small] + [loss_part, dtaps]))
    in_started, in_token = reduce_start(group_in, "in", summed)
    zeros_tail = [jnp.zeros(shp, F32) for shp in tail_shapes]
    res = _adamw(_pack([small_w[n] for n in small] + zeros_tail), summed,
                 _pack([small_m[n] for n in small] + zeros_tail), _pack([small_v[n] for n in small] + zeros_tail),
                 "adamw_small")
    unpacked = [_unpack(r, shapes + tail_shapes) for r in res]
    for i, n in enumerate(small):
        out_g[n], out_d[n], out_m[n], out_v[n] = [u[i] for u in unpacked]
    loss = unpacked[0][len(small)][0, 0]
    taps_grad = lax.dynamic_slice(unpacked[0][len(small) + 1], (0, chip * cs), (CONV_HALO, cs))
    res = _adamw(_pad_rows(w_dw[0], CONV_HALO), taps_grad, _pad_rows(m_w_dw[0], CONV_HALO), _pad_rows(v_w_dw[0], CONV_HALO),
                 "adamw_taps")
    out_g["w_dw"], out_d["w_dw"], out_m["w_dw"], out_v["w_dw"] = [r[:CONV_WIDTH][None] for r in res]
    done = reduce_finish(group_mlp, "mlp", mlp_started, in_token)
    done = reduce_finish(group_mix, "mix", mix_started, done)
    reduce_finish(group_in, "in", in_started, done)

    order = ["g_pre_mix", "w_in", "b_in", "w_dw", "b_dw", "g_conv_ln", "b_conv_ln", "w_sb_out", "w_conv_out", "w_o",
             "g_post_mix", "g_pre_mlp", "w_up", "w_down", "g_post_mlp"]
    return (loss, grad_x[None], *[out_g[n] for n in order], *[out_d[n] for n in order],
            *[out_m[n] for n in order], *[out_v[n] for n in order])
```

```python
import functools
import math

import jax
import jax.numpy as jnp
from jax import lax
from jax.experimental import pallas as pl
from jax.experimental.pallas import tpu as pltpu

F32 = jnp.float32
BF16 = jnp.bfloat16
MESH = pl.DeviceIdType.MESH

HEAD_DIM = 128
CONV_WIDTH = 31
CONV_HALO = 32
EPS = 1e-6
ADAM_LR = 0.001
ADAM_B1 = 0.9
ADAM_B2 = 0.999
ADAM_EPS = 1e-08
ADAM_WD = 0.01
ADAM_STEP = 10
N_CHIPS = 4
VMEM_LIMIT = 48 * 1024 * 1024


def _params(**kw):
    return pltpu.CompilerParams(vmem_limit_bytes=VMEM_LIMIT, **kw)


def _tile(n, want, align=128):
    if n <= want:
        return n
    for t in range(want - want % align, 0, -align):
        if n % t == 0:
            return t
    raise ValueError((n, want, align))


def _mm(a, b, *, name, ta=False, tb=False, b_groups=1, out_groups=1, bias=None, extra=None, epilogue=None,
        out_dtypes=(F32,), tm=1024, tn=1024, tk=512, after=None):
    if ta:
        K, M = a.shape
    else:
        M, K = a.shape
    if b_groups == 1:
        br, bc = b.shape
    else:
        _, br, bcg = b.shape
        bc = bcg * b_groups
    if tb:
        N, Kb = br, bc
    else:
        Kb, N = br, bc
    assert Kb == K, (name, a.shape, b.shape)
    tm, tn, tk = _tile(M, tm), _tile(N, tn), _tile(K, tk)
    if b_groups > 1:
        if tb:
            tk = _tile(K // b_groups, tk)
        else:
            tn = _tile(N // b_groups, tn)
    if out_groups > 1:
        tn = _tile(N // out_groups, tn)
        if b_groups > 1 and not tb:
            tn = _tile(N // b_groups, tn)
    nm, nn, nk = M // tm, N // tn, K // tk
    extra = tuple(extra or ())
    n_extra = len(extra)
    has_bias = bias is not None
    n_out = len(out_dtypes)

    a_spec = pl.BlockSpec((tk, tm), lambda i, j, k: (k, i)) if ta else pl.BlockSpec((tm, tk), lambda i, j, k: (i, k))
    if b_groups == 1:
        b_spec = pl.BlockSpec((tn, tk), lambda i, j, k: (j, k)) if tb else pl.BlockSpec((tk, tn), lambda i, j, k: (k, j))
    elif tb:
        kpg = (K // b_groups) // tk
        b_spec = pl.BlockSpec((None, tn, tk), lambda i, j, k: (k // kpg, j, k % kpg))
    else:
        npg = (N // b_groups) // tn
        b_spec = pl.BlockSpec((None, tk, tn), lambda i, j, k: (j // npg, k, j % npg))
    in_specs = [a_spec, b_spec]
    operands = [a, b]
    if has_bias:
        in_specs.append(pl.BlockSpec((1, tn), lambda i, j, k: (0, j)))
        operands.append(bias)
    for e in extra:
        in_specs.append(pl.BlockSpec((tm, tn), lambda i, j, k: (i, j)))
        operands.append(e)
    n_after = 0 if after is None else 1
    if after is not None:
        in_specs.append(pl.BlockSpec(memory_space=pl.ANY))
        operands.append(after)
    if out_groups == 1:
        o_spec = pl.BlockSpec((tm, tn), lambda i, j, k: (i, j))
        o_shape = (M, N)
    else:
        opg = (N // out_groups) // tn
        o_spec = pl.BlockSpec((None, tm, tn), lambda i, j, k: (j // opg, i, j % opg))
        o_shape = (out_groups, M, N // out_groups)
    dims = (((0 if ta else 1,), (1 if tb else 0,)), ((), ()))

    def body(*refs):
        a_ref, b_ref = refs[0], refs[1]
        pos = 2
        bias_ref = None
        if has_bias:
            bias_ref = refs[pos]
            pos += 1
        extra_refs = refs[pos:pos + n_extra]
        pos += n_extra + n_after
        out_refs = refs[pos:pos + n_out]
        pos += n_out
        acc_ref = refs[pos] if nk > 1 else None

        part = lax.dot_general(a_ref[...].astype(BF16), b_ref[...].astype(BF16), dims, preferred_element_type=F32)

        def finish(acc):
            if has_bias:
                acc = acc + bias_ref[...]
            outs = epilogue(acc, *[e[...] for e in extra_refs]) if epilogue is not None else (acc,)
            for o_ref, o in zip(out_refs, outs):
                o_ref[...] = o.astype(o_ref.dtype)

        if nk == 1:
            finish(part)
        else:
            k = pl.program_id(2)

            @pl.when(k == 0)
            def _():
                acc_ref[...] = part

            @pl.when(k > 0)
            def _():
                acc_ref[...] += part

            @pl.when(k == nk - 1)
            def _():
                finish(acc_ref[...])

    outs = pl.pallas_call(
        body,
        name=name,
        grid=(nm, nn, nk),
        in_specs=in_specs,
        out_specs=[o_spec] * n_out,
        out_shape=[jax.ShapeDtypeStruct(o_shape, dt) for dt in out_dtypes],
        scratch_shapes=[pltpu.VMEM((tm, tn), F32)] if nk > 1 else [],
        compiler_params=_params(dimension_semantics=("parallel", "parallel", "arbitrary")),
    )(*operands)
    return outs[0] if n_out == 1 else outs


def _rms_stats(x):
    return lax.rsqrt(jnp.mean(x * x, axis=-1, keepdims=True) + EPS)


def _rms_bwd(x, r, g, dy):
    gy = dy * g
    return r * gy - x * (r * r * r) * jnp.mean(x * gy, axis=-1, keepdims=True)


def _sigmoid(x):
    return 1.0 / (1.0 + jnp.exp(-x))


def _row_call(body, *, name, rows, tr, ins, outs, acc_outs=()):
    n = rows // tr
    in_specs = []
    for arr, kind in ins:
        if kind == "row":
            in_specs.append(pl.BlockSpec((tr, arr.shape[1]), lambda i: (i, 0)))
        else:
            in_specs.append(pl.BlockSpec(arr.shape, lambda i: (0, 0)))
    out_specs = [pl.BlockSpec((tr, w), lambda i: (i, 0)) for w, _ in outs]
    out_shape = [jax.ShapeDtypeStruct((rows, w), dt) for w, dt in outs]
    out_specs += [pl.BlockSpec((1, w), lambda i: (0, 0)) for w in acc_outs]
    out_shape += [jax.ShapeDtypeStruct((1, w), F32) for w in acc_outs]
    return pl.pallas_call(
        body, name=name, grid=(n,), in_specs=in_specs, out_specs=out_specs, out_shape=out_shape,
        compiler_params=_params(dimension_semantics=("arbitrary",)),
    )(*[a for a, _ in ins])


def _accumulate(ref, val):
    @pl.when(pl.program_id(0) == 0)
    def _():
        ref[...] = val

    @pl.when(pl.program_id(0) > 0)
    def _():
        ref[...] += val


def _rms_fwd(x, g, tr=256):
    def body(x_ref, g_ref, h_ref):
        xv = x_ref[...]
        h_ref[...] = (xv * _rms_stats(xv) * g_ref[...]).astype(BF16)

    (h,) = _row_call(body, name="rms_fwd", rows=x.shape[0], tr=tr, ins=[(x, "row"), (g, "vec")], outs=[(x.shape[1], BF16)])
    return h


def _resid_rms(x, y, g2, g3, tr=256):
    def body(x_ref, y_ref, g2_ref, g3_ref, x2_ref, h2_ref):
        yv = y_ref[...]
        x2 = x_ref[...] + yv * _rms_stats(yv) * g2_ref[...]
        x2_ref[...] = x2
        h2_ref[...] = (x2 * _rms_stats(x2) * g3_ref[...]).astype(BF16)

    d = x.shape[1]
    return _row_call(body, name="resid_rms", rows=x.shape[0], tr=tr,
                     ins=[(x, "row"), (y, "row"), (g2, "vec"), (g3, "vec")], outs=[(d, F32), (d, BF16)])


def _final(x2, dn, g4, tgt, tr=256):
    d = x2.shape[1]

    def body(x2_ref, dn_ref, g4_ref, t_ref, dx3_ref, ddn_ref, dg4_ref, loss_ref):
        dn_v = dn_ref[...]
        r = _rms_stats(dn_v)
        g = g4_ref[...]
        e = x2_ref[...] + dn_v * r * g - t_ref[...]
        dx3 = e * (1.0 / d)
        dx3_ref[...] = dx3
        ddn_ref[...] = _rms_bwd(dn_v, r, g, dx3).astype(BF16)
        _accumulate(dg4_ref, jnp.sum(dx3 * dn_v * r, axis=0, keepdims=True))
        part = 0.5 * jnp.sum(jnp.mean(e * e, axis=-1, keepdims=True), axis=0, keepdims=True)
        _accumulate(loss_ref, jnp.broadcast_to(part, loss_ref.shape))

    return _row_call(body, name="final", rows=x2.shape[0], tr=tr,
                     ins=[(x2, "row"), (dn, "row"), (g4, "vec"), (tgt, "row")],
                     outs=[(d, F32), (d, BF16)], acc_outs=(d, 128))


def _rms_bwd2(dx3, dh2, x2, y, g2, g3, tr=256):
    d = x2.shape[1]

    def body(dx3_ref, dh2_ref, x2_ref, y_ref, g2_ref, g3_ref, dx2_ref, dy_ref, dg3_ref, dg2_ref):
        x2v, dh2v, yv = x2_ref[...], dh2_ref[...], y_ref[...]
        r3 = _rms_stats(x2v)
        dx2 = dx3_ref[...] + _rms_bwd(x2v, r3, g3_ref[...], dh2v)
        dx2_ref[...] = dx2
        _accumulate(dg3_ref, jnp.sum(dh2v * x2v * r3, axis=0, keepdims=True))
        r2 = _rms_stats(yv)
        dy_ref[...] = _rms_bwd(yv, r2, g2_ref[...], dx2).astype(BF16)
        _accumulate(dg2_ref, jnp.sum(dx2 * yv * r2, axis=0, keepdims=True))

    return _row_call(body, name="rms_bwd2", rows=x2.shape[0], tr=tr,
                     ins=[(dx3, "row"), (dh2, "row"), (x2, "row"), (y, "row"), (g2, "vec"), (g3, "vec")],
                     outs=[(d, F32), (d, BF16)], acc_outs=(d, d))


def _rms_bwd1(dx2, dh, x, g1, tr=256):
    d = x.shape[1]

    def body(dx2_ref, dh_ref, x_ref, g1_ref, gx_ref, dg1_ref):
        xv, dhv = x_ref[...], dh_ref[...]
        r = _rms_stats(xv)
        gx_ref[...] = dx2_ref[...] + _rms_bwd(xv, r, g1_ref[...], dhv)
        _accumulate(dg1_ref, jnp.sum(dhv * xv * r, axis=0, keepdims=True))

    return _row_call(body, name="rms_bwd1", rows=x.shape[0], tr=tr,
                     ins=[(dx2, "row"), (dh, "row"), (x, "row"), (g1, "vec")], outs=[(d, F32)], acc_outs=(d,))


def _merge_fwd(proj, o_sb, o_cv, off_sb, off_cv, gw, tr=256):
    s, d = o_sb.shape
    nj = d // gw
    b_sb, b_cv = off_sb // gw, off_cv // gw

    def body(gs_ref, gc_ref, osb_ref, ocv_ref, m_ref):
        m_ref[...] = (_sigmoid(gs_ref[...]) * osb_ref[...] + _sigmoid(gc_ref[...]) * ocv_ref[...]).astype(BF16)

    blk = lambda i, j: (i, j)
    return pl.pallas_call(
        body, name="merge_fwd", grid=(s // tr, nj),
        in_specs=[pl.BlockSpec((tr, gw), lambda i, j: (i, b_sb + j)), pl.BlockSpec((tr, gw), lambda i, j: (i, b_cv + j)),
                  pl.BlockSpec((tr, gw), blk), pl.BlockSpec((tr, gw), blk)],
        out_specs=pl.BlockSpec((tr, gw), blk), out_shape=jax.ShapeDtypeStruct((s, d), BF16),
        compiler_params=_params(dimension_semantics=("parallel", "parallel")),
    )(proj, proj, o_sb, o_cv)


def _merge_bwd(proj, o_sb, o_cv, dmerged, off_sb, off_cv, gw, tr=256):
    s, d = o_sb.shape
    nj = d // gw
    ni = s // tr
    b_sb, b_cv = off_sb // gw, off_cv // gw

    def body(gs_ref, gc_ref, osb_ref, ocv_ref, dm_ref, dosb_ref, docv_ref, dgs_ref, dgc_ref, sgs_ref, sgc_ref):
        dm = dm_ref[...]
        s_sb, s_cv = _sigmoid(gs_ref[...]), _sigmoid(gc_ref[...])
        dosb_ref[...] = (dm * s_sb).astype(BF16)
        docv_ref[...] = (dm * s_cv).astype(BF16)
        dgs = dm * osb_ref[...] * s_sb * (1.0 - s_sb)
        dgc = dm * ocv_ref[...] * s_cv * (1.0 - s_cv)
        dgs_ref[...] = dgs.astype(BF16)
        dgc_ref[...] = dgc.astype(BF16)
        i = pl.program_id(1)
        for ref, val in ((sgs_ref, dgs), (sgc_ref, dgc)):
            col = jnp.sum(val, axis=0, keepdims=True)

            @pl.when(i == 0)
            def _():
                ref[...] = col

            @pl.when(i > 0)
            def _():
                ref[...] += col

    blk = lambda j, i: (i, j)
    outs = pl.pallas_call(
        body, name="merge_bwd", grid=(nj, ni),
        in_specs=[pl.BlockSpec((tr, gw), lambda j, i: (i, b_sb + j)), pl.BlockSpec((tr, gw), lambda j, i: (i, b_cv + j)),
                  pl.BlockSpec((tr, gw), blk), pl.BlockSpec((tr, gw), blk), pl.BlockSpec((tr, gw), blk)],
        out_specs=[pl.BlockSpec((tr, gw), blk), pl.BlockSpec((tr, gw), blk),
                   pl.BlockSpec((tr, gw), blk), pl.BlockSpec((tr, gw), blk),
                   pl.BlockSpec((1, gw), lambda j, i: (0, j)), pl.BlockSpec((1, gw), lambda j, i: (0, j))],
        out_shape=[jax.ShapeDtypeStruct((s, d), BF16)] * 4 + [jax.ShapeDtypeStruct((1, d), F32)] * 2,
        compiler_params=_params(dimension_semantics=("parallel", "arbitrary")),
    )(proj, proj, o_sb, o_cv, dmerged)
    return outs


def _split_bf16(v):
    hi = v.astype(BF16)
    lo = (v - hi.astype(F32)).astype(BF16)
    return hi, lo


def _dot_nt(a, b):
    return lax.dot_general(a, b, (((1,), (1,)), ((), ())), preferred_element_type=F32)


def _dot_tn(a, b):
    return lax.dot_general(a, b, (((0,), (0,)), ((), ())), preferred_element_type=F32)


def _dot_nn(a, b):
    return lax.dot_general(a, b, (((1,), (0,)), ((), ())), preferred_element_type=F32)


def _sum_right(v, tri):
    hi, lo = _split_bf16(v)
    return _dot_nn(hi, tri) + _dot_nn(lo, tri)


HEADS_PER_STEP = 2


LOG2_E = 1.4426950408889634


def _sb_tile(q, kb, scale, mask):
    z = _dot_nt(q, kb) * (scale * LOG2_E)
    log_b = jnp.minimum(z, 0.0) - jnp.log2(1.0 + jnp.exp2(-jnp.abs(z)))
    log_1m = log_b - z
    if mask is not None:
        log_1m = jnp.where(mask, log_1m, 0.0)
    return log_b, log_1m


def _tile_mask(tq):
    row = lax.broadcasted_iota(jnp.int32, (tq, tq), 0)
    col = lax.broadcasted_iota(jnp.int32, (tq, tq), 1)
    return col < row


def _tri(tq, before):
    r = lax.broadcasted_iota(jnp.int32, (tq, tq), 0)
    c = lax.broadcasted_iota(jnp.int32, (tq, tq), 1)
    return jnp.where((r < c) if before else (r > c), 1.0, 0.0).astype(BF16)


def _attn_fwd(proj, n_heads, d_sb, tq=256):
    s = proj.shape[0]
    tq = _tile(s, tq)
    hb = d_sb // HEAD_DIM
    scale = 1.0 / math.sqrt(HEAD_DIM)

    heads = [pl.ds(n * HEAD_DIM, HEAD_DIM) for n in range(HEADS_PER_STEP)]
    wide = HEADS_PER_STEP * HEAD_DIM
    hb //= HEADS_PER_STEP

    def body(q_ref, k_ref, v_ref, o_ref):
        i = pl.program_id(1)
        tri = _tri(tq, False)

        def tile(j, carry, diagonal):
            rows = pl.ds(pl.multiple_of(j * tq, tq), tq)
            mask = _tile_mask(tq) if diagonal else None
            out = []
            for hd, (c_l, acc) in zip(heads, carry):
                log_b, log_1m = _sb_tile(q_ref[:, hd].astype(BF16), k_ref[rows, hd].astype(BF16), scale, mask)
                a = jnp.exp2(log_b + (c_l + _sum_right(log_1m, tri)))
                if diagonal:
                    a = jnp.where(mask, a, 0.0)
                out.append((c_l + jnp.sum(log_1m, axis=1, keepdims=True),
                            acc + _dot_nn(a.astype(BF16), v_ref[rows, hd].astype(BF16))))
            return tuple(out)

        init = tuple((jnp.zeros((tq, 1), F32), jnp.zeros((tq, HEAD_DIM), F32)) for _ in heads)
        carry = lax.fori_loop(0, i, lambda jj, c: tile(i - 1 - jj, c, False), tile(i, init, True))
        for hd, (_, acc) in zip(heads, carry):
            o_ref[:, hd] = acc.astype(BF16)

    return pl.pallas_call(
        body, name="attn_fwd", grid=(n_heads // HEADS_PER_STEP, s // tq),
        in_specs=[pl.BlockSpec((tq, wide), lambda h, i: (i, h)),
                  pl.BlockSpec((s, wide), lambda h, i: (0, hb + h)),
                  pl.BlockSpec((s, wide), lambda h, i: (0, 2 * hb + h))],
        out_specs=pl.BlockSpec((tq, wide), lambda h, i: (i, h)),
        out_shape=jax.ShapeDtypeStruct((s, d_sb), BF16),
        compiler_params=_params(dimension_semantics=("parallel", "arbitrary")),
    )(proj, proj, proj)


def _attn_bwd(proj, dattn, after, n_heads, d_sb, tq=256):
    s = proj.shape[0]
    tq = _tile(s, tq)
    nq = s // tq
    hb = d_sb // HEAD_DIM // HEADS_PER_STEP
    scale = 1.0 / math.sqrt(HEAD_DIM)
    heads = [pl.ds(n * HEAD_DIM, HEAD_DIM) for n in range(HEADS_PER_STEP)]
    wide = HEADS_PER_STEP * HEAD_DIM

    def body(q_ref, k_ref, v_ref, do_ref, _, dq_ref, dk_ref, dv_ref, sq_ref, sk_ref, sv_ref, dk_acc, dv_acc, g_st, b_st):
        i = pl.program_id(1)

        @pl.when(i == 0)
        def _():
            dk_acc[...] = jnp.zeros_like(dk_acc)
            dv_acc[...] = jnp.zeros_like(dv_acc)

        tri_after = _tri(tq, False)
        tri_before = _tri(tq, True)

        def newest_first(j, c_ls, diagonal):
            rows = pl.ds(pl.multiple_of(j * tq, tq), tq)
            mask = _tile_mask(tq) if diagonal else None
            out = []
            for n, (hd, c_l) in enumerate(zip(heads, c_ls)):
                do = do_ref[:, hd].astype(BF16)
                log_b, log_1m = _sb_tile(q_ref[:, hd].astype(BF16), k_ref[rows, hd].astype(BF16), scale, mask)
                a = jnp.exp2(log_b + (c_l + _sum_right(log_1m, tri_after)))
                beta = jnp.exp2(log_b)
                if diagonal:
                    a = jnp.where(mask, a, 0.0)
                    beta = jnp.where(mask, beta, 0.0)
                g_st[n, j] = a * _dot_nt(do, v_ref[rows, hd].astype(BF16))
                b_st[n, j] = beta
                dv_acc[rows, hd] += _dot_tn(a.astype(BF16), do)
                out.append(c_l + jnp.sum(log_1m, axis=1, keepdims=True))
            return tuple(out)

        lax.fori_loop(0, i, lambda jj, c: newest_first(i - 1 - jj, c, False),
                      newest_first(i, tuple(jnp.zeros((tq, 1), F32) for _ in heads), True))

        def oldest_first(j, carry):
            rows = pl.ds(pl.multiple_of(j * tq, tq), tq)
            out = []
            for n, (hd, (c_g, dq)) in enumerate(zip(heads, carry)):
                kb = k_ref[rows, hd].astype(BF16)
                g = g_st[n, j]
                beta = b_st[n, j]
                g_before = c_g + _sum_right(g, tri_before)
                dz16 = ((g * (1.0 - beta) - g_before * beta) * scale).astype(BF16)
                dk_acc[rows, hd] += _dot_tn(dz16, q_ref[:, hd].astype(BF16))
                out.append((c_g + jnp.sum(g, axis=1, keepdims=True), dq + _dot_nn(dz16, kb)))
            return tuple(out)

        init = tuple((jnp.zeros((tq, 1), F32), jnp.zeros((tq, HEAD_DIM), F32)) for _ in heads)
        carry = lax.fori_loop(0, i + 1, oldest_first, init)
        dq = jnp.concatenate([c[1] for c in carry], axis=1)
        dq_ref[...] = dq.astype(BF16)
        col = jnp.sum(dq, axis=0, keepdims=True)

        @pl.when(i == 0)
        def _():
            sq_ref[...] = col

        @pl.when(i > 0)
        def _():
            sq_ref[...] += col

        @pl.when(i == nq - 1)
        def _():
            dk = dk_acc[...]
            dv = dv_acc[...]
            dk_ref[...] = dk.astype(BF16)
            dv_ref[...] = dv.astype(BF16)
            sk_ref[...] = jnp.sum(dk, axis=0, keepdims=True)
            sv_ref[...] = jnp.sum(dv, axis=0, keepdims=True)

    qblk = pl.BlockSpec((tq, wide), lambda h, i: (i, h))
    kblk = pl.BlockSpec((s, wide), lambda h, i: (0, h))
    col_sum = pl.BlockSpec((1, wide), lambda h, i: (0, h))
    stash = pltpu.VMEM((HEADS_PER_STEP, nq, tq, tq), F32)
    outs = pl.pallas_call(
        body, name="attn_bwd", grid=(n_heads // HEADS_PER_STEP, nq),
        in_specs=[qblk,
                  pl.BlockSpec((s, wide), lambda h, i: (0, hb + h)),
                  pl.BlockSpec((s, wide), lambda h, i: (0, 2 * hb + h)),
                  qblk, pl.BlockSpec(memory_space=pl.ANY)],
        out_specs=[qblk, kblk, kblk, col_sum, col_sum, col_sum],
        out_shape=[jax.ShapeDtypeStruct((s, d_sb), BF16)] * 3 + [jax.ShapeDtypeStruct((1, d_sb), F32)] * 3,
        scratch_shapes=[pltpu.VMEM((s, wide), F32), pltpu.VMEM((s, wide), F32), stash, stash],
        compiler_params=_params(dimension_semantics=("parallel", "arbitrary")),
    )(proj, proj, proj, dattn, after)
    return outs


LANES = 128


def _glu(proj, off_a, dc, tr=256):
    s = proj.shape[0]
    ba = off_a // dc

    def body(a_ref, b_ref, u_ref):
        u_ref[...] = a_ref[...] * _sigmoid(b_ref[...])

    return pl.pallas_call(
        body, name="glu", grid=(s // tr,),
        in_specs=[pl.BlockSpec((tr, dc), lambda i: (i, ba)), pl.BlockSpec((tr, dc), lambda i: (i, ba + 1))],
        out_specs=pl.BlockSpec((tr, dc), lambda i: (i, 0)), out_shape=jax.ShapeDtypeStruct((s, dc), F32),
        compiler_params=_params(dimension_semantics=("parallel",)),
    )(proj, proj)


def _conv_fwd(u0, taps, b_dw, g_ln, b_ln, tt=256):
    s, dc = u0.shape
    tt = _tile(s, tt)
    hpb = tt // CONV_HALO

    def body(cur_ref, halo_ref, w_ref, bdw_ref, g_ref, b_ref, u1_ref, u3_ref, xs):
        i = pl.program_id(0)
        xs[pl.ds(0, CONV_HALO), :] = jnp.where(i > 0, halo_ref[...], 0.0)
        xs[pl.ds(CONV_HALO, tt), :] = cur_ref[...]
        for c0 in range(0, dc, LANES):
            cols = pl.ds(c0, LANES)
            acc = jnp.broadcast_to(bdw_ref[:, cols], (tt, LANES))
            for k in range(CONV_WIDTH):
                acc = acc + w_ref[pl.ds(k, 1), cols] * xs[pl.ds(CONV_HALO - (CONV_WIDTH - 1) + k, tt), cols]
            u1_ref[:, cols] = acc
        u1 = u1_ref[...]
        mu = jnp.mean(u1, axis=-1, keepdims=True)
        xc = u1 - mu
        rstd = lax.rsqrt(jnp.mean(xc * xc, axis=-1, keepdims=True) + EPS)
        u2 = xc * rstd * g_ref[...] + b_ref[...]
        u3_ref[...] = (u2 * _sigmoid(u2)).astype(BF16)

    vec = pl.BlockSpec((1, dc), lambda i: (0, 0))
    return pl.pallas_call(
        body, name="conv_fwd", grid=(s // tt,),
        in_specs=[pl.BlockSpec((tt, dc), lambda i: (i, 0)),
                  pl.BlockSpec((CONV_HALO, dc), lambda i: (jnp.maximum(i * hpb - 1, 0), 0)),
                  pl.BlockSpec((CONV_HALO, dc), lambda i: (0, 0)), vec, vec, vec],
        out_specs=[pl.BlockSpec((tt, dc), lambda i: (i, 0)), pl.BlockSpec((tt, dc), lambda i: (i, 0))],
        out_shape=[jax.ShapeDtypeStruct((s, dc), F32), jax.ShapeDtypeStruct((s, dc), BF16)],
        scratch_shapes=[pltpu.VMEM((tt + CONV_HALO, dc), F32)],
        compiler_params=_params(dimension_semantics=("arbitrary",)),
    )(u0, u0, taps, b_dw, g_ln, b_ln)


def _conv_bwd_norm(u1, du3, g_ln, b_ln, tr=256):
    dc = u1.shape[1]

    def body(u1_ref, du3_ref, g_ref, b_ref, du1_ref, dg_ref, db_ref, dbdw_ref):
        u1v = u1_ref[...]
        g = g_ref[...]
        mu = jnp.mean(u1v, axis=-1, keepdims=True)
        xc = u1v - mu
        rstd = lax.rsqrt(jnp.mean(xc * xc, axis=-1, keepdims=True) + EPS)
        xhat = xc * rstd
        u2 = xhat * g + b_ref[...]
        sg = _sigmoid(u2)
        du2 = du3_ref[...] * (sg * (1.0 + u2 * (1.0 - sg)))
        _accumulate(dg_ref, jnp.sum(du2 * xhat, axis=0, keepdims=True))
        _accumulate(db_ref, jnp.sum(du2, axis=0, keepdims=True))
        gy = du2 * g
        du1 = rstd * (gy - jnp.mean(gy, axis=-1, keepdims=True) - xhat * jnp.mean(gy * xhat, axis=-1, keepdims=True))
        du1_ref[...] = du1
        _accumulate(dbdw_ref, jnp.sum(du1, axis=0, keepdims=True))

    return _row_call(body, name="conv_bwd_norm", rows=u1.shape[0], tr=tr,
                     ins=[(u1, "row"), (du3, "row"), (g_ln, "vec"), (b_ln, "vec")],
                     outs=[(dc, F32)], acc_outs=(dc, dc, dc))


def _conv_bwd_taps(du1, u0, proj, taps, off_a, tt=256):
    s, dc = u0.shape
    tt = _tile(s, tt)
    n = s // tt
    hpb = tt // CONV_HALO
    ba = off_a // dc
    lead = CONV_HALO - (CONV_WIDTH - 1)

    def body(d_ref, dnext_ref, cur_ref, halo_ref, a_ref, b_ref, w_ref, da_ref, db_ref, sa_ref, sb_ref, dw_ref,
             xs, ds, du0):
        i = pl.program_id(0)
        xs[pl.ds(0, CONV_HALO), :] = jnp.where(i > 0, halo_ref[...], 0.0)
        xs[pl.ds(CONV_HALO, tt), :] = cur_ref[...]
        ds[pl.ds(0, tt), :] = d_ref[...]
        ds[pl.ds(tt, CONV_HALO), :] = jnp.where(i < n - 1, dnext_ref[...], 0.0)

        @pl.when(i == 0)
        def _():
            dw_ref[...] = jnp.zeros_like(dw_ref)

        for c0 in range(0, dc, LANES):
            cols = pl.ds(c0, LANES)
            d_cur = ds[pl.ds(0, tt), cols]
            acc = jnp.zeros((tt, LANES), F32)
            for k in range(CONV_WIDTH):
                acc = acc + w_ref[pl.ds(k, 1), cols] * ds[pl.ds(CONV_WIDTH - 1 - k, tt), cols]
                dw_ref[pl.ds(k, 1), cols] += jnp.sum(d_cur * xs[pl.ds(lead + k, tt), cols], axis=0, keepdims=True)
            du0[:, cols] = acc
        d0 = du0[...]
        sg = _sigmoid(b_ref[...])
        da = d0 * sg
        db = d0 * a_ref[...] * sg * (1.0 - sg)
        da_ref[...] = da.astype(BF16)
        db_ref[...] = db.astype(BF16)
        _accumulate(sa_ref, jnp.sum(da, axis=0, keepdims=True))
        _accumulate(sb_ref, jnp.sum(db, axis=0, keepdims=True))

    row = pl.BlockSpec((tt, dc), lambda i: (i, 0))
    vec = pl.BlockSpec((1, dc), lambda i: (0, 0))
    taps_spec = pl.BlockSpec((CONV_HALO, dc), lambda i: (0, 0))
    return pl.pallas_call(
        body, name="conv_bwd_taps", grid=(n,),
        in_specs=[row, pl.BlockSpec((CONV_HALO, dc), lambda i: (jnp.minimum((i + 1) * hpb, n * hpb - 1), 0)),
                  row, pl.BlockSpec((CONV_HALO, dc), lambda i: (jnp.maximum(i * hpb - 1, 0), 0)),
                  pl.BlockSpec((tt, dc), lambda i: (i, ba)), pl.BlockSpec((tt, dc), lambda i: (i, ba + 1)), taps_spec],
        out_specs=[row, row, vec, vec, taps_spec],
        out_shape=[jax.ShapeDtypeStruct((s, dc), BF16), jax.ShapeDtypeStruct((s, dc), BF16),
                   jax.ShapeDtypeStruct((1, dc), F32), jax.ShapeDtypeStruct((1, dc), F32),
                   jax.ShapeDtypeStruct((CONV_HALO, dc), F32)],
        scratch_shapes=[pltpu.VMEM((tt + CONV_HALO, dc), F32), pltpu.VMEM((tt + CONV_HALO, dc), F32),
                        pltpu.VMEM((tt, dc), F32)],
        compiler_params=_params(dimension_semantics=("arbitrary",)),
    )(du1, du1, u0, u0, proj, proj, taps)


def _elementwise(body, *, name, ins, out_dtypes, tr=128):
    rows, cols = ins[0].shape
    tr = _tile(rows, tr, 8)
    spec = pl.BlockSpec((tr, cols), lambda i: (i, 0))
    return pl.pallas_call(
        body, name=name, grid=(rows // tr,), in_specs=[spec] * len(ins), out_specs=[spec] * len(out_dtypes),
        out_shape=[jax.ShapeDtypeStruct((rows, cols), dt) for dt in out_dtypes],
        compiler_params=_params(dimension_semantics=("parallel",)),
    )(*ins)


def _prefetched(body, *, name, where, grid, in_specs, out_specs, out_shape, operands):
    return pl.pallas_call(
        body, name=name, out_shape=out_shape,
        grid_spec=pltpu.PrefetchScalarGridSpec(num_scalar_prefetch=1, grid=grid, in_specs=in_specs, out_specs=out_specs),
        compiler_params=_params(dimension_semantics=("parallel",) * len(grid)),
    )(where, *operands)


def _cast_into_block(w, where, after, name, tr=256):
    rows, cols = w.shape
    tr = _tile(rows, tr, 16)

    def body(where_ref, w_ref, _, o_ref):
        o_ref[...] = w_ref[...].astype(BF16)

    return _prefetched(
        body, name=name, where=where, grid=(rows // tr,),
        in_specs=[pl.BlockSpec((tr, cols), lambda i, wh: (i, 0)), pl.BlockSpec(memory_space=pl.ANY)],
        out_specs=pl.BlockSpec((None, tr, cols), lambda i, wh: (wh[0], i, 0)),
        out_shape=jax.ShapeDtypeStruct((N_CHIPS, rows, cols), BF16), operands=[w, after])


def _add_pair(grad, theirs, where, name, tr=256):
    nb, half, cols = theirs.shape
    tr = _tile(half, tr, 16)
    nh = half // tr

    def body(where_ref, a_ref, b_ref, o_ref):
        o_ref[...] = (a_ref[...].astype(F32) + b_ref[...].astype(F32)).astype(BF16)

    blk = pl.BlockSpec((None, tr, cols), lambda b, i, wh: (b, i, 0))
    return _prefetched(
        body, name=name, where=where, grid=(nb, nh),
        in_specs=[pl.BlockSpec((None, tr, cols), lambda b, i, wh: (b, wh[1] * nh + i, 0)), blk],
        out_specs=blk, out_shape=jax.ShapeDtypeStruct(theirs.shape, BF16), operands=[grad, theirs])


def _sum_slots(part, slots, where, name, tr=256):
    _, half, cols = part.shape
    tr = _tile(half, tr, 16)
    nh = half // tr

    def body(where_ref, p_ref, s_ref, o_ref):
        o_ref[...] = ((p_ref[...].astype(F32) + s_ref[0].astype(F32)) + s_ref[1].astype(F32)) + s_ref[2].astype(F32)

    return _prefetched(
        body, name=name, where=where, grid=(nh,),
        in_specs=[pl.BlockSpec((None, tr, cols), lambda i, wh: (wh[0], i, 0)),
                  pl.BlockSpec((3, tr, cols), lambda i, wh: (0, i, 0))],
        out_specs=pl.BlockSpec((tr, cols), lambda i, wh: (wh[1] * nh + i, 0)),
        out_shape=jax.ShapeDtypeStruct((2 * half, cols), F32), operands=[part, slots])


def _adamw(w, g, m, v, name):
    bc1 = 1.0 - ADAM_B1 ** ADAM_STEP
    bc2 = 1.0 - ADAM_B2 ** ADAM_STEP

    def body(w_ref, g_ref, m_ref, v_ref, go_ref, d_ref, mo_ref, vo_ref):
        g = g_ref[...]
        m_new = ADAM_B1 * m_ref[...] + (1.0 - ADAM_B1) * g
        v_new = ADAM_B2 * v_ref[...] + (1.0 - ADAM_B2) * (g * g)
        m_hat = m_new / bc1
        v_hat = v_new / bc2
        go_ref[...] = g
        d_ref[...] = -ADAM_LR * (m_hat / (jnp.sqrt(v_hat) + ADAM_EPS) + ADAM_WD * w_ref[...])
        mo_ref[...] = m_new
        vo_ref[...] = v_new

    return _elementwise(body, name=name, ins=[w, g, m, v], out_dtypes=[F32] * 4, tr=128)


ANY = pl.BlockSpec(memory_space=pl.ANY)


def _mesh_pos():
    return lax.axis_index("x"), lax.axis_index("y"), lax.axis_index("c")


def _other_chips(x, y):
    return [(1 - x, y), (x, 1 - y), (1 - x, 1 - y)]


def _remote(src, dst, send_sems, recv_sems, idx, to):
    return pltpu.make_async_remote_copy(src_ref=src, dst_ref=dst, send_sem=send_sems.at[idx], recv_sem=recv_sems.at[idx],
                                        device_id=to, device_id_type=MESH)


HBM = pl.BlockSpec(memory_space=pltpu.HBM)
SEM = pl.BlockSpec(memory_space=pltpu.SEMAPHORE)
EFFECT = pltpu.SideEffectType.DATAFLOW_SIDE_EFFECTING


def _in_hbm(arrays):
    return [pltpu.with_memory_space_constraint(a, pltpu.HBM) for a in arrays]


def _ici_copies(slab, send_sems, recv_sems, x, y, c):
    half = slab.shape[1] // 2
    rows = pl.ds(c * half, half)
    mine = slab.at[2 * x + y, rows]
    out = []
    for j, (cx, cy) in enumerate(_other_chips(x, y)):
        got = slab.at[2 * cx + cy, rows]
        out.append((_remote(mine, mine, send_sems, recv_sems, j, (cx, cy, c)),
                    _remote(got, got, send_sems, recv_sems, j, (cx, cy, c))))
    return out


def _gather_start(slabs, after, name):
    n = len(slabs)

    def body(*refs):
        ins = refs[:n]
        send_sems, recv_sems, token = refs[2 * n + 1:3 * n + 1], refs[3 * n + 1:4 * n + 1], refs[4 * n + 1]
        x, y, c = _mesh_pos()
        for w in range(n):
            for send, _ in _ici_copies(ins[w], send_sems[w], recv_sems[w], x, y, c):
                send.start()
        token[...] = jnp.zeros_like(token)

    outs = pl.pallas_call(
        body, name=name, in_specs=[HBM] * n + [ANY],
        out_specs=[HBM] * n + [SEM] * (2 * n) + [pl.BlockSpec(memory_space=pltpu.VMEM)],
        out_shape=[pltpu.HBM(s.shape, s.dtype) for s in slabs] + [pltpu.SemaphoreType.DMA((3,))] * (2 * n)
        + [jax.ShapeDtypeStruct((8, 128), F32)],
        input_output_aliases={w: w for w in range(n)},
        compiler_params=_params(has_side_effects=EFFECT),
    )(*_in_hbm(slabs), after)
    return (outs[:n], outs[n:2 * n], outs[2 * n:3 * n]), outs[3 * n]


def _gather_wait(slabs, send_sems, recv_sems, after, name):
    n = len(slabs)

    def body(*refs):
        ins = refs[:n]
        sends, recvs = refs[n:2 * n], refs[2 * n:3 * n]
        x, y, c = _mesh_pos()
        for w in range(n):
            for send, recv in _ici_copies(ins[w], sends[w], recvs[w], x, y, c):
                send.wait_send()
                recv.wait_recv()

    return pl.pallas_call(
        body, name=name, in_specs=[HBM] * n + [SEM] * (2 * n) + [ANY], out_specs=[HBM] * n,
        out_shape=[pltpu.HBM(s.shape, s.dtype) for s in slabs],
        input_output_aliases={w: w for w in range(n)},
        compiler_params=_params(has_side_effects=EFFECT),
    )(*slabs, *send_sems, *recv_sems, after)


def _gather_forward(slabs, name):
    n = len(slabs)

    def body(*refs):
        outs = refs[n:2 * n]
        send_sems, recv_sems = refs[2 * n:]
        x, y, c = _mesh_pos()
        sibling = (x, y, 1 - c)
        chips = _other_chips(x, y)
        started = []
        for w in range(n):
            half = outs[w].shape[1] // 2
            for j, (cx, cy) in enumerate(chips):
                got = outs[w].at[2 * cx + cy, pl.ds(c * half, half)]
                fw = _remote(got, got, send_sems, recv_sems, 3 * w + j, sibling)
                fw.start()
                started.append(fw)
        for w in range(n):
            half = outs[w].shape[1] // 2
            for j, (cx, cy) in enumerate(chips):
                got = outs[w].at[2 * cx + cy, pl.ds((1 - c) * half, half)]
                _remote(got, got, send_sems, recv_sems, 3 * w + j, sibling).wait_recv()
        for cp in started:
            cp.wait_send()

    return pl.pallas_call(
        body, name=name, in_specs=[ANY] * n, out_specs=[ANY] * n,
        out_shape=[jax.ShapeDtypeStruct(s.shape, s.dtype) for s in slabs],
        input_output_aliases={w: w for w in range(n)},
        scratch_shapes=[pltpu.SemaphoreType.DMA((3 * n,)), pltpu.SemaphoreType.DMA((3 * n,))],
        compiler_params=_params(has_side_effects=True),
    )(*slabs)


def _gather_taps(taps):
    def body(in_ref, out_ref, send_sems, recv_sems, local_sem):
        x, y, c = _mesh_pos()
        me = 2 * x + y
        chips = _other_chips(x, y)
        lc = pltpu.make_async_copy(in_ref, out_ref.at[me], local_sem)
        lc.start()
        sends = [_remote(in_ref, out_ref.at[me], send_sems, recv_sems, j, (cx, cy, c)) for j, (cx, cy) in enumerate(chips)]
        for cp in sends:
            cp.start()
        for j, (cx, cy) in enumerate(chips):
            got = out_ref.at[2 * cx + cy]
            _remote(got, got, send_sems, recv_sems, j, (cx, cy, c)).wait_recv()
        for cp in sends:
            cp.wait_send()
        lc.wait()

    vm = pl.BlockSpec(memory_space=pltpu.VMEM)
    return pl.pallas_call(
        body, name="gather_taps", in_specs=[vm], out_specs=vm,
        out_shape=jax.ShapeDtypeStruct((N_CHIPS,) + taps.shape, taps.dtype),
        scratch_shapes=[pltpu.SemaphoreType.DMA((3,)), pltpu.SemaphoreType.DMA((3,)), pltpu.SemaphoreType.DMA],
        compiler_params=_params(has_side_effects=True),
    )(taps)


def _exchange_halves(grads, name):
    n = len(grads)

    def body(*refs):
        ins, their_refs = refs[:n], refs[n:2 * n]
        send_sems, recv_sems = refs[2 * n:]
        x, y, c = _mesh_pos()
        sibling = (x, y, 1 - c)
        sends = []
        for w in range(n):
            half = ins[w].shape[1] // 2
            cp = _remote(ins[w].at[:, pl.ds((1 - c) * half, half), :], their_refs[w], send_sems, recv_sems, w, sibling)
            cp.start()
            sends.append(cp)
        for w in range(n):
            _remote(their_refs[w], their_refs[w], send_sems, recv_sems, w, sibling).wait_recv()
        for cp in sends:
            cp.wait_send()

    halves = [jax.ShapeDtypeStruct((g.shape[0], g.shape[1] // 2, g.shape[2]), g.dtype) for g in grads]
    return pl.pallas_call(
        body, name=name, in_specs=[ANY] * n, out_specs=[ANY] * n, out_shape=halves,
        scratch_shapes=[pltpu.SemaphoreType.DMA((n,)), pltpu.SemaphoreType.DMA((n,))],
        compiler_params=_params(has_side_effects=True),
    )(*grads)


def _scatter_copies(part, slots, send_sems, recv_sems, x, y, c):
    out = []
    for j, (cx, cy) in enumerate(_other_chips(x, y)):
        out.append((_remote(part.at[2 * cx + cy], slots.at[j], send_sems, recv_sems, j, (cx, cy, c)),
                    _remote(slots.at[j], slots.at[j], send_sems, recv_sems, j, (cx, cy, c))))
    return out


def _scatter_start(parts, after, name):
    n = len(parts)
    slots = [lax.empty((3,) + p.shape[1:], p.dtype) for p in parts]

    def body(*refs):
        ins, lands = refs[:n], refs[n:2 * n]
        send_sems, recv_sems, token = refs[4 * n + 1:5 * n + 1], refs[5 * n + 1:6 * n + 1], refs[6 * n + 1]
        x, y, c = _mesh_pos()
        for w in range(n):
            for send, _ in _scatter_copies(ins[w], lands[w], send_sems[w], recv_sems[w], x, y, c):
                send.start()
        token[...] = jnp.zeros_like(token)

    outs = pl.pallas_call(
        body, name=name, in_specs=[HBM] * (2 * n) + [ANY],
        out_specs=[HBM] * (2 * n) + [SEM] * (2 * n) + [pl.BlockSpec(memory_space=pltpu.VMEM)],
        out_shape=[pltpu.HBM(a.shape, a.dtype) for a in list(parts) + slots] + [pltpu.SemaphoreType.DMA((3,))] * (2 * n)
        + [jax.ShapeDtypeStruct((8, 128), F32)],
        input_output_aliases={k: k for k in range(2 * n)},
        compiler_params=_params(has_side_effects=EFFECT),
    )(*_in_hbm(list(parts) + slots), after)
    return (outs[:n], outs[n:2 * n], outs[2 * n:3 * n], outs[3 * n:4 * n]), outs[4 * n]


def _scatter_wait(parts, slots, send_sems, recv_sems, after, name):
    n = len(parts)

    def body(*refs):
        ins, lands = refs[:n], refs[n:2 * n]
        sends, recvs = refs[2 * n:3 * n], refs[3 * n:4 * n]
        x, y, c = _mesh_pos()
        for w in range(n):
            for send, recv in _scatter_copies(ins[w], lands[w], sends[w], recvs[w], x, y, c):
                send.wait_send()
                recv.wait_recv()

    outs = pl.pallas_call(
        body, name=name, in_specs=[HBM] * (2 * n) + [SEM] * (2 * n) + [ANY], out_specs=[HBM] * (2 * n),
        out_shape=[pltpu.HBM(a.shape, a.dtype) for a in list(parts) + list(slots)],
        input_output_aliases={k: k for k in range(2 * n)},
        compiler_params=_params(has_side_effects=EFFECT),
    )(*parts, *slots, *send_sems, *recv_sems, after)
    return outs[:n], outs[n:]


def _share_halves(halves, name):
    n = len(halves)

    def body(*refs):
        outs = refs[n:2 * n]
        send_sems, recv_sems = refs[2 * n:]
        x, y, c = _mesh_pos()
        sibling = (x, y, 1 - c)
        sends = []
        for w in range(n):
            h = outs[w].shape[0] // 2
            mine = outs[w].at[pl.ds(c * h, h)]
            cp = _remote(mine, mine, send_sems, recv_sems, w, sibling)
            cp.start()
            sends.append(cp)
        for w in range(n):
            h = outs[w].shape[0] // 2
            got = outs[w].at[pl.ds((1 - c) * h, h)]
            _remote(got, got, send_sems, recv_sems, w, sibling).wait_recv()
        for cp in sends:
            cp.wait_send()

    return pl.pallas_call(
        body, name=name, in_specs=[ANY] * n, out_specs=[ANY] * n,
        out_shape=[jax.ShapeDtypeStruct(h.shape, h.dtype) for h in halves],
        input_output_aliases={w: w for w in range(n)},
        scratch_shapes=[pltpu.SemaphoreType.DMA((n,)), pltpu.SemaphoreType.DMA((n,))],
        compiler_params=_params(has_side_effects=True),
    )(*halves)


def _all_reduce_small(buf):
    rows, width = buf.shape

    def body(in_ref, out_ref, gathered, send_sems, recv_sems):
        x, y, c = _mesh_pos()
        me = 4 * x + 2 * y + c
        gathered[me] = in_ref[...]
        flips = [(dx, dy, dz) for dx in (0, 1) for dy in (0, 1) for dz in (0, 1)][1:]
        peers = [((1 - x) if dx else x, (1 - y) if dy else y, (1 - c) if dz else c) for dx, dy, dz in flips]
        sends = [_remote(in_ref, gathered.at[me], send_sems, recv_sems, k, peer) for k, peer in enumerate(peers)]
        for cp in sends:
            cp.start()
        for k, (px, py, pc) in enumerate(peers):
            got = gathered.at[4 * px + 2 * py + pc]
            _remote(got, got, send_sems, recv_sems, k, (px, py, pc)).wait_recv()
        for cp in sends:
            cp.wait_send()
        total = gathered[0]
        for d in range(1, 8):
            total = total + gathered[d]
        out_ref[...] = total

    vm = pl.BlockSpec(memory_space=pltpu.VMEM)
    return pl.pallas_call(
        body, name="all_reduce_small", in_specs=[vm], out_specs=vm, out_shape=jax.ShapeDtypeStruct(buf.shape, F32),
        scratch_shapes=[pltpu.VMEM((8, rows, width), F32), pltpu.SemaphoreType.DMA((7,)), pltpu.SemaphoreType.DMA((7,))],
        compiler_params=_params(has_side_effects=True),
    )(buf)


PACK_W = 2048
PACK_ROWS = 8


def _pack(vectors):
    flat = jnp.concatenate([v.reshape(-1) for v in vectors])
    unit = PACK_W * PACK_ROWS
    total = -(-flat.shape[0] // unit) * unit
    return jnp.pad(flat, (0, total - flat.shape[0])).reshape(total // PACK_W, PACK_W)


def _unpack(buf, shapes):
    flat = buf.reshape(-1)
    out, pos = [], 0
    for shp in shapes:
        size = math.prod(shp)
        out.append(flat[pos:pos + size].reshape(shp))
        pos += size
    return out


def _pad_rows(a, rows):
    return jnp.pad(a, ((0, rows - a.shape[0]), (0, 0)))


def kernel(x, g_pre_mix, w_in, b_in, w_dw, b_dw, g_conv_ln, b_conv_ln, w_sb_out, w_conv_out, w_o, g_post_mix, g_pre_mlp, w_up, w_down, g_post_mlp, loss_target, m_g_pre_mix, m_w_in, m_b_in, m_w_dw, m_b_dw, m_g_conv_ln, m_b_conv_ln, m_w_sb_out, m_w_conv_out, m_w_o, m_g_post_mix, m_g_pre_mlp, m_w_up, m_w_down, m_g_post_mlp, v_g_pre_mix, v_w_in, v_b_in, v_w_dw, v_b_dw, v_g_conv_ln, v_b_conv_ln, v_w_sb_out, v_w_conv_out, v_w_o, v_g_post_mix, v_g_pre_mlp, v_w_up, v_w_down, v_g_post_mlp):
    xs = x[0]
    tgt = loss_target[0]
    s, d = xs.shape
    d_sb = w_sb_out.shape[1]
    dc = w_conv_out.shape[1]
    d_in = w_in.shape[2] * N_CHIPS
    n_heads = d_sb // HEAD_DIM
    off_a = 3 * d_sb
    off_sb = off_a + 2 * dc
    off_cv = off_sb + d
    gw = dc
    assert off_a % dc == 0 and off_sb % gw == 0 and d % gw == 0 and d_in == off_cv + d
    chip = 2 * lax.axis_index("x") + lax.axis_index("y")

    big = dict(w_in=w_in[0], w_sb_out=w_sb_out[0], w_conv_out=w_conv_out[0], w_o=w_o[0], w_up=w_up[0], w_down=w_down[0])
    names = list(big)
    col_sharded = {"w_in", "w_sb_out", "w_conv_out", "w_up"}
    where = jnp.stack([chip, lax.axis_index("c")]).astype(jnp.int32)
    cs = w_dw.shape[2]
    taps4 = _gather_taps(_pad_rows(w_dw[0], CONV_HALO))
    first, token = _gather_start([_cast_into_block(big["w_in"], where, taps4, "cast_w_in")], taps4, "gather_start_in")
    rest, _ = _gather_start([_cast_into_block(big[n], where, token, "cast_" + n) for n in names[1:]], token,
                            "gather_start_rest")
    in_flight = dict(zip(names, zip(*[a + b for a, b in zip(first, rest)])))
    gathered = {}

    def arrive(group, after, tag):
        thru, sends, recvs = zip(*[in_flight[n] for n in group])
        landed = _gather_wait(thru, sends, recvs, after, "gather_wait_" + tag)
        for n, g4 in zip(group, _gather_forward(landed, "gather_forward_" + tag)):
            gathered[n] = g4 if n in col_sharded else g4.reshape(g4.shape[0] * g4.shape[1], g4.shape[2])

    taps = jnp.transpose(taps4, (1, 0, 2)).reshape(CONV_HALO, N_CHIPS * cs)

    h = _rms_fwd(xs, g_pre_mix)
    arrive(["w_in"], h, "in")
    proj = _mm(h, gathered["w_in"], name="mm_proj", b_groups=N_CHIPS, bias=b_in, tn=768)
    arrive(["w_sb_out", "w_conv_out", "w_o"], proj, "mix")
    attn = _attn_fwd(proj, n_heads, d_sb)
    u0 = _glu(proj, off_a, dc)
    u1, u3 = _conv_fwd(u0, taps, b_dw, g_conv_ln, b_conv_ln)
    arrive(["w_up"], u3, "up")
    o_sb = _mm(attn, gathered["w_sb_out"], name="mm_o_sb", b_groups=N_CHIPS)
    o_cv = _mm(u3, gathered["w_conv_out"], name="mm_o_cv", b_groups=N_CHIPS)
    merged = _merge_fwd(proj, o_sb, o_cv, off_sb, off_cv, gw)
    y = _mm(merged, gathered["w_o"], name="mm_y")
    x2, h2 = _resid_rms(xs, y, g_post_mix, g_pre_mlp)
    up, f = _mm(h2, gathered["w_up"], name="mm_up", b_groups=N_CHIPS, out_dtypes=(F32, BF16),
                epilogue=lambda acc: (acc, jnp.square(jnp.maximum(acc, 0.0))))
    arrive(["w_down"], f, "down")
    dn = _mm(f, gathered["w_down"], name="mm_down")
    dx3, d_dn, dg_post_mlp, loss_part = _final(x2, dn, g_post_mlp, tgt)

    moments = dict(w_in=(m_w_in, v_w_in), w_sb_out=(m_w_sb_out, v_w_sb_out), w_conv_out=(m_w_conv_out, v_w_conv_out),
                   w_o=(m_w_o, v_w_o), w_up=(m_w_up, v_w_up), w_down=(m_w_down, v_w_down))
    out_g, out_d, out_m, out_v = {}, {}, {}, {}
    grads = {}

    def reduce_start(group, tag, after):
        blocks = []
        for n in group:
            g = grads[n]
            blocks.append(g if g.ndim == 3 else g.reshape(N_CHIPS, g.shape[0] // N_CHIPS, g.shape[1]))
        theirs = _exchange_halves(blocks, "exchange_halves_" + tag)
        pair_sums = [_add_pair(g, t, where, "pair_sum_" + n) for n, g, t in zip(group, blocks, theirs)]
        return _scatter_start(pair_sums, where if after is None else after, "scatter_start_" + tag)

    def reduce_finish(group, tag, started, after):
        parts, slots = _scatter_wait(*started, after, "scatter_wait_" + tag)
        halves = [_sum_slots(p, t, where, "chip_sum_" + n) for n, p, t in zip(group, parts, slots)]
        for n, red in zip(group, _share_halves(halves, "share_halves_" + tag)):
            mm_, vv_ = moments[n]
            res = _adamw(big[n], red, mm_[0], vv_[0], "adamw_" + n)
            out_g[n], out_d[n], out_m[n], out_v[n] = [r[None] for r in res]
        return res[1]

    group_mlp, group_mix, group_in = ["w_down", "w_up"], ["w_o", "w_sb_out", "w_conv_out"], ["w_in"]
    grads["w_down"] = _mm(f, d_dn, name="mm_dw_down", ta=True, out_dtypes=(BF16,))
    dup = _mm(d_dn, gathered["w_down"], name="mm_df", tb=True, extra=[up], out_dtypes=(BF16,),
              epilogue=lambda acc, upv: (acc * (2.0 * jnp.maximum(upv, 0.0)),))
    grads["w_up"] = _mm(h2, dup, name="mm_dw_up", ta=True, out_groups=N_CHIPS, out_dtypes=(BF16,))
    mlp_started, mlp_token = reduce_start(group_mlp, "mlp", None)
    dh2 = _mm(dup, gathered["w_up"], name="mm_dh2", tb=True, b_groups=N_CHIPS, after=mlp_token)
    dx2, dy, dg_pre_mlp, dg_post_mix = _rms_bwd2(dx3, dh2, x2, y, g_post_mix, g_pre_mlp)
    grads["w_o"] = _mm(merged, dy, name="mm_dw_o", ta=True, out_dtypes=(BF16,))
    dmerged = _mm(dy, gathered["w_o"], name="mm_dmerged", tb=True)
    do_sb, do_cv, dgate_sb, dgate_cv, s_gate_sb, s_gate_cv = _merge_bwd(proj, o_sb, o_cv, dmerged, off_sb, off_cv, gw)
    grads["w_sb_out"] = _mm(attn, do_sb, name="mm_dw_sb", ta=True, out_groups=N_CHIPS, out_dtypes=(BF16,))
    dattn = _mm(do_sb, gathered["w_sb_out"], name="mm_dattn", tb=True, b_groups=N_CHIPS)
    grads["w_conv_out"] = _mm(u3, do_cv, name="mm_dw_cv", ta=True, out_groups=N_CHIPS, out_dtypes=(BF16,))
    du3 = _mm(do_cv, gathered["w_conv_out"], name="mm_du3", tb=True, b_groups=N_CHIPS)
    mix_started, mix_token = reduce_start(group_mix, "mix", None)
    dq, dk, dv, s_q, s_k, s_v = _attn_bwd(proj, dattn, mix_token, n_heads, d_sb)
    du1, dg_conv_ln, db_conv_ln, db_dw = _conv_bwd_norm(u1, du3, g_conv_ln, b_conv_ln)
    dglu_a, dglu_b, s_a, s_b, dtaps = _conv_bwd_taps(du1, u0, proj, taps, off_a)
    dproj = jnp.concatenate([dq, dk, dv, dglu_a, dglu_b, dgate_sb, dgate_cv], axis=1)
    db_in = jnp.concatenate([s_q, s_k, s_v, s_a, s_b, s_gate_sb, s_gate_cv], axis=1)
    grads["w_in"] = _mm(h, dproj, name="mm_dw_in", ta=True, out_groups=N_CHIPS, out_dtypes=(BF16,), tn=768)
    dh = _mm(dproj, gathered["w_in"], name="mm_dh", tb=True, b_groups=N_CHIPS, tk=768)
    grad_x, dg_pre_mix = _rms_bwd1(dx2, dh, xs, g_pre_mix)

    small = ["g_pre_mix", "b_in", "b_dw", "g_conv_ln", "b_conv_ln", "g_post_mix", "g_pre_mlp", "g_post_mlp"]
    small_w = dict(g_pre_mix=g_pre_mix, b_in=b_in, b_dw=b_dw, g_conv_ln=g_conv_ln, b_conv_ln=b_conv_ln,
                   g_post_mix=g_post_mix, g_pre_mlp=g_pre_mlp, g_post_mlp=g_post_mlp)
    small_m = dict(g_pre_mix=m_g_pre_mix, b_in=m_b_in, b_dw=m_b_dw, g_conv_ln=m_g_conv_ln, b_conv_ln=m_b_conv_ln,
                   g_post_mix=m_g_post_mix, g_pre_mlp=m_g_pre_mlp, g_post_mlp=m_g_post_mlp)
    small_v = dict(g_pre_mix=v_g_pre_mix, b_in=v_b_in, b_dw=v_b_dw, g_conv_ln=v_g_conv_ln, b_conv_ln=v_b_conv_ln,
                   g_post_mix=v_g_post_mix, g_pre_mlp=v_g_pre_mlp, g_post_mlp=v_g_post_mlp)
    small_g = dict(g_pre_mix=dg_pre_mix, b_in=db_in, b_dw=db_dw, g_conv_ln=dg_conv_ln, b_conv_ln=db_conv_ln,
                   g_post_mix=dg_post_mix, g_pre_mlp=dg_pre_mlp, g_post_mlp=dg_post_mlp)
    shapes = [small_w[n].shape for n in small]
    tail_shapes = [loss_part.shape, dtaps.shape]
    summed = _all_reduce_small(_pack([small_g[n] for n in small] + [loss_part, dtaps]))
    in_started, in_token = reduce_start(group_in, "in", summed)
    zeros_tail = [jnp.zeros(shp, F32) for shp in tail_shapes]
    res = _adamw(_pack([small_w[n] for n in small] + zeros_tail), summed,
                 _pack([small_m[n] for n in small] + zeros_tail), _pack([small_v[n] for n in small] + zeros_tail),
                 "adamw_small")
    unpacked = [_unpack(r, shapes + tail_shapes) for r in res]
    for i, n in enumerate(small):
        out_g[n], out_d[n], out_m[n], out_v[n] = [u[i] for u in unpacked]
    loss = unpacked[0][len(small)][0, 0]
    taps_grad = lax.dynamic_slice(unpacked[0][len(small) + 1], (0, chip * cs), (CONV_HALO, cs))
    res = _adamw(_pad_rows(w_dw[0], CONV_HALO), taps_grad, _pad_rows(m_w_dw[0], CONV_HALO), _pad_rows(v_w_dw[0], CONV_HALO),
                 "adamw_taps")
    out_g["w_dw"], out_d["w_dw"], out_m["w_dw"], out_v["w_dw"] = [r[:CONV_WIDTH][None] for r in res]
    done = reduce_finish(group_mlp, "mlp", mlp_started, in_token)
    done = reduce_finish(group_mix, "mix", mix_started, done)
    reduce_finish(group_in, "in", in_started, done)

    order = ["g_pre_mix", "w_in", "b_in", "w_dw", "b_dw", "g_conv_ln", "b_conv_ln", "w_sb_out", "w_conv_out", "w_o",
             "g_post_mix", "g_pre_mlp", "w_up", "w_down", "g_post_mlp"]
    return (loss, grad_x[None], *[out_g[n] for n in order], *[out_d[n] for n in order],
            *[out_m[n] for n in order], *[out_v[n] for n in order])
```

```python
import functools
import math

import jax
import jax.numpy as jnp
from jax import lax
from jax.experimental import pallas as pl
from jax.experimental.pallas import tpu as pltpu

F32 = jnp.float32
BF16 = jnp.bfloat16
MESH = pl.DeviceIdType.MESH

HEAD_DIM = 128
CONV_WIDTH = 31
CONV_HALO = 32
EPS = 1e-6
ADAM_LR = 0.001
ADAM_B1 = 0.9
ADAM_B2 = 0.999
ADAM_EPS = 1e-08
ADAM_WD = 0.01
ADAM_STEP = 10
N_CHIPS = 4
VMEM_LIMIT = 48 * 1024 * 1024


def _params(**kw):
    return pltpu.CompilerParams(vmem_limit_bytes=VMEM_LIMIT, **kw)


def _tile(n, want, align=128):
    if n <= want:
        return n
    for t in range(want - want % align, 0, -align):
        if n % t == 0:
            return t
    raise ValueError((n, want, align))


def _mm(a, b, *, name, ta=False, tb=False, b_groups=1, out_groups=1, bias=None, extra=None, epilogue=None,
        out_dtypes=(F32,), tm=1024, tn=1024, tk=2048, after=None):
    if ta:
        K, M = a.shape
    else:
        M, K = a.shape
    if b_groups == 1:
        br, bc = b.shape
    else:
        _, br, bcg = b.shape
        bc = bcg * b_groups
    if tb:
        N, Kb = br, bc
    else:
        Kb, N = br, bc
    assert Kb == K, (name, a.shape, b.shape)
    tm, tn, tk = _tile(M, tm), _tile(N, tn), _tile(K, tk)
    if b_groups > 1:
        if tb:
            tk = _tile(K // b_groups, tk)
        else:
            tn = _tile(N // b_groups, tn)
    if out_groups > 1:
        tn = _tile(N // out_groups, tn)
        if b_groups > 1 and not tb:
            tn = _tile(N // b_groups, tn)
    nm, nn, nk = M // tm, N // tn, K // tk
    extra = tuple(extra or ())
    n_extra = len(extra)
    has_bias = bias is not None
    n_out = len(out_dtypes)

    a_spec = pl.BlockSpec((tk, tm), lambda i, j, k: (k, i)) if ta else pl.BlockSpec((tm, tk), lambda i, j, k: (i, k))
    if b_groups == 1:
        b_spec = pl.BlockSpec((tn, tk), lambda i, j, k: (j, k)) if tb else pl.BlockSpec((tk, tn), lambda i, j, k: (k, j))
    elif tb:
        kpg = (K // b_groups) // tk
        b_spec = pl.BlockSpec((None, tn, tk), lambda i, j, k: (k // kpg, j, k % kpg))
    else:
        npg = (N // b_groups) // tn
        b_spec = pl.BlockSpec((None, tk, tn), lambda i, j, k: (j // npg, k, j % npg))
    in_specs = [a_spec, b_spec]
    operands = [a, b]
    if has_bias:
        in_specs.append(pl.BlockSpec((1, tn), lambda i, j, k: (0, j)))
        operands.append(bias)
    for e in extra:
        in_specs.append(pl.BlockSpec((tm, tn), lambda i, j, k: (i, j)))
        operands.append(e)
    n_after = 0 if after is None else 1
    if after is not None:
        in_specs.append(pl.BlockSpec(memory_space=pl.ANY))
        operands.append(after)
    if out_groups == 1:
        o_spec = pl.BlockSpec((tm, tn), lambda i, j, k: (i, j))
        o_shape = (M, N)
    else:
        opg = (N // out_groups) // tn
        o_spec = pl.BlockSpec((None, tm, tn), lambda i, j, k: (j // opg, i, j % opg))
        o_shape = (out_groups, M, N // out_groups)
    dims = (((0 if ta else 1,), (1 if tb else 0,)), ((), ()))

    def body(*refs):
        a_ref, b_ref = refs[0], refs[1]
        pos = 2
        bias_ref = None
        if has_bias:
            bias_ref = refs[pos]
            pos += 1
        extra_refs = refs[pos:pos + n_extra]
        pos += n_extra + n_after
        out_refs = refs[pos:pos + n_out]
        pos += n_out
        acc_ref = refs[pos] if nk > 1 else None

        part = lax.dot_general(a_ref[...].astype(BF16), b_ref[...].astype(BF16), dims, preferred_element_type=F32)

        def finish(acc):
            if has_bias:
                acc = acc + bias_ref[...]
            outs = epilogue(acc, *[e[...] for e in extra_refs]) if epilogue is not None else (acc,)
            for o_ref, o in zip(out_refs, outs):
                o_ref[...] = o.astype(o_ref.dtype)

        if nk == 1:
            finish(part)
        else:
            k = pl.program_id(2)

            @pl.when(k == 0)
            def _():
                acc_ref[...] = part

            @pl.when(k > 0)
            def _():
                acc_ref[...] += part

            @pl.when(k == nk - 1)
            def _():
                finish(acc_ref[...])

    outs = pl.pallas_call(
        body,
        name=name,
        grid=(nm, nn, nk),
        in_specs=in_specs,
        out_specs=[o_spec] * n_out,
        out_shape=[jax.ShapeDtypeStruct(o_shape, dt) for dt in out_dtypes],
        scratch_shapes=[pltpu.VMEM((tm, tn), F32)] if nk > 1 else [],
        compiler_params=_params(dimension_semantics=("parallel", "parallel", "arbitrary")),
    )(*operands)
    return outs[0] if n_out == 1 else outs


def _rms_stats(x):
    return lax.rsqrt(jnp.mean(x * x, axis=-1, keepdims=True) + EPS)


def _rms_bwd(x, r, g, dy):
    gy = dy * g
    return r * gy - x * (r * r * r) * jnp.mean(x * gy, axis=-1, keepdims=True)


def _sigmoid(x):
    return 1.0 / (1.0 + jnp.exp(-x))


def _row_call(body, *, name, rows, tr, ins, outs, acc_outs=()):
    n = rows // tr
    in_specs = []
    for arr, kind in ins:
        if kind == "row":
            in_specs.append(pl.BlockSpec((tr, arr.shape[1]), lambda i: (i, 0)))
        else:
            in_specs.append(pl.BlockSpec(arr.shape, lambda i: (0, 0)))
    out_specs = [pl.BlockSpec((tr, w), lambda i: (i, 0)) for w, _ in outs]
    out_shape = [jax.ShapeDtypeStruct((rows, w), dt) for w, dt in outs]
    out_specs += [pl.BlockSpec((1, w), lambda i: (0, 0)) for w in acc_outs]
    out_shape += [jax.ShapeDtypeStruct((1, w), F32) for w in acc_outs]
    return pl.pallas_call(
        body, name=name, grid=(n,), in_specs=in_specs, out_specs=out_specs, out_shape=out_shape,
        compiler_params=_params(dimension_semantics=("arbitrary",)),
    )(*[a for a, _ in ins])


def _accumulate(ref, val):
    @pl.when(pl.program_id(0) == 0)
    def _():
        ref[...] = val

    @pl.when(pl.program_id(0) > 0)
    def _():
        ref[...] += val


def _rms_fwd(x, g, tr=256):
    def body(x_ref, g_ref, h_ref):
        xv = x_ref[...]
        h_ref[...] = (xv * _rms_stats(xv) * g_ref[...]).astype(BF16)

    (h,) = _row_call(body, name="rms_fwd", rows=x.shape[0], tr=tr, ins=[(x, "row"), (g, "vec")], outs=[(x.shape[1], BF16)])
    return h


def _resid_rms(x, y, g2, g3, tr=256):
    def body(x_ref, y_ref, g2_ref, g3_ref, x2_ref, h2_ref):
        yv = y_ref[...]
        x2 = x_ref[...] + yv * _rms_stats(yv) * g2_ref[...]
        x2_ref[...] = x2
        h2_ref[...] = (x2 * _rms_stats(x2) * g3_ref[...]).astype(BF16)

    d = x.shape[1]
    return _row_call(body, name="resid_rms", rows=x.shape[0], tr=tr,
                     ins=[(x, "row"), (y, "row"), (g2, "vec"), (g3, "vec")], outs=[(d, F32), (d, BF16)])


def _final(x2, dn, g4, tgt, tr=256):
    d = x2.shape[1]

    def body(x2_ref, dn_ref, g4_ref, t_ref, dx3_ref, ddn_ref, dg4_ref, loss_ref):
        dn_v = dn_ref[...]
        r = _rms_stats(dn_v)
        g = g4_ref[...]
        e = x2_ref[...] + dn_v * r * g - t_ref[...]
        dx3 = e * (1.0 / d)
        dx3_ref[...] = dx3
        ddn_ref[...] = _rms_bwd(dn_v, r, g, dx3).astype(BF16)
        _accumulate(dg4_ref, jnp.sum(dx3 * dn_v * r, axis=0, keepdims=True))
        part = 0.5 * jnp.sum(jnp.mean(e * e, axis=-1, keepdims=True), axis=0, keepdims=True)
        _accumulate(loss_ref, jnp.broadcast_to(part, loss_ref.shape))

    return _row_call(body, name="final", rows=x2.shape[0], tr=tr,
                     ins=[(x2, "row"), (dn, "row"), (g4, "vec"), (tgt, "row")],
                     outs=[(d, F32), (d, BF16)], acc_outs=(d, 128))


def _rms_bwd2(dx3, dh2, x2, y, g2, g3, tr=256):
    d = x2.shape[1]

    def body(dx3_ref, dh2_ref, x2_ref, y_ref, g2_ref, g3_ref, dx2_ref, dy_ref, dg3_ref, dg2_ref):
        x2v, dh2v, yv = x2_ref[...], dh2_ref[...], y_ref[...]
        r3 = _rms_stats(x2v)
        dx2 = dx3_ref[...] + _rms_bwd(x2v, r3, g3_ref[...], dh2v)
        dx2_ref[...] = dx2
        _accumulate(dg3_ref, jnp.sum(dh2v * x2v * r3, axis=0, keepdims=True))
        r2 = _rms_stats(yv)
        dy_ref[...] = _rms_bwd(yv, r2, g2_ref[...], dx2).astype(BF16)
        _accumulate(dg2_ref, jnp.sum(dx2 * yv * r2, axis=0, keepdims=True))

    return _row_call(body, name="rms_bwd2", rows=x2.shape[0], tr=tr,
                     ins=[(dx3, "row"), (dh2, "row"), (x2, "row"), (y, "row"), (g2, "vec"), (g3, "vec")],
                     outs=[(d, F32), (d, BF16)], acc_outs=(d, d))


def _rms_bwd1(dx2, dh, x, g1, tr=256):
    d = x.shape[1]

    def body(dx2_ref, dh_ref, x_ref, g1_ref, gx_ref, dg1_ref):
        xv, dhv = x_ref[...], dh_ref[...]
        r = _rms_stats(xv)
        gx_ref[...] = dx2_ref[...] + _rms_bwd(xv, r, g1_ref[...], dhv)
        _accumulate(dg1_ref, jnp.sum(dhv * xv * r, axis=0, keepdims=True))

    return _row_call(body, name="rms_bwd1", rows=x.shape[0], tr=tr,
                     ins=[(dx2, "row"), (dh, "row"), (x, "row"), (g1, "vec")], outs=[(d, F32)], acc_outs=(d,))


def _merge_fwd(proj, o_sb, o_cv, off_sb, off_cv, gw, tr=256):
    s, d = o_sb.shape
    nj = d // gw
    b_sb, b_cv = off_sb // gw, off_cv // gw

    def body(gs_ref, gc_ref, osb_ref, ocv_ref, m_ref):
        m_ref[...] = (_sigmoid(gs_ref[...]) * osb_ref[...] + _sigmoid(gc_ref[...]) * ocv_ref[...]).astype(BF16)

    blk = lambda i, j: (i, j)
    return pl.pallas_call(
        body, name="merge_fwd", grid=(s // tr, nj),
        in_specs=[pl.BlockSpec((tr, gw), lambda i, j: (i, b_sb + j)), pl.BlockSpec((tr, gw), lambda i, j: (i, b_cv + j)),
                  pl.BlockSpec((tr, gw), blk), pl.BlockSpec((tr, gw), blk)],
        out_specs=pl.BlockSpec((tr, gw), blk), out_shape=jax.ShapeDtypeStruct((s, d), BF16),
        compiler_params=_params(dimension_semantics=("parallel", "parallel")),
    )(proj, proj, o_sb, o_cv)


def _merge_bwd(proj, o_sb, o_cv, dmerged, off_sb, off_cv, gw, tr=256):
    s, d = o_sb.shape
    nj = d // gw
    ni = s // tr
    b_sb, b_cv = off_sb // gw, off_cv // gw

    def body(gs_ref, gc_ref, osb_ref, ocv_ref, dm_ref, dosb_ref, docv_ref, dgs_ref, dgc_ref, sgs_ref, sgc_ref):
        dm = dm_ref[...]
        s_sb, s_cv = _sigmoid(gs_ref[...]), _sigmoid(gc_ref[...])
        dosb_ref[...] = (dm * s_sb).astype(BF16)
        docv_ref[...] = (dm * s_cv).astype(BF16)
        dgs = dm * osb_ref[...] * s_sb * (1.0 - s_sb)
        dgc = dm * ocv_ref[...] * s_cv * (1.0 - s_cv)
        dgs_ref[...] = dgs.astype(BF16)
        dgc_ref[...] = dgc.astype(BF16)
        i = pl.program_id(1)
        for ref, val in ((sgs_ref, dgs), (sgc_ref, dgc)):
            col = jnp.sum(val, axis=0, keepdims=True)

            @pl.when(i == 0)
            def _():
                ref[...] = col

            @pl.when(i > 0)
            def _():
                ref[...] += col

    blk = lambda j, i: (i, j)
    outs = pl.pallas_call(
        body, name="merge_bwd", grid=(nj, ni),
        in_specs=[pl.BlockSpec((tr, gw), lambda j, i: (i, b_sb + j)), pl.BlockSpec((tr, gw), lambda j, i: (i, b_cv + j)),
                  pl.BlockSpec((tr, gw), blk), pl.BlockSpec((tr, gw), blk), pl.BlockSpec((tr, gw), blk)],
        out_specs=[pl.BlockSpec((tr, gw), blk), pl.BlockSpec((tr, gw), blk),
                   pl.BlockSpec((tr, gw), blk), pl.BlockSpec((tr, gw), blk),
                   pl.BlockSpec((1, gw), lambda j, i: (0, j)), pl.BlockSpec((1, gw), lambda j, i: (0, j))],
        out_shape=[jax.ShapeDtypeStruct((s, d), BF16)] * 4 + [jax.ShapeDtypeStruct((1, d), F32)] * 2,
        compiler_params=_params(dimension_semantics=("parallel", "arbitrary")),
    )(proj, proj, o_sb, o_cv, dmerged)
    return outs


def _split_bf16(v):
    hi = v.astype(BF16)
    lo = (v - hi.astype(F32)).astype(BF16)
    return hi, lo


def _dot_nt(a, b):
    return lax.dot_general(a, b, (((1,), (1,)), ((), ())), preferred_element_type=F32)


def _dot_tn(a, b):
    return lax.dot_general(a, b, (((0,), (0,)), ((), ())), preferred_element_type=F32)


def _dot_nn(a, b):
    return lax.dot_general(a, b, (((1,), (0,)), ((), ())), preferred_element_type=F32)


def _sum_right(v, tri):
    hi, lo = _split_bf16(v)
    return _dot_nn(hi, tri) + _dot_nn(lo, tri)


HEADS_PER_STEP = 2


LOG2_E = 1.4426950408889634


def _sb_tile(q, kb, scale, mask):
    z = _dot_nt(q, kb) * (scale * LOG2_E)
    log_b = jnp.minimum(z, 0.0) - jnp.log2(1.0 + jnp.exp2(-jnp.abs(z)))
    log_1m = log_b - z
    if mask is not None:
        log_1m = jnp.where(mask, log_1m, 0.0)
    return log_b, log_1m


def _tile_mask(tq):
    row = lax.broadcasted_iota(jnp.int32, (tq, tq), 0)
    col = lax.broadcasted_iota(jnp.int32, (tq, tq), 1)
    return col < row


def _tri(tq, before):
    r = lax.broadcasted_iota(jnp.int32, (tq, tq), 0)
    c = lax.broadcasted_iota(jnp.int32, (tq, tq), 1)
    return jnp.where((r < c) if before else (r > c), 1.0, 0.0).astype(BF16)


def _attn_fwd(proj, n_heads, d_sb, tq=256):
    s = proj.shape[0]
    tq = _tile(s, tq)
    hb = d_sb // HEAD_DIM
    scale = 1.0 / math.sqrt(HEAD_DIM)

    heads = [pl.ds(n * HEAD_DIM, HEAD_DIM) for n in range(HEADS_PER_STEP)]
    wide = HEADS_PER_STEP * HEAD_DIM
    hb //= HEADS_PER_STEP

    def body(q_ref, k_ref, v_ref, o_ref):
        i = pl.program_id(1)
        tri = _tri(tq, False)

        def tile(j, carry, diagonal):
            rows = pl.ds(pl.multiple_of(j * tq, tq), tq)
            mask = _tile_mask(tq) if diagonal else None
            out = []
            for hd, (c_l, acc) in zip(heads, carry):
                log_b, log_1m = _sb_tile(q_ref[:, hd].astype(BF16), k_ref[rows, hd].astype(BF16), scale, mask)
                a = jnp.exp2(log_b + (c_l + _sum_right(log_1m, tri)))
                if diagonal:
                    a = jnp.where(mask, a, 0.0)
                out.append((c_l + jnp.sum(log_1m, axis=1, keepdims=True),
                            acc + _dot_nn(a.astype(BF16), v_ref[rows, hd].astype(BF16))))
            return tuple(out)

        init = tuple((jnp.zeros((tq, 1), F32), jnp.zeros((tq, HEAD_DIM), F32)) for _ in heads)
        carry = lax.fori_loop(0, i, lambda jj, c: tile(i - 1 - jj, c, False), tile(i, init, True))
        for hd, (_, acc) in zip(heads, carry):
            o_ref[:, hd] = acc.astype(BF16)

    return pl.pallas_call(
        body, name="attn_fwd", grid=(n_heads // HEADS_PER_STEP, s // tq),
        in_specs=[pl.BlockSpec((tq, wide), lambda h, i: (i, h)),
                  pl.BlockSpec((s, wide), lambda h, i: (0, hb + h)),
                  pl.BlockSpec((s, wide), lambda h, i: (0, 2 * hb + h))],
        out_specs=pl.BlockSpec((tq, wide), lambda h, i: (i, h)),
        out_shape=jax.ShapeDtypeStruct((s, d_sb), BF16),
        compiler_params=_params(dimension_semantics=("parallel", "arbitrary")),
    )(proj, proj, proj)


def _attn_bwd(proj, dattn, after, n_heads, d_sb, tq=256):
    s = proj.shape[0]
    tq = _tile(s, tq)
    nq = s // tq
    hb = d_sb // HEAD_DIM // HEADS_PER_STEP
    scale = 1.0 / math.sqrt(HEAD_DIM)
    heads = [pl.ds(n * HEAD_DIM, HEAD_DIM) for n in range(HEADS_PER_STEP)]
    wide = HEADS_PER_STEP * HEAD_DIM

    def body(q_ref, k_ref, v_ref, do_ref, _, dq_ref, dk_ref, dv_ref, sq_ref, sk_ref, sv_ref, dk_acc, dv_acc, g_st, b_st):
        i = pl.program_id(1)

        @pl.when(i == 0)
        def _():
            dk_acc[...] = jnp.zeros_like(dk_acc)
            dv_acc[...] = jnp.zeros_like(dv_acc)

        tri_after = _tri(tq, False)
        tri_before = _tri(tq, True)

        def newest_first(j, c_ls, diagonal):
            rows = pl.ds(pl.multiple_of(j * tq, tq), tq)
            mask = _tile_mask(tq) if diagonal else None
            out = []
            for n, (hd, c_l) in enumerate(zip(heads, c_ls)):
                do = do_ref[:, hd].astype(BF16)
                log_b, log_1m = _sb_tile(q_ref[:, hd].astype(BF16), k_ref[rows, hd].astype(BF16), scale, mask)
                a = jnp.exp2(log_b + (c_l + _sum_right(log_1m, tri_after)))
                beta = jnp.exp2(log_b)
                if diagonal:
                    a = jnp.where(mask, a, 0.0)
                    beta = jnp.where(mask, beta, 0.0)
                g_st[n, j] = a * _dot_nt(do, v_ref[rows, hd].astype(BF16))
                b_st[n, j] = beta
                dv_acc[rows, hd] += _dot_tn(a.astype(BF16), do)
                out.append(c_l + jnp.sum(log_1m, axis=1, keepdims=True))
            return tuple(out)

        lax.fori_loop(0, i, lambda jj, c: newest_first(i - 1 - jj, c, False),
                      newest_first(i, tuple(jnp.zeros((tq, 1), F32) for _ in heads), True))

        def oldest_first(j, carry):
            rows = pl.ds(pl.multiple_of(j * tq, tq), tq)
            out = []
            for n, (hd, (c_g, dq)) in enumerate(zip(heads, carry)):
                kb = k_ref[rows, hd].astype(BF16)
                g = g_st[n, j]
                beta = b_st[n, j]
                g_before = c_g + _sum_right(g, tri_before)
                dz16 = ((g * (1.0 - beta) - g_before * beta) * scale).astype(BF16)
                dk_acc[rows, hd] += _dot_tn(dz16, q_ref[:, hd].astype(BF16))
                out.append((c_g + jnp.sum(g, axis=1, keepdims=True), dq + _dot_nn(dz16, kb)))
            return tuple(out)

        init = tuple((jnp.zeros((tq, 1), F32), jnp.zeros((tq, HEAD_DIM), F32)) for _ in heads)
        carry = lax.fori_loop(0, i + 1, oldest_first, init)
        dq = jnp.concatenate([c[1] for c in carry], axis=1)
        dq_ref[...] = dq.astype(BF16)
        col = jnp.sum(dq, axis=0, keepdims=True)

        @pl.when(i == 0)
        def _():
            sq_ref[...] = col

        @pl.when(i > 0)
        def _():
            sq_ref[...] += col

        @pl.when(i == nq - 1)
        def _():
            dk = dk_acc[...]
            dv = dv_acc[...]
            dk_ref[...] = dk.astype(BF16)
            dv_ref[...] = dv.astype(BF16)
            sk_ref[...] = jnp.sum(dk, axis=0, keepdims=True)
            sv_ref[...] = jnp.sum(dv, axis=0, keepdims=True)

    qblk = pl.BlockSpec((tq, wide), lambda h, i: (i, h))
    kblk = pl.BlockSpec((s, wide), lambda h, i: (0, h))
    col_sum = pl.BlockSpec((1, wide), lambda h, i: (0, h))
    stash = pltpu.VMEM((HEADS_PER_STEP, nq, tq, tq), F32)
    outs = pl.pallas_call(
        body, name="attn_bwd", grid=(n_heads // HEADS_PER_STEP, nq),
        in_specs=[qblk,
                  pl.BlockSpec((s, wide), lambda h, i: (0, hb + h)),
                  pl.BlockSpec((s, wide), lambda h, i: (0, 2 * hb + h)),
                  qblk, pl.BlockSpec(memory_space=pl.ANY)],
        out_specs=[qblk, kblk, kblk, col_sum, col_sum, col_sum],
        out_shape=[jax.ShapeDtypeStruct((s, d_sb), BF16)] * 3 + [jax.ShapeDtypeStruct((1, d_sb), F32)] * 3,
        scratch_shapes=[pltpu.VMEM((s, wide), F32), pltpu.VMEM((s, wide), F32), stash, stash],
        compiler_params=_params(dimension_semantics=("parallel", "arbitrary")),
    )(proj, proj, proj, dattn, after)
    return outs


LANES = 128


def _glu(proj, off_a, dc, tr=256):
    s = proj.shape[0]
    ba = off_a // dc

    def body(a_ref, b_ref, u_ref):
        u_ref[...] = a_ref[...] * _sigmoid(b_ref[...])

    return pl.pallas_call(
        body, name="glu", grid=(s // tr,),
        in_specs=[pl.BlockSpec((tr, dc), lambda i: (i, ba)), pl.BlockSpec((tr, dc), lambda i: (i, ba + 1))],
        out_specs=pl.BlockSpec((tr, dc), lambda i: (i, 0)), out_shape=jax.ShapeDtypeStruct((s, dc), F32),
        compiler_params=_params(dimension_semantics=("parallel",)),
    )(proj, proj)


def _conv_fwd(u0, taps, b_dw, g_ln, b_ln, tt=256):
    s, dc = u0.shape
    tt = _tile(s, tt)
    hpb = tt // CONV_HALO

    def body(cur_ref, halo_ref, w_ref, bdw_ref, g_ref, b_ref, u1_ref, u3_ref, xs):
        i = pl.program_id(0)
        xs[pl.ds(0, CONV_HALO), :] = jnp.where(i > 0, halo_ref[...], 0.0)
        xs[pl.ds(CONV_HALO, tt), :] = cur_ref[...]
        for c0 in range(0, dc, LANES):
            cols = pl.ds(c0, LANES)
            acc = jnp.broadcast_to(bdw_ref[:, cols], (tt, LANES))
            for k in range(CONV_WIDTH):
                acc = acc + w_ref[pl.ds(k, 1), cols] * xs[pl.ds(CONV_HALO - (CONV_WIDTH - 1) + k, tt), cols]
            u1_ref[:, cols] = acc
        u1 = u1_ref[...]
        mu = jnp.mean(u1, axis=-1, keepdims=True)
        xc = u1 - mu
        rstd = lax.rsqrt(jnp.mean(xc * xc, axis=-1, keepdims=True) + EPS)
        u2 = xc * rstd * g_ref[...] + b_ref[...]
        u3_ref[...] = (u2 * _sigmoid(u2)).astype(BF16)

    vec = pl.BlockSpec((1, dc), lambda i: (0, 0))
    return pl.pallas_call(
        body, name="conv_fwd", grid=(s // tt,),
        in_specs=[pl.BlockSpec((tt, dc), lambda i: (i, 0)),
                  pl.BlockSpec((CONV_HALO, dc), lambda i: (jnp.maximum(i * hpb - 1, 0), 0)),
                  pl.BlockSpec((CONV_HALO, dc), lambda i: (0, 0)), vec, vec, vec],
        out_specs=[pl.BlockSpec((tt, dc), lambda i: (i, 0)), pl.BlockSpec((tt, dc), lambda i: (i, 0))],
        out_shape=[jax.ShapeDtypeStruct((s, dc), F32), jax.ShapeDtypeStruct((s, dc), BF16)],
        scratch_shapes=[pltpu.VMEM((tt + CONV_HALO, dc), F32)],
        compiler_params=_params(dimension_semantics=("arbitrary",)),
    )(u0, u0, taps, b_dw, g_ln, b_ln)


def _conv_bwd_norm(u1, du3, g_ln, b_ln, tr=256):
    dc = u1.shape[1]

    def body(u1_ref, du3_ref, g_ref, b_ref, du1_ref, dg_ref, db_ref, dbdw_ref):
        u1v = u1_ref[...]
        g = g_ref[...]
        mu = jnp.mean(u1v, axis=-1, keepdims=True)
        xc = u1v - mu
        rstd = lax.rsqrt(jnp.mean(xc * xc, axis=-1, keepdims=True) + EPS)
        xhat = xc * rstd
        u2 = xhat * g + b_ref[...]
        sg = _sigmoid(u2)
        du2 = du3_ref[...] * (sg * (1.0 + u2 * (1.0 - sg)))
        _accumulate(dg_ref, jnp.sum(du2 * xhat, axis=0, keepdims=True))
        _accumulate(db_ref, jnp.sum(du2, axis=0, keepdims=True))
        gy = du2 * g
        du1 = rstd * (gy - jnp.mean(gy, axis=-1, keepdims=True) - xhat * jnp.mean(gy * xhat, axis=-1, keepdims=True))
        du1_ref[...] = du1
        _accumulate(dbdw_ref, jnp.sum(du1, axis=0, keepdims=True))

    return _row_call(body, name="conv_bwd_norm", rows=u1.shape[0], tr=tr,
                     ins=[(u1, "row"), (du3, "row"), (g_ln, "vec"), (b_ln, "vec")],
                     outs=[(dc, F32)], acc_outs=(dc, dc, dc))


def _conv_bwd_taps(du1, u0, proj, taps, off_a, tt=256):
    s, dc = u0.shape
    tt = _tile(s, tt)
    n = s // tt
    hpb = tt // CONV_HALO
    ba = off_a // dc
    lead = CONV_HALO - (CONV_WIDTH - 1)

    def body(d_ref, dnext_ref, cur_ref, halo_ref, a_ref, b_ref, w_ref, da_ref, db_ref, sa_ref, sb_ref, dw_ref,
             xs, ds, du0):
        i = pl.program_id(0)
        xs[pl.ds(0, CONV_HALO), :] = jnp.where(i > 0, halo_ref[...], 0.0)
        xs[pl.ds(CONV_HALO, tt), :] = cur_ref[...]
        ds[pl.ds(0, tt), :] = d_ref[...]
        ds[pl.ds(tt, CONV_HALO), :] = jnp.where(i < n - 1, dnext_ref[...], 0.0)

        @pl.when(i == 0)
        def _():
            dw_ref[...] = jnp.zeros_like(dw_ref)

        for c0 in range(0, dc, LANES):
            cols = pl.ds(c0, LANES)
            d_cur = ds[pl.ds(0, tt), cols]
            acc = jnp.zeros((tt, LANES), F32)
            for k in range(CONV_WIDTH):
                acc = acc + w_ref[pl.ds(k, 1), cols] * ds[pl.ds(CONV_WIDTH - 1 - k, tt), cols]
                dw_ref[pl.ds(k, 1), cols] += jnp.sum(d_cur * xs[pl.ds(lead + k, tt), cols], axis=0, keepdims=True)
            du0[:, cols] = acc
        d0 = du0[...]
        sg = _sigmoid(b_ref[...])
        da = d0 * sg
        db = d0 * a_ref[...] * sg * (1.0 - sg)
        da_ref[...] = da.astype(BF16)
        db_ref[...] = db.astype(BF16)
        _accumulate(sa_ref, jnp.sum(da, axis=0, keepdims=True))
        _accumulate(sb_ref, jnp.sum(db, axis=0, keepdims=True))

    row = pl.BlockSpec((tt, dc), lambda i: (i, 0))
    vec = pl.BlockSpec((1, dc), lambda i: (0, 0))
    taps_spec = pl.BlockSpec((CONV_HALO, dc), lambda i: (0, 0))
    return pl.pallas_call(
        body, name="conv_bwd_taps", grid=(n,),
        in_specs=[row, pl.BlockSpec((CONV_HALO, dc), lambda i: (jnp.minimum((i + 1) * hpb, n * hpb - 1), 0)),
                  row, pl.BlockSpec((CONV_HALO, dc), lambda i: (jnp.maximum(i * hpb - 1, 0), 0)),
                  pl.BlockSpec((tt, dc), lambda i: (i, ba)), pl.BlockSpec((tt, dc), lambda i: (i, ba + 1)), taps_spec],
        out_specs=[row, row, vec, vec, taps_spec],
        out_shape=[jax.ShapeDtypeStruct((s, dc), BF16), jax.ShapeDtypeStruct((s, dc), BF16),
                   jax.ShapeDtypeStruct((1, dc), F32), jax.ShapeDtypeStruct((1, dc), F32),
                   jax.ShapeDtypeStruct((CONV_HALO, dc), F32)],
        scratch_shapes=[pltpu.VMEM((tt + CONV_HALO, dc), F32), pltpu.VMEM((tt + CONV_HALO, dc), F32),
                        pltpu.VMEM((tt, dc), F32)],
        compiler_params=_params(dimension_semantics=("arbitrary",)),
    )(du1, du1, u0, u0, proj, proj, taps)


def _elementwise(body, *, name, ins, out_dtypes, tr=128):
    rows, cols = ins[0].shape
    tr = _tile(rows, tr, 8)
    spec = pl.BlockSpec((tr, cols), lambda i: (i, 0))
    return pl.pallas_call(
        body, name=name, grid=(rows // tr,), in_specs=[spec] * len(ins), out_specs=[spec] * len(out_dtypes),
        out_shape=[jax.ShapeDtypeStruct((rows, cols), dt) for dt in out_dtypes],
        compiler_params=_params(dimension_semantics=("parallel",)),
    )(*ins)


def _prefetched(body, *, name, where, grid, in_specs, out_specs, out_shape, operands):
    return pl.pallas_call(
        body, name=name, out_shape=out_shape,
        grid_spec=pltpu.PrefetchScalarGridSpec(num_scalar_prefetch=1, grid=grid, in_specs=in_specs, out_specs=out_specs),
        compiler_params=_params(dimension_semantics=("parallel",) * len(grid)),
    )(where, *operands)


def _cast_into_block(w, where, after, name, tr=256):
    rows, cols = w.shape
    tr = _tile(rows, tr, 16)

    def body(where_ref, w_ref, _, o_ref):
        o_ref[...] = w_ref[...].astype(BF16)

    return _prefetched(
        body, name=name, where=where, grid=(rows // tr,),
        in_specs=[pl.BlockSpec((tr, cols), lambda i, wh: (i, 0)), pl.BlockSpec(memory_space=pl.ANY)],
        out_specs=pl.BlockSpec((None, tr, cols), lambda i, wh: (wh[0], i, 0)),
        out_shape=jax.ShapeDtypeStruct((N_CHIPS, rows, cols), BF16), operands=[w, after])


def _add_pair(grad, theirs, where, name, tr=256):
    nb, half, cols = theirs.shape
    tr = _tile(half, tr, 16)
    nh = half // tr

    def body(where_ref, a_ref, b_ref, o_ref):
        o_ref[...] = (a_ref[...].astype(F32) + b_ref[...].astype(F32)).astype(BF16)

    blk = pl.BlockSpec((None, tr, cols), lambda b, i, wh: (b, i, 0))
    return _prefetched(
        body, name=name, where=where, grid=(nb, nh),
        in_specs=[pl.BlockSpec((None, tr, cols), lambda b, i, wh: (b, wh[1] * nh + i, 0)), blk],
        out_specs=blk, out_shape=jax.ShapeDtypeStruct(theirs.shape, BF16), operands=[grad, theirs])


def _sum_slots(part, slots, where, name, tr=256):
    _, half, cols = part.shape
    tr = _tile(half, tr, 16)
    nh = half // tr

    def body(where_ref, p_ref, s_ref, o_ref):
        o_ref[...] = ((p_ref[...].astype(F32) + s_ref[0].astype(F32)) + s_ref[1].astype(F32)) + s_ref[2].astype(F32)

    return _prefetched(
        body, name=name, where=where, grid=(nh,),
        in_specs=[pl.BlockSpec((None, tr, cols), lambda i, wh: (wh[0], i, 0)),
                  pl.BlockSpec((3, tr, cols), lambda i, wh: (0, i, 0))],
        out_specs=pl.BlockSpec((tr, cols), lambda i, wh: (wh[1] * nh + i, 0)),
        out_shape=jax.ShapeDtypeStruct((2 * half, cols), F32), operands=[part, slots])


def _adamw(w, g, m, v, name):
    bc1 = 1.0 - ADAM_B1 ** ADAM_STEP
    bc2 = 1.0 - ADAM_B2 ** ADAM_STEP

    def body(w_ref, g_ref, m_ref, v_ref, go_ref, d_ref, mo_ref, vo_ref):
        g = g_ref[...]
        m_new = ADAM_B1 * m_ref[...] + (1.0 - ADAM_B1) * g
        v_new = ADAM_B2 * v_ref[...] + (1.0 - ADAM_B2) * (g * g)
        m_hat = m_new / bc1
        v_hat = v_new / bc2
        go_ref[...] = g
        d_ref[...] = -ADAM_LR * (m_hat / (jnp.sqrt(v_hat) + ADAM_EPS) + ADAM_WD * w_ref[...])
        mo_ref[...] = m_new
        vo_ref[...] = v_new

    return _elementwise(body, name=name, ins=[w, g, m, v], out_dtypes=[F32] * 4, tr=128)


ANY = pl.BlockSpec(memory_space=pl.ANY)


def _mesh_pos():
    return lax.axis_index("x"), lax.axis_index("y"), lax.axis_index("c")


def _other_chips(x, y):
    return [(1 - x, y), (x, 1 - y), (1 - x, 1 - y)]


def _remote(src, dst, send_sems, recv_sems, idx, to):
    return pltpu.make_async_remote_copy(src_ref=src, dst_ref=dst, send_sem=send_sems.at[idx], recv_sem=recv_sems.at[idx],
                                        device_id=to, device_id_type=MESH)


HBM = pl.BlockSpec(memory_space=pltpu.HBM)
SEM = pl.BlockSpec(memory_space=pltpu.SEMAPHORE)
EFFECT = pltpu.SideEffectType.DATAFLOW_SIDE_EFFECTING


def _in_hbm(arrays):
    return [pltpu.with_memory_space_constraint(a, pltpu.HBM) for a in arrays]


def _ici_copies(slab, send_sems, recv_sems, x, y, c):
    half = slab.shape[1] // 2
    rows = pl.ds(c * half, half)
    mine = slab.at[2 * x + y, rows]
    out = []
    for j, (cx, cy) in enumerate(_other_chips(x, y)):
        got = slab.at[2 * cx + cy, rows]
        out.append((_remote(mine, mine, send_sems, recv_sems, j, (cx, cy, c)),
                    _remote(got, got, send_sems, recv_sems, j, (cx, cy, c))))
    return out


def _gather_start(slabs, after, name):
    n = len(slabs)

    def body(*refs):
        ins = refs[:n]
        send_sems, recv_sems, token = refs[2 * n + 1:3 * n + 1], refs[3 * n + 1:4 * n + 1], refs[4 * n + 1]
        x, y, c = _mesh_pos()
        for w in range(n):
            for send, _ in _ici_copies(ins[w], send_sems[w], recv_sems[w], x, y, c):
                send.start()
        token[...] = jnp.zeros_like(token)

    outs = pl.pallas_call(
        body, name=name, in_specs=[HBM] * n + [ANY],
        out_specs=[HBM] * n + [SEM] * (2 * n) + [pl.BlockSpec(memory_space=pltpu.VMEM)],
        out_shape=[pltpu.HBM(s.shape, s.dtype) for s in slabs] + [pltpu.SemaphoreType.DMA((3,))] * (2 * n)
        + [jax.ShapeDtypeStruct((8, 128), F32)],
        input_output_aliases={w: w for w in range(n)},
        compiler_params=_params(has_side_effects=EFFECT),
    )(*_in_hbm(slabs), after)
    return (outs[:n], outs[n:2 * n], outs[2 * n:3 * n]), outs[3 * n]


def _gather_wait(slabs, send_sems, recv_sems, after, name):
    n = len(slabs)

    def body(*refs):
        ins = refs[:n]
        sends, recvs = refs[n:2 * n], refs[2 * n:3 * n]
        x, y, c = _mesh_pos()
        for w in range(n):
            for send, recv in _ici_copies(ins[w], sends[w], recvs[w], x, y, c):
                send.wait_send()
                recv.wait_recv()

    return pl.pallas_call(
        body, name=name, in_specs=[HBM] * n + [SEM] * (2 * n) + [ANY], out_specs=[HBM] * n,
        out_shape=[pltpu.HBM(s.shape, s.dtype) for s in slabs],
        input_output_aliases={w: w for w in range(n)},
        compiler_params=_params(has_side_effects=EFFECT),
    )(*slabs, *send_sems, *recv_sems, after)


def _gather_forward(slabs, name):
    n = len(slabs)

    def body(*refs):
        outs = refs[n:2 * n]
        send_sems, recv_sems = refs[2 * n:]
        x, y, c = _mesh_pos()
        sibling = (x, y, 1 - c)
        chips = _other_chips(x, y)
        started = []
        for w in range(n):
            half = outs[w].shape[1] // 2
            for j, (cx, cy) in enumerate(chips):
                got = outs[w].at[2 * cx + cy, pl.ds(c * half, half)]
                fw = _remote(got, got, send_sems, recv_sems, 3 * w + j, sibling)
                fw.start()
                started.append(fw)
        for w in range(n):
            half = outs[w].shape[1] // 2
            for j, (cx, cy) in enumerate(chips):
                got = outs[w].at[2 * cx + cy, pl.ds((1 - c) * half, half)]
                _remote(got, got, send_sems, recv_sems, 3 * w + j, sibling).wait_recv()
        for cp in started:
            cp.wait_send()

    return pl.pallas_call(
        body, name=name, in_specs=[ANY] * n, out_specs=[ANY] * n,
        out_shape=[jax.ShapeDtypeStruct(s.shape, s.dtype) for s in slabs],
        input_output_aliases={w: w for w in range(n)},
        scratch_shapes=[pltpu.SemaphoreType.DMA((3 * n,)), pltpu.SemaphoreType.DMA((3 * n,))],
        compiler_params=_params(has_side_effects=True),
    )(*slabs)


def _gather_taps(taps):
    def body(in_ref, out_ref, send_sems, recv_sems, local_sem):
        x, y, c = _mesh_pos()
        me = 2 * x + y
        chips = _other_chips(x, y)
        lc = pltpu.make_async_copy(in_ref, out_ref.at[me], local_sem)
        lc.start()
        sends = [_remote(in_ref, out_ref.at[me], send_sems, recv_sems, j, (cx, cy, c)) for j, (cx, cy) in enumerate(chips)]
        for cp in sends:
            cp.start()
        for j, (cx, cy) in enumerate(chips):
            got = out_ref.at[2 * cx + cy]
            _remote(got, got, send_sems, recv_sems, j, (cx, cy, c)).wait_recv()
        for cp in sends:
            cp.wait_send()
        lc.wait()

    vm = pl.BlockSpec(memory_space=pltpu.VMEM)
    return pl.pallas_call(
        body, name="gather_taps", in_specs=[vm], out_specs=vm,
        out_shape=jax.ShapeDtypeStruct((N_CHIPS,) + taps.shape, taps.dtype),
        scratch_shapes=[pltpu.SemaphoreType.DMA((3,)), pltpu.SemaphoreType.DMA((3,)), pltpu.SemaphoreType.DMA],
        compiler_params=_params(has_side_effects=True),
    )(taps)


def _exchange_halves(grads, name):
    n = len(grads)

    def body(*refs):
        ins, their_refs = refs[:n], refs[n:2 * n]
        send_sems, recv_sems = refs[2 * n:]
        x, y, c = _mesh_pos()
        sibling = (x, y, 1 - c)
        sends = []
        for w in range(n):
            half = ins[w].shape[1] // 2
            cp = _remote(ins[w].at[:, pl.ds((1 - c) * half, half), :], their_refs[w], send_sems, recv_sems, w, sibling)
            cp.start()
            sends.append(cp)
        for w in range(n):
            _remote(their_refs[w], their_refs[w], send_sems, recv_sems, w, sibling).wait_recv()
        for cp in sends:
            cp.wait_send()

    halves = [jax.ShapeDtypeStruct((g.shape[0], g.shape[1] // 2, g.shape[2]), g.dtype) for g in grads]
    return pl.pallas_call(
        body, name=name, in_specs=[ANY] * n, out_specs=[ANY] * n, out_shape=halves,
        scratch_shapes=[pltpu.SemaphoreType.DMA((n,)), pltpu.SemaphoreType.DMA((n,))],
        compiler_params=_params(has_side_effects=True),
    )(*grads)


def _scatter_copies(part, slots, send_sems, recv_sems, x, y, c):
    out = []
    for j, (cx, cy) in enumerate(_other_chips(x, y)):
        out.append((_remote(part.at[2 * cx + cy], slots.at[j], send_sems, recv_sems, j, (cx, cy, c)),
                    _remote(slots.at[j], slots.at[j], send_sems, recv_sems, j, (cx, cy, c))))
    return out


def _scatter_start(parts, after, name):
    n = len(parts)
    slots = [lax.empty((3,) + p.shape[1:], p.dtype) for p in parts]

    def body(*refs):
        ins, lands = refs[:n], refs[n:2 * n]
        send_sems, recv_sems, token = refs[4 * n + 1:5 * n + 1], refs[5 * n + 1:6 * n + 1], refs[6 * n + 1]
        x, y, c = _mesh_pos()
        for w in range(n):
            for send, _ in _scatter_copies(ins[w], lands[w], send_sems[w], recv_sems[w], x, y, c):
                send.start()
        token[...] = jnp.zeros_like(token)

    outs = pl.pallas_call(
        body, name=name, in_specs=[HBM] * (2 * n) + [ANY],
        out_specs=[HBM] * (2 * n) + [SEM] * (2 * n) + [pl.BlockSpec(memory_space=pltpu.VMEM)],
        out_shape=[pltpu.HBM(a.shape, a.dtype) for a in list(parts) + slots] + [pltpu.SemaphoreType.DMA((3,))] * (2 * n)
        + [jax.ShapeDtypeStruct((8, 128), F32)],
        input_output_aliases={k: k for k in range(2 * n)},
        compiler_params=_params(has_side_effects=EFFECT),
    )(*_in_hbm(list(parts) + slots), after)
    return (outs[:n], outs[n:2 * n], outs[2 * n:3 * n], outs[3 * n:4 * n]), outs[4 * n]


def _scatter_wait(parts, slots, send_sems, recv_sems, after, name):
    n = len(parts)

    def body(*refs):
        ins, lands = refs[:n], refs[n:2 * n]
        sends, recvs = refs[2 * n:3 * n], refs[3 * n:4 * n]
        x, y, c = _mesh_pos()
        for w in range(n):
            for send, recv in _scatter_copies(ins[w], lands[w], sends[w], recvs[w], x, y, c):
                send.wait_send()
                recv.wait_recv()

    outs = pl.pallas_call(
        body, name=name, in_specs=[HBM] * (2 * n) + [SEM] * (2 * n) + [ANY], out_specs=[HBM] * (2 * n),
        out_shape=[pltpu.HBM(a.shape, a.dtype) for a in list(parts) + list(slots)],
        input_output_aliases={k: k for k in range(2 * n)},
        compiler_params=_params(has_side_effects=EFFECT),
    )(*parts, *slots, *send_sems, *recv_sems, after)
    return outs[:n], outs[n:]


def _share_halves(halves, name):
    n = len(halves)

    def body(*refs):
        outs = refs[n:2 * n]
        send_sems, recv_sems = refs[2 * n:]
        x, y, c = _mesh_pos()
        sibling = (x, y, 1 - c)
        sends = []
        for w in range(n):
            h = outs[w].shape[0] // 2
            mine = outs[w].at[pl.ds(c * h, h)]
            cp = _remote(mine, mine, send_sems, recv_sems, w, sibling)
            cp.start()
            sends.append(cp)
        for w in range(n):
            h = outs[w].shape[0] // 2
            got = outs[w].at[pl.ds((1 - c) * h, h)]
            _remote(got, got, send_sems, recv_sems, w, sibling).wait_recv()
        for cp in sends:
            cp.wait_send()

    return pl.pallas_call(
        body, name=name, in_specs=[ANY] * n, out_specs=[ANY] * n,
        out_shape=[jax.ShapeDtypeStruct(h.shape, h.dtype) for h in halves],
        input_output_aliases={w: w for w in range(n)},
        scratch_shapes=[pltpu.SemaphoreType.DMA((n,)), pltpu.SemaphoreType.DMA((n,))],
        compiler_params=_params(has_side_effects=True),
    )(*halves)


def _all_reduce_small(buf):
    rows, width = buf.shape

    def body(in_ref, out_ref, gathered, send_sems, recv_sems):
        x, y, c = _mesh_pos()
        me = 4 * x + 2 * y + c
        gathered[me] = in_ref[...]
        flips = [(dx, dy, dz) for dx in (0, 1) for dy in (0, 1) for dz in (0, 1)][1:]
        peers = [((1 - x) if dx else x, (1 - y) if dy else y, (1 - c) if dz else c) for dx, dy, dz in flips]
        sends = [_remote(in_ref, gathered.at[me], send_sems, recv_sems, k, peer) for k, peer in enumerate(peers)]
        for cp in sends:
            cp.start()
        for k, (px, py, pc) in enumerate(peers):
            got = gathered.at[4 * px + 2 * py + pc]
            _remote(got, got, send_sems, recv_sems, k, (px, py, pc)).wait_recv()
        for cp in sends:
            cp.wait_send()
        total = gathered[0]
        for d in range(1, 8):
            total = total + gathered[d]
        out_ref[...] = total

    vm = pl.BlockSpec(memory_space=pltpu.VMEM)
    return pl.pallas_call(
        body, name="all_reduce_small", in_specs=[vm], out_specs=vm, out_shape=jax.ShapeDtypeStruct(buf.shape, F32),
        scratch_shapes=[pltpu.VMEM((8, rows, width), F32), pltpu.SemaphoreType.DMA((7,)), pltpu.SemaphoreType.DMA((7,))],
        compiler_params=_params(has_side_effects=True),
    )(buf)


PACK_W = 2048
PACK_ROWS = 8


def _pack(vectors):
    flat = jnp.concatenate([v.reshape(-1) for v in vectors])
    unit = PACK_W * PACK_ROWS
    total = -(-flat.shape[0] // unit) * unit
    return jnp.pad(flat, (0, total - flat.shape[0])).reshape(total // PACK_W, PACK_W)


def _unpack(buf, shapes):
    flat = buf.reshape(-1)
    out, pos = [], 0
    for shp in shapes:
        size = math.prod(shp)
        out.append(flat[pos:pos + size].reshape(shp))
        pos += size
    return out


def _pad_rows(a, rows):
    return jnp.pad(a, ((0, rows - a.shape[0]), (0, 0)))


def kernel(x, g_pre_mix, w_in, b_in, w_dw, b_dw, g_conv_ln, b_conv_ln, w_sb_out, w_conv_out, w_o, g_post_mix, g_pre_mlp, w_up, w_down, g_post_mlp, loss_target, m_g_pre_mix, m_w_in, m_b_in, m_w_dw, m_b_dw, m_g_conv_ln, m_b_conv_ln, m_w_sb_out, m_w_conv_out, m_w_o, m_g_post_mix, m_g_pre_mlp, m_w_up, m_w_down, m_g_post_mlp, v_g_pre_mix, v_w_in, v_b_in, v_w_dw, v_b_dw, v_g_conv_ln, v_b_conv_ln, v_w_sb_out, v_w_conv_out, v_w_o, v_g_post_mix, v_g_pre_mlp, v_w_up, v_w_down, v_g_post_mlp):
    xs = x[0]
    tgt = loss_target[0]
    s, d = xs.shape
    d_sb = w_sb_out.shape[1]
    dc = w_conv_out.shape[1]
    d_in = w_in.shape[2] * N_CHIPS
    n_heads = d_sb // HEAD_DIM
    off_a = 3 * d_sb
    off_sb = off_a + 2 * dc
    off_cv = off_sb + d
    gw = dc
    assert off_a % dc == 0 and off_sb % gw == 0 and d % gw == 0 and d_in == off_cv + d
    chip = 2 * lax.axis_index("x") + lax.axis_index("y")

    big = dict(w_in=w_in[0], w_sb_out=w_sb_out[0], w_conv_out=w_conv_out[0], w_o=w_o[0], w_up=w_up[0], w_down=w_down[0])
    names = list(big)
    col_sharded = {"w_in", "w_sb_out", "w_conv_out", "w_up"}
    where = jnp.stack([chip, lax.axis_index("c")]).astype(jnp.int32)
    cs = w_dw.shape[2]
    taps4 = _gather_taps(_pad_rows(w_dw[0], CONV_HALO))
    first, token = _gather_start([_cast_into_block(big["w_in"], where, taps4, "cast_w_in")], taps4, "gather_start_in")
    rest, _ = _gather_start([_cast_into_block(big[n], where, token, "cast_" + n) for n in names[1:]], token,
                            "gather_start_rest")
    in_flight = dict(zip(names, zip(*[a + b for a, b in zip(first, rest)])))
    gathered = {}

    def arrive(group, after, tag):
        thru, sends, recvs = zip(*[in_flight[n] for n in group])
        landed = _gather_wait(thru, sends, recvs, after, "gather_wait_" + tag)
        for n, g4 in zip(group, _gather_forward(landed, "gather_forward_" + tag)):
            gathered[n] = g4 if n in col_sharded else g4.reshape(g4.shape[0] * g4.shape[1], g4.shape[2])

    taps = jnp.transpose(taps4, (1, 0, 2)).reshape(CONV_HALO, N_CHIPS * cs)

    h = _rms_fwd(xs, g_pre_mix)
    arrive(["w_in"], h, "in")
    proj = _mm(h, gathered["w_in"], name="mm_proj", b_groups=N_CHIPS, bias=b_in, tn=768)
    arrive(["w_sb_out", "w_conv_out", "w_o"], proj, "mix")
    attn = _attn_fwd(proj, n_heads, d_sb)
    u0 = _glu(proj, off_a, dc)
    u1, u3 = _conv_fwd(u0, taps, b_dw, g_conv_ln, b_conv_ln)
    arrive(["w_up"], u3, "up")
    o_sb = _mm(attn, gathered["w_sb_out"], name="mm_o_sb", b_groups=N_CHIPS)
    o_cv = _mm(u3, gathered["w_conv_out"], name="mm_o_cv", b_groups=N_CHIPS)
    merged = _merge_fwd(proj, o_sb, o_cv, off_sb, off_cv, gw)
    y = _mm(merged, gathered["w_o"], name="mm_y")
    x2, h2 = _resid_rms(xs, y, g_post_mix, g_pre_mlp)
    up, f = _mm(h2, gathered["w_up"], name="mm_up", b_groups=N_CHIPS, out_dtypes=(F32, BF16),
                epilogue=lambda acc: (acc, jnp.square(jnp.maximum(acc, 0.0))))
    arrive(["w_down"], f, "down")
    dn = _mm(f, gathered["w_down"], name="mm_down")
    dx3, d_dn, dg_post_mlp, loss_part = _final(x2, dn, g_post_mlp, tgt)

    moments = dict(w_in=(m_w_in, v_w_in), w_sb_out=(m_w_sb_out, v_w_sb_out), w_conv_out=(m_w_conv_out, v_w_conv_out),
                   w_o=(m_w_o, v_w_o), w_up=(m_w_up, v_w_up), w_down=(m_w_down, v_w_down))
    out_g, out_d, out_m, out_v = {}, {}, {}, {}
    grads = {}

    def reduce_start(group, tag, after):
        blocks = []
        for n in group:
            g = grads[n]
            blocks.append(g if g.ndim == 3 else g.reshape(N_CHIPS, g.shape[0] // N_CHIPS, g.shape[1]))
        theirs = _exchange_halves(blocks, "exchange_halves_" + tag)
        pair_sums = [_add_pair(g, t, where, "pair_sum_" + n) for n, g, t in zip(group, blocks, theirs)]
        return _scatter_start(pair_sums, where if after is None else after, "scatter_start_" + tag)

    def reduce_finish(group, tag, started, after):
        parts, slots = _scatter_wait(*started, after, "scatter_wait_" + tag)
        halves = [_sum_slots(p, t, where, "chip_sum_" + n) for n, p, t in zip(group, parts, slots)]
        for n, red in zip(group, _share_halves(halves, "share_halves_" + tag)):
            mm_, vv_ = moments[n]
            res = _adamw(big[n], red, mm_[0], vv_[0], "adamw_" + n)
            out_g[n], out_d[n], out_m[n], out_v[n] = [r[None] for r in res]
        return res[1]

    group_mlp, group_mix, group_in = ["w_down", "w_up"], ["w_o", "w_sb_out", "w_conv_out"], ["w_in"]
    grads["w_down"] = _mm(f, d_dn, name="mm_dw_down", ta=True, out_dtypes=(BF16,))
    dup = _mm(d_dn, gathered["w_down"], name="mm_df", tb=True, extra=[up], out_dtypes=(BF16,),
              epilogue=lambda acc, upv: (acc * (2.0 * jnp.maximum(upv, 0.0)),))
    grads["w_up"] = _mm(h2, dup, name="mm_dw_up", ta=True, out_groups=N_CHIPS, out_dtypes=(BF16,))
    mlp_started, mlp_token = reduce_start(group_mlp, "mlp", None)
    dh2 = _mm(dup, gathered["w_up"], name="mm_dh2", tb=True, b_groups=N_CHIPS, after=mlp_token)
    dx2, dy, dg_pre_mlp, dg_post_mix = _rms_bwd2(dx3, dh2, x2, y, g_post_mix, g_pre_mlp)
    grads["w_o"] = _mm(merged, dy, name="mm_dw_o", ta=True, out_dtypes=(BF16,))
    dmerged = _mm(dy, gathered["w_o"], name="mm_dmerged", tb=True)
    do_sb, do_cv, dgate_sb, dgate_cv, s_gate_sb, s_gate_cv = _merge_bwd(proj, o_sb, o_cv, dmerged, off_sb, off_cv, gw)
    grads["w_sb_out"] = _mm(attn, do_sb, name="mm_dw_sb", ta=True, out_groups=N_CHIPS, out_dtypes=(BF16,))
    dattn = _mm(do_sb, gathered["w_sb_out"], name="mm_dattn", tb=True, b_groups=N_CHIPS)
    grads["w_conv_out"] = _mm(u3, do_cv, name="mm_dw_cv", ta=True, out_groups=N_CHIPS, out_dtypes=(BF16,))
    du3 = _mm(do_cv, gathered["w_conv_out"], name="mm_du3", tb=True, b_groups=N_CHIPS)
    mix_started, mix_token = reduce_start(group_mix, "mix", None)
    dq, dk, dv, s_q, s_k, s_v = _attn_bwd(proj, dattn, mix_token, n_heads, d_sb)
    du1, dg_conv_ln, db_conv_ln, db_dw = _conv_bwd_norm(u1, du3, g_conv_ln, b_conv_ln)
    dglu_a, dglu_b, s_a, s_b, dtaps = _conv_bwd_taps(du1, u0, proj, taps, off_a)
    dproj = jnp.concatenate([dq, dk, dv, dglu_a, dglu_b, dgate_sb, dgate_cv], axis=1)
    db_in = jnp.concatenate([s_q, s_k, s_v, s_a, s_b, s_gate_sb, s_gate_cv], axis=1)
    grads["w_in"] = _mm(h, dproj, name="mm_dw_in", ta=True, out_groups=N_CHIPS, out_dtypes=(BF16,), tn=768)
    dh = _mm(dproj, gathered["w_in"], name="mm_dh", tb=True, b_groups=N_CHIPS, tk=2304)
    grad_x, dg_pre_mix = _rms_bwd1(dx2, dh, xs, g_pre_mix)

    small = ["g_pre_mix", "b_in", "b_dw", "g_conv_ln", "b_conv_ln", "g_post_mix", "g_pre_mlp", "g_post_mlp"]
    small_w = dict(g_pre_mix=g_pre_mix, b_in=b_in, b_dw=b_dw, g_conv_ln=g_conv_ln, b_conv_ln=b_conv_ln,
                   g_post_mix=g_post_mix, g_pre_mlp=g_pre_mlp, g_post_mlp=g_post_mlp)
    small_m = dict(g_pre_mix=m_g_pre_mix, b_in=m_b_in, b_dw=m_b_dw, g_conv_ln=m_g_conv_ln, b_conv_ln=m_b_conv_ln,
                   g_post_mix=m_g_post_mix, g_pre_mlp=m_g_pre_mlp, g_post_mlp=m_g_post_mlp)
    small_v = dict(g_pre_mix=v_g_pre_mix, b_in=v_b_in, b_dw=v_b_dw, g_conv_ln=v_g_conv_ln, b_conv_ln=v_b_conv_ln,
                   g_post_mix=v_g_post_mix, g_pre_mlp=v_g_pre_mlp, g_post_mlp=v_g_post_mlp)
    small_g = dict(g_pre_mix=dg_pre_mix, b_in=db_in, b_dw=db_dw, g_conv_ln=dg_conv_ln, b_conv_ln=db_conv_ln,
                   g_post_mix=dg_post_mix, g_pre_mlp=dg_pre_mlp, g_post_mlp=dg_post_mlp)
    shapes = [small_w[n].shape for n in small]
    tail_shapes = [loss_part.shape, dtaps.shape]
    summed = _all_reduce_small(_pack([small_g[n] for n in small] + [loss_part, dtaps]))
    in_started, in_token = reduce_start(group_in, "in", summed)
    zeros_tail = [jnp.zeros(shp, F32) for shp in tail_shapes]
    res = _adamw(_pack([small_w[n] for n in small] + zeros_tail), summed,
                 _pack([small_m[n] for n in small] + zeros_tail), _pack([small_v[n] for n in small] + zeros_tail),
                 "adamw_small")
    unpacked = [_unpack(r, shapes + tail_shapes) for r in res]
    for i, n in enumerate(small):
        out_g[n], out_d[n], out_m[n], out_v[n] = [u[i] for u in unpacked]
    loss = unpacked[0][len(small)][0, 0]
    taps_grad = lax.dynamic_slice(unpacked[0][len(small) + 1], (0, chip * cs), (CONV_HALO, cs))
    res = _adamw(_pad_rows(w_dw[0], CONV_HALO), taps_grad, _pad_rows(m_w_dw[0], CONV_HALO), _pad_rows(v_w_dw[0], CONV_HALO),
                 "adamw_taps")
    out_g["w_dw"], out_d["w_dw"], out_m["w_dw"], out_v["w_dw"] = [r[:CONV_WIDTH][None] for r in res]
    done = reduce_finish(group_mlp, "mlp", mlp_started, in_token)
    done = reduce_finish(group_mix, "mix", mix_started, done)
    reduce_finish(group_in, "in", in_started, done)

    order = ["g_pre_mix", "w_in", "b_in", "w_dw", "b_dw", "g_conv_ln", "b_conv_ln", "w_sb_out", "w_conv_out", "w_o",
             "g_post_mix", "g_pre_mlp", "w_up", "w_down", "g_post_mlp"]
    return (loss, grad_x[None], *[out_g[n] for n in order], *[out_d[n] for n in order],
            *[out_m[n] for n in order], *[out_v[n] for n in order])
```

```python
import functools
import math

import jax
import jax.numpy as jnp
from jax import lax
from jax.experimental import pallas as pl
from jax.experimental.pallas import tpu as pltpu

F32 = jnp.float32
BF16 = jnp.bfloat16
MESH = pl.DeviceIdType.MESH

HEAD_DIM = 128
CONV_WIDTH = 31
CONV_HALO = 32
EPS = 1e-6
ADAM_LR = 0.001
ADAM_B1 = 0.9
ADAM_B2 = 0.999
ADAM_EPS = 1e-08
ADAM_WD = 0.01
ADAM_STEP = 10
N_CHIPS = 4
VMEM_LIMIT = 48 * 1024 * 1024


def _params(**kw):
    return pltpu.CompilerParams(vmem_limit_bytes=VMEM_LIMIT, **kw)


def _tile(n, want, align=128):
    if n <= want:
        return n
    for t in range(want - want % align, 0, -align):
        if n % t == 0:
            return t
    raise ValueError((n, want, align))


def _mm(a, b, *, name, ta=False, tb=False, b_groups=1, out_groups=1, bias=None, extra=None, epilogue=None,
        out_dtypes=(F32,), tm=1024, tn=1024, tk=2048, after=None):
    if ta:
        K, M = a.shape
    else:
        M, K = a.shape
    if b_groups == 1:
        br, bc = b.shape
    else:
        _, br, bcg = b.shape
        bc = bcg * b_groups
    if tb:
        N, Kb = br, bc
    else:
        Kb, N = br, bc
    assert Kb == K, (name, a.shape, b.shape)
    tm, tn, tk = _tile(M, tm), _tile(N, tn), _tile(K, tk)
    if b_groups > 1:
        if tb:
            tk = _tile(K // b_groups, tk)
        else:
            tn = _tile(N // b_groups, tn)
    if out_groups > 1:
        tn = _tile(N // out_groups, tn)
        if b_groups > 1 and not tb:
            tn = _tile(N // b_groups, tn)
    nm, nn, nk = M // tm, N // tn, K // tk
    extra = tuple(extra or ())
    n_extra = len(extra)
    has_bias = bias is not None
    n_out = len(out_dtypes)

    a_spec = pl.BlockSpec((tk, tm), lambda i, j, k: (k, i)) if ta else pl.BlockSpec((tm, tk), lambda i, j, k: (i, k))
    if b_groups == 1:
        b_spec = pl.BlockSpec((tn, tk), lambda i, j, k: (j, k)) if tb else pl.BlockSpec((tk, tn), lambda i, j, k: (k, j))
    elif tb:
        kpg = (K // b_groups) // tk
        b_spec = pl.BlockSpec((None, tn, tk), lambda i, j, k: (k // kpg, j, k % kpg))
    else:
        npg = (N // b_groups) // tn
        b_spec = pl.BlockSpec((None, tk, tn), lambda i, j, k: (j // npg, k, j % npg))
    in_specs = [a_spec, b_spec]
    operands = [a, b]
    if has_bias:
        in_specs.append(pl.BlockSpec((1, tn), lambda i, j, k: (0, j)))
        operands.append(bias)
    for e in extra:
        in_specs.append(pl.BlockSpec((tm, tn), lambda i, j, k: (i, j)))
        operands.append(e)
    n_after = 0 if after is None else 1
    if after is not None:
        in_specs.append(pl.BlockSpec(memory_space=pl.ANY))
        operands.append(after)
    if out_groups == 1:
        o_spec = pl.BlockSpec((tm, tn), lambda i, j, k: (i, j))
        o_shape = (M, N)
    else:
        opg = (N // out_groups) // tn
        o_spec = pl.BlockSpec((None, tm, tn), lambda i, j, k: (j // opg, i, j % opg))
        o_shape = (out_groups, M, N // out_groups)
    dims = (((0 if ta else 1,), (1 if tb else 0,)), ((), ()))

    def body(*refs):
        a_ref, b_ref = refs[0], refs[1]
        pos = 2
        bias_ref = None
        if has_bias:
            bias_ref = refs[pos]
            pos += 1
        extra_refs = refs[pos:pos + n_extra]
        pos += n_extra + n_after
        out_refs = refs[pos:pos + n_out]
        pos += n_out
        acc_ref = refs[pos] if nk > 1 else None

        part = lax.dot_general(a_ref[...].astype(BF16), b_ref[...].astype(BF16), dims, preferred_element_type=F32)

        def finish(acc):
            if has_bias:
                acc = acc + bias_ref[...]
            outs = epilogue(acc, *[e[...] for e in extra_refs]) if epilogue is not None else (acc,)
            for o_ref, o in zip(out_refs, outs):
                o_ref[...] = o.astype(o_ref.dtype)

        if nk == 1:
            finish(part)
        else:
            k = pl.program_id(2)

            @pl.when(k == 0)
            def _():
                acc_ref[...] = part

            @pl.when(k > 0)
            def _():
                acc_ref[...] += part

            @pl.when(k == nk - 1)
            def _():
                finish(acc_ref[...])

    outs = pl.pallas_call(
        body,
        name=name,
        grid=(nm, nn, nk),
        in_specs=in_specs,
        out_specs=[o_spec] * n_out,
        out_shape=[jax.ShapeDtypeStruct(o_shape, dt) for dt in out_dtypes],
        scratch_shapes=[pltpu.VMEM((tm, tn), F32)] if nk > 1 else [],
        compiler_params=_params(dimension_semantics=("parallel", "parallel", "arbitrary")),
    )(*operands)
    return outs[0] if n_out == 1 else outs


def _rms_stats(x):
    return lax.rsqrt(jnp.mean(x * x, axis=-1, keepdims=True) + EPS)


def _rms_bwd(x, r, g, dy):
    gy = dy * g
    return r * gy - x * (r * r * r) * jnp.mean(x * gy, axis=-1, keepdims=True)


def _sigmoid(x):
    return 1.0 / (1.0 + jnp.exp(-x))


def _row_call(body, *, name, rows, tr, ins, outs, acc_outs=()):
    n = rows // tr
    in_specs = []
    for arr, kind in ins:
        if kind == "row":
            in_specs.append(pl.BlockSpec((tr, arr.shape[1]), lambda i: (i, 0)))
        else:
            in_specs.append(pl.BlockSpec(arr.shape, lambda i: (0, 0)))
    out_specs = [pl.BlockSpec((tr, w), lambda i: (i, 0)) for w, _ in outs]
    out_shape = [jax.ShapeDtypeStruct((rows, w), dt) for w, dt in outs]
    out_specs += [pl.BlockSpec((1, w), lambda i: (0, 0)) for w in acc_outs]
    out_shape += [jax.ShapeDtypeStruct((1, w), F32) for w in acc_outs]
    return pl.pallas_call(
        body, name=name, grid=(n,), in_specs=in_specs, out_specs=out_specs, out_shape=out_shape,
        compiler_params=_params(dimension_semantics=("arbitrary",)),
    )(*[a for a, _ in ins])


def _accumulate(ref, val):
    @pl.when(pl.program_id(0) == 0)
    def _():
        ref[...] = val

    @pl.when(pl.program_id(0) > 0)
    def _():
        ref[...] += val


def _rms_fwd(x, g, tr=256):
    def body(x_ref, g_ref, h_ref):
        xv = x_ref[...]
        h_ref[...] = (xv * _rms_stats(xv) * g_ref[...]).astype(BF16)

    (h,) = _row_call(body, name="rms_fwd", rows=x.shape[0], tr=tr, ins=[(x, "row"), (g, "vec")], outs=[(x.shape[1], BF16)])
    return h


def _resid_rms(x, y, g2, g3, tr=256):
    def body(x_ref, y_ref, g2_ref, g3_ref, x2_ref, h2_ref):
        yv = y_ref[...]
        x2 = x_ref[...] + yv * _rms_stats(yv) * g2_ref[...]
        x2_ref[...] = x2
        h2_ref[...] = (x2 * _rms_stats(x2) * g3_ref[...]).astype(BF16)

    d = x.shape[1]
    return _row_call(body, name="resid_rms", rows=x.shape[0], tr=tr,
                     ins=[(x, "row"), (y, "row"), (g2, "vec"), (g3, "vec")], outs=[(d, F32), (d, BF16)])


def _final(x2, dn, g4, tgt, tr=256):
    d = x2.shape[1]

    def body(x2_ref, dn_ref, g4_ref, t_ref, dx3_ref, ddn_ref, dg4_ref, loss_ref):
        dn_v = dn_ref[...]
        r = _rms_stats(dn_v)
        g = g4_ref[...]
        e = x2_ref[...] + dn_v * r * g - t_ref[...]
        dx3 = e * (1.0 / d)
        dx3_ref[...] = dx3
        ddn_ref[...] = _rms_bwd(dn_v, r, g, dx3).astype(BF16)
        _accumulate(dg4_ref, jnp.sum(dx3 * dn_v * r, axis=0, keepdims=True))
        part = 0.5 * jnp.sum(jnp.mean(e * e, axis=-1, keepdims=True), axis=0, keepdims=True)
        _accumulate(loss_ref, jnp.broadcast_to(part, loss_ref.shape))

    return _row_call(body, name="final", rows=x2.shape[0], tr=tr,
                     ins=[(x2, "row"), (dn, "row"), (g4, "vec"), (tgt, "row")],
                     outs=[(d, F32), (d, BF16)], acc_outs=(d, 128))


def _rms_bwd2(dx3, dh2, x2, y, g2, g3, tr=256):
    d = x2.shape[1]

    def body(dx3_ref, dh2_ref, x2_ref, y_ref, g2_ref, g3_ref, dx2_ref, dy_ref, dg3_ref, dg2_ref):
        x2v, dh2v, yv = x2_ref[...], dh2_ref[...], y_ref[...]
        r3 = _rms_stats(x2v)
        dx2 = dx3_ref[...] + _rms_bwd(x2v, r3, g3_ref[...], dh2v)
        dx2_ref[...] = dx2
        _accumulate(dg3_ref, jnp.sum(dh2v * x2v * r3, axis=0, keepdims=True))
        r2 = _rms_stats(yv)
        dy_ref[...] = _rms_bwd(yv, r2, g2_ref[...], dx2).astype(BF16)
        _accumulate(dg2_ref, jnp.sum(dx2 * yv * r2, axis=0, keepdims=True))

    return _row_call(body, name="rms_bwd2", rows=x2.shape[0], tr=tr,
                     ins=[(dx3, "row"), (dh2, "row"), (x2, "row"), (y, "row"), (g2, "vec"), (g3, "vec")],
                     outs=[(d, F32), (d, BF16)], acc_outs=(d, d))


def _rms_bwd1(dx2, dh, x, g1, tr=256):
    d = x.shape[1]

    def body(dx2_ref, dh_ref, x_ref, g1_ref, gx_ref, dg1_ref):
        xv, dhv = x_ref[...], dh_ref[...]
        r = _rms_stats(xv)
        gx_ref[...] = dx2_ref[...] + _rms_bwd(xv, r, g1_ref[...], dhv)
        _accumulate(dg1_ref, jnp.sum(dhv * xv * r, axis=0, keepdims=True))

    return _row_call(body, name="rms_bwd1", rows=x.shape[0], tr=tr,
                     ins=[(dx2, "row"), (dh, "row"), (x, "row"), (g1, "vec")], outs=[(d, F32)], acc_outs=(d,))


def _merge_fwd(proj, o_sb, o_cv, off_sb, off_cv, gw, tr=256):
    s, d = o_sb.shape
    nj = d // gw
    b_sb, b_cv = off_sb // gw, off_cv // gw

    def body(gs_ref, gc_ref, osb_ref, ocv_ref, m_ref):
        m_ref[...] = (_sigmoid(gs_ref[...]) * osb_ref[...] + _sigmoid(gc_ref[...]) * ocv_ref[...]).astype(BF16)

    blk = lambda i, j: (i, j)
    return pl.pallas_call(
        body, name="merge_fwd", grid=(s // tr, nj),
        in_specs=[pl.BlockSpec((tr, gw), lambda i, j: (i, b_sb + j)), pl.BlockSpec((tr, gw), lambda i, j: (i, b_cv + j)),
                  pl.BlockSpec((tr, gw), blk), pl.BlockSpec((tr, gw), blk)],
        out_specs=pl.BlockSpec((tr, gw), blk), out_shape=jax.ShapeDtypeStruct((s, d), BF16),
        compiler_params=_params(dimension_semantics=("parallel", "parallel")),
    )(proj, proj, o_sb, o_cv)


def _merge_bwd(proj, o_sb, o_cv, dmerged, off_sb, off_cv, gw, tr=256):
    s, d = o_sb.shape
    nj = d // gw
    ni = s // tr
    b_sb, b_cv = off_sb // gw, off_cv // gw

    def body(gs_ref, gc_ref, osb_ref, ocv_ref, dm_ref, dosb_ref, docv_ref, dgs_ref, dgc_ref, sgs_ref, sgc_ref):
        dm = dm_ref[...]
        s_sb, s_cv = _sigmoid(gs_ref[...]), _sigmoid(gc_ref[...])
        dosb_ref[...] = (dm * s_sb).astype(BF16)
        docv_ref[...] = (dm * s_cv).astype(BF16)
        dgs = dm * osb_ref[...] * s_sb * (1.0 - s_sb)
        dgc = dm * ocv_ref[...] * s_cv * (1.0 - s_cv)
        dgs_ref[...] = dgs.astype(BF16)
        dgc_ref[...] = dgc.astype(BF16)
        i = pl.program_id(1)
        for ref, val in ((sgs_ref, dgs), (sgc_ref, dgc)):
            col = jnp.sum(val, axis=0, keepdims=True)

            @pl.when(i == 0)
            def _():
                ref[...] = col

            @pl.when(i > 0)
            def _():
                ref[...] += col

    blk = lambda j, i: (i, j)
    outs = pl.pallas_call(
        body, name="merge_bwd", grid=(nj, ni),
        in_specs=[pl.BlockSpec((tr, gw), lambda j, i: (i, b_sb + j)), pl.BlockSpec((tr, gw), lambda j, i: (i, b_cv + j)),
                  pl.BlockSpec((tr, gw), blk), pl.BlockSpec((tr, gw), blk), pl.BlockSpec((tr, gw), blk)],
        out_specs=[pl.BlockSpec((tr, gw), blk), pl.BlockSpec((tr, gw), blk),
                   pl.BlockSpec((tr, gw), blk), pl.BlockSpec((tr, gw), blk),
                   pl.BlockSpec((1, gw), lambda j, i: (0, j)), pl.BlockSpec((1, gw), lambda j, i: (0, j))],
        out_shape=[jax.ShapeDtypeStruct((s, d), BF16)] * 4 + [jax.ShapeDtypeStruct((1, d), F32)] * 2,
        compiler_params=_params(dimension_semantics=("parallel", "arbitrary")),
    )(proj, proj, o_sb, o_cv, dmerged)
    return outs


def _split_bf16(v):
    hi = v.astype(BF16)
    lo = (v - hi.astype(F32)).astype(BF16)
    return hi, lo


def _dot_nt(a, b):
    return lax.dot_general(a, b, (((1,), (1,)), ((), ())), preferred_element_type=F32)


def _dot_tn(a, b):
    return lax.dot_general(a, b, (((0,), (0,)), ((), ())), preferred_element_type=F32)


def _dot_nn(a, b):
    return lax.dot_general(a, b, (((1,), (0,)), ((), ())), preferred_element_type=F32)


def _sum_right(v, tri):
    hi, lo = _split_bf16(v)
    return _dot_nn(hi, tri) + _dot_nn(lo, tri)


HEADS_PER_STEP = 2


LOG2_E = 1.4426950408889634


def _sb_tile(q, kb, scale, mask):
    z = _dot_nt(q, kb) * (scale * LOG2_E)
    log_b = jnp.minimum(z, 0.0) - jnp.log2(1.0 + jnp.exp2(-jnp.abs(z)))
    log_1m = log_b - z
    if mask is not None:
        log_1m = jnp.where(mask, log_1m, 0.0)
    return log_b, log_1m


def _tile_mask(tq):
    row = lax.broadcasted_iota(jnp.int32, (tq, tq), 0)
    col = lax.broadcasted_iota(jnp.int32, (tq, tq), 1)
    return col < row


def _tri(tq, before):
    r = lax.broadcasted_iota(jnp.int32, (tq, tq), 0)
    c = lax.broadcasted_iota(jnp.int32, (tq, tq), 1)
    return jnp.where((r < c) if before else (r > c), 1.0, 0.0).astype(BF16)


def _attn_fwd(proj, n_heads, d_sb, tq=256):
    s = proj.shape[0]
    tq = _tile(s, tq)
    hb = d_sb // HEAD_DIM
    scale = 1.0 / math.sqrt(HEAD_DIM)

    heads = [pl.ds(n * HEAD_DIM, HEAD_DIM) for n in range(HEADS_PER_STEP)]
    wide = HEADS_PER_STEP * HEAD_DIM
    hb //= HEADS_PER_STEP

    def body(q_ref, k_ref, v_ref, o_ref):
        i = pl.program_id(1)
        tri = _tri(tq, False)

        def tile(j, carry, diagonal):
            rows = pl.ds(pl.multiple_of(j * tq, tq), tq)
            mask = _tile_mask(tq) if diagonal else None
            out = []
            for hd, (c_l, acc) in zip(heads, carry):
                log_b, log_1m = _sb_tile(q_ref[:, hd].astype(BF16), k_ref[rows, hd].astype(BF16), scale, mask)
                a = jnp.exp2(log_b + (c_l + _sum_right(log_1m, tri)))
                if diagonal:
                    a = jnp.where(mask, a, 0.0)
                out.append((c_l + jnp.sum(log_1m, axis=1, keepdims=True),
                            acc + _dot_nn(a.astype(BF16), v_ref[rows, hd].astype(BF16))))
            return tuple(out)

        init = tuple((jnp.zeros((tq, 1), F32), jnp.zeros((tq, HEAD_DIM), F32)) for _ in heads)
        carry = lax.fori_loop(0, i, lambda jj, c: tile(i - 1 - jj, c, False), tile(i, init, True))
        for hd, (_, acc) in zip(heads, carry):
            o_ref[:, hd] = acc.astype(BF16)

    return pl.pallas_call(
        body, name="attn_fwd", grid=(n_heads // HEADS_PER_STEP, s // tq),
        in_specs=[pl.BlockSpec((tq, wide), lambda h, i: (i, h)),
                  pl.BlockSpec((s, wide), lambda h, i: (0, hb + h)),
                  pl.BlockSpec((s, wide), lambda h, i: (0, 2 * hb + h))],
        out_specs=pl.BlockSpec((tq, wide), lambda h, i: (i, h)),
        out_shape=jax.ShapeDtypeStruct((s, d_sb), BF16),
        compiler_params=_params(dimension_semantics=("parallel", "arbitrary")),
    )(proj, proj, proj)


def _attn_bwd(proj, dattn, after, n_heads, d_sb, tq=256):
    s = proj.shape[0]
    tq = _tile(s, tq)
    nq = s // tq
    hb = d_sb // HEAD_DIM // HEADS_PER_STEP
    scale = 1.0 / math.sqrt(HEAD_DIM)
    heads = [pl.ds(n * HEAD_DIM, HEAD_DIM) for n in range(HEADS_PER_STEP)]
    wide = HEADS_PER_STEP * HEAD_DIM

    def body(q_ref, k_ref, v_ref, do_ref, _, dq_ref, dk_ref, dv_ref, sq_ref, sk_ref, sv_ref, dk_acc, dv_acc, g_st, b_st):
        i = pl.program_id(1)

        @pl.when(i == 0)
        def _():
            dk_acc[...] = jnp.zeros_like(dk_acc)
            dv_acc[...] = jnp.zeros_like(dv_acc)

        tri_after = _tri(tq, False)
        tri_before = _tri(tq, True)

        def newest_first(j, c_ls, diagonal):
            rows = pl.ds(pl.multiple_of(j * tq, tq), tq)
            mask = _tile_mask(tq) if diagonal else None
            out = []
            for n, (hd, c_l) in enumerate(zip(heads, c_ls)):
                do = do_ref[:, hd].astype(BF16)
                log_b, log_1m = _sb_tile(q_ref[:, hd].astype(BF16), k_ref[rows, hd].astype(BF16), scale, mask)
                a = jnp.exp2(log_b + (c_l + _sum_right(log_1m, tri_after)))
                beta = jnp.exp2(log_b)
                if diagonal:
                    a = jnp.where(mask, a, 0.0)
                    beta = jnp.where(mask, beta, 0.0)
                g_st[n, j] = a * _dot_nt(do, v_ref[rows, hd].astype(BF16))
                b_st[n, j] = beta
                dv_acc[rows, hd] += _dot_tn(a.astype(BF16), do)
                out.append(c_l + jnp.sum(log_1m, axis=1, keepdims=True))
            return tuple(out)

        lax.fori_loop(0, i, lambda jj, c: newest_first(i - 1 - jj, c, False),
                      newest_first(i, tuple(jnp.zeros((tq, 1), F32) for _ in heads), True))

        def oldest_first(j, carry):
            rows = pl.ds(pl.multiple_of(j * tq, tq), tq)
            out = []
            for n, (hd, (c_g, dq)) in enumerate(zip(heads, carry)):
                kb = k_ref[rows, hd].astype(BF16)
                g = g_st[n, j]
                beta = b_st[n, j]
                g_before = c_g + _sum_right(g, tri_before)
                dz16 = ((g * (1.0 - beta) - g_before * beta) * scale).astype(BF16)
                dk_acc[rows, hd] += _dot_tn(dz16, q_ref[:, hd].astype(BF16))
                out.append((c_g + jnp.sum(g, axis=1, keepdims=True), dq + _dot_nn(dz16, kb)))
            return tuple(out)

        init = tuple((jnp.zeros((tq, 1), F32), jnp.zeros((tq, HEAD_DIM), F32)) for _ in heads)
        carry = lax.fori_loop(0, i + 1, oldest_first, init)
        dq = jnp.concatenate([c[1] for c in carry], axis=1)
        dq_ref[...] = dq.astype(BF16)
        col = jnp.sum(dq, axis=0, keepdims=True)

        @pl.when(i == 0)
        def _():
            sq_ref[...] = col

        @pl.when(i > 0)
        def _():
            sq_ref[...] += col

        @pl.when(i == nq - 1)
        def _():
            dk = dk_acc[...]
            dv = dv_acc[...]
            dk_ref[...] = dk.astype(BF16)
            dv_ref[...] = dv.astype(BF16)
            sk_ref[...] = jnp.sum(dk, axis=0, keepdims=True)
            sv_ref[...] = jnp.sum(dv, axis=0, keepdims=True)

    qblk = pl.BlockSpec((tq, wide), lambda h, i: (i, h))
    kblk = pl.BlockSpec((s, wide), lambda h, i: (0, h))
    col_sum = pl.BlockSpec((1, wide), lambda h, i: (0, h))
    stash = pltpu.VMEM((HEADS_PER_STEP, nq, tq, tq), F32)
    outs = pl.pallas_call(
        body, name="attn_bwd", grid=(n_heads // HEADS_PER_STEP, nq),
        in_specs=[qblk,
                  pl.BlockSpec((s, wide), lambda h, i: (0, hb + h)),
                  pl.BlockSpec((s, wide), lambda h, i: (0, 2 * hb + h)),
                  qblk, pl.BlockSpec(memory_space=pl.ANY)],
        out_specs=[qblk, kblk, kblk, col_sum, col_sum, col_sum],
        out_shape=[jax.ShapeDtypeStruct((s, d_sb), BF16)] * 3 + [jax.ShapeDtypeStruct((1, d_sb), F32)] * 3,
        scratch_shapes=[pltpu.VMEM((s, wide), F32), pltpu.VMEM((s, wide), F32), stash, stash],
        compiler_params=_params(dimension_semantics=("parallel", "arbitrary")),
    )(proj, proj, proj, dattn, after)
    return outs


LANES = 128


def _glu(proj, off_a, dc, tr=256):
    s = proj.shape[0]
    ba = off_a // dc

    def body(a_ref, b_ref, u_ref):
        u_ref[...] = a_ref[...] * _sigmoid(b_ref[...])

    return pl.pallas_call(
        body, name="glu", grid=(s // tr,),
        in_specs=[pl.BlockSpec((tr, dc), lambda i: (i, ba)), pl.BlockSpec((tr, dc), lambda i: (i, ba + 1))],
        out_specs=pl.BlockSpec((tr, dc), lambda i: (i, 0)), out_shape=jax.ShapeDtypeStruct((s, dc), F32),
        compiler_params=_params(dimension_semantics=("parallel",)),
    )(proj, proj)


def _conv_fwd(u0, taps, b_dw, g_ln, b_ln, tt=256):
    s, dc = u0.shape
    tt = _tile(s, tt)
    hpb = tt // CONV_HALO

    def body(cur_ref, halo_ref, w_ref, bdw_ref, g_ref, b_ref, u1_ref, u3_ref, xs):
        i = pl.program_id(0)
        xs[pl.ds(0, CONV_HALO), :] = jnp.where(i > 0, halo_ref[...], 0.0)
        xs[pl.ds(CONV_HALO, tt), :] = cur_ref[...]
        for c0 in range(0, dc, LANES):
            cols = pl.ds(c0, LANES)
            acc = jnp.broadcast_to(bdw_ref[:, cols], (tt, LANES))
            for k in range(CONV_WIDTH):
                acc = acc + w_ref[pl.ds(k, 1), cols] * xs[pl.ds(CONV_HALO - (CONV_WIDTH - 1) + k, tt), cols]
            u1_ref[:, cols] = acc
        u1 = u1_ref[...]
        mu = jnp.mean(u1, axis=-1, keepdims=True)
        xc = u1 - mu
        rstd = lax.rsqrt(jnp.mean(xc * xc, axis=-1, keepdims=True) + EPS)
        u2 = xc * rstd * g_ref[...] + b_ref[...]
        u3_ref[...] = (u2 * _sigmoid(u2)).astype(BF16)

    vec = pl.BlockSpec((1, dc), lambda i: (0, 0))
    return pl.pallas_call(
        body, name="conv_fwd", grid=(s // tt,),
        in_specs=[pl.BlockSpec((tt, dc), lambda i: (i, 0)),
                  pl.BlockSpec((CONV_HALO, dc), lambda i: (jnp.maximum(i * hpb - 1, 0), 0)),
                  pl.BlockSpec((CONV_HALO, dc), lambda i: (0, 0)), vec, vec, vec],
        out_specs=[pl.BlockSpec((tt, dc), lambda i: (i, 0)), pl.BlockSpec((tt, dc), lambda i: (i, 0))],
        out_shape=[jax.ShapeDtypeStruct((s, dc), F32), jax.ShapeDtypeStruct((s, dc), BF16)],
        scratch_shapes=[pltpu.VMEM((tt + CONV_HALO, dc), F32)],
        compiler_params=_params(dimension_semantics=("arbitrary",)),
    )(u0, u0, taps, b_dw, g_ln, b_ln)


def _conv_bwd_norm(u1, du3, g_ln, b_ln, tr=256):
    dc = u1.shape[1]

    def body(u1_ref, du3_ref, g_ref, b_ref, du1_ref, dg_ref, db_ref, dbdw_ref):
        u1v = u1_ref[...]
        g = g_ref[...]
        mu = jnp.mean(u1v, axis=-1, keepdims=True)
        xc = u1v - mu
        rstd = lax.rsqrt(jnp.mean(xc * xc, axis=-1, keepdims=True) + EPS)
        xhat = xc * rstd
        u2 = xhat * g + b_ref[...]
        sg = _sigmoid(u2)
        du2 = du3_ref[...] * (sg * (1.0 + u2 * (1.0 - sg)))
        _accumulate(dg_ref, jnp.sum(du2 * xhat, axis=0, keepdims=True))
        _accumulate(db_ref, jnp.sum(du2, axis=0, keepdims=True))
        gy = du2 * g
        du1 = rstd * (gy - jnp.mean(gy, axis=-1, keepdims=True) - xhat * jnp.mean(gy * xhat, axis=-1, keepdims=True))
        du1_ref[...] = du1
        _accumulate(dbdw_ref, jnp.sum(du1, axis=0, keepdims=True))

    return _row_call(body, name="conv_bwd_norm", rows=u1.shape[0], tr=tr,
                     ins=[(u1, "row"), (du3, "row"), (g_ln, "vec"), (b_ln, "vec")],
                     outs=[(dc, F32)], acc_outs=(dc, dc, dc))


def _conv_bwd_taps(du1, u0, proj, taps, off_a, tt=256):
    s, dc = u0.shape
    tt = _tile(s, tt)
    n = s // tt
    hpb = tt // CONV_HALO
    ba = off_a // dc
    lead = CONV_HALO - (CONV_WIDTH - 1)

    def body(d_ref, dnext_ref, cur_ref, halo_ref, a_ref, b_ref, w_ref, da_ref, db_ref, sa_ref, sb_ref, dw_ref,
             xs, ds, du0):
        i = pl.program_id(0)
        xs[pl.ds(0, CONV_HALO), :] = jnp.where(i > 0, halo_ref[...], 0.0)
        xs[pl.ds(CONV_HALO, tt), :] = cur_ref[...]
        ds[pl.ds(0, tt), :] = d_ref[...]
        ds[pl.ds(tt, CONV_HALO), :] = jnp.where(i < n - 1, dnext_ref[...], 0.0)

        @pl.when(i == 0)
        def _():
            dw_ref[...] = jnp.zeros_like(dw_ref)

        for c0 in range(0, dc, LANES):
            cols = pl.ds(c0, LANES)
            d_cur = ds[pl.ds(0, tt), cols]
            acc = jnp.zeros((tt, LANES), F32)
            for k in range(CONV_WIDTH):
                acc = acc + w_ref[pl.ds(k, 1), cols] * ds[pl.ds(CONV_WIDTH - 1 - k, tt), cols]
                dw_ref[pl.ds(k, 1), cols] += jnp.sum(d_cur * xs[pl.ds(lead + k, tt), cols], axis=0, keepdims=True)
            du0[:, cols] = acc
        d0 = du0[...]
        sg = _sigmoid(b_ref[...])
        da = d0 * sg
        db = d0 * a_ref[...] * sg * (1.0 - sg)
        da_ref[...] = da.astype(BF16)
        db_ref[...] = db.astype(BF16)
        _accumulate(sa_ref, jnp.sum(da, axis=0, keepdims=True))
        _accumulate(sb_ref, jnp.sum(db, axis=0, keepdims=True))

    row = pl.BlockSpec((tt, dc), lambda i: (i, 0))
    vec = pl.BlockSpec((1, dc), lambda i: (0, 0))
    taps_spec = pl.BlockSpec((CONV_HALO, dc), lambda i: (0, 0))
    return pl.pallas_call(
        body, name="conv_bwd_taps", grid=(n,),
        in_specs=[row, pl.BlockSpec((CONV_HALO, dc), lambda i: (jnp.minimum((i + 1) * hpb, n * hpb - 1), 0)),
                  row, pl.BlockSpec((CONV_HALO, dc), lambda i: (jnp.maximum(i * hpb - 1, 0), 0)),
                  pl.BlockSpec((tt, dc), lambda i: (i, ba)), pl.BlockSpec((tt, dc), lambda i: (i, ba + 1)), taps_spec],
        out_specs=[row, row, vec, vec, taps_spec],
        out_shape=[jax.ShapeDtypeStruct((s, dc), BF16), jax.ShapeDtypeStruct((s, dc), BF16),
                   jax.ShapeDtypeStruct((1, dc), F32), jax.ShapeDtypeStruct((1, dc), F32),
                   jax.ShapeDtypeStruct((CONV_HALO, dc), F32)],
        scratch_shapes=[pltpu.VMEM((tt + CONV_HALO, dc), F32), pltpu.VMEM((tt + CONV_HALO, dc), F32),
                        pltpu.VMEM((tt, dc), F32)],
        compiler_params=_params(dimension_semantics=("arbitrary",)),
    )(du1, du1, u0, u0, proj, proj, taps)


def _elementwise(body, *, name, ins, out_dtypes, tr=128):
    rows, cols = ins[0].shape
    tr = _tile(rows, tr, 8)
    spec = pl.BlockSpec((tr, cols), lambda i: (i, 0))
    return pl.pallas_call(
        body, name=name, grid=(rows // tr,), in_specs=[spec] * len(ins), out_specs=[spec] * len(out_dtypes),
        out_shape=[jax.ShapeDtypeStruct((rows, cols), dt) for dt in out_dtypes],
        compiler_params=_params(dimension_semantics=("parallel",)),
    )(*ins)


def _prefetched(body, *, name, where, grid, in_specs, out_specs, out_shape, operands):
    return pl.pallas_call(
        body, name=name, out_shape=out_shape,
        grid_spec=pltpu.PrefetchScalarGridSpec(num_scalar_prefetch=1, grid=grid, in_specs=in_specs, out_specs=out_specs),
        compiler_params=_params(dimension_semantics=("parallel",) * len(grid)),
    )(where, *operands)


def _cast_into_block(w, where, after, name, tr=256):
    rows, cols = w.shape
    tr = _tile(rows, tr, 16)

    def body(where_ref, w_ref, _, o_ref):
        o_ref[...] = w_ref[...].astype(BF16)

    return _prefetched(
        body, name=name, where=where, grid=(rows // tr,),
        in_specs=[pl.BlockSpec((tr, cols), lambda i, wh: (i, 0)), pl.BlockSpec(memory_space=pl.ANY)],
        out_specs=pl.BlockSpec((None, tr, cols), lambda i, wh: (wh[0], i, 0)),
        out_shape=jax.ShapeDtypeStruct((N_CHIPS, rows, cols), BF16), operands=[w, after])


def _add_pair(grad, theirs, where, name, tr=256):
    nb, half, cols = theirs.shape
    tr = _tile(half, tr, 16)
    nh = half // tr

    def body(where_ref, a_ref, b_ref, o_ref):
        o_ref[...] = (a_ref[...].astype(F32) + b_ref[...].astype(F32)).astype(BF16)

    blk = pl.BlockSpec((None, tr, cols), lambda b, i, wh: (b, i, 0))
    return _prefetched(
        body, name=name, where=where, grid=(nb, nh),
        in_specs=[pl.BlockSpec((None, tr, cols), lambda b, i, wh: (b, wh[1] * nh + i, 0)), blk],
        out_specs=blk, out_shape=jax.ShapeDtypeStruct(theirs.shape, BF16), operands=[grad, theirs])


def _sum_slots(part, slots, where, name, tr=256):
    _, half, cols = part.shape
    tr = _tile(half, tr, 16)
    nh = half // tr

    def body(where_ref, p_ref, s_ref, o_ref):
        o_ref[...] = ((p_ref[...].astype(F32) + s_ref[0].astype(F32)) + s_ref[1].astype(F32)) + s_ref[2].astype(F32)

    return _prefetched(
        body, name=name, where=where, grid=(nh,),
        in_specs=[pl.BlockSpec((None, tr, cols), lambda i, wh: (wh[0], i, 0)),
                  pl.BlockSpec((3, tr, cols), lambda i, wh: (0, i, 0))],
        out_specs=pl.BlockSpec((tr, cols), lambda i, wh: (wh[1] * nh + i, 0)),
        out_shape=jax.ShapeDtypeStruct((2 * half, cols), F32), operands=[part, slots])


def _adamw(w, g, m, v, name):
    bc1 = 1.0 - ADAM_B1 ** ADAM_STEP
    bc2 = 1.0 - ADAM_B2 ** ADAM_STEP

    def body(w_ref, g_ref, m_ref, v_ref, go_ref, d_ref, mo_ref, vo_ref):
        g = g_ref[...]
        m_new = ADAM_B1 * m_ref[...] + (1.0 - ADAM_B1) * g
        v_new = ADAM_B2 * v_ref[...] + (1.0 - ADAM_B2) * (g * g)
        m_hat = m_new / bc1
        v_hat = v_new / bc2
        go_ref[...] = g
        d_ref[...] = -ADAM_LR * (m_hat / (jnp.sqrt(v_hat) + ADAM_EPS) + ADAM_WD * w_ref[...])
        mo_ref[...] = m_new
        vo_ref[...] = v_new

    return _elementwise(body, name=name, ins=[w, g, m, v], out_dtypes=[F32] * 4, tr=128)


ANY = pl.BlockSpec(memory_space=pl.ANY)


def _mesh_pos():
    return lax.axis_index("x"), lax.axis_index("y"), lax.axis_index("c")


def _other_chips(x, y):
    return [(1 - x, y), (x, 1 - y), (1 - x, 1 - y)]


def _remote(src, dst, send_sems, recv_sems, idx, to):
    return pltpu.make_async_remote_copy(src_ref=src, dst_ref=dst, send_sem=send_sems.at[idx], recv_sem=recv_sems.at[idx],
                                        device_id=to, device_id_type=MESH)


HBM = pl.BlockSpec(memory_space=pltpu.HBM)
SEM = pl.BlockSpec(memory_space=pltpu.SEMAPHORE)
EFFECT = pltpu.SideEffectType.DATAFLOW_SIDE_EFFECTING


def _in_hbm(arrays):
    return [pltpu.with_memory_space_constraint(a, pltpu.HBM) for a in arrays]


def _ici_copies(slab, send_sems, recv_sems, x, y, c):
    half = slab.shape[1] // 2
    rows = pl.ds(c * half, half)
    mine = slab.at[2 * x + y, rows]
    out = []
    for j, (cx, cy) in enumerate(_other_chips(x, y)):
        got = slab.at[2 * cx + cy, rows]
        out.append((_remote(mine, mine, send_sems, recv_sems, j, (cx, cy, c)),
                    _remote(got, got, send_sems, recv_sems, j, (cx, cy, c))))
    return out


def _gather_start(slabs, after, name):
    n = len(slabs)

    def body(*refs):
        ins = refs[:n]
        send_sems, recv_sems, token = refs[2 * n + 1:3 * n + 1], refs[3 * n + 1:4 * n + 1], refs[4 * n + 1]
        x, y, c = _mesh_pos()
        for w in range(n):
            for send, _ in _ici_copies(ins[w], send_sems[w], recv_sems[w], x, y, c):
                send.start()
        token[...] = jnp.zeros_like(token)

    outs = pl.pallas_call(
        body, name=name, in_specs=[HBM] * n + [ANY],
        out_specs=[HBM] * n + [SEM] * (2 * n) + [pl.BlockSpec(memory_space=pltpu.VMEM)],
        out_shape=[pltpu.HBM(s.shape, s.dtype) for s in slabs] + [pltpu.SemaphoreType.DMA((3,))] * (2 * n)
        + [jax.ShapeDtypeStruct((8, 128), F32)],
        input_output_aliases={w: w for w in range(n)},
        compiler_params=_params(has_side_effects=EFFECT),
    )(*_in_hbm(slabs), after)
    return (outs[:n], outs[n:2 * n], outs[2 * n:3 * n]), outs[3 * n]


def _gather_wait(slabs, send_sems, recv_sems, after, name):
    n = len(slabs)

    def body(*refs):
        ins = refs[:n]
        sends, recvs = refs[n:2 * n], refs[2 * n:3 * n]
        x, y, c = _mesh_pos()
        for w in range(n):
            for send, recv in _ici_copies(ins[w], sends[w], recvs[w], x, y, c):
                send.wait_send()
                recv.wait_recv()

    return pl.pallas_call(
        body, name=name, in_specs=[HBM] * n + [SEM] * (2 * n) + [ANY], out_specs=[HBM] * n,
        out_shape=[pltpu.HBM(s.shape, s.dtype) for s in slabs],
        input_output_aliases={w: w for w in range(n)},
        compiler_params=_params(has_side_effects=EFFECT),
    )(*slabs, *send_sems, *recv_sems, after)


def _gather_forward(slabs, name):
    n = len(slabs)

    def body(*refs):
        outs = refs[n:2 * n]
        send_sems, recv_sems = refs[2 * n:]
        x, y, c = _mesh_pos()
        sibling = (x, y, 1 - c)
        chips = _other_chips(x, y)
        started = []
        for w in range(n):
            half = outs[w].shape[1] // 2
            for j, (cx, cy) in enumerate(chips):
                got = outs[w].at[2 * cx + cy, pl.ds(c * half, half)]
                fw = _remote(got, got, send_sems, recv_sems, 3 * w + j, sibling)
                fw.start()
                started.append(fw)
        for w in range(n):
            half = outs[w].shape[1] // 2
            for j, (cx, cy) in enumerate(chips):
                got = outs[w].at[2 * cx + cy, pl.ds((1 - c) * half, half)]
                _remote(got, got, send_sems, recv_sems, 3 * w + j, sibling).wait_recv()
        for cp in started:
            cp.wait_send()

    return pl.pallas_call(
        body, name=name, in_specs=[ANY] * n, out_specs=[ANY] * n,
        out_shape=[jax.ShapeDtypeStruct(s.shape, s.dtype) for s in slabs],
        input_output_aliases={w: w for w in range(n)},
        scratch_shapes=[pltpu.SemaphoreType.DMA((3 * n,)), pltpu.SemaphoreType.DMA((3 * n,))],
        compiler_params=_params(has_side_effects=True),
    )(*slabs)


def _gather_taps(taps):
    def body(in_ref, out_ref, send_sems, recv_sems, local_sem):
        x, y, c = _mesh_pos()
        me = 2 * x + y
        chips = _other_chips(x, y)
        lc = pltpu.make_async_copy(in_ref, out_ref.at[me], local_sem)
        lc.start()
        sends = [_remote(in_ref, out_ref.at[me], send_sems, recv_sems, j, (cx, cy, c)) for j, (cx, cy) in enumerate(chips)]
        for cp in sends:
            cp.start()
        for j, (cx, cy) in enumerate(chips):
            got = out_ref.at[2 * cx + cy]
            _remote(got, got, send_sems, recv_sems, j, (cx, cy, c)).wait_recv()
        for cp in sends:
            cp.wait_send()
        lc.wait()

    vm = pl.BlockSpec(memory_space=pltpu.VMEM)
    return pl.pallas_call(
        body, name="gather_taps", in_specs=[vm], out_specs=vm,
        out_shape=jax.ShapeDtypeStruct((N_CHIPS,) + taps.shape, taps.dtype),
        scratch_shapes=[pltpu.SemaphoreType.DMA((3,)), pltpu.SemaphoreType.DMA((3,)), pltpu.SemaphoreType.DMA],
        compiler_params=_params(has_side_effects=True),
    )(taps)


def _exchange_halves(grads, name):
    n = len(grads)

    def body(*refs):
        ins, their_refs = refs[:n], refs[n:2 * n]
        send_sems, recv_sems = refs[2 * n:]
        x, y, c = _mesh_pos()
        sibling = (x, y, 1 - c)
        sends = []
        for w in range(n):
            half = ins[w].shape[1] // 2
            cp = _remote(ins[w].at[:, pl.ds((1 - c) * half, half), :], their_refs[w], send_sems, recv_sems, w, sibling)
            cp.start()
            sends.append(cp)
        for w in range(n):
            _remote(their_refs[w], their_refs[w], send_sems, recv_sems, w, sibling).wait_recv()
        for cp in sends:
            cp.wait_send()

    halves = [jax.ShapeDtypeStruct((g.shape[0], g.shape[1] // 2, g.shape[2]), g.dtype) for g in grads]
    return pl.pallas_call(
        body, name=name, in_specs=[ANY] * n, out_specs=[ANY] * n, out_shape=halves,
        scratch_shapes=[pltpu.SemaphoreType.DMA((n,)), pltpu.SemaphoreType.DMA((n,))],
        compiler_params=_params(has_side_effects=True),
    )(*grads)


def _scatter_copies(part, slots, send_sems, recv_sems, x, y, c):
    out = []
    for j, (cx, cy) in enumerate(_other_chips(x, y)):
        out.append((_remote(part.at[2 * cx + cy], slots.at[j], send_sems, recv_sems, j, (cx, cy, c)),
                    _remote(slots.at[j], slots.at[j], send_sems, recv_sems, j, (cx, cy, c))))
    return out


def _scatter_start(parts, after, name):
    n = len(parts)
    slots = [lax.empty((3,) + p.shape[1:], p.dtype) for p in parts]

    def body(*refs):
        ins, lands = refs[:n], refs[n:2 * n]
        send_sems, recv_sems, token = refs[4 * n + 1:5 * n + 1], refs[5 * n + 1:6 * n + 1], refs[6 * n + 1]
        x, y, c = _mesh_pos()
        for w in range(n):
            for send, _ in _scatter_copies(ins[w], lands[w], send_sems[w], recv_sems[w], x, y, c):
                send.start()
        token[...] = jnp.zeros_like(token)

    outs = pl.pallas_call(
        body, name=name, in_specs=[HBM] * (2 * n) + [ANY],
        out_specs=[HBM] * (2 * n) + [SEM] * (2 * n) + [pl.BlockSpec(memory_space=pltpu.VMEM)],
        out_shape=[pltpu.HBM(a.shape, a.dtype) for a in list(parts) + slots] + [pltpu.SemaphoreType.DMA((3,))] * (2 * n)
        + [jax.ShapeDtypeStruct((8, 128), F32)],
        input_output_aliases={k: k for k in range(2 * n)},
        compiler_params=_params(has_side_effects=EFFECT),
    )(*_in_hbm(list(parts) + slots), after)
    return (outs[:n], outs[n:2 * n], outs[2 * n:3 * n], outs[3 * n:4 * n]), outs[4 * n]


def _scatter_wait(parts, slots, send_sems, recv_sems, after, name):
    n = len(parts)

    def body(*refs):
        ins, lands = refs[:n], refs[n:2 * n]
        sends, recvs = refs[2 * n:3 * n], refs[3 * n:4 * n]
        x, y, c = _mesh_pos()
        for w in range(n):
            for send, recv in _scatter_copies(ins[w], lands[w], sends[w], recvs[w], x, y, c):
                send.wait_send()
                recv.wait_recv()

    outs = pl.pallas_call(
        body, name=name, in_specs=[HBM] * (2 * n) + [SEM] * (2 * n) + [ANY], out_specs=[HBM] * (2 * n),
        out_shape=[pltpu.HBM(a.shape, a.dtype) for a in list(parts) + list(slots)],
        input_output_aliases={k: k for k in range(2 * n)},
        compiler_params=_params(has_side_effects=EFFECT),
    )(*parts, *slots, *send_sems, *recv_sems, after)
    return outs[:n], outs[n:]


def _share_halves(halves, name):
    n = len(halves)

    def body(*refs):
        outs = refs[n:2 * n]
        send_sems, recv_sems = refs[2 * n:]
        x, y, c = _mesh_pos()
        sibling = (x, y, 1 - c)
        sends = []
        for w in range(n):
            h = outs[w].shape[0] // 2
            mine = outs[w].at[pl.ds(c * h, h)]
            cp = _remote(mine, mine, send_sems, recv_sems, w, sibling)
            cp.start()
            sends.append(cp)
        for w in range(n):
            h = outs[w].shape[0] // 2
            got = outs[w].at[pl.ds((1 - c) * h, h)]
            _remote(got, got, send_sems, recv_sems, w, sibling).wait_recv()
        for cp in sends:
            cp.wait_send()

    return pl.pallas_call(
        body, name=name, in_specs=[ANY] * n, out_specs=[ANY] * n,
        out_shape=[jax.ShapeDtypeStruct(h.shape, h.dtype) for h in halves],
        input_output_aliases={w: w for w in range(n)},
        scratch_shapes=[pltpu.SemaphoreType.DMA((n,)), pltpu.SemaphoreType.DMA((n,))],
        compiler_params=_params(has_side_effects=True),
    )(*halves)


def _all_reduce_small(buf):
    rows, width = buf.shape

    def body(in_ref, out_ref, gathered, send_sems, recv_sems):
        x, y, c = _mesh_pos()
        me = 4 * x + 2 * y + c
        gathered[me] = in_ref[...]
        flips = [(dx, dy, dz) for dx in (0, 1) for dy in (0, 1) for dz in (0, 1)][1:]
        peers = [((1 - x) if dx else x, (1 - y) if dy else y, (1 - c) if dz else c) for dx, dy, dz in flips]
        sends = [_remote(in_ref, gathered.at[me], send_sems, recv_sems, k, peer) for k, peer in enumerate(peers)]
        for cp in sends:
            cp.start()
        for k, (px, py, pc) in enumerate(peers):
            got = gathered.at[4 * px + 2 * py + pc]
            _remote(got, got, send_sems, recv_sems, k, (px, py, pc)).wait_recv()
        for cp in sends:
            cp.wait_send()
        total = gathered[0]
        for d in range(1, 8):
            total = total + gathered[d]
        out_ref[...] = total

    vm = pl.BlockSpec(memory_space=pltpu.VMEM)
    return pl.pallas_call(
        body, name="all_reduce_small", in_specs=[vm], out_specs=vm, out_shape=jax.ShapeDtypeStruct(buf.shape, F32),
        scratch_shapes=[pltpu.VMEM((8, rows, width), F32), pltpu.SemaphoreType.DMA((7,)), pltpu.SemaphoreType.DMA((7,))],
        compiler_params=_params(has_side_effects=True),
    )(buf)


PACK_W = 2048
PACK_ROWS = 8


def _pack(vectors):
    flat = jnp.concatenate([v.reshape(-1) for v in vectors])
    unit = PACK_W * PACK_ROWS
    total = -(-flat.shape[0] // unit) * unit
    return jnp.pad(flat, (0, total - flat.shape[0])).reshape(total // PACK_W, PACK_W)


def _unpack(buf, shapes):
    flat = buf.reshape(-1)
    out, pos = [], 0
    for shp in shapes:
        size = math.prod(shp)
        out.append(flat[pos:pos + size].reshape(shp))
        pos += size
    return out


def _pad_rows(a, rows):
    return jnp.pad(a, ((0, rows - a.shape[0]), (0, 0)))


def kernel(x, g_pre_mix, w_in, b_in, w_dw, b_dw, g_conv_ln, b_conv_ln, w_sb_out, w_conv_out, w_o, g_post_mix, g_pre_mlp, w_up, w_down, g_post_mlp, loss_target, m_g_pre_mix, m_w_in, m_b_in, m_w_dw, m_b_dw, m_g_conv_ln, m_b_conv_ln, m_w_sb_out, m_w_conv_out, m_w_o, m_g_post_mix, m_g_pre_mlp, m_w_up, m_w_down, m_g_post_mlp, v_g_pre_mix, v_w_in, v_b_in, v_w_dw, v_b_dw, v_g_conv_ln, v_b_conv_ln, v_w_sb_out, v_w_conv_out, v_w_o, v_g_post_mix, v_g_pre_mlp, v_w_up, v_w_down, v_g_post_mlp):
    xs = x[0]
    tgt = loss_target[0]
    s, d = xs.shape
    d_sb = w_sb_out.shape[1]
    dc = w_conv_out.shape[1]
    d_in = w_in.shape[2] * N_CHIPS
    n_heads = d_sb // HEAD_DIM
    off_a = 3 * d_sb
    off_sb = off_a + 2 * dc
    off_cv = off_sb + d
    gw = dc
    assert off_a % dc == 0 and off_sb % gw == 0 and d % gw == 0 and d_in == off_cv + d
    chip = 2 * lax.axis_index("x") + lax.axis_index("y")

    big = dict(w_in=w_in[0], w_sb_out=w_sb_out[0], w_conv_out=w_conv_out[0], w_o=w_o[0], w_up=w_up[0], w_down=w_down[0])
    names = list(big)
    col_sharded = {"w_in", "w_sb_out", "w_conv_out", "w_up"}
    where = jnp.stack([chip, lax.axis_index("c")]).astype(jnp.int32)
    cs = w_dw.shape[2]
    taps4 = _gather_taps(_pad_rows(w_dw[0], CONV_HALO))
    first, token = _gather_start([_cast_into_block(big["w_in"], where, taps4, "cast_w_in")], taps4, "gather_start_in")
    rest, _ = _gather_start([_cast_into_block(big[n], where, token, "cast_" + n) for n in names[1:]], token,
                            "gather_start_rest")
    in_flight = dict(zip(names, zip(*[a + b for a, b in zip(first, rest)])))
    gathered = {}

    def arrive(group, after, tag):
        thru, sends, recvs = zip(*[in_flight[n] for n in group])
        landed = _gather_wait(thru, sends, recvs, after, "gather_wait_" + tag)
        for n, g4 in zip(group, _gather_forward(landed, "gather_forward_" + tag)):
            gathered[n] = g4 if n in col_sharded else g4.reshape(g4.shape[0] * g4.shape[1], g4.shape[2])

    taps = jnp.transpose(taps4, (1, 0, 2)).reshape(CONV_HALO, N_CHIPS * cs)

    h = _rms_fwd(xs, g_pre_mix)
    arrive(["w_in"], h, "in")
    proj = _mm(h, gathered["w_in"], name="mm_proj", b_groups=N_CHIPS, bias=b_in, tn=768)
    arrive(["w_sb_out", "w_conv_out", "w_o"], proj, "mix")
    attn = _attn_fwd(proj, n_heads, d_sb)
    u0 = _glu(proj, off_a, dc)
    u1, u3 = _conv_fwd(u0, taps, b_dw, g_conv_ln, b_conv_ln)
    o_sb = _mm(attn, gathered["w_sb_out"], name="mm_o_sb", b_groups=N_CHIPS)
    arrive(["w_up"], o_sb, "up")
    o_cv = _mm(u3, gathered["w_conv_out"], name="mm_o_cv", b_groups=N_CHIPS)
    merged = _merge_fwd(proj, o_sb, o_cv, off_sb, off_cv, gw)
    y = _mm(merged, gathered["w_o"], name="mm_y")
    x2, h2 = _resid_rms(xs, y, g_post_mix, g_pre_mlp)
    up, f = _mm(h2, gathered["w_up"], name="mm_up", b_groups=N_CHIPS, out_dtypes=(F32, BF16),
                epilogue=lambda acc: (acc, jnp.square(jnp.maximum(acc, 0.0))))
    arrive(["w_down"], f, "down")
    dn = _mm(f, gathered["w_down"], name="mm_down")
    dx3, d_dn, dg_post_mlp, loss_part = _final(x2, dn, g_post_mlp, tgt)

    moments = dict(w_in=(m_w_in, v_w_in), w_sb_out=(m_w_sb_out, v_w_sb_out), w_conv_out=(m_w_conv_out, v_w_conv_out),
                   w_o=(m_w_o, v_w_o), w_up=(m_w_up, v_w_up), w_down=(m_w_down, v_w_down))
    out_g, out_d, out_m, out_v = {}, {}, {}, {}
    grads = {}

    def reduce_start(group, tag, after):
        blocks = []
        for n in group:
            g = grads[n]
            blocks.append(g if g.ndim == 3 else g.reshape(N_CHIPS, g.shape[0] // N_CHIPS, g.shape[1]))
        theirs = _exchange_halves(blocks, "exchange_halves_" + tag)
        pair_sums = [_add_pair(g, t, where, "pair_sum_" + n) for n, g, t in zip(group, blocks, theirs)]
        return _scatter_start(pair_sums, where if after is None else after, "scatter_start_" + tag)

    def reduce_finish(group, tag, started, after):
        parts, slots = _scatter_wait(*started, after, "scatter_wait_" + tag)
        halves = [_sum_slots(p, t, where, "chip_sum_" + n) for n, p, t in zip(group, parts, slots)]
        for n, red in zip(group, _share_halves(halves, "share_halves_" + tag)):
            mm_, vv_ = moments[n]
            res = _adamw(big[n], red, mm_[0], vv_[0], "adamw_" + n)
            out_g[n], out_d[n], out_m[n], out_v[n] = [r[None] for r in res]
        return res[1]

    group_mlp, group_mix, group_in = ["w_down", "w_up"], ["w_o", "w_sb_out", "w_conv_out"], ["w_in"]
    grads["w_down"] = _mm(f, d_dn, name="mm_dw_down", ta=True, out_dtypes=(BF16,))
    dup = _mm(d_dn, gathered["w_down"], name="mm_df", tb=True, extra=[up], out_dtypes=(BF16,),
              epilogue=lambda acc, upv: (acc * (2.0 * jnp.maximum(upv, 0.0)),))
    grads["w_up"] = _mm(h2, dup, name="mm_dw_up", ta=True, out_groups=N_CHIPS, out_dtypes=(BF16,))
    mlp_started, mlp_token = reduce_start(group_mlp, "mlp", None)
    dh2 = _mm(dup, gathered["w_up"], name="mm_dh2", tb=True, b_groups=N_CHIPS, after=mlp_token)
    dx2, dy, dg_pre_mlp, dg_post_mix = _rms_bwd2(dx3, dh2, x2, y, g_post_mix, g_pre_mlp)
    grads["w_o"] = _mm(merged, dy, name="mm_dw_o", ta=True, out_dtypes=(BF16,))
    dmerged = _mm(dy, gathered["w_o"], name="mm_dmerged", tb=True)
    do_sb, do_cv, dgate_sb, dgate_cv, s_gate_sb, s_gate_cv = _merge_bwd(proj, o_sb, o_cv, dmerged, off_sb, off_cv, gw)
    grads["w_sb_out"] = _mm(attn, do_sb, name="mm_dw_sb", ta=True, out_groups=N_CHIPS, out_dtypes=(BF16,))
    dattn = _mm(do_sb, gathered["w_sb_out"], name="mm_dattn", tb=True, b_groups=N_CHIPS)
    grads["w_conv_out"] = _mm(u3, do_cv, name="mm_dw_cv", ta=True, out_groups=N_CHIPS, out_dtypes=(BF16,))
    du3 = _mm(do_cv, gathered["w_conv_out"], name="mm_du3", tb=True, b_groups=N_CHIPS)
    mix_started, mix_token = reduce_start(group_mix, "mix", None)
    dq, dk, dv, s_q, s_k, s_v = _attn_bwd(proj, dattn, mix_token, n_heads, d_sb)
    du1, dg_conv_ln, db_conv_ln, db_dw = _conv_bwd_norm(u1, du3, g_conv_ln, b_conv_ln)
    dglu_a, dglu_b, s_a, s_b, dtaps = _conv_bwd_taps(du1, u0, proj, taps, off_a)
    dproj = jnp.concatenate([dq, dk, dv, dglu_a, dglu_b, dgate_sb, dgate_cv], axis=1)
    db_in = jnp.concatenate([s_q, s_k, s_v, s_a, s_b, s_gate_sb, s_gate_cv], axis=1)
    dh = _mm(dproj, gathered["w_in"], name="mm_dh", tb=True, b_groups=N_CHIPS, tk=2304)
    grad_x, dg_pre_mix = _rms_bwd1(dx2, dh, xs, g_pre_mix)

    small = ["g_pre_mix", "b_in", "b_dw", "g_conv_ln", "b_conv_ln", "g_post_mix", "g_pre_mlp", "g_post_mlp"]
    small_w = dict(g_pre_mix=g_pre_mix, b_in=b_in, b_dw=b_dw, g_conv_ln=g_conv_ln, b_conv_ln=b_conv_ln,
                   g_post_mix=g_post_mix, g_pre_mlp=g_pre_mlp, g_post_mlp=g_post_mlp)
    small_m = dict(g_pre_mix=m_g_pre_mix, b_in=m_b_in, b_dw=m_b_dw, g_conv_ln=m_g_conv_ln, b_conv_ln=m_b_conv_ln,
                   g_post_mix=m_g_post_mix, g_pre_mlp=m_g_pre_mlp, g_post_mlp=m_g_post_mlp)
    small_v = dict(g_pre_mix=v_g_pre_mix, b_in=v_b_in, b_dw=v_b_dw, g_conv_ln=v_g_conv_ln, b_conv_ln=v_b_conv_ln,
                   g_post_mix=v_g_post_mix, g_pre_mlp=v_g_pre_mlp, g_post_mlp=v_g_post_mlp)
    small_g = dict(g_pre_mix=dg_pre_mix, b_in=db_in, b_dw=db_dw, g_conv_ln=dg_conv_ln, b_conv_ln=db_conv_ln,
                   g_post_mix=dg_post_mix, g_pre_mlp=dg_pre_mlp, g_post_mlp=dg_post_mlp)
    shapes = [small_w[n].shape for n in small]
    tail_shapes = [loss_part.shape, dtaps.shape]
    summed = _all_reduce_small(_pack([small_g[n] for n in small] + [loss_part, dtaps]))
    grads["w_in"] = _mm(h, dproj, name="mm_dw_in", ta=True, out_groups=N_CHIPS, out_dtypes=(BF16,), tn=768, after=summed)
    in_started, in_token = reduce_start(group_in, "in", None)
    zeros_tail = [jnp.zeros(shp, F32) for shp in tail_shapes]
    res = _adamw(_pack([small_w[n] for n in small] + zeros_tail), summed,
                 _pack([small_m[n] for n in small] + zeros_tail), _pack([small_v[n] for n in small] + zeros_tail),
                 "adamw_small")
    unpacked = [_unpack(r, shapes + tail_shapes) for r in res]
    for i, n in enumerate(small):
        out_g[n], out_d[n], out_m[n], out_v[n] = [u[i] for u in unpacked]
    loss = unpacked[0][len(small)][0, 0]
    taps_grad = lax.dynamic_slice(unpacked[0][len(small) + 1], (0, chip * cs), (CONV_HALO, cs))
    res = _adamw(_pad_rows(w_dw[0], CONV_HALO), taps_grad, _pad_rows(m_w_dw[0], CONV_HALO), _pad_rows(v_w_dw[0], CONV_HALO),
                 "adamw_taps")
    out_g["w_dw"], out_d["w_dw"], out_m["w_dw"], out_v["w_dw"] = [r[:CONV_WIDTH][None] for r in res]
    done = reduce_finish(group_mlp, "mlp", mlp_started, in_token)
    done = reduce_finish(group_mix, "mix", mix_started, done)
    reduce_finish(group_in, "in", in_started, done)

    order = ["g_pre_mix", "w_in", "b_in", "w_dw", "b_dw", "g_conv_ln", "b_conv_ln", "w_sb_out", "w_conv_out", "w_o",
             "g_post_mix", "g_pre_mlp", "w_up", "w_down", "g_post_mlp"]
    return (loss, grad_x[None], *[out_g[n] for n in order], *[out_d[n] for n in order],
            *[out_m[n] for n in order], *[out_v[n] for n in order])
```

```python
import functools
import math

import jax
import jax.numpy as jnp
from jax import lax
from jax.experimental import pallas as pl
from jax.experimental.pallas import tpu as pltpu

F32 = jnp.float32
BF16 = jnp.bfloat16
MESH = pl.DeviceIdType.MESH

HEAD_DIM = 128
CONV_WIDTH = 31
CONV_HALO = 32
EPS = 1e-6
ADAM_LR = 0.001
ADAM_B1 = 0.9
ADAM_B2 = 0.999
ADAM_EPS = 1e-08
ADAM_WD = 0.01
ADAM_STEP = 10
N_CHIPS = 4
VMEM_LIMIT = 48 * 1024 * 1024


def _params(**kw):
    return pltpu.CompilerParams(vmem_limit_bytes=VMEM_LIMIT, **kw)


def _tile(n, want, align=128):
    if n <= want:
        return n
    for t in range(want - want % align, 0, -align):
        if n % t == 0:
            return t
    raise ValueError((n, want, align))


def _mm(a, b, *, name, ta=False, tb=False, b_groups=1, out_groups=1, bias=None, extra=None, epilogue=None,
        out_dtypes=(F32,), tm=1024, tn=1024, tk=2048, after=None):
    if ta:
        K, M = a.shape
    else:
        M, K = a.shape
    if b_groups == 1:
        br, bc = b.shape
    else:
        _, br, bcg = b.shape
        bc = bcg * b_groups
    if tb:
        N, Kb = br, bc
    else:
        Kb, N = br, bc
    assert Kb == K, (name, a.shape, b.shape)
    tm, tn, tk = _tile(M, tm), _tile(N, tn), _tile(K, tk)
    if b_groups > 1:
        if tb:
            tk = _tile(K // b_groups, tk)
        else:
            tn = _tile(N // b_groups, tn)
    if out_groups > 1:
        tn = _tile(N // out_groups, tn)
        if b_groups > 1 and not tb:
            tn = _tile(N // b_groups, tn)
    nm, nn, nk = M // tm, N // tn, K // tk
    extra = tuple(extra or ())
    n_extra = len(extra)
    has_bias = bias is not None
    n_out = len(out_dtypes)

    a_spec = pl.BlockSpec((tk, tm), lambda i, j, k: (k, i)) if ta else pl.BlockSpec((tm, tk), lambda i, j, k: (i, k))
    if b_groups == 1:
        b_spec = pl.BlockSpec((tn, tk), lambda i, j, k: (j, k)) if tb else pl.BlockSpec((tk, tn), lambda i, j, k: (k, j))
    elif tb:
        kpg = (K // b_groups) // tk
        b_spec = pl.BlockSpec((None, tn, tk), lambda i, j, k: (k // kpg, j, k % kpg))
    else:
        npg = (N // b_groups) // tn
        b_spec = pl.BlockSpec((None, tk, tn), lambda i, j, k: (j // npg, k, j % npg))
    in_specs = [a_spec, b_spec]
    operands = [a, b]
    if has_bias:
        in_specs.append(pl.BlockSpec((1, tn), lambda i, j, k: (0, j)))
        operands.append(bias)
    for e in extra:
        in_specs.append(pl.BlockSpec((tm, tn), lambda i, j, k: (i, j)))
        operands.append(e)
    n_after = 0 if after is None else 1
    if after is not None:
        in_specs.append(pl.BlockSpec(memory_space=pl.ANY))
        operands.append(after)
    if out_groups == 1:
        o_spec = pl.BlockSpec((tm, tn), lambda i, j, k: (i, j))
        o_shape = (M, N)
    else:
        opg = (N // out_groups) // tn
        o_spec = pl.BlockSpec((None, tm, tn), lambda i, j, k: (j // opg, i, j % opg))
        o_shape = (out_groups, M, N // out_groups)
    dims = (((0 if ta else 1,), (1 if tb else 0,)), ((), ()))

    def body(*refs):
        a_ref, b_ref = refs[0], refs[1]
        pos = 2
        bias_ref = None
        if has_bias:
            bias_ref = refs[pos]
            pos += 1
        extra_refs = refs[pos:pos + n_extra]
        pos += n_extra + n_after
        out_refs = refs[pos:pos + n_out]
        pos += n_out
        acc_ref = refs[pos] if nk > 1 else None

        part = lax.dot_general(a_ref[...].astype(BF16), b_ref[...].astype(BF16), dims, preferred_element_type=F32)

        def finish(acc):
            if has_bias:
                acc = acc + bias_ref[...]
            outs = epilogue(acc, *[e[...] for e in extra_refs]) if epilogue is not None else (acc,)
            for o_ref, o in zip(out_refs, outs):
                o_ref[...] = o.astype(o_ref.dtype)

        if nk == 1:
            finish(part)
        else:
            k = pl.program_id(2)

            @pl.when(k == 0)
            def _():
                acc_ref[...] = part

            @pl.when(k > 0)
            def _():
                acc_ref[...] += part

            @pl.when(k == nk - 1)
            def _():
                finish(acc_ref[...])

    outs = pl.pallas_call(
        body,
        name=name,
        grid=(nm, nn, nk),
        in_specs=in_specs,
        out_specs=[o_spec] * n_out,
        out_shape=[jax.ShapeDtypeStruct(o_shape, dt) for dt in out_dtypes],
        scratch_shapes=[pltpu.VMEM((tm, tn), F32)] if nk > 1 else [],
        compiler_params=_params(dimension_semantics=("parallel", "parallel", "arbitrary")),
    )(*operands)
    return outs[0] if n_out == 1 else outs


def _rms_stats(x):
    return lax.rsqrt(jnp.mean(x * x, axis=-1, keepdims=True) + EPS)


def _rms_bwd(x, r, g, dy):
    gy = dy * g
    return r * gy - x * (r * r * r) * jnp.mean(x * gy, axis=-1, keepdims=True)


def _sigmoid(x):
    return 1.0 / (1.0 + jnp.exp(-x))


def _row_call(body, *, name, rows, tr, ins, outs, acc_outs=()):
    n = rows // tr
    in_specs = []
    for arr, kind in ins:
        if kind == "row":
            in_specs.append(pl.BlockSpec((tr, arr.shape[1]), lambda i: (i, 0)))
        else:
            in_specs.append(pl.BlockSpec(arr.shape, lambda i: (0, 0)))
    out_specs = [pl.BlockSpec((tr, w), lambda i: (i, 0)) for w, _ in outs]
    out_shape = [jax.ShapeDtypeStruct((rows, w), dt) for w, dt in outs]
    out_specs += [pl.BlockSpec((1, w), lambda i: (0, 0)) for w in acc_outs]
    out_shape += [jax.ShapeDtypeStruct((1, w), F32) for w in acc_outs]
    return pl.pallas_call(
        body, name=name, grid=(n,), in_specs=in_specs, out_specs=out_specs, out_shape=out_shape,
        compiler_params=_params(dimension_semantics=("arbitrary",)),
    )(*[a for a, _ in ins])


def _accumulate(ref, val):
    @pl.when(pl.program_id(0) == 0)
    def _():
        ref[...] = val

    @pl.when(pl.program_id(0) > 0)
    def _():
        ref[...] += val


def _rms_fwd(x, g, tr=256):
    def body(x_ref, g_ref, h_ref):
        xv = x_ref[...]
        h_ref[...] = (xv * _rms_stats(xv) * g_ref[...]).astype(BF16)

    (h,) = _row_call(body, name="rms_fwd", rows=x.shape[0], tr=tr, ins=[(x, "row"), (g, "vec")], outs=[(x.shape[1], BF16)])
    return h


def _resid_rms(x, y, g2, g3, tr=256):
    def body(x_ref, y_ref, g2_ref, g3_ref, x2_ref, h2_ref):
        yv = y_ref[...]
        x2 = x_ref[...] + yv * _rms_stats(yv) * g2_ref[...]
        x2_ref[...] = x2
        h2_ref[...] = (x2 * _rms_stats(x2) * g3_ref[...]).astype(BF16)

    d = x.shape[1]
    return _row_call(body, name="resid_rms", rows=x.shape[0], tr=tr,
                     ins=[(x, "row"), (y, "row"), (g2, "vec"), (g3, "vec")], outs=[(d, F32), (d, BF16)])


def _final(x2, dn, g4, tgt, tr=256):
    d = x2.shape[1]

    def body(x2_ref, dn_ref, g4_ref, t_ref, dx3_ref, ddn_ref, dg4_ref, loss_ref):
        dn_v = dn_ref[...]
        r = _rms_stats(dn_v)
        g = g4_ref[...]
        e = x2_ref[...] + dn_v * r * g - t_ref[...]
        dx3 = e * (1.0 / d)
        dx3_ref[...] = dx3
        ddn_ref[...] = _rms_bwd(dn_v, r, g, dx3).astype(BF16)
        _accumulate(dg4_ref, jnp.sum(dx3 * dn_v * r, axis=0, keepdims=True))
        part = 0.5 * jnp.sum(jnp.mean(e * e, axis=-1, keepdims=True), axis=0, keepdims=True)
        _accumulate(loss_ref, jnp.broadcast_to(part, loss_ref.shape))

    return _row_call(body, name="final", rows=x2.shape[0], tr=tr,
                     ins=[(x2, "row"), (dn, "row"), (g4, "vec"), (tgt, "row")],
                     outs=[(d, F32), (d, BF16)], acc_outs=(d, 128))


def _rms_bwd2(dx3, dh2, x2, y, g2, g3, tr=256):
    d = x2.shape[1]

    def body(dx3_ref, dh2_ref, x2_ref, y_ref, g2_ref, g3_ref, dx2_ref, dy_ref, dg3_ref, dg2_ref):
        x2v, dh2v, yv = x2_ref[...], dh2_ref[...], y_ref[...]
        r3 = _rms_stats(x2v)
        dx2 = dx3_ref[...] + _rms_bwd(x2v, r3, g3_ref[...], dh2v)
        dx2_ref[...] = dx2
        _accumulate(dg3_ref, jnp.sum(dh2v * x2v * r3, axis=0, keepdims=True))
        r2 = _rms_stats(yv)
        dy_ref[...] = _rms_bwd(yv, r2, g2_ref[...], dx2).astype(BF16)
        _accumulate(dg2_ref, jnp.sum(dx2 * yv * r2, axis=0, keepdims=True))

    return _row_call(body, name="rms_bwd2", rows=x2.shape[0], tr=tr,
                     ins=[(dx3, "row"), (dh2, "row"), (x2, "row"), (y, "row"), (g2, "vec"), (g3, "vec")],
                     outs=[(d, F32), (d, BF16)], acc_outs=(d, d))


def _rms_bwd1(dx2, dh, x, g1, tr=256):
    d = x.shape[1]

    def body(dx2_ref, dh_ref, x_ref, g1_ref, gx_ref, dg1_ref):
        xv, dhv = x_ref[...], dh_ref[...]
        r = _rms_stats(xv)
        gx_ref[...] = dx2_ref[...] + _rms_bwd(xv, r, g1_ref[...], dhv)
        _accumulate(dg1_ref, jnp.sum(dhv * xv * r, axis=0, keepdims=True))

    return _row_call(body, name="rms_bwd1", rows=x.shape[0], tr=tr,
                     ins=[(dx2, "row"), (dh, "row"), (x, "row"), (g1, "vec")], outs=[(d, F32)], acc_outs=(d,))


def _merge_fwd(proj, o_sb, o_cv, off_sb, off_cv, gw, tr=256):
    s, d = o_sb.shape
    nj = d // gw
    b_sb, b_cv = off_sb // gw, off_cv // gw

    def body(gs_ref, gc_ref, osb_ref, ocv_ref, m_ref):
        m_ref[...] = (_sigmoid(gs_ref[...]) * osb_ref[...] + _sigmoid(gc_ref[...]) * ocv_ref[...]).astype(BF16)

    blk = lambda i, j: (i, j)
    return pl.pallas_call(
        body, name="merge_fwd", grid=(s // tr, nj),
        in_specs=[pl.BlockSpec((tr, gw), lambda i, j: (i, b_sb + j)), pl.BlockSpec((tr, gw), lambda i, j: (i, b_cv + j)),
                  pl.BlockSpec((tr, gw), blk), pl.BlockSpec((tr, gw), blk)],
        out_specs=pl.BlockSpec((tr, gw), blk), out_shape=jax.ShapeDtypeStruct((s, d), BF16),
        compiler_params=_params(dimension_semantics=("parallel", "parallel")),
    )(proj, proj, o_sb, o_cv)


def _merge_bwd(proj, o_sb, o_cv, dmerged, off_sb, off_cv, gw, tr=256):
    s, d = o_sb.shape
    nj = d // gw
    ni = s // tr
    b_sb, b_cv = off_sb // gw, off_cv // gw

    def body(gs_ref, gc_ref, osb_ref, ocv_ref, dm_ref, dosb_ref, docv_ref, dgs_ref, dgc_ref, sgs_ref, sgc_ref):
        dm = dm_ref[...]
        s_sb, s_cv = _sigmoid(gs_ref[...]), _sigmoid(gc_ref[...])
        dosb_ref[...] = (dm * s_sb).astype(BF16)
        docv_ref[...] = (dm * s_cv).astype(BF16)
        dgs = dm * osb_ref[...] * s_sb * (1.0 - s_sb)
        dgc = dm * ocv_ref[...] * s_cv * (1.0 - s_cv)
        dgs_ref[...] = dgs.astype(BF16)
        dgc_ref[...] = dgc.astype(BF16)
        i = pl.program_id(1)
        for ref, val in ((sgs_ref, dgs), (sgc_ref, dgc)):
            col = jnp.sum(val, axis=0, keepdims=True)

            @pl.when(i == 0)
            def _():
                ref[...] = col

            @pl.when(i > 0)
            def _():
                ref[...] += col

    blk = lambda j, i: (i, j)
    outs = pl.pallas_call(
        body, name="merge_bwd", grid=(nj, ni),
        in_specs=[pl.BlockSpec((tr, gw), lambda j, i: (i, b_sb + j)), pl.BlockSpec((tr, gw), lambda j, i: (i, b_cv + j)),
                  pl.BlockSpec((tr, gw), blk), pl.BlockSpec((tr, gw), blk), pl.BlockSpec((tr, gw), blk)],
        out_specs=[pl.BlockSpec((tr, gw), blk), pl.BlockSpec((tr, gw), blk),
                   pl.BlockSpec((tr, gw), blk), pl.BlockSpec((tr, gw), blk),
                   pl.BlockSpec((1, gw), lambda j, i: (0, j)), pl.BlockSpec((1, gw), lambda j, i: (0, j))],
        out_shape=[jax.ShapeDtypeStruct((s, d), BF16)] * 4 + [jax.ShapeDtypeStruct((1, d), F32)] * 2,
        compiler_params=_params(dimension_semantics=("parallel", "arbitrary")),
    )(proj, proj, o_sb, o_cv, dmerged)
    return outs


def _split_bf16(v):
    hi = v.astype(BF16)
    lo = (v - hi.astype(F32)).astype(BF16)
    return hi, lo


def _dot_nt(a, b):
    return lax.dot_general(a, b, (((1,), (1,)), ((), ())), preferred_element_type=F32)


def _dot_tn(a, b):
    return lax.dot_general(a, b, (((0,), (0,)), ((), ())), preferred_element_type=F32)


def _dot_nn(a, b):
    return lax.dot_general(a, b, (((1,), (0,)), ((), ())), preferred_element_type=F32)


def _sum_right(v, tri):
    hi, lo = _split_bf16(v)
    return _dot_nn(hi, tri) + _dot_nn(lo, tri)


HEADS_PER_STEP = 2


LOG2_E = 1.4426950408889634


def _sb_tile(q, kb, scale, mask):
    z = _dot_nt(q, kb) * (scale * LOG2_E)
    log_b = jnp.minimum(z, 0.0) - jnp.log2(1.0 + jnp.exp2(-jnp.abs(z)))
    log_1m = log_b - z
    if mask is not None:
        log_1m = jnp.where(mask, log_1m, 0.0)
    return log_b, log_1m


def _tile_mask(tq):
    row = lax.broadcasted_iota(jnp.int32, (tq, tq), 0)
    col = lax.broadcasted_iota(jnp.int32, (tq, tq), 1)
    return col < row


def _tri(tq, before):
    r = lax.broadcasted_iota(jnp.int32, (tq, tq), 0)
    c = lax.broadcasted_iota(jnp.int32, (tq, tq), 1)
    return jnp.where((r < c) if before else (r > c), 1.0, 0.0).astype(BF16)


def _attn_fwd(proj, after, n_heads, d_sb, tq=256):
    s = proj.shape[0]
    tq = _tile(s, tq)
    hb = d_sb // HEAD_DIM
    scale = 1.0 / math.sqrt(HEAD_DIM)

    heads = [pl.ds(n * HEAD_DIM, HEAD_DIM) for n in range(HEADS_PER_STEP)]
    wide = HEADS_PER_STEP * HEAD_DIM
    hb //= HEADS_PER_STEP

    def body(q_ref, k_ref, v_ref, _, o_ref):
        i = pl.program_id(1)
        tri = _tri(tq, False)

        def tile(j, carry, diagonal):
            rows = pl.ds(pl.multiple_of(j * tq, tq), tq)
            mask = _tile_mask(tq) if diagonal else None
            out = []
            for hd, (c_l, acc) in zip(heads, carry):
                log_b, log_1m = _sb_tile(q_ref[:, hd].astype(BF16), k_ref[rows, hd].astype(BF16), scale, mask)
                a = jnp.exp2(log_b + (c_l + _sum_right(log_1m, tri)))
                if diagonal:
                    a = jnp.where(mask, a, 0.0)
                out.append((c_l + jnp.sum(log_1m, axis=1, keepdims=True),
                            acc + _dot_nn(a.astype(BF16), v_ref[rows, hd].astype(BF16))))
            return tuple(out)

        init = tuple((jnp.zeros((tq, 1), F32), jnp.zeros((tq, HEAD_DIM), F32)) for _ in heads)
        carry = lax.fori_loop(0, i, lambda jj, c: tile(i - 1 - jj, c, False), tile(i, init, True))
        for hd, (_, acc) in zip(heads, carry):
            o_ref[:, hd] = acc.astype(BF16)

    return pl.pallas_call(
        body, name="attn_fwd", grid=(n_heads // HEADS_PER_STEP, s // tq),
        in_specs=[pl.BlockSpec((tq, wide), lambda h, i: (i, h)),
                  pl.BlockSpec((s, wide), lambda h, i: (0, hb + h)),
                  pl.BlockSpec((s, wide), lambda h, i: (0, 2 * hb + h)), pl.BlockSpec(memory_space=pl.ANY)],
        out_specs=pl.BlockSpec((tq, wide), lambda h, i: (i, h)),
        out_shape=jax.ShapeDtypeStruct((s, d_sb), BF16),
        compiler_params=_params(dimension_semantics=("parallel", "arbitrary")),
    )(proj, proj, proj, after)


def _attn_bwd(proj, dattn, after, n_heads, d_sb, tq=256):
    s = proj.shape[0]
    tq = _tile(s, tq)
    nq = s // tq
    hb = d_sb // HEAD_DIM // HEADS_PER_STEP
    scale = 1.0 / math.sqrt(HEAD_DIM)
    heads = [pl.ds(n * HEAD_DIM, HEAD_DIM) for n in range(HEADS_PER_STEP)]
    wide = HEADS_PER_STEP * HEAD_DIM

    def body(q_ref, k_ref, v_ref, do_ref, _, dq_ref, dk_ref, dv_ref, sq_ref, sk_ref, sv_ref, dk_acc, dv_acc, g_st, b_st):
        i = pl.program_id(1)

        @pl.when(i == 0)
        def _():
            dk_acc[...] = jnp.zeros_like(dk_acc)
            dv_acc[...] = jnp.zeros_like(dv_acc)

        tri_after = _tri(tq, False)
        tri_before = _tri(tq, True)

        def newest_first(j, c_ls, diagonal):
            rows = pl.ds(pl.multiple_of(j * tq, tq), tq)
            mask = _tile_mask(tq) if diagonal else None
            out = []
            for n, (hd, c_l) in enumerate(zip(heads, c_ls)):
                do = do_ref[:, hd].astype(BF16)
                log_b, log_1m = _sb_tile(q_ref[:, hd].astype(BF16), k_ref[rows, hd].astype(BF16), scale, mask)
                a = jnp.exp2(log_b + (c_l + _sum_right(log_1m, tri_after)))
                beta = jnp.exp2(log_b)
                if diagonal:
                    a = jnp.where(mask, a, 0.0)
                    beta = jnp.where(mask, beta, 0.0)
                g_st[n, j] = a * _dot_nt(do, v_ref[rows, hd].astype(BF16))
                b_st[n, j] = beta
                dv_acc[rows, hd] += _dot_tn(a.astype(BF16), do)
                out.append(c_l + jnp.sum(log_1m, axis=1, keepdims=True))
            return tuple(out)

        lax.fori_loop(0, i, lambda jj, c: newest_first(i - 1 - jj, c, False),
                      newest_first(i, tuple(jnp.zeros((tq, 1), F32) for _ in heads), True))

        def oldest_first(j, carry):
            rows = pl.ds(pl.multiple_of(j * tq, tq), tq)
            out = []
            for n, (hd, (c_g, dq)) in enumerate(zip(heads, carry)):
                kb = k_ref[rows, hd].astype(BF16)
                g = g_st[n, j]
                beta = b_st[n, j]
                g_before = c_g + _sum_right(g, tri_before)
                dz16 = ((g * (1.0 - beta) - g_before * beta) * scale).astype(BF16)
                dk_acc[rows, hd] += _dot_tn(dz16, q_ref[:, hd].astype(BF16))
                out.append((c_g + jnp.sum(g, axis=1, keepdims=True), dq + _dot_nn(dz16, kb)))
            return tuple(out)

        init = tuple((jnp.zeros((tq, 1), F32), jnp.zeros((tq, HEAD_DIM), F32)) for _ in heads)
        carry = lax.fori_loop(0, i + 1, oldest_first, init)
        dq = jnp.concatenate([c[1] for c in carry], axis=1)
        dq_ref[...] = dq.astype(BF16)
        col = jnp.sum(dq, axis=0, keepdims=True)

        @pl.when(i == 0)
        def _():
            sq_ref[...] = col

        @pl.when(i > 0)
        def _():
            sq_ref[...] += col

        @pl.when(i == nq - 1)
        def _():
            dk = dk_acc[...]
            dv = dv_acc[...]
            dk_ref[...] = dk.astype(BF16)
            dv_ref[...] = dv.astype(BF16)
            sk_ref[...] = jnp.sum(dk, axis=0, keepdims=True)
            sv_ref[...] = jnp.sum(dv, axis=0, keepdims=True)

    qblk = pl.BlockSpec((tq, wide), lambda h, i: (i, h))
    kblk = pl.BlockSpec((s, wide), lambda h, i: (0, h))
    col_sum = pl.BlockSpec((1, wide), lambda h, i: (0, h))
    stash = pltpu.VMEM((HEADS_PER_STEP, nq, tq, tq), F32)
    outs = pl.pallas_call(
        body, name="attn_bwd", grid=(n_heads // HEADS_PER_STEP, nq),
        in_specs=[qblk,
                  pl.BlockSpec((s, wide), lambda h, i: (0, hb + h)),
                  pl.BlockSpec((s, wide), lambda h, i: (0, 2 * hb + h)),
                  qblk, pl.BlockSpec(memory_space=pl.ANY)],
        out_specs=[qblk, kblk, kblk, col_sum, col_sum, col_sum],
        out_shape=[jax.ShapeDtypeStruct((s, d_sb), BF16)] * 3 + [jax.ShapeDtypeStruct((1, d_sb), F32)] * 3,
        scratch_shapes=[pltpu.VMEM((s, wide), F32), pltpu.VMEM((s, wide), F32), stash, stash],
        compiler_params=_params(dimension_semantics=("parallel", "arbitrary")),
    )(proj, proj, proj, dattn, after)
    return outs


LANES = 128


def _glu(proj, off_a, dc, tr=256):
    s = proj.shape[0]
    ba = off_a // dc

    def body(a_ref, b_ref, u_ref):
        u_ref[...] = a_ref[...] * _sigmoid(b_ref[...])

    return pl.pallas_call(
        body, name="glu", grid=(s // tr,),
        in_specs=[pl.BlockSpec((tr, dc), lambda i: (i, ba)), pl.BlockSpec((tr, dc), lambda i: (i, ba + 1))],
        out_specs=pl.BlockSpec((tr, dc), lambda i: (i, 0)), out_shape=jax.ShapeDtypeStruct((s, dc), F32),
        compiler_params=_params(dimension_semantics=("parallel",)),
    )(proj, proj)


def _conv_fwd(u0, taps, b_dw, g_ln, b_ln, tt=256):
    s, dc = u0.shape
    tt = _tile(s, tt)
    hpb = tt // CONV_HALO

    def body(cur_ref, halo_ref, w_ref, bdw_ref, g_ref, b_ref, u1_ref, u3_ref, xs):
        i = pl.program_id(0)
        xs[pl.ds(0, CONV_HALO), :] = jnp.where(i > 0, halo_ref[...], 0.0)
        xs[pl.ds(CONV_HALO, tt), :] = cur_ref[...]
        for c0 in range(0, dc, LANES):
            cols = pl.ds(c0, LANES)
            acc = jnp.broadcast_to(bdw_ref[:, cols], (tt, LANES))
            for k in range(CONV_WIDTH):
                acc = acc + w_ref[pl.ds(k, 1), cols] * xs[pl.ds(CONV_HALO - (CONV_WIDTH - 1) + k, tt), cols]
            u1_ref[:, cols] = acc
        u1 = u1_ref[...]
        mu = jnp.mean(u1, axis=-1, keepdims=True)
        xc = u1 - mu
        rstd = lax.rsqrt(jnp.mean(xc * xc, axis=-1, keepdims=True) + EPS)
        u2 = xc * rstd * g_ref[...] + b_ref[...]
        u3_ref[...] = (u2 * _sigmoid(u2)).astype(BF16)

    vec = pl.BlockSpec((1, dc), lambda i: (0, 0))
    return pl.pallas_call(
        body, name="conv_fwd", grid=(s // tt,),
        in_specs=[pl.BlockSpec((tt, dc), lambda i: (i, 0)),
                  pl.BlockSpec((CONV_HALO, dc), lambda i: (jnp.maximum(i * hpb - 1, 0), 0)),
                  pl.BlockSpec((CONV_HALO, dc), lambda i: (0, 0)), vec, vec, vec],
        out_specs=[pl.BlockSpec((tt, dc), lambda i: (i, 0)), pl.BlockSpec((tt, dc), lambda i: (i, 0))],
        out_shape=[jax.ShapeDtypeStruct((s, dc), F32), jax.ShapeDtypeStruct((s, dc), BF16)],
        scratch_shapes=[pltpu.VMEM((tt + CONV_HALO, dc), F32)],
        compiler_params=_params(dimension_semantics=("arbitrary",)),
    )(u0, u0, taps, b_dw, g_ln, b_ln)


def _conv_bwd_norm(u1, du3, g_ln, b_ln, tr=256):
    dc = u1.shape[1]

    def body(u1_ref, du3_ref, g_ref, b_ref, du1_ref, dg_ref, db_ref, dbdw_ref):
        u1v = u1_ref[...]
        g = g_ref[...]
        mu = jnp.mean(u1v, axis=-1, keepdims=True)
        xc = u1v - mu
        rstd = lax.rsqrt(jnp.mean(xc * xc, axis=-1, keepdims=True) + EPS)
        xhat = xc * rstd
        u2 = xhat * g + b_ref[...]
        sg = _sigmoid(u2)
        du2 = du3_ref[...] * (sg * (1.0 + u2 * (1.0 - sg)))
        _accumulate(dg_ref, jnp.sum(du2 * xhat, axis=0, keepdims=True))
        _accumulate(db_ref, jnp.sum(du2, axis=0, keepdims=True))
        gy = du2 * g
        du1 = rstd * (gy - jnp.mean(gy, axis=-1, keepdims=True) - xhat * jnp.mean(gy * xhat, axis=-1, keepdims=True))
        du1_ref[...] = du1
        _accumulate(dbdw_ref, jnp.sum(du1, axis=0, keepdims=True))

    return _row_call(body, name="conv_bwd_norm", rows=u1.shape[0], tr=tr,
                     ins=[(u1, "row"), (du3, "row"), (g_ln, "vec"), (b_ln, "vec")],
                     outs=[(dc, F32)], acc_outs=(dc, dc, dc))


def _conv_bwd_taps(du1, u0, proj, taps, off_a, tt=256):
    s, dc = u0.shape
    tt = _tile(s, tt)
    n = s // tt
    hpb = tt // CONV_HALO
    ba = off_a // dc
    lead = CONV_HALO - (CONV_WIDTH - 1)

    def body(d_ref, dnext_ref, cur_ref, halo_ref, a_ref, b_ref, w_ref, da_ref, db_ref, sa_ref, sb_ref, dw_ref,
             xs, ds, du0):
        i = pl.program_id(0)
        xs[pl.ds(0, CONV_HALO), :] = jnp.where(i > 0, halo_ref[...], 0.0)
        xs[pl.ds(CONV_HALO, tt), :] = cur_ref[...]
        ds[pl.ds(0, tt), :] = d_ref[...]
        ds[pl.ds(tt, CONV_HALO), :] = jnp.where(i < n - 1, dnext_ref[...], 0.0)

        @pl.when(i == 0)
        def _():
            dw_ref[...] = jnp.zeros_like(dw_ref)

        for c0 in range(0, dc, LANES):
            cols = pl.ds(c0, LANES)
            d_cur = ds[pl.ds(0, tt), cols]
            acc = jnp.zeros((tt, LANES), F32)
            for k in range(CONV_WIDTH):
                acc = acc + w_ref[pl.ds(k, 1), cols] * ds[pl.ds(CONV_WIDTH - 1 - k, tt), cols]
                dw_ref[pl.ds(k, 1), cols] += jnp.sum(d_cur * xs[pl.ds(lead + k, tt), cols], axis=0, keepdims=True)
            du0[:, cols] = acc
        d0 = du0[...]
        sg = _sigmoid(b_ref[...])
        da = d0 * sg
        db = d0 * a_ref[...] * sg * (1.0 - sg)
        da_ref[...] = da.astype(BF16)
        db_ref[...] = db.astype(BF16)
        _accumulate(sa_ref, jnp.sum(da, axis=0, keepdims=True))
        _accumulate(sb_ref, jnp.sum(db, axis=0, keepdims=True))

    row = pl.BlockSpec((tt, dc), lambda i: (i, 0))
    vec = pl.BlockSpec((1, dc), lambda i: (0, 0))
    taps_spec = pl.BlockSpec((CONV_HALO, dc), lambda i: (0, 0))
    return pl.pallas_call(
        body, name="conv_bwd_taps", grid=(n,),
        in_specs=[row, pl.BlockSpec((CONV_HALO, dc), lambda i: (jnp.minimum((i + 1) * hpb, n * hpb - 1), 0)),
                  row, pl.BlockSpec((CONV_HALO, dc), lambda i: (jnp.maximum(i * hpb - 1, 0), 0)),
                  pl.BlockSpec((tt, dc), lambda i: (i, ba)), pl.BlockSpec((tt, dc), lambda i: (i, ba + 1)), taps_spec],
        out_specs=[row, row, vec, vec, taps_spec],
        out_shape=[jax.ShapeDtypeStruct((s, dc), BF16), jax.ShapeDtypeStruct((s, dc), BF16),
                   jax.ShapeDtypeStruct((1, dc), F32), jax.ShapeDtypeStruct((1, dc), F32),
                   jax.ShapeDtypeStruct((CONV_HALO, dc), F32)],
        scratch_shapes=[pltpu.VMEM((tt + CONV_HALO, dc), F32), pltpu.VMEM((tt + CONV_HALO, dc), F32),
                        pltpu.VMEM((tt, dc), F32)],
        compiler_params=_params(dimension_semantics=("arbitrary",)),
    )(du1, du1, u0, u0, proj, proj, taps)


def _elementwise(body, *, name, ins, out_dtypes, tr=128):
    rows, cols = ins[0].shape
    tr = _tile(rows, tr, 8)
    spec = pl.BlockSpec((tr, cols), lambda i: (i, 0))
    return pl.pallas_call(
        body, name=name, grid=(rows // tr,), in_specs=[spec] * len(ins), out_specs=[spec] * len(out_dtypes),
        out_shape=[jax.ShapeDtypeStruct((rows, cols), dt) for dt in out_dtypes],
        compiler_params=_params(dimension_semantics=("parallel",)),
    )(*ins)


def _prefetched(body, *, name, where, grid, in_specs, out_specs, out_shape, operands):
    return pl.pallas_call(
        body, name=name, out_shape=out_shape,
        grid_spec=pltpu.PrefetchScalarGridSpec(num_scalar_prefetch=1, grid=grid, in_specs=in_specs, out_specs=out_specs),
        compiler_params=_params(dimension_semantics=("parallel",) * len(grid)),
    )(where, *operands)


def _cast_into_block(w, where, after, name, tr=256):
    rows, cols = w.shape
    tr = _tile(rows, tr, 16)

    def body(where_ref, w_ref, _, o_ref):
        o_ref[...] = w_ref[...].astype(BF16)

    return _prefetched(
        body, name=name, where=where, grid=(rows // tr,),
        in_specs=[pl.BlockSpec((tr, cols), lambda i, wh: (i, 0)), pl.BlockSpec(memory_space=pl.ANY)],
        out_specs=pl.BlockSpec((None, tr, cols), lambda i, wh: (wh[0], i, 0)),
        out_shape=jax.ShapeDtypeStruct((N_CHIPS, rows, cols), BF16), operands=[w, after])


def _add_pair(grad, theirs, where, name, tr=256):
    nb, half, cols = theirs.shape
    tr = _tile(half, tr, 16)
    nh = half // tr

    def body(where_ref, a_ref, b_ref, o_ref):
        o_ref[...] = (a_ref[...].astype(F32) + b_ref[...].astype(F32)).astype(BF16)

    blk = pl.BlockSpec((None, tr, cols), lambda b, i, wh: (b, i, 0))
    return _prefetched(
        body, name=name, where=where, grid=(nb, nh),
        in_specs=[pl.BlockSpec((None, tr, cols), lambda b, i, wh: (b, wh[1] * nh + i, 0)), blk],
        out_specs=blk, out_shape=jax.ShapeDtypeStruct(theirs.shape, BF16), operands=[grad, theirs])


def _sum_slots(part, slots, where, name, tr=256):
    _, half, cols = part.shape
    tr = _tile(half, tr, 16)
    nh = half // tr

    def body(where_ref, p_ref, s_ref, o_ref):
        o_ref[...] = ((p_ref[...].astype(F32) + s_ref[0].astype(F32)) + s_ref[1].astype(F32)) + s_ref[2].astype(F32)

    return _prefetched(
        body, name=name, where=where, grid=(nh,),
        in_specs=[pl.BlockSpec((None, tr, cols), lambda i, wh: (wh[0], i, 0)),
                  pl.BlockSpec((3, tr, cols), lambda i, wh: (0, i, 0))],
        out_specs=pl.BlockSpec((tr, cols), lambda i, wh: (wh[1] * nh + i, 0)),
        out_shape=jax.ShapeDtypeStruct((2 * half, cols), F32), operands=[part, slots])


def _adamw(w, g, m, v, name):
    bc1 = 1.0 - ADAM_B1 ** ADAM_STEP
    bc2 = 1.0 - ADAM_B2 ** ADAM_STEP

    def body(w_ref, g_ref, m_ref, v_ref, go_ref, d_ref, mo_ref, vo_ref):
        g = g_ref[...]
        m_new = ADAM_B1 * m_ref[...] + (1.0 - ADAM_B1) * g
        v_new = ADAM_B2 * v_ref[...] + (1.0 - ADAM_B2) * (g * g)
        m_hat = m_new / bc1
        v_hat = v_new / bc2
        go_ref[...] = g
        d_ref[...] = -ADAM_LR * (m_hat / (jnp.sqrt(v_hat) + ADAM_EPS) + ADAM_WD * w_ref[...])
        mo_ref[...] = m_new
        vo_ref[...] = v_new

    return _elementwise(body, name=name, ins=[w, g, m, v], out_dtypes=[F32] * 4, tr=128)


ANY = pl.BlockSpec(memory_space=pl.ANY)


def _mesh_pos():
    return lax.axis_index("x"), lax.axis_index("y"), lax.axis_index("c")


def _other_chips(x, y):
    return [(1 - x, y), (x, 1 - y), (1 - x, 1 - y)]


def _remote(src, dst, send_sems, recv_sems, idx, to):
    return pltpu.make_async_remote_copy(src_ref=src, dst_ref=dst, send_sem=send_sems.at[idx], recv_sem=recv_sems.at[idx],
                                        device_id=to, device_id_type=MESH)


HBM = pl.BlockSpec(memory_space=pltpu.HBM)
SEM = pl.BlockSpec(memory_space=pltpu.SEMAPHORE)
EFFECT = pltpu.SideEffectType.DATAFLOW_SIDE_EFFECTING


def _in_hbm(arrays):
    return [pltpu.with_memory_space_constraint(a, pltpu.HBM) for a in arrays]


def _ici_copies(slab, send_sems, recv_sems, x, y, c):
    half = slab.shape[1] // 2
    rows = pl.ds(c * half, half)
    mine = slab.at[2 * x + y, rows]
    out = []
    for j, (cx, cy) in enumerate(_other_chips(x, y)):
        got = slab.at[2 * cx + cy, rows]
        out.append((_remote(mine, mine, send_sems, recv_sems, j, (cx, cy, c)),
                    _remote(got, got, send_sems, recv_sems, j, (cx, cy, c))))
    return out


def _gather_start(slabs, after, name):
    n = len(slabs)

    def body(*refs):
        ins = refs[:n]
        send_sems, recv_sems, token = refs[2 * n + 1:3 * n + 1], refs[3 * n + 1:4 * n + 1], refs[4 * n + 1]
        x, y, c = _mesh_pos()
        for w in range(n):
            for send, _ in _ici_copies(ins[w], send_sems[w], recv_sems[w], x, y, c):
                send.start()
        token[...] = jnp.zeros_like(token)

    outs = pl.pallas_call(
        body, name=name, in_specs=[HBM] * n + [ANY],
        out_specs=[HBM] * n + [SEM] * (2 * n) + [pl.BlockSpec(memory_space=pltpu.VMEM)],
        out_shape=[pltpu.HBM(s.shape, s.dtype) for s in slabs] + [pltpu.SemaphoreType.DMA((3,))] * (2 * n)
        + [jax.ShapeDtypeStruct((8, 128), F32)],
        input_output_aliases={w: w for w in range(n)},
        compiler_params=_params(has_side_effects=EFFECT),
    )(*_in_hbm(slabs), after)
    return (outs[:n], outs[n:2 * n], outs[2 * n:3 * n]), outs[3 * n]


def _gather_wait(slabs, send_sems, recv_sems, after, name):
    n = len(slabs)

    def body(*refs):
        ins = refs[:n]
        sends, recvs = refs[n:2 * n], refs[2 * n:3 * n]
        x, y, c = _mesh_pos()
        for w in range(n):
            for send, recv in _ici_copies(ins[w], sends[w], recvs[w], x, y, c):
                send.wait_send()
                recv.wait_recv()

    return pl.pallas_call(
        body, name=name, in_specs=[HBM] * n + [SEM] * (2 * n) + [ANY], out_specs=[HBM] * n,
        out_shape=[pltpu.HBM(s.shape, s.dtype) for s in slabs],
        input_output_aliases={w: w for w in range(n)},
        compiler_params=_params(has_side_effects=EFFECT),
    )(*slabs, *send_sems, *recv_sems, after)


def _phase_start(arrays, copies, n_copies, after, name):
    n = len(arrays)

    def body(*refs):
        send_sems, recv_sems, token = refs[2 * n + 1], refs[2 * n + 2], refs[2 * n + 3]
        for send, _ in copies(refs[:n], send_sems, recv_sems, *_mesh_pos()):
            send.start()
        token[...] = jnp.zeros_like(token)

    outs = pl.pallas_call(
        body, name=name, in_specs=[HBM] * n + [ANY],
        out_specs=[HBM] * n + [SEM, SEM, pl.BlockSpec(memory_space=pltpu.VMEM)],
        out_shape=[pltpu.HBM(a.shape, a.dtype) for a in arrays]
        + [pltpu.SemaphoreType.DMA((n_copies,))] * 2 + [jax.ShapeDtypeStruct((8, 128), F32)],
        input_output_aliases={k: k for k in range(n)},
        compiler_params=_params(has_side_effects=EFFECT),
    )(*_in_hbm(arrays), after)
    return outs[:n], outs[n], outs[n + 1], outs[n + 2]


def _phase_wait(arrays, send_sems, recv_sems, copies, after, name):
    n = len(arrays)

    def body(*refs):
        for send, recv in copies(refs[:n], refs[n], refs[n + 1], *_mesh_pos()):
            send.wait_send()
            recv.wait_recv()

    return pl.pallas_call(
        body, name=name, in_specs=[HBM] * n + [SEM, SEM, ANY], out_specs=[HBM] * n,
        out_shape=[pltpu.HBM(a.shape, a.dtype) for a in arrays],
        input_output_aliases={k: k for k in range(n)},
        compiler_params=_params(has_side_effects=EFFECT),
    )(*arrays, send_sems, recv_sems, after)


def _forward_copies(slabs, send_sems, recv_sems, x, y, c):
    sibling = (x, y, 1 - c)
    out = []
    for w, slab in enumerate(slabs):
        half = slab.shape[1] // 2
        for j, (cx, cy) in enumerate(_other_chips(x, y)):
            mine = slab.at[2 * cx + cy, pl.ds(c * half, half)]
            theirs = slab.at[2 * cx + cy, pl.ds((1 - c) * half, half)]
            out.append((_remote(mine, mine, send_sems, recv_sems, 3 * w + j, sibling),
                        _remote(theirs, theirs, send_sems, recv_sems, 3 * w + j, sibling)))
    return out


def _exchange_copies(arrays, send_sems, recv_sems, x, y, c):
    n = len(arrays) // 2
    sibling = (x, y, 1 - c)
    out = []
    for w in range(n):
        grad, land = arrays[w], arrays[n + w]
        half = grad.shape[1] // 2
        out.append((_remote(grad.at[:, pl.ds((1 - c) * half, half), :], land, send_sems, recv_sems, w, sibling),
                    _remote(land, land, send_sems, recv_sems, w, sibling)))
    return out


def _gather_taps(taps):
    def body(in_ref, out_ref, send_sems, recv_sems, local_sem):
        x, y, c = _mesh_pos()
        me = 2 * x + y
        chips = _other_chips(x, y)
        lc = pltpu.make_async_copy(in_ref, out_ref.at[me], local_sem)
        lc.start()
        sends = [_remote(in_ref, out_ref.at[me], send_sems, recv_sems, j, (cx, cy, c)) for j, (cx, cy) in enumerate(chips)]
        for cp in sends:
            cp.start()
        for j, (cx, cy) in enumerate(chips):
            got = out_ref.at[2 * cx + cy]
            _remote(got, got, send_sems, recv_sems, j, (cx, cy, c)).wait_recv()
        for cp in sends:
            cp.wait_send()
        lc.wait()

    vm = pl.BlockSpec(memory_space=pltpu.VMEM)
    return pl.pallas_call(
        body, name="gather_taps", in_specs=[vm], out_specs=vm,
        out_shape=jax.ShapeDtypeStruct((N_CHIPS,) + taps.shape, taps.dtype),
        scratch_shapes=[pltpu.SemaphoreType.DMA((3,)), pltpu.SemaphoreType.DMA((3,)), pltpu.SemaphoreType.DMA],
        compiler_params=_params(has_side_effects=True),
    )(taps)


def _exchange_halves(grads, name):
    n = len(grads)

    def body(*refs):
        ins, their_refs = refs[:n], refs[n:2 * n]
        send_sems, recv_sems = refs[2 * n:]
        x, y, c = _mesh_pos()
        sibling = (x, y, 1 - c)
        sends = []
        for w in range(n):
            half = ins[w].shape[1] // 2
            cp = _remote(ins[w].at[:, pl.ds((1 - c) * half, half), :], their_refs[w], send_sems, recv_sems, w, sibling)
            cp.start()
            sends.append(cp)
        for w in range(n):
            _remote(their_refs[w], their_refs[w], send_sems, recv_sems, w, sibling).wait_recv()
        for cp in sends:
            cp.wait_send()

    halves = [jax.ShapeDtypeStruct((g.shape[0], g.shape[1] // 2, g.shape[2]), g.dtype) for g in grads]
    return pl.pallas_call(
        body, name=name, in_specs=[ANY] * n, out_specs=[ANY] * n, out_shape=halves,
        scratch_shapes=[pltpu.SemaphoreType.DMA((n,)), pltpu.SemaphoreType.DMA((n,))],
        compiler_params=_params(has_side_effects=True),
    )(*grads)


def _scatter_copies(part, slots, send_sems, recv_sems, x, y, c):
    out = []
    for j, (cx, cy) in enumerate(_other_chips(x, y)):
        out.append((_remote(part.at[2 * cx + cy], slots.at[j], send_sems, recv_sems, j, (cx, cy, c)),
                    _remote(slots.at[j], slots.at[j], send_sems, recv_sems, j, (cx, cy, c))))
    return out


def _scatter_start(parts, after, name):
    n = len(parts)
    slots = [lax.empty((3,) + p.shape[1:], p.dtype) for p in parts]

    def body(*refs):
        ins, lands = refs[:n], refs[n:2 * n]
        send_sems, recv_sems, token = refs[4 * n + 1:5 * n + 1], refs[5 * n + 1:6 * n + 1], refs[6 * n + 1]
        x, y, c = _mesh_pos()
        for w in range(n):
            for send, _ in _scatter_copies(ins[w], lands[w], send_sems[w], recv_sems[w], x, y, c):
                send.start()
        token[...] = jnp.zeros_like(token)

    outs = pl.pallas_call(
        body, name=name, in_specs=[HBM] * (2 * n) + [ANY],
        out_specs=[HBM] * (2 * n) + [SEM] * (2 * n) + [pl.BlockSpec(memory_space=pltpu.VMEM)],
        out_shape=[pltpu.HBM(a.shape, a.dtype) for a in list(parts) + slots] + [pltpu.SemaphoreType.DMA((3,))] * (2 * n)
        + [jax.ShapeDtypeStruct((8, 128), F32)],
        input_output_aliases={k: k for k in range(2 * n)},
        compiler_params=_params(has_side_effects=EFFECT),
    )(*_in_hbm(list(parts) + slots), after)
    return (outs[:n], outs[n:2 * n], outs[2 * n:3 * n], outs[3 * n:4 * n]), outs[4 * n]


def _scatter_wait(parts, slots, send_sems, recv_sems, after, name):
    n = len(parts)

    def body(*refs):
        ins, lands = refs[:n], refs[n:2 * n]
        sends, recvs = refs[2 * n:3 * n], refs[3 * n:4 * n]
        x, y, c = _mesh_pos()
        for w in range(n):
            for send, recv in _scatter_copies(ins[w], lands[w], sends[w], recvs[w], x, y, c):
                send.wait_send()
                recv.wait_recv()

    outs = pl.pallas_call(
        body, name=name, in_specs=[HBM] * (2 * n) + [SEM] * (2 * n) + [ANY], out_specs=[HBM] * (2 * n),
        out_shape=[pltpu.HBM(a.shape, a.dtype) for a in list(parts) + list(slots)],
        input_output_aliases={k: k for k in range(2 * n)},
        compiler_params=_params(has_side_effects=EFFECT),
    )(*parts, *slots, *send_sems, *recv_sems, after)
    return outs[:n], outs[n:]


def _share_halves(halves, name):
    n = len(halves)

    def body(*refs):
        outs = refs[n:2 * n]
        send_sems, recv_sems = refs[2 * n:]
        x, y, c = _mesh_pos()
        sibling = (x, y, 1 - c)
        sends = []
        for w in range(n):
            h = outs[w].shape[0] // 2
            mine = outs[w].at[pl.ds(c * h, h)]
            cp = _remote(mine, mine, send_sems, recv_sems, w, sibling)
            cp.start()
            sends.append(cp)
        for w in range(n):
            h = outs[w].shape[0] // 2
            got = outs[w].at[pl.ds((1 - c) * h, h)]
            _remote(got, got, send_sems, recv_sems, w, sibling).wait_recv()
        for cp in sends:
            cp.wait_send()

    return pl.pallas_call(
        body, name=name, in_specs=[ANY] * n, out_specs=[ANY] * n,
        out_shape=[jax.ShapeDtypeStruct(h.shape, h.dtype) for h in halves],
        input_output_aliases={w: w for w in range(n)},
        scratch_shapes=[pltpu.SemaphoreType.DMA((n,)), pltpu.SemaphoreType.DMA((n,))],
        compiler_params=_params(has_side_effects=True),
    )(*halves)


def _all_reduce_small(buf):
    rows, width = buf.shape

    def body(in_ref, out_ref, gathered, send_sems, recv_sems):
        x, y, c = _mesh_pos()
        me = 4 * x + 2 * y + c
        gathered[me] = in_ref[...]
        flips = [(dx, dy, dz) for dx in (0, 1) for dy in (0, 1) for dz in (0, 1)][1:]
        peers = [((1 - x) if dx else x, (1 - y) if dy else y, (1 - c) if dz else c) for dx, dy, dz in flips]
        sends = [_remote(in_ref, gathered.at[me], send_sems, recv_sems, k, peer) for k, peer in enumerate(peers)]
        for cp in sends:
            cp.start()
        for k, (px, py, pc) in enumerate(peers):
            got = gathered.at[4 * px + 2 * py + pc]
            _remote(got, got, send_sems, recv_sems, k, (px, py, pc)).wait_recv()
        for cp in sends:
            cp.wait_send()
        total = gathered[0]
        for d in range(1, 8):
            total = total + gathered[d]
        out_ref[...] = total

    vm = pl.BlockSpec(memory_space=pltpu.VMEM)
    return pl.pallas_call(
        body, name="all_reduce_small", in_specs=[vm], out_specs=vm, out_shape=jax.ShapeDtypeStruct(buf.shape, F32),
        scratch_shapes=[pltpu.VMEM((8, rows, width), F32), pltpu.SemaphoreType.DMA((7,)), pltpu.SemaphoreType.DMA((7,))],
        compiler_params=_params(has_side_effects=True),
    )(buf)


PACK_W = 2048
PACK_ROWS = 8


def _pack(vectors):
    flat = jnp.concatenate([v.reshape(-1) for v in vectors])
    unit = PACK_W * PACK_ROWS
    total = -(-flat.shape[0] // unit) * unit
    return jnp.pad(flat, (0, total - flat.shape[0])).reshape(total // PACK_W, PACK_W)


def _unpack(buf, shapes):
    flat = buf.reshape(-1)
    out, pos = [], 0
    for shp in shapes:
        size = math.prod(shp)
        out.append(flat[pos:pos + size].reshape(shp))
        pos += size
    return out


def _pad_rows(a, rows):
    return jnp.pad(a, ((0, rows - a.shape[0]), (0, 0)))


def kernel(x, g_pre_mix, w_in, b_in, w_dw, b_dw, g_conv_ln, b_conv_ln, w_sb_out, w_conv_out, w_o, g_post_mix, g_pre_mlp, w_up, w_down, g_post_mlp, loss_target, m_g_pre_mix, m_w_in, m_b_in, m_w_dw, m_b_dw, m_g_conv_ln, m_b_conv_ln, m_w_sb_out, m_w_conv_out, m_w_o, m_g_post_mix, m_g_pre_mlp, m_w_up, m_w_down, m_g_post_mlp, v_g_pre_mix, v_w_in, v_b_in, v_w_dw, v_b_dw, v_g_conv_ln, v_b_conv_ln, v_w_sb_out, v_w_conv_out, v_w_o, v_g_post_mix, v_g_pre_mlp, v_w_up, v_w_down, v_g_post_mlp):
    xs = x[0]
    tgt = loss_target[0]
    s, d = xs.shape
    d_sb = w_sb_out.shape[1]
    dc = w_conv_out.shape[1]
    d_in = w_in.shape[2] * N_CHIPS
    n_heads = d_sb // HEAD_DIM
    off_a = 3 * d_sb
    off_sb = off_a + 2 * dc
    off_cv = off_sb + d
    gw = dc
    assert off_a % dc == 0 and off_sb % gw == 0 and d % gw == 0 and d_in == off_cv + d
    chip = 2 * lax.axis_index("x") + lax.axis_index("y")

    big = dict(w_in=w_in[0], w_sb_out=w_sb_out[0], w_conv_out=w_conv_out[0], w_o=w_o[0], w_up=w_up[0], w_down=w_down[0])
    names = list(big)
    col_sharded = {"w_in", "w_sb_out", "w_conv_out", "w_up"}
    where = jnp.stack([chip, lax.axis_index("c")]).astype(jnp.int32)
    cs = w_dw.shape[2]
    taps4 = _gather_taps(_pad_rows(w_dw[0], CONV_HALO))
    first, token = _gather_start([_cast_into_block(big["w_in"], where, taps4, "cast_w_in")], taps4, "gather_start_in")
    rest, _ = _gather_start([_cast_into_block(big[n], where, token, "cast_" + n) for n in names[1:]], token,
                            "gather_start_rest")
    in_flight = dict(zip(names, zip(*[a + b for a, b in zip(first, rest)])))
    gathered = {}

    def land(group, after, tag):
        thru, sends, recvs = zip(*[in_flight[n] for n in group])
        landed = _gather_wait(thru, sends, recvs, after, "gather_wait_" + tag)
        return _phase_start(landed, _forward_copies, 3 * len(group), where, "forward_start_" + tag)

    def ready(group, started, after, tag):
        slabs, sends, recvs, _ = started
        for n, g4 in zip(group, _phase_wait(slabs, sends, recvs, _forward_copies, after, "forward_wait_" + tag)):
            gathered[n] = g4 if n in col_sharded else g4.reshape(g4.shape[0] * g4.shape[1], g4.shape[2])

    taps = jnp.transpose(taps4, (1, 0, 2)).reshape(CONV_HALO, N_CHIPS * cs)
    group_mix = ["w_sb_out", "w_conv_out", "w_o"]

    h = _rms_fwd(xs, g_pre_mix)
    started = land(["w_in"], h, "in")
    ready(["w_in"], started, started[3], "in")
    proj = _mm(h, gathered["w_in"], name="mm_proj", b_groups=N_CHIPS, bias=b_in, tn=768)
    mix_fwd = land(group_mix, proj, "mix")
    attn = _attn_fwd(proj, mix_fwd[3], n_heads, d_sb)
    u0 = _glu(proj, off_a, dc)
    u1, u3 = _conv_fwd(u0, taps, b_dw, g_conv_ln, b_conv_ln)
    ready(group_mix, mix_fwd, u3, "mix")
    o_sb = _mm(attn, gathered["w_sb_out"], name="mm_o_sb", b_groups=N_CHIPS)
    up_fwd = land(["w_up"], o_sb, "up")
    o_cv = _mm(u3, gathered["w_conv_out"], name="mm_o_cv", b_groups=N_CHIPS, after=up_fwd[3])
    merged = _merge_fwd(proj, o_sb, o_cv, off_sb, off_cv, gw)
    y = _mm(merged, gathered["w_o"], name="mm_y")
    x2, h2 = _resid_rms(xs, y, g_post_mix, g_pre_mlp)
    ready(["w_up"], up_fwd, h2, "up")
    down_fwd = land(["w_down"], h2, "down")
    up, f = _mm(h2, gathered["w_up"], name="mm_up", b_groups=N_CHIPS, out_dtypes=(F32, BF16), after=down_fwd[3],
                epilogue=lambda acc: (acc, jnp.square(jnp.maximum(acc, 0.0))))
    ready(["w_down"], down_fwd, f, "down")
    dn = _mm(f, gathered["w_down"], name="mm_down")
    dx3, d_dn, dg_post_mlp, loss_part = _final(x2, dn, g_post_mlp, tgt)

    moments = dict(w_in=(m_w_in, v_w_in), w_sb_out=(m_w_sb_out, v_w_sb_out), w_conv_out=(m_w_conv_out, v_w_conv_out),
                   w_o=(m_w_o, v_w_o), w_up=(m_w_up, v_w_up), w_down=(m_w_down, v_w_down))
    out_g, out_d, out_m, out_v = {}, {}, {}, {}
    grads = {}

    def blocks_of(group):
        out = []
        for n in group:
            g = grads[n]
            out.append(g if g.ndim == 3 else g.reshape(N_CHIPS, g.shape[0] // N_CHIPS, g.shape[1]))
        return out

    def reduce_start(group, tag):
        blocks = blocks_of(group)
        theirs = _exchange_halves(blocks, "exchange_halves_" + tag)
        pair_sums = [_add_pair(g, t, where, "pair_sum_" + n) for n, g, t in zip(group, blocks, theirs)]
        return _scatter_start(pair_sums, where, "scatter_start_" + tag)

    def exchange_begin(group, tag):
        blocks = blocks_of(group)
        lands = [lax.empty((b.shape[0], b.shape[1] // 2, b.shape[2]), b.dtype) for b in blocks]
        return _phase_start(blocks + lands, _exchange_copies, len(group), where, "exchange_start_" + tag)

    def scatter_begin(group, tag, exchange, after):
        arrays, sends, recvs, _ = exchange
        arrays = _phase_wait(arrays, sends, recvs, _exchange_copies, after, "exchange_wait_" + tag)
        n = len(group)
        pair_sums = [_add_pair(g, t, where, "pair_sum_" + m) for m, g, t in zip(group, arrays[:n], arrays[n:])]
        return _scatter_start(pair_sums, where, "scatter_start_" + tag)

    def reduce_finish(group, tag, started, after):
        parts, slots = _scatter_wait(*started, after, "scatter_wait_" + tag)
        halves = [_sum_slots(p, t, where, "chip_sum_" + n) for n, p, t in zip(group, parts, slots)]
        for n, red in zip(group, _share_halves(halves, "share_halves_" + tag)):
            mm_, vv_ = moments[n]
            res = _adamw(big[n], red, mm_[0], vv_[0], "adamw_" + n)
            out_g[n], out_d[n], out_m[n], out_v[n] = [r[None] for r in res]
        return res[1]

    group_mlp, group_mix, group_in = ["w_down", "w_up"], ["w_o", "w_sb_out", "w_conv_out"], ["w_in"]
    grads["w_down"] = _mm(f, d_dn, name="mm_dw_down", ta=True, out_dtypes=(BF16,))
    dup = _mm(d_dn, gathered["w_down"], name="mm_df", tb=True, extra=[up], out_dtypes=(BF16,),
              epilogue=lambda acc, upv: (acc * (2.0 * jnp.maximum(upv, 0.0)),))
    grads["w_up"] = _mm(h2, dup, name="mm_dw_up", ta=True, out_groups=N_CHIPS, out_dtypes=(BF16,))
    mlp_exchange = exchange_begin(group_mlp, "mlp")
    dh2 = _mm(dup, gathered["w_up"], name="mm_dh2", tb=True, b_groups=N_CHIPS, after=mlp_exchange[3])
    dx2, dy, dg_pre_mlp, dg_post_mix = _rms_bwd2(dx3, dh2, x2, y, g_post_mix, g_pre_mlp)
    mlp_started, mlp_token = scatter_begin(group_mlp, "mlp", mlp_exchange, dx2)
    grads["w_o"] = _mm(merged, dy, name="mm_dw_o", ta=True, out_dtypes=(BF16,), after=mlp_token)
    dmerged = _mm(dy, gathered["w_o"], name="mm_dmerged", tb=True)
    do_sb, do_cv, dgate_sb, dgate_cv, s_gate_sb, s_gate_cv = _merge_bwd(proj, o_sb, o_cv, dmerged, off_sb, off_cv, gw)
    grads["w_sb_out"] = _mm(attn, do_sb, name="mm_dw_sb", ta=True, out_groups=N_CHIPS, out_dtypes=(BF16,))
    dattn = _mm(do_sb, gathered["w_sb_out"], name="mm_dattn", tb=True, b_groups=N_CHIPS)
    grads["w_conv_out"] = _mm(u3, do_cv, name="mm_dw_cv", ta=True, out_groups=N_CHIPS, out_dtypes=(BF16,))
    mix_exchange = exchange_begin(group_mix, "mix")
    du3 = _mm(do_cv, gathered["w_conv_out"], name="mm_du3", tb=True, b_groups=N_CHIPS, after=mix_exchange[3])
    mix_started, mix_token = scatter_begin(group_mix, "mix", mix_exchange, du3)
    dq, dk, dv, s_q, s_k, s_v = _attn_bwd(proj, dattn, mix_token, n_heads, d_sb)
    du1, dg_conv_ln, db_conv_ln, db_dw = _conv_bwd_norm(u1, du3, g_conv_ln, b_conv_ln)
    dglu_a, dglu_b, s_a, s_b, dtaps = _conv_bwd_taps(du1, u0, proj, taps, off_a)
    dproj = jnp.concatenate([dq, dk, dv, dglu_a, dglu_b, dgate_sb, dgate_cv], axis=1)
    db_in = jnp.concatenate([s_q, s_k, s_v, s_a, s_b, s_gate_sb, s_gate_cv], axis=1)
    dh = _mm(dproj, gathered["w_in"], name="mm_dh", tb=True, b_groups=N_CHIPS, tk=2304)
    grad_x, dg_pre_mix = _rms_bwd1(dx2, dh, xs, g_pre_mix)

    small = ["g_pre_mix", "b_in", "b_dw", "g_conv_ln", "b_conv_ln", "g_post_mix", "g_pre_mlp", "g_post_mlp"]
    small_w = dict(g_pre_mix=g_pre_mix, b_in=b_in, b_dw=b_dw, g_conv_ln=g_conv_ln, b_conv_ln=b_conv_ln,
                   g_post_mix=g_post_mix, g_pre_mlp=g_pre_mlp, g_post_mlp=g_post_mlp)
    small_m = dict(g_pre_mix=m_g_pre_mix, b_in=m_b_in, b_dw=m_b_dw, g_conv_ln=m_g_conv_ln, b_conv_ln=m_b_conv_ln,
                   g_post_mix=m_g_post_mix, g_pre_mlp=m_g_pre_mlp, g_post_mlp=m_g_post_mlp)
    small_v = dict(g_pre_mix=v_g_pre_mix, b_in=v_b_in, b_dw=v_b_dw, g_conv_ln=v_g_conv_ln, b_conv_ln=v_b_conv_ln,
                   g_post_mix=v_g_post_mix, g_pre_mlp=v_g_pre_mlp, g_post_mlp=v_g_post_mlp)
    small_g = dict(g_pre_mix=dg_pre_mix, b_in=db_in, b_dw=db_dw, g_conv_ln=dg_conv_ln, b_conv_ln=db_conv_ln,
                   g_post_mix=dg_post_mix, g_pre_mlp=dg_pre_mlp, g_post_mlp=dg_post_mlp)
    shapes = [small_w[n].shape for n in small]
    tail_shapes = [loss_part.shape, dtaps.shape]
    summed = _all_reduce_small(_pack([small_g[n] for n in small] + [loss_part, dtaps]))
    grads["w_in"] = _mm(h, dproj, name="mm_dw_in", ta=True, out_groups=N_CHIPS, out_dtypes=(BF16,), tn=768, after=summed)
    in_started, in_token = reduce_start(group_in, "in")
    zeros_tail = [jnp.zeros(shp, F32) for shp in tail_shapes]
    res = _adamw(_pack([small_w[n] for n in small] + zeros_tail), summed,
                 _pack([small_m[n] for n in small] + zeros_tail), _pack([small_v[n] for n in small] + zeros_tail),
                 "adamw_small")
    unpacked = [_unpack(r, shapes + tail_shapes) for r in res]
    for i, n in enumerate(small):
        out_g[n], out_d[n], out_m[n], out_v[n] = [u[i] for u in unpacked]
    loss = unpacked[0][len(small)][0, 0]
    taps_grad = lax.dynamic_slice(unpacked[0][len(small) + 1], (0, chip * cs), (CONV_HALO, cs))
    res = _adamw(_pad_rows(w_dw[0], CONV_HALO), taps_grad, _pad_rows(m_w_dw[0], CONV_HALO), _pad_rows(v_w_dw[0], CONV_HALO),
                 "adamw_taps")
    out_g["w_dw"], out_d["w_dw"], out_m["w_dw"], out_v["w_dw"] = [r[:CONV_WIDTH][None] for r in res]
    done = reduce_finish(group_mlp, "mlp", mlp_started, in_token)
    done = reduce_finish(group_mix, "mix", mix_started, done)
    reduce_finish(group_in, "in", in_started, done)

    order = ["g_pre_mix", "w_in", "b_in", "w_dw", "b_dw", "g_conv_ln", "b_conv_ln", "w_sb_out", "w_conv_out", "w_o",
             "g_post_mix", "g_pre_mlp", "w_up", "w_down", "g_post_mlp"]
    return (loss, grad_x[None], *[out_g[n] for n in order], *[out_d[n] for n in order],
            *[out_m[n] for n in order], *[out_v[n] for n in order])
```

```python
import functools
import math

import jax
import jax.numpy as jnp
from jax import lax
from jax.experimental import pallas as pl
from jax.experimental.pallas import tpu as pltpu

F32 = jnp.float32
BF16 = jnp.bfloat16
MESH = pl.DeviceIdType.MESH

HEAD_DIM = 128
CONV_WIDTH = 31
CONV_HALO = 32
EPS = 1e-6
ADAM_LR = 0.001
ADAM_B1 = 0.9
ADAM_B2 = 0.999
ADAM_EPS = 1e-08
ADAM_WD = 0.01
ADAM_STEP = 10
N_CHIPS = 4
VMEM_LIMIT = 48 * 1024 * 1024


def _params(**kw):
    return pltpu.CompilerParams(vmem_limit_bytes=VMEM_LIMIT, **kw)


def _tile(n, want, align=128):
    if n <= want:
        return n
    for t in range(want - want % align, 0, -align):
        if n % t == 0:
            return t
    raise ValueError((n, want, align))


def _mm(a, b, *, name, ta=False, tb=False, b_groups=1, out_groups=1, bias=None, extra=None, epilogue=None,
        out_dtypes=(F32,), tm=1024, tn=1024, tk=2048, after=None):
    if ta:
        K, M = a.shape
    else:
        M, K = a.shape
    if b_groups == 1:
        br, bc = b.shape
    else:
        _, br, bcg = b.shape
        bc = bcg * b_groups
    if tb:
        N, Kb = br, bc
    else:
        Kb, N = br, bc
    assert Kb == K, (name, a.shape, b.shape)
    tm, tn, tk = _tile(M, tm), _tile(N, tn), _tile(K, tk)
    if b_groups > 1:
        if tb:
            tk = _tile(K // b_groups, tk)
        else:
            tn = _tile(N // b_groups, tn)
    if out_groups > 1:
        tn = _tile(N // out_groups, tn)
        if b_groups > 1 and not tb:
            tn = _tile(N // b_groups, tn)
    nm, nn, nk = M // tm, N // tn, K // tk
    extra = tuple(extra or ())
    n_extra = len(extra)
    has_bias = bias is not None
    n_out = len(out_dtypes)

    a_spec = pl.BlockSpec((tk, tm), lambda i, j, k: (k, i)) if ta else pl.BlockSpec((tm, tk), lambda i, j, k: (i, k))
    if b_groups == 1:
        b_spec = pl.BlockSpec((tn, tk), lambda i, j, k: (j, k)) if tb else pl.BlockSpec((tk, tn), lambda i, j, k: (k, j))
    elif tb:
        kpg = (K // b_groups) // tk
        b_spec = pl.BlockSpec((None, tn, tk), lambda i, j, k: (k // kpg, j, k % kpg))
    else:
        npg = (N // b_groups) // tn
        b_spec = pl.BlockSpec((None, tk, tn), lambda i, j, k: (j // npg, k, j % npg))
    in_specs = [a_spec, b_spec]
    operands = [a, b]
    if has_bias:
        in_specs.append(pl.BlockSpec((1, tn), lambda i, j, k: (0, j)))
        operands.append(bias)
    for e in extra:
        in_specs.append(pl.BlockSpec((tm, tn), lambda i, j, k: (i, j)))
        operands.append(e)
    n_after = 0 if after is None else 1
    if after is not None:
        in_specs.append(pl.BlockSpec(memory_space=pl.ANY))
        operands.append(after)
    if out_groups == 1:
        o_spec = pl.BlockSpec((tm, tn), lambda i, j, k: (i, j))
        o_shape = (M, N)
    else:
        opg = (N // out_groups) // tn
        o_spec = pl.BlockSpec((None, tm, tn), lambda i, j, k: (j // opg, i, j % opg))
        o_shape = (out_groups, M, N // out_groups)
    dims = (((0 if ta else 1,), (1 if tb else 0,)), ((), ()))

    def body(*refs):
        a_ref, b_ref = refs[0], refs[1]
        pos = 2
        bias_ref = None
        if has_bias:
            bias_ref = refs[pos]
            pos += 1
        extra_refs = refs[pos:pos + n_extra]
        pos += n_extra + n_after
        out_refs = refs[pos:pos + n_out]
        pos += n_out
        acc_ref = refs[pos] if nk > 1 else None

        part = lax.dot_general(a_ref[...].astype(BF16), b_ref[...].astype(BF16), dims, preferred_element_type=F32)

        def finish(acc):
            if has_bias:
                acc = acc + bias_ref[...]
            outs = epilogue(acc, *[e[...] for e in extra_refs]) if epilogue is not None else (acc,)
            for o_ref, o in zip(out_refs, outs):
                o_ref[...] = o.astype(o_ref.dtype)

        if nk == 1:
            finish(part)
        else:
            k = pl.program_id(2)

            @pl.when(k == 0)
            def _():
                acc_ref[...] = part

            @pl.when(k > 0)
            def _():
                acc_ref[...] += part

            @pl.when(k == nk - 1)
            def _():
                finish(acc_ref[...])

    outs = pl.pallas_call(
        body,
        name=name,
        grid=(nm, nn, nk),
        in_specs=in_specs,
        out_specs=[o_spec] * n_out,
        out_shape=[jax.ShapeDtypeStruct(o_shape, dt) for dt in out_dtypes],
        scratch_shapes=[pltpu.VMEM((tm, tn), F32)] if nk > 1 else [],
        compiler_params=_params(dimension_semantics=("parallel", "parallel", "arbitrary")),
    )(*operands)
    return outs[0] if n_out == 1 else outs


def _rms_stats(x):
    return lax.rsqrt(jnp.mean(x * x, axis=-1, keepdims=True) + EPS)


def _rms_bwd(x, r, g, dy):
    gy = dy * g
    return r * gy - x * (r * r * r) * jnp.mean(x * gy, axis=-1, keepdims=True)


def _sigmoid(x):
    return 1.0 / (1.0 + jnp.exp(-x))


def _row_call(body, *, name, rows, tr, ins, outs, acc_outs=()):
    n = rows // tr
    in_specs = []
    for arr, kind in ins:
        if kind == "row":
            in_specs.append(pl.BlockSpec((tr, arr.shape[1]), lambda i: (i, 0)))
        else:
            in_specs.append(pl.BlockSpec(arr.shape, lambda i: (0, 0)))
    out_specs = [pl.BlockSpec((tr, w), lambda i: (i, 0)) for w, _ in outs]
    out_shape = [jax.ShapeDtypeStruct((rows, w), dt) for w, dt in outs]
    out_specs += [pl.BlockSpec((1, w), lambda i: (0, 0)) for w in acc_outs]
    out_shape += [jax.ShapeDtypeStruct((1, w), F32) for w in acc_outs]
    return pl.pallas_call(
        body, name=name, grid=(n,), in_specs=in_specs, out_specs=out_specs, out_shape=out_shape,
        compiler_params=_params(dimension_semantics=("arbitrary",)),
    )(*[a for a, _ in ins])


def _accumulate(ref, val):
    @pl.when(pl.program_id(0) == 0)
    def _():
        ref[...] = val

    @pl.when(pl.program_id(0) > 0)
    def _():
        ref[...] += val


def _rms_fwd(x, g, tr=256):
    def body(x_ref, g_ref, h_ref):
        xv = x_ref[...]
        h_ref[...] = (xv * _rms_stats(xv) * g_ref[...]).astype(BF16)

    (h,) = _row_call(body, name="rms_fwd", rows=x.shape[0], tr=tr, ins=[(x, "row"), (g, "vec")], outs=[(x.shape[1], BF16)])
    return h


def _resid_rms(x, y, g2, g3, tr=256):
    def body(x_ref, y_ref, g2_ref, g3_ref, x2_ref, h2_ref):
        yv = y_ref[...]
        x2 = x_ref[...] + yv * _rms_stats(yv) * g2_ref[...]
        x2_ref[...] = x2
        h2_ref[...] = (x2 * _rms_stats(x2) * g3_ref[...]).astype(BF16)

    d = x.shape[1]
    return _row_call(body, name="resid_rms", rows=x.shape[0], tr=tr,
                     ins=[(x, "row"), (y, "row"), (g2, "vec"), (g3, "vec")], outs=[(d, F32), (d, BF16)])


def _final(x2, dn, g4, tgt, tr=256):
    d = x2.shape[1]

    def body(x2_ref, dn_ref, g4_ref, t_ref, dx3_ref, ddn_ref, dg4_ref, loss_ref):
        dn_v = dn_ref[...]
        r = _rms_stats(dn_v)
        g = g4_ref[...]
        e = x2_ref[...] + dn_v * r * g - t_ref[...]
        dx3 = e * (1.0 / d)
        dx3_ref[...] = dx3
        ddn_ref[...] = _rms_bwd(dn_v, r, g, dx3).astype(BF16)
        _accumulate(dg4_ref, jnp.sum(dx3 * dn_v * r, axis=0, keepdims=True))
        part = 0.5 * jnp.sum(jnp.mean(e * e, axis=-1, keepdims=True), axis=0, keepdims=True)
        _accumulate(loss_ref, jnp.broadcast_to(part, loss_ref.shape))

    return _row_call(body, name="final", rows=x2.shape[0], tr=tr,
                     ins=[(x2, "row"), (dn, "row"), (g4, "vec"), (tgt, "row")],
                     outs=[(d, F32), (d, BF16)], acc_outs=(d, 128))


def _rms_bwd2(dx3, dh2, x2, y, g2, g3, tr=256):
    d = x2.shape[1]

    def body(dx3_ref, dh2_ref, x2_ref, y_ref, g2_ref, g3_ref, dx2_ref, dy_ref, dg3_ref, dg2_ref):
        x2v, dh2v, yv = x2_ref[...], dh2_ref[...], y_ref[...]
        r3 = _rms_stats(x2v)
        dx2 = dx3_ref[...] + _rms_bwd(x2v, r3, g3_ref[...], dh2v)
        dx2_ref[...] = dx2
        _accumulate(dg3_ref, jnp.sum(dh2v * x2v * r3, axis=0, keepdims=True))
        r2 = _rms_stats(yv)
        dy_ref[...] = _rms_bwd(yv, r2, g2_ref[...], dx2).astype(BF16)
        _accumulate(dg2_ref, jnp.sum(dx2 * yv * r2, axis=0, keepdims=True))

    return _row_call(body, name="rms_bwd2", rows=x2.shape[0], tr=tr,
                     ins=[(dx3, "row"), (dh2, "row"), (x2, "row"), (y, "row"), (g2, "vec"), (g3, "vec")],
                     outs=[(d, F32), (d, BF16)], acc_outs=(d, d))


def _rms_bwd1(dx2, dh, x, g1, tr=256):
    d = x.shape[1]

    def body(dx2_ref, dh_ref, x_ref, g1_ref, gx_ref, dg1_ref):
        xv, dhv = x_ref[...], dh_ref[...]
        r = _rms_stats(xv)
        gx_ref[...] = dx2_ref[...] + _rms_bwd(xv, r, g1_ref[...], dhv)
        _accumulate(dg1_ref, jnp.sum(dhv * xv * r, axis=0, keepdims=True))

    return _row_call(body, name="rms_bwd1", rows=x.shape[0], tr=tr,
                     ins=[(dx2, "row"), (dh, "row"), (x, "row"), (g1, "vec")], outs=[(d, F32)], acc_outs=(d,))


def _merge_fwd(proj, o_sb, o_cv, off_sb, off_cv, gw, tr=256):
    s, d = o_sb.shape
    nj = d // gw
    b_sb, b_cv = off_sb // gw, off_cv // gw

    def body(gs_ref, gc_ref, osb_ref, ocv_ref, m_ref):
        m_ref[...] = (_sigmoid(gs_ref[...]) * osb_ref[...] + _sigmoid(gc_ref[...]) * ocv_ref[...]).astype(BF16)

    blk = lambda i, j: (i, j)
    return pl.pallas_call(
        body, name="merge_fwd", grid=(s // tr, nj),
        in_specs=[pl.BlockSpec((tr, gw), lambda i, j: (i, b_sb + j)), pl.BlockSpec((tr, gw), lambda i, j: (i, b_cv + j)),
                  pl.BlockSpec((tr, gw), blk), pl.BlockSpec((tr, gw), blk)],
        out_specs=pl.BlockSpec((tr, gw), blk), out_shape=jax.ShapeDtypeStruct((s, d), BF16),
        compiler_params=_params(dimension_semantics=("parallel", "parallel")),
    )(proj, proj, o_sb, o_cv)


def _merge_bwd(proj, o_sb, o_cv, dmerged, off_sb, off_cv, gw, tr=256):
    s, d = o_sb.shape
    nj = d // gw
    ni = s // tr
    b_sb, b_cv = off_sb // gw, off_cv // gw

    def body(gs_ref, gc_ref, osb_ref, ocv_ref, dm_ref, dosb_ref, docv_ref, dgs_ref, dgc_ref, sgs_ref, sgc_ref):
        dm = dm_ref[...]
        s_sb, s_cv = _sigmoid(gs_ref[...]), _sigmoid(gc_ref[...])
        dosb_ref[...] = (dm * s_sb).astype(BF16)
        docv_ref[...] = (dm * s_cv).astype(BF16)
        dgs = dm * osb_ref[...] * s_sb * (1.0 - s_sb)
        dgc = dm * ocv_ref[...] * s_cv * (1.0 - s_cv)
        dgs_ref[...] = dgs.astype(BF16)
        dgc_ref[...] = dgc.astype(BF16)
        i = pl.program_id(1)
        for ref, val in ((sgs_ref, dgs), (sgc_ref, dgc)):
            col = jnp.sum(val, axis=0, keepdims=True)

            @pl.when(i == 0)
            def _():
                ref[...] = col

            @pl.when(i > 0)
            def _():
                ref[...] += col

    blk = lambda j, i: (i, j)
    outs = pl.pallas_call(
        body, name="merge_bwd", grid=(nj, ni),
        in_specs=[pl.BlockSpec((tr, gw), lambda j, i: (i, b_sb + j)), pl.BlockSpec((tr, gw), lambda j, i: (i, b_cv + j)),
                  pl.BlockSpec((tr, gw), blk), pl.BlockSpec((tr, gw), blk), pl.BlockSpec((tr, gw), blk)],
        out_specs=[pl.BlockSpec((tr, gw), blk), pl.BlockSpec((tr, gw), blk),
                   pl.BlockSpec((tr, gw), blk), pl.BlockSpec((tr, gw), blk),
                   pl.BlockSpec((1, gw), lambda j, i: (0, j)), pl.BlockSpec((1, gw), lambda j, i: (0, j))],
        out_shape=[jax.ShapeDtypeStruct((s, d), BF16)] * 4 + [jax.ShapeDtypeStruct((1, d), F32)] * 2,
        compiler_params=_params(dimension_semantics=("parallel", "arbitrary")),
    )(proj, proj, o_sb, o_cv, dmerged)
    return outs


def _split_bf16(v):
    hi = v.astype(BF16)
    lo = (v - hi.astype(F32)).astype(BF16)
    return hi, lo


def _dot_nt(a, b):
    return lax.dot_general(a, b, (((1,), (1,)), ((), ())), preferred_element_type=F32)


def _dot_tn(a, b):
    return lax.dot_general(a, b, (((0,), (0,)), ((), ())), preferred_element_type=F32)


def _dot_nn(a, b):
    return lax.dot_general(a, b, (((1,), (0,)), ((), ())), preferred_element_type=F32)


def _sum_right(v, tri):
    hi, lo = _split_bf16(v)
    return _dot_nn(hi, tri) + _dot_nn(lo, tri)


HEADS_PER_STEP = 2


LOG2_E = 1.4426950408889634


def _sb_tile(q, kb, scale, mask):
    z = _dot_nt(q, kb) * (scale * LOG2_E)
    log_b = jnp.minimum(z, 0.0) - jnp.log2(1.0 + jnp.exp2(-jnp.abs(z)))
    log_1m = log_b - z
    if mask is not None:
        log_1m = jnp.where(mask, log_1m, 0.0)
    return log_b, log_1m


def _tile_mask(tq):
    row = lax.broadcasted_iota(jnp.int32, (tq, tq), 0)
    col = lax.broadcasted_iota(jnp.int32, (tq, tq), 1)
    return col < row


def _tri(tq, before):
    r = lax.broadcasted_iota(jnp.int32, (tq, tq), 0)
    c = lax.broadcasted_iota(jnp.int32, (tq, tq), 1)
    return jnp.where((r < c) if before else (r > c), 1.0, 0.0).astype(BF16)


def _attn_fwd(proj, after, n_heads, d_sb, tq=256):
    s = proj.shape[0]
    tq = _tile(s, tq)
    hb = d_sb // HEAD_DIM
    scale = 1.0 / math.sqrt(HEAD_DIM)

    heads = [pl.ds(n * HEAD_DIM, HEAD_DIM) for n in range(HEADS_PER_STEP)]
    wide = HEADS_PER_STEP * HEAD_DIM
    hb //= HEADS_PER_STEP

    def body(q_ref, k_ref, v_ref, _, o_ref):
        i = pl.program_id(1)
        tri = _tri(tq, False)

        def tile(j, carry, diagonal):
            rows = pl.ds(pl.multiple_of(j * tq, tq), tq)
            mask = _tile_mask(tq) if diagonal else None
            out = []
            for hd, (c_l, acc) in zip(heads, carry):
                log_b, log_1m = _sb_tile(q_ref[:, hd].astype(BF16), k_ref[rows, hd].astype(BF16), scale, mask)
                a = jnp.exp2(log_b + (c_l + _sum_right(log_1m, tri)))
                if diagonal:
                    a = jnp.where(mask, a, 0.0)
                out.append((c_l + jnp.sum(log_1m, axis=1, keepdims=True),
                            acc + _dot_nn(a.astype(BF16), v_ref[rows, hd].astype(BF16))))
            return tuple(out)

        init = tuple((jnp.zeros((tq, 1), F32), jnp.zeros((tq, HEAD_DIM), F32)) for _ in heads)
        carry = lax.fori_loop(0, i, lambda jj, c: tile(i - 1 - jj, c, False), tile(i, init, True))
        for hd, (_, acc) in zip(heads, carry):
            o_ref[:, hd] = acc.astype(BF16)

    return pl.pallas_call(
        body, name="attn_fwd", grid=(n_heads // HEADS_PER_STEP, s // tq),
        in_specs=[pl.BlockSpec((tq, wide), lambda h, i: (i, h)),
                  pl.BlockSpec((s, wide), lambda h, i: (0, hb + h)),
                  pl.BlockSpec((s, wide), lambda h, i: (0, 2 * hb + h)), pl.BlockSpec(memory_space=pl.ANY)],
        out_specs=pl.BlockSpec((tq, wide), lambda h, i: (i, h)),
        out_shape=jax.ShapeDtypeStruct((s, d_sb), BF16),
        compiler_params=_params(dimension_semantics=("parallel", "arbitrary")),
    )(proj, proj, proj, after)


def _attn_bwd(proj, dattn, after, n_heads, d_sb, tq=256):
    s = proj.shape[0]
    tq = _tile(s, tq)
    nq = s // tq
    hb = d_sb // HEAD_DIM // HEADS_PER_STEP
    scale = 1.0 / math.sqrt(HEAD_DIM)
    heads = [pl.ds(n * HEAD_DIM, HEAD_DIM) for n in range(HEADS_PER_STEP)]
    wide = HEADS_PER_STEP * HEAD_DIM

    def body(q_ref, k_ref, v_ref, do_ref, _, dq_ref, dk_ref, dv_ref, sq_ref, sk_ref, sv_ref, dk_acc, dv_acc, g_st, b_st):
        i = pl.program_id(1)

        @pl.when(i == 0)
        def _():
            dk_acc[...] = jnp.zeros_like(dk_acc)
            dv_acc[...] = jnp.zeros_like(dv_acc)

        tri_after = _tri(tq, False)
        tri_before = _tri(tq, True)

        def newest_first(j, c_ls, diagonal):
            rows = pl.ds(pl.multiple_of(j * tq, tq), tq)
            mask = _tile_mask(tq) if diagonal else None
            out = []
            for n, (hd, c_l) in enumerate(zip(heads, c_ls)):
                do = do_ref[:, hd].astype(BF16)
                log_b, log_1m = _sb_tile(q_ref[:, hd].astype(BF16), k_ref[rows, hd].astype(BF16), scale, mask)
                a = jnp.exp2(log_b + (c_l + _sum_right(log_1m, tri_after)))
                beta = jnp.exp2(log_b)
                if diagonal:
                    a = jnp.where(mask, a, 0.0)
                    beta = jnp.where(mask, beta, 0.0)
                g_st[n, j] = a * _dot_nt(do, v_ref[rows, hd].astype(BF16))
                b_st[n, j] = beta
                dv_acc[rows, hd] += _dot_tn(a.astype(BF16), do)
                out.append(c_l + jnp.sum(log_1m, axis=1, keepdims=True))
            return tuple(out)

        lax.fori_loop(0, i, lambda jj, c: newest_first(i - 1 - jj, c, False),
                      newest_first(i, tuple(jnp.zeros((tq, 1), F32) for _ in heads), True))

        def oldest_first(j, carry):
            rows = pl.ds(pl.multiple_of(j * tq, tq), tq)
            out = []
            for n, (hd, (c_g, dq)) in enumerate(zip(heads, carry)):
                kb = k_ref[rows, hd].astype(BF16)
                g = g_st[n, j]
                beta = b_st[n, j]
                g_before = c_g + _sum_right(g, tri_before)
                dz16 = ((g * (1.0 - beta) - g_before * beta) * scale).astype(BF16)
                dk_acc[rows, hd] += _dot_tn(dz16, q_ref[:, hd].astype(BF16))
                out.append((c_g + jnp.sum(g, axis=1, keepdims=True), dq + _dot_nn(dz16, kb)))
            return tuple(out)

        init = tuple((jnp.zeros((tq, 1), F32), jnp.zeros((tq, HEAD_DIM), F32)) for _ in heads)
        carry = lax.fori_loop(0, i + 1, oldest_first, init)
        dq = jnp.concatenate([c[1] for c in carry], axis=1)
        dq_ref[...] = dq.astype(BF16)
        col = jnp.sum(dq, axis=0, keepdims=True)

        @pl.when(i == 0)
        def _():
            sq_ref[...] = col

        @pl.when(i > 0)
        def _():
            sq_ref[...] += col

        @pl.when(i == nq - 1)
        def _():
            dk = dk_acc[...]
            dv = dv_acc[...]
            dk_ref[...] = dk.astype(BF16)
            dv_ref[...] = dv.astype(BF16)
            sk_ref[...] = jnp.sum(dk, axis=0, keepdims=True)
            sv_ref[...] = jnp.sum(dv, axis=0, keepdims=True)

    qblk = pl.BlockSpec((tq, wide), lambda h, i: (i, h))
    kblk = pl.BlockSpec((s, wide), lambda h, i: (0, h))
    col_sum = pl.BlockSpec((1, wide), lambda h, i: (0, h))
    stash = pltpu.VMEM((HEADS_PER_STEP, nq, tq, tq), F32)
    outs = pl.pallas_call(
        body, name="attn_bwd", grid=(n_heads // HEADS_PER_STEP, nq),
        in_specs=[qblk,
                  pl.BlockSpec((s, wide), lambda h, i: (0, hb + h)),
                  pl.BlockSpec((s, wide), lambda h, i: (0, 2 * hb + h)),
                  qblk, pl.BlockSpec(memory_space=pl.ANY)],
        out_specs=[qblk, kblk, kblk, col_sum, col_sum, col_sum],
        out_shape=[jax.ShapeDtypeStruct((s, d_sb), BF16)] * 3 + [jax.ShapeDtypeStruct((1, d_sb), F32)] * 3,
        scratch_shapes=[pltpu.VMEM((s, wide), F32), pltpu.VMEM((s, wide), F32), stash, stash],
        compiler_params=_params(dimension_semantics=("parallel", "arbitrary")),
    )(proj, proj, proj, dattn, after)
    return outs


LANES = 128


def _glu(proj, off_a, dc, tr=256):
    s = proj.shape[0]
    ba = off_a // dc

    def body(a_ref, b_ref, u_ref):
        u_ref[...] = a_ref[...] * _sigmoid(b_ref[...])

    return pl.pallas_call(
        body, name="glu", grid=(s // tr,),
        in_specs=[pl.BlockSpec((tr, dc), lambda i: (i, ba)), pl.BlockSpec((tr, dc), lambda i: (i, ba + 1))],
        out_specs=pl.BlockSpec((tr, dc), lambda i: (i, 0)), out_shape=jax.ShapeDtypeStruct((s, dc), F32),
        compiler_params=_params(dimension_semantics=("parallel",)),
    )(proj, proj)


def _conv_fwd(u0, taps, b_dw, g_ln, b_ln, tt=256):
    s, dc = u0.shape
    tt = _tile(s, tt)
    hpb = tt // CONV_HALO

    def body(cur_ref, halo_ref, w_ref, bdw_ref, g_ref, b_ref, u1_ref, u3_ref, xs):
        i = pl.program_id(0)
        xs[pl.ds(0, CONV_HALO), :] = jnp.where(i > 0, halo_ref[...], 0.0)
        xs[pl.ds(CONV_HALO, tt), :] = cur_ref[...]
        for c0 in range(0, dc, LANES):
            cols = pl.ds(c0, LANES)
            acc = jnp.broadcast_to(bdw_ref[:, cols], (tt, LANES))
            for k in range(CONV_WIDTH):
                acc = acc + w_ref[pl.ds(k, 1), cols] * xs[pl.ds(CONV_HALO - (CONV_WIDTH - 1) + k, tt), cols]
            u1_ref[:, cols] = acc
        u1 = u1_ref[...]
        mu = jnp.mean(u1, axis=-1, keepdims=True)
        xc = u1 - mu
        rstd = lax.rsqrt(jnp.mean(xc * xc, axis=-1, keepdims=True) + EPS)
        u2 = xc * rstd * g_ref[...] + b_ref[...]
        u3_ref[...] = (u2 * _sigmoid(u2)).astype(BF16)

    vec = pl.BlockSpec((1, dc), lambda i: (0, 0))
    return pl.pallas_call(
        body, name="conv_fwd", grid=(s // tt,),
        in_specs=[pl.BlockSpec((tt, dc), lambda i: (i, 0)),
                  pl.BlockSpec((CONV_HALO, dc), lambda i: (jnp.maximum(i * hpb - 1, 0), 0)),
                  pl.BlockSpec((CONV_HALO, dc), lambda i: (0, 0)), vec, vec, vec],
        out_specs=[pl.BlockSpec((tt, dc), lambda i: (i, 0)), pl.BlockSpec((tt, dc), lambda i: (i, 0))],
        out_shape=[jax.ShapeDtypeStruct((s, dc), F32), jax.ShapeDtypeStruct((s, dc), BF16)],
        scratch_shapes=[pltpu.VMEM((tt + CONV_HALO, dc), F32)],
        compiler_params=_params(dimension_semantics=("arbitrary",)),
    )(u0, u0, taps, b_dw, g_ln, b_ln)


def _conv_bwd_norm(u1, du3, g_ln, b_ln, tr=256):
    dc = u1.shape[1]

    def body(u1_ref, du3_ref, g_ref, b_ref, du1_ref, dg_ref, db_ref, dbdw_ref):
        u1v = u1_ref[...]
        g = g_ref[...]
        mu = jnp.mean(u1v, axis=-1, keepdims=True)
        xc = u1v - mu
        rstd = lax.rsqrt(jnp.mean(xc * xc, axis=-1, keepdims=True) + EPS)
        xhat = xc * rstd
        u2 = xhat * g + b_ref[...]
        sg = _sigmoid(u2)
        du2 = du3_ref[...] * (sg * (1.0 + u2 * (1.0 - sg)))
        _accumulate(dg_ref, jnp.sum(du2 * xhat, axis=0, keepdims=True))
        _accumulate(db_ref, jnp.sum(du2, axis=0, keepdims=True))
        gy = du2 * g
        du1 = rstd * (gy - jnp.mean(gy, axis=-1, keepdims=True) - xhat * jnp.mean(gy * xhat, axis=-1, keepdims=True))
        du1_ref[...] = du1
        _accumulate(dbdw_ref, jnp.sum(du1, axis=0, keepdims=True))

    return _row_call(body, name="conv_bwd_norm", rows=u1.shape[0], tr=tr,
                     ins=[(u1, "row"), (du3, "row"), (g_ln, "vec"), (b_ln, "vec")],
                     outs=[(dc, F32)], acc_outs=(dc, dc, dc))


def _conv_bwd_taps(du1, u0, proj, taps, off_a, tt=256):
    s, dc = u0.shape
    tt = _tile(s, tt)
    n = s // tt
    hpb = tt // CONV_HALO
    ba = off_a // dc
    lead = CONV_HALO - (CONV_WIDTH - 1)

    def body(d_ref, dnext_ref, cur_ref, halo_ref, a_ref, b_ref, w_ref, da_ref, db_ref, sa_ref, sb_ref, dw_ref,
             xs, ds, du0):
        i = pl.program_id(0)
        xs[pl.ds(0, CONV_HALO), :] = jnp.where(i > 0, halo_ref[...], 0.0)
        xs[pl.ds(CONV_HALO, tt), :] = cur_ref[...]
        ds[pl.ds(0, tt), :] = d_ref[...]
        ds[pl.ds(tt, CONV_HALO), :] = jnp.where(i < n - 1, dnext_ref[...], 0.0)

        @pl.when(i == 0)
        def _():
            dw_ref[...] = jnp.zeros_like(dw_ref)

        for c0 in range(0, dc, LANES):
            cols = pl.ds(c0, LANES)
            d_cur = ds[pl.ds(0, tt), cols]
            acc = jnp.zeros((tt, LANES), F32)
            for k in range(CONV_WIDTH):
                acc = acc + w_ref[pl.ds(k, 1), cols] * ds[pl.ds(CONV_WIDTH - 1 - k, tt), cols]
                dw_ref[pl.ds(k, 1), cols] += jnp.sum(d_cur * xs[pl.ds(lead + k, tt), cols], axis=0, keepdims=True)
            du0[:, cols] = acc
        d0 = du0[...]
        sg = _sigmoid(b_ref[...])
        da = d0 * sg
        db = d0 * a_ref[...] * sg * (1.0 - sg)
        da_ref[...] = da.astype(BF16)
        db_ref[...] = db.astype(BF16)
        _accumulate(sa_ref, jnp.sum(da, axis=0, keepdims=True))
        _accumulate(sb_ref, jnp.sum(db, axis=0, keepdims=True))

    row = pl.BlockSpec((tt, dc), lambda i: (i, 0))
    vec = pl.BlockSpec((1, dc), lambda i: (0, 0))
    taps_spec = pl.BlockSpec((CONV_HALO, dc), lambda i: (0, 0))
    return pl.pallas_call(
        body, name="conv_bwd_taps", grid=(n,),
        in_specs=[row, pl.BlockSpec((CONV_HALO, dc), lambda i: (jnp.minimum((i + 1) * hpb, n * hpb - 1), 0)),
                  row, pl.BlockSpec((CONV_HALO, dc), lambda i: (jnp.maximum(i * hpb - 1, 0), 0)),
                  pl.BlockSpec((tt, dc), lambda i: (i, ba)), pl.BlockSpec((tt, dc), lambda i: (i, ba + 1)), taps_spec],
        out_specs=[row, row, vec, vec, taps_spec],
        out_shape=[jax.ShapeDtypeStruct((s, dc), BF16), jax.ShapeDtypeStruct((s, dc), BF16),
                   jax.ShapeDtypeStruct((1, dc), F32), jax.ShapeDtypeStruct((1, dc), F32),
                   jax.ShapeDtypeStruct((CONV_HALO, dc), F32)],
        scratch_shapes=[pltpu.VMEM((tt + CONV_HALO, dc), F32), pltpu.VMEM((tt + CONV_HALO, dc), F32),
                        pltpu.VMEM((tt, dc), F32)],
        compiler_params=_params(dimension_semantics=("arbitrary",)),
    )(du1, du1, u0, u0, proj, proj, taps)


def _elementwise(body, *, name, ins, out_dtypes, tr=128):
    rows, cols = ins[0].shape
    tr = _tile(rows, tr, 8)
    spec = pl.BlockSpec((tr, cols), lambda i: (i, 0))
    return pl.pallas_call(
        body, name=name, grid=(rows // tr,), in_specs=[spec] * len(ins), out_specs=[spec] * len(out_dtypes),
        out_shape=[jax.ShapeDtypeStruct((rows, cols), dt) for dt in out_dtypes],
        compiler_params=_params(dimension_semantics=("parallel",)),
    )(*ins)


def _prefetched(body, *, name, where, grid, in_specs, out_specs, out_shape, operands):
    return pl.pallas_call(
        body, name=name, out_shape=out_shape,
        grid_spec=pltpu.PrefetchScalarGridSpec(num_scalar_prefetch=1, grid=grid, in_specs=in_specs, out_specs=out_specs),
        compiler_params=_params(dimension_semantics=("parallel",) * len(grid)),
    )(where, *operands)


def _cast_into_block(w, where, after, name, tr=256):
    rows, cols = w.shape
    tr = _tile(rows, tr, 16)

    def body(where_ref, w_ref, _, o_ref):
        o_ref[...] = w_ref[...].astype(BF16)

    return _prefetched(
        body, name=name, where=where, grid=(rows // tr,),
        in_specs=[pl.BlockSpec((tr, cols), lambda i, wh: (i, 0)), pl.BlockSpec(memory_space=pl.ANY)],
        out_specs=pl.BlockSpec((None, tr, cols), lambda i, wh: (wh[0], i, 0)),
        out_shape=jax.ShapeDtypeStruct((N_CHIPS, rows, cols), BF16), operands=[w, after])


def _add_pair(grad, theirs, where, name, tr=256):
    nb, half, cols = theirs.shape
    tr = _tile(half, tr, 16)
    nh = half // tr

    def body(where_ref, a_ref, b_ref, o_ref):
        o_ref[...] = (a_ref[...].astype(F32) + b_ref[...].astype(F32)).astype(BF16)

    blk = pl.BlockSpec((None, tr, cols), lambda b, i, wh: (b, i, 0))
    return _prefetched(
        body, name=name, where=where, grid=(nb, nh),
        in_specs=[pl.BlockSpec((None, tr, cols), lambda b, i, wh: (b, wh[1] * nh + i, 0)), blk],
        out_specs=blk, out_shape=jax.ShapeDtypeStruct(theirs.shape, BF16), operands=[grad, theirs])


def _sum_slots(part, slots, where, name, tr=256):
    _, half, cols = part.shape
    tr = _tile(half, tr, 16)
    nh = half // tr

    def body(where_ref, p_ref, s_ref, o_ref):
        o_ref[...] = ((p_ref[...].astype(F32) + s_ref[0].astype(F32)) + s_ref[1].astype(F32)) + s_ref[2].astype(F32)

    return _prefetched(
        body, name=name, where=where, grid=(nh,),
        in_specs=[pl.BlockSpec((None, tr, cols), lambda i, wh: (wh[0], i, 0)),
                  pl.BlockSpec((3, tr, cols), lambda i, wh: (0, i, 0))],
        out_specs=pl.BlockSpec((tr, cols), lambda i, wh: (wh[1] * nh + i, 0)),
        out_shape=jax.ShapeDtypeStruct((2 * half, cols), F32), operands=[part, slots])


def _adamw(w, g, m, v, name):
    bc1 = 1.0 - ADAM_B1 ** ADAM_STEP
    bc2 = 1.0 - ADAM_B2 ** ADAM_STEP

    def body(w_ref, g_ref, m_ref, v_ref, go_ref, d_ref, mo_ref, vo_ref):
        g = g_ref[...]
        m_new = ADAM_B1 * m_ref[...] + (1.0 - ADAM_B1) * g
        v_new = ADAM_B2 * v_ref[...] + (1.0 - ADAM_B2) * (g * g)
        m_hat = m_new / bc1
        v_hat = v_new / bc2
        go_ref[...] = g
        d_ref[...] = -ADAM_LR * (m_hat / (jnp.sqrt(v_hat) + ADAM_EPS) + ADAM_WD * w_ref[...])
        mo_ref[...] = m_new
        vo_ref[...] = v_new

    return _elementwise(body, name=name, ins=[w, g, m, v], out_dtypes=[F32] * 4, tr=128)


ANY = pl.BlockSpec(memory_space=pl.ANY)


def _mesh_pos():
    return lax.axis_index("x"), lax.axis_index("y"), lax.axis_index("c")


def _other_chips(x, y):
    return [(1 - x, y), (x, 1 - y), (1 - x, 1 - y)]


def _remote(src, dst, send_sems, recv_sems, idx, to):
    return pltpu.make_async_remote_copy(src_ref=src, dst_ref=dst, send_sem=send_sems.at[idx], recv_sem=recv_sems.at[idx],
                                        device_id=to, device_id_type=MESH)


HBM = pl.BlockSpec(memory_space=pltpu.HBM)
SEM = pl.BlockSpec(memory_space=pltpu.SEMAPHORE)
EFFECT = pltpu.SideEffectType.DATAFLOW_SIDE_EFFECTING


def _in_hbm(arrays):
    return [pltpu.with_memory_space_constraint(a, pltpu.HBM) for a in arrays]


def _ici_copies(slab, send_sems, recv_sems, x, y, c):
    half = slab.shape[1] // 2
    rows = pl.ds(c * half, half)
    mine = slab.at[2 * x + y, rows]
    out = []
    for j, (cx, cy) in enumerate(_other_chips(x, y)):
        got = slab.at[2 * cx + cy, rows]
        out.append((_remote(mine, mine, send_sems, recv_sems, j, (cx, cy, c)),
                    _remote(got, got, send_sems, recv_sems, j, (cx, cy, c))))
    return out


def _gather_start(slabs, after, name):
    n = len(slabs)

    def body(*refs):
        ins = refs[:n]
        send_sems, recv_sems, token = refs[2 * n + 1:3 * n + 1], refs[3 * n + 1:4 * n + 1], refs[4 * n + 1]
        x, y, c = _mesh_pos()
        for w in range(n):
            for send, _ in _ici_copies(ins[w], send_sems[w], recv_sems[w], x, y, c):
                send.start()
        token[...] = jnp.zeros_like(token)

    outs = pl.pallas_call(
        body, name=name, in_specs=[HBM] * n + [ANY],
        out_specs=[HBM] * n + [SEM] * (2 * n) + [pl.BlockSpec(memory_space=pltpu.VMEM)],
        out_shape=[pltpu.HBM(s.shape, s.dtype) for s in slabs] + [pltpu.SemaphoreType.DMA((3,))] * (2 * n)
        + [jax.ShapeDtypeStruct((8, 128), F32)],
        input_output_aliases={w: w for w in range(n)},
        compiler_params=_params(has_side_effects=EFFECT),
    )(*_in_hbm(slabs), after)
    return (outs[:n], outs[n:2 * n], outs[2 * n:3 * n]), outs[3 * n]


def _gather_wait(slabs, send_sems, recv_sems, after, name):
    n = len(slabs)

    def body(*refs):
        ins = refs[:n]
        sends, recvs = refs[n:2 * n], refs[2 * n:3 * n]
        x, y, c = _mesh_pos()
        for w in range(n):
            for send, recv in _ici_copies(ins[w], sends[w], recvs[w], x, y, c):
                send.wait_send()
                recv.wait_recv()

    return pl.pallas_call(
        body, name=name, in_specs=[HBM] * n + [SEM] * (2 * n) + [ANY], out_specs=[HBM] * n,
        out_shape=[pltpu.HBM(s.shape, s.dtype) for s in slabs],
        input_output_aliases={w: w for w in range(n)},
        compiler_params=_params(has_side_effects=EFFECT),
    )(*slabs, *send_sems, *recv_sems, after)


def _phase_start(arrays, copies, n_copies, after, name):
    n = len(arrays)

    def body(*refs):
        send_sems, recv_sems, token = refs[2 * n + 1], refs[2 * n + 2], refs[2 * n + 3]
        for send, _ in copies(refs[:n], send_sems, recv_sems, *_mesh_pos()):
            send.start()
        token[...] = jnp.zeros_like(token)

    outs = pl.pallas_call(
        body, name=name, in_specs=[HBM] * n + [ANY],
        out_specs=[HBM] * n + [SEM, SEM, pl.BlockSpec(memory_space=pltpu.VMEM)],
        out_shape=[pltpu.HBM(a.shape, a.dtype) for a in arrays]
        + [pltpu.SemaphoreType.DMA((n_copies,))] * 2 + [jax.ShapeDtypeStruct((8, 128), F32)],
        input_output_aliases={k: k for k in range(n)},
        compiler_params=_params(has_side_effects=EFFECT),
    )(*_in_hbm(arrays), after)
    return outs[:n], outs[n], outs[n + 1], outs[n + 2]


def _phase_wait(arrays, send_sems, recv_sems, copies, after, name):
    n = len(arrays)

    def body(*refs):
        for send, recv in copies(refs[:n], refs[n], refs[n + 1], *_mesh_pos()):
            send.wait_send()
            recv.wait_recv()

    return pl.pallas_call(
        body, name=name, in_specs=[HBM] * n + [SEM, SEM, ANY], out_specs=[HBM] * n,
        out_shape=[pltpu.HBM(a.shape, a.dtype) for a in arrays],
        input_output_aliases={k: k for k in range(n)},
        compiler_params=_params(has_side_effects=EFFECT),
    )(*arrays, send_sems, recv_sems, after)


def _forward_copies(slabs, send_sems, recv_sems, x, y, c):
    sibling = (x, y, 1 - c)
    out = []
    for w, slab in enumerate(slabs):
        half = slab.shape[1] // 2
        for j, (cx, cy) in enumerate(_other_chips(x, y)):
            mine = slab.at[2 * cx + cy, pl.ds(c * half, half)]
            theirs = slab.at[2 * cx + cy, pl.ds((1 - c) * half, half)]
            out.append((_remote(mine, mine, send_sems, recv_sems, 3 * w + j, sibling),
                        _remote(theirs, theirs, send_sems, recv_sems, 3 * w + j, sibling)))
    return out


def _exchange_copies(arrays, send_sems, recv_sems, x, y, c):
    n = len(arrays) // 2
    sibling = (x, y, 1 - c)
    out = []
    for w in range(n):
        grad, land = arrays[w], arrays[n + w]
        half = grad.shape[1] // 2
        out.append((_remote(grad.at[:, pl.ds((1 - c) * half, half), :], land, send_sems, recv_sems, w, sibling),
                    _remote(land, land, send_sems, recv_sems, w, sibling)))
    return out


def _gather_taps(taps):
    def body(in_ref, out_ref, send_sems, recv_sems, local_sem):
        x, y, c = _mesh_pos()
        me = 2 * x + y
        chips = _other_chips(x, y)
        lc = pltpu.make_async_copy(in_ref, out_ref.at[me], local_sem)
        lc.start()
        sends = [_remote(in_ref, out_ref.at[me], send_sems, recv_sems, j, (cx, cy, c)) for j, (cx, cy) in enumerate(chips)]
        for cp in sends:
            cp.start()
        for j, (cx, cy) in enumerate(chips):
            got = out_ref.at[2 * cx + cy]
            _remote(got, got, send_sems, recv_sems, j, (cx, cy, c)).wait_recv()
        for cp in sends:
            cp.wait_send()
        lc.wait()

    vm = pl.BlockSpec(memory_space=pltpu.VMEM)
    return pl.pallas_call(
        body, name="gather_taps", in_specs=[vm], out_specs=vm,
        out_shape=jax.ShapeDtypeStruct((N_CHIPS,) + taps.shape, taps.dtype),
        scratch_shapes=[pltpu.SemaphoreType.DMA((3,)), pltpu.SemaphoreType.DMA((3,)), pltpu.SemaphoreType.DMA],
        compiler_params=_params(has_side_effects=True),
    )(taps)


def _exchange_halves(grads, name):
    n = len(grads)

    def body(*refs):
        ins, their_refs = refs[:n], refs[n:2 * n]
        send_sems, recv_sems = refs[2 * n:]
        x, y, c = _mesh_pos()
        sibling = (x, y, 1 - c)
        sends = []
        for w in range(n):
            half = ins[w].shape[1] // 2
            cp = _remote(ins[w].at[:, pl.ds((1 - c) * half, half), :], their_refs[w], send_sems, recv_sems, w, sibling)
            cp.start()
            sends.append(cp)
        for w in range(n):
            _remote(their_refs[w], their_refs[w], send_sems, recv_sems, w, sibling).wait_recv()
        for cp in sends:
            cp.wait_send()

    halves = [jax.ShapeDtypeStruct((g.shape[0], g.shape[1] // 2, g.shape[2]), g.dtype) for g in grads]
    return pl.pallas_call(
        body, name=name, in_specs=[ANY] * n, out_specs=[ANY] * n, out_shape=halves,
        scratch_shapes=[pltpu.SemaphoreType.DMA((n,)), pltpu.SemaphoreType.DMA((n,))],
        compiler_params=_params(has_side_effects=True),
    )(*grads)


def _scatter_copies(part, slots, send_sems, recv_sems, x, y, c):
    out = []
    for j, (cx, cy) in enumerate(_other_chips(x, y)):
        out.append((_remote(part.at[2 * cx + cy], slots.at[j], send_sems, recv_sems, j, (cx, cy, c)),
                    _remote(slots.at[j], slots.at[j], send_sems, recv_sems, j, (cx, cy, c))))
    return out


def _scatter_start(parts, after, name):
    n = len(parts)
    slots = [lax.empty((3,) + p.shape[1:], p.dtype) for p in parts]

    def body(*refs):
        ins, lands = refs[:n], refs[n:2 * n]
        send_sems, recv_sems, token = refs[4 * n + 1:5 * n + 1], refs[5 * n + 1:6 * n + 1], refs[6 * n + 1]
        x, y, c = _mesh_pos()
        for w in range(n):
            for send, _ in _scatter_copies(ins[w], lands[w], send_sems[w], recv_sems[w], x, y, c):
                send.start()
        token[...] = jnp.zeros_like(token)

    outs = pl.pallas_call(
        body, name=name, in_specs=[HBM] * (2 * n) + [ANY],
        out_specs=[HBM] * (2 * n) + [SEM] * (2 * n) + [pl.BlockSpec(memory_space=pltpu.VMEM)],
        out_shape=[pltpu.HBM(a.shape, a.dtype) for a in list(parts) + slots] + [pltpu.SemaphoreType.DMA((3,))] * (2 * n)
        + [jax.ShapeDtypeStruct((8, 128), F32)],
        input_output_aliases={k: k for k in range(2 * n)},
        compiler_params=_params(has_side_effects=EFFECT),
    )(*_in_hbm(list(parts) + slots), after)
    return (outs[:n], outs[n:2 * n], outs[2 * n:3 * n], outs[3 * n:4 * n]), outs[4 * n]


def _scatter_wait(parts, slots, send_sems, recv_sems, after, name):
    n = len(parts)

    def body(*refs):
        ins, lands = refs[:n], refs[n:2 * n]
        sends, recvs = refs[2 * n:3 * n], refs[3 * n:4 * n]
        x, y, c = _mesh_pos()
        for w in range(n):
            for send, recv in _scatter_copies(ins[w], lands[w], sends[w], recvs[w], x, y, c):
                send.wait_send()
                recv.wait_recv()

    outs = pl.pallas_call(
        body, name=name, in_specs=[HBM] * (2 * n) + [SEM] * (2 * n) + [ANY], out_specs=[HBM] * (2 * n),
        out_shape=[pltpu.HBM(a.shape, a.dtype) for a in list(parts) + list(slots)],
        input_output_aliases={k: k for k in range(2 * n)},
        compiler_params=_params(has_side_effects=EFFECT),
    )(*parts, *slots, *send_sems, *recv_sems, after)
    return outs[:n], outs[n:]


def _share_halves(halves, name):
    n = len(halves)

    def body(*refs):
        outs = refs[n:2 * n]
        send_sems, recv_sems = refs[2 * n:]
        x, y, c = _mesh_pos()
        sibling = (x, y, 1 - c)
        sends = []
        for w in range(n):
            h = outs[w].shape[0] // 2
            mine = outs[w].at[pl.ds(c * h, h)]
            cp = _remote(mine, mine, send_sems, recv_sems, w, sibling)
            cp.start()
            sends.append(cp)
        for w in range(n):
            h = outs[w].shape[0] // 2
            got = outs[w].at[pl.ds((1 - c) * h, h)]
            _remote(got, got, send_sems, recv_sems, w, sibling).wait_recv()
        for cp in sends:
            cp.wait_send()

    return pl.pallas_call(
        body, name=name, in_specs=[ANY] * n, out_specs=[ANY] * n,
        out_shape=[jax.ShapeDtypeStruct(h.shape, h.dtype) for h in halves],
        input_output_aliases={w: w for w in range(n)},
        scratch_shapes=[pltpu.SemaphoreType.DMA((n,)), pltpu.SemaphoreType.DMA((n,))],
        compiler_params=_params(has_side_effects=True),
    )(*halves)


def _all_reduce_small(buf):
    rows, width = buf.shape

    def body(in_ref, out_ref, gathered, send_sems, recv_sems):
        x, y, c = _mesh_pos()
        me = 4 * x + 2 * y + c
        gathered[me] = in_ref[...]
        flips = [(dx, dy, dz) for dx in (0, 1) for dy in (0, 1) for dz in (0, 1)][1:]
        peers = [((1 - x) if dx else x, (1 - y) if dy else y, (1 - c) if dz else c) for dx, dy, dz in flips]
        sends = [_remote(in_ref, gathered.at[me], send_sems, recv_sems, k, peer) for k, peer in enumerate(peers)]
        for cp in sends:
            cp.start()
        for k, (px, py, pc) in enumerate(peers):
            got = gathered.at[4 * px + 2 * py + pc]
            _remote(got, got, send_sems, recv_sems, k, (px, py, pc)).wait_recv()
        for cp in sends:
            cp.wait_send()
        total = gathered[0]
        for d in range(1, 8):
            total = total + gathered[d]
        out_ref[...] = total

    vm = pl.BlockSpec(memory_space=pltpu.VMEM)
    return pl.pallas_call(
        body, name="all_reduce_small", in_specs=[vm], out_specs=vm, out_shape=jax.ShapeDtypeStruct(buf.shape, F32),
        scratch_shapes=[pltpu.VMEM((8, rows, width), F32), pltpu.SemaphoreType.DMA((7,)), pltpu.SemaphoreType.DMA((7,))],
        compiler_params=_params(has_side_effects=True),
    )(buf)


PACK_W = 2048
PACK_ROWS = 8


def _pack(vectors):
    flat = jnp.concatenate([v.reshape(-1) for v in vectors])
    unit = PACK_W * PACK_ROWS
    total = -(-flat.shape[0] // unit) * unit
    return jnp.pad(flat, (0, total - flat.shape[0])).reshape(total // PACK_W, PACK_W)


def _unpack(buf, shapes):
    flat = buf.reshape(-1)
    out, pos = [], 0
    for shp in shapes:
        size = math.prod(shp)
        out.append(flat[pos:pos + size].reshape(shp))
        pos += size
    return out


def _pad_rows(a, rows):
    return jnp.pad(a, ((0, rows - a.shape[0]), (0, 0)))


def kernel(x, g_pre_mix, w_in, b_in, w_dw, b_dw, g_conv_ln, b_conv_ln, w_sb_out, w_conv_out, w_o, g_post_mix, g_pre_mlp, w_up, w_down, g_post_mlp, loss_target, m_g_pre_mix, m_w_in, m_b_in, m_w_dw, m_b_dw, m_g_conv_ln, m_b_conv_ln, m_w_sb_out, m_w_conv_out, m_w_o, m_g_post_mix, m_g_pre_mlp, m_w_up, m_w_down, m_g_post_mlp, v_g_pre_mix, v_w_in, v_b_in, v_w_dw, v_b_dw, v_g_conv_ln, v_b_conv_ln, v_w_sb_out, v_w_conv_out, v_w_o, v_g_post_mix, v_g_pre_mlp, v_w_up, v_w_down, v_g_post_mlp):
    xs = x[0]
    tgt = loss_target[0]
    s, d = xs.shape
    d_sb = w_sb_out.shape[1]
    dc = w_conv_out.shape[1]
    d_in = w_in.shape[2] * N_CHIPS
    n_heads = d_sb // HEAD_DIM
    off_a = 3 * d_sb
    off_sb = off_a + 2 * dc
    off_cv = off_sb + d
    gw = dc
    assert off_a % dc == 0 and off_sb % gw == 0 and d % gw == 0 and d_in == off_cv + d
    chip = 2 * lax.axis_index("x") + lax.axis_index("y")

    big = dict(w_in=w_in[0], w_sb_out=w_sb_out[0], w_conv_out=w_conv_out[0], w_o=w_o[0], w_up=w_up[0], w_down=w_down[0])
    names = list(big)
    col_sharded = {"w_in", "w_sb_out", "w_conv_out", "w_up"}
    where = jnp.stack([chip, lax.axis_index("c")]).astype(jnp.int32)
    cs = w_dw.shape[2]
    taps4 = _gather_taps(_pad_rows(w_dw[0], CONV_HALO))
    first, token = _gather_start([_cast_into_block(big["w_in"], where, taps4, "cast_w_in")], taps4, "gather_start_in")
    rest, rest_token = _gather_start([_cast_into_block(big[n], where, token, "cast_" + n) for n in names[1:]], token,
                            "gather_start_rest")
    in_flight = dict(zip(names, zip(*[a + b for a, b in zip(first, rest)])))
    gathered = {}

    def land(group, after, tag):
        thru, sends, recvs = zip(*[in_flight[n] for n in group])
        landed = _gather_wait(thru, sends, recvs, after, "gather_wait_" + tag)
        return _phase_start(landed, _forward_copies, 3 * len(group), where, "forward_start_" + tag)

    def ready(group, started, after, tag):
        slabs, sends, recvs, _ = started
        for n, g4 in zip(group, _phase_wait(slabs, sends, recvs, _forward_copies, after, "forward_wait_" + tag)):
            gathered[n] = g4 if n in col_sharded else g4.reshape(g4.shape[0] * g4.shape[1], g4.shape[2])

    taps = jnp.transpose(taps4, (1, 0, 2)).reshape(CONV_HALO, N_CHIPS * cs)
    group_mix = ["w_sb_out", "w_conv_out", "w_o"]

    h = _rms_fwd(xs, g_pre_mix)
    started = land(["w_in"], rest_token, "in")
    ready(["w_in"], started, started[3], "in")
    proj = _mm(h, gathered["w_in"], name="mm_proj", b_groups=N_CHIPS, bias=b_in, tn=768)
    mix_fwd = land(group_mix, proj, "mix")
    attn = _attn_fwd(proj, mix_fwd[3], n_heads, d_sb)
    u0 = _glu(proj, off_a, dc)
    u1, u3 = _conv_fwd(u0, taps, b_dw, g_conv_ln, b_conv_ln)
    ready(group_mix, mix_fwd, u3, "mix")
    o_sb = _mm(attn, gathered["w_sb_out"], name="mm_o_sb", b_groups=N_CHIPS)
    up_fwd = land(["w_up"], o_sb, "up")
    o_cv = _mm(u3, gathered["w_conv_out"], name="mm_o_cv", b_groups=N_CHIPS, after=up_fwd[3])
    merged = _merge_fwd(proj, o_sb, o_cv, off_sb, off_cv, gw)
    y = _mm(merged, gathered["w_o"], name="mm_y")
    x2, h2 = _resid_rms(xs, y, g_post_mix, g_pre_mlp)
    ready(["w_up"], up_fwd, h2, "up")
    down_fwd = land(["w_down"], h2, "down")
    up, f = _mm(h2, gathered["w_up"], name="mm_up", b_groups=N_CHIPS, out_dtypes=(F32, BF16), after=down_fwd[3],
                epilogue=lambda acc: (acc, jnp.square(jnp.maximum(acc, 0.0))))
    ready(["w_down"], down_fwd, f, "down")
    dn = _mm(f, gathered["w_down"], name="mm_down")
    dx3, d_dn, dg_post_mlp, loss_part = _final(x2, dn, g_post_mlp, tgt)

    moments = dict(w_in=(m_w_in, v_w_in), w_sb_out=(m_w_sb_out, v_w_sb_out), w_conv_out=(m_w_conv_out, v_w_conv_out),
                   w_o=(m_w_o, v_w_o), w_up=(m_w_up, v_w_up), w_down=(m_w_down, v_w_down))
    out_g, out_d, out_m, out_v = {}, {}, {}, {}
    grads = {}

    def blocks_of(group):
        out = []
        for n in group:
            g = grads[n]
            out.append(g if g.ndim == 3 else g.reshape(N_CHIPS, g.shape[0] // N_CHIPS, g.shape[1]))
        return out

    def reduce_start(group, tag):
        blocks = blocks_of(group)
        theirs = _exchange_halves(blocks, "exchange_halves_" + tag)
        pair_sums = [_add_pair(g, t, where, "pair_sum_" + n) for n, g, t in zip(group, blocks, theirs)]
        return _scatter_start(pair_sums, where, "scatter_start_" + tag)

    def exchange_begin(group, tag):
        blocks = blocks_of(group)
        lands = [lax.empty((b.shape[0], b.shape[1] // 2, b.shape[2]), b.dtype) for b in blocks]
        return _phase_start(blocks + lands, _exchange_copies, len(group), where, "exchange_start_" + tag)

    def scatter_begin(group, tag, exchange, after):
        arrays, sends, recvs, _ = exchange
        arrays = _phase_wait(arrays, sends, recvs, _exchange_copies, after, "exchange_wait_" + tag)
        n = len(group)
        pair_sums = [_add_pair(g, t, where, "pair_sum_" + m) for m, g, t in zip(group, arrays[:n], arrays[n:])]
        return _scatter_start(pair_sums, where, "scatter_start_" + tag)

    def reduce_finish(group, tag, started, after):
        parts, slots = _scatter_wait(*started, after, "scatter_wait_" + tag)
        halves = [_sum_slots(p, t, where, "chip_sum_" + n) for n, p, t in zip(group, parts, slots)]
        for n, red in zip(group, _share_halves(halves, "share_halves_" + tag)):
            mm_, vv_ = moments[n]
            res = _adamw(big[n], red, mm_[0], vv_[0], "adamw_" + n)
            out_g[n], out_d[n], out_m[n], out_v[n] = [r[None] for r in res]
        return res[1]

    group_mlp, group_mix, group_in = ["w_down", "w_up"], ["w_o", "w_sb_out", "w_conv_out"], ["w_in"]
    grads["w_down"] = _mm(f, d_dn, name="mm_dw_down", ta=True, out_dtypes=(BF16,))
    dup = _mm(d_dn, gathered["w_down"], name="mm_df", tb=True, extra=[up], out_dtypes=(BF16,),
              epilogue=lambda acc, upv: (acc * (2.0 * jnp.maximum(upv, 0.0)),))
    grads["w_up"] = _mm(h2, dup, name="mm_dw_up", ta=True, out_groups=N_CHIPS, out_dtypes=(BF16,))
    mlp_exchange = exchange_begin(group_mlp, "mlp")
    dh2 = _mm(dup, gathered["w_up"], name="mm_dh2", tb=True, b_groups=N_CHIPS, after=mlp_exchange[3])
    dx2, dy, dg_pre_mlp, dg_post_mix = _rms_bwd2(dx3, dh2, x2, y, g_post_mix, g_pre_mlp)
    mlp_started, mlp_token = scatter_begin(group_mlp, "mlp", mlp_exchange, dx2)
    grads["w_o"] = _mm(merged, dy, name="mm_dw_o", ta=True, out_dtypes=(BF16,), after=mlp_token)
    dmerged = _mm(dy, gathered["w_o"], name="mm_dmerged", tb=True)
    do_sb, do_cv, dgate_sb, dgate_cv, s_gate_sb, s_gate_cv = _merge_bwd(proj, o_sb, o_cv, dmerged, off_sb, off_cv, gw)
    grads["w_sb_out"] = _mm(attn, do_sb, name="mm_dw_sb", ta=True, out_groups=N_CHIPS, out_dtypes=(BF16,))
    dattn = _mm(do_sb, gathered["w_sb_out"], name="mm_dattn", tb=True, b_groups=N_CHIPS)
    grads["w_conv_out"] = _mm(u3, do_cv, name="mm_dw_cv", ta=True, out_groups=N_CHIPS, out_dtypes=(BF16,))
    mix_exchange = exchange_begin(group_mix, "mix")
    du3 = _mm(do_cv, gathered["w_conv_out"], name="mm_du3", tb=True, b_groups=N_CHIPS, after=mix_exchange[3])
    mix_started, mix_token = scatter_begin(group_mix, "mix", mix_exchange, du3)
    dq, dk, dv, s_q, s_k, s_v = _attn_bwd(proj, dattn, mix_token, n_heads, d_sb)
    du1, dg_conv_ln, db_conv_ln, db_dw = _conv_bwd_norm(u1, du3, g_conv_ln, b_conv_ln)
    dglu_a, dglu_b, s_a, s_b, dtaps = _conv_bwd_taps(du1, u0, proj, taps, off_a)
    dproj = jnp.concatenate([dq, dk, dv, dglu_a, dglu_b, dgate_sb, dgate_cv], axis=1)
    db_in = jnp.concatenate([s_q, s_k, s_v, s_a, s_b, s_gate_sb, s_gate_cv], axis=1)
    dh = _mm(dproj, gathered["w_in"], name="mm_dh", tb=True, b_groups=N_CHIPS, tk=2304)
    grad_x, dg_pre_mix = _rms_bwd1(dx2, dh, xs, g_pre_mix)

    small = ["g_pre_mix", "b_in", "b_dw", "g_conv_ln", "b_conv_ln", "g_post_mix", "g_pre_mlp", "g_post_mlp"]
    small_w = dict(g_pre_mix=g_pre_mix, b_in=b_in, b_dw=b_dw, g_conv_ln=g_conv_ln, b_conv_ln=b_conv_ln,
                   g_post_mix=g_post_mix, g_pre_mlp=g_pre_mlp, g_post_mlp=g_post_mlp)
    small_m = dict(g_pre_mix=m_g_pre_mix, b_in=m_b_in, b_dw=m_b_dw, g_conv_ln=m_g_conv_ln, b_conv_ln=m_b_conv_ln,
                   g_post_mix=m_g_post_mix, g_pre_mlp=m_g_pre_mlp, g_post_mlp=m_g_post_mlp)
    small_v = dict(g_pre_mix=v_g_pre_mix, b_in=v_b_in, b_dw=v_b_dw, g_conv_ln=v_g_conv_ln, b_conv_ln=v_b_conv_ln,
                   g_post_mix=v_g_post_mix, g_pre_mlp=v_g_pre_mlp, g_post_mlp=v_g_post_mlp)
    small_g = dict(g_pre_mix=dg_pre_mix, b_in=db_in, b_dw=db_dw, g_conv_ln=dg_conv_ln, b_conv_ln=db_conv_ln,
                   g_post_mix=dg_post_mix, g_pre_mlp=dg_pre_mlp, g_post_mlp=dg_post_mlp)
    shapes = [small_w[n].shape for n in small]
    tail_shapes = [loss_part.shape, dtaps.shape]
    summed = _all_reduce_small(_pack([small_g[n] for n in small] + [loss_part, dtaps]))
    grads["w_in"] = _mm(h, dproj, name="mm_dw_in", ta=True, out_groups=N_CHIPS, out_dtypes=(BF16,), tn=768, after=summed)
    in_started, in_token = reduce_start(group_in, "in")
    zeros_tail = [jnp.zeros(shp, F32) for shp in tail_shapes]
    res = _adamw(_pack([small_w[n] for n in small] + zeros_tail), summed,
                 _pack([small_m[n] for n in small] + zeros_tail), _pack([small_v[n] for n in small] + zeros_tail),
                 "adamw_small")
    unpacked = [_unpack(r, shapes + tail_shapes) for r in res]
    for i, n in enumerate(small):
        out_g[n], out_d[n], out_m[n], out_v[n] = [u[i] for u in unpacked]
    loss = unpacked[0][len(small)][0, 0]
    taps_grad = lax.dynamic_slice(unpacked[0][len(small) + 1], (0, chip * cs), (CONV_HALO, cs))
    res = _adamw(_pad_rows(w_dw[0], CONV_HALO), taps_grad, _pad_rows(m_w_dw[0], CONV_HALO), _pad_rows(v_w_dw[0], CONV_HALO),
                 "adamw_taps")
    out_g["w_dw"], out_d["w_dw"], out_m["w_dw"], out_v["w_dw"] = [r[:CONV_WIDTH][None] for r in res]
    done = reduce_finish(group_mlp, "mlp", mlp_started, in_token)
    done = reduce_finish(group_mix, "mix", mix_started, done)
    reduce_finish(group_in, "in", in_started, done)

    order = ["g_pre_mix", "w_in", "b_in", "w_dw", "b_dw", "g_conv_ln", "b_conv_ln", "w_sb_out", "w_conv_out", "w_o",
             "g_post_mix", "g_pre_mlp", "w_up", "w_down", "g_post_mlp"]
    return (loss, grad_x[None], *[out_g[n] for n in order], *[out_d[n] for n in order],
            *[out_m[n] for n in order], *[out_v[n] for n in order])
```

```python
import functools
import math

import jax
import jax.numpy as jnp
from jax import lax
from jax.experimental import pallas as pl
from jax.experimental.pallas import tpu as pltpu

F32 = jnp.float32
BF16 = jnp.bfloat16
MESH = pl.DeviceIdType.MESH

HEAD_DIM = 128
CONV_WIDTH = 31
CONV_HALO = 32
EPS = 1e-6
ADAM_LR = 0.001
ADAM_B1 = 0.9
ADAM_B2 = 0.999
ADAM_EPS = 1e-08
ADAM_WD = 0.01
ADAM_STEP = 10
N_CHIPS = 4
VMEM_LIMIT = 48 * 1024 * 1024


def _params(**kw):
    return pltpu.CompilerParams(vmem_limit_bytes=VMEM_LIMIT, **kw)


def _tile(n, want, align=128):
    if n <= want:
        return n
    for t in range(want - want % align, 0, -align):
        if n % t == 0:
            return t
    raise ValueError((n, want, align))


def _mm(a, b, *, name, ta=False, tb=False, b_groups=1, out_groups=1, bias=None, extra=None, epilogue=None,
        out_dtypes=(F32,), tm=1024, tn=1024, tk=2048, after=None):
    if ta:
        K, M = a.shape
    else:
        M, K = a.shape
    if b_groups == 1:
        br, bc = b.shape
    else:
        _, br, bcg = b.shape
        bc = bcg * b_groups
    if tb:
        N, Kb = br, bc
    else:
        Kb, N = br, bc
    assert Kb == K, (name, a.shape, b.shape)
    tm, tn, tk = _tile(M, tm), _tile(N, tn), _tile(K, tk)
    if b_groups > 1:
        if tb:
            tk = _tile(K // b_groups, tk)
        else:
            tn = _tile(N // b_groups, tn)
    if out_groups > 1:
        tn = _tile(N // out_groups, tn)
        if b_groups > 1 and not tb:
            tn = _tile(N // b_groups, tn)
    nm, nn, nk = M // tm, N // tn, K // tk
    extra = tuple(extra or ())
    n_extra = len(extra)
    has_bias = bias is not None
    n_out = len(out_dtypes)

    a_spec = pl.BlockSpec((tk, tm), lambda i, j, k: (k, i)) if ta else pl.BlockSpec((tm, tk), lambda i, j, k: (i, k))
    if b_groups == 1:
        b_spec = pl.BlockSpec((tn, tk), lambda i, j, k: (j, k)) if tb else pl.BlockSpec((tk, tn), lambda i, j, k: (k, j))
    elif tb:
        kpg = (K // b_groups) // tk
        b_spec = pl.BlockSpec((None, tn, tk), lambda i, j, k: (k // kpg, j, k % kpg))
    else:
        npg = (N // b_groups) // tn
        b_spec = pl.BlockSpec((None, tk, tn), lambda i, j, k: (j // npg, k, j % npg))
    in_specs = [a_spec, b_spec]
    operands = [a, b]
    if has_bias:
        in_specs.append(pl.BlockSpec((1, tn), lambda i, j, k: (0, j)))
        operands.append(bias)
    for e in extra:
        in_specs.append(pl.BlockSpec((tm, tn), lambda i, j, k: (i, j)))
        operands.append(e)
    n_after = 0 if after is None else 1
    if after is not None:
        in_specs.append(pl.BlockSpec(memory_space=pl.ANY))
        operands.append(after)
    if out_groups == 1:
        o_spec = pl.BlockSpec((tm, tn), lambda i, j, k: (i, j))
        o_shape = (M, N)
    else:
        opg = (N // out_groups) // tn
        o_spec = pl.BlockSpec((None, tm, tn), lambda i, j, k: (j // opg, i, j % opg))
        o_shape = (out_groups, M, N // out_groups)
    dims = (((0 if ta else 1,), (1 if tb else 0,)), ((), ()))

    def body(*refs):
        a_ref, b_ref = refs[0], refs[1]
        pos = 2
        bias_ref = None
        if has_bias:
            bias_ref = refs[pos]
            pos += 1
        extra_refs = refs[pos:pos + n_extra]
        pos += n_extra + n_after
        out_refs = refs[pos:pos + n_out]
        pos += n_out
        acc_ref = refs[pos] if nk > 1 else None

        part = lax.dot_general(a_ref[...].astype(BF16), b_ref[...].astype(BF16), dims, preferred_element_type=F32)

        def finish(acc):
            if has_bias:
                acc = acc + bias_ref[...]
            outs = epilogue(acc, *[e[...] for e in extra_refs]) if epilogue is not None else (acc,)
            for o_ref, o in zip(out_refs, outs):
                o_ref[...] = o.astype(o_ref.dtype)

        if nk == 1:
            finish(part)
        else:
            k = pl.program_id(2)

            @pl.when(k == 0)
            def _():
                acc_ref[...] = part

            @pl.when(k > 0)
            def _():
                acc_ref[...] += part

            @pl.when(k == nk - 1)
            def _():
                finish(acc_ref[...])

    outs = pl.pallas_call(
        body,
        name=name,
        grid=(nm, nn, nk),
        in_specs=in_specs,
        out_specs=[o_spec] * n_out,
        out_shape=[jax.ShapeDtypeStruct(o_shape, dt) for dt in out_dtypes],
        scratch_shapes=[pltpu.VMEM((tm, tn), F32)] if nk > 1 else [],
        compiler_params=_params(dimension_semantics=("parallel", "parallel", "arbitrary")),
    )(*operands)
    return outs[0] if n_out == 1 else outs


def _rms_stats(x):
    return lax.rsqrt(jnp.mean(x * x, axis=-1, keepdims=True) + EPS)


def _rms_bwd(x, r, g, dy):
    gy = dy * g
    return r * gy - x * (r * r * r) * jnp.mean(x * gy, axis=-1, keepdims=True)


def _sigmoid(x):
    return 1.0 / (1.0 + jnp.exp(-x))


def _row_call(body, *, name, rows, tr, ins, outs, acc_outs=()):
    n = rows // tr
    in_specs = []
    for arr, kind in ins:
        if kind == "row":
            in_specs.append(pl.BlockSpec((tr, arr.shape[1]), lambda i: (i, 0)))
        else:
            in_specs.append(pl.BlockSpec(arr.shape, lambda i: (0, 0)))
    out_specs = [pl.BlockSpec((tr, w), lambda i: (i, 0)) for w, _ in outs]
    out_shape = [jax.ShapeDtypeStruct((rows, w), dt) for w, dt in outs]
    out_specs += [pl.BlockSpec((1, w), lambda i: (0, 0)) for w in acc_outs]
    out_shape += [jax.ShapeDtypeStruct((1, w), F32) for w in acc_outs]
    return pl.pallas_call(
        body, name=name, grid=(n,), in_specs=in_specs, out_specs=out_specs, out_shape=out_shape,
        compiler_params=_params(dimension_semantics=("arbitrary",)),
    )(*[a for a, _ in ins])


def _accumulate(ref, val):
    @pl.when(pl.program_id(0) == 0)
    def _():
        ref[...] = val

    @pl.when(pl.program_id(0) > 0)
    def _():
        ref[...] += val


def _rms_fwd(x, g, tr=256):
    def body(x_ref, g_ref, h_ref):
        xv = x_ref[...]
        h_ref[...] = (xv * _rms_stats(xv) * g_ref[...]).astype(BF16)

    (h,) = _row_call(body, name="rms_fwd", rows=x.shape[0], tr=tr, ins=[(x, "row"), (g, "vec")], outs=[(x.shape[1], BF16)])
    return h


def _resid_rms(x, y, g2, g3, tr=256):
    def body(x_ref, y_ref, g2_ref, g3_ref, x2_ref, h2_ref):
        yv = y_ref[...]
        x2 = x_ref[...] + yv * _rms_stats(yv) * g2_ref[...]
        x2_ref[...] = x2
        h2_ref[...] = (x2 * _rms_stats(x2) * g3_ref[...]).astype(BF16)

    d = x.shape[1]
    return _row_call(body, name="resid_rms", rows=x.shape[0], tr=tr,
                     ins=[(x, "row"), (y, "row"), (g2, "vec"), (g3, "vec")], outs=[(d, F32), (d, BF16)])


def _final(x2, dn, g4, tgt, tr=256):
    d = x2.shape[1]

    def body(x2_ref, dn_ref, g4_ref, t_ref, dx3_ref, ddn_ref, dg4_ref, loss_ref):
        dn_v = dn_ref[...]
        r = _rms_stats(dn_v)
        g = g4_ref[...]
        e = x2_ref[...] + dn_v * r * g - t_ref[...]
        dx3 = e * (1.0 / d)
        dx3_ref[...] = dx3
        ddn_ref[...] = _rms_bwd(dn_v, r, g, dx3).astype(BF16)
        _accumulate(dg4_ref, jnp.sum(dx3 * dn_v * r, axis=0, keepdims=True))
        part = 0.5 * jnp.sum(jnp.mean(e * e, axis=-1, keepdims=True), axis=0, keepdims=True)
        _accumulate(loss_ref, jnp.broadcast_to(part, loss_ref.shape))

    return _row_call(body, name="final", rows=x2.shape[0], tr=tr,
                     ins=[(x2, "row"), (dn, "row"), (g4, "vec"), (tgt, "row")],
                     outs=[(d, F32), (d, BF16)], acc_outs=(d, 128))


def _rms_bwd2(dx3, dh2, x2, y, g2, g3, tr=256):
    d = x2.shape[1]

    def body(dx3_ref, dh2_ref, x2_ref, y_ref, g2_ref, g3_ref, dx2_ref, dy_ref, dg3_ref, dg2_ref):
        x2v, dh2v, yv = x2_ref[...], dh2_ref[...], y_ref[...]
        r3 = _rms_stats(x2v)
        dx2 = dx3_ref[...] + _rms_bwd(x2v, r3, g3_ref[...], dh2v)
        dx2_ref[...] = dx2
        _accumulate(dg3_ref, jnp.sum(dh2v * x2v * r3, axis=0, keepdims=True))
        r2 = _rms_stats(yv)
        dy_ref[...] = _rms_bwd(yv, r2, g2_ref[...], dx2).astype(BF16)
        _accumulate(dg2_ref, jnp.sum(dx2 * yv * r2, axis=0, keepdims=True))

    return _row_call(body, name="rms_bwd2", rows=x2.shape[0], tr=tr,
                     ins=[(dx3, "row"), (dh2, "row"), (x2, "row"), (y, "row"), (g2, "vec"), (g3, "vec")],
                     outs=[(d, F32), (d, BF16)], acc_outs=(d, d))


def _rms_bwd1(dx2, dh, x, g1, tr=256):
    d = x.shape[1]

    def body(dx2_ref, dh_ref, x_ref, g1_ref, gx_ref, dg1_ref):
        xv, dhv = x_ref[...], dh_ref[...]
        r = _rms_stats(xv)
        gx_ref[...] = dx2_ref[...] + _rms_bwd(xv, r, g1_ref[...], dhv)
        _accumulate(dg1_ref, jnp.sum(dhv * xv * r, axis=0, keepdims=True))

    return _row_call(body, name="rms_bwd1", rows=x.shape[0], tr=tr,
                     ins=[(dx2, "row"), (dh, "row"), (x, "row"), (g1, "vec")], outs=[(d, F32)], acc_outs=(d,))


def _merge_fwd(proj, o_sb, o_cv, off_sb, off_cv, gw, tr=256):
    s, d = o_sb.shape
    nj = d // gw
    b_sb, b_cv = off_sb // gw, off_cv // gw

    def body(gs_ref, gc_ref, osb_ref, ocv_ref, m_ref):
        m_ref[...] = (_sigmoid(gs_ref[...]) * osb_ref[...] + _sigmoid(gc_ref[...]) * ocv_ref[...]).astype(BF16)

    blk = lambda i, j: (i, j)
    return pl.pallas_call(
        body, name="merge_fwd", grid=(s // tr, nj),
        in_specs=[pl.BlockSpec((tr, gw), lambda i, j: (i, b_sb + j)), pl.BlockSpec((tr, gw), lambda i, j: (i, b_cv + j)),
                  pl.BlockSpec((tr, gw), blk), pl.BlockSpec((tr, gw), blk)],
        out_specs=pl.BlockSpec((tr, gw), blk), out_shape=jax.ShapeDtypeStruct((s, d), BF16),
        compiler_params=_params(dimension_semantics=("parallel", "parallel")),
    )(proj, proj, o_sb, o_cv)


def _merge_bwd(proj, o_sb, o_cv, dmerged, off_sb, off_cv, gw, tr=256):
    s, d = o_sb.shape
    nj = d // gw
    ni = s // tr
    b_sb, b_cv = off_sb // gw, off_cv // gw

    def body(gs_ref, gc_ref, osb_ref, ocv_ref, dm_ref, dosb_ref, docv_ref, dgs_ref, dgc_ref, sgs_ref, sgc_ref):
        dm = dm_ref[...]
        s_sb, s_cv = _sigmoid(gs_ref[...]), _sigmoid(gc_ref[...])
        dosb_ref[...] = (dm * s_sb).astype(BF16)
        docv_ref[...] = (dm * s_cv).astype(BF16)
        dgs = dm * osb_ref[...] * s_sb * (1.0 - s_sb)
        dgc = dm * ocv_ref[...] * s_cv * (1.0 - s_cv)
        dgs_ref[...] = dgs.astype(BF16)
        dgc_ref[...] = dgc.astype(BF16)
        i = pl.program_id(1)
        for ref, val in ((sgs_ref, dgs), (sgc_ref, dgc)):
            col = jnp.sum(val, axis=0, keepdims=True)

            @pl.when(i == 0)
            def _():
                ref[...] = col

            @pl.when(i > 0)
            def _():
                ref[...] += col

    blk = lambda j, i: (i, j)
    outs = pl.pallas_call(
        body, name="merge_bwd", grid=(nj, ni),
        in_specs=[pl.BlockSpec((tr, gw), lambda j, i: (i, b_sb + j)), pl.BlockSpec((tr, gw), lambda j, i: (i, b_cv + j)),
                  pl.BlockSpec((tr, gw), blk), pl.BlockSpec((tr, gw), blk), pl.BlockSpec((tr, gw), blk)],
        out_specs=[pl.BlockSpec((tr, gw), blk), pl.BlockSpec((tr, gw), blk),
                   pl.BlockSpec((tr, gw), blk), pl.BlockSpec((tr, gw), blk),
                   pl.BlockSpec((1, gw), lambda j, i: (0, j)), pl.BlockSpec((1, gw), lambda j, i: (0, j))],
        out_shape=[jax.ShapeDtypeStruct((s, d), BF16)] * 4 + [jax.ShapeDtypeStruct((1, d), F32)] * 2,
        compiler_params=_params(dimension_semantics=("parallel", "arbitrary")),
    )(proj, proj, o_sb, o_cv, dmerged)
    return outs


def _split_bf16(v):
    hi = v.astype(BF16)
    lo = (v - hi.astype(F32)).astype(BF16)
    return hi, lo


def _dot_nt(a, b):
    return lax.dot_general(a, b, (((1,), (1,)), ((), ())), preferred_element_type=F32)


def _dot_tn(a, b):
    return lax.dot_general(a, b, (((0,), (0,)), ((), ())), preferred_element_type=F32)


def _dot_nn(a, b):
    return lax.dot_general(a, b, (((1,), (0,)), ((), ())), preferred_element_type=F32)


def _sum_right(v, tri):
    hi, lo = _split_bf16(v)
    return _dot_nn(hi, tri) + _dot_nn(lo, tri)


HEADS_PER_STEP = 2


LOG2_E = 1.4426950408889634


def _sb_tile(q, kb, scale, mask):
    z = _dot_nt(q, kb) * (scale * LOG2_E)
    log_b = jnp.minimum(z, 0.0) - jnp.log2(1.0 + jnp.exp2(-jnp.abs(z)))
    log_1m = log_b - z
    if mask is not None:
        log_1m = jnp.where(mask, log_1m, 0.0)
    return log_b, log_1m


def _tile_mask(tq):
    row = lax.broadcasted_iota(jnp.int32, (tq, tq), 0)
    col = lax.broadcasted_iota(jnp.int32, (tq, tq), 1)
    return col < row


def _tri(tq, before):
    r = lax.broadcasted_iota(jnp.int32, (tq, tq), 0)
    c = lax.broadcasted_iota(jnp.int32, (tq, tq), 1)
    return jnp.where((r < c) if before else (r > c), 1.0, 0.0).astype(BF16)


def _attn_fwd(proj, after, n_heads, d_sb, tq=256):
    s = proj.shape[0]
    tq = _tile(s, tq)
    hb = d_sb // HEAD_DIM
    scale = 1.0 / math.sqrt(HEAD_DIM)

    heads = [pl.ds(n * HEAD_DIM, HEAD_DIM) for n in range(HEADS_PER_STEP)]
    wide = HEADS_PER_STEP * HEAD_DIM
    hb //= HEADS_PER_STEP

    def body(q_ref, k_ref, v_ref, _, o_ref):
        i = pl.program_id(1)
        tri = _tri(tq, False)

        def tile(j, carry, diagonal):
            rows = pl.ds(pl.multiple_of(j * tq, tq), tq)
            mask = _tile_mask(tq) if diagonal else None
            out = []
            for hd, (c_l, acc) in zip(heads, carry):
                log_b, log_1m = _sb_tile(q_ref[:, hd].astype(BF16), k_ref[rows, hd].astype(BF16), scale, mask)
                a = jnp.exp2(log_b + (c_l + _sum_right(log_1m, tri)))
                if diagonal:
                    a = jnp.where(mask, a, 0.0)
                out.append((c_l + jnp.sum(log_1m, axis=1, keepdims=True),
                            acc + _dot_nn(a.astype(BF16), v_ref[rows, hd].astype(BF16))))
            return tuple(out)

        init = tuple((jnp.zeros((tq, 1), F32), jnp.zeros((tq, HEAD_DIM), F32)) for _ in heads)
        carry = lax.fori_loop(0, i, lambda jj, c: tile(i - 1 - jj, c, False), tile(i, init, True))
        for hd, (_, acc) in zip(heads, carry):
            o_ref[:, hd] = acc.astype(BF16)

    return pl.pallas_call(
        body, name="attn_fwd", grid=(n_heads // HEADS_PER_STEP, s // tq),
        in_specs=[pl.BlockSpec((tq, wide), lambda h, i: (i, h)),
                  pl.BlockSpec((s, wide), lambda h, i: (0, hb + h)),
                  pl.BlockSpec((s, wide), lambda h, i: (0, 2 * hb + h)), pl.BlockSpec(memory_space=pl.ANY)],
        out_specs=pl.BlockSpec((tq, wide), lambda h, i: (i, h)),
        out_shape=jax.ShapeDtypeStruct((s, d_sb), BF16),
        compiler_params=_params(dimension_semantics=("parallel", "arbitrary")),
    )(proj, proj, proj, after)


def _attn_bwd(proj, dattn, after, n_heads, d_sb, tq=256):
    s = proj.shape[0]
    tq = _tile(s, tq)
    nq = s // tq
    hb = d_sb // HEAD_DIM // HEADS_PER_STEP
    scale = 1.0 / math.sqrt(HEAD_DIM)
    heads = [pl.ds(n * HEAD_DIM, HEAD_DIM) for n in range(HEADS_PER_STEP)]
    wide = HEADS_PER_STEP * HEAD_DIM

    def body(q_ref, k_ref, v_ref, do_ref, _, dq_ref, dk_ref, dv_ref, sq_ref, sk_ref, sv_ref, dk_acc, dv_acc, g_st, b_st):
        i = pl.program_id(1)

        @pl.when(i == 0)
        def _():
            dk_acc[...] = jnp.zeros_like(dk_acc)
            dv_acc[...] = jnp.zeros_like(dv_acc)

        tri_after = _tri(tq, False)
        tri_before = _tri(tq, True)

        def newest_first(j, c_ls, diagonal):
            rows = pl.ds(pl.multiple_of(j * tq, tq), tq)
            mask = _tile_mask(tq) if diagonal else None
            out = []
            for n, (hd, c_l) in enumerate(zip(heads, c_ls)):
                do = do_ref[:, hd].astype(BF16)
                log_b, log_1m = _sb_tile(q_ref[:, hd].astype(BF16), k_ref[rows, hd].astype(BF16), scale, mask)
                a = jnp.exp2(log_b + (c_l + _sum_right(log_1m, tri_after)))
                beta = jnp.exp2(log_b)
                if diagonal:
                    a = jnp.where(mask, a, 0.0)
                    beta = jnp.where(mask, beta, 0.0)
                g_st[n, j] = a * _dot_nt(do, v_ref[rows, hd].astype(BF16))
                b_st[n, j] = beta
                dv_acc[rows, hd] += _dot_tn(a.astype(BF16), do)
                out.append(c_l + jnp.sum(log_1m, axis=1, keepdims=True))
            return tuple(out)

        lax.fori_loop(0, i, lambda jj, c: newest_first(i - 1 - jj, c, False),
                      newest_first(i, tuple(jnp.zeros((tq, 1), F32) for _ in heads), True))

        def oldest_first(j, carry):
            rows = pl.ds(pl.multiple_of(j * tq, tq), tq)
            out = []
            for n, (hd, (c_g, dq)) in enumerate(zip(heads, carry)):
                kb = k_ref[rows, hd].astype(BF16)
                g = g_st[n, j]
                beta = b_st[n, j]
                g_before = c_g + _sum_right(g, tri_before)
                dz16 = ((g * (1.0 - beta) - g_before * beta) * scale).astype(BF16)
                dk_acc[rows, hd] += _dot_tn(dz16, q_ref[:, hd].astype(BF16))
                out.append((c_g + jnp.sum(g, axis=1, keepdims=True), dq + _dot_nn(dz16, kb)))
            return tuple(out)

        init = tuple((jnp.zeros((tq, 1), F32), jnp.zeros((tq, HEAD_DIM), F32)) for _ in heads)
        carry = lax.fori_loop(0, i + 1, oldest_first, init)
        dq = jnp.concatenate([c[1] for c in carry], axis=1)
        dq_ref[...] = dq.astype(BF16)
        col = jnp.sum(dq, axis=0, keepdims=True)

        @pl.when(i == 0)
        def _():
            sq_ref[...] = col

        @pl.when(i > 0)
        def _():
            sq_ref[...] += col

        @pl.when(i == nq - 1)
        def _():
            dk = dk_acc[...]
            dv = dv_acc[...]
            dk_ref[...] = dk.astype(BF16)
            dv_ref[...] = dv.astype(BF16)
            sk_ref[...] = jnp.sum(dk, axis=0, keepdims=True)
            sv_ref[...] = jnp.sum(dv, axis=0, keepdims=True)

    qblk = pl.BlockSpec((tq, wide), lambda h, i: (i, h))
    kblk = pl.BlockSpec((s, wide), lambda h, i: (0, h))
    col_sum = pl.BlockSpec((1, wide), lambda h, i: (0, h))
    stash = pltpu.VMEM((HEADS_PER_STEP, nq, tq, tq), F32)
    outs = pl.pallas_call(
        body, name="attn_bwd", grid=(n_heads // HEADS_PER_STEP, nq),
        in_specs=[qblk,
                  pl.BlockSpec((s, wide), lambda h, i: (0, hb + h)),
                  pl.BlockSpec((s, wide), lambda h, i: (0, 2 * hb + h)),
                  qblk, pl.BlockSpec(memory_space=pl.ANY)],
        out_specs=[qblk, kblk, kblk, col_sum, col_sum, col_sum],
        out_shape=[jax.ShapeDtypeStruct((s, d_sb), BF16)] * 3 + [jax.ShapeDtypeStruct((1, d_sb), F32)] * 3,
        scratch_shapes=[pltpu.VMEM((s, wide), F32), pltpu.VMEM((s, wide), F32), stash, stash],
        compiler_params=_params(dimension_semantics=("parallel", "arbitrary")),
    )(proj, proj, proj, dattn, after)
    return outs


LANES = 128


def _glu(proj, off_a, dc, tr=256):
    s = proj.shape[0]
    ba = off_a // dc

    def body(a_ref, b_ref, u_ref):
        u_ref[...] = a_ref[...] * _sigmoid(b_ref[...])

    return pl.pallas_call(
        body, name="glu", grid=(s // tr,),
        in_specs=[pl.BlockSpec((tr, dc), lambda i: (i, ba)), pl.BlockSpec((tr, dc), lambda i: (i, ba + 1))],
        out_specs=pl.BlockSpec((tr, dc), lambda i: (i, 0)), out_shape=jax.ShapeDtypeStruct((s, dc), F32),
        compiler_params=_params(dimension_semantics=("parallel",)),
    )(proj, proj)


def _conv_fwd(u0, taps, b_dw, g_ln, b_ln, tt=256):
    s, dc = u0.shape
    tt = _tile(s, tt)
    hpb = tt // CONV_HALO

    def body(cur_ref, halo_ref, w_ref, bdw_ref, g_ref, b_ref, u1_ref, u3_ref, xs):
        i = pl.program_id(0)
        xs[pl.ds(0, CONV_HALO), :] = jnp.where(i > 0, halo_ref[...], 0.0)
        xs[pl.ds(CONV_HALO, tt), :] = cur_ref[...]
        for c0 in range(0, dc, LANES):
            cols = pl.ds(c0, LANES)
            acc = jnp.broadcast_to(bdw_ref[:, cols], (tt, LANES))
            for k in range(CONV_WIDTH):
                acc = acc + w_ref[pl.ds(k, 1), cols] * xs[pl.ds(CONV_HALO - (CONV_WIDTH - 1) + k, tt), cols]
            u1_ref[:, cols] = acc
        u1 = u1_ref[...]
        mu = jnp.mean(u1, axis=-1, keepdims=True)
        xc = u1 - mu
        rstd = lax.rsqrt(jnp.mean(xc * xc, axis=-1, keepdims=True) + EPS)
        u2 = xc * rstd * g_ref[...] + b_ref[...]
        u3_ref[...] = (u2 * _sigmoid(u2)).astype(BF16)

    vec = pl.BlockSpec((1, dc), lambda i: (0, 0))
    return pl.pallas_call(
        body, name="conv_fwd", grid=(s // tt,),
        in_specs=[pl.BlockSpec((tt, dc), lambda i: (i, 0)),
                  pl.BlockSpec((CONV_HALO, dc), lambda i: (jnp.maximum(i * hpb - 1, 0), 0)),
                  pl.BlockSpec((CONV_HALO, dc), lambda i: (0, 0)), vec, vec, vec],
        out_specs=[pl.BlockSpec((tt, dc), lambda i: (i, 0)), pl.BlockSpec((tt, dc), lambda i: (i, 0))],
        out_shape=[jax.ShapeDtypeStruct((s, dc), F32), jax.ShapeDtypeStruct((s, dc), BF16)],
        scratch_shapes=[pltpu.VMEM((tt + CONV_HALO, dc), F32)],
        compiler_params=_params(dimension_semantics=("arbitrary",)),
    )(u0, u0, taps, b_dw, g_ln, b_ln)


def _conv_bwd_norm(u1, du3, g_ln, b_ln, tr=256):
    dc = u1.shape[1]

    def body(u1_ref, du3_ref, g_ref, b_ref, du1_ref, dg_ref, db_ref, dbdw_ref):
        u1v = u1_ref[...]
        g = g_ref[...]
        mu = jnp.mean(u1v, axis=-1, keepdims=True)
        xc = u1v - mu
        rstd = lax.rsqrt(jnp.mean(xc * xc, axis=-1, keepdims=True) + EPS)
        xhat = xc * rstd
        u2 = xhat * g + b_ref[...]
        sg = _sigmoid(u2)
        du2 = du3_ref[...] * (sg * (1.0 + u2 * (1.0 - sg)))
        _accumulate(dg_ref, jnp.sum(du2 * xhat, axis=0, keepdims=True))
        _accumulate(db_ref, jnp.sum(du2, axis=0, keepdims=True))
        gy = du2 * g
        du1 = rstd * (gy - jnp.mean(gy, axis=-1, keepdims=True) - xhat * jnp.mean(gy * xhat, axis=-1, keepdims=True))
        du1_ref[...] = du1
        _accumulate(dbdw_ref, jnp.sum(du1, axis=0, keepdims=True))

    return _row_call(body, name="conv_bwd_norm", rows=u1.shape[0], tr=tr,
                     ins=[(u1, "row"), (du3, "row"), (g_ln, "vec"), (b_ln, "vec")],
                     outs=[(dc, F32)], acc_outs=(dc, dc, dc))


def _conv_bwd_taps(du1, u0, proj, taps, off_a, tt=256):
    s, dc = u0.shape
    tt = _tile(s, tt)
    n = s // tt
    hpb = tt // CONV_HALO
    ba = off_a // dc
    lead = CONV_HALO - (CONV_WIDTH - 1)

    def body(d_ref, dnext_ref, cur_ref, halo_ref, a_ref, b_ref, w_ref, da_ref, db_ref, sa_ref, sb_ref, dw_ref,
             xs, ds, du0):
        i = pl.program_id(0)
        xs[pl.ds(0, CONV_HALO), :] = jnp.where(i > 0, halo_ref[...], 0.0)
        xs[pl.ds(CONV_HALO, tt), :] = cur_ref[...]
        ds[pl.ds(0, tt), :] = d_ref[...]
        ds[pl.ds(tt, CONV_HALO), :] = jnp.where(i < n - 1, dnext_ref[...], 0.0)

        @pl.when(i == 0)
        def _():
            dw_ref[...] = jnp.zeros_like(dw_ref)

        for c0 in range(0, dc, LANES):
            cols = pl.ds(c0, LANES)
            d_cur = ds[pl.ds(0, tt), cols]
            acc = jnp.zeros((tt, LANES), F32)
            for k in range(CONV_WIDTH):
                acc = acc + w_ref[pl.ds(k, 1), cols] * ds[pl.ds(CONV_WIDTH - 1 - k, tt), cols]
                dw_ref[pl.ds(k, 1), cols] += jnp.sum(d_cur * xs[pl.ds(lead + k, tt), cols], axis=0, keepdims=True)
            du0[:, cols] = acc
        d0 = du0[...]
        sg = _sigmoid(b_ref[...])
        da = d0 * sg
        db = d0 * a_ref[...] * sg * (1.0 - sg)
        da_ref[...] = da.astype(BF16)
        db_ref[...] = db.astype(BF16)
        _accumulate(sa_ref, jnp.sum(da, axis=0, keepdims=True))
        _accumulate(sb_ref, jnp.sum(db, axis=0, keepdims=True))

    row = pl.BlockSpec((tt, dc), lambda i: (i, 0))
    vec = pl.BlockSpec((1, dc), lambda i: (0, 0))
    taps_spec = pl.BlockSpec((CONV_HALO, dc), lambda i: (0, 0))
    return pl.pallas_call(
        body, name="conv_bwd_taps", grid=(n,),
        in_specs=[row, pl.BlockSpec((CONV_HALO, dc), lambda i: (jnp.minimum((i + 1) * hpb, n * hpb - 1), 0)),
                  row, pl.BlockSpec((CONV_HALO, dc), lambda i: (jnp.maximum(i * hpb - 1, 0), 0)),
                  pl.BlockSpec((tt, dc), lambda i: (i, ba)), pl.BlockSpec((tt, dc), lambda i: (i, ba + 1)), taps_spec],
        out_specs=[row, row, vec, vec, taps_spec],
        out_shape=[jax.ShapeDtypeStruct((s, dc), BF16), jax.ShapeDtypeStruct((s, dc), BF16),
                   jax.ShapeDtypeStruct((1, dc), F32), jax.ShapeDtypeStruct((1, dc), F32),
                   jax.ShapeDtypeStruct((CONV_HALO, dc), F32)],
        scratch_shapes=[pltpu.VMEM((tt + CONV_HALO, dc), F32), pltpu.VMEM((tt + CONV_HALO, dc), F32),
                        pltpu.VMEM((tt, dc), F32)],
        compiler_params=_params(dimension_semantics=("arbitrary",)),
    )(du1, du1, u0, u0, proj, proj, taps)


def _elementwise(body, *, name, ins, out_dtypes, tr=128):
    rows, cols = ins[0].shape
    tr = _tile(rows, tr, 8)
    spec = pl.BlockSpec((tr, cols), lambda i: (i, 0))
    return pl.pallas_call(
        body, name=name, grid=(rows // tr,), in_specs=[spec] * len(ins), out_specs=[spec] * len(out_dtypes),
        out_shape=[jax.ShapeDtypeStruct((rows, cols), dt) for dt in out_dtypes],
        compiler_params=_params(dimension_semantics=("parallel",)),
    )(*ins)


def _prefetched(body, *, name, where, grid, in_specs, out_specs, out_shape, operands):
    return pl.pallas_call(
        body, name=name, out_shape=out_shape,
        grid_spec=pltpu.PrefetchScalarGridSpec(num_scalar_prefetch=1, grid=grid, in_specs=in_specs, out_specs=out_specs),
        compiler_params=_params(dimension_semantics=("parallel",) * len(grid)),
    )(where, *operands)


def _cast_into_block(w, where, after, name, tr=256):
    rows, cols = w.shape
    tr = _tile(rows, tr, 16)

    def body(where_ref, w_ref, _, o_ref):
        o_ref[...] = w_ref[...].astype(BF16)

    return _prefetched(
        body, name=name, where=where, grid=(rows // tr,),
        in_specs=[pl.BlockSpec((tr, cols), lambda i, wh: (i, 0)), pl.BlockSpec(memory_space=pl.ANY)],
        out_specs=pl.BlockSpec((None, tr, cols), lambda i, wh: (wh[0], i, 0)),
        out_shape=jax.ShapeDtypeStruct((N_CHIPS, rows, cols), BF16), operands=[w, after])


def _add_pair(grad, theirs, where, name, tr=512):
    nb, half, cols = theirs.shape
    tr = _tile(half, tr, 16)
    nh = half // tr

    def body(where_ref, a_ref, b_ref, o_ref):
        o_ref[...] = (a_ref[...].astype(F32) + b_ref[...].astype(F32)).astype(BF16)

    blk = pl.BlockSpec((None, tr, cols), lambda b, i, wh: (b, i, 0))
    return _prefetched(
        body, name=name, where=where, grid=(nb, nh),
        in_specs=[pl.BlockSpec((None, tr, cols), lambda b, i, wh: (b, wh[1] * nh + i, 0)), blk],
        out_specs=blk, out_shape=jax.ShapeDtypeStruct(theirs.shape, BF16), operands=[grad, theirs])


def _sum_slots(part, slots, where, name, tr=512):
    _, half, cols = part.shape
    tr = _tile(half, tr, 16)
    nh = half // tr

    def body(where_ref, p_ref, s_ref, o_ref):
        o_ref[...] = ((p_ref[...].astype(F32) + s_ref[0].astype(F32)) + s_ref[1].astype(F32)) + s_ref[2].astype(F32)

    return _prefetched(
        body, name=name, where=where, grid=(nh,),
        in_specs=[pl.BlockSpec((None, tr, cols), lambda i, wh: (wh[0], i, 0)),
                  pl.BlockSpec((3, tr, cols), lambda i, wh: (0, i, 0))],
        out_specs=pl.BlockSpec((tr, cols), lambda i, wh: (wh[1] * nh + i, 0)),
        out_shape=jax.ShapeDtypeStruct((2 * half, cols), F32), operands=[part, slots])


def _adamw(w, g, m, v, name):
    bc1 = 1.0 - ADAM_B1 ** ADAM_STEP
    bc2 = 1.0 - ADAM_B2 ** ADAM_STEP

    def body(w_ref, g_ref, m_ref, v_ref, go_ref, d_ref, mo_ref, vo_ref):
        g = g_ref[...]
        m_new = ADAM_B1 * m_ref[...] + (1.0 - ADAM_B1) * g
        v_new = ADAM_B2 * v_ref[...] + (1.0 - ADAM_B2) * (g * g)
        m_hat = m_new / bc1
        v_hat = v_new / bc2
        go_ref[...] = g
        d_ref[...] = -ADAM_LR * (m_hat / (jnp.sqrt(v_hat) + ADAM_EPS) + ADAM_WD * w_ref[...])
        mo_ref[...] = m_new
        vo_ref[...] = v_new

    return _elementwise(body, name=name, ins=[w, g, m, v], out_dtypes=[F32] * 4, tr=128)


ANY = pl.BlockSpec(memory_space=pl.ANY)


def _mesh_pos():
    return lax.axis_index("x"), lax.axis_index("y"), lax.axis_index("c")


def _other_chips(x, y):
    return [(1 - x, y), (x, 1 - y), (1 - x, 1 - y)]


def _remote(src, dst, send_sems, recv_sems, idx, to):
    return pltpu.make_async_remote_copy(src_ref=src, dst_ref=dst, send_sem=send_sems.at[idx], recv_sem=recv_sems.at[idx],
                                        device_id=to, device_id_type=MESH)


HBM = pl.BlockSpec(memory_space=pltpu.HBM)
SEM = pl.BlockSpec(memory_space=pltpu.SEMAPHORE)
EFFECT = pltpu.SideEffectType.DATAFLOW_SIDE_EFFECTING


def _in_hbm(arrays):
    return [pltpu.with_memory_space_constraint(a, pltpu.HBM) for a in arrays]


def _ici_copies(slab, send_sems, recv_sems, x, y, c):
    half = slab.shape[1] // 2
    rows = pl.ds(c * half, half)
    mine = slab.at[2 * x + y, rows]
    out = []
    for j, (cx, cy) in enumerate(_other_chips(x, y)):
        got = slab.at[2 * cx + cy, rows]
        out.append((_remote(mine, mine, send_sems, recv_sems, j, (cx, cy, c)),
                    _remote(got, got, send_sems, recv_sems, j, (cx, cy, c))))
    return out


def _gather_start(slabs, after, name):
    n = len(slabs)

    def body(*refs):
        ins = refs[:n]
        send_sems, recv_sems, token = refs[2 * n + 1:3 * n + 1], refs[3 * n + 1:4 * n + 1], refs[4 * n + 1]
        x, y, c = _mesh_pos()
        for w in range(n):
            for send, _ in _ici_copies(ins[w], send_sems[w], recv_sems[w], x, y, c):
                send.start()
        token[...] = jnp.zeros_like(token)

    outs = pl.pallas_call(
        body, name=name, in_specs=[HBM] * n + [ANY],
        out_specs=[HBM] * n + [SEM] * (2 * n) + [pl.BlockSpec(memory_space=pltpu.VMEM)],
        out_shape=[pltpu.HBM(s.shape, s.dtype) for s in slabs] + [pltpu.SemaphoreType.DMA((3,))] * (2 * n)
        + [jax.ShapeDtypeStruct((8, 128), F32)],
        input_output_aliases={w: w for w in range(n)},
        compiler_params=_params(has_side_effects=EFFECT),
    )(*_in_hbm(slabs), after)
    return (outs[:n], outs[n:2 * n], outs[2 * n:3 * n]), outs[3 * n]


def _gather_wait(slabs, send_sems, recv_sems, after, name):
    n = len(slabs)

    def body(*refs):
        ins = refs[:n]
        sends, recvs = refs[n:2 * n], refs[2 * n:3 * n]
        x, y, c = _mesh_pos()
        for w in range(n):
            for send, recv in _ici_copies(ins[w], sends[w], recvs[w], x, y, c):
                send.wait_send()
                recv.wait_recv()

    return pl.pallas_call(
        body, name=name, in_specs=[HBM] * n + [SEM] * (2 * n) + [ANY], out_specs=[HBM] * n,
        out_shape=[pltpu.HBM(s.shape, s.dtype) for s in slabs],
        input_output_aliases={w: w for w in range(n)},
        compiler_params=_params(has_side_effects=EFFECT),
    )(*slabs, *send_sems, *recv_sems, after)


def _phase_start(arrays, copies, n_copies, after, name):
    n = len(arrays)

    def body(*refs):
        send_sems, recv_sems, token = refs[2 * n + 1], refs[2 * n + 2], refs[2 * n + 3]
        for send, _ in copies(refs[:n], send_sems, recv_sems, *_mesh_pos()):
            send.start()
        token[...] = jnp.zeros_like(token)

    outs = pl.pallas_call(
        body, name=name, in_specs=[HBM] * n + [ANY],
        out_specs=[HBM] * n + [SEM, SEM, pl.BlockSpec(memory_space=pltpu.VMEM)],
        out_shape=[pltpu.HBM(a.shape, a.dtype) for a in arrays]
        + [pltpu.SemaphoreType.DMA((n_copies,))] * 2 + [jax.ShapeDtypeStruct((8, 128), F32)],
        input_output_aliases={k: k for k in range(n)},
        compiler_params=_params(has_side_effects=EFFECT),
    )(*_in_hbm(arrays), after)
    return outs[:n], outs[n], outs[n + 1], outs[n + 2]


def _phase_wait(arrays, send_sems, recv_sems, copies, after, name):
    n = len(arrays)

    def body(*refs):
        for send, recv in copies(refs[:n], refs[n], refs[n + 1], *_mesh_pos()):
            send.wait_send()
            recv.wait_recv()

    return pl.pallas_call(
        body, name=name, in_specs=[HBM] * n + [SEM, SEM, ANY], out_specs=[HBM] * n,
        out_shape=[pltpu.HBM(a.shape, a.dtype) for a in arrays],
        input_output_aliases={k: k for k in range(n)},
        compiler_params=_params(has_side_effects=EFFECT),
    )(*arrays, send_sems, recv_sems, after)


def _forward_copies(slabs, send_sems, recv_sems, x, y, c):
    sibling = (x, y, 1 - c)
    out = []
    for w, slab in enumerate(slabs):
        half = slab.shape[1] // 2
        for j, (cx, cy) in enumerate(_other_chips(x, y)):
            mine = slab.at[2 * cx + cy, pl.ds(c * half, half)]
            theirs = slab.at[2 * cx + cy, pl.ds((1 - c) * half, half)]
            out.append((_remote(mine, mine, send_sems, recv_sems, 3 * w + j, sibling),
                        _remote(theirs, theirs, send_sems, recv_sems, 3 * w + j, sibling)))
    return out


def _exchange_copies(arrays, send_sems, recv_sems, x, y, c):
    n = len(arrays) // 2
    sibling = (x, y, 1 - c)
    out = []
    for w in range(n):
        grad, land = arrays[w], arrays[n + w]
        half = grad.shape[1] // 2
        out.append((_remote(grad.at[:, pl.ds((1 - c) * half, half), :], land, send_sems, recv_sems, w, sibling),
                    _remote(land, land, send_sems, recv_sems, w, sibling)))
    return out


def _gather_taps(taps):
    def body(in_ref, out_ref, send_sems, recv_sems, local_sem):
        x, y, c = _mesh_pos()
        me = 2 * x + y
        chips = _other_chips(x, y)
        lc = pltpu.make_async_copy(in_ref, out_ref.at[me], local_sem)
        lc.start()
        sends = [_remote(in_ref, out_ref.at[me], send_sems, recv_sems, j, (cx, cy, c)) for j, (cx, cy) in enumerate(chips)]
        for cp in sends:
            cp.start()
        for j, (cx, cy) in enumerate(chips):
            got = out_ref.at[2 * cx + cy]
            _remote(got, got, send_sems, recv_sems, j, (cx, cy, c)).wait_recv()
        for cp in sends:
            cp.wait_send()
        lc.wait()

    vm = pl.BlockSpec(memory_space=pltpu.VMEM)
    return pl.pallas_call(
        body, name="gather_taps", in_specs=[vm], out_specs=vm,
        out_shape=jax.ShapeDtypeStruct((N_CHIPS,) + taps.shape, taps.dtype),
        scratch_shapes=[pltpu.SemaphoreType.DMA((3,)), pltpu.SemaphoreType.DMA((3,)), pltpu.SemaphoreType.DMA],
        compiler_params=_params(has_side_effects=True),
    )(taps)


def _exchange_halves(grads, name):
    n = len(grads)

    def body(*refs):
        ins, their_refs = refs[:n], refs[n:2 * n]
        send_sems, recv_sems = refs[2 * n:]
        x, y, c = _mesh_pos()
        sibling = (x, y, 1 - c)
        sends = []
        for w in range(n):
            half = ins[w].shape[1] // 2
            cp = _remote(ins[w].at[:, pl.ds((1 - c) * half, half), :], their_refs[w], send_sems, recv_sems, w, sibling)
            cp.start()
            sends.append(cp)
        for w in range(n):
            _remote(their_refs[w], their_refs[w], send_sems, recv_sems, w, sibling).wait_recv()
        for cp in sends:
            cp.wait_send()

    halves = [jax.ShapeDtypeStruct((g.shape[0], g.shape[1] // 2, g.shape[2]), g.dtype) for g in grads]
    return pl.pallas_call(
        body, name=name, in_specs=[ANY] * n, out_specs=[ANY] * n, out_shape=halves,
        scratch_shapes=[pltpu.SemaphoreType.DMA((n,)), pltpu.SemaphoreType.DMA((n,))],
        compiler_params=_params(has_side_effects=True),
    )(*grads)


def _scatter_copies(part, slots, send_sems, recv_sems, x, y, c):
    out = []
    for j, (cx, cy) in enumerate(_other_chips(x, y)):
        out.append((_remote(part.at[2 * cx + cy], slots.at[j], send_sems, recv_sems, j, (cx, cy, c)),
                    _remote(slots.at[j], slots.at[j], send_sems, recv_sems, j, (cx, cy, c))))
    return out


def _scatter_start(parts, after, name):
    n = len(parts)
    slots = [lax.empty((3,) + p.shape[1:], p.dtype) for p in parts]

    def body(*refs):
        ins, lands = refs[:n], refs[n:2 * n]
        send_sems, recv_sems, token = refs[4 * n + 1:5 * n + 1], refs[5 * n + 1:6 * n + 1], refs[6 * n + 1]
        x, y, c = _mesh_pos()
        for w in range(n):
            for send, _ in _scatter_copies(ins[w], lands[w], send_sems[w], recv_sems[w], x, y, c):
                send.start()
        token[...] = jnp.zeros_like(token)

    outs = pl.pallas_call(
        body, name=name, in_specs=[HBM] * (2 * n) + [ANY],
        out_specs=[HBM] * (2 * n) + [SEM] * (2 * n) + [pl.BlockSpec(memory_space=pltpu.VMEM)],
        out_shape=[pltpu.HBM(a.shape, a.dtype) for a in list(parts) + slots] + [pltpu.SemaphoreType.DMA((3,))] * (2 * n)
        + [jax.ShapeDtypeStruct((8, 128), F32)],
        input_output_aliases={k: k for k in range(2 * n)},
        compiler_params=_params(has_side_effects=EFFECT),
    )(*_in_hbm(list(parts) + slots), after)
    return (outs[:n], outs[n:2 * n], outs[2 * n:3 * n], outs[3 * n:4 * n]), outs[4 * n]


def _scatter_wait(parts, slots, send_sems, recv_sems, after, name):
    n = len(parts)

    def body(*refs):
        ins, lands = refs[:n], refs[n:2 * n]
        sends, recvs = refs[2 * n:3 * n], refs[3 * n:4 * n]
        x, y, c = _mesh_pos()
        for w in range(n):
            for send, recv in _scatter_copies(ins[w], lands[w], sends[w], recvs[w], x, y, c):
                send.wait_send()
                recv.wait_recv()

    outs = pl.pallas_call(
        body, name=name, in_specs=[HBM] * (2 * n) + [SEM] * (2 * n) + [ANY], out_specs=[HBM] * (2 * n),
        out_shape=[pltpu.HBM(a.shape, a.dtype) for a in list(parts) + list(slots)],
        input_output_aliases={k: k for k in range(2 * n)},
        compiler_params=_params(has_side_effects=EFFECT),
    )(*parts, *slots, *send_sems, *recv_sems, after)
    return outs[:n], outs[n:]


def _share_halves(halves, name):
    n = len(halves)

    def body(*refs):
        outs = refs[n:2 * n]
        send_sems, recv_sems = refs[2 * n:]
        x, y, c = _mesh_pos()
        sibling = (x, y, 1 - c)
        sends = []
        for w in range(n):
            h = outs[w].shape[0] // 2
            mine = outs[w].at[pl.ds(c * h, h)]
            cp = _remote(mine, mine, send_sems, recv_sems, w, sibling)
            cp.start()
            sends.append(cp)
        for w in range(n):
            h = outs[w].shape[0] // 2
            got = outs[w].at[pl.ds((1 - c) * h, h)]
            _remote(got, got, send_sems, recv_sems, w, sibling).wait_recv()
        for cp in sends:
            cp.wait_send()

    return pl.pallas_call(
        body, name=name, in_specs=[ANY] * n, out_specs=[ANY] * n,
        out_shape=[jax.ShapeDtypeStruct(h.shape, h.dtype) for h in halves],
        input_output_aliases={w: w for w in range(n)},
        scratch_shapes=[pltpu.SemaphoreType.DMA((n,)), pltpu.SemaphoreType.DMA((n,))],
        compiler_params=_params(has_side_effects=True),
    )(*halves)


def _all_reduce_small(buf):
    rows, width = buf.shape

    def body(in_ref, out_ref, gathered, send_sems, recv_sems):
        x, y, c = _mesh_pos()
        me = 4 * x + 2 * y + c
        gathered[me] = in_ref[...]
        flips = [(dx, dy, dz) for dx in (0, 1) for dy in (0, 1) for dz in (0, 1)][1:]
        peers = [((1 - x) if dx else x, (1 - y) if dy else y, (1 - c) if dz else c) for dx, dy, dz in flips]
        sends = [_remote(in_ref, gathered.at[me], send_sems, recv_sems, k, peer) for k, peer in enumerate(peers)]
        for cp in sends:
            cp.start()
        for k, (px, py, pc) in enumerate(peers):
            got = gathered.at[4 * px + 2 * py + pc]
            _remote(got, got, send_sems, recv_sems, k, (px, py, pc)).wait_recv()
        for cp in sends:
            cp.wait_send()
        total = gathered[0]
        for d in range(1, 8):
            total = total + gathered[d]
        out_ref[...] = total

    vm = pl.BlockSpec(memory_space=pltpu.VMEM)
    return pl.pallas_call(
        body, name="all_reduce_small", in_specs=[vm], out_specs=vm, out_shape=jax.ShapeDtypeStruct(buf.shape, F32),
        scratch_shapes=[pltpu.VMEM((8, rows, width), F32), pltpu.SemaphoreType.DMA((7,)), pltpu.SemaphoreType.DMA((7,))],
        compiler_params=_params(has_side_effects=True),
    )(buf)


PACK_W = 2048
PACK_ROWS = 8


def _pack(vectors):
    flat = jnp.concatenate([v.reshape(-1) for v in vectors])
    unit = PACK_W * PACK_ROWS
    total = -(-flat.shape[0] // unit) * unit
    return jnp.pad(flat, (0, total - flat.shape[0])).reshape(total // PACK_W, PACK_W)


def _unpack(buf, shapes):
    flat = buf.reshape(-1)
    out, pos = [], 0
    for shp in shapes:
        size = math.prod(shp)
        out.append(flat[pos:pos + size].reshape(shp))
        pos += size
    return out


def _pad_rows(a, rows):
    return jnp.pad(a, ((0, rows - a.shape[0]), (0, 0)))


def kernel(x, g_pre_mix, w_in, b_in, w_dw, b_dw, g_conv_ln, b_conv_ln, w_sb_out, w_conv_out, w_o, g_post_mix, g_pre_mlp, w_up, w_down, g_post_mlp, loss_target, m_g_pre_mix, m_w_in, m_b_in, m_w_dw, m_b_dw, m_g_conv_ln, m_b_conv_ln, m_w_sb_out, m_w_conv_out, m_w_o, m_g_post_mix, m_g_pre_mlp, m_w_up, m_w_down, m_g_post_mlp, v_g_pre_mix, v_w_in, v_b_in, v_w_dw, v_b_dw, v_g_conv_ln, v_b_conv_ln, v_w_sb_out, v_w_conv_out, v_w_o, v_g_post_mix, v_g_pre_mlp, v_w_up, v_w_down, v_g_post_mlp):
    xs = x[0]
    tgt = loss_target[0]
    s, d = xs.shape
    d_sb = w_sb_out.shape[1]
    dc = w_conv_out.shape[1]
    d_in = w_in.shape[2] * N_CHIPS
    n_heads = d_sb // HEAD_DIM
    off_a = 3 * d_sb
    off_sb = off_a + 2 * dc
    off_cv = off_sb + d
    gw = dc
    assert off_a % dc == 0 and off_sb % gw == 0 and d % gw == 0 and d_in == off_cv + d
    chip = 2 * lax.axis_index("x") + lax.axis_index("y")

    big = dict(w_in=w_in[0], w_sb_out=w_sb_out[0], w_conv_out=w_conv_out[0], w_o=w_o[0], w_up=w_up[0], w_down=w_down[0])
    names = list(big)
    col_sharded = {"w_in", "w_sb_out", "w_conv_out", "w_up"}
    where = jnp.stack([chip, lax.axis_index("c")]).astype(jnp.int32)
    cs = w_dw.shape[2]
    taps4 = _gather_taps(_pad_rows(w_dw[0], CONV_HALO))
    first, token = _gather_start([_cast_into_block(big["w_in"], where, taps4, "cast_w_in")], taps4, "gather_start_in")
    rest, rest_token = _gather_start([_cast_into_block(big[n], where, token, "cast_" + n) for n in names[1:]], token,
                            "gather_start_rest")
    in_flight = dict(zip(names, zip(*[a + b for a, b in zip(first, rest)])))
    gathered = {}

    def land(group, after, tag):
        thru, sends, recvs = zip(*[in_flight[n] for n in group])
        landed = _gather_wait(thru, sends, recvs, after, "gather_wait_" + tag)
        return _phase_start(landed, _forward_copies, 3 * len(group), where, "forward_start_" + tag)

    def ready(group, started, after, tag):
        slabs, sends, recvs, _ = started
        for n, g4 in zip(group, _phase_wait(slabs, sends, recvs, _forward_copies, after, "forward_wait_" + tag)):
            gathered[n] = g4 if n in col_sharded else g4.reshape(g4.shape[0] * g4.shape[1], g4.shape[2])

    taps = jnp.transpose(taps4, (1, 0, 2)).reshape(CONV_HALO, N_CHIPS * cs)
    group_mix = ["w_sb_out", "w_conv_out", "w_o"]

    h = _rms_fwd(xs, g_pre_mix)
    started = land(["w_in"], rest_token, "in")
    ready(["w_in"], started, h, "in")
    proj = _mm(h, gathered["w_in"], name="mm_proj", b_groups=N_CHIPS, bias=b_in, tn=768)
    mix_fwd = land(group_mix, proj, "mix")
    attn = _attn_fwd(proj, mix_fwd[3], n_heads, d_sb)
    u0 = _glu(proj, off_a, dc)
    u1, u3 = _conv_fwd(u0, taps, b_dw, g_conv_ln, b_conv_ln)
    ready(group_mix, mix_fwd, u3, "mix")
    o_sb = _mm(attn, gathered["w_sb_out"], name="mm_o_sb", b_groups=N_CHIPS)
    up_fwd = land(["w_up"], o_sb, "up")
    o_cv = _mm(u3, gathered["w_conv_out"], name="mm_o_cv", b_groups=N_CHIPS, after=up_fwd[3])
    merged = _merge_fwd(proj, o_sb, o_cv, off_sb, off_cv, gw)
    y = _mm(merged, gathered["w_o"], name="mm_y")
    x2, h2 = _resid_rms(xs, y, g_post_mix, g_pre_mlp)
    ready(["w_up"], up_fwd, h2, "up")
    down_fwd = land(["w_down"], h2, "down")
    up, f = _mm(h2, gathered["w_up"], name="mm_up", b_groups=N_CHIPS, out_dtypes=(F32, BF16), after=down_fwd[3],
                epilogue=lambda acc: (acc, jnp.square(jnp.maximum(acc, 0.0))))
    ready(["w_down"], down_fwd, f, "down")
    dn = _mm(f, gathered["w_down"], name="mm_down")
    dx3, d_dn, dg_post_mlp, loss_part = _final(x2, dn, g_post_mlp, tgt)

    moments = dict(w_in=(m_w_in, v_w_in), w_sb_out=(m_w_sb_out, v_w_sb_out), w_conv_out=(m_w_conv_out, v_w_conv_out),
                   w_o=(m_w_o, v_w_o), w_up=(m_w_up, v_w_up), w_down=(m_w_down, v_w_down))
    out_g, out_d, out_m, out_v = {}, {}, {}, {}
    grads = {}

    def blocks_of(group):
        out = []
        for n in group:
            g = grads[n]
            out.append(g if g.ndim == 3 else g.reshape(N_CHIPS, g.shape[0] // N_CHIPS, g.shape[1]))
        return out

    def reduce_start(group, tag):
        blocks = blocks_of(group)
        theirs = _exchange_halves(blocks, "exchange_halves_" + tag)
        pair_sums = [_add_pair(g, t, where, "pair_sum_" + n) for n, g, t in zip(group, blocks, theirs)]
        return _scatter_start(pair_sums, where, "scatter_start_" + tag)

    def exchange_begin(group, tag):
        blocks = blocks_of(group)
        lands = [lax.empty((b.shape[0], b.shape[1] // 2, b.shape[2]), b.dtype) for b in blocks]
        return _phase_start(blocks + lands, _exchange_copies, len(group), where, "exchange_start_" + tag)

    def scatter_begin(group, tag, exchange, after):
        arrays, sends, recvs, _ = exchange
        arrays = _phase_wait(arrays, sends, recvs, _exchange_copies, after, "exchange_wait_" + tag)
        n = len(group)
        pair_sums = [_add_pair(g, t, where, "pair_sum_" + m) for m, g, t in zip(group, arrays[:n], arrays[n:])]
        return _scatter_start(pair_sums, where, "scatter_start_" + tag)

    def reduce_finish(group, tag, started, after):
        parts, slots = _scatter_wait(*started, after, "scatter_wait_" + tag)
        halves = [_sum_slots(p, t, where, "chip_sum_" + n) for n, p, t in zip(group, parts, slots)]
        for n, red in zip(group, _share_halves(halves, "share_halves_" + tag)):
            mm_, vv_ = moments[n]
            res = _adamw(big[n], red, mm_[0], vv_[0], "adamw_" + n)
            out_g[n], out_d[n], out_m[n], out_v[n] = [r[None] for r in res]
        return res[1]

    group_mlp, group_mix, group_in = ["w_down", "w_up"], ["w_o", "w_sb_out", "w_conv_out"], ["w_in"]
    grads["w_down"] = _mm(f, d_dn, name="mm_dw_down", ta=True, out_dtypes=(BF16,))
    dup = _mm(d_dn, gathered["w_down"], name="mm_df", tb=True, extra=[up], out_dtypes=(BF16,),
              epilogue=lambda acc, upv: (acc * (2.0 * jnp.maximum(upv, 0.0)),))
    grads["w_up"] = _mm(h2, dup, name="mm_dw_up", ta=True, out_groups=N_CHIPS, out_dtypes=(BF16,))
    mlp_exchange = exchange_begin(group_mlp, "mlp")
    dh2 = _mm(dup, gathered["w_up"], name="mm_dh2", tb=True, b_groups=N_CHIPS, after=mlp_exchange[3])
    dx2, dy, dg_pre_mlp, dg_post_mix = _rms_bwd2(dx3, dh2, x2, y, g_post_mix, g_pre_mlp)
    mlp_started, mlp_token = scatter_begin(group_mlp, "mlp", mlp_exchange, dx2)
    grads["w_o"] = _mm(merged, dy, name="mm_dw_o", ta=True, out_dtypes=(BF16,), after=mlp_token)
    dmerged = _mm(dy, gathered["w_o"], name="mm_dmerged", tb=True)
    do_sb, do_cv, dgate_sb, dgate_cv, s_gate_sb, s_gate_cv = _merge_bwd(proj, o_sb, o_cv, dmerged, off_sb, off_cv, gw)
    grads["w_sb_out"] = _mm(attn, do_sb, name="mm_dw_sb", ta=True, out_groups=N_CHIPS, out_dtypes=(BF16,))
    dattn = _mm(do_sb, gathered["w_sb_out"], name="mm_dattn", tb=True, b_groups=N_CHIPS)
    grads["w_conv_out"] = _mm(u3, do_cv, name="mm_dw_cv", ta=True, out_groups=N_CHIPS, out_dtypes=(BF16,))
    mix_exchange = exchange_begin(group_mix, "mix")
    du3 = _mm(do_cv, gathered["w_conv_out"], name="mm_du3", tb=True, b_groups=N_CHIPS, after=mix_exchange[3])
    mix_started, mix_token = scatter_begin(group_mix, "mix", mix_exchange, du3)
    dq, dk, dv, s_q, s_k, s_v = _attn_bwd(proj, dattn, mix_token, n_heads, d_sb)
    du1, dg_conv_ln, db_conv_ln, db_dw = _conv_bwd_norm(u1, du3, g_conv_ln, b_conv_ln)
    dglu_a, dglu_b, s_a, s_b, dtaps = _conv_bwd_taps(du1, u0, proj, taps, off_a)
    dproj = jnp.concatenate([dq, dk, dv, dglu_a, dglu_b, dgate_sb, dgate_cv], axis=1)
    db_in = jnp.concatenate([s_q, s_k, s_v, s_a, s_b, s_gate_sb, s_gate_cv], axis=1)
    dh = _mm(dproj, gathered["w_in"], name="mm_dh", tb=True, b_groups=N_CHIPS, tk=2304)
    grad_x, dg_pre_mix = _rms_bwd1(dx2, dh, xs, g_pre_mix)

    small = ["g_pre_mix", "b_in", "b_dw", "g_conv_ln", "b_conv_ln", "g_post_mix", "g_pre_mlp", "g_post_mlp"]
    small_w = dict(g_pre_mix=g_pre_mix, b_in=b_in, b_dw=b_dw, g_conv_ln=g_conv_ln, b_conv_ln=b_conv_ln,
                   g_post_mix=g_post_mix, g_pre_mlp=g_pre_mlp, g_post_mlp=g_post_mlp)
    small_m = dict(g_pre_mix=m_g_pre_mix, b_in=m_b_in, b_dw=m_b_dw, g_conv_ln=m_g_conv_ln, b_conv_ln=m_b_conv_ln,
                   g_post_mix=m_g_post_mix, g_pre_mlp=m_g_pre_mlp, g_post_mlp=m_g_post_mlp)
    small_v = dict(g_pre_mix=v_g_pre_mix, b_in=v_b_in, b_dw=v_b_dw, g_conv_ln=v_g_conv_ln, b_conv_ln=v_b_conv_ln,
                   g_post_mix=v_g_post_mix, g_pre_mlp=v_g_pre_mlp, g_post_mlp=v_g_post_mlp)
    small_g = dict(g_pre_mix=dg_pre_mix, b_in=db_in, b_dw=db_dw, g_conv_ln=dg_conv_ln, b_conv_ln=db_conv_ln,
                   g_post_mix=dg_post_mix, g_pre_mlp=dg_pre_mlp, g_post_mlp=dg_post_mlp)
    shapes = [small_w[n].shape for n in small]
    tail_shapes = [loss_part.shape, dtaps.shape]
    summed = _all_reduce_small(_pack([small_g[n] for n in small] + [loss_part, dtaps]))
    grads["w_in"] = _mm(h, dproj, name="mm_dw_in", ta=True, out_groups=N_CHIPS, out_dtypes=(BF16,), tn=768, after=summed)
    in_started, in_token = reduce_start(group_in, "in")
    zeros_tail = [jnp.zeros(shp, F32) for shp in tail_shapes]
    res = _adamw(_pack([small_w[n] for n in small] + zeros_tail), summed,
                 _pack([small_m[n] for n in small] + zeros_tail), _pack([small_v[n] for n in small] + zeros_tail),
                 "adamw_small")
    unpacked = [_unpack(r, shapes + tail_shapes) for r in res]
    for i, n in enumerate(small):
        out_g[n], out_d[n], out_m[n], out_v[n] = [u[i] for u in unpacked]
    loss = unpacked[0][len(small)][0, 0]
    taps_grad = lax.dynamic_slice(unpacked[0][len(small) + 1], (0, chip * cs), (CONV_HALO, cs))
    res = _adamw(_pad_rows(w_dw[0], CONV_HALO), taps_grad, _pad_rows(m_w_dw[0], CONV_HALO), _pad_rows(v_w_dw[0], CONV_HALO),
                 "adamw_taps")
    out_g["w_dw"], out_d["w_dw"], out_m["w_dw"], out_v["w_dw"] = [r[:CONV_WIDTH][None] for r in res]
    done = reduce_finish(group_mlp, "mlp", mlp_started, in_token)
    done = reduce_finish(group_mix, "mix", mix_started, done)
    reduce_finish(group_in, "in", in_started, done)

    order = ["g_pre_mix", "w_in", "b_in", "w_dw", "b_dw", "g_conv_ln", "b_conv_ln", "w_sb_out", "w_conv_out", "w_o",
             "g_post_mix", "g_pre_mlp", "w_up", "w_down", "g_post_mlp"]
    return (loss, grad_x[None], *[out_g[n] for n in order], *[out_d[n] for n in order],
            *[out_m[n] for n in order], *[out_v[n] for n in order])
```

```python
import functools
import math

import jax
import jax.numpy as jnp
from jax import lax
from jax.experimental import pallas as pl
from jax.experimental.pallas import tpu as pltpu

F32 = jnp.float32
BF16 = jnp.bfloat16
MESH = pl.DeviceIdType.MESH

HEAD_DIM = 128
CONV_WIDTH = 31
CONV_HALO = 32
EPS = 1e-6
ADAM_LR = 0.001
ADAM_B1 = 0.9
ADAM_B2 = 0.999
ADAM_EPS = 1e-08
ADAM_WD = 0.01
ADAM_STEP = 10
N_CHIPS = 4
VMEM_LIMIT = 48 * 1024 * 1024


def _params(**kw):
    return pltpu.CompilerParams(vmem_limit_bytes=VMEM_LIMIT, **kw)


def _tile(n, want, align=128):
    if n <= want:
        return n
    for t in range(want - want % align, 0, -align):
        if n % t == 0:
            return t
    raise ValueError((n, want, align))


def _mm(a, b, *, name, ta=False, tb=False, b_groups=1, out_groups=1, bias=None, extra=None, epilogue=None,
        out_dtypes=(F32,), tm=1024, tn=1024, tk=2048, after=None):
    if ta:
        K, M = a.shape
    else:
        M, K = a.shape
    if b_groups == 1:
        br, bc = b.shape
    else:
        _, br, bcg = b.shape
        bc = bcg * b_groups
    if tb:
        N, Kb = br, bc
    else:
        Kb, N = br, bc
    assert Kb == K, (name, a.shape, b.shape)
    tm, tn, tk = _tile(M, tm), _tile(N, tn), _tile(K, tk)
    if b_groups > 1:
        if tb:
            tk = _tile(K // b_groups, tk)
        else:
            tn = _tile(N // b_groups, tn)
    if out_groups > 1:
        tn = _tile(N // out_groups, tn)
        if b_groups > 1 and not tb:
            tn = _tile(N // b_groups, tn)
    nm, nn, nk = M // tm, N // tn, K // tk
    extra = tuple(extra or ())
    n_extra = len(extra)
    has_bias = bias is not None
    n_out = len(out_dtypes)

    a_spec = pl.BlockSpec((tk, tm), lambda i, j, k: (k, i)) if ta else pl.BlockSpec((tm, tk), lambda i, j, k: (i, k))
    if b_groups == 1:
        b_spec = pl.BlockSpec((tn, tk), lambda i, j, k: (j, k)) if tb else pl.BlockSpec((tk, tn), lambda i, j, k: (k, j))
    elif tb:
        kpg = (K // b_groups) // tk
        b_spec = pl.BlockSpec((None, tn, tk), lambda i, j, k: (k // kpg, j, k % kpg))
    else:
        npg = (N // b_groups) // tn
        b_spec = pl.BlockSpec((None, tk, tn), lambda i, j, k: (j // npg, k, j % npg))
    in_specs = [a_spec, b_spec]
    operands = [a, b]
    if has_bias:
        in_specs.append(pl.BlockSpec((1, tn), lambda i, j, k: (0, j)))
        operands.append(bias)
    for e in extra:
        in_specs.append(pl.BlockSpec((tm, tn), lambda i, j, k: (i, j)))
        operands.append(e)
    n_after = 0 if after is None else 1
    if after is not None:
        in_specs.append(pl.BlockSpec(memory_space=pl.ANY))
        operands.append(after)
    if out_groups == 1:
        o_spec = pl.BlockSpec((tm, tn), lambda i, j, k: (i, j))
        o_shape = (M, N)
    else:
        opg = (N // out_groups) // tn
        o_spec = pl.BlockSpec((None, tm, tn), lambda i, j, k: (j // opg, i, j % opg))
        o_shape = (out_groups, M, N // out_groups)
    dims = (((0 if ta else 1,), (1 if tb else 0,)), ((), ()))

    def body(*refs):
        a_ref, b_ref = refs[0], refs[1]
        pos = 2
        bias_ref = None
        if has_bias:
            bias_ref = refs[pos]
            pos += 1
        extra_refs = refs[pos:pos + n_extra]
        pos += n_extra + n_after
        out_refs = refs[pos:pos + n_out]
        pos += n_out
        acc_ref = refs[pos] if nk > 1 else None

        part = lax.dot_general(a_ref[...].astype(BF16), b_ref[...].astype(BF16), dims, preferred_element_type=F32)

        def finish(acc):
            if has_bias:
                acc = acc + bias_ref[...]
            outs = epilogue(acc, *[e[...] for e in extra_refs]) if epilogue is not None else (acc,)
            for o_ref, o in zip(out_refs, outs):
                o_ref[...] = o.astype(o_ref.dtype)

        if nk == 1:
            finish(part)
        else:
            k = pl.program_id(2)

            @pl.when(k == 0)
            def _():
                acc_ref[...] = part

            @pl.when(k > 0)
            def _():
                acc_ref[...] += part

            @pl.when(k == nk - 1)
            def _():
                finish(acc_ref[...])

    outs = pl.pallas_call(
        body,
        name=name,
        grid=(nm, nn, nk),
        in_specs=in_specs,
        out_specs=[o_spec] * n_out,
        out_shape=[jax.ShapeDtypeStruct(o_shape, dt) for dt in out_dtypes],
        scratch_shapes=[pltpu.VMEM((tm, tn), F32)] if nk > 1 else [],
        compiler_params=_params(dimension_semantics=("parallel", "parallel", "arbitrary")),
    )(*operands)
    return outs[0] if n_out == 1 else outs


def _rms_stats(x):
    return lax.rsqrt(jnp.mean(x * x, axis=-1, keepdims=True) + EPS)


def _rms_bwd(x, r, g, dy):
    gy = dy * g
    return r * gy - x * (r * r * r) * jnp.mean(x * gy, axis=-1, keepdims=True)


def _sigmoid(x):
    return 1.0 / (1.0 + jnp.exp(-x))


def _row_call(body, *, name, rows, tr, ins, outs, acc_outs=()):
    n = rows // tr
    in_specs = []
    for arr, kind in ins:
        if kind == "row":
            in_specs.append(pl.BlockSpec((tr, arr.shape[1]), lambda i: (i, 0)))
        else:
            in_specs.append(pl.BlockSpec(arr.shape, lambda i: (0, 0)))
    out_specs = [pl.BlockSpec((tr, w), lambda i: (i, 0)) for w, _ in outs]
    out_shape = [jax.ShapeDtypeStruct((rows, w), dt) for w, dt in outs]
    out_specs += [pl.BlockSpec((1, w), lambda i: (0, 0)) for w in acc_outs]
    out_shape += [jax.ShapeDtypeStruct((1, w), F32) for w in acc_outs]
    return pl.pallas_call(
        body, name=name, grid=(n,), in_specs=in_specs, out_specs=out_specs, out_shape=out_shape,
        compiler_params=_params(dimension_semantics=("arbitrary",)),
    )(*[a for a, _ in ins])


def _accumulate(ref, val):
    @pl.when(pl.program_id(0) == 0)
    def _():
        ref[...] = val

    @pl.when(pl.program_id(0) > 0)
    def _():
        ref[...] += val


def _rms_fwd(x, g, tr=256):
    def body(x_ref, g_ref, h_ref):
        xv = x_ref[...]
        h_ref[...] = (xv * _rms_stats(xv) * g_ref[...]).astype(BF16)

    (h,) = _row_call(body, name="rms_fwd", rows=x.shape[0], tr=tr, ins=[(x, "row"), (g, "vec")], outs=[(x.shape[1], BF16)])
    return h


def _resid_rms(x, y, g2, g3, tr=256):
    def body(x_ref, y_ref, g2_ref, g3_ref, x2_ref, h2_ref):
        yv = y_ref[...]
        x2 = x_ref[...] + yv * _rms_stats(yv) * g2_ref[...]
        x2_ref[...] = x2
        h2_ref[...] = (x2 * _rms_stats(x2) * g3_ref[...]).astype(BF16)

    d = x.shape[1]
    return _row_call(body, name="resid_rms", rows=x.shape[0], tr=tr,
                     ins=[(x, "row"), (y, "row"), (g2, "vec"), (g3, "vec")], outs=[(d, F32), (d, BF16)])


def _final(x2, dn, g4, tgt, tr=256):
    d = x2.shape[1]

    def body(x2_ref, dn_ref, g4_ref, t_ref, dx3_ref, ddn_ref, dg4_ref, loss_ref):
        dn_v = dn_ref[...]
        r = _rms_stats(dn_v)
        g = g4_ref[...]
        e = x2_ref[...] + dn_v * r * g - t_ref[...]
        dx3 = e * (1.0 / d)
        dx3_ref[...] = dx3
        ddn_ref[...] = _rms_bwd(dn_v, r, g, dx3).astype(BF16)
        _accumulate(dg4_ref, jnp.sum(dx3 * dn_v * r, axis=0, keepdims=True))
        part = 0.5 * jnp.sum(jnp.mean(e * e, axis=-1, keepdims=True), axis=0, keepdims=True)
        _accumulate(loss_ref, jnp.broadcast_to(part, loss_ref.shape))

    return _row_call(body, name="final", rows=x2.shape[0], tr=tr,
                     ins=[(x2, "row"), (dn, "row"), (g4, "vec"), (tgt, "row")],
                     outs=[(d, F32), (d, BF16)], acc_outs=(d, 128))


def _rms_bwd2(dx3, dh2, x2, y, g2, g3, tr=256):
    d = x2.shape[1]

    def body(dx3_ref, dh2_ref, x2_ref, y_ref, g2_ref, g3_ref, dx2_ref, dy_ref, dg3_ref, dg2_ref):
        x2v, dh2v, yv = x2_ref[...], dh2_ref[...], y_ref[...]
        r3 = _rms_stats(x2v)
        dx2 = dx3_ref[...] + _rms_bwd(x2v, r3, g3_ref[...], dh2v)
        dx2_ref[...] = dx2
        _accumulate(dg3_ref, jnp.sum(dh2v * x2v * r3, axis=0, keepdims=True))
        r2 = _rms_stats(yv)
        dy_ref[...] = _rms_bwd(yv, r2, g2_ref[...], dx2).astype(BF16)
        _accumulate(dg2_ref, jnp.sum(dx2 * yv * r2, axis=0, keepdims=True))

    return _row_call(body, name="rms_bwd2", rows=x2.shape[0], tr=tr,
                     ins=[(dx3, "row"), (dh2, "row"), (x2, "row"), (y, "row"), (g2, "vec"), (g3, "vec")],
                     outs=[(d, F32), (d, BF16)], acc_outs=(d, d))


def _rms_bwd1(dx2, dh, x, g1, tr=256):
    d = x.shape[1]

    def body(dx2_ref, dh_ref, x_ref, g1_ref, gx_ref, dg1_ref):
        xv, dhv = x_ref[...], dh_ref[...]
        r = _rms_stats(xv)
        gx_ref[...] = dx2_ref[...] + _rms_bwd(xv, r, g1_ref[...], dhv)
        _accumulate(dg1_ref, jnp.sum(dhv * xv * r, axis=0, keepdims=True))

    return _row_call(body, name="rms_bwd1", rows=x.shape[0], tr=tr,
                     ins=[(dx2, "row"), (dh, "row"), (x, "row"), (g1, "vec")], outs=[(d, F32)], acc_outs=(d,))


def _merge_fwd(proj, o_sb, o_cv, off_sb, off_cv, gw, tr=256):
    s, d = o_sb.shape
    nj = d // gw
    b_sb, b_cv = off_sb // gw, off_cv // gw

    def body(gs_ref, gc_ref, osb_ref, ocv_ref, m_ref):
        m_ref[...] = (_sigmoid(gs_ref[...]) * osb_ref[...] + _sigmoid(gc_ref[...]) * ocv_ref[...]).astype(BF16)

    blk = lambda i, j: (i, j)
    return pl.pallas_call(
        body, name="merge_fwd", grid=(s // tr, nj),
        in_specs=[pl.BlockSpec((tr, gw), lambda i, j: (i, b_sb + j)), pl.BlockSpec((tr, gw), lambda i, j: (i, b_cv + j)),
                  pl.BlockSpec((tr, gw), blk), pl.BlockSpec((tr, gw), blk)],
        out_specs=pl.BlockSpec((tr, gw), blk), out_shape=jax.ShapeDtypeStruct((s, d), BF16),
        compiler_params=_params(dimension_semantics=("parallel", "parallel")),
    )(proj, proj, o_sb, o_cv)


def _merge_bwd(proj, o_sb, o_cv, dmerged, off_sb, off_cv, gw, tr=256):
    s, d = o_sb.shape
    nj = d // gw
    ni = s // tr
    b_sb, b_cv = off_sb // gw, off_cv // gw

    def body(gs_ref, gc_ref, osb_ref, ocv_ref, dm_ref, dosb_ref, docv_ref, dgs_ref, dgc_ref, sgs_ref, sgc_ref):
        dm = dm_ref[...]
        s_sb, s_cv = _sigmoid(gs_ref[...]), _sigmoid(gc_ref[...])
        dosb_ref[...] = (dm * s_sb).astype(BF16)
        docv_ref[...] = (dm * s_cv).astype(BF16)
        dgs = dm * osb_ref[...] * s_sb * (1.0 - s_sb)
        dgc = dm * ocv_ref[...] * s_cv * (1.0 - s_cv)
        dgs_ref[...] = dgs.astype(BF16)
        dgc_ref[...] = dgc.astype(BF16)
        i = pl.program_id(1)
        for ref, val in ((sgs_ref, dgs), (sgc_ref, dgc)):
            col = jnp.sum(val, axis=0, keepdims=True)

            @pl.when(i == 0)
            def _():
                ref[...] = col

            @pl.when(i > 0)
            def _():
                ref[...] += col

    blk = lambda j, i: (i, j)
    outs = pl.pallas_call(
        body, name="merge_bwd", grid=(nj, ni),
        in_specs=[pl.BlockSpec((tr, gw), lambda j, i: (i, b_sb + j)), pl.BlockSpec((tr, gw), lambda j, i: (i, b_cv + j)),
                  pl.BlockSpec((tr, gw), blk), pl.BlockSpec((tr, gw), blk), pl.BlockSpec((tr, gw), blk)],
        out_specs=[pl.BlockSpec((tr, gw), blk), pl.BlockSpec((tr, gw), blk),
                   pl.BlockSpec((tr, gw), blk), pl.BlockSpec((tr, gw), blk),
                   pl.BlockSpec((1, gw), lambda j, i: (0, j)), pl.BlockSpec((1, gw), lambda j, i: (0, j))],
        out_shape=[jax.ShapeDtypeStruct((s, d), BF16)] * 4 + [jax.ShapeDtypeStruct((1, d), F32)] * 2,
        compiler_params=_params(dimension_semantics=("parallel", "arbitrary")),
    )(proj, proj, o_sb, o_cv, dmerged)
    return outs


def _split_bf16(v):
    hi = v.astype(BF16)
    lo = (v - hi.astype(F32)).astype(BF16)
    return hi, lo


def _dot_nt(a, b):
    return lax.dot_general(a, b, (((1,), (1,)), ((), ())), preferred_element_type=F32)


def _dot_tn(a, b):
    return lax.dot_general(a, b, (((0,), (0,)), ((), ())), preferred_element_type=F32)


def _dot_nn(a, b):
    return lax.dot_general(a, b, (((1,), (0,)), ((), ())), preferred_element_type=F32)


def _sum_right(v, tri):
    hi, lo = _split_bf16(v)
    return _dot_nn(hi, tri) + _dot_nn(lo, tri)


HEADS_PER_STEP = 2


LOG2_E = 1.4426950408889634


def _sb_tile(q, kb, scale, mask):
    z = _dot_nt(q, kb) * (scale * LOG2_E)
    log_b = jnp.minimum(z, 0.0) - jnp.log2(1.0 + jnp.exp2(-jnp.abs(z)))
    log_1m = log_b - z
    if mask is not None:
        log_1m = jnp.where(mask, log_1m, 0.0)
    return log_b, log_1m


def _tile_mask(tq):
    row = lax.broadcasted_iota(jnp.int32, (tq, tq), 0)
    col = lax.broadcasted_iota(jnp.int32, (tq, tq), 1)
    return col < row


def _tri(tq, before):
    r = lax.broadcasted_iota(jnp.int32, (tq, tq), 0)
    c = lax.broadcasted_iota(jnp.int32, (tq, tq), 1)
    return jnp.where((r < c) if before else (r > c), 1.0, 0.0).astype(BF16)


def _attn_fwd(proj, after, n_heads, d_sb, tq=256):
    s = proj.shape[0]
    tq = _tile(s, tq)
    hb = d_sb // HEAD_DIM
    scale = 1.0 / math.sqrt(HEAD_DIM)

    heads = [pl.ds(n * HEAD_DIM, HEAD_DIM) for n in range(HEADS_PER_STEP)]
    wide = HEADS_PER_STEP * HEAD_DIM
    hb //= HEADS_PER_STEP

    def body(q_ref, k_ref, v_ref, _, o_ref):
        i = pl.program_id(1)
        tri = _tri(tq, False)

        def tile(j, carry, diagonal):
            rows = pl.ds(pl.multiple_of(j * tq, tq), tq)
            mask = _tile_mask(tq) if diagonal else None
            out = []
            for hd, (c_l, acc) in zip(heads, carry):
                log_b, log_1m = _sb_tile(q_ref[:, hd].astype(BF16), k_ref[rows, hd].astype(BF16), scale, mask)
                a = jnp.exp2(log_b + (c_l + _sum_right(log_1m, tri)))
                if diagonal:
                    a = jnp.where(mask, a, 0.0)
                out.append((c_l + jnp.sum(log_1m, axis=1, keepdims=True),
                            acc + _dot_nn(a.astype(BF16), v_ref[rows, hd].astype(BF16))))
            return tuple(out)

        init = tuple((jnp.zeros((tq, 1), F32), jnp.zeros((tq, HEAD_DIM), F32)) for _ in heads)
        carry = lax.fori_loop(0, i, lambda jj, c: tile(i - 1 - jj, c, False), tile(i, init, True))
        for hd, (_, acc) in zip(heads, carry):
            o_ref[:, hd] = acc.astype(BF16)

    return pl.pallas_call(
        body, name="attn_fwd", grid=(n_heads // HEADS_PER_STEP, s // tq),
        in_specs=[pl.BlockSpec((tq, wide), lambda h, i: (i, h)),
                  pl.BlockSpec((s, wide), lambda h, i: (0, hb + h)),
                  pl.BlockSpec((s, wide), lambda h, i: (0, 2 * hb + h)), pl.BlockSpec(memory_space=pl.ANY)],
        out_specs=pl.BlockSpec((tq, wide), lambda h, i: (i, h)),
        out_shape=jax.ShapeDtypeStruct((s, d_sb), BF16),
        compiler_params=_params(dimension_semantics=("parallel", "arbitrary")),
    )(proj, proj, proj, after)


def _attn_bwd(proj, dattn, after, n_heads, d_sb, tq=256):
    s = proj.shape[0]
    tq = _tile(s, tq)
    nq = s // tq
    hb = d_sb // HEAD_DIM // HEADS_PER_STEP
    scale = 1.0 / math.sqrt(HEAD_DIM)
    heads = [pl.ds(n * HEAD_DIM, HEAD_DIM) for n in range(HEADS_PER_STEP)]
    wide = HEADS_PER_STEP * HEAD_DIM

    def body(q_ref, k_ref, v_ref, do_ref, _, dq_ref, dk_ref, dv_ref, sq_ref, sk_ref, sv_ref, dk_acc, dv_acc, g_st, b_st):
        i = pl.program_id(1)

        @pl.when(i == 0)
        def _():
            dk_acc[...] = jnp.zeros_like(dk_acc)
            dv_acc[...] = jnp.zeros_like(dv_acc)

        tri_after = _tri(tq, False)
        tri_before = _tri(tq, True)

        def newest_first(j, c_ls, diagonal):
            rows = pl.ds(pl.multiple_of(j * tq, tq), tq)
            mask = _tile_mask(tq) if diagonal else None
            out = []
            for n, (hd, c_l) in enumerate(zip(heads, c_ls)):
                do = do_ref[:, hd].astype(BF16)
                log_b, log_1m = _sb_tile(q_ref[:, hd].astype(BF16), k_ref[rows, hd].astype(BF16), scale, mask)
                a = jnp.exp2(log_b + (c_l + _sum_right(log_1m, tri_after)))
                beta = jnp.exp2(log_b)
                if diagonal:
                    a = jnp.where(mask, a, 0.0)
                    beta = jnp.where(mask, beta, 0.0)
                g_st[n, j] = a * _dot_nt(do, v_ref[rows, hd].astype(BF16))
                b_st[n, j] = beta
                dv_acc[rows, hd] += _dot_tn(a.astype(BF16), do)
                out.append(c_l + jnp.sum(log_1m, axis=1, keepdims=True))
            return tuple(out)

        lax.fori_loop(0, i, lambda jj, c: newest_first(i - 1 - jj, c, False),
                      newest_first(i, tuple(jnp.zeros((tq, 1), F32) for _ in heads), True))

        def oldest_first(j, carry):
            rows = pl.ds(pl.multiple_of(j * tq, tq), tq)
            out = []
            for n, (hd, (c_g, dq)) in enumerate(zip(heads, carry)):
                kb = k_ref[rows, hd].astype(BF16)
                g = g_st[n, j]
                beta = b_st[n, j]
                g_before = c_g + _sum_right(g, tri_before)
                dz16 = ((g * (1.0 - beta) - g_before * beta) * scale).astype(BF16)
                dk_acc[rows, hd] += _dot_tn(dz16, q_ref[:, hd].astype(BF16))
                out.append((c_g + jnp.sum(g, axis=1, keepdims=True), dq + _dot_nn(dz16, kb)))
            return tuple(out)

        init = tuple((jnp.zeros((tq, 1), F32), jnp.zeros((tq, HEAD_DIM), F32)) for _ in heads)
        carry = lax.fori_loop(0, i + 1, oldest_first, init)
        dq = jnp.concatenate([c[1] for c in carry], axis=1)
        dq_ref[...] = dq.astype(BF16)
        col = jnp.sum(dq, axis=0, keepdims=True)

        @pl.when(i == 0)
        def _():
            sq_ref[...] = col

        @pl.when(i > 0)
        def _():
            sq_ref[...] += col

        @pl.when(i == nq - 1)
        def _():
            dk = dk_acc[...]
            dv = dv_acc[...]
            dk_ref[...] = dk.astype(BF16)
            dv_ref[...] = dv.astype(BF16)
            sk_ref[...] = jnp.sum(dk, axis=0, keepdims=True)
            sv_ref[...] = jnp.sum(dv, axis=0, keepdims=True)

    qblk = pl.BlockSpec((tq, wide), lambda h, i: (i, h))
    kblk = pl.BlockSpec((s, wide), lambda h, i: (0, h))
    col_sum = pl.BlockSpec((1, wide), lambda h, i: (0, h))
    stash = pltpu.VMEM((HEADS_PER_STEP, nq, tq, tq), F32)
    outs = pl.pallas_call(
        body, name="attn_bwd", grid=(n_heads // HEADS_PER_STEP, nq),
        in_specs=[qblk,
                  pl.BlockSpec((s, wide), lambda h, i: (0, hb + h)),
                  pl.BlockSpec((s, wide), lambda h, i: (0, 2 * hb + h)),
                  qblk, pl.BlockSpec(memory_space=pl.ANY)],
        out_specs=[qblk, kblk, kblk, col_sum, col_sum, col_sum],
        out_shape=[jax.ShapeDtypeStruct((s, d_sb), BF16)] * 3 + [jax.ShapeDtypeStruct((1, d_sb), F32)] * 3,
        scratch_shapes=[pltpu.VMEM((s, wide), F32), pltpu.VMEM((s, wide), F32), stash, stash],
        compiler_params=_params(dimension_semantics=("parallel", "arbitrary")),
    )(proj, proj, proj, dattn, after)
    return outs


LANES = 128


def _glu(proj, off_a, dc, tr=256):
    s = proj.shape[0]
    ba = off_a // dc

    def body(a_ref, b_ref, u_ref):
        u_ref[...] = a_ref[...] * _sigmoid(b_ref[...])

    return pl.pallas_call(
        body, name="glu", grid=(s // tr,),
        in_specs=[pl.BlockSpec((tr, dc), lambda i: (i, ba)), pl.BlockSpec((tr, dc), lambda i: (i, ba + 1))],
        out_specs=pl.BlockSpec((tr, dc), lambda i: (i, 0)), out_shape=jax.ShapeDtypeStruct((s, dc), F32),
        compiler_params=_params(dimension_semantics=("parallel",)),
    )(proj, proj)


def _conv_fwd(u0, taps, b_dw, g_ln, b_ln, tt=256):
    s, dc = u0.shape
    tt = _tile(s, tt)
    hpb = tt // CONV_HALO

    def body(cur_ref, halo_ref, w_ref, bdw_ref, g_ref, b_ref, u1_ref, u3_ref, xs):
        i = pl.program_id(0)
        xs[pl.ds(0, CONV_HALO), :] = jnp.where(i > 0, halo_ref[...], 0.0)
        xs[pl.ds(CONV_HALO, tt), :] = cur_ref[...]
        for c0 in range(0, dc, LANES):
            cols = pl.ds(c0, LANES)
            acc = jnp.broadcast_to(bdw_ref[:, cols], (tt, LANES))
            for k in range(CONV_WIDTH):
                acc = acc + w_ref[pl.ds(k, 1), cols] * xs[pl.ds(CONV_HALO - (CONV_WIDTH - 1) + k, tt), cols]
            u1_ref[:, cols] = acc
        u1 = u1_ref[...]
        mu = jnp.mean(u1, axis=-1, keepdims=True)
        xc = u1 - mu
        rstd = lax.rsqrt(jnp.mean(xc * xc, axis=-1, keepdims=True) + EPS)
        u2 = xc * rstd * g_ref[...] + b_ref[...]
        u3_ref[...] = (u2 * _sigmoid(u2)).astype(BF16)

    vec = pl.BlockSpec((1, dc), lambda i: (0, 0))
    return pl.pallas_call(
        body, name="conv_fwd", grid=(s // tt,),
        in_specs=[pl.BlockSpec((tt, dc), lambda i: (i, 0)),
                  pl.BlockSpec((CONV_HALO, dc), lambda i: (jnp.maximum(i * hpb - 1, 0), 0)),
                  pl.BlockSpec((CONV_HALO, dc), lambda i: (0, 0)), vec, vec, vec],
        out_specs=[pl.BlockSpec((tt, dc), lambda i: (i, 0)), pl.BlockSpec((tt, dc), lambda i: (i, 0))],
        out_shape=[jax.ShapeDtypeStruct((s, dc), F32), jax.ShapeDtypeStruct((s, dc), BF16)],
        scratch_shapes=[pltpu.VMEM((tt + CONV_HALO, dc), F32)],
        compiler_params=_params(dimension_semantics=("arbitrary",)),
    )(u0, u0, taps, b_dw, g_ln, b_ln)


def _conv_bwd_norm(u1, du3, g_ln, b_ln, tr=256):
    dc = u1.shape[1]

    def body(u1_ref, du3_ref, g_ref, b_ref, du1_ref, dg_ref, db_ref, dbdw_ref):
        u1v = u1_ref[...]
        g = g_ref[...]
        mu = jnp.mean(u1v, axis=-1, keepdims=True)
        xc = u1v - mu
        rstd = lax.rsqrt(jnp.mean(xc * xc, axis=-1, keepdims=True) + EPS)
        xhat = xc * rstd
        u2 = xhat * g + b_ref[...]
        sg = _sigmoid(u2)
        du2 = du3_ref[...] * (sg * (1.0 + u2 * (1.0 - sg)))
        _accumulate(dg_ref, jnp.sum(du2 * xhat, axis=0, keepdims=True))
        _accumulate(db_ref, jnp.sum(du2, axis=0, keepdims=True))
        gy = du2 * g
        du1 = rstd * (gy - jnp.mean(gy, axis=-1, keepdims=True) - xhat * jnp.mean(gy * xhat, axis=-1, keepdims=True))
        du1_ref[...] = du1
        _accumulate(dbdw_ref, jnp.sum(du1, axis=0, keepdims=True))

    return _row_call(body, name="conv_bwd_norm", rows=u1.shape[0], tr=tr,
                     ins=[(u1, "row"), (du3, "row"), (g_ln, "vec"), (b_ln, "vec")],
                     outs=[(dc, F32)], acc_outs=(dc, dc, dc))


def _conv_bwd_taps(du1, u0, proj, taps, off_a, tt=256):
    s, dc = u0.shape
    tt = _tile(s, tt)
    n = s // tt
    hpb = tt // CONV_HALO
    ba = off_a // dc
    lead = CONV_HALO - (CONV_WIDTH - 1)

    def body(d_ref, dnext_ref, cur_ref, halo_ref, a_ref, b_ref, w_ref, da_ref, db_ref, sa_ref, sb_ref, dw_ref,
             xs, ds, du0):
        i = pl.program_id(0)
        xs[pl.ds(0, CONV_HALO), :] = jnp.where(i > 0, halo_ref[...], 0.0)
        xs[pl.ds(CONV_HALO, tt), :] = cur_ref[...]
        ds[pl.ds(0, tt), :] = d_ref[...]
        ds[pl.ds(tt, CONV_HALO), :] = jnp.where(i < n - 1, dnext_ref[...], 0.0)

        @pl.when(i == 0)
        def _():
            dw_ref[...] = jnp.zeros_like(dw_ref)

        for c0 in range(0, dc, LANES):
            cols = pl.ds(c0, LANES)
            d_cur = ds[pl.ds(0, tt), cols]
            acc = jnp.zeros((tt, LANES), F32)
            for k in range(CONV_WIDTH):
                acc = acc + w_ref[pl.ds(k, 1), cols] * ds[pl.ds(CONV_WIDTH - 1 - k, tt), cols]
                dw_ref[pl.ds(k, 1), cols] += jnp.sum(d_cur * xs[pl.ds(lead + k, tt), cols], axis=0, keepdims=True)
            du0[:, cols] = acc
        d0 = du0[...]
        sg = _sigmoid(b_ref[...])
        da = d0 * sg
        db = d0 * a_ref[...] * sg * (1.0 - sg)
        da_ref[...] = da.astype(BF16)
        db_ref[...] = db.astype(BF16)
        _accumulate(sa_ref, jnp.sum(da, axis=0, keepdims=True))
        _accumulate(sb_ref, jnp.sum(db, axis=0, keepdims=True))

    row = pl.BlockSpec((tt, dc), lambda i: (i, 0))
    vec = pl.BlockSpec((1, dc), lambda i: (0, 0))
    taps_spec = pl.BlockSpec((CONV_HALO, dc), lambda i: (0, 0))
    return pl.pallas_call(
        body, name="conv_bwd_taps", grid=(n,),
        in_specs=[row, pl.BlockSpec((CONV_HALO, dc), lambda i: (jnp.minimum((i + 1) * hpb, n * hpb - 1), 0)),
                  row, pl.BlockSpec((CONV_HALO, dc), lambda i: (jnp.maximum(i * hpb - 1, 0), 0)),
                  pl.BlockSpec((tt, dc), lambda i: (i, ba)), pl.BlockSpec((tt, dc), lambda i: (i, ba + 1)), taps_spec],
        out_specs=[row, row, vec, vec, taps_spec],
        out_shape=[jax.ShapeDtypeStruct((s, dc), BF16), jax.ShapeDtypeStruct((s, dc), BF16),
                   jax.ShapeDtypeStruct((1, dc), F32), jax.ShapeDtypeStruct((1, dc), F32),
                   jax.ShapeDtypeStruct((CONV_HALO, dc), F32)],
        scratch_shapes=[pltpu.VMEM((tt + CONV_HALO, dc), F32), pltpu.VMEM((tt + CONV_HALO, dc), F32),
                        pltpu.VMEM((tt, dc), F32)],
        compiler_params=_params(dimension_semantics=("arbitrary",)),
    )(du1, du1, u0, u0, proj, proj, taps)


def _elementwise(body, *, name, ins, out_dtypes, tr=128):
    rows, cols = ins[0].shape
    tr = _tile(rows, tr, 8)
    spec = pl.BlockSpec((tr, cols), lambda i: (i, 0))
    return pl.pallas_call(
        body, name=name, grid=(rows // tr,), in_specs=[spec] * len(ins), out_specs=[spec] * len(out_dtypes),
        out_shape=[jax.ShapeDtypeStruct((rows, cols), dt) for dt in out_dtypes],
        compiler_params=_params(dimension_semantics=("parallel",)),
    )(*ins)


def _prefetched(body, *, name, where, grid, in_specs, out_specs, out_shape, operands):
    return pl.pallas_call(
        body, name=name, out_shape=out_shape,
        grid_spec=pltpu.PrefetchScalarGridSpec(num_scalar_prefetch=1, grid=grid, in_specs=in_specs, out_specs=out_specs),
        compiler_params=_params(dimension_semantics=("parallel",) * len(grid)),
    )(where, *operands)


def _cast_into_block(w, where, after, name, tr=256):
    rows, cols = w.shape
    tr = _tile(rows, tr, 16)

    def body(where_ref, w_ref, _, o_ref):
        o_ref[...] = w_ref[...].astype(BF16)

    return _prefetched(
        body, name=name, where=where, grid=(rows // tr,),
        in_specs=[pl.BlockSpec((tr, cols), lambda i, wh: (i, 0)), pl.BlockSpec(memory_space=pl.ANY)],
        out_specs=pl.BlockSpec((None, tr, cols), lambda i, wh: (wh[0], i, 0)),
        out_shape=jax.ShapeDtypeStruct((N_CHIPS, rows, cols), BF16), operands=[w, after])


def _add_pair(grad, theirs, where, name, tr=512):
    nb, half, cols = theirs.shape
    tr = _tile(half, tr, 16)
    nh = half // tr

    def body(where_ref, a_ref, b_ref, o_ref):
        o_ref[...] = a_ref[...] + b_ref[...]

    blk = pl.BlockSpec((None, tr, cols), lambda b, i, wh: (b, i, 0))
    return _prefetched(
        body, name=name, where=where, grid=(nb, nh),
        in_specs=[pl.BlockSpec((None, tr, cols), lambda b, i, wh: (b, wh[1] * nh + i, 0)), blk],
        out_specs=blk, out_shape=jax.ShapeDtypeStruct(theirs.shape, BF16), operands=[grad, theirs])


def _sum_slots(part, slots, where, name, tr=512):
    _, half, cols = part.shape
    tr = _tile(half, tr, 16)
    nh = half // tr

    def body(where_ref, p_ref, s_ref, o_ref):
        o_ref[...] = ((p_ref[...].astype(F32) + s_ref[0].astype(F32)) + s_ref[1].astype(F32)) + s_ref[2].astype(F32)

    return _prefetched(
        body, name=name, where=where, grid=(nh,),
        in_specs=[pl.BlockSpec((None, tr, cols), lambda i, wh: (wh[0], i, 0)),
                  pl.BlockSpec((3, tr, cols), lambda i, wh: (0, i, 0))],
        out_specs=pl.BlockSpec((tr, cols), lambda i, wh: (wh[1] * nh + i, 0)),
        out_shape=jax.ShapeDtypeStruct((2 * half, cols), F32), operands=[part, slots])


def _adamw(w, g, m, v, name):
    bc1 = 1.0 - ADAM_B1 ** ADAM_STEP
    bc2 = 1.0 - ADAM_B2 ** ADAM_STEP

    def body(w_ref, g_ref, m_ref, v_ref, go_ref, d_ref, mo_ref, vo_ref):
        g = g_ref[...]
        m_new = ADAM_B1 * m_ref[...] + (1.0 - ADAM_B1) * g
        v_new = ADAM_B2 * v_ref[...] + (1.0 - ADAM_B2) * (g * g)
        m_hat = m_new / bc1
        v_hat = v_new / bc2
        go_ref[...] = g
        d_ref[...] = -ADAM_LR * (m_hat / (jnp.sqrt(v_hat) + ADAM_EPS) + ADAM_WD * w_ref[...])
        mo_ref[...] = m_new
        vo_ref[...] = v_new

    return _elementwise(body, name=name, ins=[w, g, m, v], out_dtypes=[F32] * 4, tr=128)


ANY = pl.BlockSpec(memory_space=pl.ANY)


def _mesh_pos():
    return lax.axis_index("x"), lax.axis_index("y"), lax.axis_index("c")


def _other_chips(x, y):
    return [(1 - x, y), (x, 1 - y), (1 - x, 1 - y)]


def _remote(src, dst, send_sems, recv_sems, idx, to):
    return pltpu.make_async_remote_copy(src_ref=src, dst_ref=dst, send_sem=send_sems.at[idx], recv_sem=recv_sems.at[idx],
                                        device_id=to, device_id_type=MESH)


HBM = pl.BlockSpec(memory_space=pltpu.HBM)
SEM = pl.BlockSpec(memory_space=pltpu.SEMAPHORE)
EFFECT = pltpu.SideEffectType.DATAFLOW_SIDE_EFFECTING


def _in_hbm(arrays):
    return [pltpu.with_memory_space_constraint(a, pltpu.HBM) for a in arrays]


def _ici_copies(slab, send_sems, recv_sems, x, y, c):
    half = slab.shape[1] // 2
    rows = pl.ds(c * half, half)
    mine = slab.at[2 * x + y, rows]
    out = []
    for j, (cx, cy) in enumerate(_other_chips(x, y)):
        got = slab.at[2 * cx + cy, rows]
        out.append((_remote(mine, mine, send_sems, recv_sems, j, (cx, cy, c)),
                    _remote(got, got, send_sems, recv_sems, j, (cx, cy, c))))
    return out


def _gather_start(slabs, after, name):
    n = len(slabs)

    def body(*refs):
        ins = refs[:n]
        send_sems, recv_sems, token = refs[2 * n + 1:3 * n + 1], refs[3 * n + 1:4 * n + 1], refs[4 * n + 1]
        x, y, c = _mesh_pos()
        for w in range(n):
            for send, _ in _ici_copies(ins[w], send_sems[w], recv_sems[w], x, y, c):
                send.start()
        token[...] = jnp.zeros_like(token)

    outs = pl.pallas_call(
        body, name=name, in_specs=[HBM] * n + [ANY],
        out_specs=[HBM] * n + [SEM] * (2 * n) + [pl.BlockSpec(memory_space=pltpu.VMEM)],
        out_shape=[pltpu.HBM(s.shape, s.dtype) for s in slabs] + [pltpu.SemaphoreType.DMA((3,))] * (2 * n)
        + [jax.ShapeDtypeStruct((8, 128), F32)],
        input_output_aliases={w: w for w in range(n)},
        compiler_params=_params(has_side_effects=EFFECT),
    )(*_in_hbm(slabs), after)
    return (outs[:n], outs[n:2 * n], outs[2 * n:3 * n]), outs[3 * n]


def _gather_wait(slabs, send_sems, recv_sems, after, name):
    n = len(slabs)

    def body(*refs):
        ins = refs[:n]
        sends, recvs = refs[n:2 * n], refs[2 * n:3 * n]
        x, y, c = _mesh_pos()
        for w in range(n):
            for send, recv in _ici_copies(ins[w], sends[w], recvs[w], x, y, c):
                send.wait_send()
                recv.wait_recv()

    return pl.pallas_call(
        body, name=name, in_specs=[HBM] * n + [SEM] * (2 * n) + [ANY], out_specs=[HBM] * n,
        out_shape=[pltpu.HBM(s.shape, s.dtype) for s in slabs],
        input_output_aliases={w: w for w in range(n)},
        compiler_params=_params(has_side_effects=EFFECT),
    )(*slabs, *send_sems, *recv_sems, after)


def _phase_start(arrays, copies, n_copies, after, name):
    n = len(arrays)

    def body(*refs):
        send_sems, recv_sems, token = refs[2 * n + 1], refs[2 * n + 2], refs[2 * n + 3]
        for send, _ in copies(refs[:n], send_sems, recv_sems, *_mesh_pos()):
            send.start()
        token[...] = jnp.zeros_like(token)

    outs = pl.pallas_call(
        body, name=name, in_specs=[HBM] * n + [ANY],
        out_specs=[HBM] * n + [SEM, SEM, pl.BlockSpec(memory_space=pltpu.VMEM)],
        out_shape=[pltpu.HBM(a.shape, a.dtype) for a in arrays]
        + [pltpu.SemaphoreType.DMA((n_copies,))] * 2 + [jax.ShapeDtypeStruct((8, 128), F32)],
        input_output_aliases={k: k for k in range(n)},
        compiler_params=_params(has_side_effects=EFFECT),
    )(*_in_hbm(arrays), after)
    return outs[:n], outs[n], outs[n + 1], outs[n + 2]


def _phase_wait(arrays, send_sems, recv_sems, copies, after, name):
    n = len(arrays)

    def body(*refs):
        for send, recv in copies(refs[:n], refs[n], refs[n + 1], *_mesh_pos()):
            send.wait_send()
            recv.wait_recv()

    return pl.pallas_call(
        body, name=name, in_specs=[HBM] * n + [SEM, SEM, ANY], out_specs=[HBM] * n,
        out_shape=[pltpu.HBM(a.shape, a.dtype) for a in arrays],
        input_output_aliases={k: k for k in range(n)},
        compiler_params=_params(has_side_effects=EFFECT),
    )(*arrays, send_sems, recv_sems, after)


def _forward_copies(slabs, send_sems, recv_sems, x, y, c):
    sibling = (x, y, 1 - c)
    out = []
    for w, slab in enumerate(slabs):
        half = slab.shape[1] // 2
        for j, (cx, cy) in enumerate(_other_chips(x, y)):
            mine = slab.at[2 * cx + cy, pl.ds(c * half, half)]
            theirs = slab.at[2 * cx + cy, pl.ds((1 - c) * half, half)]
            out.append((_remote(mine, mine, send_sems, recv_sems, 3 * w + j, sibling),
                        _remote(theirs, theirs, send_sems, recv_sems, 3 * w + j, sibling)))
    return out


def _exchange_copies(arrays, send_sems, recv_sems, x, y, c):
    n = len(arrays) // 2
    sibling = (x, y, 1 - c)
    out = []
    for w in range(n):
        grad, land = arrays[w], arrays[n + w]
        half = grad.shape[1] // 2
        out.append((_remote(grad.at[:, pl.ds((1 - c) * half, half), :], land, send_sems, recv_sems, w, sibling),
                    _remote(land, land, send_sems, recv_sems, w, sibling)))
    return out


def _gather_taps(taps):
    def body(in_ref, out_ref, send_sems, recv_sems, local_sem):
        x, y, c = _mesh_pos()
        me = 2 * x + y
        chips = _other_chips(x, y)
        lc = pltpu.make_async_copy(in_ref, out_ref.at[me], local_sem)
        lc.start()
        sends = [_remote(in_ref, out_ref.at[me], send_sems, recv_sems, j, (cx, cy, c)) for j, (cx, cy) in enumerate(chips)]
        for cp in sends:
            cp.start()
        for j, (cx, cy) in enumerate(chips):
            got = out_ref.at[2 * cx + cy]
            _remote(got, got, send_sems, recv_sems, j, (cx, cy, c)).wait_recv()
        for cp in sends:
            cp.wait_send()
        lc.wait()

    vm = pl.BlockSpec(memory_space=pltpu.VMEM)
    return pl.pallas_call(
        body, name="gather_taps", in_specs=[vm], out_specs=vm,
        out_shape=jax.ShapeDtypeStruct((N_CHIPS,) + taps.shape, taps.dtype),
        scratch_shapes=[pltpu.SemaphoreType.DMA((3,)), pltpu.SemaphoreType.DMA((3,)), pltpu.SemaphoreType.DMA],
        compiler_params=_params(has_side_effects=True),
    )(taps)


def _exchange_halves(grads, name):
    n = len(grads)

    def body(*refs):
        ins, their_refs = refs[:n], refs[n:2 * n]
        send_sems, recv_sems = refs[2 * n:]
        x, y, c = _mesh_pos()
        sibling = (x, y, 1 - c)
        sends = []
        for w in range(n):
            half = ins[w].shape[1] // 2
            cp = _remote(ins[w].at[:, pl.ds((1 - c) * half, half), :], their_refs[w], send_sems, recv_sems, w, sibling)
            cp.start()
            sends.append(cp)
        for w in range(n):
            _remote(their_refs[w], their_refs[w], send_sems, recv_sems, w, sibling).wait_recv()
        for cp in sends:
            cp.wait_send()

    halves = [jax.ShapeDtypeStruct((g.shape[0], g.shape[1] // 2, g.shape[2]), g.dtype) for g in grads]
    return pl.pallas_call(
        body, name=name, in_specs=[ANY] * n, out_specs=[ANY] * n, out_shape=halves,
        scratch_shapes=[pltpu.SemaphoreType.DMA((n,)), pltpu.SemaphoreType.DMA((n,))],
        compiler_params=_params(has_side_effects=True),
    )(*grads)


def _scatter_copies(part, slots, send_sems, recv_sems, x, y, c):
    out = []
    for j, (cx, cy) in enumerate(_other_chips(x, y)):
        out.append((_remote(part.at[2 * cx + cy], slots.at[j], send_sems, recv_sems, j, (cx, cy, c)),
                    _remote(slots.at[j], slots.at[j], send_sems, recv_sems, j, (cx, cy, c))))
    return out


def _scatter_start(parts, after, name):
    n = len(parts)
    slots = [lax.empty((3,) + p.shape[1:], p.dtype) for p in parts]

    def body(*refs):
        ins, lands = refs[:n], refs[n:2 * n]
        send_sems, recv_sems, token = refs[4 * n + 1:5 * n + 1], refs[5 * n + 1:6 * n + 1], refs[6 * n + 1]
        x, y, c = _mesh_pos()
        for w in range(n):
            for send, _ in _scatter_copies(ins[w], lands[w], send_sems[w], recv_sems[w], x, y, c):
                send.start()
        token[...] = jnp.zeros_like(token)

    outs = pl.pallas_call(
        body, name=name, in_specs=[HBM] * (2 * n) + [ANY],
        out_specs=[HBM] * (2 * n) + [SEM] * (2 * n) + [pl.BlockSpec(memory_space=pltpu.VMEM)],
        out_shape=[pltpu.HBM(a.shape, a.dtype) for a in list(parts) + slots] + [pltpu.SemaphoreType.DMA((3,))] * (2 * n)
        + [jax.ShapeDtypeStruct((8, 128), F32)],
        input_output_aliases={k: k for k in range(2 * n)},
        compiler_params=_params(has_side_effects=EFFECT),
    )(*_in_hbm(list(parts) + slots), after)
    return (outs[:n], outs[n:2 * n], outs[2 * n:3 * n], outs[3 * n:4 * n]), outs[4 * n]


def _scatter_wait(parts, slots, send_sems, recv_sems, after, name):
    n = len(parts)

    def body(*refs):
        ins, lands = refs[:n], refs[n:2 * n]
        sends, recvs = refs[2 * n:3 * n], refs[3 * n:4 * n]
        x, y, c = _mesh_pos()
        for w in range(n):
            for send, recv in _scatter_copies(ins[w], lands[w], sends[w], recvs[w], x, y, c):
                send.wait_send()
                recv.wait_recv()

    outs = pl.pallas_call(
        body, name=name, in_specs=[HBM] * (2 * n) + [SEM] * (2 * n) + [ANY], out_specs=[HBM] * (2 * n),
        out_shape=[pltpu.HBM(a.shape, a.dtype) for a in list(parts) + list(slots)],
        input_output_aliases={k: k for k in range(2 * n)},
        compiler_params=_params(has_side_effects=EFFECT),
    )(*parts, *slots, *send_sems, *recv_sems, after)
    return outs[:n], outs[n:]


def _share_halves(halves, name):
    n = len(halves)

    def body(*refs):
        outs = refs[n:2 * n]
        send_sems, recv_sems = refs[2 * n:]
        x, y, c = _mesh_pos()
        sibling = (x, y, 1 - c)
        sends = []
        for w in range(n):
            h = outs[w].shape[0] // 2
            mine = outs[w].at[pl.ds(c * h, h)]
            cp = _remote(mine, mine, send_sems, recv_sems, w, sibling)
            cp.start()
            sends.append(cp)
        for w in range(n):
            h = outs[w].shape[0] // 2
            got = outs[w].at[pl.ds((1 - c) * h, h)]
            _remote(got, got, send_sems, recv_sems, w, sibling).wait_recv()
        for cp in sends:
            cp.wait_send()

    return pl.pallas_call(
        body, name=name, in_specs=[ANY] * n, out_specs=[ANY] * n,
        out_shape=[jax.ShapeDtypeStruct(h.shape, h.dtype) for h in halves],
        input_output_aliases={w: w for w in range(n)},
        scratch_shapes=[pltpu.SemaphoreType.DMA((n,)), pltpu.SemaphoreType.DMA((n,))],
        compiler_params=_params(has_side_effects=True),
    )(*halves)


def _all_reduce_small(buf, after, name):
    rows, width = buf.shape

    def body(in_ref, _, out_ref, gathered, send_sems, recv_sems):
        x, y, c = _mesh_pos()
        me = 4 * x + 2 * y + c
        gathered[me] = in_ref[...]
        flips = [(dx, dy, dz) for dx in (0, 1) for dy in (0, 1) for dz in (0, 1)][1:]
        peers = [((1 - x) if dx else x, (1 - y) if dy else y, (1 - c) if dz else c) for dx, dy, dz in flips]
        sends = [_remote(in_ref, gathered.at[me], send_sems, recv_sems, k, peer) for k, peer in enumerate(peers)]
        for cp in sends:
            cp.start()
        for k, (px, py, pc) in enumerate(peers):
            got = gathered.at[4 * px + 2 * py + pc]
            _remote(got, got, send_sems, recv_sems, k, (px, py, pc)).wait_recv()
        for cp in sends:
            cp.wait_send()
        total = gathered[0]
        for d in range(1, 8):
            total = total + gathered[d]
        out_ref[...] = total

    vm = pl.BlockSpec(memory_space=pltpu.VMEM)
    return pl.pallas_call(
        body, name=name, in_specs=[vm, ANY], out_specs=vm, out_shape=jax.ShapeDtypeStruct(buf.shape, F32),
        scratch_shapes=[pltpu.VMEM((8, rows, width), F32), pltpu.SemaphoreType.DMA((7,)), pltpu.SemaphoreType.DMA((7,))],
        compiler_params=_params(has_side_effects=True),
    )(buf, after)


PACK_W = 2048
PACK_ROWS = 8


def _pack(vectors):
    flat = jnp.concatenate([v.reshape(-1) for v in vectors])
    unit = PACK_W * PACK_ROWS
    total = -(-flat.shape[0] // unit) * unit
    return jnp.pad(flat, (0, total - flat.shape[0])).reshape(total // PACK_W, PACK_W)


def _unpack(buf, shapes):
    flat = buf.reshape(-1)
    out, pos = [], 0
    for shp in shapes:
        size = math.prod(shp)
        out.append(flat[pos:pos + size].reshape(shp))
        pos += size
    return out


def _pad_rows(a, rows):
    return jnp.pad(a, ((0, rows - a.shape[0]), (0, 0)))


def kernel(x, g_pre_mix, w_in, b_in, w_dw, b_dw, g_conv_ln, b_conv_ln, w_sb_out, w_conv_out, w_o, g_post_mix, g_pre_mlp, w_up, w_down, g_post_mlp, loss_target, m_g_pre_mix, m_w_in, m_b_in, m_w_dw, m_b_dw, m_g_conv_ln, m_b_conv_ln, m_w_sb_out, m_w_conv_out, m_w_o, m_g_post_mix, m_g_pre_mlp, m_w_up, m_w_down, m_g_post_mlp, v_g_pre_mix, v_w_in, v_b_in, v_w_dw, v_b_dw, v_g_conv_ln, v_b_conv_ln, v_w_sb_out, v_w_conv_out, v_w_o, v_g_post_mix, v_g_pre_mlp, v_w_up, v_w_down, v_g_post_mlp):
    xs = x[0]
    tgt = loss_target[0]
    s, d = xs.shape
    d_sb = w_sb_out.shape[1]
    dc = w_conv_out.shape[1]
    d_in = w_in.shape[2] * N_CHIPS
    n_heads = d_sb // HEAD_DIM
    off_a = 3 * d_sb
    off_sb = off_a + 2 * dc
    off_cv = off_sb + d
    gw = dc
    assert off_a % dc == 0 and off_sb % gw == 0 and d % gw == 0 and d_in == off_cv + d
    chip = 2 * lax.axis_index("x") + lax.axis_index("y")

    big = dict(w_in=w_in[0], w_sb_out=w_sb_out[0], w_conv_out=w_conv_out[0], w_o=w_o[0], w_up=w_up[0], w_down=w_down[0])
    names = list(big)
    col_sharded = {"w_in", "w_sb_out", "w_conv_out", "w_up"}
    where = jnp.stack([chip, lax.axis_index("c")]).astype(jnp.int32)
    cs = w_dw.shape[2]
    taps4 = _gather_taps(_pad_rows(w_dw[0], CONV_HALO))
    first, token = _gather_start([_cast_into_block(big["w_in"], where, taps4, "cast_w_in")], taps4, "gather_start_in")
    rest, rest_token = _gather_start([_cast_into_block(big[n], where, token, "cast_" + n) for n in names[1:]], token,
                            "gather_start_rest")
    in_flight = dict(zip(names, zip(*[a + b for a, b in zip(first, rest)])))
    gathered = {}

    def land(group, after, tag):
        thru, sends, recvs = zip(*[in_flight[n] for n in group])
        landed = _gather_wait(thru, sends, recvs, after, "gather_wait_" + tag)
        return _phase_start(landed, _forward_copies, 3 * len(group), where, "forward_start_" + tag)

    def ready(group, started, after, tag):
        slabs, sends, recvs, _ = started
        for n, g4 in zip(group, _phase_wait(slabs, sends, recvs, _forward_copies, after, "forward_wait_" + tag)):
            gathered[n] = g4 if n in col_sharded else g4.reshape(g4.shape[0] * g4.shape[1], g4.shape[2])

    taps = jnp.transpose(taps4, (1, 0, 2)).reshape(CONV_HALO, N_CHIPS * cs)
    group_mix = ["w_sb_out", "w_conv_out", "w_o"]

    h = _rms_fwd(xs, g_pre_mix)
    started = land(["w_in"], rest_token, "in")
    ready(["w_in"], started, h, "in")
    proj = _mm(h, gathered["w_in"], name="mm_proj", b_groups=N_CHIPS, bias=b_in, tn=768)
    mix_fwd = land(group_mix, proj, "mix")
    attn = _attn_fwd(proj, mix_fwd[3], n_heads, d_sb)
    u0 = _glu(proj, off_a, dc)
    u1, u3 = _conv_fwd(u0, taps, b_dw, g_conv_ln, b_conv_ln)
    ready(group_mix, mix_fwd, u3, "mix")
    o_sb = _mm(attn, gathered["w_sb_out"], name="mm_o_sb", b_groups=N_CHIPS)
    up_fwd = land(["w_up"], o_sb, "up")
    o_cv = _mm(u3, gathered["w_conv_out"], name="mm_o_cv", b_groups=N_CHIPS, after=up_fwd[3])
    merged = _merge_fwd(proj, o_sb, o_cv, off_sb, off_cv, gw)
    y = _mm(merged, gathered["w_o"], name="mm_y")
    x2, h2 = _resid_rms(xs, y, g_post_mix, g_pre_mlp)
    ready(["w_up"], up_fwd, h2, "up")
    down_fwd = land(["w_down"], h2, "down")
    up, f = _mm(h2, gathered["w_up"], name="mm_up", b_groups=N_CHIPS, out_dtypes=(F32, BF16), after=down_fwd[3],
                epilogue=lambda acc: (acc, jnp.square(jnp.maximum(acc, 0.0))))
    ready(["w_down"], down_fwd, f, "down")
    dn = _mm(f, gathered["w_down"], name="mm_down")
    dx3, d_dn, dg_post_mlp, loss_part = _final(x2, dn, g_post_mlp, tgt)

    moments = dict(w_in=(m_w_in, v_w_in), w_sb_out=(m_w_sb_out, v_w_sb_out), w_conv_out=(m_w_conv_out, v_w_conv_out),
                   w_o=(m_w_o, v_w_o), w_up=(m_w_up, v_w_up), w_down=(m_w_down, v_w_down))
    out_g, out_d, out_m, out_v = {}, {}, {}, {}
    grads = {}

    def blocks_of(group):
        out = []
        for n in group:
            g = grads[n]
            out.append(g if g.ndim == 3 else g.reshape(N_CHIPS, g.shape[0] // N_CHIPS, g.shape[1]))
        return out

    def reduce_start(group, tag):
        blocks = blocks_of(group)
        theirs = _exchange_halves(blocks, "exchange_halves_" + tag)
        pair_sums = [_add_pair(g, t, where, "pair_sum_" + n) for n, g, t in zip(group, blocks, theirs)]
        return _scatter_start(pair_sums, where, "scatter_start_" + tag)

    def exchange_begin(group, tag):
        blocks = blocks_of(group)
        lands = [lax.empty((b.shape[0], b.shape[1] // 2, b.shape[2]), b.dtype) for b in blocks]
        return _phase_start(blocks + lands, _exchange_copies, len(group), where, "exchange_start_" + tag)

    def scatter_begin(group, tag, exchange, after):
        arrays, sends, recvs, _ = exchange
        arrays = _phase_wait(arrays, sends, recvs, _exchange_copies, after, "exchange_wait_" + tag)
        n = len(group)
        pair_sums = [_add_pair(g, t, where, "pair_sum_" + m) for m, g, t in zip(group, arrays[:n], arrays[n:])]
        return _scatter_start(pair_sums, where, "scatter_start_" + tag)

    def reduce_finish(group, tag, started, after):
        parts, slots = _scatter_wait(*started, after, "scatter_wait_" + tag)
        halves = [_sum_slots(p, t, where, "chip_sum_" + n) for n, p, t in zip(group, parts, slots)]
        for n, red in zip(group, _share_halves(halves, "share_halves_" + tag)):
            mm_, vv_ = moments[n]
            res = _adamw(big[n], red, mm_[0], vv_[0], "adamw_" + n)
            out_g[n], out_d[n], out_m[n], out_v[n] = [r[None] for r in res]
        return res[1]

    group_mlp, group_mix, group_in = ["w_down", "w_up"], ["w_o", "w_sb_out", "w_conv_out"], ["w_in"]
    grads["w_down"] = _mm(f, d_dn, name="mm_dw_down", ta=True, out_dtypes=(BF16,))
    dup = _mm(d_dn, gathered["w_down"], name="mm_df", tb=True, extra=[up], out_dtypes=(BF16,),
              epilogue=lambda acc, upv: (acc * (2.0 * jnp.maximum(upv, 0.0)),))
    grads["w_up"] = _mm(h2, dup, name="mm_dw_up", ta=True, out_groups=N_CHIPS, out_dtypes=(BF16,))
    mlp_exchange = exchange_begin(group_mlp, "mlp")
    dh2 = _mm(dup, gathered["w_up"], name="mm_dh2", tb=True, b_groups=N_CHIPS, after=mlp_exchange[3])
    dx2, dy, dg_pre_mlp, dg_post_mix = _rms_bwd2(dx3, dh2, x2, y, g_post_mix, g_pre_mlp)
    mlp_started, mlp_token = scatter_begin(group_mlp, "mlp", mlp_exchange, dx2)
    grads["w_o"] = _mm(merged, dy, name="mm_dw_o", ta=True, out_dtypes=(BF16,), after=mlp_token)
    dmerged = _mm(dy, gathered["w_o"], name="mm_dmerged", tb=True)
    do_sb, do_cv, dgate_sb, dgate_cv, s_gate_sb, s_gate_cv = _merge_bwd(proj, o_sb, o_cv, dmerged, off_sb, off_cv, gw)
    grads["w_sb_out"] = _mm(attn, do_sb, name="mm_dw_sb", ta=True, out_groups=N_CHIPS, out_dtypes=(BF16,))
    dattn = _mm(do_sb, gathered["w_sb_out"], name="mm_dattn", tb=True, b_groups=N_CHIPS)
    grads["w_conv_out"] = _mm(u3, do_cv, name="mm_dw_cv", ta=True, out_groups=N_CHIPS, out_dtypes=(BF16,))
    mix_exchange = exchange_begin(group_mix, "mix")
    du3 = _mm(do_cv, gathered["w_conv_out"], name="mm_du3", tb=True, b_groups=N_CHIPS, after=mix_exchange[3])
    mix_started, mix_token = scatter_begin(group_mix, "mix", mix_exchange, du3)
    dq, dk, dv, s_q, s_k, s_v = _attn_bwd(proj, dattn, mix_token, n_heads, d_sb)
    du1, dg_conv_ln, db_conv_ln, db_dw = _conv_bwd_norm(u1, du3, g_conv_ln, b_conv_ln)
    dglu_a, dglu_b, s_a, s_b, dtaps = _conv_bwd_taps(du1, u0, proj, taps, off_a)
    dproj = jnp.concatenate([dq, dk, dv, dglu_a, dglu_b, dgate_sb, dgate_cv], axis=1)
    db_in = jnp.concatenate([s_q, s_k, s_v, s_a, s_b, s_gate_sb, s_gate_cv], axis=1)
    small_w = dict(g_pre_mix=g_pre_mix, b_in=b_in, b_dw=b_dw, g_conv_ln=g_conv_ln, b_conv_ln=b_conv_ln,
                   g_post_mix=g_post_mix, g_pre_mlp=g_pre_mlp, g_post_mlp=g_post_mlp)
    small_m = dict(g_pre_mix=m_g_pre_mix, b_in=m_b_in, b_dw=m_b_dw, g_conv_ln=m_g_conv_ln, b_conv_ln=m_b_conv_ln,
                   g_post_mix=m_g_post_mix, g_pre_mlp=m_g_pre_mlp, g_post_mlp=m_g_post_mlp)
    small_v = dict(g_pre_mix=v_g_pre_mix, b_in=v_b_in, b_dw=v_b_dw, g_conv_ln=v_g_conv_ln, b_conv_ln=v_b_conv_ln,
                   g_post_mix=v_g_post_mix, g_pre_mlp=v_g_pre_mlp, g_post_mlp=v_g_post_mlp)
    small_g = dict(b_in=db_in, b_dw=db_dw, g_conv_ln=dg_conv_ln, b_conv_ln=db_conv_ln,
                   g_post_mix=dg_post_mix, g_pre_mlp=dg_pre_mlp, g_post_mlp=dg_post_mlp)

    def small_update(group, tails, after, tag):
        shapes = [small_w[n].shape for n in group] + [t.shape for t in tails]
        summed = _all_reduce_small(_pack([small_g[n] for n in group] + tails), after, "all_reduce_small_" + tag)
        zeros_tail = [jnp.zeros(t.shape, F32) for t in tails]
        res = _adamw(_pack([small_w[n] for n in group] + zeros_tail), summed,
                     _pack([small_m[n] for n in group] + zeros_tail), _pack([small_v[n] for n in group] + zeros_tail),
                     "adamw_small_" + tag)
        unpacked = [_unpack(r, shapes) for r in res]
        for i, n in enumerate(group):
            out_g[n], out_d[n], out_m[n], out_v[n] = [u[i] for u in unpacked]
        return summed, unpacked[0][len(group):]

    early = ["b_in", "b_dw", "g_conv_ln", "b_conv_ln", "g_post_mix", "g_pre_mlp", "g_post_mlp"]
    summed, (loss_sum, taps_sum) = small_update(early, [loss_part, dtaps], dproj, "early")
    loss = loss_sum[0, 0]
    taps_grad = lax.dynamic_slice(taps_sum, (0, chip * cs), (CONV_HALO, cs))
    res = _adamw(_pad_rows(w_dw[0], CONV_HALO), taps_grad, _pad_rows(m_w_dw[0], CONV_HALO), _pad_rows(v_w_dw[0], CONV_HALO),
                 "adamw_taps")
    out_g["w_dw"], out_d["w_dw"], out_m["w_dw"], out_v["w_dw"] = [r[:CONV_WIDTH][None] for r in res]
    grads["w_in"] = _mm(h, dproj, name="mm_dw_in", ta=True, out_groups=N_CHIPS, out_dtypes=(BF16,), tn=768, after=summed)
    in_started, in_token = reduce_start(group_in, "in")
    dh = _mm(dproj, gathered["w_in"], name="mm_dh", tb=True, b_groups=N_CHIPS, tk=2304, after=in_token)
    grad_x, dg_pre_mix = _rms_bwd1(dx2, dh, xs, g_pre_mix)
    done = reduce_finish(group_mlp, "mlp", mlp_started, grad_x)
    done = reduce_finish(group_mix, "mix", mix_started, done)
    small_g["g_pre_mix"] = dg_pre_mix
    summed, _ = small_update(["g_pre_mix"], [], done, "late")
    reduce_finish(group_in, "in", in_started, summed)

    order = ["g_pre_mix", "w_in", "b_in", "w_dw", "b_dw", "g_conv_ln", "b_conv_ln", "w_sb_out", "w_conv_out", "w_o",
             "g_post_mix", "g_pre_mlp", "w_up", "w_down", "g_post_mlp"]
    return (loss, grad_x[None], *[out_g[n] for n in order], *[out_d[n] for n in order],
            *[out_m[n] for n in order], *[out_v[n] for n in order])
```

```python
import functools
import math

import jax
import jax.numpy as jnp
from jax import lax
from jax.experimental import pallas as pl
from jax.experimental.pallas import tpu as pltpu

F32 = jnp.float32
BF16 = jnp.bfloat16
MESH = pl.DeviceIdType.MESH

HEAD_DIM = 128
CONV_WIDTH = 31
CONV_HALO = 32
EPS = 1e-6
ADAM_LR = 0.001
ADAM_B1 = 0.9
ADAM_B2 = 0.999
ADAM_EPS = 1e-08
ADAM_WD = 0.01
ADAM_STEP = 10
N_CHIPS = 4
VMEM_LIMIT = 48 * 1024 * 1024


def _params(**kw):
    return pltpu.CompilerParams(vmem_limit_bytes=VMEM_LIMIT, **kw)


def _tile(n, want, align=128):
    if n <= want:
        return n
    for t in range(want - want % align, 0, -align):
        if n % t == 0:
            return t
    raise ValueError((n, want, align))


def _mm(a, b, *, name, ta=False, tb=False, b_groups=1, out_groups=1, bias=None, extra=None, epilogue=None,
        out_dtypes=(F32,), tm=1024, tn=1024, tk=2048, after=None):
    if ta:
        K, M = a.shape
    else:
        M, K = a.shape
    if b_groups == 1:
        br, bc = b.shape
    else:
        _, br, bcg = b.shape
        bc = bcg * b_groups
    if tb:
        N, Kb = br, bc
    else:
        Kb, N = br, bc
    assert Kb == K, (name, a.shape, b.shape)
    tm, tn, tk = _tile(M, tm), _tile(N, tn), _tile(K, tk)
    if b_groups > 1:
        if tb:
            tk = _tile(K // b_groups, tk)
        else:
            tn = _tile(N // b_groups, tn)
    if out_groups > 1:
        tn = _tile(N // out_groups, tn)
        if b_groups > 1 and not tb:
            tn = _tile(N // b_groups, tn)
    nm, nn, nk = M // tm, N // tn, K // tk
    extra = tuple(extra or ())
    n_extra = len(extra)
    has_bias = bias is not None
    n_out = len(out_dtypes)

    a_spec = pl.BlockSpec((tk, tm), lambda i, j, k: (k, i)) if ta else pl.BlockSpec((tm, tk), lambda i, j, k: (i, k))
    if b_groups == 1:
        b_spec = pl.BlockSpec((tn, tk), lambda i, j, k: (j, k)) if tb else pl.BlockSpec((tk, tn), lambda i, j, k: (k, j))
    elif tb:
        kpg = (K // b_groups) // tk
        b_spec = pl.BlockSpec((None, tn, tk), lambda i, j, k: (k // kpg, j, k % kpg))
    else:
        npg = (N // b_groups) // tn
        b_spec = pl.BlockSpec((None, tk, tn), lambda i, j, k: (j // npg, k, j % npg))
    in_specs = [a_spec, b_spec]
    operands = [a, b]
    if has_bias:
        in_specs.append(pl.BlockSpec((1, tn), lambda i, j, k: (0, j)))
        operands.append(bias)
    for e in extra:
        in_specs.append(pl.BlockSpec((tm, tn), lambda i, j, k: (i, j)))
        operands.append(e)
    n_after = 0 if after is None else 1
    if after is not None:
        in_specs.append(pl.BlockSpec(memory_space=pl.ANY))
        operands.append(after)
    if out_groups == 1:
        o_spec = pl.BlockSpec((tm, tn), lambda i, j, k: (i, j))
        o_shape = (M, N)
    else:
        opg = (N // out_groups) // tn
        o_spec = pl.BlockSpec((None, tm, tn), lambda i, j, k: (j // opg, i, j % opg))
        o_shape = (out_groups, M, N // out_groups)
    dims = (((0 if ta else 1,), (1 if tb else 0,)), ((), ()))

    def body(*refs):
        a_ref, b_ref = refs[0], refs[1]
        pos = 2
        bias_ref = None
        if has_bias:
            bias_ref = refs[pos]
            pos += 1
        extra_refs = refs[pos:pos + n_extra]
        pos += n_extra + n_after
        out_refs = refs[pos:pos + n_out]
        pos += n_out
        acc_ref = refs[pos] if nk > 1 else None

        part = lax.dot_general(a_ref[...].astype(BF16), b_ref[...].astype(BF16), dims, preferred_element_type=F32)

        def finish(acc):
            if has_bias:
                acc = acc + bias_ref[...]
            outs = epilogue(acc, *[e[...] for e in extra_refs]) if epilogue is not None else (acc,)
            for o_ref, o in zip(out_refs, outs):
                o_ref[...] = o.astype(o_ref.dtype)

        if nk == 1:
            finish(part)
        else:
            k = pl.program_id(2)

            @pl.when(k == 0)
            def _():
                acc_ref[...] = part

            @pl.when(k > 0)
            def _():
                acc_ref[...] += part

            @pl.when(k == nk - 1)
            def _():
                finish(acc_ref[...])

    outs = pl.pallas_call(
        body,
        name=name,
        grid=(nm, nn, nk),
        in_specs=in_specs,
        out_specs=[o_spec] * n_out,
        out_shape=[jax.ShapeDtypeStruct(o_shape, dt) for dt in out_dtypes],
        scratch_shapes=[pltpu.VMEM((tm, tn), F32)] if nk > 1 else [],
        compiler_params=_params(dimension_semantics=("parallel", "parallel", "arbitrary")),
    )(*operands)
    return outs[0] if n_out == 1 else outs


def _rms_stats(x):
    return lax.rsqrt(jnp.mean(x * x, axis=-1, keepdims=True) + EPS)


def _rms_bwd(x, r, g, dy):
    gy = dy * g
    return r * gy - x * (r * r * r) * jnp.mean(x * gy, axis=-1, keepdims=True)


def _sigmoid(x):
    return 1.0 / (1.0 + jnp.exp(-x))


def _row_call(body, *, name, rows, tr, ins, outs, acc_outs=()):
    n = rows // tr
    in_specs = []
    for arr, kind in ins:
        if kind == "row":
            in_specs.append(pl.BlockSpec((tr, arr.shape[1]), lambda i: (i, 0)))
        else:
            in_specs.append(pl.BlockSpec(arr.shape, lambda i: (0, 0)))
    out_specs = [pl.BlockSpec((tr, w), lambda i: (i, 0)) for w, _ in outs]
    out_shape = [jax.ShapeDtypeStruct((rows, w), dt) for w, dt in outs]
    out_specs += [pl.BlockSpec((1, w), lambda i: (0, 0)) for w in acc_outs]
    out_shape += [jax.ShapeDtypeStruct((1, w), F32) for w in acc_outs]
    return pl.pallas_call(
        body, name=name, grid=(n,), in_specs=in_specs, out_specs=out_specs, out_shape=out_shape,
        compiler_params=_params(dimension_semantics=("arbitrary",)),
    )(*[a for a, _ in ins])


def _accumulate(ref, val):
    @pl.when(pl.program_id(0) == 0)
    def _():
        ref[...] = val

    @pl.when(pl.program_id(0) > 0)
    def _():
        ref[...] += val


def _rms_fwd(x, g, tr=256):
    def body(x_ref, g_ref, h_ref):
        xv = x_ref[...]
        h_ref[...] = (xv * _rms_stats(xv) * g_ref[...]).astype(BF16)

    (h,) = _row_call(body, name="rms_fwd", rows=x.shape[0], tr=tr, ins=[(x, "row"), (g, "vec")], outs=[(x.shape[1], BF16)])
    return h


def _resid_rms(x, y, g2, g3, tr=256):
    def body(x_ref, y_ref, g2_ref, g3_ref, x2_ref, h2_ref):
        yv = y_ref[...]
        x2 = x_ref[...] + yv * _rms_stats(yv) * g2_ref[...]
        x2_ref[...] = x2
        h2_ref[...] = (x2 * _rms_stats(x2) * g3_ref[...]).astype(BF16)

    d = x.shape[1]
    return _row_call(body, name="resid_rms", rows=x.shape[0], tr=tr,
                     ins=[(x, "row"), (y, "row"), (g2, "vec"), (g3, "vec")], outs=[(d, F32), (d, BF16)])


def _final(x2, dn, g4, tgt, tr=256):
    d = x2.shape[1]

    def body(x2_ref, dn_ref, g4_ref, t_ref, dx3_ref, ddn_ref, dg4_ref, loss_ref):
        dn_v = dn_ref[...]
        r = _rms_stats(dn_v)
        g = g4_ref[...]
        e = x2_ref[...] + dn_v * r * g - t_ref[...]
        dx3 = e * (1.0 / d)
        dx3_ref[...] = dx3
        ddn_ref[...] = _rms_bwd(dn_v, r, g, dx3).astype(BF16)
        _accumulate(dg4_ref, jnp.sum(dx3 * dn_v * r, axis=0, keepdims=True))
        part = 0.5 * jnp.sum(jnp.mean(e * e, axis=-1, keepdims=True), axis=0, keepdims=True)
        _accumulate(loss_ref, jnp.broadcast_to(part, loss_ref.shape))

    return _row_call(body, name="final", rows=x2.shape[0], tr=tr,
                     ins=[(x2, "row"), (dn, "row"), (g4, "vec"), (tgt, "row")],
                     outs=[(d, F32), (d, BF16)], acc_outs=(d, 128))


def _rms_bwd2(dx3, dh2, x2, y, g2, g3, tr=256):
    d = x2.shape[1]

    def body(dx3_ref, dh2_ref, x2_ref, y_ref, g2_ref, g3_ref, dx2_ref, dy_ref, dg3_ref, dg2_ref):
        x2v, dh2v, yv = x2_ref[...], dh2_ref[...], y_ref[...]
        r3 = _rms_stats(x2v)
        dx2 = dx3_ref[...] + _rms_bwd(x2v, r3, g3_ref[...], dh2v)
        dx2_ref[...] = dx2
        _accumulate(dg3_ref, jnp.sum(dh2v * x2v * r3, axis=0, keepdims=True))
        r2 = _rms_stats(yv)
        dy_ref[...] = _rms_bwd(yv, r2, g2_ref[...], dx2).astype(BF16)
        _accumulate(dg2_ref, jnp.sum(dx2 * yv * r2, axis=0, keepdims=True))

    return _row_call(body, name="rms_bwd2", rows=x2.shape[0], tr=tr,
                     ins=[(dx3, "row"), (dh2, "row"), (x2, "row"), (y, "row"), (g2, "vec"), (g3, "vec")],
                     outs=[(d, F32), (d, BF16)], acc_outs=(d, d))


def _rms_bwd1(dx2, dh, x, g1, tr=256):
    d = x.shape[1]

    def body(dx2_ref, dh_ref, x_ref, g1_ref, gx_ref, dg1_ref):
        xv, dhv = x_ref[...], dh_ref[...]
        r = _rms_stats(xv)
        gx_ref[...] = dx2_ref[...] + _rms_bwd(xv, r, g1_ref[...], dhv)
        _accumulate(dg1_ref, jnp.sum(dhv * xv * r, axis=0, keepdims=True))

    return _row_call(body, name="rms_bwd1", rows=x.shape[0], tr=tr,
                     ins=[(dx2, "row"), (dh, "row"), (x, "row"), (g1, "vec")], outs=[(d, F32)], acc_outs=(d,))


def _merge_fwd(proj, o_sb, o_cv, off_sb, off_cv, gw, tr=256):
    s, d = o_sb.shape
    nj = d // gw
    b_sb, b_cv = off_sb // gw, off_cv // gw

    def body(gs_ref, gc_ref, osb_ref, ocv_ref, m_ref):
        m_ref[...] = (_sigmoid(gs_ref[...]) * osb_ref[...] + _sigmoid(gc_ref[...]) * ocv_ref[...]).astype(BF16)

    blk = lambda i, j: (i, j)
    return pl.pallas_call(
        body, name="merge_fwd", grid=(s // tr, nj),
        in_specs=[pl.BlockSpec((tr, gw), lambda i, j: (i, b_sb + j)), pl.BlockSpec((tr, gw), lambda i, j: (i, b_cv + j)),
                  pl.BlockSpec((tr, gw), blk), pl.BlockSpec((tr, gw), blk)],
        out_specs=pl.BlockSpec((tr, gw), blk), out_shape=jax.ShapeDtypeStruct((s, d), BF16),
        compiler_params=_params(dimension_semantics=("parallel", "parallel")),
    )(proj, proj, o_sb, o_cv)


def _merge_bwd(proj, o_sb, o_cv, dmerged, off_sb, off_cv, gw, tr=256):
    s, d = o_sb.shape
    nj = d // gw
    ni = s // tr
    b_sb, b_cv = off_sb // gw, off_cv // gw

    def body(gs_ref, gc_ref, osb_ref, ocv_ref, dm_ref, dosb_ref, docv_ref, dgs_ref, dgc_ref, sgs_ref, sgc_ref):
        dm = dm_ref[...]
        s_sb, s_cv = _sigmoid(gs_ref[...]), _sigmoid(gc_ref[...])
        dosb_ref[...] = (dm * s_sb).astype(BF16)
        docv_ref[...] = (dm * s_cv).astype(BF16)
        dgs = dm * osb_ref[...] * s_sb * (1.0 - s_sb)
        dgc = dm * ocv_ref[...] * s_cv * (1.0 - s_cv)
        dgs_ref[...] = dgs.astype(BF16)
        dgc_ref[...] = dgc.astype(BF16)
        i = pl.program_id(1)
        for ref, val in ((sgs_ref, dgs), (sgc_ref, dgc)):
            col = jnp.sum(val, axis=0, keepdims=True)

            @pl.when(i == 0)
            def _():
                ref[...] = col

            @pl.when(i > 0)
            def _():
                ref[...] += col

    blk = lambda j, i: (i, j)
    outs = pl.pallas_call(
        body, name="merge_bwd", grid=(nj, ni),
        in_specs=[pl.BlockSpec((tr, gw), lambda j, i: (i, b_sb + j)), pl.BlockSpec((tr, gw), lambda j, i: (i, b_cv + j)),
                  pl.BlockSpec((tr, gw), blk), pl.BlockSpec((tr, gw), blk), pl.BlockSpec((tr, gw), blk)],
        out_specs=[pl.BlockSpec((tr, gw), blk), pl.BlockSpec((tr, gw), blk),
                   pl.BlockSpec((tr, gw), blk), pl.BlockSpec((tr, gw), blk),
                   pl.BlockSpec((1, gw), lambda j, i: (0, j)), pl.BlockSpec((1, gw), lambda j, i: (0, j))],
        out_shape=[jax.ShapeDtypeStruct((s, d), BF16)] * 4 + [jax.ShapeDtypeStruct((1, d), F32)] * 2,
        compiler_params=_params(dimension_semantics=("parallel", "arbitrary")),
    )(proj, proj, o_sb, o_cv, dmerged)
    return outs


def _split_bf16(v):
    hi = v.astype(BF16)
    lo = (v - hi.astype(F32)).astype(BF16)
    return hi, lo


def _dot_nt(a, b):
    return lax.dot_general(a, b, (((1,), (1,)), ((), ())), preferred_element_type=F32)


def _dot_tn(a, b):
    return lax.dot_general(a, b, (((0,), (0,)), ((), ())), preferred_element_type=F32)


def _dot_nn(a, b):
    return lax.dot_general(a, b, (((1,), (0,)), ((), ())), preferred_element_type=F32)


def _sum_right(v, tri):
    hi, lo = _split_bf16(v)
    return _dot_nn(hi, tri) + _dot_nn(lo, tri)


HEADS_PER_STEP = 2


LOG2_E = 1.4426950408889634


def _sb_tile(q, kb, scale, mask):
    z = _dot_nt(q, kb) * (scale * LOG2_E)
    log_b = jnp.minimum(z, 0.0) - jnp.log2(1.0 + jnp.exp2(-jnp.abs(z)))
    log_1m = log_b - z
    if mask is not None:
        log_1m = jnp.where(mask, log_1m, 0.0)
    return log_b, log_1m


def _tile_mask(tq):
    row = lax.broadcasted_iota(jnp.int32, (tq, tq), 0)
    col = lax.broadcasted_iota(jnp.int32, (tq, tq), 1)
    return col < row


def _tri(tq, before):
    r = lax.broadcasted_iota(jnp.int32, (tq, tq), 0)
    c = lax.broadcasted_iota(jnp.int32, (tq, tq), 1)
    return jnp.where((r < c) if before else (r > c), 1.0, 0.0).astype(BF16)


def _attn_fwd(proj, after, n_heads, d_sb, tq=256):
    s = proj.shape[0]
    tq = _tile(s, tq)
    hb = d_sb // HEAD_DIM
    scale = 1.0 / math.sqrt(HEAD_DIM)

    heads = [pl.ds(n * HEAD_DIM, HEAD_DIM) for n in range(HEADS_PER_STEP)]
    wide = HEADS_PER_STEP * HEAD_DIM
    hb //= HEADS_PER_STEP

    def body(q_ref, k_ref, v_ref, _, o_ref):
        i = pl.program_id(1)
        tri = _tri(tq, False)

        def tile(j, carry, diagonal):
            rows = pl.ds(pl.multiple_of(j * tq, tq), tq)
            mask = _tile_mask(tq) if diagonal else None
            out = []
            for hd, (c_l, acc) in zip(heads, carry):
                log_b, log_1m = _sb_tile(q_ref[:, hd].astype(BF16), k_ref[rows, hd].astype(BF16), scale, mask)
                a = jnp.exp2(log_b + (c_l + _sum_right(log_1m, tri)))
                if diagonal:
                    a = jnp.where(mask, a, 0.0)
                out.append((c_l + jnp.sum(log_1m, axis=1, keepdims=True),
                            acc + _dot_nn(a.astype(BF16), v_ref[rows, hd].astype(BF16))))
            return tuple(out)

        init = tuple((jnp.zeros((tq, 1), F32), jnp.zeros((tq, HEAD_DIM), F32)) for _ in heads)
        carry = lax.fori_loop(0, i, lambda jj, c: tile(i - 1 - jj, c, False), tile(i, init, True))
        for hd, (_, acc) in zip(heads, carry):
            o_ref[:, hd] = acc.astype(BF16)

    return pl.pallas_call(
        body, name="attn_fwd", grid=(n_heads // HEADS_PER_STEP, s // tq),
        in_specs=[pl.BlockSpec((tq, wide), lambda h, i: (i, h)),
                  pl.BlockSpec((s, wide), lambda h, i: (0, hb + h)),
                  pl.BlockSpec((s, wide), lambda h, i: (0, 2 * hb + h)), pl.BlockSpec(memory_space=pl.ANY)],
        out_specs=pl.BlockSpec((tq, wide), lambda h, i: (i, h)),
        out_shape=jax.ShapeDtypeStruct((s, d_sb), BF16),
        compiler_params=_params(dimension_semantics=("parallel", "arbitrary")),
    )(proj, proj, proj, after)


def _attn_bwd(proj, dattn, after, n_heads, d_sb, tq=256):
    s = proj.shape[0]
    tq = _tile(s, tq)
    nq = s // tq
    hb = d_sb // HEAD_DIM // HEADS_PER_STEP
    scale = 1.0 / math.sqrt(HEAD_DIM)
    heads = [pl.ds(n * HEAD_DIM, HEAD_DIM) for n in range(HEADS_PER_STEP)]
    wide = HEADS_PER_STEP * HEAD_DIM

    def body(q_ref, k_ref, v_ref, do_ref, _, dq_ref, dk_ref, dv_ref, sq_ref, sk_ref, sv_ref, dk_acc, dv_acc, g_st, b_st):
        i = pl.program_id(1)

        @pl.when(i == 0)
        def _():
            dk_acc[...] = jnp.zeros_like(dk_acc)
            dv_acc[...] = jnp.zeros_like(dv_acc)

        tri_after = _tri(tq, False)
        tri_before = _tri(tq, True)

        def newest_first(j, c_ls, diagonal):
            rows = pl.ds(pl.multiple_of(j * tq, tq), tq)
            mask = _tile_mask(tq) if diagonal else None
            out = []
            for n, (hd, c_l) in enumerate(zip(heads, c_ls)):
                do = do_ref[:, hd].astype(BF16)
                log_b, log_1m = _sb_tile(q_ref[:, hd].astype(BF16), k_ref[rows, hd].astype(BF16), scale, mask)
                a = jnp.exp2(log_b + (c_l + _sum_right(log_1m, tri_after)))
                beta = jnp.exp2(log_b)
                if diagonal:
                    a = jnp.where(mask, a, 0.0)
                    beta = jnp.where(mask, beta, 0.0)
                g_st[n, j] = a * _dot_nt(do, v_ref[rows, hd].astype(BF16))
                b_st[n, j] = beta
                dv_acc[rows, hd] += _dot_tn(a.astype(BF16), do)
                out.append(c_l + jnp.sum(log_1m, axis=1, keepdims=True))
            return tuple(out)

        lax.fori_loop(0, i, lambda jj, c: newest_first(i - 1 - jj, c, False),
                      newest_first(i, tuple(jnp.zeros((tq, 1), F32) for _ in heads), True))

        def oldest_first(j, carry):
            rows = pl.ds(pl.multiple_of(j * tq, tq), tq)
            out = []
            for n, (hd, (c_g, dq)) in enumerate(zip(heads, carry)):
                kb = k_ref[rows, hd].astype(BF16)
                g = g_st[n, j]
                beta = b_st[n, j]
                g_before = c_g + _sum_right(g, tri_before)
                dz16 = ((g * (1.0 - beta) - g_before * beta) * scale).astype(BF16)
                dk_acc[rows, hd] += _dot_tn(dz16, q_ref[:, hd].astype(BF16))
                out.append((c_g + jnp.sum(g, axis=1, keepdims=True), dq + _dot_nn(dz16, kb)))
            return tuple(out)

        init = tuple((jnp.zeros((tq, 1), F32), jnp.zeros((tq, HEAD_DIM), F32)) for _ in heads)
        carry = lax.fori_loop(0, i + 1, oldest_first, init)
        dq = jnp.concatenate([c[1] for c in carry], axis=1)
        dq_ref[...] = dq.astype(BF16)
        col = jnp.sum(dq, axis=0, keepdims=True)

        @pl.when(i == 0)
        def _():
            sq_ref[...] = col

        @pl.when(i > 0)
        def _():
            sq_ref[...] += col

        @pl.when(i == nq - 1)
        def _():
            dk = dk_acc[...]
            dv = dv_acc[...]
            dk_ref[...] = dk.astype(BF16)
            dv_ref[...] = dv.astype(BF16)
            sk_ref[...] = jnp.sum(dk, axis=0, keepdims=True)
            sv_ref[...] = jnp.sum(dv, axis=0, keepdims=True)

    qblk = pl.BlockSpec((tq, wide), lambda h, i: (i, h))
    kblk = pl.BlockSpec((s, wide), lambda h, i: (0, h))
    col_sum = pl.BlockSpec((1, wide), lambda h, i: (0, h))
    stash = pltpu.VMEM((HEADS_PER_STEP, nq, tq, tq), F32)
    outs = pl.pallas_call(
        body, name="attn_bwd", grid=(n_heads // HEADS_PER_STEP, nq),
        in_specs=[qblk,
                  pl.BlockSpec((s, wide), lambda h, i: (0, hb + h)),
                  pl.BlockSpec((s, wide), lambda h, i: (0, 2 * hb + h)),
                  qblk, pl.BlockSpec(memory_space=pl.ANY)],
        out_specs=[qblk, kblk, kblk, col_sum, col_sum, col_sum],
        out_shape=[jax.ShapeDtypeStruct((s, d_sb), BF16)] * 3 + [jax.ShapeDtypeStruct((1, d_sb), F32)] * 3,
        scratch_shapes=[pltpu.VMEM((s, wide), F32), pltpu.VMEM((s, wide), F32), stash, stash],
        compiler_params=_params(dimension_semantics=("parallel", "arbitrary")),
    )(proj, proj, proj, dattn, after)
    return outs


LANES = 128


def _glu(proj, off_a, dc, tr=256):
    s = proj.shape[0]
    ba = off_a // dc

    def body(a_ref, b_ref, u_ref):
        u_ref[...] = a_ref[...] * _sigmoid(b_ref[...])

    return pl.pallas_call(
        body, name="glu", grid=(s // tr,),
        in_specs=[pl.BlockSpec((tr, dc), lambda i: (i, ba)), pl.BlockSpec((tr, dc), lambda i: (i, ba + 1))],
        out_specs=pl.BlockSpec((tr, dc), lambda i: (i, 0)), out_shape=jax.ShapeDtypeStruct((s, dc), F32),
        compiler_params=_params(dimension_semantics=("parallel",)),
    )(proj, proj)


def _conv_fwd(u0, taps, b_dw, g_ln, b_ln, tt=256):
    s, dc = u0.shape
    tt = _tile(s, tt)
    hpb = tt // CONV_HALO

    def body(cur_ref, halo_ref, w_ref, bdw_ref, g_ref, b_ref, u1_ref, u3_ref, xs):
        i = pl.program_id(0)
        xs[pl.ds(0, CONV_HALO), :] = jnp.where(i > 0, halo_ref[...], 0.0)
        xs[pl.ds(CONV_HALO, tt), :] = cur_ref[...]
        for c0 in range(0, dc, LANES):
            cols = pl.ds(c0, LANES)
            acc = jnp.broadcast_to(bdw_ref[:, cols], (tt, LANES))
            for k in range(CONV_WIDTH):
                acc = acc + w_ref[pl.ds(k, 1), cols] * xs[pl.ds(CONV_HALO - (CONV_WIDTH - 1) + k, tt), cols]
            u1_ref[:, cols] = acc
        u1 = u1_ref[...]
        mu = jnp.mean(u1, axis=-1, keepdims=True)
        xc = u1 - mu
        rstd = lax.rsqrt(jnp.mean(xc * xc, axis=-1, keepdims=True) + EPS)
        u2 = xc * rstd * g_ref[...] + b_ref[...]
        u3_ref[...] = (u2 * _sigmoid(u2)).astype(BF16)

    vec = pl.BlockSpec((1, dc), lambda i: (0, 0))
    return pl.pallas_call(
        body, name="conv_fwd", grid=(s // tt,),
        in_specs=[pl.BlockSpec((tt, dc), lambda i: (i, 0)),
                  pl.BlockSpec((CONV_HALO, dc), lambda i: (jnp.maximum(i * hpb - 1, 0), 0)),
                  pl.BlockSpec((CONV_HALO, dc), lambda i: (0, 0)), vec, vec, vec],
        out_specs=[pl.BlockSpec((tt, dc), lambda i: (i, 0)), pl.BlockSpec((tt, dc), lambda i: (i, 0))],
        out_shape=[jax.ShapeDtypeStruct((s, dc), F32), jax.ShapeDtypeStruct((s, dc), BF16)],
        scratch_shapes=[pltpu.VMEM((tt + CONV_HALO, dc), F32)],
        compiler_params=_params(dimension_semantics=("arbitrary",)),
    )(u0, u0, taps, b_dw, g_ln, b_ln)


def _conv_bwd_norm(u1, du3, g_ln, b_ln, tr=256):
    dc = u1.shape[1]

    def body(u1_ref, du3_ref, g_ref, b_ref, du1_ref, dg_ref, db_ref, dbdw_ref):
        u1v = u1_ref[...]
        g = g_ref[...]
        mu = jnp.mean(u1v, axis=-1, keepdims=True)
        xc = u1v - mu
        rstd = lax.rsqrt(jnp.mean(xc * xc, axis=-1, keepdims=True) + EPS)
        xhat = xc * rstd
        u2 = xhat * g + b_ref[...]
        sg = _sigmoid(u2)
        du2 = du3_ref[...] * (sg * (1.0 + u2 * (1.0 - sg)))
        _accumulate(dg_ref, jnp.sum(du2 * xhat, axis=0, keepdims=True))
        _accumulate(db_ref, jnp.sum(du2, axis=0, keepdims=True))
        gy = du2 * g
        du1 = rstd * (gy - jnp.mean(gy, axis=-1, keepdims=True) - xhat * jnp.mean(gy * xhat, axis=-1, keepdims=True))
        du1_ref[...] = du1
        _accumulate(dbdw_ref, jnp.sum(du1, axis=0, keepdims=True))

    return _row_call(body, name="conv_bwd_norm", rows=u1.shape[0], tr=tr,
                     ins=[(u1, "row"), (du3, "row"), (g_ln, "vec"), (b_ln, "vec")],
                     outs=[(dc, F32)], acc_outs=(dc, dc, dc))


def _conv_bwd_taps(du1, u0, proj, taps, off_a, tt=256):
    s, dc = u0.shape
    tt = _tile(s, tt)
    n = s // tt
    hpb = tt // CONV_HALO
    ba = off_a // dc
    lead = CONV_HALO - (CONV_WIDTH - 1)

    def body(d_ref, dnext_ref, cur_ref, halo_ref, a_ref, b_ref, w_ref, da_ref, db_ref, sa_ref, sb_ref, dw_ref,
             xs, ds, du0):
        i = pl.program_id(0)
        xs[pl.ds(0, CONV_HALO), :] = jnp.where(i > 0, halo_ref[...], 0.0)
        xs[pl.ds(CONV_HALO, tt), :] = cur_ref[...]
        ds[pl.ds(0, tt), :] = d_ref[...]
        ds[pl.ds(tt, CONV_HALO), :] = jnp.where(i < n - 1, dnext_ref[...], 0.0)

        @pl.when(i == 0)
        def _():
            dw_ref[...] = jnp.zeros_like(dw_ref)

        for c0 in range(0, dc, LANES):
            cols = pl.ds(c0, LANES)
            d_cur = ds[pl.ds(0, tt), cols]
            acc = jnp.zeros((tt, LANES), F32)
            for k in range(CONV_WIDTH):
                acc = acc + w_ref[pl.ds(k, 1), cols] * ds[pl.ds(CONV_WIDTH - 1 - k, tt), cols]
                dw_ref[pl.ds(k, 1), cols] += jnp.sum(d_cur * xs[pl.ds(lead + k, tt), cols], axis=0, keepdims=True)
            du0[:, cols] = acc
        d0 = du0[...]
        sg = _sigmoid(b_ref[...])
        da = d0 * sg
        db = d0 * a_ref[...] * sg * (1.0 - sg)
        da_ref[...] = da.astype(BF16)
        db_ref[...] = db.astype(BF16)
        _accumulate(sa_ref, jnp.sum(da, axis=0, keepdims=True))
        _accumulate(sb_ref, jnp.sum(db, axis=0, keepdims=True))

    row = pl.BlockSpec((tt, dc), lambda i: (i, 0))
    vec = pl.BlockSpec((1, dc), lambda i: (0, 0))
    taps_spec = pl.BlockSpec((CONV_HALO, dc), lambda i: (0, 0))
    return pl.pallas_call(
        body, name="conv_bwd_taps", grid=(n,),
        in_specs=[row, pl.BlockSpec((CONV_HALO, dc), lambda i: (jnp.minimum((i + 1) * hpb, n * hpb - 1), 0)),
                  row, pl.BlockSpec((CONV_HALO, dc), lambda i: (jnp.maximum(i * hpb - 1, 0), 0)),
                  pl.BlockSpec((tt, dc), lambda i: (i, ba)), pl.BlockSpec((tt, dc), lambda i: (i, ba + 1)), taps_spec],
        out_specs=[row, row, vec, vec, taps_spec],
        out_shape=[jax.ShapeDtypeStruct((s, dc), BF16), jax.ShapeDtypeStruct((s, dc), BF16),
                   jax.ShapeDtypeStruct((1, dc), F32), jax.ShapeDtypeStruct((1, dc), F32),
                   jax.ShapeDtypeStruct((CONV_HALO, dc), F32)],
        scratch_shapes=[pltpu.VMEM((tt + CONV_HALO, dc), F32), pltpu.VMEM((tt + CONV_HALO, dc), F32),
                        pltpu.VMEM((tt, dc), F32)],
        compiler_params=_params(dimension_semantics=("arbitrary",)),
    )(du1, du1, u0, u0, proj, proj, taps)


def _elementwise(body, *, name, ins, out_dtypes, after, tr=128):
    rows, cols = ins[0].shape
    tr = _tile(rows, tr, 8)
    spec = pl.BlockSpec((tr, cols), lambda i: (i, 0))
    n_in = len(ins)

    def with_after(*refs):
        body(*refs[:n_in], *refs[n_in + 1:])

    return pl.pallas_call(
        with_after, name=name, grid=(rows // tr,), in_specs=[spec] * n_in + [pl.BlockSpec(memory_space=pl.ANY)],
        out_specs=[spec] * len(out_dtypes),
        out_shape=[jax.ShapeDtypeStruct((rows, cols), dt) for dt in out_dtypes],
        compiler_params=_params(dimension_semantics=("parallel",)),
    )(*ins, after)


def _prefetched(body, *, name, where, grid, in_specs, out_specs, out_shape, operands):
    return pl.pallas_call(
        body, name=name, out_shape=out_shape,
        grid_spec=pltpu.PrefetchScalarGridSpec(num_scalar_prefetch=1, grid=grid, in_specs=in_specs, out_specs=out_specs),
        compiler_params=_params(dimension_semantics=("parallel",) * len(grid)),
    )(where, *operands)


def _cast_into_block(w, where, after, name, tr=256):
    rows, cols = w.shape
    tr = _tile(rows, tr, 16)

    def body(where_ref, w_ref, _, o_ref):
        o_ref[...] = w_ref[...].astype(BF16)

    return _prefetched(
        body, name=name, where=where, grid=(rows // tr,),
        in_specs=[pl.BlockSpec((tr, cols), lambda i, wh: (i, 0)), pl.BlockSpec(memory_space=pl.ANY)],
        out_specs=pl.BlockSpec((None, tr, cols), lambda i, wh: (wh[0], i, 0)),
        out_shape=jax.ShapeDtypeStruct((N_CHIPS, rows, cols), BF16), operands=[w, after])


def _add_pair(grad, theirs, where, name, tr=512):
    nb, half, cols = theirs.shape
    tr = _tile(half, tr, 16)
    nh = half // tr

    def body(where_ref, a_ref, b_ref, o_ref):
        o_ref[...] = a_ref[...] + b_ref[...]

    blk = pl.BlockSpec((None, tr, cols), lambda b, i, wh: (b, i, 0))
    return _prefetched(
        body, name=name, where=where, grid=(nb, nh),
        in_specs=[pl.BlockSpec((None, tr, cols), lambda b, i, wh: (b, wh[1] * nh + i, 0)), blk],
        out_specs=blk, out_shape=jax.ShapeDtypeStruct(theirs.shape, BF16), operands=[grad, theirs])


def _sum_slots(part, slots, where, name, tr=512):
    _, half, cols = part.shape
    tr = _tile(half, tr, 16)
    nh = half // tr

    def body(where_ref, p_ref, s_ref, o_ref):
        o_ref[...] = ((p_ref[...].astype(F32) + s_ref[0].astype(F32)) + s_ref[1].astype(F32)) + s_ref[2].astype(F32)

    return _prefetched(
        body, name=name, where=where, grid=(nh,),
        in_specs=[pl.BlockSpec((None, tr, cols), lambda i, wh: (wh[0], i, 0)),
                  pl.BlockSpec((3, tr, cols), lambda i, wh: (0, i, 0))],
        out_specs=pl.BlockSpec((tr, cols), lambda i, wh: (wh[1] * nh + i, 0)),
        out_shape=jax.ShapeDtypeStruct((2 * half, cols), F32), operands=[part, slots])


def _adamw(w, g, m, v, after, name):
    bc1 = 1.0 - ADAM_B1 ** ADAM_STEP
    bc2 = 1.0 - ADAM_B2 ** ADAM_STEP

    def body(w_ref, g_ref, m_ref, v_ref, go_ref, d_ref, mo_ref, vo_ref):
        g = g_ref[...]
        m_new = ADAM_B1 * m_ref[...] + (1.0 - ADAM_B1) * g
        v_new = ADAM_B2 * v_ref[...] + (1.0 - ADAM_B2) * (g * g)
        m_hat = m_new / bc1
        v_hat = v_new / bc2
        go_ref[...] = g
        d_ref[...] = -ADAM_LR * (m_hat / (jnp.sqrt(v_hat) + ADAM_EPS) + ADAM_WD * w_ref[...])
        mo_ref[...] = m_new
        vo_ref[...] = v_new

    return _elementwise(body, name=name, ins=[w, g, m, v], out_dtypes=[F32] * 4, after=after, tr=128)


ANY = pl.BlockSpec(memory_space=pl.ANY)


def _mesh_pos():
    return lax.axis_index("x"), lax.axis_index("y"), lax.axis_index("c")


def _other_chips(x, y):
    return [(1 - x, y), (x, 1 - y), (1 - x, 1 - y)]


def _remote(src, dst, send_sems, recv_sems, idx, to):
    return pltpu.make_async_remote_copy(src_ref=src, dst_ref=dst, send_sem=send_sems.at[idx], recv_sem=recv_sems.at[idx],
                                        device_id=to, device_id_type=MESH)


HBM = pl.BlockSpec(memory_space=pltpu.HBM)
SEM = pl.BlockSpec(memory_space=pltpu.SEMAPHORE)
EFFECT = pltpu.SideEffectType.DATAFLOW_SIDE_EFFECTING


def _in_hbm(arrays):
    return [pltpu.with_memory_space_constraint(a, pltpu.HBM) for a in arrays]


def _ici_copies(slab, send_sems, recv_sems, x, y, c):
    half = slab.shape[1] // 2
    rows = pl.ds(c * half, half)
    mine = slab.at[2 * x + y, rows]
    out = []
    for j, (cx, cy) in enumerate(_other_chips(x, y)):
        got = slab.at[2 * cx + cy, rows]
        out.append((_remote(mine, mine, send_sems, recv_sems, j, (cx, cy, c)),
                    _remote(got, got, send_sems, recv_sems, j, (cx, cy, c))))
    return out


def _gather_start(slabs, after, name):
    n = len(slabs)

    def body(*refs):
        ins = refs[:n]
        send_sems, recv_sems, token = refs[2 * n + 1:3 * n + 1], refs[3 * n + 1:4 * n + 1], refs[4 * n + 1]
        x, y, c = _mesh_pos()
        for w in range(n):
            for send, _ in _ici_copies(ins[w], send_sems[w], recv_sems[w], x, y, c):
                send.start()
        token[...] = jnp.zeros_like(token)

    outs = pl.pallas_call(
        body, name=name, in_specs=[HBM] * n + [ANY],
        out_specs=[HBM] * n + [SEM] * (2 * n) + [pl.BlockSpec(memory_space=pltpu.VMEM)],
        out_shape=[pltpu.HBM(s.shape, s.dtype) for s in slabs] + [pltpu.SemaphoreType.DMA((3,))] * (2 * n)
        + [jax.ShapeDtypeStruct((8, 128), F32)],
        input_output_aliases={w: w for w in range(n)},
        compiler_params=_params(has_side_effects=EFFECT),
    )(*_in_hbm(slabs), after)
    return (outs[:n], outs[n:2 * n], outs[2 * n:3 * n]), outs[3 * n]


def _gather_wait(slabs, send_sems, recv_sems, after, name):
    n = len(slabs)

    def body(*refs):
        ins = refs[:n]
        sends, recvs = refs[n:2 * n], refs[2 * n:3 * n]
        x, y, c = _mesh_pos()
        for w in range(n):
            for send, recv in _ici_copies(ins[w], sends[w], recvs[w], x, y, c):
                send.wait_send()
                recv.wait_recv()

    return pl.pallas_call(
        body, name=name, in_specs=[HBM] * n + [SEM] * (2 * n) + [ANY], out_specs=[HBM] * n,
        out_shape=[pltpu.HBM(s.shape, s.dtype) for s in slabs],
        input_output_aliases={w: w for w in range(n)},
        compiler_params=_params(has_side_effects=EFFECT),
    )(*slabs, *send_sems, *recv_sems, after)


def _phase_start(arrays, copies, n_copies, after, name):
    n = len(arrays)

    def body(*refs):
        send_sems, recv_sems, token = refs[2 * n + 1], refs[2 * n + 2], refs[2 * n + 3]
        for send, _ in copies(refs[:n], send_sems, recv_sems, *_mesh_pos()):
            send.start()
        token[...] = jnp.zeros_like(token)

    outs = pl.pallas_call(
        body, name=name, in_specs=[HBM] * n + [ANY],
        out_specs=[HBM] * n + [SEM, SEM, pl.BlockSpec(memory_space=pltpu.VMEM)],
        out_shape=[pltpu.HBM(a.shape, a.dtype) for a in arrays]
        + [pltpu.SemaphoreType.DMA((n_copies,))] * 2 + [jax.ShapeDtypeStruct((8, 128), F32)],
        input_output_aliases={k: k for k in range(n)},
        compiler_params=_params(has_side_effects=EFFECT),
    )(*_in_hbm(arrays), after)
    return outs[:n], outs[n], outs[n + 1], outs[n + 2]


def _phase_wait(arrays, send_sems, recv_sems, copies, after, name):
    n = len(arrays)

    def body(*refs):
        for send, recv in copies(refs[:n], refs[n], refs[n + 1], *_mesh_pos()):
            send.wait_send()
            recv.wait_recv()

    return pl.pallas_call(
        body, name=name, in_specs=[HBM] * n + [SEM, SEM, ANY], out_specs=[HBM] * n,
        out_shape=[pltpu.HBM(a.shape, a.dtype) for a in arrays],
        input_output_aliases={k: k for k in range(n)},
        compiler_params=_params(has_side_effects=EFFECT),
    )(*arrays, send_sems, recv_sems, after)


def _forward_copies(slabs, send_sems, recv_sems, x, y, c):
    sibling = (x, y, 1 - c)
    out = []
    for w, slab in enumerate(slabs):
        half = slab.shape[1] // 2
        for j, (cx, cy) in enumerate(_other_chips(x, y)):
            mine = slab.at[2 * cx + cy, pl.ds(c * half, half)]
            theirs = slab.at[2 * cx + cy, pl.ds((1 - c) * half, half)]
            out.append((_remote(mine, mine, send_sems, recv_sems, 3 * w + j, sibling),
                        _remote(theirs, theirs, send_sems, recv_sems, 3 * w + j, sibling)))
    return out


def _share_copies(halves, send_sems, recv_sems, x, y, c):
    sibling = (x, y, 1 - c)
    out = []
    for w, full in enumerate(halves):
        h = full.shape[0] // 2
        mine, theirs = full.at[pl.ds(c * h, h)], full.at[pl.ds((1 - c) * h, h)]
        out.append((_remote(mine, mine, send_sems, recv_sems, w, sibling),
                    _remote(theirs, theirs, send_sems, recv_sems, w, sibling)))
    return out


def _exchange_copies(arrays, send_sems, recv_sems, x, y, c):
    n = len(arrays) // 2
    sibling = (x, y, 1 - c)
    out = []
    for w in range(n):
        grad, land = arrays[w], arrays[n + w]
        half = grad.shape[1] // 2
        out.append((_remote(grad.at[:, pl.ds((1 - c) * half, half), :], land, send_sems, recv_sems, w, sibling),
                    _remote(land, land, send_sems, recv_sems, w, sibling)))
    return out


def _gather_taps(taps):
    def body(in_ref, out_ref, send_sems, recv_sems, local_sem):
        x, y, c = _mesh_pos()
        me = 2 * x + y
        chips = _other_chips(x, y)
        lc = pltpu.make_async_copy(in_ref, out_ref.at[me], local_sem)
        lc.start()
        sends = [_remote(in_ref, out_ref.at[me], send_sems, recv_sems, j, (cx, cy, c)) for j, (cx, cy) in enumerate(chips)]
        for cp in sends:
            cp.start()
        for j, (cx, cy) in enumerate(chips):
            got = out_ref.at[2 * cx + cy]
            _remote(got, got, send_sems, recv_sems, j, (cx, cy, c)).wait_recv()
        for cp in sends:
            cp.wait_send()
        lc.wait()

    vm = pl.BlockSpec(memory_space=pltpu.VMEM)
    return pl.pallas_call(
        body, name="gather_taps", in_specs=[vm], out_specs=vm,
        out_shape=jax.ShapeDtypeStruct((N_CHIPS,) + taps.shape, taps.dtype),
        scratch_shapes=[pltpu.SemaphoreType.DMA((3,)), pltpu.SemaphoreType.DMA((3,)), pltpu.SemaphoreType.DMA],
        compiler_params=_params(has_side_effects=True),
    )(taps)


def _scatter_copies(part, slots, send_sems, recv_sems, x, y, c):
    out = []
    for j, (cx, cy) in enumerate(_other_chips(x, y)):
        out.append((_remote(part.at[2 * cx + cy], slots.at[j], send_sems, recv_sems, j, (cx, cy, c)),
                    _remote(slots.at[j], slots.at[j], send_sems, recv_sems, j, (cx, cy, c))))
    return out


def _scatter_start(parts, after, name):
    n = len(parts)
    slots = [lax.empty((3,) + p.shape[1:], p.dtype) for p in parts]

    def body(*refs):
        ins, lands = refs[:n], refs[n:2 * n]
        send_sems, recv_sems, token = refs[4 * n + 1:5 * n + 1], refs[5 * n + 1:6 * n + 1], refs[6 * n + 1]
        x, y, c = _mesh_pos()
        for w in range(n):
            for send, _ in _scatter_copies(ins[w], lands[w], send_sems[w], recv_sems[w], x, y, c):
                send.start()
        token[...] = jnp.zeros_like(token)

    outs = pl.pallas_call(
        body, name=name, in_specs=[HBM] * (2 * n) + [ANY],
        out_specs=[HBM] * (2 * n) + [SEM] * (2 * n) + [pl.BlockSpec(memory_space=pltpu.VMEM)],
        out_shape=[pltpu.HBM(a.shape, a.dtype) for a in list(parts) + slots] + [pltpu.SemaphoreType.DMA((3,))] * (2 * n)
        + [jax.ShapeDtypeStruct((8, 128), F32)],
        input_output_aliases={k: k for k in range(2 * n)},
        compiler_params=_params(has_side_effects=EFFECT),
    )(*_in_hbm(list(parts) + slots), after)
    return (outs[:n], outs[n:2 * n], outs[2 * n:3 * n], outs[3 * n:4 * n]), outs[4 * n]


def _scatter_wait(parts, slots, send_sems, recv_sems, after, name):
    n = len(parts)

    def body(*refs):
        ins, lands = refs[:n], refs[n:2 * n]
        sends, recvs = refs[2 * n:3 * n], refs[3 * n:4 * n]
        x, y, c = _mesh_pos()
        for w in range(n):
            for send, recv in _scatter_copies(ins[w], lands[w], sends[w], recvs[w], x, y, c):
                send.wait_send()
                recv.wait_recv()

    outs = pl.pallas_call(
        body, name=name, in_specs=[HBM] * (2 * n) + [SEM] * (2 * n) + [ANY], out_specs=[HBM] * (2 * n),
        out_shape=[pltpu.HBM(a.shape, a.dtype) for a in list(parts) + list(slots)],
        input_output_aliases={k: k for k in range(2 * n)},
        compiler_params=_params(has_side_effects=EFFECT),
    )(*parts, *slots, *send_sems, *recv_sems, after)
    return outs[:n], outs[n:]


def _all_reduce_small(buf, after, name):
    rows, width = buf.shape

    def body(in_ref, _, out_ref, gathered, send_sems, recv_sems):
        x, y, c = _mesh_pos()
        me = 4 * x + 2 * y + c
        gathered[me] = in_ref[...]
        flips = [(dx, dy, dz) for dx in (0, 1) for dy in (0, 1) for dz in (0, 1)][1:]
        peers = [((1 - x) if dx else x, (1 - y) if dy else y, (1 - c) if dz else c) for dx, dy, dz in flips]
        sends = [_remote(in_ref, gathered.at[me], send_sems, recv_sems, k, peer) for k, peer in enumerate(peers)]
        for cp in sends:
            cp.start()
        for k, (px, py, pc) in enumerate(peers):
            got = gathered.at[4 * px + 2 * py + pc]
            _remote(got, got, send_sems, recv_sems, k, (px, py, pc)).wait_recv()
        for cp in sends:
            cp.wait_send()
        total = gathered[0]
        for d in range(1, 8):
            total = total + gathered[d]
        out_ref[...] = total

    vm = pl.BlockSpec(memory_space=pltpu.VMEM)
    return pl.pallas_call(
        body, name=name, in_specs=[vm, ANY], out_specs=vm, out_shape=jax.ShapeDtypeStruct(buf.shape, F32),
        scratch_shapes=[pltpu.VMEM((8, rows, width), F32), pltpu.SemaphoreType.DMA((7,)), pltpu.SemaphoreType.DMA((7,))],
        compiler_params=_params(has_side_effects=True),
    )(buf, after)


PACK_W = 2048
PACK_ROWS = 8


def _pack(vectors):
    flat = jnp.concatenate([v.reshape(-1) for v in vectors])
    unit = PACK_W * PACK_ROWS
    total = -(-flat.shape[0] // unit) * unit
    return jnp.pad(flat, (0, total - flat.shape[0])).reshape(total // PACK_W, PACK_W)


def _unpack(buf, shapes):
    flat = buf.reshape(-1)
    out, pos = [], 0
    for shp in shapes:
        size = math.prod(shp)
        out.append(flat[pos:pos + size].reshape(shp))
        pos += size
    return out


def _pad_rows(a, rows):
    return jnp.pad(a, ((0, rows - a.shape[0]), (0, 0)))


def kernel(x, g_pre_mix, w_in, b_in, w_dw, b_dw, g_conv_ln, b_conv_ln, w_sb_out, w_conv_out, w_o, g_post_mix, g_pre_mlp, w_up, w_down, g_post_mlp, loss_target, m_g_pre_mix, m_w_in, m_b_in, m_w_dw, m_b_dw, m_g_conv_ln, m_b_conv_ln, m_w_sb_out, m_w_conv_out, m_w_o, m_g_post_mix, m_g_pre_mlp, m_w_up, m_w_down, m_g_post_mlp, v_g_pre_mix, v_w_in, v_b_in, v_w_dw, v_b_dw, v_g_conv_ln, v_b_conv_ln, v_w_sb_out, v_w_conv_out, v_w_o, v_g_post_mix, v_g_pre_mlp, v_w_up, v_w_down, v_g_post_mlp):
    xs = x[0]
    tgt = loss_target[0]
    s, d = xs.shape
    d_sb = w_sb_out.shape[1]
    dc = w_conv_out.shape[1]
    d_in = w_in.shape[2] * N_CHIPS
    n_heads = d_sb // HEAD_DIM
    off_a = 3 * d_sb
    off_sb = off_a + 2 * dc
    off_cv = off_sb + d
    gw = dc
    assert off_a % dc == 0 and off_sb % gw == 0 and d % gw == 0 and d_in == off_cv + d
    chip = 2 * lax.axis_index("x") + lax.axis_index("y")

    big = dict(w_in=w_in[0], w_sb_out=w_sb_out[0], w_conv_out=w_conv_out[0], w_o=w_o[0], w_up=w_up[0], w_down=w_down[0])
    names = list(big)
    col_sharded = {"w_in", "w_sb_out", "w_conv_out", "w_up"}
    where = jnp.stack([chip, lax.axis_index("c")]).astype(jnp.int32)
    cs = w_dw.shape[2]
    taps4 = _gather_taps(_pad_rows(w_dw[0], CONV_HALO))
    first, token = _gather_start([_cast_into_block(big["w_in"], where, taps4, "cast_w_in")], taps4, "gather_start_in")
    rest, rest_token = _gather_start([_cast_into_block(big[n], where, token, "cast_" + n) for n in names[1:]], token,
                            "gather_start_rest")
    in_flight = dict(zip(names, zip(*[a + b for a, b in zip(first, rest)])))
    gathered = {}

    def land(group, after, tag):
        thru, sends, recvs = zip(*[in_flight[n] for n in group])
        landed = _gather_wait(thru, sends, recvs, after, "gather_wait_" + tag)
        return _phase_start(landed, _forward_copies, 3 * len(group), where, "forward_start_" + tag)

    def ready(group, started, after, tag):
        slabs, sends, recvs, _ = started
        for n, g4 in zip(group, _phase_wait(slabs, sends, recvs, _forward_copies, after, "forward_wait_" + tag)):
            gathered[n] = g4 if n in col_sharded else g4.reshape(g4.shape[0] * g4.shape[1], g4.shape[2])

    taps = jnp.transpose(taps4, (1, 0, 2)).reshape(CONV_HALO, N_CHIPS * cs)
    group_mix = ["w_sb_out", "w_conv_out", "w_o"]

    h = _rms_fwd(xs, g_pre_mix)
    started = land(["w_in"], rest_token, "in")
    ready(["w_in"], started, h, "in")
    proj = _mm(h, gathered["w_in"], name="mm_proj", b_groups=N_CHIPS, bias=b_in, tn=768)
    mix_fwd = land(group_mix, proj, "mix")
    attn = _attn_fwd(proj, mix_fwd[3], n_heads, d_sb)
    u0 = _glu(proj, off_a, dc)
    u1, u3 = _conv_fwd(u0, taps, b_dw, g_conv_ln, b_conv_ln)
    ready(group_mix, mix_fwd, u3, "mix")
    o_sb = _mm(attn, gathered["w_sb_out"], name="mm_o_sb", b_groups=N_CHIPS)
    up_fwd = land(["w_up"], o_sb, "up")
    o_cv = _mm(u3, gathered["w_conv_out"], name="mm_o_cv", b_groups=N_CHIPS, after=up_fwd[3])
    merged = _merge_fwd(proj, o_sb, o_cv, off_sb, off_cv, gw)
    y = _mm(merged, gathered["w_o"], name="mm_y")
    x2, h2 = _resid_rms(xs, y, g_post_mix, g_pre_mlp)
    ready(["w_up"], up_fwd, h2, "up")
    down_fwd = land(["w_down"], h2, "down")
    up, f = _mm(h2, gathered["w_up"], name="mm_up", b_groups=N_CHIPS, out_dtypes=(F32, BF16), after=down_fwd[3],
                epilogue=lambda acc: (acc, jnp.square(jnp.maximum(acc, 0.0))))
    ready(["w_down"], down_fwd, f, "down")
    dn = _mm(f, gathered["w_down"], name="mm_down")
    dx3, d_dn, dg_post_mlp, loss_part = _final(x2, dn, g_post_mlp, tgt)

    moments = dict(w_in=(m_w_in, v_w_in), w_sb_out=(m_w_sb_out, v_w_sb_out), w_conv_out=(m_w_conv_out, v_w_conv_out),
                   w_o=(m_w_o, v_w_o), w_up=(m_w_up, v_w_up), w_down=(m_w_down, v_w_down))
    out_g, out_d, out_m, out_v = {}, {}, {}, {}
    grads = {}

    def blocks_of(group):
        out = []
        for n in group:
            g = grads[n]
            out.append(g if g.ndim == 3 else g.reshape(N_CHIPS, g.shape[0] // N_CHIPS, g.shape[1]))
        return out

    def exchange_begin(group, tag):
        blocks = blocks_of(group)
        lands = [lax.empty((b.shape[0], b.shape[1] // 2, b.shape[2]), b.dtype) for b in blocks]
        return _phase_start(blocks + lands, _exchange_copies, len(group), where, "exchange_start_" + tag)

    def scatter_begin(group, tag, exchange, after):
        arrays, sends, recvs, _ = exchange
        arrays = _phase_wait(arrays, sends, recvs, _exchange_copies, after, "exchange_wait_" + tag)
        n = len(group)
        pair_sums = [_add_pair(g, t, where, "pair_sum_" + m) for m, g, t in zip(group, arrays[:n], arrays[n:])]
        return _scatter_start(pair_sums, where, "scatter_start_" + tag)

    def reduce_sum(group, tag, started, after):
        parts, slots = _scatter_wait(*started, after, "scatter_wait_" + tag)
        halves = [_sum_slots(p, t, where, "chip_sum_" + n) for n, p, t in zip(group, parts, slots)]
        return _phase_start(halves, _share_copies, len(group), where, "share_start_" + tag)

    def reduce_update(group, tag, shared, after):
        halves, sends, recvs, _ = shared
        for n, red in zip(group, _phase_wait(halves, sends, recvs, _share_copies, after, "share_wait_" + tag)):
            mm_, vv_ = moments[n]
            res = _adamw(big[n], red, mm_[0], vv_[0], after, "adamw_" + n)
            out_g[n], out_d[n], out_m[n], out_v[n] = [r[None] for r in res]
            after = res[1]
        return after

    group_mlp, group_mix, group_in = ["w_down", "w_up"], ["w_o", "w_sb_out", "w_conv_out"], ["w_in"]
    grads["w_down"] = _mm(f, d_dn, name="mm_dw_down", ta=True, out_dtypes=(BF16,))
    dup = _mm(d_dn, gathered["w_down"], name="mm_df", tb=True, extra=[up], out_dtypes=(BF16,),
              epilogue=lambda acc, upv: (acc * (2.0 * jnp.maximum(upv, 0.0)),))
    grads["w_up"] = _mm(h2, dup, name="mm_dw_up", ta=True, out_groups=N_CHIPS, out_dtypes=(BF16,))
    mlp_exchange = exchange_begin(group_mlp, "mlp")
    dh2 = _mm(dup, gathered["w_up"], name="mm_dh2", tb=True, b_groups=N_CHIPS, after=mlp_exchange[3])
    dx2, dy, dg_pre_mlp, dg_post_mix = _rms_bwd2(dx3, dh2, x2, y, g_post_mix, g_pre_mlp)
    mlp_started, mlp_token = scatter_begin(group_mlp, "mlp", mlp_exchange, dx2)
    grads["w_o"] = _mm(merged, dy, name="mm_dw_o", ta=True, out_dtypes=(BF16,), after=mlp_token)
    dmerged = _mm(dy, gathered["w_o"], name="mm_dmerged", tb=True)
    do_sb, do_cv, dgate_sb, dgate_cv, s_gate_sb, s_gate_cv = _merge_bwd(proj, o_sb, o_cv, dmerged, off_sb, off_cv, gw)
    grads["w_sb_out"] = _mm(attn, do_sb, name="mm_dw_sb", ta=True, out_groups=N_CHIPS, out_dtypes=(BF16,))
    dattn = _mm(do_sb, gathered["w_sb_out"], name="mm_dattn", tb=True, b_groups=N_CHIPS)
    grads["w_conv_out"] = _mm(u3, do_cv, name="mm_dw_cv", ta=True, out_groups=N_CHIPS, out_dtypes=(BF16,))
    mix_exchange = exchange_begin(group_mix, "mix")
    du3 = _mm(do_cv, gathered["w_conv_out"], name="mm_du3", tb=True, b_groups=N_CHIPS, after=mix_exchange[3])
    mix_started, mix_token = scatter_begin(group_mix, "mix", mix_exchange, du3)
    dq, dk, dv, s_q, s_k, s_v = _attn_bwd(proj, dattn, mix_token, n_heads, d_sb)
    du1, dg_conv_ln, db_conv_ln, db_dw = _conv_bwd_norm(u1, du3, g_conv_ln, b_conv_ln)
    dglu_a, dglu_b, s_a, s_b, dtaps = _conv_bwd_taps(du1, u0, proj, taps, off_a)
    dproj = jnp.concatenate([dq, dk, dv, dglu_a, dglu_b, dgate_sb, dgate_cv], axis=1)
    db_in = jnp.concatenate([s_q, s_k, s_v, s_a, s_b, s_gate_sb, s_gate_cv], axis=1)
    small_w = dict(g_pre_mix=g_pre_mix, b_in=b_in, b_dw=b_dw, g_conv_ln=g_conv_ln, b_conv_ln=b_conv_ln,
                   g_post_mix=g_post_mix, g_pre_mlp=g_pre_mlp, g_post_mlp=g_post_mlp)
    small_m = dict(g_pre_mix=m_g_pre_mix, b_in=m_b_in, b_dw=m_b_dw, g_conv_ln=m_g_conv_ln, b_conv_ln=m_b_conv_ln,
                   g_post_mix=m_g_post_mix, g_pre_mlp=m_g_pre_mlp, g_post_mlp=m_g_post_mlp)
    small_v = dict(g_pre_mix=v_g_pre_mix, b_in=v_b_in, b_dw=v_b_dw, g_conv_ln=v_g_conv_ln, b_conv_ln=v_b_conv_ln,
                   g_post_mix=v_g_post_mix, g_pre_mlp=v_g_pre_mlp, g_post_mlp=v_g_post_mlp)
    small_g = dict(b_in=db_in, b_dw=db_dw, g_conv_ln=dg_conv_ln, b_conv_ln=db_conv_ln,
                   g_post_mix=dg_post_mix, g_pre_mlp=dg_pre_mlp, g_post_mlp=dg_post_mlp)

    def small_update(group, tails, after, tag):
        shapes = [small_w[n].shape for n in group] + [t.shape for t in tails]
        summed = _all_reduce_small(_pack([small_g[n] for n in group] + tails), after, "all_reduce_small_" + tag)
        zeros_tail = [jnp.zeros(t.shape, F32) for t in tails]
        res = _adamw(_pack([small_w[n] for n in group] + zeros_tail), summed,
                     _pack([small_m[n] for n in group] + zeros_tail), _pack([small_v[n] for n in group] + zeros_tail),
                     summed, "adamw_small_" + tag)
        unpacked = [_unpack(r, shapes) for r in res]
        for i, n in enumerate(group):
            out_g[n], out_d[n], out_m[n], out_v[n] = [u[i] for u in unpacked]
        return summed, unpacked[0][len(group):]

    early = ["b_in", "b_dw", "g_conv_ln", "b_conv_ln", "g_post_mix", "g_pre_mlp", "g_post_mlp"]
    summed, (loss_sum, taps_sum) = small_update(early, [loss_part, dtaps], dproj, "early")
    loss = loss_sum[0, 0]
    taps_grad = lax.dynamic_slice(taps_sum, (0, chip * cs), (CONV_HALO, cs))
    res = _adamw(_pad_rows(w_dw[0], CONV_HALO), taps_grad, _pad_rows(m_w_dw[0], CONV_HALO), _pad_rows(v_w_dw[0], CONV_HALO),
                 taps_grad, "adamw_taps")
    out_g["w_dw"], out_d["w_dw"], out_m["w_dw"], out_v["w_dw"] = [r[:CONV_WIDTH][None] for r in res]
    grads["w_in"] = _mm(h, dproj, name="mm_dw_in", ta=True, out_groups=N_CHIPS, out_dtypes=(BF16,), tn=768, after=summed)
    in_exchange = exchange_begin(group_in, "in")
    dh = _mm(dproj, gathered["w_in"], name="mm_dh", tb=True, b_groups=N_CHIPS, tk=2304, after=in_exchange[3])
    in_started, in_token = scatter_begin(group_in, "in", in_exchange, dh)
    grad_x, dg_pre_mix = _rms_bwd1(dx2, dh, xs, g_pre_mix)
    mlp_shared = reduce_sum(group_mlp, "mlp", mlp_started, in_token)
    mix_shared = reduce_sum(group_mix, "mix", mix_started, mlp_shared[3])
    done = reduce_update(group_mlp, "mlp", mlp_shared, mix_shared[3])
    done = reduce_update(group_mix, "mix", mix_shared, done)
    small_g["g_pre_mix"] = dg_pre_mix
    summed, _ = small_update(["g_pre_mix"], [], done, "late")
    in_shared = reduce_sum(group_in, "in", in_started, summed)
    reduce_update(group_in, "in", in_shared, in_shared[3])

    order = ["g_pre_mix", "w_in", "b_in", "w_dw", "b_dw", "g_conv_ln", "b_conv_ln", "w_sb_out", "w_conv_out", "w_o",
             "g_post_mix", "g_pre_mlp", "w_up", "w_down", "g_post_mlp"]
    return (loss, grad_x[None], *[out_g[n] for n in order], *[out_d[n] for n in order],
            *[out_m[n] for n in order], *[out_v[n] for n in order])
```

```python
import math

import jax
import jax.numpy as jnp
from jax import lax
from jax.experimental import pallas as pl
from jax.experimental.pallas import tpu as pltpu

F32 = jnp.float32
BF16 = jnp.bfloat16
MESH = pl.DeviceIdType.MESH

HEAD_DIM = 128
CONV_WIDTH = 31
CONV_HALO = 32
EPS = 1e-6
ADAM_LR = 0.001
ADAM_B1 = 0.9
ADAM_B2 = 0.999
ADAM_EPS = 1e-08
ADAM_WD = 0.01
ADAM_STEP = 10
N_CHIPS = 4
VMEM_LIMIT = 48 * 1024 * 1024


def _params(**kw):
    return pltpu.CompilerParams(vmem_limit_bytes=VMEM_LIMIT, **kw)


def _tile(n, want, align=128):
    if n <= want:
        return n
    for t in range(want - want % align, 0, -align):
        if n % t == 0:
            return t
    raise ValueError((n, want, align))


def _mm(a, b, *, name, ta=False, tb=False, b_groups=1, out_groups=1, bias=None, extra=None, epilogue=None,
        out_dtypes=(F32,), tm=1024, tn=1024, tk=2048, after=None):
    if ta:
        K, M = a.shape
    else:
        M, K = a.shape
    if b_groups == 1:
        br, bc = b.shape
    else:
        _, br, bcg = b.shape
        bc = bcg * b_groups
    if tb:
        N, Kb = br, bc
    else:
        Kb, N = br, bc
    assert Kb == K, (name, a.shape, b.shape)
    tm, tn, tk = _tile(M, tm), _tile(N, tn), _tile(K, tk)
    if b_groups > 1:
        if tb:
            tk = _tile(K // b_groups, tk)
        else:
            tn = _tile(N // b_groups, tn)
    if out_groups > 1:
        tn = _tile(N // out_groups, tn)
        if b_groups > 1 and not tb:
            tn = _tile(N // b_groups, tn)
    nm, nn, nk = M // tm, N // tn, K // tk
    extra = tuple(extra or ())
    n_extra = len(extra)
    has_bias = bias is not None
    n_out = len(out_dtypes)

    a_spec = pl.BlockSpec((tk, tm), lambda i, j, k: (k, i)) if ta else pl.BlockSpec((tm, tk), lambda i, j, k: (i, k))
    if b_groups == 1:
        b_spec = pl.BlockSpec((tn, tk), lambda i, j, k: (j, k)) if tb else pl.BlockSpec((tk, tn), lambda i, j, k: (k, j))
    elif tb:
        kpg = (K // b_groups) // tk
        b_spec = pl.BlockSpec((None, tn, tk), lambda i, j, k: (k // kpg, j, k % kpg))
    else:
        npg = (N // b_groups) // tn
        b_spec = pl.BlockSpec((None, tk, tn), lambda i, j, k: (j // npg, k, j % npg))
    in_specs = [a_spec, b_spec]
    operands = [a, b]
    if has_bias:
        in_specs.append(pl.BlockSpec((1, tn), lambda i, j, k: (0, j)))
        operands.append(bias)
    for e in extra:
        in_specs.append(pl.BlockSpec((tm, tn), lambda i, j, k: (i, j)))
        operands.append(e)
    n_after = 0 if after is None else 1
    if after is not None:
        in_specs.append(pl.BlockSpec(memory_space=pl.ANY))
        operands.append(after)
    if out_groups == 1:
        o_spec = pl.BlockSpec((tm, tn), lambda i, j, k: (i, j))
        o_shape = (M, N)
    else:
        opg = (N // out_groups) // tn
        o_spec = pl.BlockSpec((None, tm, tn), lambda i, j, k: (j // opg, i, j % opg))
        o_shape = (out_groups, M, N // out_groups)
    dims = (((0 if ta else 1,), (1 if tb else 0,)), ((), ()))

    def body(*refs):
        a_ref, b_ref = refs[0], refs[1]
        pos = 2
        bias_ref = None
        if has_bias:
            bias_ref = refs[pos]
            pos += 1
        extra_refs = refs[pos:pos + n_extra]
        pos += n_extra + n_after
        out_refs = refs[pos:pos + n_out]
        pos += n_out
        acc_ref = refs[pos] if nk > 1 else None

        part = lax.dot_general(a_ref[...].astype(BF16), b_ref[...].astype(BF16), dims, preferred_element_type=F32)

        def finish(acc):
            if has_bias:
                acc = acc + bias_ref[...]
            outs = epilogue(acc, *[e[...] for e in extra_refs]) if epilogue is not None else (acc,)
            for o_ref, o in zip(out_refs, outs):
                o_ref[...] = o.astype(o_ref.dtype)

        if nk == 1:
            finish(part)
        else:
            k = pl.program_id(2)

            @pl.when(k == 0)
            def _():
                acc_ref[...] = part

            @pl.when(k > 0)
            def _():
                acc_ref[...] += part

            @pl.when(k == nk - 1)
            def _():
                finish(acc_ref[...])

    outs = pl.pallas_call(
        body,
        name=name,
        grid=(nm, nn, nk),
        in_specs=in_specs,
        out_specs=[o_spec] * n_out,
        out_shape=[jax.ShapeDtypeStruct(o_shape, dt) for dt in out_dtypes],
        scratch_shapes=[pltpu.VMEM((tm, tn), F32)] if nk > 1 else [],
        compiler_params=_params(dimension_semantics=("parallel", "parallel", "arbitrary")),
    )(*operands)
    return outs[0] if n_out == 1 else outs


def _rms_stats(x):
    return lax.rsqrt(jnp.mean(x * x, axis=-1, keepdims=True) + EPS)


def _rms_bwd(x, r, g, dy):
    gy = dy * g
    return r * gy - x * (r * r * r) * jnp.mean(x * gy, axis=-1, keepdims=True)


def _sigmoid(x):
    return 1.0 / (1.0 + jnp.exp(-x))


def _row_call(body, *, name, rows, tr, ins, outs, acc_outs=()):
    n = rows // tr
    in_specs = []
    for arr, kind in ins:
        if kind == "row":
            in_specs.append(pl.BlockSpec((tr, arr.shape[1]), lambda i: (i, 0)))
        else:
            in_specs.append(pl.BlockSpec(arr.shape, lambda i: (0, 0)))
    out_specs = [pl.BlockSpec((tr, w), lambda i: (i, 0)) for w, _ in outs]
    out_shape = [jax.ShapeDtypeStruct((rows, w), dt) for w, dt in outs]
    out_specs += [pl.BlockSpec((1, w), lambda i: (0, 0)) for w in acc_outs]
    out_shape += [jax.ShapeDtypeStruct((1, w), F32) for w in acc_outs]
    return pl.pallas_call(
        body, name=name, grid=(n,), in_specs=in_specs, out_specs=out_specs, out_shape=out_shape,
        compiler_params=_params(dimension_semantics=("arbitrary",)),
    )(*[a for a, _ in ins])


def _accumulate(ref, val):
    @pl.when(pl.program_id(0) == 0)
    def _():
        ref[...] = val

    @pl.when(pl.program_id(0) > 0)
    def _():
        ref[...] += val


def _rms_fwd(x, g, tr=256):
    def body(x_ref, g_ref, h_ref):
        xv = x_ref[...]
        h_ref[...] = (xv * _rms_stats(xv) * g_ref[...]).astype(BF16)

    (h,) = _row_call(body, name="rms_fwd", rows=x.shape[0], tr=tr, ins=[(x, "row"), (g, "vec")], outs=[(x.shape[1], BF16)])
    return h


def _resid_rms(x, y, g2, g3, tr=256):
    def body(x_ref, y_ref, g2_ref, g3_ref, x2_ref, h2_ref):
        yv = y_ref[...]
        x2 = x_ref[...] + yv * _rms_stats(yv) * g2_ref[...]
        x2_ref[...] = x2
        h2_ref[...] = (x2 * _rms_stats(x2) * g3_ref[...]).astype(BF16)

    d = x.shape[1]
    return _row_call(body, name="resid_rms", rows=x.shape[0], tr=tr,
                     ins=[(x, "row"), (y, "row"), (g2, "vec"), (g3, "vec")], outs=[(d, F32), (d, BF16)])


def _final(x2, dn, g4, tgt, tr=256):
    d = x2.shape[1]

    def body(x2_ref, dn_ref, g4_ref, t_ref, dx3_ref, ddn_ref, dg4_ref, loss_ref):
        dn_v = dn_ref[...]
        r = _rms_stats(dn_v)
        g = g4_ref[...]
        e = x2_ref[...] + dn_v * r * g - t_ref[...]
        dx3 = e * (1.0 / d)
        dx3_ref[...] = dx3
        ddn_ref[...] = _rms_bwd(dn_v, r, g, dx3).astype(BF16)
        _accumulate(dg4_ref, jnp.sum(dx3 * dn_v * r, axis=0, keepdims=True))
        part = 0.5 * jnp.sum(jnp.mean(e * e, axis=-1, keepdims=True), axis=0, keepdims=True)
        _accumulate(loss_ref, jnp.broadcast_to(part, loss_ref.shape))

    return _row_call(body, name="final", rows=x2.shape[0], tr=tr,
                     ins=[(x2, "row"), (dn, "row"), (g4, "vec"), (tgt, "row")],
                     outs=[(d, F32), (d, BF16)], acc_outs=(d, 128))


def _rms_bwd2(dx3, dh2, x2, y, g2, g3, tr=256):
    d = x2.shape[1]

    def body(dx3_ref, dh2_ref, x2_ref, y_ref, g2_ref, g3_ref, dx2_ref, dy_ref, dg3_ref, dg2_ref):
        x2v, dh2v, yv = x2_ref[...], dh2_ref[...], y_ref[...]
        r3 = _rms_stats(x2v)
        dx2 = dx3_ref[...] + _rms_bwd(x2v, r3, g3_ref[...], dh2v)
        dx2_ref[...] = dx2
        _accumulate(dg3_ref, jnp.sum(dh2v * x2v * r3, axis=0, keepdims=True))
        r2 = _rms_stats(yv)
        dy_ref[...] = _rms_bwd(yv, r2, g2_ref[...], dx2).astype(BF16)
        _accumulate(dg2_ref, jnp.sum(dx2 * yv * r2, axis=0, keepdims=True))

    return _row_call(body, name="rms_bwd2", rows=x2.shape[0], tr=tr,
                     ins=[(dx3, "row"), (dh2, "row"), (x2, "row"), (y, "row"), (g2, "vec"), (g3, "vec")],
                     outs=[(d, F32), (d, BF16)], acc_outs=(d, d))


def _rms_bwd1(dx2, dh, x, g1, tr=256):
    d = x.shape[1]

    def body(dx2_ref, dh_ref, x_ref, g1_ref, gx_ref, dg1_ref):
        xv, dhv = x_ref[...], dh_ref[...]
        r = _rms_stats(xv)
        gx_ref[...] = dx2_ref[...] + _rms_bwd(xv, r, g1_ref[...], dhv)
        _accumulate(dg1_ref, jnp.sum(dhv * xv * r, axis=0, keepdims=True))

    return _row_call(body, name="rms_bwd1", rows=x.shape[0], tr=tr,
                     ins=[(dx2, "row"), (dh, "row"), (x, "row"), (g1, "vec")], outs=[(d, F32)], acc_outs=(d,))


def _merge_fwd(proj, o_sb, o_cv, off_sb, off_cv, gw, tr=256):
    s, d = o_sb.shape
    nj = d // gw
    b_sb, b_cv = off_sb // gw, off_cv // gw

    def body(gs_ref, gc_ref, osb_ref, ocv_ref, m_ref):
        m_ref[...] = (_sigmoid(gs_ref[...]) * osb_ref[...] + _sigmoid(gc_ref[...]) * ocv_ref[...]).astype(BF16)

    blk = lambda i, j: (i, j)
    return pl.pallas_call(
        body, name="merge_fwd", grid=(s // tr, nj),
        in_specs=[pl.BlockSpec((tr, gw), lambda i, j: (i, b_sb + j)), pl.BlockSpec((tr, gw), lambda i, j: (i, b_cv + j)),
                  pl.BlockSpec((tr, gw), blk), pl.BlockSpec((tr, gw), blk)],
        out_specs=pl.BlockSpec((tr, gw), blk), out_shape=jax.ShapeDtypeStruct((s, d), BF16),
        compiler_params=_params(dimension_semantics=("parallel", "parallel")),
    )(proj, proj, o_sb, o_cv)


def _merge_bwd(proj, o_sb, o_cv, dmerged, off_sb, off_cv, gw, tr=256):
    s, d = o_sb.shape
    nj = d // gw
    ni = s // tr
    b_sb, b_cv = off_sb // gw, off_cv // gw

    def body(gs_ref, gc_ref, osb_ref, ocv_ref, dm_ref, dosb_ref, docv_ref, dgs_ref, dgc_ref, sgs_ref, sgc_ref):
        dm = dm_ref[...]
        s_sb, s_cv = _sigmoid(gs_ref[...]), _sigmoid(gc_ref[...])
        dosb_ref[...] = (dm * s_sb).astype(BF16)
        docv_ref[...] = (dm * s_cv).astype(BF16)
        dgs = dm * osb_ref[...] * s_sb * (1.0 - s_sb)
        dgc = dm * ocv_ref[...] * s_cv * (1.0 - s_cv)
        dgs_ref[...] = dgs.astype(BF16)
        dgc_ref[...] = dgc.astype(BF16)
        i = pl.program_id(1)
        for ref, val in ((sgs_ref, dgs), (sgc_ref, dgc)):
            col = jnp.sum(val, axis=0, keepdims=True)

            @pl.when(i == 0)
            def _():
                ref[...] = col

            @pl.when(i > 0)
            def _():
                ref[...] += col

    blk = lambda j, i: (i, j)
    outs = pl.pallas_call(
        body, name="merge_bwd", grid=(nj, ni),
        in_specs=[pl.BlockSpec((tr, gw), lambda j, i: (i, b_sb + j)), pl.BlockSpec((tr, gw), lambda j, i: (i, b_cv + j)),
                  pl.BlockSpec((tr, gw), blk), pl.BlockSpec((tr, gw), blk), pl.BlockSpec((tr, gw), blk)],
        out_specs=[pl.BlockSpec((tr, gw), blk), pl.BlockSpec((tr, gw), blk),
                   pl.BlockSpec((tr, gw), blk), pl.BlockSpec((tr, gw), blk),
                   pl.BlockSpec((1, gw), lambda j, i: (0, j)), pl.BlockSpec((1, gw), lambda j, i: (0, j))],
        out_shape=[jax.ShapeDtypeStruct((s, d), BF16)] * 4 + [jax.ShapeDtypeStruct((1, d), F32)] * 2,
        compiler_params=_params(dimension_semantics=("parallel", "arbitrary")),
    )(proj, proj, o_sb, o_cv, dmerged)
    return outs


def _split_bf16(v):
    hi = v.astype(BF16)
    lo = (v - hi.astype(F32)).astype(BF16)
    return hi, lo


def _dot_nt(a, b):
    return lax.dot_general(a, b, (((1,), (1,)), ((), ())), preferred_element_type=F32)


def _dot_tn(a, b):
    return lax.dot_general(a, b, (((0,), (0,)), ((), ())), preferred_element_type=F32)


def _dot_nn(a, b):
    return lax.dot_general(a, b, (((1,), (0,)), ((), ())), preferred_element_type=F32)


def _sum_right(v, tri):
    hi, lo = _split_bf16(v)
    return _dot_nn(hi, tri) + _dot_nn(lo, tri)


HEADS_PER_STEP = 2
HEADS_PER_STEP_FWD = 4


LOG2_E = 1.4426950408889634


def _sb_tile(q, kb, scale, mask):
    z = _dot_nt(q, kb) * (scale * LOG2_E)
    log_b = jnp.minimum(z, 0.0) - jnp.log2(1.0 + jnp.exp2(-jnp.abs(z)))
    log_1m = log_b - z
    if mask is not None:
        log_1m = jnp.where(mask, log_1m, 0.0)
    return log_b, log_1m


def _tile_mask(tq):
    row = lax.broadcasted_iota(jnp.int32, (tq, tq), 0)
    col = lax.broadcasted_iota(jnp.int32, (tq, tq), 1)
    return col < row


def _tri(tq, before):
    r = lax.broadcasted_iota(jnp.int32, (tq, tq), 0)
    c = lax.broadcasted_iota(jnp.int32, (tq, tq), 1)
    return jnp.where((r < c) if before else (r > c), 1.0, 0.0).astype(BF16)


def _attn_fwd(proj, after, n_heads, d_sb, tq=256):
    s = proj.shape[0]
    tq = _tile(s, tq)
    hb = d_sb // HEAD_DIM
    scale = 1.0 / math.sqrt(HEAD_DIM)

    per_step = min(HEADS_PER_STEP_FWD, n_heads)
    heads = [pl.ds(n * HEAD_DIM, HEAD_DIM) for n in range(per_step)]
    wide = per_step * HEAD_DIM
    hb //= per_step

    def body(q_ref, k_ref, v_ref, _, o_ref):
        i = pl.program_id(1)
        tri = _tri(tq, False)

        def tile(j, carry, diagonal):
            rows = pl.ds(pl.multiple_of(j * tq, tq), tq)
            mask = _tile_mask(tq) if diagonal else None
            out = []
            for hd, (c_l, acc) in zip(heads, carry):
                log_b, log_1m = _sb_tile(q_ref[:, hd].astype(BF16), k_ref[rows, hd].astype(BF16), scale, mask)
                a = jnp.exp2(log_b + (c_l + _sum_right(log_1m, tri)))
                if diagonal:
                    a = jnp.where(mask, a, 0.0)
                out.append((c_l + jnp.sum(log_1m, axis=1, keepdims=True),
                            acc + _dot_nn(a.astype(BF16), v_ref[rows, hd].astype(BF16))))
            return tuple(out)

        init = tuple((jnp.zeros((tq, 1), F32), jnp.zeros((tq, HEAD_DIM), F32)) for _ in heads)
        carry = lax.fori_loop(0, i, lambda jj, c: tile(i - 1 - jj, c, False), tile(i, init, True))
        for hd, (_, acc) in zip(heads, carry):
            o_ref[:, hd] = acc.astype(BF16)

    return pl.pallas_call(
        body, name="attn_fwd", grid=(n_heads // per_step, s // tq),
        in_specs=[pl.BlockSpec((tq, wide), lambda h, i: (i, h)),
                  pl.BlockSpec((s, wide), lambda h, i: (0, hb + h)),
                  pl.BlockSpec((s, wide), lambda h, i: (0, 2 * hb + h)), pl.BlockSpec(memory_space=pl.ANY)],
        out_specs=pl.BlockSpec((tq, wide), lambda h, i: (i, h)),
        out_shape=jax.ShapeDtypeStruct((s, d_sb), BF16),
        compiler_params=_params(dimension_semantics=("parallel", "arbitrary")),
    )(proj, proj, proj, after)


def _attn_bwd(proj, dattn, after, n_heads, d_sb, tq=256):
    s = proj.shape[0]
    tq = _tile(s, tq)
    nq = s // tq
    hb = d_sb // HEAD_DIM // HEADS_PER_STEP
    scale = 1.0 / math.sqrt(HEAD_DIM)
    heads = [pl.ds(n * HEAD_DIM, HEAD_DIM) for n in range(HEADS_PER_STEP)]
    wide = HEADS_PER_STEP * HEAD_DIM

    def body(q_ref, k_ref, v_ref, do_ref, _, dq_ref, dk_ref, dv_ref, sq_ref, sk_ref, sv_ref, dk_acc, dv_acc, g_st, b_st):
        i = pl.program_id(1)

        @pl.when(i == 0)
        def _():
            dk_acc[...] = jnp.zeros_like(dk_acc)
            dv_acc[...] = jnp.zeros_like(dv_acc)

        tri_after = _tri(tq, False)
        tri_before = _tri(tq, True)

        def newest_first(j, c_ls, diagonal):
            rows = pl.ds(pl.multiple_of(j * tq, tq), tq)
            mask = _tile_mask(tq) if diagonal else None
            out = []
            for n, (hd, c_l) in enumerate(zip(heads, c_ls)):
                do = do_ref[:, hd].astype(BF16)
                log_b, log_1m = _sb_tile(q_ref[:, hd].astype(BF16), k_ref[rows, hd].astype(BF16), scale, mask)
                a = jnp.exp2(log_b + (c_l + _sum_right(log_1m, tri_after)))
                beta = jnp.exp2(log_b)
                if diagonal:
                    a = jnp.where(mask, a, 0.0)
                    beta = jnp.where(mask, beta, 0.0)
                g_st[n, j] = a * _dot_nt(do, v_ref[rows, hd].astype(BF16))
                b_st[n, j] = beta
                dv_acc[rows, hd] += _dot_tn(a.astype(BF16), do)
                out.append(c_l + jnp.sum(log_1m, axis=1, keepdims=True))
            return tuple(out)

        lax.fori_loop(0, i, lambda jj, c: newest_first(i - 1 - jj, c, False),
                      newest_first(i, tuple(jnp.zeros((tq, 1), F32) for _ in heads), True))

        def oldest_first(j, carry):
            rows = pl.ds(pl.multiple_of(j * tq, tq), tq)
            out = []
            for n, (hd, (c_g, dq)) in enumerate(zip(heads, carry)):
                kb = k_ref[rows, hd].astype(BF16)
                g = g_st[n, j]
                beta = b_st[n, j]
                g_before = c_g + _sum_right(g, tri_before)
                dz16 = ((g * (1.0 - beta) - g_before * beta) * scale).astype(BF16)
                dk_acc[rows, hd] += _dot_tn(dz16, q_ref[:, hd].astype(BF16))
                out.append((c_g + jnp.sum(g, axis=1, keepdims=True), dq + _dot_nn(dz16, kb)))
            return tuple(out)

        init = tuple((jnp.zeros((tq, 1), F32), jnp.zeros((tq, HEAD_DIM), F32)) for _ in heads)
        carry = lax.fori_loop(0, i + 1, oldest_first, init)
        dq = jnp.concatenate([c[1] for c in carry], axis=1)
        dq_ref[...] = dq.astype(BF16)
        col = jnp.sum(dq, axis=0, keepdims=True)

        @pl.when(i == 0)
        def _():
            sq_ref[...] = col

        @pl.when(i > 0)
        def _():
            sq_ref[...] += col

        @pl.when(i == nq - 1)
        def _():
            dk = dk_acc[...]
            dv = dv_acc[...]
            dk_ref[...] = dk.astype(BF16)
            dv_ref[...] = dv.astype(BF16)
            sk_ref[...] = jnp.sum(dk, axis=0, keepdims=True)
            sv_ref[...] = jnp.sum(dv, axis=0, keepdims=True)

    qblk = pl.BlockSpec((tq, wide), lambda h, i: (i, h))
    kblk = pl.BlockSpec((s, wide), lambda h, i: (0, h))
    col_sum = pl.BlockSpec((1, wide), lambda h, i: (0, h))
    stash = pltpu.VMEM((HEADS_PER_STEP, nq, tq, tq), F32)
    outs = pl.pallas_call(
        body, name="attn_bwd", grid=(n_heads // HEADS_PER_STEP, nq),
        in_specs=[qblk,
                  pl.BlockSpec((s, wide), lambda h, i: (0, hb + h)),
                  pl.BlockSpec((s, wide), lambda h, i: (0, 2 * hb + h)),
                  qblk, pl.BlockSpec(memory_space=pl.ANY)],
        out_specs=[qblk, kblk, kblk, col_sum, col_sum, col_sum],
        out_shape=[jax.ShapeDtypeStruct((s, d_sb), BF16)] * 3 + [jax.ShapeDtypeStruct((1, d_sb), F32)] * 3,
        scratch_shapes=[pltpu.VMEM((s, wide), F32), pltpu.VMEM((s, wide), F32), stash, stash],
        compiler_params=_params(dimension_semantics=("parallel", "arbitrary")),
    )(proj, proj, proj, dattn, after)
    return outs


LANES = 128


def _glu(proj, off_a, dc, tr=256):
    s = proj.shape[0]
    ba = off_a // dc

    def body(a_ref, b_ref, u_ref):
        u_ref[...] = a_ref[...] * _sigmoid(b_ref[...])

    return pl.pallas_call(
        body, name="glu", grid=(s // tr,),
        in_specs=[pl.BlockSpec((tr, dc), lambda i: (i, ba)), pl.BlockSpec((tr, dc), lambda i: (i, ba + 1))],
        out_specs=pl.BlockSpec((tr, dc), lambda i: (i, 0)), out_shape=jax.ShapeDtypeStruct((s, dc), F32),
        compiler_params=_params(dimension_semantics=("parallel",)),
    )(proj, proj)


SUBLANES = 8


def _shifted_rows(src, cols, shifted, tt):
    span = tt + CONV_HALO - SUBLANES
    for b in range(1, SUBLANES):
        shifted[b, pl.ds(0, span), :] = src[pl.ds(b, span), cols]

    def rows(o):
        a, b = divmod(o, SUBLANES)
        return src[pl.ds(SUBLANES * a, tt), cols] if b == 0 else shifted[b, pl.ds(SUBLANES * a, tt), :]

    return rows


def _conv_fwd(u0, taps, b_dw, g_ln, b_ln, tt=256):
    s, dc = u0.shape
    tt = _tile(s, tt)
    hpb = tt // CONV_HALO

    def body(cur_ref, halo_ref, w_ref, bdw_ref, g_ref, b_ref, u1_ref, u3_ref, xs, xs_shifted):
        i = pl.program_id(0)
        xs[pl.ds(0, CONV_HALO), :] = jnp.where(i > 0, halo_ref[...], 0.0)
        xs[pl.ds(CONV_HALO, tt), :] = cur_ref[...]
        for c0 in range(0, dc, LANES):
            cols = pl.ds(c0, LANES)
            x_rows = _shifted_rows(xs, cols, xs_shifted, tt)
            acc = jnp.broadcast_to(bdw_ref[:, cols], (tt, LANES))
            for k in range(CONV_WIDTH):
                acc = acc + w_ref[pl.ds(k, 1), cols] * x_rows(CONV_HALO - (CONV_WIDTH - 1) + k)
            u1_ref[:, cols] = acc
        u1 = u1_ref[...]
        mu = jnp.mean(u1, axis=-1, keepdims=True)
        xc = u1 - mu
        rstd = lax.rsqrt(jnp.mean(xc * xc, axis=-1, keepdims=True) + EPS)
        u2 = xc * rstd * g_ref[...] + b_ref[...]
        u3_ref[...] = (u2 * _sigmoid(u2)).astype(BF16)

    vec = pl.BlockSpec((1, dc), lambda i: (0, 0))
    return pl.pallas_call(
        body, name="conv_fwd", grid=(s // tt,),
        in_specs=[pl.BlockSpec((tt, dc), lambda i: (i, 0)),
                  pl.BlockSpec((CONV_HALO, dc), lambda i: (jnp.maximum(i * hpb - 1, 0), 0)),
                  pl.BlockSpec((CONV_HALO, dc), lambda i: (0, 0)), vec, vec, vec],
        out_specs=[pl.BlockSpec((tt, dc), lambda i: (i, 0)), pl.BlockSpec((tt, dc), lambda i: (i, 0))],
        out_shape=[jax.ShapeDtypeStruct((s, dc), F32), jax.ShapeDtypeStruct((s, dc), BF16)],
        scratch_shapes=[pltpu.VMEM((tt + CONV_HALO, dc), F32), pltpu.VMEM((SUBLANES, tt + CONV_HALO, LANES), F32)],
        compiler_params=_params(dimension_semantics=("arbitrary",)),
    )(u0, u0, taps, b_dw, g_ln, b_ln)


def _conv_bwd_norm(u1, du3, g_ln, b_ln, tr=256):
    dc = u1.shape[1]

    def body(u1_ref, du3_ref, g_ref, b_ref, du1_ref, dg_ref, db_ref, dbdw_ref):
        u1v = u1_ref[...]
        g = g_ref[...]
        mu = jnp.mean(u1v, axis=-1, keepdims=True)
        xc = u1v - mu
        rstd = lax.rsqrt(jnp.mean(xc * xc, axis=-1, keepdims=True) + EPS)
        xhat = xc * rstd
        u2 = xhat * g + b_ref[...]
        sg = _sigmoid(u2)
        du2 = du3_ref[...] * (sg * (1.0 + u2 * (1.0 - sg)))
        _accumulate(dg_ref, jnp.sum(du2 * xhat, axis=0, keepdims=True))
        _accumulate(db_ref, jnp.sum(du2, axis=0, keepdims=True))
        gy = du2 * g
        du1 = rstd * (gy - jnp.mean(gy, axis=-1, keepdims=True) - xhat * jnp.mean(gy * xhat, axis=-1, keepdims=True))
        du1_ref[...] = du1
        _accumulate(dbdw_ref, jnp.sum(du1, axis=0, keepdims=True))

    return _row_call(body, name="conv_bwd_norm", rows=u1.shape[0], tr=tr,
                     ins=[(u1, "row"), (du3, "row"), (g_ln, "vec"), (b_ln, "vec")],
                     outs=[(dc, F32)], acc_outs=(dc, dc, dc))


def _conv_bwd_taps(du1, u0, proj, taps, off_a, tt=256):
    s, dc = u0.shape
    tt = _tile(s, tt)
    n = s // tt
    hpb = tt // CONV_HALO
    ba = off_a // dc
    lead = CONV_HALO - (CONV_WIDTH - 1)

    def body(d_ref, dnext_ref, cur_ref, halo_ref, a_ref, b_ref, w_ref, da_ref, db_ref, sa_ref, sb_ref, dw_ref,
             xs, ds, du0, xs_shifted, ds_shifted):
        i = pl.program_id(0)
        xs[pl.ds(0, CONV_HALO), :] = jnp.where(i > 0, halo_ref[...], 0.0)
        xs[pl.ds(CONV_HALO, tt), :] = cur_ref[...]
        ds[pl.ds(0, tt), :] = d_ref[...]
        ds[pl.ds(tt, CONV_HALO), :] = jnp.where(i < n - 1, dnext_ref[...], 0.0)

        @pl.when(i == 0)
        def _():
            dw_ref[...] = jnp.zeros_like(dw_ref)

        for c0 in range(0, dc, LANES):
            cols = pl.ds(c0, LANES)
            x_rows = _shifted_rows(xs, cols, xs_shifted, tt)
            d_rows = _shifted_rows(ds, cols, ds_shifted, tt)
            d_cur = d_rows(0)
            acc = jnp.zeros((tt, LANES), F32)
            for k in range(CONV_WIDTH):
                acc = acc + w_ref[pl.ds(k, 1), cols] * d_rows(CONV_WIDTH - 1 - k)
                dw_ref[pl.ds(k, 1), cols] += jnp.sum(d_cur * x_rows(lead + k), axis=0, keepdims=True)
            du0[:, cols] = acc
        d0 = du0[...]
        sg = _sigmoid(b_ref[...])
        da = d0 * sg
        db = d0 * a_ref[...] * sg * (1.0 - sg)
        da_ref[...] = da.astype(BF16)
        db_ref[...] = db.astype(BF16)
        _accumulate(sa_ref, jnp.sum(da, axis=0, keepdims=True))
        _accumulate(sb_ref, jnp.sum(db, axis=0, keepdims=True))

    row = pl.BlockSpec((tt, dc), lambda i: (i, 0))
    vec = pl.BlockSpec((1, dc), lambda i: (0, 0))
    taps_spec = pl.BlockSpec((CONV_HALO, dc), lambda i: (0, 0))
    return pl.pallas_call(
        body, name="conv_bwd_taps", grid=(n,),
        in_specs=[row, pl.BlockSpec((CONV_HALO, dc), lambda i: (jnp.minimum((i + 1) * hpb, n * hpb - 1), 0)),
                  row, pl.BlockSpec((CONV_HALO, dc), lambda i: (jnp.maximum(i * hpb - 1, 0), 0)),
                  pl.BlockSpec((tt, dc), lambda i: (i, ba)), pl.BlockSpec((tt, dc), lambda i: (i, ba + 1)), taps_spec],
        out_specs=[row, row, vec, vec, taps_spec],
        out_shape=[jax.ShapeDtypeStruct((s, dc), BF16), jax.ShapeDtypeStruct((s, dc), BF16),
                   jax.ShapeDtypeStruct((1, dc), F32), jax.ShapeDtypeStruct((1, dc), F32),
                   jax.ShapeDtypeStruct((CONV_HALO, dc), F32)],
        scratch_shapes=[pltpu.VMEM((tt + CONV_HALO, dc), F32), pltpu.VMEM((tt + CONV_HALO, dc), F32),
                        pltpu.VMEM((tt, dc), F32),
                        pltpu.VMEM((SUBLANES, tt + CONV_HALO, LANES), F32), pltpu.VMEM((SUBLANES, tt + CONV_HALO, LANES), F32)],
        compiler_params=_params(dimension_semantics=("arbitrary",)),
    )(du1, du1, u0, u0, proj, proj, taps)


def _elementwise(body, *, name, ins, out_dtypes, after, tr=128):
    rows, cols = ins[0].shape
    tr = _tile(rows, tr, 8)
    spec = pl.BlockSpec((tr, cols), lambda i: (i, 0))
    n_in = len(ins)

    def with_after(*refs):
        body(*refs[:n_in], *refs[n_in + 1:])

    return pl.pallas_call(
        with_after, name=name, grid=(rows // tr,), in_specs=[spec] * n_in + [pl.BlockSpec(memory_space=pl.ANY)],
        out_specs=[spec] * len(out_dtypes),
        out_shape=[jax.ShapeDtypeStruct((rows, cols), dt) for dt in out_dtypes],
        compiler_params=_params(dimension_semantics=("parallel",)),
    )(*ins, after)


def _prefetched(body, *, name, where, grid, in_specs, out_specs, out_shape, operands):
    return pl.pallas_call(
        body, name=name, out_shape=out_shape,
        grid_spec=pltpu.PrefetchScalarGridSpec(num_scalar_prefetch=1, grid=grid, in_specs=in_specs, out_specs=out_specs),
        compiler_params=_params(dimension_semantics=("parallel",) * len(grid)),
    )(where, *operands)


def _cast_into_block(w, where, after, name, tr=256):
    rows, cols = w.shape
    tr = _tile(rows, tr, 16)

    def body(where_ref, w_ref, _, o_ref):
        o_ref[...] = w_ref[...].astype(BF16)

    return _prefetched(
        body, name=name, where=where, grid=(rows // tr,),
        in_specs=[pl.BlockSpec((tr, cols), lambda i, wh: (i, 0)), pl.BlockSpec(memory_space=pl.ANY)],
        out_specs=pl.BlockSpec((None, tr, cols), lambda i, wh: (wh[0], i, 0)),
        out_shape=jax.ShapeDtypeStruct((N_CHIPS, rows, cols), BF16), operands=[w, after])


def _add_pair(grad, theirs, where, name, tr=512):
    nb, half, cols = theirs.shape
    tr = _tile(half, tr, 16)
    nh = half // tr

    def body(where_ref, a_ref, b_ref, o_ref):
        o_ref[...] = a_ref[...] + b_ref[...]

    blk = pl.BlockSpec((None, tr, cols), lambda b, i, wh: (b, i, 0))
    return _prefetched(
        body, name=name, where=where, grid=(nb, nh),
        in_specs=[pl.BlockSpec((None, tr, cols), lambda b, i, wh: (b, wh[1] * nh + i, 0)), blk],
        out_specs=blk, out_shape=jax.ShapeDtypeStruct(theirs.shape, BF16), operands=[grad, theirs])


def _sum_slots(part, slots, where, name, tr=512):
    _, half, cols = part.shape
    tr = _tile(half, tr, 16)
    nh = half // tr

    def body(where_ref, p_ref, s_ref, o_ref):
        o_ref[...] = ((p_ref[...].astype(F32) + s_ref[0].astype(F32)) + s_ref[1].astype(F32)) + s_ref[2].astype(F32)

    return _prefetched(
        body, name=name, where=where, grid=(nh,),
        in_specs=[pl.BlockSpec((None, tr, cols), lambda i, wh: (wh[0], i, 0)),
                  pl.BlockSpec((3, tr, cols), lambda i, wh: (0, i, 0))],
        out_specs=pl.BlockSpec((tr, cols), lambda i, wh: (wh[1] * nh + i, 0)),
        out_shape=jax.ShapeDtypeStruct((2 * half, cols), F32), operands=[part, slots])


def _adamw(w, g, m, v, after, name):
    bc1 = 1.0 - ADAM_B1 ** ADAM_STEP
    bc2 = 1.0 - ADAM_B2 ** ADAM_STEP

    def body(w_ref, g_ref, m_ref, v_ref, go_ref, d_ref, mo_ref, vo_ref):
        g = g_ref[...]
        m_new = ADAM_B1 * m_ref[...] + (1.0 - ADAM_B1) * g
        v_new = ADAM_B2 * v_ref[...] + (1.0 - ADAM_B2) * (g * g)
        m_hat = m_new / bc1
        v_hat = v_new / bc2
        go_ref[...] = g
        d_ref[...] = -ADAM_LR * (m_hat / (jnp.sqrt(v_hat) + ADAM_EPS) + ADAM_WD * w_ref[...])
        mo_ref[...] = m_new
        vo_ref[...] = v_new

    return _elementwise(body, name=name, ins=[w, g, m, v], out_dtypes=[F32] * 4, after=after, tr=128)


ANY = pl.BlockSpec(memory_space=pl.ANY)


def _mesh_pos():
    return lax.axis_index("x"), lax.axis_index("y"), lax.axis_index("c")


def _other_chips(x, y):
    return [(1 - x, y), (x, 1 - y), (1 - x, 1 - y)]


def _remote(src, dst, send_sems, recv_sems, idx, to):
    return pltpu.make_async_remote_copy(src_ref=src, dst_ref=dst, send_sem=send_sems.at[idx], recv_sem=recv_sems.at[idx],
                                        device_id=to, device_id_type=MESH)


HBM = pl.BlockSpec(memory_space=pltpu.HBM)
SEM = pl.BlockSpec(memory_space=pltpu.SEMAPHORE)
EFFECT = pltpu.SideEffectType.DATAFLOW_SIDE_EFFECTING


def _in_hbm(arrays):
    return [pltpu.with_memory_space_constraint(a, pltpu.HBM) for a in arrays]


def _ici_copies(slab, send_sems, recv_sems, x, y, c):
    half = slab.shape[1] // 2
    rows = pl.ds(c * half, half)
    mine = slab.at[2 * x + y, rows]
    out = []
    for j, (cx, cy) in enumerate(_other_chips(x, y)):
        got = slab.at[2 * cx + cy, rows]
        out.append((_remote(mine, mine, send_sems, recv_sems, j, (cx, cy, c)),
                    _remote(got, got, send_sems, recv_sems, j, (cx, cy, c))))
    return out


def _gather_start(slabs, after, name):
    n = len(slabs)

    def body(*refs):
        ins = refs[:n]
        send_sems, recv_sems, token = refs[2 * n + 1:3 * n + 1], refs[3 * n + 1:4 * n + 1], refs[4 * n + 1]
        x, y, c = _mesh_pos()
        for w in range(n):
            for send, _ in _ici_copies(ins[w], send_sems[w], recv_sems[w], x, y, c):
                send.start()
        token[...] = jnp.zeros_like(token)

    outs = pl.pallas_call(
        body, name=name, in_specs=[HBM] * n + [ANY],
        out_specs=[HBM] * n + [SEM] * (2 * n) + [pl.BlockSpec(memory_space=pltpu.VMEM)],
        out_shape=[pltpu.HBM(s.shape, s.dtype) for s in slabs] + [pltpu.SemaphoreType.DMA((3,))] * (2 * n)
        + [jax.ShapeDtypeStruct((8, 128), F32)],
        input_output_aliases={w: w for w in range(n)},
        compiler_params=_params(has_side_effects=EFFECT),
    )(*_in_hbm(slabs), after)
    return (outs[:n], outs[n:2 * n], outs[2 * n:3 * n]), outs[3 * n]


def _gather_wait(slabs, send_sems, recv_sems, after, name):
    n = len(slabs)

    def body(*refs):
        ins = refs[:n]
        sends, recvs = refs[n:2 * n], refs[2 * n:3 * n]
        x, y, c = _mesh_pos()
        for w in range(n):
            for send, recv in _ici_copies(ins[w], sends[w], recvs[w], x, y, c):
                send.wait_send()
                recv.wait_recv()

    return pl.pallas_call(
        body, name=name, in_specs=[HBM] * n + [SEM] * (2 * n) + [ANY], out_specs=[HBM] * n,
        out_shape=[pltpu.HBM(s.shape, s.dtype) for s in slabs],
        input_output_aliases={w: w for w in range(n)},
        compiler_params=_params(has_side_effects=EFFECT),
    )(*slabs, *send_sems, *recv_sems, after)


def _phase_start(arrays, copies, n_copies, after, name):
    n = len(arrays)

    def body(*refs):
        send_sems, recv_sems, token = refs[2 * n + 1], refs[2 * n + 2], refs[2 * n + 3]
        for send, _ in copies(refs[:n], send_sems, recv_sems, *_mesh_pos()):
            send.start()
        token[...] = jnp.zeros_like(token)

    outs = pl.pallas_call(
        body, name=name, in_specs=[HBM] * n + [ANY],
        out_specs=[HBM] * n + [SEM, SEM, pl.BlockSpec(memory_space=pltpu.VMEM)],
        out_shape=[pltpu.HBM(a.shape, a.dtype) for a in arrays]
        + [pltpu.SemaphoreType.DMA((n_copies,))] * 2 + [jax.ShapeDtypeStruct((8, 128), F32)],
        input_output_aliases={k: k for k in range(n)},
        compiler_params=_params(has_side_effects=EFFECT),
    )(*_in_hbm(arrays), after)
    return outs[:n], outs[n], outs[n + 1], outs[n + 2]


def _phase_wait(arrays, send_sems, recv_sems, copies, after, name):
    n = len(arrays)

    def body(*refs):
        for send, recv in copies(refs[:n], refs[n], refs[n + 1], *_mesh_pos()):
            send.wait_send()
            recv.wait_recv()

    return pl.pallas_call(
        body, name=name, in_specs=[HBM] * n + [SEM, SEM, ANY], out_specs=[HBM] * n,
        out_shape=[pltpu.HBM(a.shape, a.dtype) for a in arrays],
        input_output_aliases={k: k for k in range(n)},
        compiler_params=_params(has_side_effects=EFFECT),
    )(*arrays, send_sems, recv_sems, after)


def _forward_copies(slabs, send_sems, recv_sems, x, y, c):
    sibling = (x, y, 1 - c)
    out = []
    for w, slab in enumerate(slabs):
        half = slab.shape[1] // 2
        for j, (cx, cy) in enumerate(_other_chips(x, y)):
            mine = slab.at[2 * cx + cy, pl.ds(c * half, half)]
            theirs = slab.at[2 * cx + cy, pl.ds((1 - c) * half, half)]
            out.append((_remote(mine, mine, send_sems, recv_sems, 3 * w + j, sibling),
                        _remote(theirs, theirs, send_sems, recv_sems, 3 * w + j, sibling)))
    return out


def _share_copies(halves, send_sems, recv_sems, x, y, c):
    sibling = (x, y, 1 - c)
    out = []
    for w, full in enumerate(halves):
        h = full.shape[0] // 2
        mine, theirs = full.at[pl.ds(c * h, h)], full.at[pl.ds((1 - c) * h, h)]
        out.append((_remote(mine, mine, send_sems, recv_sems, w, sibling),
                    _remote(theirs, theirs, send_sems, recv_sems, w, sibling)))
    return out


def _exchange_copies(arrays, send_sems, recv_sems, x, y, c):
    n = len(arrays) // 2
    sibling = (x, y, 1 - c)
    out = []
    for w in range(n):
        grad, land = arrays[w], arrays[n + w]
        half = grad.shape[1] // 2
        out.append((_remote(grad.at[:, pl.ds((1 - c) * half, half), :], land, send_sems, recv_sems, w, sibling),
                    _remote(land, land, send_sems, recv_sems, w, sibling)))
    return out


def _gather_taps(taps):
    def body(in_ref, out_ref, send_sems, recv_sems, local_sem):
        x, y, c = _mesh_pos()
        me = 2 * x + y
        chips = _other_chips(x, y)
        lc = pltpu.make_async_copy(in_ref, out_ref.at[me], local_sem)
        lc.start()
        sends = [_remote(in_ref, out_ref.at[me], send_sems, recv_sems, j, (cx, cy, c)) for j, (cx, cy) in enumerate(chips)]
        for cp in sends:
            cp.start()
        for j, (cx, cy) in enumerate(chips):
            got = out_ref.at[2 * cx + cy]
            _remote(got, got, send_sems, recv_sems, j, (cx, cy, c)).wait_recv()
        for cp in sends:
            cp.wait_send()
        lc.wait()

    vm = pl.BlockSpec(memory_space=pltpu.VMEM)
    return pl.pallas_call(
        body, name="gather_taps", in_specs=[vm], out_specs=vm,
        out_shape=jax.ShapeDtypeStruct((N_CHIPS,) + taps.shape, taps.dtype),
        scratch_shapes=[pltpu.SemaphoreType.DMA((3,)), pltpu.SemaphoreType.DMA((3,)), pltpu.SemaphoreType.DMA],
        compiler_params=_params(has_side_effects=True),
    )(taps)


def _scatter_copies(part, slots, send_sems, recv_sems, x, y, c):
    out = []
    for j, (cx, cy) in enumerate(_other_chips(x, y)):
        out.append((_remote(part.at[2 * cx + cy], slots.at[j], send_sems, recv_sems, j, (cx, cy, c)),
                    _remote(slots.at[j], slots.at[j], send_sems, recv_sems, j, (cx, cy, c))))
    return out


def _scatter_start(parts, after, name):
    n = len(parts)
    slots = [lax.empty((3,) + p.shape[1:], p.dtype) for p in parts]

    def body(*refs):
        ins, lands = refs[:n], refs[n:2 * n]
        send_sems, recv_sems, token = refs[4 * n + 1:5 * n + 1], refs[5 * n + 1:6 * n + 1], refs[6 * n + 1]
        x, y, c = _mesh_pos()
        for w in range(n):
            for send, _ in _scatter_copies(ins[w], lands[w], send_sems[w], recv_sems[w], x, y, c):
                send.start()
        token[...] = jnp.zeros_like(token)

    outs = pl.pallas_call(
        body, name=name, in_specs=[HBM] * (2 * n) + [ANY],
        out_specs=[HBM] * (2 * n) + [SEM] * (2 * n) + [pl.BlockSpec(memory_space=pltpu.VMEM)],
        out_shape=[pltpu.HBM(a.shape, a.dtype) for a in list(parts) + slots] + [pltpu.SemaphoreType.DMA((3,))] * (2 * n)
        + [jax.ShapeDtypeStruct((8, 128), F32)],
        input_output_aliases={k: k for k in range(2 * n)},
        compiler_params=_params(has_side_effects=EFFECT),
    )(*_in_hbm(list(parts) + slots), after)
    return (outs[:n], outs[n:2 * n], outs[2 * n:3 * n], outs[3 * n:4 * n]), outs[4 * n]


def _scatter_wait(parts, slots, send_sems, recv_sems, after, name):
    n = len(parts)

    def body(*refs):
        ins, lands = refs[:n], refs[n:2 * n]
        sends, recvs = refs[2 * n:3 * n], refs[3 * n:4 * n]
        x, y, c = _mesh_pos()
        for w in range(n):
            for send, recv in _scatter_copies(ins[w], lands[w], sends[w], recvs[w], x, y, c):
                send.wait_send()
                recv.wait_recv()

    outs = pl.pallas_call(
        body, name=name, in_specs=[HBM] * (2 * n) + [SEM] * (2 * n) + [ANY], out_specs=[HBM] * (2 * n),
        out_shape=[pltpu.HBM(a.shape, a.dtype) for a in list(parts) + list(slots)],
        input_output_aliases={k: k for k in range(2 * n)},
        compiler_params=_params(has_side_effects=EFFECT),
    )(*parts, *slots, *send_sems, *recv_sems, after)
    return outs[:n], outs[n:]


def _all_reduce_small(buf, after, name):
    rows, width = buf.shape

    def body(in_ref, _, out_ref, gathered, send_sems, recv_sems):
        x, y, c = _mesh_pos()
        me = 4 * x + 2 * y + c
        gathered[me] = in_ref[...]
        flips = [(dx, dy, dz) for dx in (0, 1) for dy in (0, 1) for dz in (0, 1)][1:]
        peers = [((1 - x) if dx else x, (1 - y) if dy else y, (1 - c) if dz else c) for dx, dy, dz in flips]
        sends = [_remote(in_ref, gathered.at[me], send_sems, recv_sems, k, peer) for k, peer in enumerate(peers)]
        for cp in sends:
            cp.start()
        for k, (px, py, pc) in enumerate(peers):
            got = gathered.at[4 * px + 2 * py + pc]
            _remote(got, got, send_sems, recv_sems, k, (px, py, pc)).wait_recv()
        for cp in sends:
            cp.wait_send()
        total = gathered[0]
        for d in range(1, 8):
            total = total + gathered[d]
        out_ref[...] = total

    vm = pl.BlockSpec(memory_space=pltpu.VMEM)
    return pl.pallas_call(
        body, name=name, in_specs=[vm, ANY], out_specs=vm, out_shape=jax.ShapeDtypeStruct(buf.shape, F32),
        scratch_shapes=[pltpu.VMEM((8, rows, width), F32), pltpu.SemaphoreType.DMA((7,)), pltpu.SemaphoreType.DMA((7,))],
        compiler_params=_params(has_side_effects=True),
    )(buf, after)


PACK_W = 2048
PACK_ROWS = 8


def _pack(vectors):
    flat = jnp.concatenate([v.reshape(-1) for v in vectors])
    unit = PACK_W * PACK_ROWS
    total = -(-flat.shape[0] // unit) * unit
    return jnp.pad(flat, (0, total - flat.shape[0])).reshape(total // PACK_W, PACK_W)


def _unpack(buf, shapes):
    flat = buf.reshape(-1)
    out, pos = [], 0
    for shp in shapes:
        size = math.prod(shp)
        out.append(flat[pos:pos + size].reshape(shp))
        pos += size
    return out


def _pad_rows(a, rows):
    return jnp.pad(a, ((0, rows - a.shape[0]), (0, 0)))


def kernel(x, g_pre_mix, w_in, b_in, w_dw, b_dw, g_conv_ln, b_conv_ln, w_sb_out, w_conv_out, w_o, g_post_mix, g_pre_mlp, w_up, w_down, g_post_mlp, loss_target, m_g_pre_mix, m_w_in, m_b_in, m_w_dw, m_b_dw, m_g_conv_ln, m_b_conv_ln, m_w_sb_out, m_w_conv_out, m_w_o, m_g_post_mix, m_g_pre_mlp, m_w_up, m_w_down, m_g_post_mlp, v_g_pre_mix, v_w_in, v_b_in, v_w_dw, v_b_dw, v_g_conv_ln, v_b_conv_ln, v_w_sb_out, v_w_conv_out, v_w_o, v_g_post_mix, v_g_pre_mlp, v_w_up, v_w_down, v_g_post_mlp):
    xs = x[0]
    tgt = loss_target[0]
    s, d = xs.shape
    d_sb = w_sb_out.shape[1]
    dc = w_conv_out.shape[1]
    d_in = w_in.shape[2] * N_CHIPS
    n_heads = d_sb // HEAD_DIM
    off_a = 3 * d_sb
    off_sb = off_a + 2 * dc
    off_cv = off_sb + d
    gw = dc
    assert off_a % dc == 0 and off_sb % gw == 0 and d % gw == 0 and d_in == off_cv + d
    chip = 2 * lax.axis_index("x") + lax.axis_index("y")

    big = dict(w_in=w_in[0], w_sb_out=w_sb_out[0], w_conv_out=w_conv_out[0], w_o=w_o[0], w_up=w_up[0], w_down=w_down[0])
    names = list(big)
    col_sharded = {"w_in", "w_sb_out", "w_conv_out", "w_up"}
    where = jnp.stack([chip, lax.axis_index("c")]).astype(jnp.int32)
    cs = w_dw.shape[2]
    taps4 = _gather_taps(_pad_rows(w_dw[0], CONV_HALO))
    first, token = _gather_start([_cast_into_block(big["w_in"], where, taps4, "cast_w_in")], taps4, "gather_start_in")
    rest, rest_token = _gather_start([_cast_into_block(big[n], where, token, "cast_" + n) for n in names[1:]], token,
                            "gather_start_rest")
    in_flight = dict(zip(names, zip(*[a + b for a, b in zip(first, rest)])))
    gathered = {}

    def land(group, after, tag):
        thru, sends, recvs = zip(*[in_flight[n] for n in group])
        landed = _gather_wait(thru, sends, recvs, after, "gather_wait_" + tag)
        return _phase_start(landed, _forward_copies, 3 * len(group), where, "forward_start_" + tag)

    def ready(group, started, after, tag):
        slabs, sends, recvs, _ = started
        for n, g4 in zip(group, _phase_wait(slabs, sends, recvs, _forward_copies, after, "forward_wait_" + tag)):
            gathered[n] = g4 if n in col_sharded else g4.reshape(g4.shape[0] * g4.shape[1], g4.shape[2])

    taps = jnp.transpose(taps4, (1, 0, 2)).reshape(CONV_HALO, N_CHIPS * cs)
    group_mix = ["w_sb_out", "w_conv_out", "w_o"]

    h = _rms_fwd(xs, g_pre_mix)
    started = land(["w_in"], rest_token, "in")
    ready(["w_in"], started, h, "in")
    proj = _mm(h, gathered["w_in"], name="mm_proj", b_groups=N_CHIPS, bias=b_in, tn=768)
    mix_fwd = land(group_mix, proj, "mix")
    attn = _attn_fwd(proj, mix_fwd[3], n_heads, d_sb)
    u0 = _glu(proj, off_a, dc)
    u1, u3 = _conv_fwd(u0, taps, b_dw, g_conv_ln, b_conv_ln)
    ready(group_mix, mix_fwd, u3, "mix")
    o_sb = _mm(attn, gathered["w_sb_out"], name="mm_o_sb", b_groups=N_CHIPS)
    up_fwd = land(["w_up"], o_sb, "up")
    o_cv = _mm(u3, gathered["w_conv_out"], name="mm_o_cv", b_groups=N_CHIPS, after=up_fwd[3])
    merged = _merge_fwd(proj, o_sb, o_cv, off_sb, off_cv, gw)
    y = _mm(merged, gathered["w_o"], name="mm_y")
    x2, h2 = _resid_rms(xs, y, g_post_mix, g_pre_mlp)
    ready(["w_up"], up_fwd, h2, "up")
    down_fwd = land(["w_down"], h2, "down")
    up, f = _mm(h2, gathered["w_up"], name="mm_up", b_groups=N_CHIPS, out_dtypes=(F32, BF16), after=down_fwd[3],
                epilogue=lambda acc: (acc, jnp.square(jnp.maximum(acc, 0.0))))
    ready(["w_down"], down_fwd, f, "down")
    dn = _mm(f, gathered["w_down"], name="mm_down")
    dx3, d_dn, dg_post_mlp, loss_part = _final(x2, dn, g_post_mlp, tgt)

    moments = dict(w_in=(m_w_in, v_w_in), w_sb_out=(m_w_sb_out, v_w_sb_out), w_conv_out=(m_w_conv_out, v_w_conv_out),
                   w_o=(m_w_o, v_w_o), w_up=(m_w_up, v_w_up), w_down=(m_w_down, v_w_down))
    out_g, out_d, out_m, out_v = {}, {}, {}, {}
    grads = {}

    def blocks_of(group):
        out = []
        for n in group:
            g = grads[n]
            out.append(g if g.ndim == 3 else g.reshape(N_CHIPS, g.shape[0] // N_CHIPS, g.shape[1]))
        return out

    def exchange_begin(group, tag):
        blocks = blocks_of(group)
        lands = [lax.empty((b.shape[0], b.shape[1] // 2, b.shape[2]), b.dtype) for b in blocks]
        return _phase_start(blocks + lands, _exchange_copies, len(group), where, "exchange_start_" + tag)

    def scatter_begin(group, tag, exchange, after):
        arrays, sends, recvs, _ = exchange
        arrays = _phase_wait(arrays, sends, recvs, _exchange_copies, after, "exchange_wait_" + tag)
        n = len(group)
        pair_sums = [_add_pair(g, t, where, "pair_sum_" + m) for m, g, t in zip(group, arrays[:n], arrays[n:])]
        return _scatter_start(pair_sums, where, "scatter_start_" + tag)

    def reduce_sum(group, tag, started, after):
        parts, slots = _scatter_wait(*started, after, "scatter_wait_" + tag)
        halves = [_sum_slots(p, t, where, "chip_sum_" + n) for n, p, t in zip(group, parts, slots)]
        return _phase_start(halves, _share_copies, len(group), where, "share_start_" + tag)

    def reduce_update(group, tag, shared, after):
        halves, sends, recvs, _ = shared
        for n, red in zip(group, _phase_wait(halves, sends, recvs, _share_copies, after, "share_wait_" + tag)):
            mm_, vv_ = moments[n]
            res = _adamw(big[n], red, mm_[0], vv_[0], after, "adamw_" + n)
            out_g[n], out_d[n], out_m[n], out_v[n] = [r[None] for r in res]
            after = res[1]
        return after

    group_mlp, group_mix, group_in = ["w_down", "w_up"], ["w_o", "w_sb_out", "w_conv_out"], ["w_in"]
    grads["w_down"] = _mm(f, d_dn, name="mm_dw_down", ta=True, out_dtypes=(BF16,))
    dup = _mm(d_dn, gathered["w_down"], name="mm_df", tb=True, extra=[up], out_dtypes=(BF16,),
              epilogue=lambda acc, upv: (acc * (2.0 * jnp.maximum(upv, 0.0)),))
    grads["w_up"] = _mm(h2, dup, name="mm_dw_up", ta=True, out_groups=N_CHIPS, out_dtypes=(BF16,))
    mlp_exchange = exchange_begin(group_mlp, "mlp")
    dh2 = _mm(dup, gathered["w_up"], name="mm_dh2", tb=True, b_groups=N_CHIPS, after=mlp_exchange[3])
    dx2, dy, dg_pre_mlp, dg_post_mix = _rms_bwd2(dx3, dh2, x2, y, g_post_mix, g_pre_mlp)
    mlp_started, mlp_token = scatter_begin(group_mlp, "mlp", mlp_exchange, dx2)
    grads["w_o"] = _mm(merged, dy, name="mm_dw_o", ta=True, out_dtypes=(BF16,), after=mlp_token)
    dmerged = _mm(dy, gathered["w_o"], name="mm_dmerged", tb=True)
    do_sb, do_cv, dgate_sb, dgate_cv, s_gate_sb, s_gate_cv = _merge_bwd(proj, o_sb, o_cv, dmerged, off_sb, off_cv, gw)
    grads["w_sb_out"] = _mm(attn, do_sb, name="mm_dw_sb", ta=True, out_groups=N_CHIPS, out_dtypes=(BF16,))
    dattn = _mm(do_sb, gathered["w_sb_out"], name="mm_dattn", tb=True, b_groups=N_CHIPS)
    grads["w_conv_out"] = _mm(u3, do_cv, name="mm_dw_cv", ta=True, out_groups=N_CHIPS, out_dtypes=(BF16,))
    mix_exchange = exchange_begin(group_mix, "mix")
    du3 = _mm(do_cv, gathered["w_conv_out"], name="mm_du3", tb=True, b_groups=N_CHIPS, after=mix_exchange[3])
    mix_started, mix_token = scatter_begin(group_mix, "mix", mix_exchange, du3)
    dq, dk, dv, s_q, s_k, s_v = _attn_bwd(proj, dattn, mix_token, n_heads, d_sb)
    du1, dg_conv_ln, db_conv_ln, db_dw = _conv_bwd_norm(u1, du3, g_conv_ln, b_conv_ln)
    dglu_a, dglu_b, s_a, s_b, dtaps = _conv_bwd_taps(du1, u0, proj, taps, off_a)
    dproj = jnp.concatenate([dq, dk, dv, dglu_a, dglu_b, dgate_sb, dgate_cv], axis=1)
    db_in = jnp.concatenate([s_q, s_k, s_v, s_a, s_b, s_gate_sb, s_gate_cv], axis=1)
    small_w = dict(g_pre_mix=g_pre_mix, b_in=b_in, b_dw=b_dw, g_conv_ln=g_conv_ln, b_conv_ln=b_conv_ln,
                   g_post_mix=g_post_mix, g_pre_mlp=g_pre_mlp, g_post_mlp=g_post_mlp)
    small_m = dict(g_pre_mix=m_g_pre_mix, b_in=m_b_in, b_dw=m_b_dw, g_conv_ln=m_g_conv_ln, b_conv_ln=m_b_conv_ln,
                   g_post_mix=m_g_post_mix, g_pre_mlp=m_g_pre_mlp, g_post_mlp=m_g_post_mlp)
    small_v = dict(g_pre_mix=v_g_pre_mix, b_in=v_b_in, b_dw=v_b_dw, g_conv_ln=v_g_conv_ln, b_conv_ln=v_b_conv_ln,
                   g_post_mix=v_g_post_mix, g_pre_mlp=v_g_pre_mlp, g_post_mlp=v_g_post_mlp)
    small_g = dict(b_in=db_in, b_dw=db_dw, g_conv_ln=dg_conv_ln, b_conv_ln=db_conv_ln,
                   g_post_mix=dg_post_mix, g_pre_mlp=dg_pre_mlp, g_post_mlp=dg_post_mlp)

    def small_update(group, tails, after, tag):
        shapes = [small_w[n].shape for n in group] + [t.shape for t in tails]
        summed = _all_reduce_small(_pack([small_g[n] for n in group] + tails), after, "all_reduce_small_" + tag)
        zeros_tail = [jnp.zeros(t.shape, F32) for t in tails]
        res = _adamw(_pack([small_w[n] for n in group] + zeros_tail), summed,
                     _pack([small_m[n] for n in group] + zeros_tail), _pack([small_v[n] for n in group] + zeros_tail),
                     summed, "adamw_small_" + tag)
        unpacked = [_unpack(r, shapes) for r in res]
        for i, n in enumerate(group):
            out_g[n], out_d[n], out_m[n], out_v[n] = [u[i] for u in unpacked]
        return summed, unpacked[0][len(group):]

    early = ["b_in", "b_dw", "g_conv_ln", "b_conv_ln", "g_post_mix", "g_pre_mlp", "g_post_mlp"]
    summed, (loss_sum, taps_sum) = small_update(early, [loss_part, dtaps], dproj, "early")
    loss = loss_sum[0, 0]
    taps_grad = lax.dynamic_slice(taps_sum, (0, chip * cs), (CONV_HALO, cs))
    res = _adamw(_pad_rows(w_dw[0], CONV_HALO), taps_grad, _pad_rows(m_w_dw[0], CONV_HALO), _pad_rows(v_w_dw[0], CONV_HALO),
                 taps_grad, "adamw_taps")
    out_g["w_dw"], out_d["w_dw"], out_m["w_dw"], out_v["w_dw"] = [r[:CONV_WIDTH][None] for r in res]
    grads["w_in"] = _mm(h, dproj, name="mm_dw_in", ta=True, out_groups=N_CHIPS, out_dtypes=(BF16,), tn=768, after=summed)
    in_exchange = exchange_begin(group_in, "in")
    dh = _mm(dproj, gathered["w_in"], name="mm_dh", tb=True, b_groups=N_CHIPS, tk=2304, after=in_exchange[3])
    in_started, in_token = scatter_begin(group_in, "in", in_exchange, dh)
    grad_x, dg_pre_mix = _rms_bwd1(dx2, dh, xs, g_pre_mix)
    mlp_shared = reduce_sum(group_mlp, "mlp", mlp_started, in_token)
    mix_shared = reduce_sum(group_mix, "mix", mix_started, mlp_shared[3])
    done = reduce_update(group_mlp, "mlp", mlp_shared, mix_shared[3])
    done = reduce_update(group_mix, "mix", mix_shared, done)
    small_g["g_pre_mix"] = dg_pre_mix
    summed, _ = small_update(["g_pre_mix"], [], done, "late")
    in_shared = reduce_sum(group_in, "in", in_started, summed)
    reduce_update(group_in, "in", in_shared, in_shared[3])

    order = ["g_pre_mix", "w_in", "b_in", "w_dw", "b_dw", "g_conv_ln", "b_conv_ln", "w_sb_out", "w_conv_out", "w_o",
             "g_post_mix", "g_pre_mlp", "w_up", "w_down", "g_post_mlp"]
    return (loss, grad_x[None], *[out_g[n] for n in order], *[out_d[n] for n in order],
            *[out_m[n] for n in order], *[out_v[n] for n in order])
```

```python
import math

import jax
import jax.numpy as jnp
from jax import lax
from jax.experimental import pallas as pl
from jax.experimental.pallas import tpu as pltpu

F32 = jnp.float32
BF16 = jnp.bfloat16
MESH = pl.DeviceIdType.MESH

HEAD_DIM = 128
CONV_WIDTH = 31
CONV_HALO = 32
EPS = 1e-6
ADAM_LR = 0.001
ADAM_B1 = 0.9
ADAM_B2 = 0.999
ADAM_EPS = 1e-08
ADAM_WD = 0.01
ADAM_STEP = 10
N_CHIPS = 4
VMEM_LIMIT = 48 * 1024 * 1024


def _params(**kw):
    return pltpu.CompilerParams(vmem_limit_bytes=VMEM_LIMIT, **kw)


def _tile(n, want, align=128):
    if n <= want:
        return n
    for t in range(want - want % align, 0, -align):
        if n % t == 0:
            return t
    raise ValueError((n, want, align))


def _mm(a, b, *, name, ta=False, tb=False, b_groups=1, out_groups=1, bias=None, extra=None, epilogue=None,
        out_dtypes=(F32,), tm=1024, tn=1024, tk=2048, after=None):
    if ta:
        K, M = a.shape
    else:
        M, K = a.shape
    if b_groups == 1:
        br, bc = b.shape
    else:
        _, br, bcg = b.shape
        bc = bcg * b_groups
    if tb:
        N, Kb = br, bc
    else:
        Kb, N = br, bc
    assert Kb == K, (name, a.shape, b.shape)
    tm, tn, tk = _tile(M, tm), _tile(N, tn), _tile(K, tk)
    if b_groups > 1:
        if tb:
            tk = _tile(K // b_groups, tk)
        else:
            tn = _tile(N // b_groups, tn)
    if out_groups > 1:
        tn = _tile(N // out_groups, tn)
        if b_groups > 1 and not tb:
            tn = _tile(N // b_groups, tn)
    nm, nn, nk = M // tm, N // tn, K // tk
    extra = tuple(extra or ())
    n_extra = len(extra)
    has_bias = bias is not None
    n_out = len(out_dtypes)

    a_spec = pl.BlockSpec((tk, tm), lambda i, j, k: (k, i)) if ta else pl.BlockSpec((tm, tk), lambda i, j, k: (i, k))
    if b_groups == 1:
        b_spec = pl.BlockSpec((tn, tk), lambda i, j, k: (j, k)) if tb else pl.BlockSpec((tk, tn), lambda i, j, k: (k, j))
    elif tb:
        kpg = (K // b_groups) // tk
        b_spec = pl.BlockSpec((None, tn, tk), lambda i, j, k: (k // kpg, j, k % kpg))
    else:
        npg = (N // b_groups) // tn
        b_spec = pl.BlockSpec((None, tk, tn), lambda i, j, k: (j // npg, k, j % npg))
    in_specs = [a_spec, b_spec]
    operands = [a, b]
    if has_bias:
        in_specs.append(pl.BlockSpec((1, tn), lambda i, j, k: (0, j)))
        operands.append(bias)
    for e in extra:
        in_specs.append(pl.BlockSpec((tm, tn), lambda i, j, k: (i, j)))
        operands.append(e)
    n_after = 0 if after is None else 1
    if after is not None:
        in_specs.append(pl.BlockSpec(memory_space=pl.ANY))
        operands.append(after)
    if out_groups == 1:
        o_spec = pl.BlockSpec((tm, tn), lambda i, j, k: (i, j))
        o_shape = (M, N)
    else:
        opg = (N // out_groups) // tn
        o_spec = pl.BlockSpec((None, tm, tn), lambda i, j, k: (j // opg, i, j % opg))
        o_shape = (out_groups, M, N // out_groups)
    dims = (((0 if ta else 1,), (1 if tb else 0,)), ((), ()))

    def body(*refs):
        a_ref, b_ref = refs[0], refs[1]
        pos = 2
        bias_ref = None
        if has_bias:
            bias_ref = refs[pos]
            pos += 1
        extra_refs = refs[pos:pos + n_extra]
        pos += n_extra + n_after
        out_refs = refs[pos:pos + n_out]
        pos += n_out
        acc_ref = refs[pos] if nk > 1 else None

        part = lax.dot_general(a_ref[...].astype(BF16), b_ref[...].astype(BF16), dims, preferred_element_type=F32)

        def finish(acc):
            if has_bias:
                acc = acc + bias_ref[...]
            outs = epilogue(acc, *[e[...] for e in extra_refs]) if epilogue is not None else (acc,)
            for o_ref, o in zip(out_refs, outs):
                o_ref[...] = o.astype(o_ref.dtype)

        if nk == 1:
            finish(part)
        else:
            k = pl.program_id(2)

            @pl.when(k == 0)
            def _():
                acc_ref[...] = part

            @pl.when(k > 0)
            def _():
                acc_ref[...] += part

            @pl.when(k == nk - 1)
            def _():
                finish(acc_ref[...])

    outs = pl.pallas_call(
        body,
        name=name,
        grid=(nm, nn, nk),
        in_specs=in_specs,
        out_specs=[o_spec] * n_out,
        out_shape=[jax.ShapeDtypeStruct(o_shape, dt) for dt in out_dtypes],
        scratch_shapes=[pltpu.VMEM((tm, tn), F32)] if nk > 1 else [],
        compiler_params=_params(dimension_semantics=("parallel", "parallel", "arbitrary")),
    )(*operands)
    return outs[0] if n_out == 1 else outs


def _mm_blocks(a, slab, bias, where, flips, prev, name, tm=1024, tn=768):
    m, k = a.shape
    n_blocks, _, nb = slab.shape
    tm, tn = _tile(m, tm), _tile(nb, tn)
    per_block = nb // tn

    def body(where_ref, flips_ref, a_ref, b_ref, bias_ref, *rest):
        o_ref = rest[-1]
        o_ref[...] = _dot_nn(a_ref[...].astype(BF16), b_ref[...]) + bias_ref[...]

    def block(r, wh, fl):
        return jnp.bitwise_xor(wh[0], fl[r])

    in_specs = [pl.BlockSpec((tm, k), lambda i, r, t, wh, fl: (i, 0)),
                pl.BlockSpec((None, k, tn), lambda i, r, t, wh, fl: (block(r, wh, fl), 0, t)),
                pl.BlockSpec((1, tn), lambda i, r, t, wh, fl: (0, block(r, wh, fl) * per_block + t))]
    operands = [a, slab, bias]
    if prev is not None:
        in_specs.append(pl.BlockSpec(memory_space=pl.ANY))
        operands.append(prev)
    return pl.pallas_call(
        body, name=name, out_shape=jax.ShapeDtypeStruct((m, n_blocks * nb), F32),
        grid_spec=pltpu.PrefetchScalarGridSpec(
            num_scalar_prefetch=2, grid=(m // tm, len(flips), per_block), in_specs=in_specs,
            out_specs=pl.BlockSpec((tm, tn), lambda i, r, t, wh, fl: (i, block(r, wh, fl) * per_block + t))),
        input_output_aliases={} if prev is None else {5: 0},
        compiler_params=_params(dimension_semantics=("parallel", "arbitrary", "arbitrary")),
    )(where, jnp.asarray(flips, jnp.int32), *operands)


def _rms_stats(x):
    return lax.rsqrt(jnp.mean(x * x, axis=-1, keepdims=True) + EPS)


def _rms_bwd(x, r, g, dy):
    gy = dy * g
    return r * gy - x * (r * r * r) * jnp.mean(x * gy, axis=-1, keepdims=True)


def _sigmoid(x):
    return 1.0 / (1.0 + jnp.exp(-x))


def _row_call(body, *, name, rows, tr, ins, outs, acc_outs=()):
    n = rows // tr
    in_specs = []
    for arr, kind in ins:
        if kind == "row":
            in_specs.append(pl.BlockSpec((tr, arr.shape[1]), lambda i: (i, 0)))
        else:
            in_specs.append(pl.BlockSpec(arr.shape, lambda i: (0, 0)))
    out_specs = [pl.BlockSpec((tr, w), lambda i: (i, 0)) for w, _ in outs]
    out_shape = [jax.ShapeDtypeStruct((rows, w), dt) for w, dt in outs]
    out_specs += [pl.BlockSpec((1, w), lambda i: (0, 0)) for w in acc_outs]
    out_shape += [jax.ShapeDtypeStruct((1, w), F32) for w in acc_outs]
    return pl.pallas_call(
        body, name=name, grid=(n,), in_specs=in_specs, out_specs=out_specs, out_shape=out_shape,
        compiler_params=_params(dimension_semantics=("arbitrary",)),
    )(*[a for a, _ in ins])


def _accumulate(ref, val):
    @pl.when(pl.program_id(0) == 0)
    def _():
        ref[...] = val

    @pl.when(pl.program_id(0) > 0)
    def _():
        ref[...] += val


def _rms_fwd(x, g, tr=256):
    def body(x_ref, g_ref, h_ref):
        xv = x_ref[...]
        h_ref[...] = (xv * _rms_stats(xv) * g_ref[...]).astype(BF16)

    (h,) = _row_call(body, name="rms_fwd", rows=x.shape[0], tr=tr, ins=[(x, "row"), (g, "vec")], outs=[(x.shape[1], BF16)])
    return h


def _resid_rms(x, y, g2, g3, tr=256):
    def body(x_ref, y_ref, g2_ref, g3_ref, x2_ref, h2_ref):
        yv = y_ref[...]
        x2 = x_ref[...] + yv * _rms_stats(yv) * g2_ref[...]
        x2_ref[...] = x2
        h2_ref[...] = (x2 * _rms_stats(x2) * g3_ref[...]).astype(BF16)

    d = x.shape[1]
    return _row_call(body, name="resid_rms", rows=x.shape[0], tr=tr,
                     ins=[(x, "row"), (y, "row"), (g2, "vec"), (g3, "vec")], outs=[(d, F32), (d, BF16)])


def _final(x2, dn, g4, tgt, tr=256):
    d = x2.shape[1]

    def body(x2_ref, dn_ref, g4_ref, t_ref, dx3_ref, ddn_ref, dg4_ref, loss_ref):
        dn_v = dn_ref[...]
        r = _rms_stats(dn_v)
        g = g4_ref[...]
        e = x2_ref[...] + dn_v * r * g - t_ref[...]
        dx3 = e * (1.0 / d)
        dx3_ref[...] = dx3
        ddn_ref[...] = _rms_bwd(dn_v, r, g, dx3).astype(BF16)
        _accumulate(dg4_ref, jnp.sum(dx3 * dn_v * r, axis=0, keepdims=True))
        part = 0.5 * jnp.sum(jnp.mean(e * e, axis=-1, keepdims=True), axis=0, keepdims=True)
        _accumulate(loss_ref, jnp.broadcast_to(part, loss_ref.shape))

    return _row_call(body, name="final", rows=x2.shape[0], tr=tr,
                     ins=[(x2, "row"), (dn, "row"), (g4, "vec"), (tgt, "row")],
                     outs=[(d, F32), (d, BF16)], acc_outs=(d, 128))


def _rms_bwd2(dx3, dh2, x2, y, g2, g3, tr=256):
    d = x2.shape[1]

    def body(dx3_ref, dh2_ref, x2_ref, y_ref, g2_ref, g3_ref, dx2_ref, dy_ref, dg3_ref, dg2_ref):
        x2v, dh2v, yv = x2_ref[...], dh2_ref[...], y_ref[...]
        r3 = _rms_stats(x2v)
        dx2 = dx3_ref[...] + _rms_bwd(x2v, r3, g3_ref[...], dh2v)
        dx2_ref[...] = dx2
        _accumulate(dg3_ref, jnp.sum(dh2v * x2v * r3, axis=0, keepdims=True))
        r2 = _rms_stats(yv)
        dy_ref[...] = _rms_bwd(yv, r2, g2_ref[...], dx2).astype(BF16)
        _accumulate(dg2_ref, jnp.sum(dx2 * yv * r2, axis=0, keepdims=True))

    return _row_call(body, name="rms_bwd2", rows=x2.shape[0], tr=tr,
                     ins=[(dx3, "row"), (dh2, "row"), (x2, "row"), (y, "row"), (g2, "vec"), (g3, "vec")],
                     outs=[(d, F32), (d, BF16)], acc_outs=(d, d))


def _rms_bwd1(dx2, dh, x, g1, tr=256):
    d = x.shape[1]

    def body(dx2_ref, dh_ref, x_ref, g1_ref, gx_ref, dg1_ref):
        xv, dhv = x_ref[...], dh_ref[...]
        r = _rms_stats(xv)
        gx_ref[...] = dx2_ref[...] + _rms_bwd(xv, r, g1_ref[...], dhv)
        _accumulate(dg1_ref, jnp.sum(dhv * xv * r, axis=0, keepdims=True))

    return _row_call(body, name="rms_bwd1", rows=x.shape[0], tr=tr,
                     ins=[(dx2, "row"), (dh, "row"), (x, "row"), (g1, "vec")], outs=[(d, F32)], acc_outs=(d,))


def _merge_fwd(proj, o_sb, o_cv, off_sb, off_cv, gw, tr=256):
    s, d = o_sb.shape
    nj = d // gw
    b_sb, b_cv = off_sb // gw, off_cv // gw

    def body(gs_ref, gc_ref, osb_ref, ocv_ref, m_ref):
        m_ref[...] = (_sigmoid(gs_ref[...]) * osb_ref[...] + _sigmoid(gc_ref[...]) * ocv_ref[...]).astype(BF16)

    blk = lambda i, j: (i, j)
    return pl.pallas_call(
        body, name="merge_fwd", grid=(s // tr, nj),
        in_specs=[pl.BlockSpec((tr, gw), lambda i, j: (i, b_sb + j)), pl.BlockSpec((tr, gw), lambda i, j: (i, b_cv + j)),
                  pl.BlockSpec((tr, gw), blk), pl.BlockSpec((tr, gw), blk)],
        out_specs=pl.BlockSpec((tr, gw), blk), out_shape=jax.ShapeDtypeStruct((s, d), BF16),
        compiler_params=_params(dimension_semantics=("parallel", "parallel")),
    )(proj, proj, o_sb, o_cv)


def _merge_bwd(proj, o_sb, o_cv, dmerged, off_sb, off_cv, gw, tr=256):
    s, d = o_sb.shape
    nj = d // gw
    ni = s // tr
    b_sb, b_cv = off_sb // gw, off_cv // gw

    def body(gs_ref, gc_ref, osb_ref, ocv_ref, dm_ref, dosb_ref, docv_ref, dgs_ref, dgc_ref, sgs_ref, sgc_ref):
        dm = dm_ref[...]
        s_sb, s_cv = _sigmoid(gs_ref[...]), _sigmoid(gc_ref[...])
        dosb_ref[...] = (dm * s_sb).astype(BF16)
        docv_ref[...] = (dm * s_cv).astype(BF16)
        dgs = dm * osb_ref[...] * s_sb * (1.0 - s_sb)
        dgc = dm * ocv_ref[...] * s_cv * (1.0 - s_cv)
        dgs_ref[...] = dgs.astype(BF16)
        dgc_ref[...] = dgc.astype(BF16)
        i = pl.program_id(1)
        for ref, val in ((sgs_ref, dgs), (sgc_ref, dgc)):
            col = jnp.sum(val, axis=0, keepdims=True)

            @pl.when(i == 0)
            def _():
                ref[...] = col

            @pl.when(i > 0)
            def _():
                ref[...] += col

    blk = lambda j, i: (i, j)
    outs = pl.pallas_call(
        body, name="merge_bwd", grid=(nj, ni),
        in_specs=[pl.BlockSpec((tr, gw), lambda j, i: (i, b_sb + j)), pl.BlockSpec((tr, gw), lambda j, i: (i, b_cv + j)),
                  pl.BlockSpec((tr, gw), blk), pl.BlockSpec((tr, gw), blk), pl.BlockSpec((tr, gw), blk)],
        out_specs=[pl.BlockSpec((tr, gw), blk), pl.BlockSpec((tr, gw), blk),
                   pl.BlockSpec((tr, gw), blk), pl.BlockSpec((tr, gw), blk),
                   pl.BlockSpec((1, gw), lambda j, i: (0, j)), pl.BlockSpec((1, gw), lambda j, i: (0, j))],
        out_shape=[jax.ShapeDtypeStruct((s, d), BF16)] * 4 + [jax.ShapeDtypeStruct((1, d), F32)] * 2,
        compiler_params=_params(dimension_semantics=("parallel", "arbitrary")),
    )(proj, proj, o_sb, o_cv, dmerged)
    return outs


def _split_bf16(v):
    hi = v.astype(BF16)
    lo = (v - hi.astype(F32)).astype(BF16)
    return hi, lo


def _dot_nt(a, b):
    return lax.dot_general(a, b, (((1,), (1,)), ((), ())), preferred_element_type=F32)


def _dot_tn(a, b):
    return lax.dot_general(a, b, (((0,), (0,)), ((), ())), preferred_element_type=F32)


def _dot_nn(a, b):
    return lax.dot_general(a, b, (((1,), (0,)), ((), ())), preferred_element_type=F32)


def _sum_right(v, tri):
    hi, lo = _split_bf16(v)
    return _dot_nn(hi, tri) + _dot_nn(lo, tri)


HEADS_PER_STEP = 2
HEADS_PER_STEP_FWD = 4


LOG2_E = 1.4426950408889634


def _sb_tile(q, kb, scale, mask):
    z = _dot_nt(q, kb) * (scale * LOG2_E)
    log_b = jnp.minimum(z, 0.0) - jnp.log2(1.0 + jnp.exp2(-jnp.abs(z)))
    log_1m = log_b - z
    if mask is not None:
        log_1m = jnp.where(mask, log_1m, 0.0)
    return log_b, log_1m


def _tile_mask(tq):
    row = lax.broadcasted_iota(jnp.int32, (tq, tq), 0)
    col = lax.broadcasted_iota(jnp.int32, (tq, tq), 1)
    return col < row


def _tri(tq, before):
    r = lax.broadcasted_iota(jnp.int32, (tq, tq), 0)
    c = lax.broadcasted_iota(jnp.int32, (tq, tq), 1)
    return jnp.where((r < c) if before else (r > c), 1.0, 0.0).astype(BF16)


def _attn_fwd(proj, after, n_heads, d_sb, tq=256):
    s = proj.shape[0]
    tq = _tile(s, tq)
    hb = d_sb // HEAD_DIM
    scale = 1.0 / math.sqrt(HEAD_DIM)

    per_step = min(HEADS_PER_STEP_FWD, n_heads)
    heads = [pl.ds(n * HEAD_DIM, HEAD_DIM) for n in range(per_step)]
    wide = per_step * HEAD_DIM
    hb //= per_step

    def body(q_ref, k_ref, v_ref, _, o_ref):
        i = pl.program_id(1)
        tri = _tri(tq, False)

        def tile(j, carry, diagonal):
            rows = pl.ds(pl.multiple_of(j * tq, tq), tq)
            mask = _tile_mask(tq) if diagonal else None
            out = []
            for hd, (c_l, acc) in zip(heads, carry):
                log_b, log_1m = _sb_tile(q_ref[:, hd].astype(BF16), k_ref[rows, hd].astype(BF16), scale, mask)
                a = jnp.exp2(log_b + (c_l + _sum_right(log_1m, tri)))
                if diagonal:
                    a = jnp.where(mask, a, 0.0)
                out.append((c_l + jnp.sum(log_1m, axis=1, keepdims=True),
                            acc + _dot_nn(a.astype(BF16), v_ref[rows, hd].astype(BF16))))
            return tuple(out)

        init = tuple((jnp.zeros((tq, 1), F32), jnp.zeros((tq, HEAD_DIM), F32)) for _ in heads)
        carry = lax.fori_loop(0, i, lambda jj, c: tile(i - 1 - jj, c, False), tile(i, init, True))
        for hd, (_, acc) in zip(heads, carry):
            o_ref[:, hd] = acc.astype(BF16)

    return pl.pallas_call(
        body, name="attn_fwd", grid=(n_heads // per_step, s // tq),
        in_specs=[pl.BlockSpec((tq, wide), lambda h, i: (i, h)),
                  pl.BlockSpec((s, wide), lambda h, i: (0, hb + h)),
                  pl.BlockSpec((s, wide), lambda h, i: (0, 2 * hb + h)), pl.BlockSpec(memory_space=pl.ANY)],
        out_specs=pl.BlockSpec((tq, wide), lambda h, i: (i, h)),
        out_shape=jax.ShapeDtypeStruct((s, d_sb), BF16),
        compiler_params=_params(dimension_semantics=("parallel", "arbitrary")),
    )(proj, proj, proj, after)


def _attn_bwd(proj, dattn, after, n_heads, d_sb, tq=256):
    s = proj.shape[0]
    tq = _tile(s, tq)
    nq = s // tq
    hb = d_sb // HEAD_DIM // HEADS_PER_STEP
    scale = 1.0 / math.sqrt(HEAD_DIM)
    heads = [pl.ds(n * HEAD_DIM, HEAD_DIM) for n in range(HEADS_PER_STEP)]
    wide = HEADS_PER_STEP * HEAD_DIM

    def body(q_ref, k_ref, v_ref, do_ref, _, dq_ref, dk_ref, dv_ref, sq_ref, sk_ref, sv_ref, dk_acc, dv_acc, g_st, b_st):
        i = pl.program_id(1)

        @pl.when(i == 0)
        def _():
            dk_acc[...] = jnp.zeros_like(dk_acc)
            dv_acc[...] = jnp.zeros_like(dv_acc)

        tri_after = _tri(tq, False)
        tri_before = _tri(tq, True)

        def newest_first(j, c_ls, diagonal):
            rows = pl.ds(pl.multiple_of(j * tq, tq), tq)
            mask = _tile_mask(tq) if diagonal else None
            out = []
            for n, (hd, c_l) in enumerate(zip(heads, c_ls)):
                do = do_ref[:, hd].astype(BF16)
                log_b, log_1m = _sb_tile(q_ref[:, hd].astype(BF16), k_ref[rows, hd].astype(BF16), scale, mask)
                a = jnp.exp2(log_b + (c_l + _sum_right(log_1m, tri_after)))
                beta = jnp.exp2(log_b)
                if diagonal:
                    a = jnp.where(mask, a, 0.0)
                    beta = jnp.where(mask, beta, 0.0)
                g_st[n, j] = a * _dot_nt(do, v_ref[rows, hd].astype(BF16))
                b_st[n, j] = beta
                dv_acc[rows, hd] += _dot_tn(a.astype(BF16), do)
                out.append(c_l + jnp.sum(log_1m, axis=1, keepdims=True))
            return tuple(out)

        lax.fori_loop(0, i, lambda jj, c: newest_first(i - 1 - jj, c, False),
                      newest_first(i, tuple(jnp.zeros((tq, 1), F32) for _ in heads), True))

        def oldest_first(j, carry):
            rows = pl.ds(pl.multiple_of(j * tq, tq), tq)
            out = []
            for n, (hd, (c_g, dq)) in enumerate(zip(heads, carry)):
                kb = k_ref[rows, hd].astype(BF16)
                g = g_st[n, j]
                beta = b_st[n, j]
                g_before = c_g + _sum_right(g, tri_before)
                dz16 = ((g * (1.0 - beta) - g_before * beta) * scale).astype(BF16)
                dk_acc[rows, hd] += _dot_tn(dz16, q_ref[:, hd].astype(BF16))
                out.append((c_g + jnp.sum(g, axis=1, keepdims=True), dq + _dot_nn(dz16, kb)))
            return tuple(out)

        init = tuple((jnp.zeros((tq, 1), F32), jnp.zeros((tq, HEAD_DIM), F32)) for _ in heads)
        carry = lax.fori_loop(0, i + 1, oldest_first, init)
        dq = jnp.concatenate([c[1] for c in carry], axis=1)
        dq_ref[...] = dq.astype(BF16)
        col = jnp.sum(dq, axis=0, keepdims=True)

        @pl.when(i == 0)
        def _():
            sq_ref[...] = col

        @pl.when(i > 0)
        def _():
            sq_ref[...] += col

        @pl.when(i == nq - 1)
        def _():
            dk = dk_acc[...]
            dv = dv_acc[...]
            dk_ref[...] = dk.astype(BF16)
            dv_ref[...] = dv.astype(BF16)
            sk_ref[...] = jnp.sum(dk, axis=0, keepdims=True)
            sv_ref[...] = jnp.sum(dv, axis=0, keepdims=True)

    qblk = pl.BlockSpec((tq, wide), lambda h, i: (i, h))
    kblk = pl.BlockSpec((s, wide), lambda h, i: (0, h))
    col_sum = pl.BlockSpec((1, wide), lambda h, i: (0, h))
    stash = pltpu.VMEM((HEADS_PER_STEP, nq, tq, tq), F32)
    outs = pl.pallas_call(
        body, name="attn_bwd", grid=(n_heads // HEADS_PER_STEP, nq),
        in_specs=[qblk,
                  pl.BlockSpec((s, wide), lambda h, i: (0, hb + h)),
                  pl.BlockSpec((s, wide), lambda h, i: (0, 2 * hb + h)),
                  qblk, pl.BlockSpec(memory_space=pl.ANY)],
        out_specs=[qblk, kblk, kblk, col_sum, col_sum, col_sum],
        out_shape=[jax.ShapeDtypeStruct((s, d_sb), BF16)] * 3 + [jax.ShapeDtypeStruct((1, d_sb), F32)] * 3,
        scratch_shapes=[pltpu.VMEM((s, wide), F32), pltpu.VMEM((s, wide), F32), stash, stash],
        compiler_params=_params(dimension_semantics=("parallel", "arbitrary")),
    )(proj, proj, proj, dattn, after)
    return outs


LANES = 128


def _glu(proj, off_a, dc, tr=256):
    s = proj.shape[0]
    ba = off_a // dc

    def body(a_ref, b_ref, u_ref):
        u_ref[...] = a_ref[...] * _sigmoid(b_ref[...])

    return pl.pallas_call(
        body, name="glu", grid=(s // tr,),
        in_specs=[pl.BlockSpec((tr, dc), lambda i: (i, ba)), pl.BlockSpec((tr, dc), lambda i: (i, ba + 1))],
        out_specs=pl.BlockSpec((tr, dc), lambda i: (i, 0)), out_shape=jax.ShapeDtypeStruct((s, dc), F32),
        compiler_params=_params(dimension_semantics=("parallel",)),
    )(proj, proj)


SUBLANES = 8


def _shifted_rows(src, cols, shifted, tt):
    span = tt + CONV_HALO - SUBLANES
    for b in range(1, SUBLANES):
        shifted[b, pl.ds(0, span), :] = src[pl.ds(b, span), cols]

    def rows(o):
        a, b = divmod(o, SUBLANES)
        return src[pl.ds(SUBLANES * a, tt), cols] if b == 0 else shifted[b, pl.ds(SUBLANES * a, tt), :]

    return rows


def _conv_fwd(u0, taps, b_dw, g_ln, b_ln, tt=256):
    s, dc = u0.shape
    tt = _tile(s, tt)
    hpb = tt // CONV_HALO

    def body(cur_ref, halo_ref, w_ref, bdw_ref, g_ref, b_ref, u1_ref, u3_ref, xs, xs_shifted):
        i = pl.program_id(0)
        xs[pl.ds(0, CONV_HALO), :] = jnp.where(i > 0, halo_ref[...], 0.0)
        xs[pl.ds(CONV_HALO, tt), :] = cur_ref[...]
        for c0 in range(0, dc, LANES):
            cols = pl.ds(c0, LANES)
            x_rows = _shifted_rows(xs, cols, xs_shifted, tt)
            acc = jnp.broadcast_to(bdw_ref[:, cols], (tt, LANES))
            for k in range(CONV_WIDTH):
                acc = acc + w_ref[pl.ds(k, 1), cols] * x_rows(CONV_HALO - (CONV_WIDTH - 1) + k)
            u1_ref[:, cols] = acc
        u1 = u1_ref[...]
        mu = jnp.mean(u1, axis=-1, keepdims=True)
        xc = u1 - mu
        rstd = lax.rsqrt(jnp.mean(xc * xc, axis=-1, keepdims=True) + EPS)
        u2 = xc * rstd * g_ref[...] + b_ref[...]
        u3_ref[...] = (u2 * _sigmoid(u2)).astype(BF16)

    vec = pl.BlockSpec((1, dc), lambda i: (0, 0))
    return pl.pallas_call(
        body, name="conv_fwd", grid=(s // tt,),
        in_specs=[pl.BlockSpec((tt, dc), lambda i: (i, 0)),
                  pl.BlockSpec((CONV_HALO, dc), lambda i: (jnp.maximum(i * hpb - 1, 0), 0)),
                  pl.BlockSpec((CONV_HALO, dc), lambda i: (0, 0)), vec, vec, vec],
        out_specs=[pl.BlockSpec((tt, dc), lambda i: (i, 0)), pl.BlockSpec((tt, dc), lambda i: (i, 0))],
        out_shape=[jax.ShapeDtypeStruct((s, dc), F32), jax.ShapeDtypeStruct((s, dc), BF16)],
        scratch_shapes=[pltpu.VMEM((tt + CONV_HALO, dc), F32), pltpu.VMEM((SUBLANES, tt + CONV_HALO, LANES), F32)],
        compiler_params=_params(dimension_semantics=("arbitrary",)),
    )(u0, u0, taps, b_dw, g_ln, b_ln)


def _conv_bwd_norm(u1, du3, g_ln, b_ln, tr=256):
    dc = u1.shape[1]

    def body(u1_ref, du3_ref, g_ref, b_ref, du1_ref, dg_ref, db_ref, dbdw_ref):
        u1v = u1_ref[...]
        g = g_ref[...]
        mu = jnp.mean(u1v, axis=-1, keepdims=True)
        xc = u1v - mu
        rstd = lax.rsqrt(jnp.mean(xc * xc, axis=-1, keepdims=True) + EPS)
        xhat = xc * rstd
        u2 = xhat * g + b_ref[...]
        sg = _sigmoid(u2)
        du2 = du3_ref[...] * (sg * (1.0 + u2 * (1.0 - sg)))
        _accumulate(dg_ref, jnp.sum(du2 * xhat, axis=0, keepdims=True))
        _accumulate(db_ref, jnp.sum(du2, axis=0, keepdims=True))
        gy = du2 * g
        du1 = rstd * (gy - jnp.mean(gy, axis=-1, keepdims=True) - xhat * jnp.mean(gy * xhat, axis=-1, keepdims=True))
        du1_ref[...] = du1
        _accumulate(dbdw_ref, jnp.sum(du1, axis=0, keepdims=True))

    return _row_call(body, name="conv_bwd_norm", rows=u1.shape[0], tr=tr,
                     ins=[(u1, "row"), (du3, "row"), (g_ln, "vec"), (b_ln, "vec")],
                     outs=[(dc, F32)], acc_outs=(dc, dc, dc))


def _conv_bwd_taps(du1, u0, proj, taps, off_a, tt=256):
    s, dc = u0.shape
    tt = _tile(s, tt)
    n = s // tt
    hpb = tt // CONV_HALO
    ba = off_a // dc
    lead = CONV_HALO - (CONV_WIDTH - 1)

    def body(d_ref, dnext_ref, cur_ref, halo_ref, a_ref, b_ref, w_ref, da_ref, db_ref, sa_ref, sb_ref, dw_ref,
             xs, ds, du0, xs_shifted, ds_shifted):
        i = pl.program_id(0)
        xs[pl.ds(0, CONV_HALO), :] = jnp.where(i > 0, halo_ref[...], 0.0)
        xs[pl.ds(CONV_HALO, tt), :] = cur_ref[...]
        ds[pl.ds(0, tt), :] = d_ref[...]
        ds[pl.ds(tt, CONV_HALO), :] = jnp.where(i < n - 1, dnext_ref[...], 0.0)

        @pl.when(i == 0)
        def _():
            dw_ref[...] = jnp.zeros_like(dw_ref)

        for c0 in range(0, dc, LANES):
            cols = pl.ds(c0, LANES)
            x_rows = _shifted_rows(xs, cols, xs_shifted, tt)
            d_rows = _shifted_rows(ds, cols, ds_shifted, tt)
            d_cur = d_rows(0)
            acc = jnp.zeros((tt, LANES), F32)
            for k in range(CONV_WIDTH):
                acc = acc + w_ref[pl.ds(k, 1), cols] * d_rows(CONV_WIDTH - 1 - k)
                dw_ref[pl.ds(k, 1), cols] += jnp.sum(d_cur * x_rows(lead + k), axis=0, keepdims=True)
            du0[:, cols] = acc
        d0 = du0[...]
        sg = _sigmoid(b_ref[...])
        da = d0 * sg
        db = d0 * a_ref[...] * sg * (1.0 - sg)
        da_ref[...] = da.astype(BF16)
        db_ref[...] = db.astype(BF16)
        _accumulate(sa_ref, jnp.sum(da, axis=0, keepdims=True))
        _accumulate(sb_ref, jnp.sum(db, axis=0, keepdims=True))

    row = pl.BlockSpec((tt, dc), lambda i: (i, 0))
    vec = pl.BlockSpec((1, dc), lambda i: (0, 0))
    taps_spec = pl.BlockSpec((CONV_HALO, dc), lambda i: (0, 0))
    return pl.pallas_call(
        body, name="conv_bwd_taps", grid=(n,),
        in_specs=[row, pl.BlockSpec((CONV_HALO, dc), lambda i: (jnp.minimum((i + 1) * hpb, n * hpb - 1), 0)),
                  row, pl.BlockSpec((CONV_HALO, dc), lambda i: (jnp.maximum(i * hpb - 1, 0), 0)),
                  pl.BlockSpec((tt, dc), lambda i: (i, ba)), pl.BlockSpec((tt, dc), lambda i: (i, ba + 1)), taps_spec],
        out_specs=[row, row, vec, vec, taps_spec],
        out_shape=[jax.ShapeDtypeStruct((s, dc), BF16), jax.ShapeDtypeStruct((s, dc), BF16),
                   jax.ShapeDtypeStruct((1, dc), F32), jax.ShapeDtypeStruct((1, dc), F32),
                   jax.ShapeDtypeStruct((CONV_HALO, dc), F32)],
        scratch_shapes=[pltpu.VMEM((tt + CONV_HALO, dc), F32), pltpu.VMEM((tt + CONV_HALO, dc), F32),
                        pltpu.VMEM((tt, dc), F32),
                        pltpu.VMEM((SUBLANES, tt + CONV_HALO, LANES), F32), pltpu.VMEM((SUBLANES, tt + CONV_HALO, LANES), F32)],
        compiler_params=_params(dimension_semantics=("arbitrary",)),
    )(du1, du1, u0, u0, proj, proj, taps)


def _elementwise(body, *, name, ins, out_dtypes, after, tr=128):
    rows, cols = ins[0].shape
    tr = _tile(rows, tr, 8)
    spec = pl.BlockSpec((tr, cols), lambda i: (i, 0))
    n_in = len(ins)

    def with_after(*refs):
        body(*refs[:n_in], *refs[n_in + 1:])

    return pl.pallas_call(
        with_after, name=name, grid=(rows // tr,), in_specs=[spec] * n_in + [pl.BlockSpec(memory_space=pl.ANY)],
        out_specs=[spec] * len(out_dtypes),
        out_shape=[jax.ShapeDtypeStruct((rows, cols), dt) for dt in out_dtypes],
        compiler_params=_params(dimension_semantics=("parallel",)),
    )(*ins, after)


def _prefetched(body, *, name, where, grid, in_specs, out_specs, out_shape, operands):
    return pl.pallas_call(
        body, name=name, out_shape=out_shape,
        grid_spec=pltpu.PrefetchScalarGridSpec(num_scalar_prefetch=1, grid=grid, in_specs=in_specs, out_specs=out_specs),
        compiler_params=_params(dimension_semantics=("parallel",) * len(grid)),
    )(where, *operands)


def _cast_into_block(w, where, after, name, tr=256):
    rows, cols = w.shape
    tr = _tile(rows, tr, 16)

    def body(where_ref, w_ref, _, o_ref):
        o_ref[...] = w_ref[...].astype(BF16)

    return _prefetched(
        body, name=name, where=where, grid=(rows // tr,),
        in_specs=[pl.BlockSpec((tr, cols), lambda i, wh: (i, 0)), pl.BlockSpec(memory_space=pl.ANY)],
        out_specs=pl.BlockSpec((None, tr, cols), lambda i, wh: (wh[0], i, 0)),
        out_shape=jax.ShapeDtypeStruct((N_CHIPS, rows, cols), BF16), operands=[w, after])


def _add_pair(grad, theirs, where, name, tr=512):
    nb, half, cols = theirs.shape
    tr = _tile(half, tr, 16)
    nh = half // tr

    def body(where_ref, a_ref, b_ref, o_ref):
        o_ref[...] = a_ref[...] + b_ref[...]

    blk = pl.BlockSpec((None, tr, cols), lambda b, i, wh: (b, i, 0))
    return _prefetched(
        body, name=name, where=where, grid=(nb, nh),
        in_specs=[pl.BlockSpec((None, tr, cols), lambda b, i, wh: (b, wh[1] * nh + i, 0)), blk],
        out_specs=blk, out_shape=jax.ShapeDtypeStruct(theirs.shape, BF16), operands=[grad, theirs])


def _sum_slots(part, slots, where, name, tr=512):
    _, half, cols = part.shape
    tr = _tile(half, tr, 16)
    nh = half // tr

    def body(where_ref, p_ref, s_ref, o_ref):
        o_ref[...] = ((p_ref[...].astype(F32) + s_ref[0].astype(F32)) + s_ref[1].astype(F32)) + s_ref[2].astype(F32)

    return _prefetched(
        body, name=name, where=where, grid=(nh,),
        in_specs=[pl.BlockSpec((None, tr, cols), lambda i, wh: (wh[0], i, 0)),
                  pl.BlockSpec((3, tr, cols), lambda i, wh: (0, i, 0))],
        out_specs=pl.BlockSpec((tr, cols), lambda i, wh: (wh[1] * nh + i, 0)),
        out_shape=jax.ShapeDtypeStruct((2 * half, cols), F32), operands=[part, slots])


def _adamw(w, g, m, v, after, name):
    bc1 = 1.0 - ADAM_B1 ** ADAM_STEP
    bc2 = 1.0 - ADAM_B2 ** ADAM_STEP

    def body(w_ref, g_ref, m_ref, v_ref, go_ref, d_ref, mo_ref, vo_ref):
        g = g_ref[...]
        m_new = ADAM_B1 * m_ref[...] + (1.0 - ADAM_B1) * g
        v_new = ADAM_B2 * v_ref[...] + (1.0 - ADAM_B2) * (g * g)
        m_hat = m_new / bc1
        v_hat = v_new / bc2
        go_ref[...] = g
        d_ref[...] = -ADAM_LR * (m_hat / (jnp.sqrt(v_hat) + ADAM_EPS) + ADAM_WD * w_ref[...])
        mo_ref[...] = m_new
        vo_ref[...] = v_new

    return _elementwise(body, name=name, ins=[w, g, m, v], out_dtypes=[F32] * 4, after=after, tr=128)


ANY = pl.BlockSpec(memory_space=pl.ANY)


def _mesh_pos():
    return lax.axis_index("x"), lax.axis_index("y"), lax.axis_index("c")


def _other_chips(x, y):
    return [(1 - x, y), (x, 1 - y), (1 - x, 1 - y)]


def _remote(src, dst, send_sems, recv_sems, idx, to):
    return pltpu.make_async_remote_copy(src_ref=src, dst_ref=dst, send_sem=send_sems.at[idx], recv_sem=recv_sems.at[idx],
                                        device_id=to, device_id_type=MESH)


HBM = pl.BlockSpec(memory_space=pltpu.HBM)
SEM = pl.BlockSpec(memory_space=pltpu.SEMAPHORE)
EFFECT = pltpu.SideEffectType.DATAFLOW_SIDE_EFFECTING


def _in_hbm(arrays):
    return [pltpu.with_memory_space_constraint(a, pltpu.HBM) for a in arrays]


def _ici_copies(slab, send_sems, recv_sems, x, y, c):
    half = slab.shape[1] // 2
    rows = pl.ds(c * half, half)
    mine = slab.at[2 * x + y, rows]
    out = []
    for j, (cx, cy) in enumerate(_other_chips(x, y)):
        got = slab.at[2 * cx + cy, rows]
        out.append((_remote(mine, mine, send_sems, recv_sems, j, (cx, cy, c)),
                    _remote(got, got, send_sems, recv_sems, j, (cx, cy, c))))
    return out


def _gather_start(slabs, after, name):
    n = len(slabs)

    def body(*refs):
        ins = refs[:n]
        send_sems, recv_sems, token = refs[2 * n + 1:3 * n + 1], refs[3 * n + 1:4 * n + 1], refs[4 * n + 1]
        x, y, c = _mesh_pos()
        for w in range(n):
            for send, _ in _ici_copies(ins[w], send_sems[w], recv_sems[w], x, y, c):
                send.start()
        token[...] = jnp.zeros_like(token)

    outs = pl.pallas_call(
        body, name=name, in_specs=[HBM] * n + [ANY],
        out_specs=[HBM] * n + [SEM] * (2 * n) + [pl.BlockSpec(memory_space=pltpu.VMEM)],
        out_shape=[pltpu.HBM(s.shape, s.dtype) for s in slabs] + [pltpu.SemaphoreType.DMA((3,))] * (2 * n)
        + [jax.ShapeDtypeStruct((8, 128), F32)],
        input_output_aliases={w: w for w in range(n)},
        compiler_params=_params(has_side_effects=EFFECT),
    )(*_in_hbm(slabs), after)
    return (outs[:n], outs[n:2 * n], outs[2 * n:3 * n]), outs[3 * n]


ALL_CHIPS = (0, 1, 2)


def _gather_wait(slabs, send_sems, recv_sems, after, name, which=ALL_CHIPS):
    n = len(slabs)

    def body(*refs):
        ins = refs[:n]
        sends, recvs = refs[n:2 * n], refs[2 * n:3 * n]
        x, y, c = _mesh_pos()
        for w in range(n):
            copies = _ici_copies(ins[w], sends[w], recvs[w], x, y, c)
            for j in which:
                copies[j][0].wait_send()
                copies[j][1].wait_recv()

    return pl.pallas_call(
        body, name=name, in_specs=[HBM] * n + [SEM] * (2 * n) + [ANY], out_specs=[HBM] * n,
        out_shape=[pltpu.HBM(s.shape, s.dtype) for s in slabs],
        input_output_aliases={w: w for w in range(n)},
        compiler_params=_params(has_side_effects=EFFECT),
    )(*slabs, *send_sems, *recv_sems, after)


def _phase_start(arrays, copies, n_copies, after, name):
    n = len(arrays)

    def body(*refs):
        send_sems, recv_sems, token = refs[2 * n + 1], refs[2 * n + 2], refs[2 * n + 3]
        for send, _ in copies(refs[:n], send_sems, recv_sems, *_mesh_pos()):
            send.start()
        token[...] = jnp.zeros_like(token)

    outs = pl.pallas_call(
        body, name=name, in_specs=[HBM] * n + [ANY],
        out_specs=[HBM] * n + [SEM, SEM, pl.BlockSpec(memory_space=pltpu.VMEM)],
        out_shape=[pltpu.HBM(a.shape, a.dtype) for a in arrays]
        + [pltpu.SemaphoreType.DMA((n_copies,))] * 2 + [jax.ShapeDtypeStruct((8, 128), F32)],
        input_output_aliases={k: k for k in range(n)},
        compiler_params=_params(has_side_effects=EFFECT),
    )(*_in_hbm(arrays), after)
    return outs[:n], outs[n], outs[n + 1], outs[n + 2]


def _phase_wait(arrays, send_sems, recv_sems, copies, after, name):
    n = len(arrays)

    def body(*refs):
        for send, recv in copies(refs[:n], refs[n], refs[n + 1], *_mesh_pos()):
            send.wait_send()
            recv.wait_recv()

    return pl.pallas_call(
        body, name=name, in_specs=[HBM] * n + [SEM, SEM, ANY], out_specs=[HBM] * n,
        out_shape=[pltpu.HBM(a.shape, a.dtype) for a in arrays],
        input_output_aliases={k: k for k in range(n)},
        compiler_params=_params(has_side_effects=EFFECT),
    )(*arrays, send_sems, recv_sems, after)


def _forward_copies(which):
    def copies(slabs, send_sems, recv_sems, x, y, c):
        sibling = (x, y, 1 - c)
        chips = _other_chips(x, y)
        out = []
        for w, slab in enumerate(slabs):
            half = slab.shape[1] // 2
            for j in which:
                cx, cy = chips[j]
                mine = slab.at[2 * cx + cy, pl.ds(c * half, half)]
                theirs = slab.at[2 * cx + cy, pl.ds((1 - c) * half, half)]
                out.append((_remote(mine, mine, send_sems, recv_sems, 3 * w + j, sibling),
                            _remote(theirs, theirs, send_sems, recv_sems, 3 * w + j, sibling)))
        return out

    return copies


def _share_copies(halves, send_sems, recv_sems, x, y, c):
    sibling = (x, y, 1 - c)
    out = []
    for w, full in enumerate(halves):
        h = full.shape[0] // 2
        mine, theirs = full.at[pl.ds(c * h, h)], full.at[pl.ds((1 - c) * h, h)]
        out.append((_remote(mine, mine, send_sems, recv_sems, w, sibling),
                    _remote(theirs, theirs, send_sems, recv_sems, w, sibling)))
    return out


def _exchange_copies(arrays, send_sems, recv_sems, x, y, c):
    n = len(arrays) // 2
    sibling = (x, y, 1 - c)
    out = []
    for w in range(n):
        grad, land = arrays[w], arrays[n + w]
        half = grad.shape[1] // 2
        out.append((_remote(grad.at[:, pl.ds((1 - c) * half, half), :], land, send_sems, recv_sems, w, sibling),
                    _remote(land, land, send_sems, recv_sems, w, sibling)))
    return out


def _gather_taps(taps):
    def body(in_ref, out_ref, send_sems, recv_sems, local_sem):
        x, y, c = _mesh_pos()
        me = 2 * x + y
        chips = _other_chips(x, y)
        lc = pltpu.make_async_copy(in_ref, out_ref.at[me], local_sem)
        lc.start()
        sends = [_remote(in_ref, out_ref.at[me], send_sems, recv_sems, j, (cx, cy, c)) for j, (cx, cy) in enumerate(chips)]
        for cp in sends:
            cp.start()
        for j, (cx, cy) in enumerate(chips):
            got = out_ref.at[2 * cx + cy]
            _remote(got, got, send_sems, recv_sems, j, (cx, cy, c)).wait_recv()
        for cp in sends:
            cp.wait_send()
        lc.wait()

    vm = pl.BlockSpec(memory_space=pltpu.VMEM)
    return pl.pallas_call(
        body, name="gather_taps", in_specs=[vm], out_specs=vm,
        out_shape=jax.ShapeDtypeStruct((N_CHIPS,) + taps.shape, taps.dtype),
        scratch_shapes=[pltpu.SemaphoreType.DMA((3,)), pltpu.SemaphoreType.DMA((3,)), pltpu.SemaphoreType.DMA],
        compiler_params=_params(has_side_effects=True),
    )(taps)


def _scatter_copies(part, slots, send_sems, recv_sems, x, y, c):
    out = []
    for j, (cx, cy) in enumerate(_other_chips(x, y)):
        out.append((_remote(part.at[2 * cx + cy], slots.at[j], send_sems, recv_sems, j, (cx, cy, c)),
                    _remote(slots.at[j], slots.at[j], send_sems, recv_sems, j, (cx, cy, c))))
    return out


def _scatter_start(parts, after, name):
    n = len(parts)
    slots = [lax.empty((3,) + p.shape[1:], p.dtype) for p in parts]

    def body(*refs):
        ins, lands = refs[:n], refs[n:2 * n]
        send_sems, recv_sems, token = refs[4 * n + 1:5 * n + 1], refs[5 * n + 1:6 * n + 1], refs[6 * n + 1]
        x, y, c = _mesh_pos()
        for w in range(n):
            for send, _ in _scatter_copies(ins[w], lands[w], send_sems[w], recv_sems[w], x, y, c):
                send.start()
        token[...] = jnp.zeros_like(token)

    outs = pl.pallas_call(
        body, name=name, in_specs=[HBM] * (2 * n) + [ANY],
        out_specs=[HBM] * (2 * n) + [SEM] * (2 * n) + [pl.BlockSpec(memory_space=pltpu.VMEM)],
        out_shape=[pltpu.HBM(a.shape, a.dtype) for a in list(parts) + slots] + [pltpu.SemaphoreType.DMA((3,))] * (2 * n)
        + [jax.ShapeDtypeStruct((8, 128), F32)],
        input_output_aliases={k: k for k in range(2 * n)},
        compiler_params=_params(has_side_effects=EFFECT),
    )(*_in_hbm(list(parts) + slots), after)
    return (outs[:n], outs[n:2 * n], outs[2 * n:3 * n], outs[3 * n:4 * n]), outs[4 * n]


def _scatter_wait(parts, slots, send_sems, recv_sems, after, name):
    n = len(parts)

    def body(*refs):
        ins, lands = refs[:n], refs[n:2 * n]
        sends, recvs = refs[2 * n:3 * n], refs[3 * n:4 * n]
        x, y, c = _mesh_pos()
        for w in range(n):
            for send, recv in _scatter_copies(ins[w], lands[w], sends[w], recvs[w], x, y, c):
                send.wait_send()
                recv.wait_recv()

    outs = pl.pallas_call(
        body, name=name, in_specs=[HBM] * (2 * n) + [SEM] * (2 * n) + [ANY], out_specs=[HBM] * (2 * n),
        out_shape=[pltpu.HBM(a.shape, a.dtype) for a in list(parts) + list(slots)],
        input_output_aliases={k: k for k in range(2 * n)},
        compiler_params=_params(has_side_effects=EFFECT),
    )(*parts, *slots, *send_sems, *recv_sems, after)
    return outs[:n], outs[n:]


def _all_reduce_small(buf, after, name):
    rows, width = buf.shape

    def body(in_ref, _, out_ref, gathered, send_sems, recv_sems):
        x, y, c = _mesh_pos()
        me = 4 * x + 2 * y + c
        gathered[me] = in_ref[...]
        flips = [(dx, dy, dz) for dx in (0, 1) for dy in (0, 1) for dz in (0, 1)][1:]
        peers = [((1 - x) if dx else x, (1 - y) if dy else y, (1 - c) if dz else c) for dx, dy, dz in flips]
        sends = [_remote(in_ref, gathered.at[me], send_sems, recv_sems, k, peer) for k, peer in enumerate(peers)]
        for cp in sends:
            cp.start()
        for k, (px, py, pc) in enumerate(peers):
            got = gathered.at[4 * px + 2 * py + pc]
            _remote(got, got, send_sems, recv_sems, k, (px, py, pc)).wait_recv()
        for cp in sends:
            cp.wait_send()
        total = gathered[0]
        for d in range(1, 8):
            total = total + gathered[d]
        out_ref[...] = total

    vm = pl.BlockSpec(memory_space=pltpu.VMEM)
    return pl.pallas_call(
        body, name=name, in_specs=[vm, ANY], out_specs=vm, out_shape=jax.ShapeDtypeStruct(buf.shape, F32),
        scratch_shapes=[pltpu.VMEM((8, rows, width), F32), pltpu.SemaphoreType.DMA((7,)), pltpu.SemaphoreType.DMA((7,))],
        compiler_params=_params(has_side_effects=True),
    )(buf, after)


PACK_W = 2048
PACK_ROWS = 8


def _pack(vectors):
    flat = jnp.concatenate([v.reshape(-1) for v in vectors])
    unit = PACK_W * PACK_ROWS
    total = -(-flat.shape[0] // unit) * unit
    return jnp.pad(flat, (0, total - flat.shape[0])).reshape(total // PACK_W, PACK_W)


def _unpack(buf, shapes):
    flat = buf.reshape(-1)
    out, pos = [], 0
    for shp in shapes:
        size = math.prod(shp)
        out.append(flat[pos:pos + size].reshape(shp))
        pos += size
    return out


def _pad_rows(a, rows):
    return jnp.pad(a, ((0, rows - a.shape[0]), (0, 0)))


def kernel(x, g_pre_mix, w_in, b_in, w_dw, b_dw, g_conv_ln, b_conv_ln, w_sb_out, w_conv_out, w_o, g_post_mix, g_pre_mlp, w_up, w_down, g_post_mlp, loss_target, m_g_pre_mix, m_w_in, m_b_in, m_w_dw, m_b_dw, m_g_conv_ln, m_b_conv_ln, m_w_sb_out, m_w_conv_out, m_w_o, m_g_post_mix, m_g_pre_mlp, m_w_up, m_w_down, m_g_post_mlp, v_g_pre_mix, v_w_in, v_b_in, v_w_dw, v_b_dw, v_g_conv_ln, v_b_conv_ln, v_w_sb_out, v_w_conv_out, v_w_o, v_g_post_mix, v_g_pre_mlp, v_w_up, v_w_down, v_g_post_mlp):
    xs = x[0]
    tgt = loss_target[0]
    s, d = xs.shape
    d_sb = w_sb_out.shape[1]
    dc = w_conv_out.shape[1]
    d_in = w_in.shape[2] * N_CHIPS
    n_heads = d_sb // HEAD_DIM
    off_a = 3 * d_sb
    off_sb = off_a + 2 * dc
    off_cv = off_sb + d
    gw = dc
    assert off_a % dc == 0 and off_sb % gw == 0 and d % gw == 0 and d_in == off_cv + d
    chip = 2 * lax.axis_index("x") + lax.axis_index("y")

    big = dict(w_in=w_in[0], w_sb_out=w_sb_out[0], w_conv_out=w_conv_out[0], w_o=w_o[0], w_up=w_up[0], w_down=w_down[0])
    names = list(big)
    col_sharded = {"w_in", "w_sb_out", "w_conv_out", "w_up"}
    where = jnp.stack([chip, lax.axis_index("c")]).astype(jnp.int32)
    cs = w_dw.shape[2]
    taps4 = _gather_taps(_pad_rows(w_dw[0], CONV_HALO))
    first, token = _gather_start([_cast_into_block(big["w_in"], where, taps4, "cast_w_in")], taps4, "gather_start_in")
    rest, rest_token = _gather_start([_cast_into_block(big[n], where, token, "cast_" + n) for n in names[1:]], token,
                            "gather_start_rest")
    in_flight = dict(zip(names, zip(*[a + b for a, b in zip(first, rest)])))
    gathered = {}

    def land(group, after, tag):
        thru, sends, recvs = zip(*[in_flight[n] for n in group])
        landed = _gather_wait(thru, sends, recvs, after, "gather_wait_" + tag)
        return _phase_start(landed, _forward_copies(ALL_CHIPS), 3 * len(group), where, "forward_start_" + tag)

    def ready(group, started, after, tag):
        slabs, sends, recvs, _ = started
        whole = _phase_wait(slabs, sends, recvs, _forward_copies(ALL_CHIPS), after, "forward_wait_" + tag)
        for n, g4 in zip(group, whole):
            gathered[n] = g4 if n in col_sharded else g4.reshape(g4.shape[0] * g4.shape[1], g4.shape[2])

    taps = jnp.transpose(taps4, (1, 0, 2)).reshape(CONV_HALO, N_CHIPS * cs)
    group_mix = ["w_sb_out", "w_conv_out", "w_o"]

    h = _rms_fwd(xs, g_pre_mix)
    near, far = (0, 1), (2,)
    slab, sends, recvs = [[v] for v in in_flight["w_in"]]
    slab = _gather_wait(slab, sends, recvs, rest_token, "gather_wait_in_near", near)
    moving = _phase_start(slab, _forward_copies(near), 3, where, "forward_start_in_near")
    slab = _phase_wait(*moving[:3], _forward_copies(near), h, "forward_wait_in_near")
    proj = _mm_blocks(h, slab[0], b_in, where, [0, 2, 1], None, "mm_proj_near")
    slab = _gather_wait(slab, sends, recvs, proj, "gather_wait_in_far", far)
    moving = _phase_start(slab, _forward_copies(far), 3, where, "forward_start_in_far")
    slab = _phase_wait(*moving[:3], _forward_copies(far), moving[3], "forward_wait_in_far")
    gathered["w_in"] = slab[0]
    proj = _mm_blocks(h, slab[0], b_in, where, [3], proj, "mm_proj_far")
    mix_fwd = land(group_mix, proj, "mix")
    attn = _attn_fwd(proj, mix_fwd[3], n_heads, d_sb)
    u0 = _glu(proj, off_a, dc)
    u1, u3 = _conv_fwd(u0, taps, b_dw, g_conv_ln, b_conv_ln)
    ready(group_mix, mix_fwd, u3, "mix")
    o_sb = _mm(attn, gathered["w_sb_out"], name="mm_o_sb", b_groups=N_CHIPS)
    up_fwd = land(["w_up"], o_sb, "up")
    o_cv = _mm(u3, gathered["w_conv_out"], name="mm_o_cv", b_groups=N_CHIPS, after=up_fwd[3])
    merged = _merge_fwd(proj, o_sb, o_cv, off_sb, off_cv, gw)
    y = _mm(merged, gathered["w_o"], name="mm_y")
    x2, h2 = _resid_rms(xs, y, g_post_mix, g_pre_mlp)
    ready(["w_up"], up_fwd, h2, "up")
    down_fwd = land(["w_down"], h2, "down")
    up, f = _mm(h2, gathered["w_up"], name="mm_up", b_groups=N_CHIPS, out_dtypes=(F32, BF16), after=down_fwd[3],
                epilogue=lambda acc: (acc, jnp.square(jnp.maximum(acc, 0.0))))
    ready(["w_down"], down_fwd, f, "down")
    dn = _mm(f, gathered["w_down"], name="mm_down")
    dx3, d_dn, dg_post_mlp, loss_part = _final(x2, dn, g_post_mlp, tgt)

    moments = dict(w_in=(m_w_in, v_w_in), w_sb_out=(m_w_sb_out, v_w_sb_out), w_conv_out=(m_w_conv_out, v_w_conv_out),
                   w_o=(m_w_o, v_w_o), w_up=(m_w_up, v_w_up), w_down=(m_w_down, v_w_down))
    out_g, out_d, out_m, out_v = {}, {}, {}, {}
    grads = {}

    def blocks_of(group):
        out = []
        for n in group:
            g = grads[n]
            out.append(g if g.ndim == 3 else g.reshape(N_CHIPS, g.shape[0] // N_CHIPS, g.shape[1]))
        return out

    def exchange_begin(group, tag):
        blocks = blocks_of(group)
        lands = [lax.empty((b.shape[0], b.shape[1] // 2, b.shape[2]), b.dtype) for b in blocks]
        return _phase_start(blocks + lands, _exchange_copies, len(group), where, "exchange_start_" + tag)

    def scatter_begin(group, tag, exchange, after):
        arrays, sends, recvs, _ = exchange
        arrays = _phase_wait(arrays, sends, recvs, _exchange_copies, after, "exchange_wait_" + tag)
        n = len(group)
        pair_sums = [_add_pair(g, t, where, "pair_sum_" + m) for m, g, t in zip(group, arrays[:n], arrays[n:])]
        return _scatter_start(pair_sums, where, "scatter_start_" + tag)

    def reduce_sum(group, tag, started, after):
        parts, slots = _scatter_wait(*started, after, "scatter_wait_" + tag)
        halves = [_sum_slots(p, t, where, "chip_sum_" + n) for n, p, t in zip(group, parts, slots)]
        return _phase_start(halves, _share_copies, len(group), where, "share_start_" + tag)

    def reduce_update(group, tag, shared, after):
        halves, sends, recvs, _ = shared
        for n, red in zip(group, _phase_wait(halves, sends, recvs, _share_copies, after, "share_wait_" + tag)):
            mm_, vv_ = moments[n]
            res = _adamw(big[n], red, mm_[0], vv_[0], after, "adamw_" + n)
            out_g[n], out_d[n], out_m[n], out_v[n] = [r[None] for r in res]
            after = res[1]
        return after

    group_mlp, group_mix, group_in = ["w_down", "w_up"], ["w_o", "w_sb_out", "w_conv_out"], ["w_in"]
    grads["w_down"] = _mm(f, d_dn, name="mm_dw_down", ta=True, out_dtypes=(BF16,))
    dup = _mm(d_dn, gathered["w_down"], name="mm_df", tb=True, extra=[up], out_dtypes=(BF16,),
              epilogue=lambda acc, upv: (acc * (2.0 * jnp.maximum(upv, 0.0)),))
    grads["w_up"] = _mm(h2, dup, name="mm_dw_up", ta=True, out_groups=N_CHIPS, out_dtypes=(BF16,))
    mlp_exchange = exchange_begin(group_mlp, "mlp")
    dh2 = _mm(dup, gathered["w_up"], name="mm_dh2", tb=True, b_groups=N_CHIPS, after=mlp_exchange[3])
    dx2, dy, dg_pre_mlp, dg_post_mix = _rms_bwd2(dx3, dh2, x2, y, g_post_mix, g_pre_mlp)
    mlp_started, mlp_token = scatter_begin(group_mlp, "mlp", mlp_exchange, dx2)
    grads["w_o"] = _mm(merged, dy, name="mm_dw_o", ta=True, out_dtypes=(BF16,), after=mlp_token)
    dmerged = _mm(dy, gathered["w_o"], name="mm_dmerged", tb=True)
    do_sb, do_cv, dgate_sb, dgate_cv, s_gate_sb, s_gate_cv = _merge_bwd(proj, o_sb, o_cv, dmerged, off_sb, off_cv, gw)
    grads["w_sb_out"] = _mm(attn, do_sb, name="mm_dw_sb", ta=True, out_groups=N_CHIPS, out_dtypes=(BF16,))
    dattn = _mm(do_sb, gathered["w_sb_out"], name="mm_dattn", tb=True, b_groups=N_CHIPS)
    grads["w_conv_out"] = _mm(u3, do_cv, name="mm_dw_cv", ta=True, out_groups=N_CHIPS, out_dtypes=(BF16,))
    mix_exchange = exchange_begin(group_mix, "mix")
    du3 = _mm(do_cv, gathered["w_conv_out"], name="mm_du3", tb=True, b_groups=N_CHIPS, after=mix_exchange[3])
    mix_started, mix_token = scatter_begin(group_mix, "mix", mix_exchange, du3)
    dq, dk, dv, s_q, s_k, s_v = _attn_bwd(proj, dattn, mix_token, n_heads, d_sb)
    du1, dg_conv_ln, db_conv_ln, db_dw = _conv_bwd_norm(u1, du3, g_conv_ln, b_conv_ln)
    dglu_a, dglu_b, s_a, s_b, dtaps = _conv_bwd_taps(du1, u0, proj, taps, off_a)
    dproj = jnp.concatenate([dq, dk, dv, dglu_a, dglu_b, dgate_sb, dgate_cv], axis=1)
    db_in = jnp.concatenate([s_q, s_k, s_v, s_a, s_b, s_gate_sb, s_gate_cv], axis=1)
    small_w = dict(g_pre_mix=g_pre_mix, b_in=b_in, b_dw=b_dw, g_conv_ln=g_conv_ln, b_conv_ln=b_conv_ln,
                   g_post_mix=g_post_mix, g_pre_mlp=g_pre_mlp, g_post_mlp=g_post_mlp)
    small_m = dict(g_pre_mix=m_g_pre_mix, b_in=m_b_in, b_dw=m_b_dw, g_conv_ln=m_g_conv_ln, b_conv_ln=m_b_conv_ln,
                   g_post_mix=m_g_post_mix, g_pre_mlp=m_g_pre_mlp, g_post_mlp=m_g_post_mlp)
    small_v = dict(g_pre_mix=v_g_pre_mix, b_in=v_b_in, b_dw=v_b_dw, g_conv_ln=v_g_conv_ln, b_conv_ln=v_b_conv_ln,
                   g_post_mix=v_g_post_mix, g_pre_mlp=v_g_pre_mlp, g_post_mlp=v_g_post_mlp)
    small_g = dict(b_in=db_in, b_dw=db_dw, g_conv_ln=dg_conv_ln, b_conv_ln=db_conv_ln,
                   g_post_mix=dg_post_mix, g_pre_mlp=dg_pre_mlp, g_post_mlp=dg_post_mlp)

    def small_update(group, tails, after, tag):
        shapes = [small_w[n].shape for n in group] + [t.shape for t in tails]
        summed = _all_reduce_small(_pack([small_g[n] for n in group] + tails), after, "all_reduce_small_" + tag)
        zeros_tail = [jnp.zeros(t.shape, F32) for t in tails]
        res = _adamw(_pack([small_w[n] for n in group] + zeros_tail), summed,
                     _pack([small_m[n] for n in group] + zeros_tail), _pack([small_v[n] for n in group] + zeros_tail),
                     summed, "adamw_small_" + tag)
        unpacked = [_unpack(r, shapes) for r in res]
        for i, n in enumerate(group):
            out_g[n], out_d[n], out_m[n], out_v[n] = [u[i] for u in unpacked]
        return summed, unpacked[0][len(group):]

    early = ["b_in", "b_dw", "g_conv_ln", "b_conv_ln", "g_post_mix", "g_pre_mlp", "g_post_mlp"]
    summed, (loss_sum, taps_sum) = small_update(early, [loss_part, dtaps], dproj, "early")
    loss = loss_sum[0, 0]
    taps_grad = lax.dynamic_slice(taps_sum, (0, chip * cs), (CONV_HALO, cs))
    res = _adamw(_pad_rows(w_dw[0], CONV_HALO), taps_grad, _pad_rows(m_w_dw[0], CONV_HALO), _pad_rows(v_w_dw[0], CONV_HALO),
                 taps_grad, "adamw_taps")
    out_g["w_dw"], out_d["w_dw"], out_m["w_dw"], out_v["w_dw"] = [r[:CONV_WIDTH][None] for r in res]
    grads["w_in"] = _mm(h, dproj, name="mm_dw_in", ta=True, out_groups=N_CHIPS, out_dtypes=(BF16,), tn=768, after=summed)
    in_exchange = exchange_begin(group_in, "in")
    dh = _mm(dproj, gathered["w_in"], name="mm_dh", tb=True, b_groups=N_CHIPS, tk=2304, after=in_exchange[3])
    in_started, in_token = scatter_begin(group_in, "in", in_exchange, dh)
    grad_x, dg_pre_mix = _rms_bwd1(dx2, dh, xs, g_pre_mix)
    mlp_shared = reduce_sum(group_mlp, "mlp", mlp_started, in_token)
    mix_shared = reduce_sum(group_mix, "mix", mix_started, mlp_shared[3])
    done = reduce_update(group_mlp, "mlp", mlp_shared, mix_shared[3])
    done = reduce_update(group_mix, "mix", mix_shared, done)
    small_g["g_pre_mix"] = dg_pre_mix
    summed, _ = small_update(["g_pre_mix"], [], done, "late")
    in_shared = reduce_sum(group_in, "in", in_started, summed)
    reduce_update(group_in, "in", in_shared, in_shared[3])

    order = ["g_pre_mix", "w_in", "b_in", "w_dw", "b_dw", "g_conv_ln", "b_conv_ln", "w_sb_out", "w_conv_out", "w_o",
             "g_post_mix", "g_pre_mlp", "w_up", "w_down", "g_post_mlp"]
    return (loss, grad_x[None], *[out_g[n] for n in order], *[out_d[n] for n in order],
            *[out_m[n] for n in order], *[out_v[n] for n in order])
```

```python
import math

import jax
import jax.numpy as jnp
from jax import lax
from jax.experimental import pallas as pl
from jax.experimental.pallas import tpu as pltpu

F32 = jnp.float32
BF16 = jnp.bfloat16
MESH = pl.DeviceIdType.MESH

HEAD_DIM = 128
CONV_WIDTH = 31
CONV_HALO = 32
EPS = 1e-6
ADAM_LR = 0.001
ADAM_B1 = 0.9
ADAM_B2 = 0.999
ADAM_EPS = 1e-08
ADAM_WD = 0.01
ADAM_STEP = 10
N_CHIPS = 4
VMEM_LIMIT = 56 * 1024 * 1024


def _params(**kw):
    return pltpu.CompilerParams(vmem_limit_bytes=VMEM_LIMIT, **kw)


def _tile(n, want, align=128):
    if n <= want:
        return n
    for t in range(want - want % align, 0, -align):
        if n % t == 0:
            return t
    raise ValueError((n, want, align))


def _mm(a, b, *, name, ta=False, tb=False, b_groups=1, out_groups=1, bias=None, extra=None, epilogue=None,
        out_dtypes=(F32,), tm=1024, tn=1024, tk=2048, after=None):
    if ta:
        K, M = a.shape
    else:
        M, K = a.shape
    if b_groups == 1:
        br, bc = b.shape
    else:
        _, br, bcg = b.shape
        bc = bcg * b_groups
    if tb:
        N, Kb = br, bc
    else:
        Kb, N = br, bc
    assert Kb == K, (name, a.shape, b.shape)
    tm, tn, tk = _tile(M, tm), _tile(N, tn), _tile(K, tk)
    if b_groups > 1:
        if tb:
            tk = _tile(K // b_groups, tk)
        else:
            tn = _tile(N // b_groups, tn)
    if out_groups > 1:
        tn = _tile(N // out_groups, tn)
        if b_groups > 1 and not tb:
            tn = _tile(N // b_groups, tn)
    nm, nn, nk = M // tm, N // tn, K // tk
    extra = tuple(extra or ())
    n_extra = len(extra)
    has_bias = bias is not None
    n_out = len(out_dtypes)

    a_spec = pl.BlockSpec((tk, tm), lambda i, j, k: (k, i)) if ta else pl.BlockSpec((tm, tk), lambda i, j, k: (i, k))
    if b_groups == 1:
        b_spec = pl.BlockSpec((tn, tk), lambda i, j, k: (j, k)) if tb else pl.BlockSpec((tk, tn), lambda i, j, k: (k, j))
    elif tb:
        kpg = (K // b_groups) // tk
        b_spec = pl.BlockSpec((None, tn, tk), lambda i, j, k: (k // kpg, j, k % kpg))
    else:
        npg = (N // b_groups) // tn
        b_spec = pl.BlockSpec((None, tk, tn), lambda i, j, k: (j // npg, k, j % npg))
    in_specs = [a_spec, b_spec]
    operands = [a, b]
    if has_bias:
        in_specs.append(pl.BlockSpec((1, tn), lambda i, j, k: (0, j)))
        operands.append(bias)
    for e in extra:
        in_specs.append(pl.BlockSpec((tm, tn), lambda i, j, k: (i, j)))
        operands.append(e)
    n_after = 0 if after is None else 1
    if after is not None:
        in_specs.append(pl.BlockSpec(memory_space=pl.ANY))
        operands.append(after)
    if out_groups == 1:
        o_spec = pl.BlockSpec((tm, tn), lambda i, j, k: (i, j))
        o_shape = (M, N)
    else:
        opg = (N // out_groups) // tn
        o_spec = pl.BlockSpec((None, tm, tn), lambda i, j, k: (j // opg, i, j % opg))
        o_shape = (out_groups, M, N // out_groups)
    dims = (((0 if ta else 1,), (1 if tb else 0,)), ((), ()))

    def body(*refs):
        a_ref, b_ref = refs[0], refs[1]
        pos = 2
        bias_ref = None
        if has_bias:
            bias_ref = refs[pos]
            pos += 1
        extra_refs = refs[pos:pos + n_extra]
        pos += n_extra + n_after
        out_refs = refs[pos:pos + n_out]
        pos += n_out
        acc_ref = refs[pos] if nk > 1 else None

        part = lax.dot_general(a_ref[...].astype(BF16), b_ref[...].astype(BF16), dims, preferred_element_type=F32)

        def finish(acc):
            if has_bias:
                acc = acc + bias_ref[...]
            outs = epilogue(acc, *[e[...] for e in extra_refs]) if epilogue is not None else (acc,)
            for o_ref, o in zip(out_refs, outs):
                o_ref[...] = o.astype(o_ref.dtype)

        if nk == 1:
            finish(part)
        else:
            k = pl.program_id(2)

            @pl.when(k == 0)
            def _():
                acc_ref[...] = part

            @pl.when(k > 0)
            def _():
                acc_ref[...] += part

            @pl.when(k == nk - 1)
            def _():
                finish(acc_ref[...])

    outs = pl.pallas_call(
        body,
        name=name,
        grid=(nm, nn, nk),
        in_specs=in_specs,
        out_specs=[o_spec] * n_out,
        out_shape=[jax.ShapeDtypeStruct(o_shape, dt) for dt in out_dtypes],
        scratch_shapes=[pltpu.VMEM((tm, tn), F32)] if nk > 1 else [],
        compiler_params=_params(dimension_semantics=("parallel", "parallel", "arbitrary")),
    )(*operands)
    return outs[0] if n_out == 1 else outs


def _rms_stats(x):
    return lax.rsqrt(jnp.mean(x * x, axis=-1, keepdims=True) + EPS)


def _rms_bwd(x, r, g, dy):
    gy = dy * g
    return r * gy - x * (r * r * r) * jnp.mean(x * gy, axis=-1, keepdims=True)


def _sigmoid(x):
    return 1.0 / (1.0 + jnp.exp(-x))


def _row_call(body, *, name, rows, tr, ins, outs, acc_outs=()):
    n = rows // tr
    in_specs = []
    for arr, kind in ins:
        if kind == "row":
            in_specs.append(pl.BlockSpec((tr, arr.shape[1]), lambda i: (i, 0)))
        else:
            in_specs.append(pl.BlockSpec(arr.shape, lambda i: (0, 0)))
    out_specs = [pl.BlockSpec((tr, w), lambda i: (i, 0)) for w, _ in outs]
    out_shape = [jax.ShapeDtypeStruct((rows, w), dt) for w, dt in outs]
    out_specs += [pl.BlockSpec((1, w), lambda i: (0, 0)) for w in acc_outs]
    out_shape += [jax.ShapeDtypeStruct((1, w), F32) for w in acc_outs]
    return pl.pallas_call(
        body, name=name, grid=(n,), in_specs=in_specs, out_specs=out_specs, out_shape=out_shape,
        compiler_params=_params(dimension_semantics=("arbitrary",)),
    )(*[a for a, _ in ins])


def _accumulate(ref, val):
    @pl.when(pl.program_id(0) == 0)
    def _():
        ref[...] = val

    @pl.when(pl.program_id(0) > 0)
    def _():
        ref[...] += val


def _rms_fwd(x, g, tr=256):
    def body(x_ref, g_ref, h_ref):
        xv = x_ref[...]
        h_ref[...] = (xv * _rms_stats(xv) * g_ref[...]).astype(BF16)

    (h,) = _row_call(body, name="rms_fwd", rows=x.shape[0], tr=tr, ins=[(x, "row"), (g, "vec")], outs=[(x.shape[1], BF16)])
    return h


def _resid_rms(x, y, g2, g3, tr=256):
    def body(x_ref, y_ref, g2_ref, g3_ref, x2_ref, h2_ref):
        yv = y_ref[...]
        x2 = x_ref[...] + yv * _rms_stats(yv) * g2_ref[...]
        x2_ref[...] = x2
        h2_ref[...] = (x2 * _rms_stats(x2) * g3_ref[...]).astype(BF16)

    d = x.shape[1]
    return _row_call(body, name="resid_rms", rows=x.shape[0], tr=tr,
                     ins=[(x, "row"), (y, "row"), (g2, "vec"), (g3, "vec")], outs=[(d, F32), (d, BF16)])


def _final(x2, dn, g4, tgt, tr=256):
    d = x2.shape[1]

    def body(x2_ref, dn_ref, g4_ref, t_ref, dx3_ref, ddn_ref, dg4_ref, loss_ref):
        dn_v = dn_ref[...]
        r = _rms_stats(dn_v)
        g = g4_ref[...]
        e = x2_ref[...] + dn_v * r * g - t_ref[...]
        dx3 = e * (1.0 / d)
        dx3_ref[...] = dx3
        ddn_ref[...] = _rms_bwd(dn_v, r, g, dx3).astype(BF16)
        _accumulate(dg4_ref, jnp.sum(dx3 * dn_v * r, axis=0, keepdims=True))
        part = 0.5 * jnp.sum(jnp.mean(e * e, axis=-1, keepdims=True), axis=0, keepdims=True)
        _accumulate(loss_ref, jnp.broadcast_to(part, loss_ref.shape))

    return _row_call(body, name="final", rows=x2.shape[0], tr=tr,
                     ins=[(x2, "row"), (dn, "row"), (g4, "vec"), (tgt, "row")],
                     outs=[(d, F32), (d, BF16)], acc_outs=(d, 128))


def _rms_bwd2(dx3, dh2, x2, y, g2, g3, tr=256):
    d = x2.shape[1]

    def body(dx3_ref, dh2_ref, x2_ref, y_ref, g2_ref, g3_ref, dx2_ref, dy_ref, dg3_ref, dg2_ref):
        x2v, dh2v, yv = x2_ref[...], dh2_ref[...], y_ref[...]
        r3 = _rms_stats(x2v)
        dx2 = dx3_ref[...] + _rms_bwd(x2v, r3, g3_ref[...], dh2v)
        dx2_ref[...] = dx2
        _accumulate(dg3_ref, jnp.sum(dh2v * x2v * r3, axis=0, keepdims=True))
        r2 = _rms_stats(yv)
        dy_ref[...] = _rms_bwd(yv, r2, g2_ref[...], dx2).astype(BF16)
        _accumulate(dg2_ref, jnp.sum(dx2 * yv * r2, axis=0, keepdims=True))

    return _row_call(body, name="rms_bwd2", rows=x2.shape[0], tr=tr,
                     ins=[(dx3, "row"), (dh2, "row"), (x2, "row"), (y, "row"), (g2, "vec"), (g3, "vec")],
                     outs=[(d, F32), (d, BF16)], acc_outs=(d, d))


def _rms_bwd1(dx2, dh, x, g1, tr=256):
    d = x.shape[1]

    def body(dx2_ref, dh_ref, x_ref, g1_ref, gx_ref, dg1_ref):
        xv, dhv = x_ref[...], dh_ref[...]
        r = _rms_stats(xv)
        gx_ref[...] = dx2_ref[...] + _rms_bwd(xv, r, g1_ref[...], dhv)
        _accumulate(dg1_ref, jnp.sum(dhv * xv * r, axis=0, keepdims=True))

    return _row_call(body, name="rms_bwd1", rows=x.shape[0], tr=tr,
                     ins=[(dx2, "row"), (dh, "row"), (x, "row"), (g1, "vec")], outs=[(d, F32)], acc_outs=(d,))


def _merge_fwd(proj, o_sb, o_cv, off_sb, off_cv, gw, tr=256):
    s, d = o_sb.shape
    nj = d // gw
    b_sb, b_cv = off_sb // gw, off_cv // gw

    def body(gs_ref, gc_ref, osb_ref, ocv_ref, m_ref):
        m_ref[...] = (_sigmoid(gs_ref[...]) * osb_ref[...] + _sigmoid(gc_ref[...]) * ocv_ref[...]).astype(BF16)

    blk = lambda i, j: (i, j)
    return pl.pallas_call(
        body, name="merge_fwd", grid=(s // tr, nj),
        in_specs=[pl.BlockSpec((tr, gw), lambda i, j: (i, b_sb + j)), pl.BlockSpec((tr, gw), lambda i, j: (i, b_cv + j)),
                  pl.BlockSpec((tr, gw), blk), pl.BlockSpec((tr, gw), blk)],
        out_specs=pl.BlockSpec((tr, gw), blk), out_shape=jax.ShapeDtypeStruct((s, d), BF16),
        compiler_params=_params(dimension_semantics=("parallel", "parallel")),
    )(proj, proj, o_sb, o_cv)


def _merge_bwd(proj, o_sb, o_cv, dmerged, off_sb, off_cv, gw, tr=256):
    s, d = o_sb.shape
    nj = d // gw
    ni = s // tr
    b_sb, b_cv = off_sb // gw, off_cv // gw

    def body(gs_ref, gc_ref, osb_ref, ocv_ref, dm_ref, dosb_ref, docv_ref, dgs_ref, dgc_ref, sgs_ref, sgc_ref):
        dm = dm_ref[...]
        s_sb, s_cv = _sigmoid(gs_ref[...]), _sigmoid(gc_ref[...])
        dosb_ref[...] = (dm * s_sb).astype(BF16)
        docv_ref[...] = (dm * s_cv).astype(BF16)
        dgs = dm * osb_ref[...] * s_sb * (1.0 - s_sb)
        dgc = dm * ocv_ref[...] * s_cv * (1.0 - s_cv)
        dgs_ref[...] = dgs.astype(BF16)
        dgc_ref[...] = dgc.astype(BF16)
        i = pl.program_id(1)
        for ref, val in ((sgs_ref, dgs), (sgc_ref, dgc)):
            col = jnp.sum(val, axis=0, keepdims=True)

            @pl.when(i == 0)
            def _():
                ref[...] = col

            @pl.when(i > 0)
            def _():
                ref[...] += col

    blk = lambda j, i: (i, j)
    outs = pl.pallas_call(
        body, name="merge_bwd", grid=(nj, ni),
        in_specs=[pl.BlockSpec((tr, gw), lambda j, i: (i, b_sb + j)), pl.BlockSpec((tr, gw), lambda j, i: (i, b_cv + j)),
                  pl.BlockSpec((tr, gw), blk), pl.BlockSpec((tr, gw), blk), pl.BlockSpec((tr, gw), blk)],
        out_specs=[pl.BlockSpec((tr, gw), blk), pl.BlockSpec((tr, gw), blk),
                   pl.BlockSpec((tr, gw), blk), pl.BlockSpec((tr, gw), blk),
                   pl.BlockSpec((1, gw), lambda j, i: (0, j)), pl.BlockSpec((1, gw), lambda j, i: (0, j))],
        out_shape=[jax.ShapeDtypeStruct((s, d), BF16)] * 4 + [jax.ShapeDtypeStruct((1, d), F32)] * 2,
        compiler_params=_params(dimension_semantics=("parallel", "arbitrary")),
    )(proj, proj, o_sb, o_cv, dmerged)
    return outs


def _split_bf16(v):
    hi = v.astype(BF16)
    lo = (v - hi.astype(F32)).astype(BF16)
    return hi, lo


def _dot_nt(a, b):
    return lax.dot_general(a, b, (((1,), (1,)), ((), ())), preferred_element_type=F32)


def _dot_tn(a, b):
    return lax.dot_general(a, b, (((0,), (0,)), ((), ())), preferred_element_type=F32)


def _dot_nn(a, b):
    return lax.dot_general(a, b, (((1,), (0,)), ((), ())), preferred_element_type=F32)


def _sum_right(v, tri):
    hi, lo = _split_bf16(v)
    return _dot_nn(hi, tri) + _dot_nn(lo, tri)


HEADS_PER_STEP = 4
HEADS_PER_STEP_FWD = 4


LOG2_E = 1.4426950408889634


def _sb_tile(q, kb, scale, mask):
    z = _dot_nt(q, kb) * (scale * LOG2_E)
    log_b = jnp.minimum(z, 0.0) - jnp.log2(1.0 + jnp.exp2(-jnp.abs(z)))
    log_1m = log_b - z
    if mask is not None:
        log_1m = jnp.where(mask, log_1m, 0.0)
    return log_b, log_1m


def _tile_mask(tq):
    row = lax.broadcasted_iota(jnp.int32, (tq, tq), 0)
    col = lax.broadcasted_iota(jnp.int32, (tq, tq), 1)
    return col < row


def _tri(tq, before):
    r = lax.broadcasted_iota(jnp.int32, (tq, tq), 0)
    c = lax.broadcasted_iota(jnp.int32, (tq, tq), 1)
    return jnp.where((r < c) if before else (r > c), 1.0, 0.0).astype(BF16)


def _attn_fwd(proj, after, n_heads, d_sb, tq=256):
    s = proj.shape[0]
    tq = _tile(s, tq)
    hb = d_sb // HEAD_DIM
    scale = 1.0 / math.sqrt(HEAD_DIM)

    per_step = min(HEADS_PER_STEP_FWD, n_heads)
    heads = [pl.ds(n * HEAD_DIM, HEAD_DIM) for n in range(per_step)]
    wide = per_step * HEAD_DIM
    hb //= per_step

    def body(q_ref, k_ref, v_ref, _, o_ref):
        i = pl.program_id(1)
        tri = _tri(tq, False)

        def tile(j, carry, diagonal):
            rows = pl.ds(pl.multiple_of(j * tq, tq), tq)
            mask = _tile_mask(tq) if diagonal else None
            out = []
            for hd, (c_l, acc) in zip(heads, carry):
                log_b, log_1m = _sb_tile(q_ref[:, hd].astype(BF16), k_ref[rows, hd].astype(BF16), scale, mask)
                a = jnp.exp2(log_b + (c_l + _sum_right(log_1m, tri)))
                if diagonal:
                    a = jnp.where(mask, a, 0.0)
                out.append((c_l + jnp.sum(log_1m, axis=1, keepdims=True),
                            acc + _dot_nn(a.astype(BF16), v_ref[rows, hd].astype(BF16))))
            return tuple(out)

        init = tuple((jnp.zeros((tq, 1), F32), jnp.zeros((tq, HEAD_DIM), F32)) for _ in heads)
        carry = lax.fori_loop(0, i, lambda jj, c: tile(i - 1 - jj, c, False), tile(i, init, True))
        for hd, (_, acc) in zip(heads, carry):
            o_ref[:, hd] = acc.astype(BF16)

    return pl.pallas_call(
        body, name="attn_fwd", grid=(n_heads // per_step, s // tq),
        in_specs=[pl.BlockSpec((tq, wide), lambda h, i: (i, h)),
                  pl.BlockSpec((s, wide), lambda h, i: (0, hb + h)),
                  pl.BlockSpec((s, wide), lambda h, i: (0, 2 * hb + h)), pl.BlockSpec(memory_space=pl.ANY)],
        out_specs=pl.BlockSpec((tq, wide), lambda h, i: (i, h)),
        out_shape=jax.ShapeDtypeStruct((s, d_sb), BF16),
        compiler_params=_params(dimension_semantics=("parallel", "arbitrary")),
    )(proj, proj, proj, after)


def _attn_bwd(proj, dattn, after, n_heads, d_sb, tq=256):
    s = proj.shape[0]
    tq = _tile(s, tq)
    nq = s // tq
    hb = d_sb // HEAD_DIM // HEADS_PER_STEP
    scale = 1.0 / math.sqrt(HEAD_DIM)
    heads = [pl.ds(n * HEAD_DIM, HEAD_DIM) for n in range(HEADS_PER_STEP)]
    wide = HEADS_PER_STEP * HEAD_DIM

    def body(q_ref, k_ref, v_ref, do_ref, _, dq_ref, dk_ref, dv_ref, sq_ref, sk_ref, sv_ref, dk_acc, dv_acc, g_st, b_st):
        i = pl.program_id(1)

        @pl.when(i == 0)
        def _():
            dk_acc[...] = jnp.zeros_like(dk_acc)
            dv_acc[...] = jnp.zeros_like(dv_acc)

        tri_after = _tri(tq, False)
        tri_before = _tri(tq, True)

        def newest_first(j, c_ls, diagonal):
            rows = pl.ds(pl.multiple_of(j * tq, tq), tq)
            mask = _tile_mask(tq) if diagonal else None
            out = []
            for n, (hd, c_l) in enumerate(zip(heads, c_ls)):
                do = do_ref[:, hd].astype(BF16)
                log_b, log_1m = _sb_tile(q_ref[:, hd].astype(BF16), k_ref[rows, hd].astype(BF16), scale, mask)
                a = jnp.exp2(log_b + (c_l + _sum_right(log_1m, tri_after)))
                beta = jnp.exp2(log_b)
                if diagonal:
                    a = jnp.where(mask, a, 0.0)
                    beta = jnp.where(mask, beta, 0.0)
                g_st[n, j] = a * _dot_nt(do, v_ref[rows, hd].astype(BF16))
                b_st[n, j] = beta
                dv_acc[rows, hd] += _dot_tn(a.astype(BF16), do)
                out.append(c_l + jnp.sum(log_1m, axis=1, keepdims=True))
            return tuple(out)

        lax.fori_loop(0, i, lambda jj, c: newest_first(i - 1 - jj, c, False),
                      newest_first(i, tuple(jnp.zeros((tq, 1), F32) for _ in heads), True))

        def oldest_first(j, carry):
            rows = pl.ds(pl.multiple_of(j * tq, tq), tq)
            out = []
            for n, (hd, (c_g, dq)) in enumerate(zip(heads, carry)):
                kb = k_ref[rows, hd].astype(BF16)
                g = g_st[n, j]
                beta = b_st[n, j]
                g_before = c_g + _sum_right(g, tri_before)
                dz16 = ((g * (1.0 - beta) - g_before * beta) * scale).astype(BF16)
                dk_acc[rows, hd] += _dot_tn(dz16, q_ref[:, hd].astype(BF16))
                out.append((c_g + jnp.sum(g, axis=1, keepdims=True), dq + _dot_nn(dz16, kb)))
            return tuple(out)

        init = tuple((jnp.zeros((tq, 1), F32), jnp.zeros((tq, HEAD_DIM), F32)) for _ in heads)
        carry = lax.fori_loop(0, i + 1, oldest_first, init)
        dq = jnp.concatenate([c[1] for c in carry], axis=1)
        dq_ref[...] = dq.astype(BF16)
        col = jnp.sum(dq, axis=0, keepdims=True)

        @pl.when(i == 0)
        def _():
            sq_ref[...] = col

        @pl.when(i > 0)
        def _():
            sq_ref[...] += col

        @pl.when(i == nq - 1)
        def _():
            dk = dk_acc[...]
            dv = dv_acc[...]
            dk_ref[...] = dk.astype(BF16)
            dv_ref[...] = dv.astype(BF16)
            sk_ref[...] = jnp.sum(dk, axis=0, keepdims=True)
            sv_ref[...] = jnp.sum(dv, axis=0, keepdims=True)

    qblk = pl.BlockSpec((tq, wide), lambda h, i: (i, h))
    kblk = pl.BlockSpec((s, wide), lambda h, i: (0, h))
    col_sum = pl.BlockSpec((1, wide), lambda h, i: (0, h))
    stash = pltpu.VMEM((HEADS_PER_STEP, nq, tq, tq), F32)
    outs = pl.pallas_call(
        body, name="attn_bwd", grid=(n_heads // HEADS_PER_STEP, nq),
        in_specs=[qblk,
                  pl.BlockSpec((s, wide), lambda h, i: (0, hb + h)),
                  pl.BlockSpec((s, wide), lambda h, i: (0, 2 * hb + h)),
                  qblk, pl.BlockSpec(memory_space=pl.ANY)],
        out_specs=[qblk, kblk, kblk, col_sum, col_sum, col_sum],
        out_shape=[jax.ShapeDtypeStruct((s, d_sb), BF16)] * 3 + [jax.ShapeDtypeStruct((1, d_sb), F32)] * 3,
        scratch_shapes=[pltpu.VMEM((s, wide), F32), pltpu.VMEM((s, wide), F32), stash, stash],
        compiler_params=_params(dimension_semantics=("parallel", "arbitrary")),
    )(proj, proj, proj, dattn, after)
    return outs


LANES = 128


def _glu(proj, off_a, dc, tr=256):
    s = proj.shape[0]
    ba = off_a // dc

    def body(a_ref, b_ref, u_ref):
        u_ref[...] = a_ref[...] * _sigmoid(b_ref[...])

    return pl.pallas_call(
        body, name="glu", grid=(s // tr,),
        in_specs=[pl.BlockSpec((tr, dc), lambda i: (i, ba)), pl.BlockSpec((tr, dc), lambda i: (i, ba + 1))],
        out_specs=pl.BlockSpec((tr, dc), lambda i: (i, 0)), out_shape=jax.ShapeDtypeStruct((s, dc), F32),
        compiler_params=_params(dimension_semantics=("parallel",)),
    )(proj, proj)


SUBLANES = 8


def _shifted_rows(src, cols, shifted, tt):
    span = tt + CONV_HALO - SUBLANES
    for b in range(1, SUBLANES):
        shifted[b, pl.ds(0, span), :] = src[pl.ds(b, span), cols]

    def rows(o):
        a, b = divmod(o, SUBLANES)
        return src[pl.ds(SUBLANES * a, tt), cols] if b == 0 else shifted[b, pl.ds(SUBLANES * a, tt), :]

    return rows


def _conv_fwd(u0, taps, b_dw, g_ln, b_ln, tt=256):
    s, dc = u0.shape
    tt = _tile(s, tt)
    hpb = tt // CONV_HALO

    def body(cur_ref, halo_ref, w_ref, bdw_ref, g_ref, b_ref, u1_ref, u3_ref, xs, xs_shifted):
        i = pl.program_id(0)
        xs[pl.ds(0, CONV_HALO), :] = jnp.where(i > 0, halo_ref[...], 0.0)
        xs[pl.ds(CONV_HALO, tt), :] = cur_ref[...]
        for c0 in range(0, dc, LANES):
            cols = pl.ds(c0, LANES)
            x_rows = _shifted_rows(xs, cols, xs_shifted, tt)
            acc = jnp.broadcast_to(bdw_ref[:, cols], (tt, LANES))
            for k in range(CONV_WIDTH):
                acc = acc + w_ref[pl.ds(k, 1), cols] * x_rows(CONV_HALO - (CONV_WIDTH - 1) + k)
            u1_ref[:, cols] = acc
        u1 = u1_ref[...]
        mu = jnp.mean(u1, axis=-1, keepdims=True)
        xc = u1 - mu
        rstd = lax.rsqrt(jnp.mean(xc * xc, axis=-1, keepdims=True) + EPS)
        u2 = xc * rstd * g_ref[...] + b_ref[...]
        u3_ref[...] = (u2 * _sigmoid(u2)).astype(BF16)

    vec = pl.BlockSpec((1, dc), lambda i: (0, 0))
    return pl.pallas_call(
        body, name="conv_fwd", grid=(s // tt,),
        in_specs=[pl.BlockSpec((tt, dc), lambda i: (i, 0)),
                  pl.BlockSpec((CONV_HALO, dc), lambda i: (jnp.maximum(i * hpb - 1, 0), 0)),
                  pl.BlockSpec((CONV_HALO, dc), lambda i: (0, 0)), vec, vec, vec],
        out_specs=[pl.BlockSpec((tt, dc), lambda i: (i, 0)), pl.BlockSpec((tt, dc), lambda i: (i, 0))],
        out_shape=[jax.ShapeDtypeStruct((s, dc), F32), jax.ShapeDtypeStruct((s, dc), BF16)],
        scratch_shapes=[pltpu.VMEM((tt + CONV_HALO, dc), F32), pltpu.VMEM((SUBLANES, tt + CONV_HALO, LANES), F32)],
        compiler_params=_params(dimension_semantics=("arbitrary",)),
    )(u0, u0, taps, b_dw, g_ln, b_ln)


def _conv_bwd_norm(u1, du3, g_ln, b_ln, tr=256):
    dc = u1.shape[1]

    def body(u1_ref, du3_ref, g_ref, b_ref, du1_ref, dg_ref, db_ref, dbdw_ref):
        u1v = u1_ref[...]
        g = g_ref[...]
        mu = jnp.mean(u1v, axis=-1, keepdims=True)
        xc = u1v - mu
        rstd = lax.rsqrt(jnp.mean(xc * xc, axis=-1, keepdims=True) + EPS)
        xhat = xc * rstd
        u2 = xhat * g + b_ref[...]
        sg = _sigmoid(u2)
        du2 = du3_ref[...] * (sg * (1.0 + u2 * (1.0 - sg)))
        _accumulate(dg_ref, jnp.sum(du2 * xhat, axis=0, keepdims=True))
        _accumulate(db_ref, jnp.sum(du2, axis=0, keepdims=True))
        gy = du2 * g
        du1 = rstd * (gy - jnp.mean(gy, axis=-1, keepdims=True) - xhat * jnp.mean(gy * xhat, axis=-1, keepdims=True))
        du1_ref[...] = du1
        _accumulate(dbdw_ref, jnp.sum(du1, axis=0, keepdims=True))

    return _row_call(body, name="conv_bwd_norm", rows=u1.shape[0], tr=tr,
                     ins=[(u1, "row"), (du3, "row"), (g_ln, "vec"), (b_ln, "vec")],
                     outs=[(dc, F32)], acc_outs=(dc, dc, dc))


def _conv_bwd_taps(du1, u0, proj, taps, off_a, tt=256):
    s, dc = u0.shape
    tt = _tile(s, tt)
    n = s // tt
    hpb = tt // CONV_HALO
    ba = off_a // dc
    lead = CONV_HALO - (CONV_WIDTH - 1)

    def body(d_ref, dnext_ref, cur_ref, halo_ref, a_ref, b_ref, w_ref, da_ref, db_ref, sa_ref, sb_ref, dw_ref,
             xs, ds, du0, xs_shifted, ds_shifted):
        i = pl.program_id(0)
        xs[pl.ds(0, CONV_HALO), :] = jnp.where(i > 0, halo_ref[...], 0.0)
        xs[pl.ds(CONV_HALO, tt), :] = cur_ref[...]
        ds[pl.ds(0, tt), :] = d_ref[...]
        ds[pl.ds(tt, CONV_HALO), :] = jnp.where(i < n - 1, dnext_ref[...], 0.0)

        @pl.when(i == 0)
        def _():
            dw_ref[...] = jnp.zeros_like(dw_ref)

        for c0 in range(0, dc, LANES):
            cols = pl.ds(c0, LANES)
            x_rows = _shifted_rows(xs, cols, xs_shifted, tt)
            d_rows = _shifted_rows(ds, cols, ds_shifted, tt)
            d_cur = d_rows(0)
            acc = jnp.zeros((tt, LANES), F32)
            for k in range(CONV_WIDTH):
                acc = acc + w_ref[pl.ds(k, 1), cols] * d_rows(CONV_WIDTH - 1 - k)
                dw_ref[pl.ds(k, 1), cols] += jnp.sum(d_cur * x_rows(lead + k), axis=0, keepdims=True)
            du0[:, cols] = acc
        d0 = du0[...]
        sg = _sigmoid(b_ref[...])
        da = d0 * sg
        db = d0 * a_ref[...] * sg * (1.0 - sg)
        da_ref[...] = da.astype(BF16)
        db_ref[...] = db.astype(BF16)
        _accumulate(sa_ref, jnp.sum(da, axis=0, keepdims=True))
        _accumulate(sb_ref, jnp.sum(db, axis=0, keepdims=True))

    row = pl.BlockSpec((tt, dc), lambda i: (i, 0))
    vec = pl.BlockSpec((1, dc), lambda i: (0, 0))
    taps_spec = pl.BlockSpec((CONV_HALO, dc), lambda i: (0, 0))
    return pl.pallas_call(
        body, name="conv_bwd_taps", grid=(n,),
        in_specs=[row, pl.BlockSpec((CONV_HALO, dc), lambda i: (jnp.minimum((i + 1) * hpb, n * hpb - 1), 0)),
                  row, pl.BlockSpec((CONV_HALO, dc), lambda i: (jnp.maximum(i * hpb - 1, 0), 0)),
                  pl.BlockSpec((tt, dc), lambda i: (i, ba)), pl.BlockSpec((tt, dc), lambda i: (i, ba + 1)), taps_spec],
        out_specs=[row, row, vec, vec, taps_spec],
        out_shape=[jax.ShapeDtypeStruct((s, dc), BF16), jax.ShapeDtypeStruct((s, dc), BF16),
                   jax.ShapeDtypeStruct((1, dc), F32), jax.ShapeDtypeStruct((1, dc), F32),
                   jax.ShapeDtypeStruct((CONV_HALO, dc), F32)],
        scratch_shapes=[pltpu.VMEM((tt + CONV_HALO, dc), F32), pltpu.VMEM((tt + CONV_HALO, dc), F32),
                        pltpu.VMEM((tt, dc), F32),
                        pltpu.VMEM((SUBLANES, tt + CONV_HALO, LANES), F32), pltpu.VMEM((SUBLANES, tt + CONV_HALO, LANES), F32)],
        compiler_params=_params(dimension_semantics=("arbitrary",)),
    )(du1, du1, u0, u0, proj, proj, taps)


def _elementwise(body, *, name, ins, out_dtypes, after, tr=128):
    rows, cols = ins[0].shape
    tr = _tile(rows, tr, 8)
    spec = pl.BlockSpec((tr, cols), lambda i: (i, 0))
    n_in = len(ins)

    def with_after(*refs):
        body(*refs[:n_in], *refs[n_in + 1:])

    return pl.pallas_call(
        with_after, name=name, grid=(rows // tr,), in_specs=[spec] * n_in + [pl.BlockSpec(memory_space=pl.ANY)],
        out_specs=[spec] * len(out_dtypes),
        out_shape=[jax.ShapeDtypeStruct((rows, cols), dt) for dt in out_dtypes],
        compiler_params=_params(dimension_semantics=("parallel",)),
    )(*ins, after)


def _prefetched(body, *, name, where, grid, in_specs, out_specs, out_shape, operands):
    return pl.pallas_call(
        body, name=name, out_shape=out_shape,
        grid_spec=pltpu.PrefetchScalarGridSpec(num_scalar_prefetch=1, grid=grid, in_specs=in_specs, out_specs=out_specs),
        compiler_params=_params(dimension_semantics=("parallel",) * len(grid)),
    )(where, *operands)


def _cast_into_block(w, where, after, name, tr=256):
    rows, cols = w.shape
    tr = _tile(rows, tr, 16)

    def body(where_ref, w_ref, _, o_ref):
        o_ref[...] = w_ref[...].astype(BF16)

    return _prefetched(
        body, name=name, where=where, grid=(rows // tr,),
        in_specs=[pl.BlockSpec((tr, cols), lambda i, wh: (i, 0)), pl.BlockSpec(memory_space=pl.ANY)],
        out_specs=pl.BlockSpec((None, tr, cols), lambda i, wh: (wh[0], i, 0)),
        out_shape=jax.ShapeDtypeStruct((N_CHIPS, rows, cols), BF16), operands=[w, after])


def _add_pair(grad, theirs, where, name, tr=512):
    nb, half, cols = theirs.shape
    tr = _tile(half, tr, 16)
    nh = half // tr

    def body(where_ref, a_ref, b_ref, o_ref):
        o_ref[...] = a_ref[...] + b_ref[...]

    blk = pl.BlockSpec((None, tr, cols), lambda b, i, wh: (b, i, 0))
    return _prefetched(
        body, name=name, where=where, grid=(nb, nh),
        in_specs=[pl.BlockSpec((None, tr, cols), lambda b, i, wh: (b, wh[1] * nh + i, 0)), blk],
        out_specs=blk, out_shape=jax.ShapeDtypeStruct(theirs.shape, BF16), operands=[grad, theirs])


def _sum_slots(part, slots, where, name, tr=512):
    _, half, cols = part.shape
    tr = _tile(half, tr, 16)
    nh = half // tr

    def body(where_ref, p_ref, s_ref, o_ref):
        o_ref[...] = ((p_ref[...].astype(F32) + s_ref[0].astype(F32)) + s_ref[1].astype(F32)) + s_ref[2].astype(F32)

    return _prefetched(
        body, name=name, where=where, grid=(nh,),
        in_specs=[pl.BlockSpec((None, tr, cols), lambda i, wh: (wh[0], i, 0)),
                  pl.BlockSpec((3, tr, cols), lambda i, wh: (0, i, 0))],
        out_specs=pl.BlockSpec((tr, cols), lambda i, wh: (wh[1] * nh + i, 0)),
        out_shape=jax.ShapeDtypeStruct((2 * half, cols), F32), operands=[part, slots])


def _adamw(w, g, m, v, after, name):
    bc1 = 1.0 - ADAM_B1 ** ADAM_STEP
    bc2 = 1.0 - ADAM_B2 ** ADAM_STEP

    def body(w_ref, g_ref, m_ref, v_ref, go_ref, d_ref, mo_ref, vo_ref):
        g = g_ref[...]
        m_new = ADAM_B1 * m_ref[...] + (1.0 - ADAM_B1) * g
        v_new = ADAM_B2 * v_ref[...] + (1.0 - ADAM_B2) * (g * g)
        m_hat = m_new / bc1
        v_hat = v_new / bc2
        go_ref[...] = g
        d_ref[...] = -ADAM_LR * (m_hat / (jnp.sqrt(v_hat) + ADAM_EPS) + ADAM_WD * w_ref[...])
        mo_ref[...] = m_new
        vo_ref[...] = v_new

    return _elementwise(body, name=name, ins=[w, g, m, v], out_dtypes=[F32] * 4, after=after, tr=128)


ANY = pl.BlockSpec(memory_space=pl.ANY)


def _mesh_pos():
    return lax.axis_index("x"), lax.axis_index("y"), lax.axis_index("c")


def _other_chips(x, y):
    return [(1 - x, y), (x, 1 - y), (1 - x, 1 - y)]


def _remote(src, dst, send_sems, recv_sems, idx, to):
    return pltpu.make_async_remote_copy(src_ref=src, dst_ref=dst, send_sem=send_sems.at[idx], recv_sem=recv_sems.at[idx],
                                        device_id=to, device_id_type=MESH)


HBM = pl.BlockSpec(memory_space=pltpu.HBM)
SEM = pl.BlockSpec(memory_space=pltpu.SEMAPHORE)
EFFECT = pltpu.SideEffectType.DATAFLOW_SIDE_EFFECTING


def _in_hbm(arrays):
    return [pltpu.with_memory_space_constraint(a, pltpu.HBM) for a in arrays]


def _ici_copies(slab, send_sems, recv_sems, x, y, c):
    half = slab.shape[1] // 2
    rows = pl.ds(c * half, half)
    mine = slab.at[2 * x + y, rows]
    out = []
    for j, (cx, cy) in enumerate(_other_chips(x, y)):
        got = slab.at[2 * cx + cy, rows]
        out.append((_remote(mine, mine, send_sems, recv_sems, j, (cx, cy, c)),
                    _remote(got, got, send_sems, recv_sems, j, (cx, cy, c))))
    return out


def _gather_start(slabs, after, name):
    n = len(slabs)

    def body(*refs):
        ins = refs[:n]
        send_sems, recv_sems, token = refs[2 * n + 1:3 * n + 1], refs[3 * n + 1:4 * n + 1], refs[4 * n + 1]
        x, y, c = _mesh_pos()
        for w in range(n):
            for send, _ in _ici_copies(ins[w], send_sems[w], recv_sems[w], x, y, c):
                send.start()
        token[...] = jnp.zeros_like(token)

    outs = pl.pallas_call(
        body, name=name, in_specs=[HBM] * n + [ANY],
        out_specs=[HBM] * n + [SEM] * (2 * n) + [pl.BlockSpec(memory_space=pltpu.VMEM)],
        out_shape=[pltpu.HBM(s.shape, s.dtype) for s in slabs] + [pltpu.SemaphoreType.DMA((3,))] * (2 * n)
        + [jax.ShapeDtypeStruct((8, 128), F32)],
        input_output_aliases={w: w for w in range(n)},
        compiler_params=_params(has_side_effects=EFFECT),
    )(*_in_hbm(slabs), after)
    return (outs[:n], outs[n:2 * n], outs[2 * n:3 * n]), outs[3 * n]


def _gather_wait(slabs, send_sems, recv_sems, after, name):
    n = len(slabs)

    def body(*refs):
        ins = refs[:n]
        sends, recvs = refs[n:2 * n], refs[2 * n:3 * n]
        x, y, c = _mesh_pos()
        for w in range(n):
            for send, recv in _ici_copies(ins[w], sends[w], recvs[w], x, y, c):
                send.wait_send()
                recv.wait_recv()

    return pl.pallas_call(
        body, name=name, in_specs=[HBM] * n + [SEM] * (2 * n) + [ANY], out_specs=[HBM] * n,
        out_shape=[pltpu.HBM(s.shape, s.dtype) for s in slabs],
        input_output_aliases={w: w for w in range(n)},
        compiler_params=_params(has_side_effects=EFFECT),
    )(*slabs, *send_sems, *recv_sems, after)


def _phase_start(arrays, copies, n_copies, after, name):
    n = len(arrays)

    def body(*refs):
        send_sems, recv_sems, token = refs[2 * n + 1], refs[2 * n + 2], refs[2 * n + 3]
        for send, _ in copies(refs[:n], send_sems, recv_sems, *_mesh_pos()):
            send.start()
        token[...] = jnp.zeros_like(token)

    outs = pl.pallas_call(
        body, name=name, in_specs=[HBM] * n + [ANY],
        out_specs=[HBM] * n + [SEM, SEM, pl.BlockSpec(memory_space=pltpu.VMEM)],
        out_shape=[pltpu.HBM(a.shape, a.dtype) for a in arrays]
        + [pltpu.SemaphoreType.DMA((n_copies,))] * 2 + [jax.ShapeDtypeStruct((8, 128), F32)],
        input_output_aliases={k: k for k in range(n)},
        compiler_params=_params(has_side_effects=EFFECT),
    )(*_in_hbm(arrays), after)
    return outs[:n], outs[n], outs[n + 1], outs[n + 2]


def _phase_wait(arrays, send_sems, recv_sems, copies, after, name):
    n = len(arrays)

    def body(*refs):
        for send, recv in copies(refs[:n], refs[n], refs[n + 1], *_mesh_pos()):
            send.wait_send()
            recv.wait_recv()

    return pl.pallas_call(
        body, name=name, in_specs=[HBM] * n + [SEM, SEM, ANY], out_specs=[HBM] * n,
        out_shape=[pltpu.HBM(a.shape, a.dtype) for a in arrays],
        input_output_aliases={k: k for k in range(n)},
        compiler_params=_params(has_side_effects=EFFECT),
    )(*arrays, send_sems, recv_sems, after)


def _forward_copies(slabs, send_sems, recv_sems, x, y, c):
    sibling = (x, y, 1 - c)
    out = []
    for w, slab in enumerate(slabs):
        half = slab.shape[1] // 2
        for j, (cx, cy) in enumerate(_other_chips(x, y)):
            mine = slab.at[2 * cx + cy, pl.ds(c * half, half)]
            theirs = slab.at[2 * cx + cy, pl.ds((1 - c) * half, half)]
            out.append((_remote(mine, mine, send_sems, recv_sems, 3 * w + j, sibling),
                        _remote(theirs, theirs, send_sems, recv_sems, 3 * w + j, sibling)))
    return out


def _share_copies(halves, send_sems, recv_sems, x, y, c):
    sibling = (x, y, 1 - c)
    out = []
    for w, full in enumerate(halves):
        h = full.shape[0] // 2
        mine, theirs = full.at[pl.ds(c * h, h)], full.at[pl.ds((1 - c) * h, h)]
        out.append((_remote(mine, mine, send_sems, recv_sems, w, sibling),
                    _remote(theirs, theirs, send_sems, recv_sems, w, sibling)))
    return out


def _exchange_copies(arrays, send_sems, recv_sems, x, y, c):
    n = len(arrays) // 2
    sibling = (x, y, 1 - c)
    out = []
    for w in range(n):
        grad, land = arrays[w], arrays[n + w]
        half = grad.shape[1] // 2
        out.append((_remote(grad.at[:, pl.ds((1 - c) * half, half), :], land, send_sems, recv_sems, w, sibling),
                    _remote(land, land, send_sems, recv_sems, w, sibling)))
    return out


def _gather_taps(taps):
    def body(in_ref, out_ref, send_sems, recv_sems, local_sem):
        x, y, c = _mesh_pos()
        me = 2 * x + y
        chips = _other_chips(x, y)
        lc = pltpu.make_async_copy(in_ref, out_ref.at[me], local_sem)
        lc.start()
        sends = [_remote(in_ref, out_ref.at[me], send_sems, recv_sems, j, (cx, cy, c)) for j, (cx, cy) in enumerate(chips)]
        for cp in sends:
            cp.start()
        for j, (cx, cy) in enumerate(chips):
            got = out_ref.at[2 * cx + cy]
            _remote(got, got, send_sems, recv_sems, j, (cx, cy, c)).wait_recv()
        for cp in sends:
            cp.wait_send()
        lc.wait()

    vm = pl.BlockSpec(memory_space=pltpu.VMEM)
    return pl.pallas_call(
        body, name="gather_taps", in_specs=[vm], out_specs=vm,
        out_shape=jax.ShapeDtypeStruct((N_CHIPS,) + taps.shape, taps.dtype),
        scratch_shapes=[pltpu.SemaphoreType.DMA((3,)), pltpu.SemaphoreType.DMA((3,)), pltpu.SemaphoreType.DMA],
        compiler_params=_params(has_side_effects=True),
    )(taps)


def _scatter_copies(part, slots, send_sems, recv_sems, x, y, c):
    out = []
    for j, (cx, cy) in enumerate(_other_chips(x, y)):
        out.append((_remote(part.at[2 * cx + cy], slots.at[j], send_sems, recv_sems, j, (cx, cy, c)),
                    _remote(slots.at[j], slots.at[j], send_sems, recv_sems, j, (cx, cy, c))))
    return out


def _scatter_start(parts, after, name):
    n = len(parts)
    slots = [lax.empty((3,) + p.shape[1:], p.dtype) for p in parts]

    def body(*refs):
        ins, lands = refs[:n], refs[n:2 * n]
        send_sems, recv_sems, token = refs[4 * n + 1:5 * n + 1], refs[5 * n + 1:6 * n + 1], refs[6 * n + 1]
        x, y, c = _mesh_pos()
        for w in range(n):
            for send, _ in _scatter_copies(ins[w], lands[w], send_sems[w], recv_sems[w], x, y, c):
                send.start()
        token[...] = jnp.zeros_like(token)

    outs = pl.pallas_call(
        body, name=name, in_specs=[HBM] * (2 * n) + [ANY],
        out_specs=[HBM] * (2 * n) + [SEM] * (2 * n) + [pl.BlockSpec(memory_space=pltpu.VMEM)],
        out_shape=[pltpu.HBM(a.shape, a.dtype) for a in list(parts) + slots] + [pltpu.SemaphoreType.DMA((3,))] * (2 * n)
        + [jax.ShapeDtypeStruct((8, 128), F32)],
        input_output_aliases={k: k for k in range(2 * n)},
        compiler_params=_params(has_side_effects=EFFECT),
    )(*_in_hbm(list(parts) + slots), after)
    return (outs[:n], outs[n:2 * n], outs[2 * n:3 * n], outs[3 * n:4 * n]), outs[4 * n]


def _scatter_wait(parts, slots, send_sems, recv_sems, after, name):
    n = len(parts)

    def body(*refs):
        ins, lands = refs[:n], refs[n:2 * n]
        sends, recvs = refs[2 * n:3 * n], refs[3 * n:4 * n]
        x, y, c = _mesh_pos()
        for w in range(n):
            for send, recv in _scatter_copies(ins[w], lands[w], sends[w], recvs[w], x, y, c):
                send.wait_send()
                recv.wait_recv()

    outs = pl.pallas_call(
        body, name=name, in_specs=[HBM] * (2 * n) + [SEM] * (2 * n) + [ANY], out_specs=[HBM] * (2 * n),
        out_shape=[pltpu.HBM(a.shape, a.dtype) for a in list(parts) + list(slots)],
        input_output_aliases={k: k for k in range(2 * n)},
        compiler_params=_params(has_side_effects=EFFECT),
    )(*parts, *slots, *send_sems, *recv_sems, after)
    return outs[:n], outs[n:]


def _all_reduce_small(buf, after, name):
    rows, width = buf.shape

    def body(in_ref, _, out_ref, gathered, send_sems, recv_sems):
        x, y, c = _mesh_pos()
        me = 4 * x + 2 * y + c
        gathered[me] = in_ref[...]
        flips = [(dx, dy, dz) for dx in (0, 1) for dy in (0, 1) for dz in (0, 1)][1:]
        peers = [((1 - x) if dx else x, (1 - y) if dy else y, (1 - c) if dz else c) for dx, dy, dz in flips]
        sends = [_remote(in_ref, gathered.at[me], send_sems, recv_sems, k, peer) for k, peer in enumerate(peers)]
        for cp in sends:
            cp.start()
        for k, (px, py, pc) in enumerate(peers):
            got = gathered.at[4 * px + 2 * py + pc]
            _remote(got, got, send_sems, recv_sems, k, (px, py, pc)).wait_recv()
        for cp in sends:
            cp.wait_send()
        total = gathered[0]
        for d in range(1, 8):
            total = total + gathered[d]
        out_ref[...] = total

    vm = pl.BlockSpec(memory_space=pltpu.VMEM)
    return pl.pallas_call(
        body, name=name, in_specs=[vm, ANY], out_specs=vm, out_shape=jax.ShapeDtypeStruct(buf.shape, F32),
        scratch_shapes=[pltpu.VMEM((8, rows, width), F32), pltpu.SemaphoreType.DMA((7,)), pltpu.SemaphoreType.DMA((7,))],
        compiler_params=_params(has_side_effects=True),
    )(buf, after)


PACK_W = 2048
PACK_ROWS = 8


def _pack(vectors):
    flat = jnp.concatenate([v.reshape(-1) for v in vectors])
    unit = PACK_W * PACK_ROWS
    total = -(-flat.shape[0] // unit) * unit
    return jnp.pad(flat, (0, total - flat.shape[0])).reshape(total // PACK_W, PACK_W)


def _unpack(buf, shapes):
    flat = buf.reshape(-1)
    out, pos = [], 0
    for shp in shapes:
        size = math.prod(shp)
        out.append(flat[pos:pos + size].reshape(shp))
        pos += size
    return out


def _pad_rows(a, rows):
    return jnp.pad(a, ((0, rows - a.shape[0]), (0, 0)))


def kernel(x, g_pre_mix, w_in, b_in, w_dw, b_dw, g_conv_ln, b_conv_ln, w_sb_out, w_conv_out, w_o, g_post_mix, g_pre_mlp, w_up, w_down, g_post_mlp, loss_target, m_g_pre_mix, m_w_in, m_b_in, m_w_dw, m_b_dw, m_g_conv_ln, m_b_conv_ln, m_w_sb_out, m_w_conv_out, m_w_o, m_g_post_mix, m_g_pre_mlp, m_w_up, m_w_down, m_g_post_mlp, v_g_pre_mix, v_w_in, v_b_in, v_w_dw, v_b_dw, v_g_conv_ln, v_b_conv_ln, v_w_sb_out, v_w_conv_out, v_w_o, v_g_post_mix, v_g_pre_mlp, v_w_up, v_w_down, v_g_post_mlp):
    xs = x[0]
    tgt = loss_target[0]
    s, d = xs.shape
    d_sb = w_sb_out.shape[1]
    dc = w_conv_out.shape[1]
    d_in = w_in.shape[2] * N_CHIPS
    n_heads = d_sb // HEAD_DIM
    off_a = 3 * d_sb
    off_sb = off_a + 2 * dc
    off_cv = off_sb + d
    gw = dc
    assert off_a % dc == 0 and off_sb % gw == 0 and d % gw == 0 and d_in == off_cv + d
    chip = 2 * lax.axis_index("x") + lax.axis_index("y")

    big = dict(w_in=w_in[0], w_sb_out=w_sb_out[0], w_conv_out=w_conv_out[0], w_o=w_o[0], w_up=w_up[0], w_down=w_down[0])
    names = list(big)
    col_sharded = {"w_in", "w_sb_out", "w_conv_out", "w_up"}
    where = jnp.stack([chip, lax.axis_index("c")]).astype(jnp.int32)
    cs = w_dw.shape[2]
    taps4 = _gather_taps(_pad_rows(w_dw[0], CONV_HALO))
    first, token = _gather_start([_cast_into_block(big["w_in"], where, taps4, "cast_w_in")], taps4, "gather_start_in")
    rest, rest_token = _gather_start([_cast_into_block(big[n], where, token, "cast_" + n) for n in names[1:]], token,
                            "gather_start_rest")
    in_flight = dict(zip(names, zip(*[a + b for a, b in zip(first, rest)])))
    gathered = {}

    def land(group, after, tag):
        thru, sends, recvs = zip(*[in_flight[n] for n in group])
        landed = _gather_wait(thru, sends, recvs, after, "gather_wait_" + tag)
        return _phase_start(landed, _forward_copies, 3 * len(group), where, "forward_start_" + tag)

    def ready(group, started, after, tag):
        slabs, sends, recvs, _ = started
        for n, g4 in zip(group, _phase_wait(slabs, sends, recvs, _forward_copies, after, "forward_wait_" + tag)):
            gathered[n] = g4 if n in col_sharded else g4.reshape(g4.shape[0] * g4.shape[1], g4.shape[2])

    taps = jnp.transpose(taps4, (1, 0, 2)).reshape(CONV_HALO, N_CHIPS * cs)
    group_mix = ["w_sb_out", "w_conv_out", "w_o"]

    h = _rms_fwd(xs, g_pre_mix)
    started = land(["w_in"], rest_token, "in")
    ready(["w_in"], started, h, "in")
    proj = _mm(h, gathered["w_in"], name="mm_proj", b_groups=N_CHIPS, bias=b_in, tn=768)
    mix_fwd = land(group_mix, proj, "mix")
    attn = _attn_fwd(proj, mix_fwd[3], n_heads, d_sb)
    u0 = _glu(proj, off_a, dc)
    u1, u3 = _conv_fwd(u0, taps, b_dw, g_conv_ln, b_conv_ln)
    ready(group_mix, mix_fwd, u3, "mix")
    o_sb = _mm(attn, gathered["w_sb_out"], name="mm_o_sb", b_groups=N_CHIPS)
    up_fwd = land(["w_up"], o_sb, "up")
    o_cv = _mm(u3, gathered["w_conv_out"], name="mm_o_cv", b_groups=N_CHIPS, after=up_fwd[3])
    merged = _merge_fwd(proj, o_sb, o_cv, off_sb, off_cv, gw)
    y = _mm(merged, gathered["w_o"], name="mm_y")
    x2, h2 = _resid_rms(xs, y, g_post_mix, g_pre_mlp)
    ready(["w_up"], up_fwd, h2, "up")
    down_fwd = land(["w_down"], h2, "down")
    up, f = _mm(h2, gathered["w_up"], name="mm_up", b_groups=N_CHIPS, out_dtypes=(F32, BF16), after=down_fwd[3],
                epilogue=lambda acc: (acc, jnp.square(jnp.maximum(acc, 0.0))))
    ready(["w_down"], down_fwd, f, "down")
    dn = _mm(f, gathered["w_down"], name="mm_down")
    dx3, d_dn, dg_post_mlp, loss_part = _final(x2, dn, g_post_mlp, tgt)

    moments = dict(w_in=(m_w_in, v_w_in), w_sb_out=(m_w_sb_out, v_w_sb_out), w_conv_out=(m_w_conv_out, v_w_conv_out),
                   w_o=(m_w_o, v_w_o), w_up=(m_w_up, v_w_up), w_down=(m_w_down, v_w_down))
    out_g, out_d, out_m, out_v = {}, {}, {}, {}
    grads = {}

    def blocks_of(group):
        out = []
        for n in group:
            g = grads[n]
            out.append(g if g.ndim == 3 else g.reshape(N_CHIPS, g.shape[0] // N_CHIPS, g.shape[1]))
        return out

    def exchange_begin(group, tag):
        blocks = blocks_of(group)
        lands = [lax.empty((b.shape[0], b.shape[1] // 2, b.shape[2]), b.dtype) for b in blocks]
        return _phase_start(blocks + lands, _exchange_copies, len(group), where, "exchange_start_" + tag)

    def scatter_begin(group, tag, exchange, after):
        arrays, sends, recvs, _ = exchange
        arrays = _phase_wait(arrays, sends, recvs, _exchange_copies, after, "exchange_wait_" + tag)
        n = len(group)
        pair_sums = [_add_pair(g, t, where, "pair_sum_" + m) for m, g, t in zip(group, arrays[:n], arrays[n:])]
        return _scatter_start(pair_sums, where, "scatter_start_" + tag)

    def reduce_sum(group, tag, started, after):
        parts, slots = _scatter_wait(*started, after, "scatter_wait_" + tag)
        halves = [_sum_slots(p, t, where, "chip_sum_" + n) for n, p, t in zip(group, parts, slots)]
        return _phase_start(halves, _share_copies, len(group), where, "share_start_" + tag)

    def reduce_update(group, tag, shared, after):
        halves, sends, recvs, _ = shared
        for n, red in zip(group, _phase_wait(halves, sends, recvs, _share_copies, after, "share_wait_" + tag)):
            mm_, vv_ = moments[n]
            res = _adamw(big[n], red, mm_[0], vv_[0], after, "adamw_" + n)
            out_g[n], out_d[n], out_m[n], out_v[n] = [r[None] for r in res]
            after = res[1]
        return after

    group_mlp, group_mix, group_in = ["w_down", "w_up"], ["w_o", "w_sb_out", "w_conv_out"], ["w_in"]
    grads["w_down"] = _mm(f, d_dn, name="mm_dw_down", ta=True, out_dtypes=(BF16,))
    dup = _mm(d_dn, gathered["w_down"], name="mm_df", tb=True, extra=[up], out_dtypes=(BF16,),
              epilogue=lambda acc, upv: (acc * (2.0 * jnp.maximum(upv, 0.0)),))
    grads["w_up"] = _mm(h2, dup, name="mm_dw_up", ta=True, out_groups=N_CHIPS, out_dtypes=(BF16,))
    mlp_exchange = exchange_begin(group_mlp, "mlp")
    dh2 = _mm(dup, gathered["w_up"], name="mm_dh2", tb=True, b_groups=N_CHIPS, after=mlp_exchange[3])
    dx2, dy, dg_pre_mlp, dg_post_mix = _rms_bwd2(dx3, dh2, x2, y, g_post_mix, g_pre_mlp)
    mlp_started, mlp_token = scatter_begin(group_mlp, "mlp", mlp_exchange, dx2)
    grads["w_o"] = _mm(merged, dy, name="mm_dw_o", ta=True, out_dtypes=(BF16,), after=mlp_token)
    dmerged = _mm(dy, gathered["w_o"], name="mm_dmerged", tb=True)
    do_sb, do_cv, dgate_sb, dgate_cv, s_gate_sb, s_gate_cv = _merge_bwd(proj, o_sb, o_cv, dmerged, off_sb, off_cv, gw)
    grads["w_sb_out"] = _mm(attn, do_sb, name="mm_dw_sb", ta=True, out_groups=N_CHIPS, out_dtypes=(BF16,))
    dattn = _mm(do_sb, gathered["w_sb_out"], name="mm_dattn", tb=True, b_groups=N_CHIPS)
    grads["w_conv_out"] = _mm(u3, do_cv, name="mm_dw_cv", ta=True, out_groups=N_CHIPS, out_dtypes=(BF16,))
    mix_exchange = exchange_begin(group_mix, "mix")
    du3 = _mm(do_cv, gathered["w_conv_out"], name="mm_du3", tb=True, b_groups=N_CHIPS, after=mix_exchange[3])
    mix_started, mix_token = scatter_begin(group_mix, "mix", mix_exchange, du3)
    dq, dk, dv, s_q, s_k, s_v = _attn_bwd(proj, dattn, mix_token, n_heads, d_sb)
    du1, dg_conv_ln, db_conv_ln, db_dw = _conv_bwd_norm(u1, du3, g_conv_ln, b_conv_ln)
    dglu_a, dglu_b, s_a, s_b, dtaps = _conv_bwd_taps(du1, u0, proj, taps, off_a)
    dproj = jnp.concatenate([dq, dk, dv, dglu_a, dglu_b, dgate_sb, dgate_cv], axis=1)
    db_in = jnp.concatenate([s_q, s_k, s_v, s_a, s_b, s_gate_sb, s_gate_cv], axis=1)
    small_w = dict(g_pre_mix=g_pre_mix, b_in=b_in, b_dw=b_dw, g_conv_ln=g_conv_ln, b_conv_ln=b_conv_ln,
                   g_post_mix=g_post_mix, g_pre_mlp=g_pre_mlp, g_post_mlp=g_post_mlp)
    small_m = dict(g_pre_mix=m_g_pre_mix, b_in=m_b_in, b_dw=m_b_dw, g_conv_ln=m_g_conv_ln, b_conv_ln=m_b_conv_ln,
                   g_post_mix=m_g_post_mix, g_pre_mlp=m_g_pre_mlp, g_post_mlp=m_g_post_mlp)
    small_v = dict(g_pre_mix=v_g_pre_mix, b_in=v_b_in, b_dw=v_b_dw, g_conv_ln=v_g_conv_ln, b_conv_ln=v_b_conv_ln,
                   g_post_mix=v_g_post_mix, g_pre_mlp=v_g_pre_mlp, g_post_mlp=v_g_post_mlp)
    small_g = dict(b_in=db_in, b_dw=db_dw, g_conv_ln=dg_conv_ln, b_conv_ln=db_conv_ln,
                   g_post_mix=dg_post_mix, g_pre_mlp=dg_pre_mlp, g_post_mlp=dg_post_mlp)

    def small_update(group, tails, after, tag):
        shapes = [small_w[n].shape for n in group] + [t.shape for t in tails]
        summed = _all_reduce_small(_pack([small_g[n] for n in group] + tails), after, "all_reduce_small_" + tag)
        zeros_tail = [jnp.zeros(t.shape, F32) for t in tails]
        res = _adamw(_pack([small_w[n] for n in group] + zeros_tail), summed,
                     _pack([small_m[n] for n in group] + zeros_tail), _pack([small_v[n] for n in group] + zeros_tail),
                     summed, "adamw_small_" + tag)
        unpacked = [_unpack(r, shapes) for r in res]
        for i, n in enumerate(group):
            out_g[n], out_d[n], out_m[n], out_v[n] = [u[i] for u in unpacked]
        return summed, unpacked[0][len(group):]

    early = ["b_in", "b_dw", "g_conv_ln", "b_conv_ln", "g_post_mix", "g_pre_mlp", "g_post_mlp"]
    summed, (loss_sum, taps_sum) = small_update(early, [loss_part, dtaps], dproj, "early")
    loss = loss_sum[0, 0]
    taps_grad = lax.dynamic_slice(taps_sum, (0, chip * cs), (CONV_HALO, cs))
    res = _adamw(_pad_rows(w_dw[0], CONV_HALO), taps_grad, _pad_rows(m_w_dw[0], CONV_HALO), _pad_rows(v_w_dw[0], CONV_HALO),
                 taps_grad, "adamw_taps")
    out_g["w_dw"], out_d["w_dw"], out_m["w_dw"], out_v["w_dw"] = [r[:CONV_WIDTH][None] for r in res]
    grads["w_in"] = _mm(h, dproj, name="mm_dw_in", ta=True, out_groups=N_CHIPS, out_dtypes=(BF16,), tn=768, after=summed)
    in_exchange = exchange_begin(group_in, "in")
    dh = _mm(dproj, gathered["w_in"], name="mm_dh", tb=True, b_groups=N_CHIPS, tk=2304, after=in_exchange[3])
    in_started, in_token = scatter_begin(group_in, "in", in_exchange, dh)
    grad_x, dg_pre_mix = _rms_bwd1(dx2, dh, xs, g_pre_mix)
    mlp_shared = reduce_sum(group_mlp, "mlp", mlp_started, in_token)
    mix_shared = reduce_sum(group_mix, "mix", mix_started, mlp_shared[3])
    done = reduce_update(group_mlp, "mlp", mlp_shared, mix_shared[3])
    done = reduce_update(group_mix, "mix", mix_shared, done)
    small_g["g_pre_mix"] = dg_pre_mix
    summed, _ = small_update(["g_pre_mix"], [], done, "late")
    in_shared = reduce_sum(group_in, "in", in_started, summed)
    reduce_update(group_in, "in", in_shared, in_shared[3])

    order = ["g_pre_mix", "w_in", "b_in", "w_dw", "b_dw", "g_conv_ln", "b_conv_ln", "w_sb_out", "w_conv_out", "w_o",
             "g_post_mix", "g_pre_mlp", "w_up", "w_down", "g_post_mlp"]
    return (loss, grad_x[None], *[out_g[n] for n in order], *[out_d[n] for n in order],
            *[out_m[n] for n in order], *[out_v[n] for n in order])
```

```python
import math

import jax
import jax.numpy as jnp
from jax import lax
from jax.experimental import pallas as pl
from jax.experimental.pallas import tpu as pltpu

F32 = jnp.float32
BF16 = jnp.bfloat16
MESH = pl.DeviceIdType.MESH

HEAD_DIM = 128
CONV_WIDTH = 31
CONV_HALO = 32
EPS = 1e-6
ADAM_LR = 0.001
ADAM_B1 = 0.9
ADAM_B2 = 0.999
ADAM_EPS = 1e-08
ADAM_WD = 0.01
ADAM_STEP = 10
N_CHIPS = 4
VMEM_LIMIT = 56 * 1024 * 1024


def _params(**kw):
    return pltpu.CompilerParams(vmem_limit_bytes=VMEM_LIMIT, **kw)


def _tile(n, want, align=128):
    if n <= want:
        return n
    for t in range(want - want % align, 0, -align):
        if n % t == 0:
            return t
    raise ValueError((n, want, align))


def _mm(a, b, *, name, ta=False, tb=False, b_groups=1, out_groups=1, bias=None, extra=None, epilogue=None,
        out_dtypes=(F32,), tm=1024, tn=1024, tk=2048, after=None):
    if ta:
        K, M = a.shape
    else:
        M, K = a.shape
    if b_groups == 1:
        br, bc = b.shape
    else:
        _, br, bcg = b.shape
        bc = bcg * b_groups
    if tb:
        N, Kb = br, bc
    else:
        Kb, N = br, bc
    assert Kb == K, (name, a.shape, b.shape)
    tm, tn, tk = _tile(M, tm), _tile(N, tn), _tile(K, tk)
    if b_groups > 1:
        if tb:
            tk = _tile(K // b_groups, tk)
        else:
            tn = _tile(N // b_groups, tn)
    if out_groups > 1:
        tn = _tile(N // out_groups, tn)
        if b_groups > 1 and not tb:
            tn = _tile(N // b_groups, tn)
    nm, nn, nk = M // tm, N // tn, K // tk
    extra = tuple(extra or ())
    n_extra = len(extra)
    has_bias = bias is not None
    n_out = len(out_dtypes)

    a_spec = pl.BlockSpec((tk, tm), lambda i, j, k: (k, i)) if ta else pl.BlockSpec((tm, tk), lambda i, j, k: (i, k))
    if b_groups == 1:
        b_spec = pl.BlockSpec((tn, tk), lambda i, j, k: (j, k)) if tb else pl.BlockSpec((tk, tn), lambda i, j, k: (k, j))
    elif tb:
        kpg = (K // b_groups) // tk
        b_spec = pl.BlockSpec((None, tn, tk), lambda i, j, k: (k // kpg, j, k % kpg))
    else:
        npg = (N // b_groups) // tn
        b_spec = pl.BlockSpec((None, tk, tn), lambda i, j, k: (j // npg, k, j % npg))
    in_specs = [a_spec, b_spec]
    operands = [a, b]
    if has_bias:
        in_specs.append(pl.BlockSpec((1, tn), lambda i, j, k: (0, j)))
        operands.append(bias)
    for e in extra:
        in_specs.append(pl.BlockSpec((tm, tn), lambda i, j, k: (i, j)))
        operands.append(e)
    n_after = 0 if after is None else 1
    if after is not None:
        in_specs.append(pl.BlockSpec(memory_space=pl.ANY))
        operands.append(after)
    if out_groups == 1:
        o_spec = pl.BlockSpec((tm, tn), lambda i, j, k: (i, j))
        o_shape = (M, N)
    else:
        opg = (N // out_groups) // tn
        o_spec = pl.BlockSpec((None, tm, tn), lambda i, j, k: (j // opg, i, j % opg))
        o_shape = (out_groups, M, N // out_groups)
    dims = (((0 if ta else 1,), (1 if tb else 0,)), ((), ()))

    def body(*refs):
        a_ref, b_ref = refs[0], refs[1]
        pos = 2
        bias_ref = None
        if has_bias:
            bias_ref = refs[pos]
            pos += 1
        extra_refs = refs[pos:pos + n_extra]
        pos += n_extra + n_after
        out_refs = refs[pos:pos + n_out]
        pos += n_out
        acc_ref = refs[pos] if nk > 1 else None

        part = lax.dot_general(a_ref[...].astype(BF16), b_ref[...].astype(BF16), dims, preferred_element_type=F32)

        def finish(acc):
            if has_bias:
                acc = acc + bias_ref[...]
            outs = epilogue(acc, *[e[...] for e in extra_refs]) if epilogue is not None else (acc,)
            for o_ref, o in zip(out_refs, outs):
                o_ref[...] = o.astype(o_ref.dtype)

        if nk == 1:
            finish(part)
        else:
            k = pl.program_id(2)

            @pl.when(k == 0)
            def _():
                acc_ref[...] = part

            @pl.when(k > 0)
            def _():
                acc_ref[...] += part

            @pl.when(k == nk - 1)
            def _():
                finish(acc_ref[...])

    outs = pl.pallas_call(
        body,
        name=name,
        grid=(nm, nn, nk),
        in_specs=in_specs,
        out_specs=[o_spec] * n_out,
        out_shape=[jax.ShapeDtypeStruct(o_shape, dt) for dt in out_dtypes],
        scratch_shapes=[pltpu.VMEM((tm, tn), F32)] if nk > 1 else [],
        compiler_params=_params(dimension_semantics=("parallel", "parallel", "arbitrary")),
    )(*operands)
    return outs[0] if n_out == 1 else outs


def _rms_stats(x):
    return lax.rsqrt(jnp.mean(x * x, axis=-1, keepdims=True) + EPS)


def _rms_bwd(x, r, g, dy):
    gy = dy * g
    return r * gy - x * (r * r * r) * jnp.mean(x * gy, axis=-1, keepdims=True)


def _sigmoid(x):
    return 1.0 / (1.0 + jnp.exp(-x))


def _row_call(body, *, name, rows, tr, ins, outs, acc_outs=()):
    n = rows // tr
    in_specs = []
    for arr, kind in ins:
        if kind == "row":
            in_specs.append(pl.BlockSpec((tr, arr.shape[1]), lambda i: (i, 0)))
        else:
            in_specs.append(pl.BlockSpec(arr.shape, lambda i: (0, 0)))
    out_specs = [pl.BlockSpec((tr, w), lambda i: (i, 0)) for w, _ in outs]
    out_shape = [jax.ShapeDtypeStruct((rows, w), dt) for w, dt in outs]
    out_specs += [pl.BlockSpec((1, w), lambda i: (0, 0)) for w in acc_outs]
    out_shape += [jax.ShapeDtypeStruct((1, w), F32) for w in acc_outs]
    return pl.pallas_call(
        body, name=name, grid=(n,), in_specs=in_specs, out_specs=out_specs, out_shape=out_shape,
        compiler_params=_params(dimension_semantics=("arbitrary",)),
    )(*[a for a, _ in ins])


def _accumulate(ref, val):
    @pl.when(pl.program_id(0) == 0)
    def _():
        ref[...] = val

    @pl.when(pl.program_id(0) > 0)
    def _():
        ref[...] += val


def _rms_fwd(x, g, tr=256):
    def body(x_ref, g_ref, h_ref):
        xv = x_ref[...]
        h_ref[...] = (xv * _rms_stats(xv) * g_ref[...]).astype(BF16)

    (h,) = _row_call(body, name="rms_fwd", rows=x.shape[0], tr=tr, ins=[(x, "row"), (g, "vec")], outs=[(x.shape[1], BF16)])
    return h


def _resid_rms(x, y, g2, g3, tr=256):
    def body(x_ref, y_ref, g2_ref, g3_ref, x2_ref, h2_ref):
        yv = y_ref[...]
        x2 = x_ref[...] + yv * _rms_stats(yv) * g2_ref[...]
        x2_ref[...] = x2
        h2_ref[...] = (x2 * _rms_stats(x2) * g3_ref[...]).astype(BF16)

    d = x.shape[1]
    return _row_call(body, name="resid_rms", rows=x.shape[0], tr=tr,
                     ins=[(x, "row"), (y, "row"), (g2, "vec"), (g3, "vec")], outs=[(d, F32), (d, BF16)])


def _final(x2, dn, g4, tgt, tr=256):
    d = x2.shape[1]

    def body(x2_ref, dn_ref, g4_ref, t_ref, dx3_ref, ddn_ref, dg4_ref, loss_ref):
        dn_v = dn_ref[...]
        r = _rms_stats(dn_v)
        g = g4_ref[...]
        e = x2_ref[...] + dn_v * r * g - t_ref[...]
        dx3 = e * (1.0 / d)
        dx3_ref[...] = dx3
        ddn_ref[...] = _rms_bwd(dn_v, r, g, dx3).astype(BF16)
        _accumulate(dg4_ref, jnp.sum(dx3 * dn_v * r, axis=0, keepdims=True))
        part = 0.5 * jnp.sum(jnp.mean(e * e, axis=-1, keepdims=True), axis=0, keepdims=True)
        _accumulate(loss_ref, jnp.broadcast_to(part, loss_ref.shape))

    return _row_call(body, name="final", rows=x2.shape[0], tr=tr,
                     ins=[(x2, "row"), (dn, "row"), (g4, "vec"), (tgt, "row")],
                     outs=[(d, F32), (d, BF16)], acc_outs=(d, 128))


def _rms_bwd2(dx3, dh2, x2, y, g2, g3, tr=256):
    d = x2.shape[1]

    def body(dx3_ref, dh2_ref, x2_ref, y_ref, g2_ref, g3_ref, dx2_ref, dy_ref, dg3_ref, dg2_ref):
        x2v, dh2v, yv = x2_ref[...], dh2_ref[...], y_ref[...]
        r3 = _rms_stats(x2v)
        dx2 = dx3_ref[...] + _rms_bwd(x2v, r3, g3_ref[...], dh2v)
        dx2_ref[...] = dx2
        _accumulate(dg3_ref, jnp.sum(dh2v * x2v * r3, axis=0, keepdims=True))
        r2 = _rms_stats(yv)
        dy_ref[...] = _rms_bwd(yv, r2, g2_ref[...], dx2).astype(BF16)
        _accumulate(dg2_ref, jnp.sum(dx2 * yv * r2, axis=0, keepdims=True))

    return _row_call(body, name="rms_bwd2", rows=x2.shape[0], tr=tr,
                     ins=[(dx3, "row"), (dh2, "row"), (x2, "row"), (y, "row"), (g2, "vec"), (g3, "vec")],
                     outs=[(d, F32), (d, BF16)], acc_outs=(d, d))


def _rms_bwd1(dx2, dh, x, g1, tr=256):
    d = x.shape[1]

    def body(dx2_ref, dh_ref, x_ref, g1_ref, gx_ref, dg1_ref):
        xv, dhv = x_ref[...], dh_ref[...]
        r = _rms_stats(xv)
        gx_ref[...] = dx2_ref[...] + _rms_bwd(xv, r, g1_ref[...], dhv)
        _accumulate(dg1_ref, jnp.sum(dhv * xv * r, axis=0, keepdims=True))

    return _row_call(body, name="rms_bwd1", rows=x.shape[0], tr=tr,
                     ins=[(dx2, "row"), (dh, "row"), (x, "row"), (g1, "vec")], outs=[(d, F32)], acc_outs=(d,))


def _merge_fwd(proj, o_sb, o_cv, off_sb, off_cv, gw, tr=256):
    s, d = o_sb.shape
    nj = d // gw
    b_sb, b_cv = off_sb // gw, off_cv // gw

    def body(gs_ref, gc_ref, osb_ref, ocv_ref, m_ref):
        m_ref[...] = (_sigmoid(gs_ref[...]) * osb_ref[...] + _sigmoid(gc_ref[...]) * ocv_ref[...]).astype(BF16)

    blk = lambda i, j: (i, j)
    return pl.pallas_call(
        body, name="merge_fwd", grid=(s // tr, nj),
        in_specs=[pl.BlockSpec((tr, gw), lambda i, j: (i, b_sb + j)), pl.BlockSpec((tr, gw), lambda i, j: (i, b_cv + j)),
                  pl.BlockSpec((tr, gw), blk), pl.BlockSpec((tr, gw), blk)],
        out_specs=pl.BlockSpec((tr, gw), blk), out_shape=jax.ShapeDtypeStruct((s, d), BF16),
        compiler_params=_params(dimension_semantics=("parallel", "parallel")),
    )(proj, proj, o_sb, o_cv)


def _merge_bwd(proj, o_sb, o_cv, dmerged, off_sb, off_cv, gw, tr=256):
    s, d = o_sb.shape
    nj = d // gw
    ni = s // tr
    b_sb, b_cv = off_sb // gw, off_cv // gw

    def body(gs_ref, gc_ref, osb_ref, ocv_ref, dm_ref, dosb_ref, docv_ref, dgs_ref, dgc_ref, sgs_ref, sgc_ref):
        dm = dm_ref[...]
        s_sb, s_cv = _sigmoid(gs_ref[...]), _sigmoid(gc_ref[...])
        dosb_ref[...] = (dm * s_sb).astype(BF16)
        docv_ref[...] = (dm * s_cv).astype(BF16)
        dgs = dm * osb_ref[...] * s_sb * (1.0 - s_sb)
        dgc = dm * ocv_ref[...] * s_cv * (1.0 - s_cv)
        dgs_ref[...] = dgs.astype(BF16)
        dgc_ref[...] = dgc.astype(BF16)
        i = pl.program_id(1)
        for ref, val in ((sgs_ref, dgs), (sgc_ref, dgc)):
            col = jnp.sum(val, axis=0, keepdims=True)

            @pl.when(i == 0)
            def _():
                ref[...] = col

            @pl.when(i > 0)
            def _():
                ref[...] += col

    blk = lambda j, i: (i, j)
    outs = pl.pallas_call(
        body, name="merge_bwd", grid=(nj, ni),
        in_specs=[pl.BlockSpec((tr, gw), lambda j, i: (i, b_sb + j)), pl.BlockSpec((tr, gw), lambda j, i: (i, b_cv + j)),
                  pl.BlockSpec((tr, gw), blk), pl.BlockSpec((tr, gw), blk), pl.BlockSpec((tr, gw), blk)],
        out_specs=[pl.BlockSpec((tr, gw), blk), pl.BlockSpec((tr, gw), blk),
                   pl.BlockSpec((tr, gw), blk), pl.BlockSpec((tr, gw), blk),
                   pl.BlockSpec((1, gw), lambda j, i: (0, j)), pl.BlockSpec((1, gw), lambda j, i: (0, j))],
        out_shape=[jax.ShapeDtypeStruct((s, d), BF16)] * 4 + [jax.ShapeDtypeStruct((1, d), F32)] * 2,
        compiler_params=_params(dimension_semantics=("parallel", "arbitrary")),
    )(proj, proj, o_sb, o_cv, dmerged)
    return outs


def _split_bf16(v):
    hi = v.astype(BF16)
    lo = (v - hi.astype(F32)).astype(BF16)
    return hi, lo


def _dot_nt(a, b):
    return lax.dot_general(a, b, (((1,), (1,)), ((), ())), preferred_element_type=F32)


def _dot_tn(a, b):
    return lax.dot_general(a, b, (((0,), (0,)), ((), ())), preferred_element_type=F32)


def _dot_nn(a, b):
    return lax.dot_general(a, b, (((1,), (0,)), ((), ())), preferred_element_type=F32)


def _sum_right(v, tri):
    hi, lo = _split_bf16(v)
    return _dot_nn(hi, tri) + _dot_nn(lo, tri)


HEADS_PER_STEP = 4


LOG2_E = 1.4426950408889634


def _sb_tile(q, kb, scale, mask):
    z = _dot_nt(q, kb) * (scale * LOG2_E)
    log_b = jnp.minimum(z, 0.0) - jnp.log2(1.0 + jnp.exp2(-jnp.abs(z)))
    log_1m = log_b - z
    if mask is not None:
        log_1m = jnp.where(mask, log_1m, 0.0)
    return log_b, log_1m


def _tile_mask(tq):
    row = lax.broadcasted_iota(jnp.int32, (tq, tq), 0)
    col = lax.broadcasted_iota(jnp.int32, (tq, tq), 1)
    return col < row


def _tri(tq, before):
    r = lax.broadcasted_iota(jnp.int32, (tq, tq), 0)
    c = lax.broadcasted_iota(jnp.int32, (tq, tq), 1)
    return jnp.where((r < c) if before else (r > c), 1.0, 0.0).astype(BF16)


def _attn_fwd(proj, after, n_heads, d_sb, tq=256):
    s = proj.shape[0]
    tq = _tile(s, tq)
    hb = d_sb // HEAD_DIM
    scale = 1.0 / math.sqrt(HEAD_DIM)

    per_step = min(HEADS_PER_STEP, n_heads)
    heads = [pl.ds(n * HEAD_DIM, HEAD_DIM) for n in range(per_step)]
    wide = per_step * HEAD_DIM
    hb //= per_step

    def body(q_ref, k_ref, v_ref, _, o_ref):
        i = pl.program_id(1)
        tri = _tri(tq, False)

        def tile(j, carry, diagonal):
            rows = pl.ds(pl.multiple_of(j * tq, tq), tq)
            mask = _tile_mask(tq) if diagonal else None
            out = []
            for hd, (c_l, acc) in zip(heads, carry):
                log_b, log_1m = _sb_tile(q_ref[:, hd].astype(BF16), k_ref[rows, hd].astype(BF16), scale, mask)
                a = jnp.exp2(log_b + (c_l + _sum_right(log_1m, tri)))
                if diagonal:
                    a = jnp.where(mask, a, 0.0)
                out.append((c_l + jnp.sum(log_1m, axis=1, keepdims=True),
                            acc + _dot_nn(a.astype(BF16), v_ref[rows, hd].astype(BF16))))
            return tuple(out)

        init = tuple((jnp.zeros((tq, 1), F32), jnp.zeros((tq, HEAD_DIM), F32)) for _ in heads)
        carry = lax.fori_loop(0, i, lambda jj, c: tile(i - 1 - jj, c, False), tile(i, init, True))
        for hd, (_, acc) in zip(heads, carry):
            o_ref[:, hd] = acc.astype(BF16)

    return pl.pallas_call(
        body, name="attn_fwd", grid=(n_heads // per_step, s // tq),
        in_specs=[pl.BlockSpec((tq, wide), lambda h, i: (i, h)),
                  pl.BlockSpec((s, wide), lambda h, i: (0, hb + h)),
                  pl.BlockSpec((s, wide), lambda h, i: (0, 2 * hb + h)), pl.BlockSpec(memory_space=pl.ANY)],
        out_specs=pl.BlockSpec((tq, wide), lambda h, i: (i, h)),
        out_shape=jax.ShapeDtypeStruct((s, d_sb), BF16),
        compiler_params=_params(dimension_semantics=("parallel", "arbitrary")),
    )(proj, proj, proj, after)


def _attn_bwd(proj, dattn, after, n_heads, d_sb, tq=256):
    s = proj.shape[0]
    tq = _tile(s, tq)
    nq = s // tq
    hb = d_sb // HEAD_DIM // HEADS_PER_STEP
    scale = 1.0 / math.sqrt(HEAD_DIM)
    heads = [pl.ds(n * HEAD_DIM, HEAD_DIM) for n in range(HEADS_PER_STEP)]
    wide = HEADS_PER_STEP * HEAD_DIM

    def body(q_ref, k_ref, v_ref, do_ref, _, dq_ref, dk_ref, dv_ref, sq_ref, sk_ref, sv_ref, dk_acc, dv_acc, g_st, b_st):
        i = pl.program_id(1)

        @pl.when(i == 0)
        def _():
            dk_acc[...] = jnp.zeros_like(dk_acc)
            dv_acc[...] = jnp.zeros_like(dv_acc)

        tri_after = _tri(tq, False)
        tri_before = _tri(tq, True)

        def newest_first(j, c_ls, diagonal):
            rows = pl.ds(pl.multiple_of(j * tq, tq), tq)
            mask = _tile_mask(tq) if diagonal else None
            out = []
            for n, (hd, c_l) in enumerate(zip(heads, c_ls)):
                do = do_ref[:, hd].astype(BF16)
                log_b, log_1m = _sb_tile(q_ref[:, hd].astype(BF16), k_ref[rows, hd].astype(BF16), scale, mask)
                a = jnp.exp2(log_b + (c_l + _sum_right(log_1m, tri_after)))
                beta = jnp.exp2(log_b)
                if diagonal:
                    a = jnp.where(mask, a, 0.0)
                    beta = jnp.where(mask, beta, 0.0)
                g_st[n, j] = a * _dot_nt(do, v_ref[rows, hd].astype(BF16))
                b_st[n, j] = beta
                dv_acc[rows, hd] += _dot_tn(a.astype(BF16), do)
                out.append(c_l + jnp.sum(log_1m, axis=1, keepdims=True))
            return tuple(out)

        lax.fori_loop(0, i, lambda jj, c: newest_first(i - 1 - jj, c, False),
                      newest_first(i, tuple(jnp.zeros((tq, 1), F32) for _ in heads), True))

        def oldest_first(j, carry):
            rows = pl.ds(pl.multiple_of(j * tq, tq), tq)
            out = []
            for n, (hd, (c_g, dq)) in enumerate(zip(heads, carry)):
                kb = k_ref[rows, hd].astype(BF16)
                g = g_st[n, j]
                beta = b_st[n, j]
                g_before = c_g + _sum_right(g, tri_before)
                dz16 = ((g * (1.0 - beta) - g_before * beta) * scale).astype(BF16)
                dk_acc[rows, hd] += _dot_tn(dz16, q_ref[:, hd].astype(BF16))
                out.append((c_g + jnp.sum(g, axis=1, keepdims=True), dq + _dot_nn(dz16, kb)))
            return tuple(out)

        init = tuple((jnp.zeros((tq, 1), F32), jnp.zeros((tq, HEAD_DIM), F32)) for _ in heads)
        carry = lax.fori_loop(0, i + 1, oldest_first, init)
        dq = jnp.concatenate([c[1] for c in carry], axis=1)
        dq_ref[...] = dq.astype(BF16)
        col = jnp.sum(dq, axis=0, keepdims=True)

        @pl.when(i == 0)
        def _():
            sq_ref[...] = col

        @pl.when(i > 0)
        def _():
            sq_ref[...] += col

        @pl.when(i == nq - 1)
        def _():
            dk = dk_acc[...]
            dv = dv_acc[...]
            dk_ref[...] = dk.astype(BF16)
            dv_ref[...] = dv.astype(BF16)
            sk_ref[...] = jnp.sum(dk, axis=0, keepdims=True)
            sv_ref[...] = jnp.sum(dv, axis=0, keepdims=True)

    qblk = pl.BlockSpec((tq, wide), lambda h, i: (i, h))
    kblk = pl.BlockSpec((s, wide), lambda h, i: (0, h))
    col_sum = pl.BlockSpec((1, wide), lambda h, i: (0, h))
    stash = pltpu.VMEM((HEADS_PER_STEP, nq, tq, tq), F32)
    outs = pl.pallas_call(
        body, name="attn_bwd", grid=(n_heads // HEADS_PER_STEP, nq),
        in_specs=[qblk,
                  pl.BlockSpec((s, wide), lambda h, i: (0, hb + h)),
                  pl.BlockSpec((s, wide), lambda h, i: (0, 2 * hb + h)),
                  qblk, pl.BlockSpec(memory_space=pl.ANY)],
        out_specs=[qblk, kblk, kblk, col_sum, col_sum, col_sum],
        out_shape=[jax.ShapeDtypeStruct((s, d_sb), BF16)] * 3 + [jax.ShapeDtypeStruct((1, d_sb), F32)] * 3,
        scratch_shapes=[pltpu.VMEM((s, wide), F32), pltpu.VMEM((s, wide), F32), stash, stash],
        compiler_params=_params(dimension_semantics=("parallel", "arbitrary")),
    )(proj, proj, proj, dattn, after)
    return outs


LANES = 128


def _glu(proj, off_a, dc, tr=256):
    s = proj.shape[0]
    ba = off_a // dc

    def body(a_ref, b_ref, u_ref):
        u_ref[...] = a_ref[...] * _sigmoid(b_ref[...])

    return pl.pallas_call(
        body, name="glu", grid=(s // tr,),
        in_specs=[pl.BlockSpec((tr, dc), lambda i: (i, ba)), pl.BlockSpec((tr, dc), lambda i: (i, ba + 1))],
        out_specs=pl.BlockSpec((tr, dc), lambda i: (i, 0)), out_shape=jax.ShapeDtypeStruct((s, dc), F32),
        compiler_params=_params(dimension_semantics=("parallel",)),
    )(proj, proj)


SUBLANES = 8


def _shifted_rows(src, cols, shifted, tt):
    span = tt + CONV_HALO - SUBLANES
    for b in range(1, SUBLANES):
        shifted[b, pl.ds(0, span), :] = src[pl.ds(b, span), cols]

    def rows(o):
        a, b = divmod(o, SUBLANES)
        return src[pl.ds(SUBLANES * a, tt), cols] if b == 0 else shifted[b, pl.ds(SUBLANES * a, tt), :]

    return rows


def _conv_fwd(u0, taps, b_dw, g_ln, b_ln, tt=256):
    s, dc = u0.shape
    tt = _tile(s, tt)
    hpb = tt // CONV_HALO

    def body(cur_ref, halo_ref, w_ref, bdw_ref, g_ref, b_ref, u1_ref, u3_ref, xs, xs_shifted):
        i = pl.program_id(0)
        xs[pl.ds(0, CONV_HALO), :] = jnp.where(i > 0, halo_ref[...], 0.0)
        xs[pl.ds(CONV_HALO, tt), :] = cur_ref[...]
        for c0 in range(0, dc, LANES):
            cols = pl.ds(c0, LANES)
            x_rows = _shifted_rows(xs, cols, xs_shifted, tt)
            acc = jnp.broadcast_to(bdw_ref[:, cols], (tt, LANES))
            for k in range(CONV_WIDTH):
                acc = acc + w_ref[pl.ds(k, 1), cols] * x_rows(CONV_HALO - (CONV_WIDTH - 1) + k)
            u1_ref[:, cols] = acc
        u1 = u1_ref[...]
        mu = jnp.mean(u1, axis=-1, keepdims=True)
        xc = u1 - mu
        rstd = lax.rsqrt(jnp.mean(xc * xc, axis=-1, keepdims=True) + EPS)
        u2 = xc * rstd * g_ref[...] + b_ref[...]
        u3_ref[...] = (u2 * _sigmoid(u2)).astype(BF16)

    vec = pl.BlockSpec((1, dc), lambda i: (0, 0))
    return pl.pallas_call(
        body, name="conv_fwd", grid=(s // tt,),
        in_specs=[pl.BlockSpec((tt, dc), lambda i: (i, 0)),
                  pl.BlockSpec((CONV_HALO, dc), lambda i: (jnp.maximum(i * hpb - 1, 0), 0)),
                  pl.BlockSpec((CONV_HALO, dc), lambda i: (0, 0)), vec, vec, vec],
        out_specs=[pl.BlockSpec((tt, dc), lambda i: (i, 0)), pl.BlockSpec((tt, dc), lambda i: (i, 0))],
        out_shape=[jax.ShapeDtypeStruct((s, dc), F32), jax.ShapeDtypeStruct((s, dc), BF16)],
        scratch_shapes=[pltpu.VMEM((tt + CONV_HALO, dc), F32), pltpu.VMEM((SUBLANES, tt + CONV_HALO, LANES), F32)],
        compiler_params=_params(dimension_semantics=("arbitrary",)),
    )(u0, u0, taps, b_dw, g_ln, b_ln)


def _conv_bwd_norm(u1, du3, g_ln, b_ln, tr=256):
    dc = u1.shape[1]

    def body(u1_ref, du3_ref, g_ref, b_ref, du1_ref, dg_ref, db_ref, dbdw_ref):
        u1v = u1_ref[...]
        g = g_ref[...]
        mu = jnp.mean(u1v, axis=-1, keepdims=True)
        xc = u1v - mu
        rstd = lax.rsqrt(jnp.mean(xc * xc, axis=-1, keepdims=True) + EPS)
        xhat = xc * rstd
        u2 = xhat * g + b_ref[...]
        sg = _sigmoid(u2)
        du2 = du3_ref[...] * (sg * (1.0 + u2 * (1.0 - sg)))
        _accumulate(dg_ref, jnp.sum(du2 * xhat, axis=0, keepdims=True))
        _accumulate(db_ref, jnp.sum(du2, axis=0, keepdims=True))
        gy = du2 * g
        du1 = rstd * (gy - jnp.mean(gy, axis=-1, keepdims=True) - xhat * jnp.mean(gy * xhat, axis=-1, keepdims=True))
        du1_ref[...] = du1
        _accumulate(dbdw_ref, jnp.sum(du1, axis=0, keepdims=True))

    return _row_call(body, name="conv_bwd_norm", rows=u1.shape[0], tr=tr,
                     ins=[(u1, "row"), (du3, "row"), (g_ln, "vec"), (b_ln, "vec")],
                     outs=[(dc, F32)], acc_outs=(dc, dc, dc))


def _conv_bwd_taps(du1, u0, proj, taps, off_a, tt=256):
    s, dc = u0.shape
    tt = _tile(s, tt)
    n = s // tt
    hpb = tt // CONV_HALO
    ba = off_a // dc
    lead = CONV_HALO - (CONV_WIDTH - 1)

    def body(d_ref, dnext_ref, cur_ref, halo_ref, a_ref, b_ref, w_ref, da_ref, db_ref, sa_ref, sb_ref, dw_ref,
             xs, ds, du0, xs_shifted, ds_shifted):
        i = pl.program_id(0)
        xs[pl.ds(0, CONV_HALO), :] = jnp.where(i > 0, halo_ref[...], 0.0)
        xs[pl.ds(CONV_HALO, tt), :] = cur_ref[...]
        ds[pl.ds(0, tt), :] = d_ref[...]
        ds[pl.ds(tt, CONV_HALO), :] = jnp.where(i < n - 1, dnext_ref[...], 0.0)

        @pl.when(i == 0)
        def _():
            dw_ref[...] = jnp.zeros_like(dw_ref)

        for c0 in range(0, dc, LANES):
            cols = pl.ds(c0, LANES)
            x_rows = _shifted_rows(xs, cols, xs_shifted, tt)
            d_rows = _shifted_rows(ds, cols, ds_shifted, tt)
            d_cur = d_rows(0)
            acc = jnp.zeros((tt, LANES), F32)
            for k in range(CONV_WIDTH):
                acc = acc + w_ref[pl.ds(k, 1), cols] * d_rows(CONV_WIDTH - 1 - k)
                dw_ref[pl.ds(k, 1), cols] += jnp.sum(d_cur * x_rows(lead + k), axis=0, keepdims=True)
            du0[:, cols] = acc
        d0 = du0[...]
        sg = _sigmoid(b_ref[...])
        da = d0 * sg
        db = d0 * a_ref[...] * sg * (1.0 - sg)
        da_ref[...] = da.astype(BF16)
        db_ref[...] = db.astype(BF16)
        _accumulate(sa_ref, jnp.sum(da, axis=0, keepdims=True))
        _accumulate(sb_ref, jnp.sum(db, axis=0, keepdims=True))

    row = pl.BlockSpec((tt, dc), lambda i: (i, 0))
    vec = pl.BlockSpec((1, dc), lambda i: (0, 0))
    taps_spec = pl.BlockSpec((CONV_HALO, dc), lambda i: (0, 0))
    return pl.pallas_call(
        body, name="conv_bwd_taps", grid=(n,),
        in_specs=[row, pl.BlockSpec((CONV_HALO, dc), lambda i: (jnp.minimum((i + 1) * hpb, n * hpb - 1), 0)),
                  row, pl.BlockSpec((CONV_HALO, dc), lambda i: (jnp.maximum(i * hpb - 1, 0), 0)),
                  pl.BlockSpec((tt, dc), lambda i: (i, ba)), pl.BlockSpec((tt, dc), lambda i: (i, ba + 1)), taps_spec],
        out_specs=[row, row, vec, vec, taps_spec],
        out_shape=[jax.ShapeDtypeStruct((s, dc), BF16), jax.ShapeDtypeStruct((s, dc), BF16),
                   jax.ShapeDtypeStruct((1, dc), F32), jax.ShapeDtypeStruct((1, dc), F32),
                   jax.ShapeDtypeStruct((CONV_HALO, dc), F32)],
        scratch_shapes=[pltpu.VMEM((tt + CONV_HALO, dc), F32), pltpu.VMEM((tt + CONV_HALO, dc), F32),
                        pltpu.VMEM((tt, dc), F32),
                        pltpu.VMEM((SUBLANES, tt + CONV_HALO, LANES), F32), pltpu.VMEM((SUBLANES, tt + CONV_HALO, LANES), F32)],
        compiler_params=_params(dimension_semantics=("arbitrary",)),
    )(du1, du1, u0, u0, proj, proj, taps)


def _elementwise(body, *, name, ins, out_dtypes, after, tr=128):
    rows, cols = ins[0].shape
    tr = _tile(rows, tr, 8)
    spec = pl.BlockSpec((tr, cols), lambda i: (i, 0))
    n_in = len(ins)

    def with_after(*refs):
        body(*refs[:n_in], *refs[n_in + 1:])

    return pl.pallas_call(
        with_after, name=name, grid=(rows // tr,), in_specs=[spec] * n_in + [pl.BlockSpec(memory_space=pl.ANY)],
        out_specs=[spec] * len(out_dtypes),
        out_shape=[jax.ShapeDtypeStruct((rows, cols), dt) for dt in out_dtypes],
        compiler_params=_params(dimension_semantics=("parallel",)),
    )(*ins, after)


def _prefetched(body, *, name, where, grid, in_specs, out_specs, out_shape, operands):
    return pl.pallas_call(
        body, name=name, out_shape=out_shape,
        grid_spec=pltpu.PrefetchScalarGridSpec(num_scalar_prefetch=1, grid=grid, in_specs=in_specs, out_specs=out_specs),
        compiler_params=_params(dimension_semantics=("parallel",) * len(grid)),
    )(where, *operands)


def _cast_into_block(w, where, after, name, tr=256):
    rows, cols = w.shape
    tr = _tile(rows, tr, 16)

    def body(where_ref, w_ref, _, o_ref):
        o_ref[...] = w_ref[...].astype(BF16)

    return _prefetched(
        body, name=name, where=where, grid=(rows // tr,),
        in_specs=[pl.BlockSpec((tr, cols), lambda i, wh: (i, 0)), pl.BlockSpec(memory_space=pl.ANY)],
        out_specs=pl.BlockSpec((None, tr, cols), lambda i, wh: (wh[0], i, 0)),
        out_shape=jax.ShapeDtypeStruct((N_CHIPS, rows, cols), BF16), operands=[w, after])


def _add_pair(grad, theirs, where, name, tr=512):
    nb, half, cols = theirs.shape
    tr = _tile(half, tr, 16)
    nh = half // tr

    def body(where_ref, a_ref, b_ref, o_ref):
        o_ref[...] = a_ref[...] + b_ref[...]

    blk = pl.BlockSpec((None, tr, cols), lambda b, i, wh: (b, i, 0))
    return _prefetched(
        body, name=name, where=where, grid=(nb, nh),
        in_specs=[pl.BlockSpec((None, tr, cols), lambda b, i, wh: (b, wh[1] * nh + i, 0)), blk],
        out_specs=blk, out_shape=jax.ShapeDtypeStruct(theirs.shape, BF16), operands=[grad, theirs])


def _sum_slots(part, slots, where, name, tr=512):
    _, half, cols = part.shape
    tr = _tile(half, tr, 16)
    nh = half // tr

    def body(where_ref, p_ref, s_ref, o_ref):
        o_ref[...] = ((p_ref[...].astype(F32) + s_ref[0].astype(F32)) + s_ref[1].astype(F32)) + s_ref[2].astype(F32)

    return _prefetched(
        body, name=name, where=where, grid=(nh,),
        in_specs=[pl.BlockSpec((None, tr, cols), lambda i, wh: (wh[0], i, 0)),
                  pl.BlockSpec((3, tr, cols), lambda i, wh: (0, i, 0))],
        out_specs=pl.BlockSpec((tr, cols), lambda i, wh: (wh[1] * nh + i, 0)),
        out_shape=jax.ShapeDtypeStruct((2 * half, cols), F32), operands=[part, slots])


def _adamw(w, g, m, v, after, name):
    bc1 = 1.0 - ADAM_B1 ** ADAM_STEP
    bc2 = 1.0 - ADAM_B2 ** ADAM_STEP

    def body(w_ref, g_ref, m_ref, v_ref, go_ref, d_ref, mo_ref, vo_ref):
        g = g_ref[...]
        m_new = ADAM_B1 * m_ref[...] + (1.0 - ADAM_B1) * g
        v_new = ADAM_B2 * v_ref[...] + (1.0 - ADAM_B2) * (g * g)
        m_hat = m_new / bc1
        v_hat = v_new / bc2
        go_ref[...] = g
        d_ref[...] = -ADAM_LR * (m_hat / (jnp.sqrt(v_hat) + ADAM_EPS) + ADAM_WD * w_ref[...])
        mo_ref[...] = m_new
        vo_ref[...] = v_new

    return _elementwise(body, name=name, ins=[w, g, m, v], out_dtypes=[F32] * 4, after=after, tr=256)


ANY = pl.BlockSpec(memory_space=pl.ANY)


def _mesh_pos():
    return lax.axis_index("x"), lax.axis_index("y"), lax.axis_index("c")


def _other_chips(x, y):
    return [(1 - x, y), (x, 1 - y), (1 - x, 1 - y)]


def _remote(src, dst, send_sems, recv_sems, idx, to):
    return pltpu.make_async_remote_copy(src_ref=src, dst_ref=dst, send_sem=send_sems.at[idx], recv_sem=recv_sems.at[idx],
                                        device_id=to, device_id_type=MESH)


HBM = pl.BlockSpec(memory_space=pltpu.HBM)
SEM = pl.BlockSpec(memory_space=pltpu.SEMAPHORE)
EFFECT = pltpu.SideEffectType.DATAFLOW_SIDE_EFFECTING


def _in_hbm(arrays):
    return [pltpu.with_memory_space_constraint(a, pltpu.HBM) for a in arrays]


def _ici_copies(slab, send_sems, recv_sems, x, y, c):
    half = slab.shape[1] // 2
    rows = pl.ds(c * half, half)
    mine = slab.at[2 * x + y, rows]
    out = []
    for j, (cx, cy) in enumerate(_other_chips(x, y)):
        got = slab.at[2 * cx + cy, rows]
        out.append((_remote(mine, mine, send_sems, recv_sems, j, (cx, cy, c)),
                    _remote(got, got, send_sems, recv_sems, j, (cx, cy, c))))
    return out


def _gather_start(slabs, after, name):
    n = len(slabs)

    def body(*refs):
        ins = refs[:n]
        send_sems, recv_sems, token = refs[2 * n + 1:3 * n + 1], refs[3 * n + 1:4 * n + 1], refs[4 * n + 1]
        x, y, c = _mesh_pos()
        for w in range(n):
            for send, _ in _ici_copies(ins[w], send_sems[w], recv_sems[w], x, y, c):
                send.start()
        token[...] = jnp.zeros_like(token)

    outs = pl.pallas_call(
        body, name=name, in_specs=[HBM] * n + [ANY],
        out_specs=[HBM] * n + [SEM] * (2 * n) + [pl.BlockSpec(memory_space=pltpu.VMEM)],
        out_shape=[pltpu.HBM(s.shape, s.dtype) for s in slabs] + [pltpu.SemaphoreType.DMA((3,))] * (2 * n)
        + [jax.ShapeDtypeStruct((8, 128), F32)],
        input_output_aliases={w: w for w in range(n)},
        compiler_params=_params(has_side_effects=EFFECT),
    )(*_in_hbm(slabs), after)
    return (outs[:n], outs[n:2 * n], outs[2 * n:3 * n]), outs[3 * n]


def _gather_wait(slabs, send_sems, recv_sems, after, name):
    n = len(slabs)

    def body(*refs):
        ins = refs[:n]
        sends, recvs = refs[n:2 * n], refs[2 * n:3 * n]
        x, y, c = _mesh_pos()
        for w in range(n):
            for send, recv in _ici_copies(ins[w], sends[w], recvs[w], x, y, c):
                send.wait_send()
                recv.wait_recv()

    return pl.pallas_call(
        body, name=name, in_specs=[HBM] * n + [SEM] * (2 * n) + [ANY], out_specs=[HBM] * n,
        out_shape=[pltpu.HBM(s.shape, s.dtype) for s in slabs],
        input_output_aliases={w: w for w in range(n)},
        compiler_params=_params(has_side_effects=EFFECT),
    )(*slabs, *send_sems, *recv_sems, after)


def _phase_start(arrays, copies, n_copies, after, name):
    n = len(arrays)

    def body(*refs):
        send_sems, recv_sems, token = refs[2 * n + 1], refs[2 * n + 2], refs[2 * n + 3]
        for send, _ in copies(refs[:n], send_sems, recv_sems, *_mesh_pos()):
            send.start()
        token[...] = jnp.zeros_like(token)

    outs = pl.pallas_call(
        body, name=name, in_specs=[HBM] * n + [ANY],
        out_specs=[HBM] * n + [SEM, SEM, pl.BlockSpec(memory_space=pltpu.VMEM)],
        out_shape=[pltpu.HBM(a.shape, a.dtype) for a in arrays]
        + [pltpu.SemaphoreType.DMA((n_copies,))] * 2 + [jax.ShapeDtypeStruct((8, 128), F32)],
        input_output_aliases={k: k for k in range(n)},
        compiler_params=_params(has_side_effects=EFFECT),
    )(*_in_hbm(arrays), after)
    return outs[:n], outs[n], outs[n + 1], outs[n + 2]


def _phase_wait(arrays, send_sems, recv_sems, copies, after, name):
    n = len(arrays)

    def body(*refs):
        for send, recv in copies(refs[:n], refs[n], refs[n + 1], *_mesh_pos()):
            send.wait_send()
            recv.wait_recv()

    return pl.pallas_call(
        body, name=name, in_specs=[HBM] * n + [SEM, SEM, ANY], out_specs=[HBM] * n,
        out_shape=[pltpu.HBM(a.shape, a.dtype) for a in arrays],
        input_output_aliases={k: k for k in range(n)},
        compiler_params=_params(has_side_effects=EFFECT),
    )(*arrays, send_sems, recv_sems, after)


def _forward_copies(slabs, send_sems, recv_sems, x, y, c):
    sibling = (x, y, 1 - c)
    out = []
    for w, slab in enumerate(slabs):
        half = slab.shape[1] // 2
        for j, (cx, cy) in enumerate(_other_chips(x, y)):
            mine = slab.at[2 * cx + cy, pl.ds(c * half, half)]
            theirs = slab.at[2 * cx + cy, pl.ds((1 - c) * half, half)]
            out.append((_remote(mine, mine, send_sems, recv_sems, 3 * w + j, sibling),
                        _remote(theirs, theirs, send_sems, recv_sems, 3 * w + j, sibling)))
    return out


def _share_copies(halves, send_sems, recv_sems, x, y, c):
    sibling = (x, y, 1 - c)
    out = []
    for w, full in enumerate(halves):
        h = full.shape[0] // 2
        mine, theirs = full.at[pl.ds(c * h, h)], full.at[pl.ds((1 - c) * h, h)]
        out.append((_remote(mine, mine, send_sems, recv_sems, w, sibling),
                    _remote(theirs, theirs, send_sems, recv_sems, w, sibling)))
    return out


def _exchange_copies(arrays, send_sems, recv_sems, x, y, c):
    n = len(arrays) // 2
    sibling = (x, y, 1 - c)
    out = []
    for w in range(n):
        grad, land = arrays[w], arrays[n + w]
        half = grad.shape[1] // 2
        out.append((_remote(grad.at[:, pl.ds((1 - c) * half, half), :], land, send_sems, recv_sems, w, sibling),
                    _remote(land, land, send_sems, recv_sems, w, sibling)))
    return out


def _gather_taps(taps):
    def body(in_ref, out_ref, send_sems, recv_sems, local_sem):
        x, y, c = _mesh_pos()
        me = 2 * x + y
        chips = _other_chips(x, y)
        lc = pltpu.make_async_copy(in_ref, out_ref.at[me], local_sem)
        lc.start()
        sends = [_remote(in_ref, out_ref.at[me], send_sems, recv_sems, j, (cx, cy, c)) for j, (cx, cy) in enumerate(chips)]
        for cp in sends:
            cp.start()
        for j, (cx, cy) in enumerate(chips):
            got = out_ref.at[2 * cx + cy]
            _remote(got, got, send_sems, recv_sems, j, (cx, cy, c)).wait_recv()
        for cp in sends:
            cp.wait_send()
        lc.wait()

    vm = pl.BlockSpec(memory_space=pltpu.VMEM)
    return pl.pallas_call(
        body, name="gather_taps", in_specs=[vm], out_specs=vm,
        out_shape=jax.ShapeDtypeStruct((N_CHIPS,) + taps.shape, taps.dtype),
        scratch_shapes=[pltpu.SemaphoreType.DMA((3,)), pltpu.SemaphoreType.DMA((3,)), pltpu.SemaphoreType.DMA],
        compiler_params=_params(has_side_effects=True),
    )(taps)


def _scatter_copies(part, slots, send_sems, recv_sems, x, y, c):
    out = []
    for j, (cx, cy) in enumerate(_other_chips(x, y)):
        out.append((_remote(part.at[2 * cx + cy], slots.at[j], send_sems, recv_sems, j, (cx, cy, c)),
                    _remote(slots.at[j], slots.at[j], send_sems, recv_sems, j, (cx, cy, c))))
    return out


def _scatter_start(parts, after, name):
    n = len(parts)
    slots = [lax.empty((3,) + p.shape[1:], p.dtype) for p in parts]

    def body(*refs):
        ins, lands = refs[:n], refs[n:2 * n]
        send_sems, recv_sems, token = refs[4 * n + 1:5 * n + 1], refs[5 * n + 1:6 * n + 1], refs[6 * n + 1]
        x, y, c = _mesh_pos()
        for w in range(n):
            for send, _ in _scatter_copies(ins[w], lands[w], send_sems[w], recv_sems[w], x, y, c):
                send.start()
        token[...] = jnp.zeros_like(token)

    outs = pl.pallas_call(
        body, name=name, in_specs=[HBM] * (2 * n) + [ANY],
        out_specs=[HBM] * (2 * n) + [SEM] * (2 * n) + [pl.BlockSpec(memory_space=pltpu.VMEM)],
        out_shape=[pltpu.HBM(a.shape, a.dtype) for a in list(parts) + slots] + [pltpu.SemaphoreType.DMA((3,))] * (2 * n)
        + [jax.ShapeDtypeStruct((8, 128), F32)],
        input_output_aliases={k: k for k in range(2 * n)},
        compiler_params=_params(has_side_effects=EFFECT),
    )(*_in_hbm(list(parts) + slots), after)
    return (outs[:n], outs[n:2 * n], outs[2 * n:3 * n], outs[3 * n:4 * n]), outs[4 * n]


def _scatter_wait(parts, slots, send_sems, recv_sems, after, name):
    n = len(parts)

    def body(*refs):
        ins, lands = refs[:n], refs[n:2 * n]
        sends, recvs = refs[2 * n:3 * n], refs[3 * n:4 * n]
        x, y, c = _mesh_pos()
        for w in range(n):
            for send, recv in _scatter_copies(ins[w], lands[w], sends[w], recvs[w], x, y, c):
                send.wait_send()
                recv.wait_recv()

    outs = pl.pallas_call(
        body, name=name, in_specs=[HBM] * (2 * n) + [SEM] * (2 * n) + [ANY], out_specs=[HBM] * (2 * n),
        out_shape=[pltpu.HBM(a.shape, a.dtype) for a in list(parts) + list(slots)],
        input_output_aliases={k: k for k in range(2 * n)},
        compiler_params=_params(has_side_effects=EFFECT),
    )(*parts, *slots, *send_sems, *recv_sems, after)
    return outs[:n], outs[n:]


def _all_reduce_small(buf, after, name):
    rows, width = buf.shape

    def body(in_ref, _, out_ref, gathered, send_sems, recv_sems):
        x, y, c = _mesh_pos()
        me = 4 * x + 2 * y + c
        gathered[me] = in_ref[...]
        flips = [(dx, dy, dz) for dx in (0, 1) for dy in (0, 1) for dz in (0, 1)][1:]
        peers = [((1 - x) if dx else x, (1 - y) if dy else y, (1 - c) if dz else c) for dx, dy, dz in flips]
        sends = [_remote(in_ref, gathered.at[me], send_sems, recv_sems, k, peer) for k, peer in enumerate(peers)]
        for cp in sends:
            cp.start()
        for k, (px, py, pc) in enumerate(peers):
            got = gathered.at[4 * px + 2 * py + pc]
            _remote(got, got, send_sems, recv_sems, k, (px, py, pc)).wait_recv()
        for cp in sends:
            cp.wait_send()
        total = gathered[0]
        for d in range(1, 8):
            total = total + gathered[d]
        out_ref[...] = total

    vm = pl.BlockSpec(memory_space=pltpu.VMEM)
    return pl.pallas_call(
        body, name=name, in_specs=[vm, ANY], out_specs=vm, out_shape=jax.ShapeDtypeStruct(buf.shape, F32),
        scratch_shapes=[pltpu.VMEM((8, rows, width), F32), pltpu.SemaphoreType.DMA((7,)), pltpu.SemaphoreType.DMA((7,))],
        compiler_params=_params(has_side_effects=True),
    )(buf, after)


PACK_W = 2048
PACK_ROWS = 8


def _pack(vectors):
    flat = jnp.concatenate([v.reshape(-1) for v in vectors])
    unit = PACK_W * PACK_ROWS
    total = -(-flat.shape[0] // unit) * unit
    return jnp.pad(flat, (0, total - flat.shape[0])).reshape(total // PACK_W, PACK_W)


def _unpack(buf, shapes):
    flat = buf.reshape(-1)
    out, pos = [], 0
    for shp in shapes:
        size = math.prod(shp)
        out.append(flat[pos:pos + size].reshape(shp))
        pos += size
    return out


def _pad_rows(a, rows):
    return jnp.pad(a, ((0, rows - a.shape[0]), (0, 0)))


def kernel(x, g_pre_mix, w_in, b_in, w_dw, b_dw, g_conv_ln, b_conv_ln, w_sb_out, w_conv_out, w_o, g_post_mix, g_pre_mlp, w_up, w_down, g_post_mlp, loss_target, m_g_pre_mix, m_w_in, m_b_in, m_w_dw, m_b_dw, m_g_conv_ln, m_b_conv_ln, m_w_sb_out, m_w_conv_out, m_w_o, m_g_post_mix, m_g_pre_mlp, m_w_up, m_w_down, m_g_post_mlp, v_g_pre_mix, v_w_in, v_b_in, v_w_dw, v_b_dw, v_g_conv_ln, v_b_conv_ln, v_w_sb_out, v_w_conv_out, v_w_o, v_g_post_mix, v_g_pre_mlp, v_w_up, v_w_down, v_g_post_mlp):
    xs = x[0]
    tgt = loss_target[0]
    s, d = xs.shape
    d_sb = w_sb_out.shape[1]
    dc = w_conv_out.shape[1]
    d_in = w_in.shape[2] * N_CHIPS
    n_heads = d_sb // HEAD_DIM
    off_a = 3 * d_sb
    off_sb = off_a + 2 * dc
    off_cv = off_sb + d
    gw = dc
    assert off_a % dc == 0 and off_sb % gw == 0 and d % gw == 0 and d_in == off_cv + d
    chip = 2 * lax.axis_index("x") + lax.axis_index("y")

    big = dict(w_in=w_in[0], w_sb_out=w_sb_out[0], w_conv_out=w_conv_out[0], w_o=w_o[0], w_up=w_up[0], w_down=w_down[0])
    names = list(big)
    col_sharded = {"w_in", "w_sb_out", "w_conv_out", "w_up"}
    where = jnp.stack([chip, lax.axis_index("c")]).astype(jnp.int32)
    cs = w_dw.shape[2]
    taps4 = _gather_taps(_pad_rows(w_dw[0], CONV_HALO))
    first, token = _gather_start([_cast_into_block(big["w_in"], where, taps4, "cast_w_in")], taps4, "gather_start_in")
    rest, rest_token = _gather_start([_cast_into_block(big[n], where, token, "cast_" + n) for n in names[1:]], token,
                            "gather_start_rest")
    in_flight = dict(zip(names, zip(*[a + b for a, b in zip(first, rest)])))
    gathered = {}

    def land(group, after, tag):
        thru, sends, recvs = zip(*[in_flight[n] for n in group])
        landed = _gather_wait(thru, sends, recvs, after, "gather_wait_" + tag)
        return _phase_start(landed, _forward_copies, 3 * len(group), where, "forward_start_" + tag)

    def ready(group, started, after, tag):
        slabs, sends, recvs, _ = started
        for n, g4 in zip(group, _phase_wait(slabs, sends, recvs, _forward_copies, after, "forward_wait_" + tag)):
            gathered[n] = g4 if n in col_sharded else g4.reshape(g4.shape[0] * g4.shape[1], g4.shape[2])

    taps = jnp.transpose(taps4, (1, 0, 2)).reshape(CONV_HALO, N_CHIPS * cs)
    group_mix = ["w_sb_out", "w_conv_out", "w_o"]

    h = _rms_fwd(xs, g_pre_mix)
    started = land(["w_in"], rest_token, "in")
    ready(["w_in"], started, h, "in")
    proj = _mm(h, gathered["w_in"], name="mm_proj", b_groups=N_CHIPS, bias=b_in, tn=768)
    mix_fwd = land(group_mix, proj, "mix")
    attn = _attn_fwd(proj, mix_fwd[3], n_heads, d_sb)
    u0 = _glu(proj, off_a, dc)
    u1, u3 = _conv_fwd(u0, taps, b_dw, g_conv_ln, b_conv_ln)
    ready(group_mix, mix_fwd, u3, "mix")
    o_sb = _mm(attn, gathered["w_sb_out"], name="mm_o_sb", b_groups=N_CHIPS)
    up_fwd = land(["w_up"], o_sb, "up")
    o_cv = _mm(u3, gathered["w_conv_out"], name="mm_o_cv", b_groups=N_CHIPS, after=up_fwd[3])
    merged = _merge_fwd(proj, o_sb, o_cv, off_sb, off_cv, gw)
    y = _mm(merged, gathered["w_o"], name="mm_y")
    x2, h2 = _resid_rms(xs, y, g_post_mix, g_pre_mlp)
    ready(["w_up"], up_fwd, h2, "up")
    down_fwd = land(["w_down"], h2, "down")
    up, f = _mm(h2, gathered["w_up"], name="mm_up", b_groups=N_CHIPS, out_dtypes=(F32, BF16), after=down_fwd[3],
                epilogue=lambda acc: (acc, jnp.square(jnp.maximum(acc, 0.0))))
    ready(["w_down"], down_fwd, f, "down")
    dn = _mm(f, gathered["w_down"], name="mm_down", tm=512, tk=4096)
    dx3, d_dn, dg_post_mlp, loss_part = _final(x2, dn, g_post_mlp, tgt)

    moments = dict(w_in=(m_w_in, v_w_in), w_sb_out=(m_w_sb_out, v_w_sb_out), w_conv_out=(m_w_conv_out, v_w_conv_out),
                   w_o=(m_w_o, v_w_o), w_up=(m_w_up, v_w_up), w_down=(m_w_down, v_w_down))
    out_g, out_d, out_m, out_v = {}, {}, {}, {}
    grads = {}

    def blocks_of(group):
        out = []
        for n in group:
            g = grads[n]
            out.append(g if g.ndim == 3 else g.reshape(N_CHIPS, g.shape[0] // N_CHIPS, g.shape[1]))
        return out

    def exchange_begin(group, tag):
        blocks = blocks_of(group)
        lands = [lax.empty((b.shape[0], b.shape[1] // 2, b.shape[2]), b.dtype) for b in blocks]
        return _phase_start(blocks + lands, _exchange_copies, len(group), where, "exchange_start_" + tag)

    def scatter_begin(group, tag, exchange, after):
        arrays, sends, recvs, _ = exchange
        arrays = _phase_wait(arrays, sends, recvs, _exchange_copies, after, "exchange_wait_" + tag)
        n = len(group)
        pair_sums = [_add_pair(g, t, where, "pair_sum_" + m) for m, g, t in zip(group, arrays[:n], arrays[n:])]
        return _scatter_start(pair_sums, where, "scatter_start_" + tag)

    def reduce_sum(group, tag, started, after):
        parts, slots = _scatter_wait(*started, after, "scatter_wait_" + tag)
        halves = [_sum_slots(p, t, where, "chip_sum_" + n) for n, p, t in zip(group, parts, slots)]
        return _phase_start(halves, _share_copies, len(group), where, "share_start_" + tag)

    def reduce_update(group, tag, shared, after):
        halves, sends, recvs, _ = shared
        for n, red in zip(group, _phase_wait(halves, sends, recvs, _share_copies, after, "share_wait_" + tag)):
            mm_, vv_ = moments[n]
            res = _adamw(big[n], red, mm_[0], vv_[0], after, "adamw_" + n)
            out_g[n], out_d[n], out_m[n], out_v[n] = [r[None] for r in res]
            after = res[1]
        return after

    group_mlp, group_mix, group_in = ["w_down", "w_up"], ["w_o", "w_sb_out", "w_conv_out"], ["w_in"]
    grads["w_down"] = _mm(f, d_dn, name="mm_dw_down", ta=True, out_dtypes=(BF16,))
    dup = _mm(d_dn, gathered["w_down"], name="mm_df", tb=True, extra=[up], out_dtypes=(BF16,),
              epilogue=lambda acc, upv: (acc * (2.0 * jnp.maximum(upv, 0.0)),))
    grads["w_up"] = _mm(h2, dup, name="mm_dw_up", ta=True, out_groups=N_CHIPS, out_dtypes=(BF16,))
    mlp_exchange = exchange_begin(group_mlp, "mlp")
    dh2 = _mm(dup, gathered["w_up"], name="mm_dh2", tb=True, b_groups=N_CHIPS, after=mlp_exchange[3])
    dx2, dy, dg_pre_mlp, dg_post_mix = _rms_bwd2(dx3, dh2, x2, y, g_post_mix, g_pre_mlp)
    mlp_started, mlp_token = scatter_begin(group_mlp, "mlp", mlp_exchange, dx2)
    grads["w_o"] = _mm(merged, dy, name="mm_dw_o", ta=True, out_dtypes=(BF16,), after=mlp_token)
    dmerged = _mm(dy, gathered["w_o"], name="mm_dmerged", tb=True)
    do_sb, do_cv, dgate_sb, dgate_cv, s_gate_sb, s_gate_cv = _merge_bwd(proj, o_sb, o_cv, dmerged, off_sb, off_cv, gw)
    grads["w_sb_out"] = _mm(attn, do_sb, name="mm_dw_sb", ta=True, out_groups=N_CHIPS, out_dtypes=(BF16,))
    dattn = _mm(do_sb, gathered["w_sb_out"], name="mm_dattn", tb=True, b_groups=N_CHIPS)
    grads["w_conv_out"] = _mm(u3, do_cv, name="mm_dw_cv", ta=True, out_groups=N_CHIPS, out_dtypes=(BF16,))
    mix_exchange = exchange_begin(group_mix, "mix")
    du3 = _mm(do_cv, gathered["w_conv_out"], name="mm_du3", tb=True, b_groups=N_CHIPS, after=mix_exchange[3])
    mix_started, mix_token = scatter_begin(group_mix, "mix", mix_exchange, du3)
    dq, dk, dv, s_q, s_k, s_v = _attn_bwd(proj, dattn, mix_token, n_heads, d_sb)
    du1, dg_conv_ln, db_conv_ln, db_dw = _conv_bwd_norm(u1, du3, g_conv_ln, b_conv_ln)
    dglu_a, dglu_b, s_a, s_b, dtaps = _conv_bwd_taps(du1, u0, proj, taps, off_a)
    dproj = jnp.concatenate([dq, dk, dv, dglu_a, dglu_b, dgate_sb, dgate_cv], axis=1)
    db_in = jnp.concatenate([s_q, s_k, s_v, s_a, s_b, s_gate_sb, s_gate_cv], axis=1)
    small_w = dict(g_pre_mix=g_pre_mix, b_in=b_in, b_dw=b_dw, g_conv_ln=g_conv_ln, b_conv_ln=b_conv_ln,
                   g_post_mix=g_post_mix, g_pre_mlp=g_pre_mlp, g_post_mlp=g_post_mlp)
    small_m = dict(g_pre_mix=m_g_pre_mix, b_in=m_b_in, b_dw=m_b_dw, g_conv_ln=m_g_conv_ln, b_conv_ln=m_b_conv_ln,
                   g_post_mix=m_g_post_mix, g_pre_mlp=m_g_pre_mlp, g_post_mlp=m_g_post_mlp)
    small_v = dict(g_pre_mix=v_g_pre_mix, b_in=v_b_in, b_dw=v_b_dw, g_conv_ln=v_g_conv_ln, b_conv_ln=v_b_conv_ln,
                   g_post_mix=v_g_post_mix, g_pre_mlp=v_g_pre_mlp, g_post_mlp=v_g_post_mlp)
    small_g = dict(b_in=db_in, b_dw=db_dw, g_conv_ln=dg_conv_ln, b_conv_ln=db_conv_ln,
                   g_post_mix=dg_post_mix, g_pre_mlp=dg_pre_mlp, g_post_mlp=dg_post_mlp)

    def small_update(group, tails, after, tag):
        shapes = [small_w[n].shape for n in group] + [t.shape for t in tails]
        summed = _all_reduce_small(_pack([small_g[n] for n in group] + tails), after, "all_reduce_small_" + tag)
        zeros_tail = [jnp.zeros(t.shape, F32) for t in tails]
        res = _adamw(_pack([small_w[n] for n in group] + zeros_tail), summed,
                     _pack([small_m[n] for n in group] + zeros_tail), _pack([small_v[n] for n in group] + zeros_tail),
                     summed, "adamw_small_" + tag)
        unpacked = [_unpack(r, shapes) for r in res]
        for i, n in enumerate(group):
            out_g[n], out_d[n], out_m[n], out_v[n] = [u[i] for u in unpacked]
        return summed, unpacked[0][len(group):]

    early = ["b_in", "b_dw", "g_conv_ln", "b_conv_ln", "g_post_mix", "g_pre_mlp", "g_post_mlp"]
    summed, (loss_sum, taps_sum) = small_update(early, [loss_part, dtaps], dproj, "early")
    loss = loss_sum[0, 0]
    taps_grad = lax.dynamic_slice(taps_sum, (0, chip * cs), (CONV_HALO, cs))
    res = _adamw(_pad_rows(w_dw[0], CONV_HALO), taps_grad, _pad_rows(m_w_dw[0], CONV_HALO), _pad_rows(v_w_dw[0], CONV_HALO),
                 taps_grad, "adamw_taps")
    out_g["w_dw"], out_d["w_dw"], out_m["w_dw"], out_v["w_dw"] = [r[:CONV_WIDTH][None] for r in res]
    grads["w_in"] = _mm(h, dproj, name="mm_dw_in", ta=True, out_groups=N_CHIPS, out_dtypes=(BF16,), tn=768, after=summed)
    in_exchange = exchange_begin(group_in, "in")
    dh = _mm(dproj, gathered["w_in"], name="mm_dh", tb=True, b_groups=N_CHIPS, tk=2304, after=in_exchange[3])
    in_started, in_token = scatter_begin(group_in, "in", in_exchange, dh)
    grad_x, dg_pre_mix = _rms_bwd1(dx2, dh, xs, g_pre_mix)
    mlp_shared = reduce_sum(group_mlp, "mlp", mlp_started, in_token)
    mix_shared = reduce_sum(group_mix, "mix", mix_started, mlp_shared[3])
    done = reduce_update(group_mlp, "mlp", mlp_shared, mix_shared[3])
    done = reduce_update(group_mix, "mix", mix_shared, done)
    small_g["g_pre_mix"] = dg_pre_mix
    summed, _ = small_update(["g_pre_mix"], [], done, "late")
    in_shared = reduce_sum(group_in, "in", in_started, summed)
    reduce_update(group_in, "in", in_shared, in_shared[3])

    order = ["g_pre_mix", "w_in", "b_in", "w_dw", "b_dw", "g_conv_ln", "b_conv_ln", "w_sb_out", "w_conv_out", "w_o",
             "g_post_mix", "g_pre_mlp", "w_up", "w_down", "g_post_mlp"]
    return (loss, grad_x[None], *[out_g[n] for n in order], *[out_d[n] for n in order],
            *[out_m[n] for n in order], *[out_v[n] for n in order])
```

```python
import math

import jax
import jax.numpy as jnp
from jax import lax
from jax.experimental import pallas as pl
from jax.experimental.pallas import tpu as pltpu

F32 = jnp.float32
BF16 = jnp.bfloat16
MESH = pl.DeviceIdType.MESH

HEAD_DIM = 128
CONV_WIDTH = 31
CONV_HALO = 32
EPS = 1e-6
ADAM_LR = 0.001
ADAM_B1 = 0.9
ADAM_B2 = 0.999
ADAM_EPS = 1e-08
ADAM_WD = 0.01
ADAM_STEP = 10
N_CHIPS = 4
VMEM_LIMIT = 56 * 1024 * 1024


def _params(**kw):
    return pltpu.CompilerParams(vmem_limit_bytes=VMEM_LIMIT, **kw)


def _tile(n, want, align=128):
    if n <= want:
        return n
    for t in range(want - want % align, 0, -align):
        if n % t == 0:
            return t
    raise ValueError((n, want, align))


def _mm(a, b, *, name, ta=False, tb=False, b_groups=1, out_groups=1, bias=None, extra=None, epilogue=None,
        out_dtypes=(F32,), tm=1024, tn=1024, tk=2048, after=None):
    if ta:
        K, M = a.shape
    else:
        M, K = a.shape
    if b_groups == 1:
        br, bc = b.shape
    else:
        _, br, bcg = b.shape
        bc = bcg * b_groups
    if tb:
        N, Kb = br, bc
    else:
        Kb, N = br, bc
    assert Kb == K, (name, a.shape, b.shape)
    tm, tn, tk = _tile(M, tm), _tile(N, tn), _tile(K, tk)
    if b_groups > 1:
        if tb:
            tk = _tile(K // b_groups, tk)
        else:
            tn = _tile(N // b_groups, tn)
    if out_groups > 1:
        tn = _tile(N // out_groups, tn)
        if b_groups > 1 and not tb:
            tn = _tile(N // b_groups, tn)
    nm, nn, nk = M // tm, N // tn, K // tk
    extra = tuple(extra or ())
    n_extra = len(extra)
    has_bias = bias is not None
    n_out = len(out_dtypes)

    a_spec = pl.BlockSpec((tk, tm), lambda i, j, k: (k, i)) if ta else pl.BlockSpec((tm, tk), lambda i, j, k: (i, k))
    if b_groups == 1:
        b_spec = pl.BlockSpec((tn, tk), lambda i, j, k: (j, k)) if tb else pl.BlockSpec((tk, tn), lambda i, j, k: (k, j))
    elif tb:
        kpg = (K // b_groups) // tk
        b_spec = pl.BlockSpec((None, tn, tk), lambda i, j, k: (k // kpg, j, k % kpg))
    else:
        npg = (N // b_groups) // tn
        b_spec = pl.BlockSpec((None, tk, tn), lambda i, j, k: (j // npg, k, j % npg))
    in_specs = [a_spec, b_spec]
    operands = [a, b]
    if has_bias:
        in_specs.append(pl.BlockSpec((1, tn), lambda i, j, k: (0, j)))
        operands.append(bias)
    for e in extra:
        in_specs.append(pl.BlockSpec((tm, tn), lambda i, j, k: (i, j)))
        operands.append(e)
    n_after = 0 if after is None else 1
    if after is not None:
        in_specs.append(pl.BlockSpec(memory_space=pl.ANY))
        operands.append(after)
    if out_groups == 1:
        o_spec = pl.BlockSpec((tm, tn), lambda i, j, k: (i, j))
        o_shape = (M, N)
    else:
        opg = (N // out_groups) // tn
        o_spec = pl.BlockSpec((None, tm, tn), lambda i, j, k: (j // opg, i, j % opg))
        o_shape = (out_groups, M, N // out_groups)
    dims = (((0 if ta else 1,), (1 if tb else 0,)), ((), ()))

    def body(*refs):
        a_ref, b_ref = refs[0], refs[1]
        pos = 2
        bias_ref = None
        if has_bias:
            bias_ref = refs[pos]
            pos += 1
        extra_refs = refs[pos:pos + n_extra]
        pos += n_extra + n_after
        out_refs = refs[pos:pos + n_out]
        pos += n_out
        acc_ref = refs[pos] if nk > 1 else None

        part = lax.dot_general(a_ref[...].astype(BF16), b_ref[...].astype(BF16), dims, preferred_element_type=F32)

        def finish(acc):
            if has_bias:
                acc = acc + bias_ref[...]
            outs = epilogue(acc, *[e[...] for e in extra_refs]) if epilogue is not None else (acc,)
            for o_ref, o in zip(out_refs, outs):
                o_ref[...] = o.astype(o_ref.dtype)

        if nk == 1:
            finish(part)
        else:
            k = pl.program_id(2)

            @pl.when(k == 0)
            def _():
                acc_ref[...] = part

            @pl.when(k > 0)
            def _():
                acc_ref[...] += part

            @pl.when(k == nk - 1)
            def _():
                finish(acc_ref[...])

    outs = pl.pallas_call(
        body,
        name=name,
        grid=(nm, nn, nk),
        in_specs=in_specs,
        out_specs=[o_spec] * n_out,
        out_shape=[jax.ShapeDtypeStruct(o_shape, dt) for dt in out_dtypes],
        scratch_shapes=[pltpu.VMEM((tm, tn), F32)] if nk > 1 else [],
        compiler_params=_params(dimension_semantics=("parallel", "parallel", "arbitrary")),
    )(*operands)
    return outs[0] if n_out == 1 else outs


def _rms_stats(x):
    return lax.rsqrt(jnp.mean(x * x, axis=-1, keepdims=True) + EPS)


def _rms_bwd(x, r, g, dy):
    gy = dy * g
    return r * gy - x * (r * r * r) * jnp.mean(x * gy, axis=-1, keepdims=True)


def _sigmoid(x):
    return 1.0 / (1.0 + jnp.exp(-x))


def _row_call(body, *, name, rows, tr, ins, outs, acc_outs=()):
    n = rows // tr
    in_specs = []
    for arr, kind in ins:
        if kind == "row":
            in_specs.append(pl.BlockSpec((tr, arr.shape[1]), lambda i: (i, 0)))
        else:
            in_specs.append(pl.BlockSpec(arr.shape, lambda i: (0, 0)))
    out_specs = [pl.BlockSpec((tr, w), lambda i: (i, 0)) for w, _ in outs]
    out_shape = [jax.ShapeDtypeStruct((rows, w), dt) for w, dt in outs]
    out_specs += [pl.BlockSpec((1, w), lambda i: (0, 0)) for w in acc_outs]
    out_shape += [jax.ShapeDtypeStruct((1, w), F32) for w in acc_outs]
    return pl.pallas_call(
        body, name=name, grid=(n,), in_specs=in_specs, out_specs=out_specs, out_shape=out_shape,
        compiler_params=_params(dimension_semantics=("arbitrary",)),
    )(*[a for a, _ in ins])


def _accumulate(ref, val):
    @pl.when(pl.program_id(0) == 0)
    def _():
        ref[...] = val

    @pl.when(pl.program_id(0) > 0)
    def _():
        ref[...] += val


def _rms_fwd(x, g, tr=256):
    def body(x_ref, g_ref, h_ref):
        xv = x_ref[...]
        h_ref[...] = (xv * _rms_stats(xv) * g_ref[...]).astype(BF16)

    (h,) = _row_call(body, name="rms_fwd", rows=x.shape[0], tr=tr, ins=[(x, "row"), (g, "vec")], outs=[(x.shape[1], BF16)])
    return h


def _resid_rms(x, y, g2, g3, tr=256):
    def body(x_ref, y_ref, g2_ref, g3_ref, x2_ref, h2_ref):
        yv = y_ref[...]
        x2 = x_ref[...] + yv * _rms_stats(yv) * g2_ref[...]
        x2_ref[...] = x2
        h2_ref[...] = (x2 * _rms_stats(x2) * g3_ref[...]).astype(BF16)

    d = x.shape[1]
    return _row_call(body, name="resid_rms", rows=x.shape[0], tr=tr,
                     ins=[(x, "row"), (y, "row"), (g2, "vec"), (g3, "vec")], outs=[(d, F32), (d, BF16)])


def _final(x2, dn, g4, tgt, tr=256):
    d = x2.shape[1]

    def body(x2_ref, dn_ref, g4_ref, t_ref, dx3_ref, ddn_ref, dg4_ref, loss_ref):
        dn_v = dn_ref[...]
        r = _rms_stats(dn_v)
        g = g4_ref[...]
        e = x2_ref[...] + dn_v * r * g - t_ref[...]
        dx3 = e * (1.0 / d)
        dx3_ref[...] = dx3
        ddn_ref[...] = _rms_bwd(dn_v, r, g, dx3).astype(BF16)
        _accumulate(dg4_ref, jnp.sum(dx3 * dn_v * r, axis=0, keepdims=True))
        part = 0.5 * jnp.sum(jnp.mean(e * e, axis=-1, keepdims=True), axis=0, keepdims=True)
        _accumulate(loss_ref, jnp.broadcast_to(part, loss_ref.shape))

    return _row_call(body, name="final", rows=x2.shape[0], tr=tr,
                     ins=[(x2, "row"), (dn, "row"), (g4, "vec"), (tgt, "row")],
                     outs=[(d, F32), (d, BF16)], acc_outs=(d, 128))


def _rms_bwd2(dx3, dh2, x2, y, g2, g3, tr=256):
    d = x2.shape[1]

    def body(dx3_ref, dh2_ref, x2_ref, y_ref, g2_ref, g3_ref, dx2_ref, dy_ref, dg3_ref, dg2_ref):
        x2v, dh2v, yv = x2_ref[...], dh2_ref[...], y_ref[...]
        r3 = _rms_stats(x2v)
        dx2 = dx3_ref[...] + _rms_bwd(x2v, r3, g3_ref[...], dh2v)
        dx2_ref[...] = dx2
        _accumulate(dg3_ref, jnp.sum(dh2v * x2v * r3, axis=0, keepdims=True))
        r2 = _rms_stats(yv)
        dy_ref[...] = _rms_bwd(yv, r2, g2_ref[...], dx2).astype(BF16)
        _accumulate(dg2_ref, jnp.sum(dx2 * yv * r2, axis=0, keepdims=True))

    return _row_call(body, name="rms_bwd2", rows=x2.shape[0], tr=tr,
                     ins=[(dx3, "row"), (dh2, "row"), (x2, "row"), (y, "row"), (g2, "vec"), (g3, "vec")],
                     outs=[(d, F32), (d, BF16)], acc_outs=(d, d))


def _rms_bwd1(dx2, dh, x, g1, tr=256):
    d = x.shape[1]

    def body(dx2_ref, dh_ref, x_ref, g1_ref, gx_ref, dg1_ref):
        xv, dhv = x_ref[...], dh_ref[...]
        r = _rms_stats(xv)
        gx_ref[...] = dx2_ref[...] + _rms_bwd(xv, r, g1_ref[...], dhv)
        _accumulate(dg1_ref, jnp.sum(dhv * xv * r, axis=0, keepdims=True))

    return _row_call(body, name="rms_bwd1", rows=x.shape[0], tr=tr,
                     ins=[(dx2, "row"), (dh, "row"), (x, "row"), (g1, "vec")], outs=[(d, F32)], acc_outs=(d,))


def _merge_fwd(proj, o_sb, o_cv, off_sb, off_cv, gw, tr=256):
    s, d = o_sb.shape
    nj = d // gw
    b_sb, b_cv = off_sb // gw, off_cv // gw

    def body(gs_ref, gc_ref, osb_ref, ocv_ref, m_ref):
        m_ref[...] = (_sigmoid(gs_ref[...]) * osb_ref[...] + _sigmoid(gc_ref[...]) * ocv_ref[...]).astype(BF16)

    blk = lambda i, j: (i, j)
    return pl.pallas_call(
        body, name="merge_fwd", grid=(s // tr, nj),
        in_specs=[pl.BlockSpec((tr, gw), lambda i, j: (i, b_sb + j)), pl.BlockSpec((tr, gw), lambda i, j: (i, b_cv + j)),
                  pl.BlockSpec((tr, gw), blk), pl.BlockSpec((tr, gw), blk)],
        out_specs=pl.BlockSpec((tr, gw), blk), out_shape=jax.ShapeDtypeStruct((s, d), BF16),
        compiler_params=_params(dimension_semantics=("parallel", "parallel")),
    )(proj, proj, o_sb, o_cv)


def _merge_bwd(proj, o_sb, o_cv, dmerged, off_sb, off_cv, gw, tr=256):
    s, d = o_sb.shape
    nj = d // gw
    ni = s // tr
    b_sb, b_cv = off_sb // gw, off_cv // gw

    def body(gs_ref, gc_ref, osb_ref, ocv_ref, dm_ref, dosb_ref, docv_ref, dgs_ref, dgc_ref, sgs_ref, sgc_ref):
        dm = dm_ref[...]
        s_sb, s_cv = _sigmoid(gs_ref[...]), _sigmoid(gc_ref[...])
        dosb_ref[...] = (dm * s_sb).astype(BF16)
        docv_ref[...] = (dm * s_cv).astype(BF16)
        dgs = dm * osb_ref[...] * s_sb * (1.0 - s_sb)
        dgc = dm * ocv_ref[...] * s_cv * (1.0 - s_cv)
        dgs_ref[...] = dgs.astype(BF16)
        dgc_ref[...] = dgc.astype(BF16)
        i = pl.program_id(1)
        for ref, val in ((sgs_ref, dgs), (sgc_ref, dgc)):
            col = jnp.sum(val, axis=0, keepdims=True)

            @pl.when(i == 0)
            def _():
                ref[...] = col

            @pl.when(i > 0)
            def _():
                ref[...] += col

    blk = lambda j, i: (i, j)
    outs = pl.pallas_call(
        body, name="merge_bwd", grid=(nj, ni),
        in_specs=[pl.BlockSpec((tr, gw), lambda j, i: (i, b_sb + j)), pl.BlockSpec((tr, gw), lambda j, i: (i, b_cv + j)),
                  pl.BlockSpec((tr, gw), blk), pl.BlockSpec((tr, gw), blk), pl.BlockSpec((tr, gw), blk)],
        out_specs=[pl.BlockSpec((tr, gw), blk), pl.BlockSpec((tr, gw), blk),
                   pl.BlockSpec((tr, gw), blk), pl.BlockSpec((tr, gw), blk),
                   pl.BlockSpec((1, gw), lambda j, i: (0, j)), pl.BlockSpec((1, gw), lambda j, i: (0, j))],
        out_shape=[jax.ShapeDtypeStruct((s, d), BF16)] * 4 + [jax.ShapeDtypeStruct((1, d), F32)] * 2,
        compiler_params=_params(dimension_semantics=("parallel", "arbitrary")),
    )(proj, proj, o_sb, o_cv, dmerged)
    return outs


def _split_bf16(v):
    hi = v.astype(BF16)
    lo = (v - hi.astype(F32)).astype(BF16)
    return hi, lo


def _dot_nt(a, b):
    return lax.dot_general(a, b, (((1,), (1,)), ((), ())), preferred_element_type=F32)


def _dot_tn(a, b):
    return lax.dot_general(a, b, (((0,), (0,)), ((), ())), preferred_element_type=F32)


def _dot_nn(a, b):
    return lax.dot_general(a, b, (((1,), (0,)), ((), ())), preferred_element_type=F32)


def _sum_right(v, tri):
    hi, lo = _split_bf16(v)
    return _dot_nn(hi, tri) + _dot_nn(lo, tri)


HEADS_PER_STEP = 4


LOG2_E = 1.4426950408889634


def _sb_tile(q, kb, scale, mask):
    z = _dot_nt(q, kb) * (scale * LOG2_E)
    log_b = jnp.minimum(z, 0.0) - jnp.log2(1.0 + jnp.exp2(-jnp.abs(z)))
    log_1m = log_b - z
    if mask is not None:
        log_1m = jnp.where(mask, log_1m, 0.0)
    return log_b, log_1m


def _tile_mask(tq):
    row = lax.broadcasted_iota(jnp.int32, (tq, tq), 0)
    col = lax.broadcasted_iota(jnp.int32, (tq, tq), 1)
    return col < row


def _tri(tq, before):
    r = lax.broadcasted_iota(jnp.int32, (tq, tq), 0)
    c = lax.broadcasted_iota(jnp.int32, (tq, tq), 1)
    return jnp.where((r < c) if before else (r > c), 1.0, 0.0).astype(BF16)


def _attn_fwd(proj, after, n_heads, d_sb, tq=256):
    s = proj.shape[0]
    tq = _tile(s, tq)
    hb = d_sb // HEAD_DIM
    scale = 1.0 / math.sqrt(HEAD_DIM)

    per_step = min(HEADS_PER_STEP, n_heads)
    heads = [pl.ds(n * HEAD_DIM, HEAD_DIM) for n in range(per_step)]
    wide = per_step * HEAD_DIM
    hb //= per_step

    def body(q_ref, k_ref, v_ref, _, o_ref):
        i = pl.program_id(1)
        tri = _tri(tq, False)

        def tile(j, carry, diagonal):
            rows = pl.ds(pl.multiple_of(j * tq, tq), tq)
            mask = _tile_mask(tq) if diagonal else None
            out = []
            for hd, (c_l, acc) in zip(heads, carry):
                log_b, log_1m = _sb_tile(q_ref[:, hd].astype(BF16), k_ref[rows, hd].astype(BF16), scale, mask)
                a = jnp.exp2(log_b + (c_l + _sum_right(log_1m, tri)))
                if diagonal:
                    a = jnp.where(mask, a, 0.0)
                out.append((c_l + jnp.sum(log_1m, axis=1, keepdims=True),
                            acc + _dot_nn(a.astype(BF16), v_ref[rows, hd].astype(BF16))))
            return tuple(out)

        init = tuple((jnp.zeros((tq, 1), F32), jnp.zeros((tq, HEAD_DIM), F32)) for _ in heads)
        carry = lax.fori_loop(0, i, lambda jj, c: tile(i - 1 - jj, c, False), tile(i, init, True))
        for hd, (_, acc) in zip(heads, carry):
            o_ref[:, hd] = acc.astype(BF16)

    return pl.pallas_call(
        body, name="attn_fwd", grid=(n_heads // per_step, s // tq),
        in_specs=[pl.BlockSpec((tq, wide), lambda h, i: (i, h)),
                  pl.BlockSpec((s, wide), lambda h, i: (0, hb + h)),
                  pl.BlockSpec((s, wide), lambda h, i: (0, 2 * hb + h)), pl.BlockSpec(memory_space=pl.ANY)],
        out_specs=pl.BlockSpec((tq, wide), lambda h, i: (i, h)),
        out_shape=jax.ShapeDtypeStruct((s, d_sb), BF16),
        compiler_params=_params(dimension_semantics=("parallel", "arbitrary")),
    )(proj, proj, proj, after)


def _attn_bwd(proj, dattn, after, n_heads, d_sb, tq=256):
    s = proj.shape[0]
    tq = _tile(s, tq)
    nq = s // tq
    hb = d_sb // HEAD_DIM // HEADS_PER_STEP
    scale = 1.0 / math.sqrt(HEAD_DIM)
    heads = [pl.ds(n * HEAD_DIM, HEAD_DIM) for n in range(HEADS_PER_STEP)]
    wide = HEADS_PER_STEP * HEAD_DIM

    def body(q_ref, k_ref, v_ref, do_ref, _, dq_ref, dk_ref, dv_ref, sq_ref, sk_ref, sv_ref, dk_acc, dv_acc, g_st, b_st):
        i = pl.program_id(1)

        @pl.when(i == 0)
        def _():
            dk_acc[...] = jnp.zeros_like(dk_acc)
            dv_acc[...] = jnp.zeros_like(dv_acc)

        tri_after = _tri(tq, False)
        tri_before = _tri(tq, True)

        def newest_first(j, c_ls, diagonal):
            rows = pl.ds(pl.multiple_of(j * tq, tq), tq)
            mask = _tile_mask(tq) if diagonal else None
            out = []
            for n, (hd, c_l) in enumerate(zip(heads, c_ls)):
                do = do_ref[:, hd].astype(BF16)
                log_b, log_1m = _sb_tile(q_ref[:, hd].astype(BF16), k_ref[rows, hd].astype(BF16), scale, mask)
                a = jnp.exp2(log_b + (c_l + _sum_right(log_1m, tri_after)))
                beta = jnp.exp2(log_b)
                if diagonal:
                    a = jnp.where(mask, a, 0.0)
                    beta = jnp.where(mask, beta, 0.0)
                g_st[n, j] = a * _dot_nt(do, v_ref[rows, hd].astype(BF16))
                b_st[n, j] = beta
                dv_acc[rows, hd] += _dot_tn(a.astype(BF16), do)
                out.append(c_l + jnp.sum(log_1m, axis=1, keepdims=True))
            return tuple(out)

        lax.fori_loop(0, i, lambda jj, c: newest_first(i - 1 - jj, c, False),
                      newest_first(i, tuple(jnp.zeros((tq, 1), F32) for _ in heads), True))

        def oldest_first(j, carry):
            rows = pl.ds(pl.multiple_of(j * tq, tq), tq)
            out = []
            for n, (hd, (c_g, dq)) in enumerate(zip(heads, carry)):
                kb = k_ref[rows, hd].astype(BF16)
                g = g_st[n, j]
                beta = b_st[n, j]
                g_before = c_g + _sum_right(g, tri_before)
                dz16 = ((g * (1.0 - beta) - g_before * beta) * scale).astype(BF16)
                dk_acc[rows, hd] += _dot_tn(dz16, q_ref[:, hd].astype(BF16))
                out.append((c_g + jnp.sum(g, axis=1, keepdims=True), dq + _dot_nn(dz16, kb)))
            return tuple(out)

        init = tuple((jnp.zeros((tq, 1), F32), jnp.zeros((tq, HEAD_DIM), F32)) for _ in heads)
        carry = lax.fori_loop(0, i + 1, oldest_first, init)
        dq = jnp.concatenate([c[1] for c in carry], axis=1)
        dq_ref[...] = dq.astype(BF16)
        col = jnp.sum(dq, axis=0, keepdims=True)

        @pl.when(i == 0)
        def _():
            sq_ref[...] = col

        @pl.when(i > 0)
        def _():
            sq_ref[...] += col

        @pl.when(i == nq - 1)
        def _():
            dk = dk_acc[...]
            dv = dv_acc[...]
            dk_ref[...] = dk.astype(BF16)
            dv_ref[...] = dv.astype(BF16)
            sk_ref[...] = jnp.sum(dk, axis=0, keepdims=True)
            sv_ref[...] = jnp.sum(dv, axis=0, keepdims=True)

    qblk = pl.BlockSpec((tq, wide), lambda h, i: (i, h))
    kblk = pl.BlockSpec((s, wide), lambda h, i: (0, h))
    col_sum = pl.BlockSpec((1, wide), lambda h, i: (0, h))
    stash = pltpu.VMEM((HEADS_PER_STEP, nq, tq, tq), F32)
    outs = pl.pallas_call(
        body, name="attn_bwd", grid=(n_heads // HEADS_PER_STEP, nq),
        in_specs=[qblk,
                  pl.BlockSpec((s, wide), lambda h, i: (0, hb + h)),
                  pl.BlockSpec((s, wide), lambda h, i: (0, 2 * hb + h)),
                  qblk, pl.BlockSpec(memory_space=pl.ANY)],
        out_specs=[qblk, kblk, kblk, col_sum, col_sum, col_sum],
        out_shape=[jax.ShapeDtypeStruct((s, d_sb), BF16)] * 3 + [jax.ShapeDtypeStruct((1, d_sb), F32)] * 3,
        scratch_shapes=[pltpu.VMEM((s, wide), F32), pltpu.VMEM((s, wide), F32), stash, stash],
        compiler_params=_params(dimension_semantics=("parallel", "arbitrary")),
    )(proj, proj, proj, dattn, after)
    return outs


LANES = 128


def _glu(proj, off_a, dc, tr=256):
    s = proj.shape[0]
    ba = off_a // dc

    def body(a_ref, b_ref, u_ref):
        u_ref[...] = a_ref[...] * _sigmoid(b_ref[...])

    return pl.pallas_call(
        body, name="glu", grid=(s // tr,),
        in_specs=[pl.BlockSpec((tr, dc), lambda i: (i, ba)), pl.BlockSpec((tr, dc), lambda i: (i, ba + 1))],
        out_specs=pl.BlockSpec((tr, dc), lambda i: (i, 0)), out_shape=jax.ShapeDtypeStruct((s, dc), F32),
        compiler_params=_params(dimension_semantics=("parallel",)),
    )(proj, proj)


SUBLANES = 8


def _shifted_rows(src, cols, shifted, tt):
    span = tt + CONV_HALO - SUBLANES
    for b in range(1, SUBLANES):
        shifted[b, pl.ds(0, span), :] = src[pl.ds(b, span), cols]

    def rows(o):
        a, b = divmod(o, SUBLANES)
        return src[pl.ds(SUBLANES * a, tt), cols] if b == 0 else shifted[b, pl.ds(SUBLANES * a, tt), :]

    return rows


def _conv_fwd(u0, taps, b_dw, g_ln, b_ln, tt=256):
    s, dc = u0.shape
    tt = _tile(s, tt)
    hpb = tt // CONV_HALO

    def body(cur_ref, halo_ref, w_ref, bdw_ref, g_ref, b_ref, u1_ref, u3_ref, xs, xs_shifted):
        i = pl.program_id(0)
        xs[pl.ds(0, CONV_HALO), :] = jnp.where(i > 0, halo_ref[...], 0.0)
        xs[pl.ds(CONV_HALO, tt), :] = cur_ref[...]
        for c0 in range(0, dc, LANES):
            cols = pl.ds(c0, LANES)
            x_rows = _shifted_rows(xs, cols, xs_shifted, tt)
            acc = jnp.broadcast_to(bdw_ref[:, cols], (tt, LANES))
            for k in range(CONV_WIDTH):
                acc = acc + w_ref[pl.ds(k, 1), cols] * x_rows(CONV_HALO - (CONV_WIDTH - 1) + k)
            u1_ref[:, cols] = acc
        u1 = u1_ref[...]
        mu = jnp.mean(u1, axis=-1, keepdims=True)
        xc = u1 - mu
        rstd = lax.rsqrt(jnp.mean(xc * xc, axis=-1, keepdims=True) + EPS)
        u2 = xc * rstd * g_ref[...] + b_ref[...]
        u3_ref[...] = (u2 * _sigmoid(u2)).astype(BF16)

    vec = pl.BlockSpec((1, dc), lambda i: (0, 0))
    return pl.pallas_call(
        body, name="conv_fwd", grid=(s // tt,),
        in_specs=[pl.BlockSpec((tt, dc), lambda i: (i, 0)),
                  pl.BlockSpec((CONV_HALO, dc), lambda i: (jnp.maximum(i * hpb - 1, 0), 0)),
                  pl.BlockSpec((CONV_HALO, dc), lambda i: (0, 0)), vec, vec, vec],
        out_specs=[pl.BlockSpec((tt, dc), lambda i: (i, 0)), pl.BlockSpec((tt, dc), lambda i: (i, 0))],
        out_shape=[jax.ShapeDtypeStruct((s, dc), F32), jax.ShapeDtypeStruct((s, dc), BF16)],
        scratch_shapes=[pltpu.VMEM((tt + CONV_HALO, dc), F32), pltpu.VMEM((SUBLANES, tt + CONV_HALO, LANES), F32)],
        compiler_params=_params(dimension_semantics=("arbitrary",)),
    )(u0, u0, taps, b_dw, g_ln, b_ln)


def _conv_bwd_norm(u1, du3, g_ln, b_ln, tr=256):
    dc = u1.shape[1]

    def body(u1_ref, du3_ref, g_ref, b_ref, du1_ref, dg_ref, db_ref, dbdw_ref):
        u1v = u1_ref[...]
        g = g_ref[...]
        mu = jnp.mean(u1v, axis=-1, keepdims=True)
        xc = u1v - mu
        rstd = lax.rsqrt(jnp.mean(xc * xc, axis=-1, keepdims=True) + EPS)
        xhat = xc * rstd
        u2 = xhat * g + b_ref[...]
        sg = _sigmoid(u2)
        du2 = du3_ref[...] * (sg * (1.0 + u2 * (1.0 - sg)))
        _accumulate(dg_ref, jnp.sum(du2 * xhat, axis=0, keepdims=True))
        _accumulate(db_ref, jnp.sum(du2, axis=0, keepdims=True))
        gy = du2 * g
        du1 = rstd * (gy - jnp.mean(gy, axis=-1, keepdims=True) - xhat * jnp.mean(gy * xhat, axis=-1, keepdims=True))
        du1_ref[...] = du1
        _accumulate(dbdw_ref, jnp.sum(du1, axis=0, keepdims=True))

    return _row_call(body, name="conv_bwd_norm", rows=u1.shape[0], tr=tr,
                     ins=[(u1, "row"), (du3, "row"), (g_ln, "vec"), (b_ln, "vec")],
                     outs=[(dc, F32)], acc_outs=(dc, dc, dc))


def _conv_bwd_taps(du1, u0, proj, taps, off_a, tt=256):
    s, dc = u0.shape
    tt = _tile(s, tt)
    n = s // tt
    hpb = tt // CONV_HALO
    ba = off_a // dc
    lead = CONV_HALO - (CONV_WIDTH - 1)

    def body(d_ref, dnext_ref, cur_ref, halo_ref, a_ref, b_ref, w_ref, da_ref, db_ref, sa_ref, sb_ref, dw_ref,
             xs, ds, du0, xs_shifted, ds_shifted):
        i = pl.program_id(0)
        xs[pl.ds(0, CONV_HALO), :] = jnp.where(i > 0, halo_ref[...], 0.0)
        xs[pl.ds(CONV_HALO, tt), :] = cur_ref[...]
        ds[pl.ds(0, tt), :] = d_ref[...]
        ds[pl.ds(tt, CONV_HALO), :] = jnp.where(i < n - 1, dnext_ref[...], 0.0)

        @pl.when(i == 0)
        def _():
            dw_ref[...] = jnp.zeros_like(dw_ref)

        for c0 in range(0, dc, LANES):
            cols = pl.ds(c0, LANES)
            x_rows = _shifted_rows(xs, cols, xs_shifted, tt)
            d_rows = _shifted_rows(ds, cols, ds_shifted, tt)
            d_cur = d_rows(0)
            acc = jnp.zeros((tt, LANES), F32)
            for k in range(CONV_WIDTH):
                acc = acc + w_ref[pl.ds(k, 1), cols] * d_rows(CONV_WIDTH - 1 - k)
                dw_ref[pl.ds(k, 1), cols] += jnp.sum(d_cur * x_rows(lead + k), axis=0, keepdims=True)
            du0[:, cols] = acc
        d0 = du0[...]
        sg = _sigmoid(b_ref[...])
        da = d0 * sg
        db = d0 * a_ref[...] * sg * (1.0 - sg)
        da_ref[...] = da.astype(BF16)
        db_ref[...] = db.astype(BF16)
        _accumulate(sa_ref, jnp.sum(da, axis=0, keepdims=True))
        _accumulate(sb_ref, jnp.sum(db, axis=0, keepdims=True))

    row = pl.BlockSpec((tt, dc), lambda i: (i, 0))
    vec = pl.BlockSpec((1, dc), lambda i: (0, 0))
    taps_spec = pl.BlockSpec((CONV_HALO, dc), lambda i: (0, 0))
    return pl.pallas_call(
        body, name="conv_bwd_taps", grid=(n,),
        in_specs=[row, pl.BlockSpec((CONV_HALO, dc), lambda i: (jnp.minimum((i + 1) * hpb, n * hpb - 1), 0)),
                  row, pl.BlockSpec((CONV_HALO, dc), lambda i: (jnp.maximum(i * hpb - 1, 0), 0)),
                  pl.BlockSpec((tt, dc), lambda i: (i, ba)), pl.BlockSpec((tt, dc), lambda i: (i, ba + 1)), taps_spec],
        out_specs=[row, row, vec, vec, taps_spec],
        out_shape=[jax.ShapeDtypeStruct((s, dc), BF16), jax.ShapeDtypeStruct((s, dc), BF16),
                   jax.ShapeDtypeStruct((1, dc), F32), jax.ShapeDtypeStruct((1, dc), F32),
                   jax.ShapeDtypeStruct((CONV_HALO, dc), F32)],
        scratch_shapes=[pltpu.VMEM((tt + CONV_HALO, dc), F32), pltpu.VMEM((tt + CONV_HALO, dc), F32),
                        pltpu.VMEM((tt, dc), F32),
                        pltpu.VMEM((SUBLANES, tt + CONV_HALO, LANES), F32), pltpu.VMEM((SUBLANES, tt + CONV_HALO, LANES), F32)],
        compiler_params=_params(dimension_semantics=("arbitrary",)),
    )(du1, du1, u0, u0, proj, proj, taps)


def _elementwise(body, *, name, ins, out_dtypes, after, tr=128):
    rows, cols = ins[0].shape
    tr = _tile(rows, tr, 8)
    spec = pl.BlockSpec((tr, cols), lambda i: (i, 0))
    n_in = len(ins)

    def with_after(*refs):
        body(*refs[:n_in], *refs[n_in + 1:])

    return pl.pallas_call(
        with_after, name=name, grid=(rows // tr,), in_specs=[spec] * n_in + [pl.BlockSpec(memory_space=pl.ANY)],
        out_specs=[spec] * len(out_dtypes),
        out_shape=[jax.ShapeDtypeStruct((rows, cols), dt) for dt in out_dtypes],
        compiler_params=_params(dimension_semantics=("parallel",)),
    )(*ins, after)


def _prefetched(body, *, name, where, grid, in_specs, out_specs, out_shape, operands):
    return pl.pallas_call(
        body, name=name, out_shape=out_shape,
        grid_spec=pltpu.PrefetchScalarGridSpec(num_scalar_prefetch=1, grid=grid, in_specs=in_specs, out_specs=out_specs),
        compiler_params=_params(dimension_semantics=("parallel",) * len(grid)),
    )(where, *operands)


def _cast_into_block(w, where, after, name, tr=256):
    rows, cols = w.shape
    tr = _tile(rows, tr, 16)

    def body(where_ref, w_ref, _, o_ref):
        o_ref[...] = w_ref[...].astype(BF16)

    return _prefetched(
        body, name=name, where=where, grid=(rows // tr,),
        in_specs=[pl.BlockSpec((tr, cols), lambda i, wh: (i, 0)), pl.BlockSpec(memory_space=pl.ANY)],
        out_specs=pl.BlockSpec((None, tr, cols), lambda i, wh: (wh[0], i, 0)),
        out_shape=jax.ShapeDtypeStruct((N_CHIPS, rows, cols), BF16), operands=[w, after])


def _add_pair(grad, theirs, where, name, tr=512):
    nb, half, cols = theirs.shape
    tr = _tile(half, tr, 16)
    nh = half // tr

    def body(where_ref, a_ref, b_ref, o_ref):
        o_ref[...] = a_ref[...] + b_ref[...]

    blk = pl.BlockSpec((None, tr, cols), lambda b, i, wh: (b, i, 0))
    return _prefetched(
        body, name=name, where=where, grid=(nb, nh),
        in_specs=[pl.BlockSpec((None, tr, cols), lambda b, i, wh: (b, wh[1] * nh + i, 0)), blk],
        out_specs=blk, out_shape=jax.ShapeDtypeStruct(theirs.shape, BF16), operands=[grad, theirs])


def _sum_slots(part, slots, where, name, tr=512):
    _, half, cols = part.shape
    tr = _tile(half, tr, 16)
    nh = half // tr

    def body(where_ref, p_ref, s_ref, o_ref):
        o_ref[...] = ((p_ref[...].astype(F32) + s_ref[0].astype(F32)) + s_ref[1].astype(F32)) + s_ref[2].astype(F32)

    return _prefetched(
        body, name=name, where=where, grid=(nh,),
        in_specs=[pl.BlockSpec((None, tr, cols), lambda i, wh: (wh[0], i, 0)),
                  pl.BlockSpec((3, tr, cols), lambda i, wh: (0, i, 0))],
        out_specs=pl.BlockSpec((tr, cols), lambda i, wh: (wh[1] * nh + i, 0)),
        out_shape=jax.ShapeDtypeStruct((2 * half, cols), F32), operands=[part, slots])


def _adamw(w, g, m, v, after, name):
    bc1 = 1.0 - ADAM_B1 ** ADAM_STEP
    bc2 = 1.0 - ADAM_B2 ** ADAM_STEP

    def body(w_ref, g_ref, m_ref, v_ref, go_ref, d_ref, mo_ref, vo_ref):
        g = g_ref[...]
        m_new = ADAM_B1 * m_ref[...] + (1.0 - ADAM_B1) * g
        v_new = ADAM_B2 * v_ref[...] + (1.0 - ADAM_B2) * (g * g)
        m_hat = m_new / bc1
        v_hat = v_new / bc2
        go_ref[...] = g
        d_ref[...] = -ADAM_LR * (m_hat / (jnp.sqrt(v_hat) + ADAM_EPS) + ADAM_WD * w_ref[...])
        mo_ref[...] = m_new
        vo_ref[...] = v_new

    return _elementwise(body, name=name, ins=[w, g, m, v], out_dtypes=[F32] * 4, after=after, tr=256)


ANY = pl.BlockSpec(memory_space=pl.ANY)


def _mesh_pos():
    return lax.axis_index("x"), lax.axis_index("y"), lax.axis_index("c")


def _other_chips(x, y):
    return [(1 - x, y), (x, 1 - y), (1 - x, 1 - y)]


def _remote(src, dst, send_sems, recv_sems, idx, to):
    return pltpu.make_async_remote_copy(src_ref=src, dst_ref=dst, send_sem=send_sems.at[idx], recv_sem=recv_sems.at[idx],
                                        device_id=to, device_id_type=MESH)


HBM = pl.BlockSpec(memory_space=pltpu.HBM)
SEM = pl.BlockSpec(memory_space=pltpu.SEMAPHORE)
EFFECT = pltpu.SideEffectType.DATAFLOW_SIDE_EFFECTING


def _in_hbm(arrays):
    return [pltpu.with_memory_space_constraint(a, pltpu.HBM) for a in arrays]


def _ici_copies(slab, send_sems, recv_sems, x, y, c):
    half = slab.shape[1] // 2
    rows = pl.ds(c * half, half)
    mine = slab.at[2 * x + y, rows]
    out = []
    for j, (cx, cy) in enumerate(_other_chips(x, y)):
        got = slab.at[2 * cx + cy, rows]
        out.append((_remote(mine, mine, send_sems, recv_sems, j, (cx, cy, c)),
                    _remote(got, got, send_sems, recv_sems, j, (cx, cy, c))))
    return out


def _gather_start(slabs, after, name):
    n = len(slabs)

    def body(*refs):
        ins = refs[:n]
        send_sems, recv_sems, token = refs[2 * n + 1:3 * n + 1], refs[3 * n + 1:4 * n + 1], refs[4 * n + 1]
        x, y, c = _mesh_pos()
        for w in range(n):
            for send, _ in reversed(_ici_copies(ins[w], send_sems[w], recv_sems[w], x, y, c)):
                send.start()
        token[...] = jnp.zeros_like(token)

    outs = pl.pallas_call(
        body, name=name, in_specs=[HBM] * n + [ANY],
        out_specs=[HBM] * n + [SEM] * (2 * n) + [pl.BlockSpec(memory_space=pltpu.VMEM)],
        out_shape=[pltpu.HBM(s.shape, s.dtype) for s in slabs] + [pltpu.SemaphoreType.DMA((3,))] * (2 * n)
        + [jax.ShapeDtypeStruct((8, 128), F32)],
        input_output_aliases={w: w for w in range(n)},
        compiler_params=_params(has_side_effects=EFFECT),
    )(*_in_hbm(slabs), after)
    return (outs[:n], outs[n:2 * n], outs[2 * n:3 * n]), outs[3 * n]


def _gather_wait(slabs, send_sems, recv_sems, after, name):
    n = len(slabs)

    def body(*refs):
        ins = refs[:n]
        sends, recvs = refs[n:2 * n], refs[2 * n:3 * n]
        x, y, c = _mesh_pos()
        for w in range(n):
            for send, recv in _ici_copies(ins[w], sends[w], recvs[w], x, y, c):
                send.wait_send()
                recv.wait_recv()

    return pl.pallas_call(
        body, name=name, in_specs=[HBM] * n + [SEM] * (2 * n) + [ANY], out_specs=[HBM] * n,
        out_shape=[pltpu.HBM(s.shape, s.dtype) for s in slabs],
        input_output_aliases={w: w for w in range(n)},
        compiler_params=_params(has_side_effects=EFFECT),
    )(*slabs, *send_sems, *recv_sems, after)


def _phase_start(arrays, copies, n_copies, after, name):
    n = len(arrays)

    def body(*refs):
        send_sems, recv_sems, token = refs[2 * n + 1], refs[2 * n + 2], refs[2 * n + 3]
        for send, _ in copies(refs[:n], send_sems, recv_sems, *_mesh_pos()):
            send.start()
        token[...] = jnp.zeros_like(token)

    outs = pl.pallas_call(
        body, name=name, in_specs=[HBM] * n + [ANY],
        out_specs=[HBM] * n + [SEM, SEM, pl.BlockSpec(memory_space=pltpu.VMEM)],
        out_shape=[pltpu.HBM(a.shape, a.dtype) for a in arrays]
        + [pltpu.SemaphoreType.DMA((n_copies,))] * 2 + [jax.ShapeDtypeStruct((8, 128), F32)],
        input_output_aliases={k: k for k in range(n)},
        compiler_params=_params(has_side_effects=EFFECT),
    )(*_in_hbm(arrays), after)
    return outs[:n], outs[n], outs[n + 1], outs[n + 2]


def _phase_wait(arrays, send_sems, recv_sems, copies, after, name):
    n = len(arrays)

    def body(*refs):
        for send, recv in copies(refs[:n], refs[n], refs[n + 1], *_mesh_pos()):
            send.wait_send()
            recv.wait_recv()

    return pl.pallas_call(
        body, name=name, in_specs=[HBM] * n + [SEM, SEM, ANY], out_specs=[HBM] * n,
        out_shape=[pltpu.HBM(a.shape, a.dtype) for a in arrays],
        input_output_aliases={k: k for k in range(n)},
        compiler_params=_params(has_side_effects=EFFECT),
    )(*arrays, send_sems, recv_sems, after)


def _forward_copies(slabs, send_sems, recv_sems, x, y, c):
    sibling = (x, y, 1 - c)
    out = []
    for w, slab in enumerate(slabs):
        half = slab.shape[1] // 2
        for j, (cx, cy) in enumerate(_other_chips(x, y)):
            mine = slab.at[2 * cx + cy, pl.ds(c * half, half)]
            theirs = slab.at[2 * cx + cy, pl.ds((1 - c) * half, half)]
            out.append((_remote(mine, mine, send_sems, recv_sems, 3 * w + j, sibling),
                        _remote(theirs, theirs, send_sems, recv_sems, 3 * w + j, sibling)))
    return out


def _share_copies(halves, send_sems, recv_sems, x, y, c):
    sibling = (x, y, 1 - c)
    out = []
    for w, full in enumerate(halves):
        h = full.shape[0] // 2
        mine, theirs = full.at[pl.ds(c * h, h)], full.at[pl.ds((1 - c) * h, h)]
        out.append((_remote(mine, mine, send_sems, recv_sems, w, sibling),
                    _remote(theirs, theirs, send_sems, recv_sems, w, sibling)))
    return out


def _exchange_copies(arrays, send_sems, recv_sems, x, y, c):
    n = len(arrays) // 2
    sibling = (x, y, 1 - c)
    out = []
    for w in range(n):
        grad, land = arrays[w], arrays[n + w]
        half = grad.shape[1] // 2
        out.append((_remote(grad.at[:, pl.ds((1 - c) * half, half), :], land, send_sems, recv_sems, w, sibling),
                    _remote(land, land, send_sems, recv_sems, w, sibling)))
    return out


def _gather_taps(taps):
    def body(in_ref, out_ref, send_sems, recv_sems, local_sem):
        x, y, c = _mesh_pos()
        me = 2 * x + y
        chips = _other_chips(x, y)
        lc = pltpu.make_async_copy(in_ref, out_ref.at[me], local_sem)
        lc.start()
        sends = [_remote(in_ref, out_ref.at[me], send_sems, recv_sems, j, (cx, cy, c)) for j, (cx, cy) in enumerate(chips)]
        for cp in sends:
            cp.start()
        for j, (cx, cy) in enumerate(chips):
            got = out_ref.at[2 * cx + cy]
            _remote(got, got, send_sems, recv_sems, j, (cx, cy, c)).wait_recv()
        for cp in sends:
            cp.wait_send()
        lc.wait()

    vm = pl.BlockSpec(memory_space=pltpu.VMEM)
    return pl.pallas_call(
        body, name="gather_taps", in_specs=[vm], out_specs=vm,
        out_shape=jax.ShapeDtypeStruct((N_CHIPS,) + taps.shape, taps.dtype),
        scratch_shapes=[pltpu.SemaphoreType.DMA((3,)), pltpu.SemaphoreType.DMA((3,)), pltpu.SemaphoreType.DMA],
        compiler_params=_params(has_side_effects=True),
    )(taps)


def _scatter_copies(part, slots, send_sems, recv_sems, x, y, c):
    out = []
    for j, (cx, cy) in enumerate(_other_chips(x, y)):
        out.append((_remote(part.at[2 * cx + cy], slots.at[j], send_sems, recv_sems, j, (cx, cy, c)),
                    _remote(slots.at[j], slots.at[j], send_sems, recv_sems, j, (cx, cy, c))))
    return out


def _scatter_start(parts, after, name):
    n = len(parts)
    slots = [lax.empty((3,) + p.shape[1:], p.dtype) for p in parts]

    def body(*refs):
        ins, lands = refs[:n], refs[n:2 * n]
        send_sems, recv_sems, token = refs[4 * n + 1:5 * n + 1], refs[5 * n + 1:6 * n + 1], refs[6 * n + 1]
        x, y, c = _mesh_pos()
        for w in range(n):
            for send, _ in reversed(_scatter_copies(ins[w], lands[w], send_sems[w], recv_sems[w], x, y, c)):
                send.start()
        token[...] = jnp.zeros_like(token)

    outs = pl.pallas_call(
        body, name=name, in_specs=[HBM] * (2 * n) + [ANY],
        out_specs=[HBM] * (2 * n) + [SEM] * (2 * n) + [pl.BlockSpec(memory_space=pltpu.VMEM)],
        out_shape=[pltpu.HBM(a.shape, a.dtype) for a in list(parts) + slots] + [pltpu.SemaphoreType.DMA((3,))] * (2 * n)
        + [jax.ShapeDtypeStruct((8, 128), F32)],
        input_output_aliases={k: k for k in range(2 * n)},
        compiler_params=_params(has_side_effects=EFFECT),
    )(*_in_hbm(list(parts) + slots), after)
    return (outs[:n], outs[n:2 * n], outs[2 * n:3 * n], outs[3 * n:4 * n]), outs[4 * n]


def _scatter_wait(parts, slots, send_sems, recv_sems, after, name):
    n = len(parts)

    def body(*refs):
        ins, lands = refs[:n], refs[n:2 * n]
        sends, recvs = refs[2 * n:3 * n], refs[3 * n:4 * n]
        x, y, c = _mesh_pos()
        for w in range(n):
            for send, recv in _scatter_copies(ins[w], lands[w], sends[w], recvs[w], x, y, c):
                send.wait_send()
                recv.wait_recv()

    outs = pl.pallas_call(
        body, name=name, in_specs=[HBM] * (2 * n) + [SEM] * (2 * n) + [ANY], out_specs=[HBM] * (2 * n),
        out_shape=[pltpu.HBM(a.shape, a.dtype) for a in list(parts) + list(slots)],
        input_output_aliases={k: k for k in range(2 * n)},
        compiler_params=_params(has_side_effects=EFFECT),
    )(*parts, *slots, *send_sems, *recv_sems, after)
    return outs[:n], outs[n:]


def _all_reduce_small(buf, after, name):
    rows, width = buf.shape

    def body(in_ref, _, out_ref, gathered, send_sems, recv_sems):
        x, y, c = _mesh_pos()
        me = 4 * x + 2 * y + c
        gathered[me] = in_ref[...]
        flips = [(dx, dy, dz) for dx in (0, 1) for dy in (0, 1) for dz in (0, 1)][1:]
        peers = [((1 - x) if dx else x, (1 - y) if dy else y, (1 - c) if dz else c) for dx, dy, dz in flips]
        sends = [_remote(in_ref, gathered.at[me], send_sems, recv_sems, k, peer) for k, peer in enumerate(peers)]
        for cp in sends:
            cp.start()
        for k, (px, py, pc) in enumerate(peers):
            got = gathered.at[4 * px + 2 * py + pc]
            _remote(got, got, send_sems, recv_sems, k, (px, py, pc)).wait_recv()
        for cp in sends:
            cp.wait_send()
        total = gathered[0]
        for d in range(1, 8):
            total = total + gathered[d]
        out_ref[...] = total

    vm = pl.BlockSpec(memory_space=pltpu.VMEM)
    return pl.pallas_call(
        body, name=name, in_specs=[vm, ANY], out_specs=vm, out_shape=jax.ShapeDtypeStruct(buf.shape, F32),
        scratch_shapes=[pltpu.VMEM((8, rows, width), F32), pltpu.SemaphoreType.DMA((7,)), pltpu.SemaphoreType.DMA((7,))],
        compiler_params=_params(has_side_effects=True),
    )(buf, after)


PACK_W = 2048
PACK_ROWS = 8


def _pack(vectors):
    flat = jnp.concatenate([v.reshape(-1) for v in vectors])
    unit = PACK_W * PACK_ROWS
    total = -(-flat.shape[0] // unit) * unit
    return jnp.pad(flat, (0, total - flat.shape[0])).reshape(total // PACK_W, PACK_W)


def _unpack(buf, shapes):
    flat = buf.reshape(-1)
    out, pos = [], 0
    for shp in shapes:
        size = math.prod(shp)
        out.append(flat[pos:pos + size].reshape(shp))
        pos += size
    return out


def _pad_rows(a, rows):
    return jnp.pad(a, ((0, rows - a.shape[0]), (0, 0)))


def kernel(x, g_pre_mix, w_in, b_in, w_dw, b_dw, g_conv_ln, b_conv_ln, w_sb_out, w_conv_out, w_o, g_post_mix, g_pre_mlp, w_up, w_down, g_post_mlp, loss_target, m_g_pre_mix, m_w_in, m_b_in, m_w_dw, m_b_dw, m_g_conv_ln, m_b_conv_ln, m_w_sb_out, m_w_conv_out, m_w_o, m_g_post_mix, m_g_pre_mlp, m_w_up, m_w_down, m_g_post_mlp, v_g_pre_mix, v_w_in, v_b_in, v_w_dw, v_b_dw, v_g_conv_ln, v_b_conv_ln, v_w_sb_out, v_w_conv_out, v_w_o, v_g_post_mix, v_g_pre_mlp, v_w_up, v_w_down, v_g_post_mlp):
    xs = x[0]
    tgt = loss_target[0]
    s, d = xs.shape
    d_sb = w_sb_out.shape[1]
    dc = w_conv_out.shape[1]
    d_in = w_in.shape[2] * N_CHIPS
    n_heads = d_sb // HEAD_DIM
    off_a = 3 * d_sb
    off_sb = off_a + 2 * dc
    off_cv = off_sb + d
    gw = dc
    assert off_a % dc == 0 and off_sb % gw == 0 and d % gw == 0 and d_in == off_cv + d
    chip = 2 * lax.axis_index("x") + lax.axis_index("y")

    big = dict(w_in=w_in[0], w_sb_out=w_sb_out[0], w_conv_out=w_conv_out[0], w_o=w_o[0], w_up=w_up[0], w_down=w_down[0])
    names = list(big)
    col_sharded = {"w_in", "w_sb_out", "w_conv_out", "w_up"}
    where = jnp.stack([chip, lax.axis_index("c")]).astype(jnp.int32)
    cs = w_dw.shape[2]
    taps4 = _gather_taps(_pad_rows(w_dw[0], CONV_HALO))
    first, token = _gather_start([_cast_into_block(big["w_in"], where, taps4, "cast_w_in")], taps4, "gather_start_in")
    rest, rest_token = _gather_start([_cast_into_block(big[n], where, token, "cast_" + n) for n in names[1:]], token,
                            "gather_start_rest")
    in_flight = dict(zip(names, zip(*[a + b for a, b in zip(first, rest)])))
    gathered = {}

    def land(group, after, tag):
        thru, sends, recvs = zip(*[in_flight[n] for n in group])
        landed = _gather_wait(thru, sends, recvs, after, "gather_wait_" + tag)
        return _phase_start(landed, _forward_copies, 3 * len(group), where, "forward_start_" + tag)

    def ready(group, started, after, tag):
        slabs, sends, recvs, _ = started
        for n, g4 in zip(group, _phase_wait(slabs, sends, recvs, _forward_copies, after, "forward_wait_" + tag)):
            gathered[n] = g4 if n in col_sharded else g4.reshape(g4.shape[0] * g4.shape[1], g4.shape[2])

    taps = jnp.transpose(taps4, (1, 0, 2)).reshape(CONV_HALO, N_CHIPS * cs)
    group_mix = ["w_sb_out", "w_conv_out", "w_o"]

    h = _rms_fwd(xs, g_pre_mix)
    started = land(["w_in"], rest_token, "in")
    ready(["w_in"], started, h, "in")
    proj = _mm(h, gathered["w_in"], name="mm_proj", b_groups=N_CHIPS, bias=b_in, tn=768)
    mix_fwd = land(group_mix, proj, "mix")
    attn = _attn_fwd(proj, mix_fwd[3], n_heads, d_sb)
    u0 = _glu(proj, off_a, dc)
    u1, u3 = _conv_fwd(u0, taps, b_dw, g_conv_ln, b_conv_ln)
    ready(group_mix, mix_fwd, u3, "mix")
    o_sb = _mm(attn, gathered["w_sb_out"], name="mm_o_sb", b_groups=N_CHIPS)
    up_fwd = land(["w_up"], o_sb, "up")
    o_cv = _mm(u3, gathered["w_conv_out"], name="mm_o_cv", b_groups=N_CHIPS, after=up_fwd[3])
    merged = _merge_fwd(proj, o_sb, o_cv, off_sb, off_cv, gw)
    y = _mm(merged, gathered["w_o"], name="mm_y")
    x2, h2 = _resid_rms(xs, y, g_post_mix, g_pre_mlp)
    ready(["w_up"], up_fwd, h2, "up")
    down_fwd = land(["w_down"], h2, "down")
    up, f = _mm(h2, gathered["w_up"], name="mm_up", b_groups=N_CHIPS, out_dtypes=(F32, BF16), after=down_fwd[3],
                epilogue=lambda acc: (acc, jnp.square(jnp.maximum(acc, 0.0))))
    ready(["w_down"], down_fwd, f, "down")
    dn = _mm(f, gathered["w_down"], name="mm_down", tm=512, tk=4096)
    dx3, d_dn, dg_post_mlp, loss_part = _final(x2, dn, g_post_mlp, tgt)

    moments = dict(w_in=(m_w_in, v_w_in), w_sb_out=(m_w_sb_out, v_w_sb_out), w_conv_out=(m_w_conv_out, v_w_conv_out),
                   w_o=(m_w_o, v_w_o), w_up=(m_w_up, v_w_up), w_down=(m_w_down, v_w_down))
    out_g, out_d, out_m, out_v = {}, {}, {}, {}
    grads = {}

    def blocks_of(group):
        out = []
        for n in group:
            g = grads[n]
            out.append(g if g.ndim == 3 else g.reshape(N_CHIPS, g.shape[0] // N_CHIPS, g.shape[1]))
        return out

    def exchange_begin(group, tag):
        blocks = blocks_of(group)
        lands = [lax.empty((b.shape[0], b.shape[1] // 2, b.shape[2]), b.dtype) for b in blocks]
        return _phase_start(blocks + lands, _exchange_copies, len(group), where, "exchange_start_" + tag)

    def scatter_begin(group, tag, exchange, after):
        arrays, sends, recvs, _ = exchange
        arrays = _phase_wait(arrays, sends, recvs, _exchange_copies, after, "exchange_wait_" + tag)
        n = len(group)
        pair_sums = [_add_pair(g, t, where, "pair_sum_" + m) for m, g, t in zip(group, arrays[:n], arrays[n:])]
        return _scatter_start(pair_sums, where, "scatter_start_" + tag)

    def reduce_sum(group, tag, started, after):
        parts, slots = _scatter_wait(*started, after, "scatter_wait_" + tag)
        halves = [_sum_slots(p, t, where, "chip_sum_" + n) for n, p, t in zip(group, parts, slots)]
        return _phase_start(halves, _share_copies, len(group), where, "share_start_" + tag)

    def reduce_update(group, tag, shared, after):
        halves, sends, recvs, _ = shared
        for n, red in zip(group, _phase_wait(halves, sends, recvs, _share_copies, after, "share_wait_" + tag)):
            mm_, vv_ = moments[n]
            res = _adamw(big[n], red, mm_[0], vv_[0], after, "adamw_" + n)
            out_g[n], out_d[n], out_m[n], out_v[n] = [r[None] for r in res]
            after = res[1]
        return after

    group_mlp, group_mix, group_in = ["w_down", "w_up"], ["w_o", "w_sb_out", "w_conv_out"], ["w_in"]
    grads["w_down"] = _mm(f, d_dn, name="mm_dw_down", ta=True, out_dtypes=(BF16,))
    dup = _mm(d_dn, gathered["w_down"], name="mm_df", tb=True, extra=[up], out_dtypes=(BF16,),
              epilogue=lambda acc, upv: (acc * (2.0 * jnp.maximum(upv, 0.0)),))
    grads["w_up"] = _mm(h2, dup, name="mm_dw_up", ta=True, out_groups=N_CHIPS, out_dtypes=(BF16,))
    mlp_exchange = exchange_begin(group_mlp, "mlp")
    dh2 = _mm(dup, gathered["w_up"], name="mm_dh2", tb=True, b_groups=N_CHIPS, after=mlp_exchange[3])
    dx2, dy, dg_pre_mlp, dg_post_mix = _rms_bwd2(dx3, dh2, x2, y, g_post_mix, g_pre_mlp)
    mlp_started, mlp_token = scatter_begin(group_mlp, "mlp", mlp_exchange, dx2)
    grads["w_o"] = _mm(merged, dy, name="mm_dw_o", ta=True, out_dtypes=(BF16,), after=mlp_token)
    dmerged = _mm(dy, gathered["w_o"], name="mm_dmerged", tb=True)
    do_sb, do_cv, dgate_sb, dgate_cv, s_gate_sb, s_gate_cv = _merge_bwd(proj, o_sb, o_cv, dmerged, off_sb, off_cv, gw)
    grads["w_sb_out"] = _mm(attn, do_sb, name="mm_dw_sb", ta=True, out_groups=N_CHIPS, out_dtypes=(BF16,))
    dattn = _mm(do_sb, gathered["w_sb_out"], name="mm_dattn", tb=True, b_groups=N_CHIPS)
    grads["w_conv_out"] = _mm(u3, do_cv, name="mm_dw_cv", ta=True, out_groups=N_CHIPS, out_dtypes=(BF16,))
    mix_exchange = exchange_begin(group_mix, "mix")
    du3 = _mm(do_cv, gathered["w_conv_out"], name="mm_du3", tb=True, b_groups=N_CHIPS, after=mix_exchange[3])
    mix_started, mix_token = scatter_begin(group_mix, "mix", mix_exchange, du3)
    dq, dk, dv, s_q, s_k, s_v = _attn_bwd(proj, dattn, mix_token, n_heads, d_sb)
    du1, dg_conv_ln, db_conv_ln, db_dw = _conv_bwd_norm(u1, du3, g_conv_ln, b_conv_ln)
    dglu_a, dglu_b, s_a, s_b, dtaps = _conv_bwd_taps(du1, u0, proj, taps, off_a)
    dproj = jnp.concatenate([dq, dk, dv, dglu_a, dglu_b, dgate_sb, dgate_cv], axis=1)
    db_in = jnp.concatenate([s_q, s_k, s_v, s_a, s_b, s_gate_sb, s_gate_cv], axis=1)
    small_w = dict(g_pre_mix=g_pre_mix, b_in=b_in, b_dw=b_dw, g_conv_ln=g_conv_ln, b_conv_ln=b_conv_ln,
                   g_post_mix=g_post_mix, g_pre_mlp=g_pre_mlp, g_post_mlp=g_post_mlp)
    small_m = dict(g_pre_mix=m_g_pre_mix, b_in=m_b_in, b_dw=m_b_dw, g_conv_ln=m_g_conv_ln, b_conv_ln=m_b_conv_ln,
                   g_post_mix=m_g_post_mix, g_pre_mlp=m_g_pre_mlp, g_post_mlp=m_g_post_mlp)
    small_v = dict(g_pre_mix=v_g_pre_mix, b_in=v_b_in, b_dw=v_b_dw, g_conv_ln=v_g_conv_ln, b_conv_ln=v_b_conv_ln,
                   g_post_mix=v_g_post_mix, g_pre_mlp=v_g_pre_mlp, g_post_mlp=v_g_post_mlp)
    small_g = dict(b_in=db_in, b_dw=db_dw, g_conv_ln=dg_conv_ln, b_conv_ln=db_conv_ln,
                   g_post_mix=dg_post_mix, g_pre_mlp=dg_pre_mlp, g_post_mlp=dg_post_mlp)

    def small_update(group, tails, after, tag):
        shapes = [small_w[n].shape for n in group] + [t.shape for t in tails]
        summed = _all_reduce_small(_pack([small_g[n] for n in group] + tails), after, "all_reduce_small_" + tag)
        zeros_tail = [jnp.zeros(t.shape, F32) for t in tails]
        res = _adamw(_pack([small_w[n] for n in group] + zeros_tail), summed,
                     _pack([small_m[n] for n in group] + zeros_tail), _pack([small_v[n] for n in group] + zeros_tail),
                     summed, "adamw_small_" + tag)
        unpacked = [_unpack(r, shapes) for r in res]
        for i, n in enumerate(group):
            out_g[n], out_d[n], out_m[n], out_v[n] = [u[i] for u in unpacked]
        return summed, unpacked[0][len(group):]

    early = ["b_in", "b_dw", "g_conv_ln", "b_conv_ln", "g_post_mix", "g_pre_mlp", "g_post_mlp"]
    summed, (loss_sum, taps_sum) = small_update(early, [loss_part, dtaps], dproj, "early")
    loss = loss_sum[0, 0]
    taps_grad = lax.dynamic_slice(taps_sum, (0, chip * cs), (CONV_HALO, cs))
    res = _adamw(_pad_rows(w_dw[0], CONV_HALO), taps_grad, _pad_rows(m_w_dw[0], CONV_HALO), _pad_rows(v_w_dw[0], CONV_HALO),
                 taps_grad, "adamw_taps")
    out_g["w_dw"], out_d["w_dw"], out_m["w_dw"], out_v["w_dw"] = [r[:CONV_WIDTH][None] for r in res]
    grads["w_in"] = _mm(h, dproj, name="mm_dw_in", ta=True, out_groups=N_CHIPS, out_dtypes=(BF16,), tn=768, after=summed)
    in_exchange = exchange_begin(group_in, "in")
    dh = _mm(dproj, gathered["w_in"], name="mm_dh", tb=True, b_groups=N_CHIPS, tk=2304, after=in_exchange[3])
    in_started, in_token = scatter_begin(group_in, "in", in_exchange, dh)
    grad_x, dg_pre_mix = _rms_bwd1(dx2, dh, xs, g_pre_mix)
    mlp_shared = reduce_sum(group_mlp, "mlp", mlp_started, in_token)
    mix_shared = reduce_sum(group_mix, "mix", mix_started, mlp_shared[3])
    done = reduce_update(group_mlp, "mlp", mlp_shared, mix_shared[3])
    done = reduce_update(group_mix, "mix", mix_shared, done)
    small_g["g_pre_mix"] = dg_pre_mix
    summed, _ = small_update(["g_pre_mix"], [], done, "late")
    in_shared = reduce_sum(group_in, "in", in_started, summed)
    reduce_update(group_in, "in", in_shared, in_shared[3])

    order = ["g_pre_mix", "w_in", "b_in", "w_dw", "b_dw", "g_conv_ln", "b_conv_ln", "w_sb_out", "w_conv_out", "w_o",
             "g_post_mix", "g_pre_mlp", "w_up", "w_down", "g_post_mlp"]
    return (loss, grad_x[None], *[out_g[n] for n in order], *[out_d[n] for n in order],
            *[out_m[n] for n in order], *[out_v[n] for n in order])
```
